```python
import math
import jax, jax.numpy as jnp
from jax import lax
import numpy as np

D_MODEL = 1024
BATCH = 16
SEQ = 2048
DEPTH = 1

CHUNK = 64
Q_BLOCK = 128
SB_HEADS = 16
SB_HEAD_DIM = 64
SB_WIDTH = SB_HEADS * SB_HEAD_DIM
CV_WIDTH = D_MODEL
CV_KERNEL = 31
LN_EPS = 1e-5
DEEPNORM_ALPHA = (2.0 * DEPTH) ** 0.25
DEEPNORM_BETA = (8.0 * DEPTH) ** -0.25
SPLIT_SIZES = (SB_WIDTH, SB_WIDTH, SB_WIDTH, SB_WIDTH, CV_WIDTH, CV_WIDTH, CV_WIDTH, D_MODEL, D_MODEL)
IN_WIDTH = sum(SPLIT_SIZES)
SPLIT_POINTS = tuple(int(i) for i in np.cumsum(SPLIT_SIZES)[:-1])

kernel_name = "stickbreak_conformer_gated_hybrid"


def _layer_norm(x, g, b):
    xf = x.astype(jnp.float32)
    mu = jnp.mean(xf, axis=-1, keepdims=True)
    var = jnp.mean(jnp.square(xf - mu), axis=-1, keepdims=True)
    y = (xf - mu) * lax.rsqrt(var + LN_EPS) * g.astype(jnp.float32) + b.astype(jnp.float32)
    return y.astype(x.dtype)


def _stick_breaking_attention(q, k, v):
    seq = q.shape[1]
    scale = 1.0 / math.sqrt(q.shape[-1])
    qf = q.astype(jnp.float32) * scale
    kf = k.astype(jnp.float32)
    vf = v.astype(jnp.float32)
    outs = []
    for start in range(0, seq, Q_BLOCK):
        end = start + Q_BLOCK
        logits = jnp.einsum('bqhd,bkhd->bhqk', qf[:, start:end], kf[:, :end])
        t_idx = start + jnp.arange(Q_BLOCK)[:, None]
        s_idx = jnp.arange(end)[None, :]
        mask = s_idx < t_idx
        log_not_beta = jnp.where(mask, jax.nn.log_sigmoid(-logits), 0.0)
        later = lax.cumsum(log_not_beta, axis=3, reverse=True) - log_not_beta
        weights = jnp.where(mask, jnp.exp(jax.nn.log_sigmoid(logits) + later), 0.0)
        outs.append(jnp.einsum('bhqk,bkhd->bqhd', weights, vf[:, :end]))
    return jnp.concatenate(outs, axis=1).astype(q.dtype)


def _causal_depthwise_conv(u, w, b):
    kw = w.shape[0]
    y = lax.conv_general_dilated(
        u, w[:, None, :].astype(u.dtype), window_strides=(1,), padding=[(kw - 1, 0)],
        dimension_numbers=('NWC', 'WIO', 'NWC'), feature_group_count=u.shape[-1])
    return y + b.astype(u.dtype)


def _hybrid_layer(x, w_in, w_sb_proj, conv_w, conv_b, conv_ln_g, conv_ln_b,
                  w_cv_proj, w_out, ln_g, ln_b):
    bsz, seq, _ = x.shape
    h = x @ w_in
    q, k, v, z_sb, c_val, c_gate, z_cv, g_sb, g_cv = jnp.split(h, SPLIT_POINTS, axis=-1)

    to_heads = lambda t: t.reshape(bsz, seq, SB_HEADS, SB_HEAD_DIM)
    o_sb = _stick_breaking_attention(to_heads(q), to_heads(k), to_heads(v)).reshape(bsz, seq, SB_WIDTH)
    y_sb = (o_sb * jax.nn.silu(z_sb)) @ w_sb_proj

    u = c_val * jax.nn.sigmoid(c_gate)
    u = _causal_depthwise_conv(u, conv_w, conv_b)
    u = jax.nn.silu(_layer_norm(u, conv_ln_g, conv_ln_b))
    y_cv = (u * jax.nn.silu(z_cv)) @ w_cv_proj

    merged = jax.nn.sigmoid(g_sb) * y_sb + jax.nn.sigmoid(g_cv) * y_cv
    return _layer_norm(DEEPNORM_ALPHA * x + merged @ w_out, ln_g, ln_b)


def _fwd_setup_inputs(seed: int = 0) -> dict:
    key = jax.random.key(seed)
    ks = jax.random.split(key, 14)
    n = lambda kk, shape: jax.random.normal(kk, shape, dtype=jnp.float32)
    x = n(ks[0], (BATCH, SEQ, D_MODEL))
    ln_in_g = 1.0 + 0.02 * n(ks[1], (D_MODEL,))
    ln_in_b = 0.02 * n(ks[2], (D_MODEL,))
    w_in = n(ks[3], (DEPTH, D_MODEL, IN_WIDTH)) * D_MODEL ** -0.5
    w_sb_proj = n(ks[4], (DEPTH, SB_WIDTH, D_MODEL)) * (SB_WIDTH ** -0.5) * DEEPNORM_BETA
    conv_w = n(ks[5], (DEPTH, CV_KERNEL, CV_WIDTH)) * CV_KERNEL ** -0.5
    conv_b = 0.02 * n(ks[6], (DEPTH, CV_WIDTH))
    conv_ln_g = 1.0 + 0.02 * n(ks[7], (DEPTH, CV_WIDTH))
    conv_ln_b = 0.02 * n(ks[8], (DEPTH, CV_WIDTH))
    w_cv_proj = n(ks[9], (DEPTH, CV_WIDTH, D_MODEL)) * (CV_WIDTH ** -0.5) * DEEPNORM_BETA
    w_out = n(ks[10], (DEPTH, D_MODEL, D_MODEL)) * (D_MODEL ** -0.5) * DEEPNORM_BETA
    ln_post_g = 1.0 + 0.02 * n(ks[11], (DEPTH, D_MODEL))
    ln_post_b = 0.02 * n(ks[12], (DEPTH, D_MODEL))
    return {"x": x, "ln_in_g": ln_in_g, "ln_in_b": ln_in_b, "w_in": w_in,
            "w_sb_proj": w_sb_proj, "conv_w": conv_w, "conv_b": conv_b,
            "conv_ln_g": conv_ln_g, "conv_ln_b": conv_ln_b, "w_cv_proj": w_cv_proj,
            "w_out": w_out, "ln_post_g": ln_post_g, "ln_post_b": ln_post_b}


def _fwd_reference(x, ln_in_g, ln_in_b, w_in, w_sb_proj, conv_w, conv_b, conv_ln_g, conv_ln_b,
              w_cv_proj, w_out, ln_post_g, ln_post_b):
    h = _layer_norm(x, ln_in_g, ln_in_b)
    for l in range(DEPTH):
        h = _hybrid_layer(h, w_in[l], w_sb_proj[l], conv_w[l], conv_b[l], conv_ln_g[l],
                          conv_ln_b[l], w_cv_proj[l], w_out[l], ln_post_g[l], ln_post_b[l])
    return h


import jax as _jax
import jax.numpy as _jnp

TWIN_FORMAT = 'train_step'
FWD_PARAMS = ['x', 'ln_in_g', 'ln_in_b', 'w_in', 'w_sb_proj', 'conv_w', 'conv_b', 'conv_ln_g', 'conv_ln_b', 'w_cv_proj', 'w_out', 'ln_post_g', 'ln_post_b']
TWIN_WEIGHTS = ['ln_in_g', 'ln_in_b', 'w_in', 'w_sb_proj', 'conv_w', 'conv_b', 'conv_ln_g', 'conv_ln_b', 'w_cv_proj', 'w_out', 'ln_post_g', 'ln_post_b']
TWIN_DIFF_INPUT = 'x'
TWIN_INPUTS = ['x', 'ln_in_g', 'ln_in_b', 'w_in', 'w_sb_proj', 'conv_w', 'conv_b', 'conv_ln_g', 'conv_ln_b', 'w_cv_proj', 'w_out', 'ln_post_g', 'ln_post_b', 'loss_target', 'm_ln_in_g', 'm_ln_in_b', 'm_w_in', 'm_w_sb_proj', 'm_conv_w', 'm_conv_b', 'm_conv_ln_g', 'm_conv_ln_b', 'm_w_cv_proj', 'm_w_out', 'm_ln_post_g', 'm_ln_post_b', 'v_ln_in_g', 'v_ln_in_b', 'v_w_in', 'v_w_sb_proj', 'v_conv_w', 'v_conv_b', 'v_conv_ln_g', 'v_conv_ln_b', 'v_w_cv_proj', 'v_w_out', 'v_ln_post_g', 'v_ln_post_b']
TWIN_OUTPUTS = ['loss', 'grad_x', 'grad_ln_in_g', 'grad_ln_in_b', 'grad_w_in', 'grad_w_sb_proj', 'grad_conv_w', 'grad_conv_b', 'grad_conv_ln_g', 'grad_conv_ln_b', 'grad_w_cv_proj', 'grad_w_out', 'grad_ln_post_g', 'grad_ln_post_b', 'delta_ln_in_g', 'delta_ln_in_b', 'delta_w_in', 'delta_w_sb_proj', 'delta_conv_w', 'delta_conv_b', 'delta_conv_ln_g', 'delta_conv_ln_b', 'delta_w_cv_proj', 'delta_w_out', 'delta_ln_post_g', 'delta_ln_post_b', 'new_m_ln_in_g', 'new_m_ln_in_b', 'new_m_w_in', 'new_m_w_sb_proj', 'new_m_conv_w', 'new_m_conv_b', 'new_m_conv_ln_g', 'new_m_conv_ln_b', 'new_m_w_cv_proj', 'new_m_w_out', 'new_m_ln_post_g', 'new_m_ln_post_b', 'new_v_ln_in_g', 'new_v_ln_in_b', 'new_v_w_in', 'new_v_w_sb_proj', 'new_v_conv_w', 'new_v_conv_b', 'new_v_conv_ln_g', 'new_v_conv_ln_b', 'new_v_w_cv_proj', 'new_v_w_out', 'new_v_ln_post_g', 'new_v_ln_post_b']
TWIN_LEAF_KINDS = {'loss': 'loss', 'grad_x': 'grad_x', 'grad_ln_in_g': 'grad_w', 'grad_ln_in_b': 'grad_w', 'grad_w_in': 'grad_w', 'grad_w_sb_proj': 'grad_w', 'grad_conv_w': 'grad_w', 'grad_conv_b': 'grad_w', 'grad_conv_ln_g': 'grad_w', 'grad_conv_ln_b': 'grad_w', 'grad_w_cv_proj': 'grad_w', 'grad_w_out': 'grad_w', 'grad_ln_post_g': 'grad_w', 'grad_ln_post_b': 'grad_w', 'delta_ln_in_g': 'delta_w', 'delta_ln_in_b': 'delta_w', 'delta_w_in': 'delta_w', 'delta_w_sb_proj': 'delta_w', 'delta_conv_w': 'delta_w', 'delta_conv_b': 'delta_w', 'delta_conv_ln_g': 'delta_w', 'delta_conv_ln_b': 'delta_w', 'delta_w_cv_proj': 'delta_w', 'delta_w_out': 'delta_w', 'delta_ln_post_g': 'delta_w', 'delta_ln_post_b': 'delta_w', 'new_m_ln_in_g': 'new_m', 'new_m_ln_in_b': 'new_m', 'new_m_w_in': 'new_m', 'new_m_w_sb_proj': 'new_m', 'new_m_conv_w': 'new_m', 'new_m_conv_b': 'new_m', 'new_m_conv_ln_g': 'new_m', 'new_m_conv_ln_b': 'new_m', 'new_m_w_cv_proj': 'new_m', 'new_m_w_out': 'new_m', 'new_m_ln_post_g': 'new_m', 'new_m_ln_post_b': 'new_m', 'new_v_ln_in_g': 'new_v', 'new_v_ln_in_b': 'new_v', 'new_v_w_in': 'new_v', 'new_v_w_sb_proj': 'new_v', 'new_v_conv_w': 'new_v', 'new_v_conv_b': 'new_v', 'new_v_conv_ln_g': 'new_v', 'new_v_conv_ln_b': 'new_v', 'new_v_w_cv_proj': 'new_v', 'new_v_w_out': 'new_v', 'new_v_ln_post_g': 'new_v', 'new_v_ln_post_b': 'new_v'}


def _forward(args):
    return _fwd_reference(*[args[k] for k in FWD_PARAMS])


def _output_shape():
    out = _jax.eval_shape(lambda: _forward(_fwd_setup_inputs(0)))
    return out.shape, out.dtype

N_MICROBATCH = 1
ADAM_LR = 0.001
ADAM_B1 = 0.9
ADAM_B2 = 0.999
ADAM_EPS = 1e-08
ADAM_WD = 0.01
ADAM_STEP = 10
PER_EXAMPLE_BATCH_AXIS = {'x': 0, 'loss_target': 0}
SHARED_INPUTS = []
_WEIGHT_DTYPES = {'ln_in_g': _jnp.float32, 'ln_in_b': _jnp.float32, 'w_in': _jnp.float32, 'w_sb_proj': _jnp.float32, 'conv_w': _jnp.float32, 'conv_b': _jnp.float32, 'conv_ln_g': _jnp.float32, 'conv_ln_b': _jnp.float32, 'w_cv_proj': _jnp.float32, 'w_out': _jnp.float32, 'ln_post_g': _jnp.float32, 'ln_post_b': _jnp.float32}
MOMENT_SCALE = {'ln_in_g': 1.054111e+00, 'ln_in_b': 4.414572e-01, 'w_in': 7.918151e-03, 'w_sb_proj': 1.839426e-02, 'conv_w': 1.060121e-02, 'conv_b': 2.212949e-02, 'conv_ln_g': 1.260479e-02, 'conv_ln_b': 1.176393e-02, 'w_cv_proj': 1.731006e-02, 'w_out': 2.526697e-02, 'ln_post_g': 3.201177e+01, 'ln_post_b': 6.953930e-01}


def _to_microbatches(a, axis):
    t = _jnp.moveaxis(a, axis, 0)
    t = t.reshape((N_MICROBATCH, t.shape[0] // N_MICROBATCH) + t.shape[1:])
    return _jnp.moveaxis(t, 1, axis + 1)


def setup_inputs(seed: int = 0) -> dict:
    inp = _fwd_setup_inputs(seed)
    key = _jax.random.fold_in(_jax.random.key(seed), 7919)
    shape, _ = _output_shape()
    out = dict(inp)
    out["loss_target"] = _jax.random.normal(_jax.random.fold_in(key, 0), shape, _jnp.float32)
    for i, name in enumerate(TWIN_WEIGHTS):
        w = inp[name].astype(_jnp.float32)
        if MOMENT_SCALE is None:
            s = _jnp.sqrt(_jnp.mean(_jnp.square(w)) + 1e-30)
        else:
            s = MOMENT_SCALE[name]
        km, kv = _jax.random.split(_jax.random.fold_in(key, i + 1))
        out[name] = w
        out["m_" + name] = s * _jax.random.normal(km, w.shape, _jnp.float32)
        out["v_" + name] = (s * s) * _jax.random.uniform(kv, w.shape, _jnp.float32, 0.5, 1.5)
    if N_MICROBATCH > 1:
        for name, axis in PER_EXAMPLE_BATCH_AXIS.items():
            out[name] = _to_microbatches(out[name], axis)
    return {'x': out['x'], 'ln_in_g': out['ln_in_g'], 'ln_in_b': out['ln_in_b'], 'w_in': out['w_in'], 'w_sb_proj': out['w_sb_proj'], 'conv_w': out['conv_w'], 'conv_b': out['conv_b'], 'conv_ln_g': out['conv_ln_g'], 'conv_ln_b': out['conv_ln_b'], 'w_cv_proj': out['w_cv_proj'], 'w_out': out['w_out'], 'ln_post_g': out['ln_post_g'], 'ln_post_b': out['ln_post_b'], 'loss_target': out['loss_target'], 'm_ln_in_g': out['m_ln_in_g'], 'm_ln_in_b': out['m_ln_in_b'], 'm_w_in': out['m_w_in'], 'm_w_sb_proj': out['m_w_sb_proj'], 'm_conv_w': out['m_conv_w'], 'm_conv_b': out['m_conv_b'], 'm_conv_ln_g': out['m_conv_ln_g'], 'm_conv_ln_b': out['m_conv_ln_b'], 'm_w_cv_proj': out['m_w_cv_proj'], 'm_w_out': out['m_w_out'], 'm_ln_post_g': out['m_ln_post_g'], 'm_ln_post_b': out['m_ln_post_b'], 'v_ln_in_g': out['v_ln_in_g'], 'v_ln_in_b': out['v_ln_in_b'], 'v_w_in': out['v_w_in'], 'v_w_sb_proj': out['v_w_sb_proj'], 'v_conv_w': out['v_conv_w'], 'v_conv_b': out['v_conv_b'], 'v_conv_ln_g': out['v_conv_ln_g'], 'v_conv_ln_b': out['v_conv_ln_b'], 'v_w_cv_proj': out['v_w_cv_proj'], 'v_w_out': out['v_w_out'], 'v_ln_post_g': out['v_ln_post_g'], 'v_ln_post_b': out['v_ln_post_b']}


def _loss(weights, diff, rest, loss_target):
    with _jax.named_scope("forward"):
        args = {**rest, TWIN_DIFF_INPUT: diff, **{k: w.astype(_WEIGHT_DTYPES[k]) for k, w in weights.items()}}
        y = _forward(args)
    with _jax.named_scope("loss_head"):
        err = _jnp.square(y.astype(_jnp.float32) - loss_target)
        return 0.5 * _jnp.sum(_jnp.mean(err, axis=-1)) if err.ndim else 0.5 * err


def _adamw(w, g, m, v):
    m = ADAM_B1 * m + (1.0 - ADAM_B1) * g
    v = ADAM_B2 * v + (1.0 - ADAM_B2) * _jnp.square(g)
    m_hat = m / (1.0 - ADAM_B1 ** ADAM_STEP)
    v_hat = v / (1.0 - ADAM_B2 ** ADAM_STEP)
    delta = -ADAM_LR * (m_hat / (_jnp.sqrt(v_hat) + ADAM_EPS) + ADAM_WD * w)
    return delta, m, v


def reference(x, ln_in_g, ln_in_b, w_in, w_sb_proj, conv_w, conv_b, conv_ln_g, conv_ln_b, w_cv_proj, w_out, ln_post_g, ln_post_b, loss_target, m_ln_in_g, m_ln_in_b, m_w_in, m_w_sb_proj, m_conv_w, m_conv_b, m_conv_ln_g, m_conv_ln_b, m_w_cv_proj, m_w_out, m_ln_post_g, m_ln_post_b, v_ln_in_g, v_ln_in_b, v_w_in, v_w_sb_proj, v_conv_w, v_conv_b, v_conv_ln_g, v_conv_ln_b, v_w_cv_proj, v_w_out, v_ln_post_g, v_ln_post_b):
    given = dict(x=x, ln_in_g=ln_in_g, ln_in_b=ln_in_b, w_in=w_in, w_sb_proj=w_sb_proj, conv_w=conv_w, conv_b=conv_b, conv_ln_g=conv_ln_g, conv_ln_b=conv_ln_b, w_cv_proj=w_cv_proj, w_out=w_out, ln_post_g=ln_post_g, ln_post_b=ln_post_b, loss_target=loss_target, m_ln_in_g=m_ln_in_g, m_ln_in_b=m_ln_in_b, m_w_in=m_w_in, m_w_sb_proj=m_w_sb_proj, m_conv_w=m_conv_w, m_conv_b=m_conv_b, m_conv_ln_g=m_conv_ln_g, m_conv_ln_b=m_conv_ln_b, m_w_cv_proj=m_w_cv_proj, m_w_out=m_w_out, m_ln_post_g=m_ln_post_g, m_ln_post_b=m_ln_post_b, v_ln_in_g=v_ln_in_g, v_ln_in_b=v_ln_in_b, v_w_in=v_w_in, v_w_sb_proj=v_w_sb_proj, v_conv_w=v_conv_w, v_conv_b=v_conv_b, v_conv_ln_g=v_conv_ln_g, v_conv_ln_b=v_conv_ln_b, v_w_cv_proj=v_w_cv_proj, v_w_out=v_w_out, v_ln_post_g=v_ln_post_g, v_ln_post_b=v_ln_post_b)
    weights = {n: given[n] for n in TWIN_WEIGHTS}
    shared = {n: given[n] for n in SHARED_INPUTS}
    per_example = {n: given[n] for n in ['x']}
    grad_fn = _jax.value_and_grad(_loss, argnums=(0, 1))

    def one_microbatch(ex, loss_target):
        ex = dict(ex)
        diff = ex.pop(TWIN_DIFF_INPUT)
        return grad_fn(weights, diff, {**shared, **ex}, loss_target)

    if N_MICROBATCH == 1:
        loss, (grad_w, grad_x) = one_microbatch(per_example, given["loss_target"])
    else:
        def body(carry, xs):
            loss_sum, grad_sum = carry
            l_k, (gw_k, gx_k) = one_microbatch(xs[0], xs[1])
            with _jax.named_scope("update"):
                return (loss_sum + l_k, _jax.tree.map(_jnp.add, grad_sum, gw_k)), gx_k

        init = (_jnp.zeros((), _jnp.float32), _jax.tree.map(_jnp.zeros_like, weights))
        (loss, grad_w), grad_x = _jax.lax.scan(body, init, (per_example, given["loss_target"]))
    with _jax.named_scope("update"):
        delta_w, new_m, new_v = {}, {}, {}
        for n in TWIN_WEIGHTS:
            delta_w[n], new_m[n], new_v[n] = _adamw(weights[n], grad_w[n], given["m_" + n], given["v_" + n])
    return (loss, grad_x, *[grad_w[n] for n in TWIN_WEIGHTS], *[delta_w[n] for n in TWIN_WEIGHTS],
            *[new_m[n] for n in TWIN_WEIGHTS], *[new_v[n] for n in TWIN_WEIGHTS])
```

```python
import functools

import jax
import jax.numpy as jnp
from jax import lax
from jax.experimental import pallas as pl
from jax.experimental.pallas import tpu as pltpu

F32 = jnp.float32
BF16 = jnp.bfloat16

D_MODEL = 1024
N_HEADS = 16
HEAD_DIM = 64
HEAD_PAIRS = N_HEADS // 2
N_PIECES = 9
IN_WIDTH = N_PIECES * D_MODEL
Q_BLOCK = 128
CONV_K = 31
CONV_ROWS = 32
HALO = 32
LN_EPS = 1e-5
ALPHA = 2.0 ** 0.25
Q_SCALE = 0.125
N_CHIPS = 4
N_DEV = 8
SHARD_IN = IN_WIDTH // N_CHIPS
SHARD_SQ = D_MODEL // N_CHIPS
HALF = D_MODEL // 2
SMALL_ROWS = 40

ADAM_LR = 0.001
ADAM_B1 = 0.9
ADAM_B2 = 0.999
ADAM_EPS = 1e-08
ADAM_WD = 0.01
ADAM_STEP = 10

MESH = pl.DeviceIdType.MESH
ANY = pl.BlockSpec(memory_space=pl.ANY)
VMEM_LIMIT = 60 * 1024 * 1024

NT = (((1,), (1,)), ((), ()))
TN = (((0,), (0,)), ((), ()))


def _sigmoid(x):
    return 1.0 / (1.0 + jnp.exp(-x))


def _dot(a, b):
    return jnp.dot(a, b, preferred_element_type=F32)


def _dot_nt(a, b):
    return lax.dot_general(a, b, NT, preferred_element_type=F32)


def _dot_tn(a, b):
    return lax.dot_general(a, b, TN, preferred_element_type=F32)


def _split_bf16(x):
    hi = x.astype(BF16)
    lo = (x - hi.astype(F32)).astype(BF16)
    return hi, lo


def _ln_stats(x):
    mu = jnp.mean(x, axis=-1, keepdims=True)
    xc = x - mu
    var = jnp.mean(xc * xc, axis=-1, keepdims=True)
    rstd = lax.rsqrt(var + LN_EPS)
    return xc * rstd, rstd


def _ln_bwd(dy, xhat, rstd, g):
    dxh = dy * g
    m1 = jnp.mean(dxh, axis=-1, keepdims=True)
    m2 = jnp.mean(dxh * xhat, axis=-1, keepdims=True)
    return rstd * (dxh - m1 - xhat * m2)


def _params(*sem):
    return pltpu.CompilerParams(dimension_semantics=sem, vmem_limit_bytes=VMEM_LIMIT)


def _ln_in(x, g, b):
    t = x.shape[0]
    tm = min(512, t)

    def body(x_ref, g_ref, b_ref, hf_ref, hb_ref):
        xhat, _ = _ln_stats(x_ref[...])
        y = xhat * g_ref[...] + b_ref[...]
        hf_ref[...] = y
        hb_ref[...] = y.astype(BF16)

    row = pl.BlockSpec((tm, D_MODEL), lambda i: (i, 0))
    vec = pl.BlockSpec((1, D_MODEL), lambda i: (0, 0))
    return pl.pallas_call(
        body, grid=(t // tm,), in_specs=[row, vec, vec], out_specs=[row, row],
        out_shape=[jax.ShapeDtypeStruct((t, D_MODEL), F32), jax.ShapeDtypeStruct((t, D_MODEL), BF16)],
        compiler_params=_params("parallel"), name="ln_in")(x, g, b)


def _in_proj(h, w, col0, ncol, out_dtype, scale_first, name):
    t = h.shape[0]
    tm = min(512, t)

    def body(h_ref, w_ref, o_ref):
        res = _dot(h_ref[...], w_ref[...])
        if scale_first:
            res = res * jnp.where(pl.program_id(0) == 0, Q_SCALE, 1.0)
        o_ref[...] = res.astype(out_dtype)

    return pl.pallas_call(
        body, grid=(ncol, t // tm),
        in_specs=[pl.BlockSpec((tm, D_MODEL), lambda j, i: (i, 0)),
                  pl.BlockSpec((D_MODEL, D_MODEL), lambda j, i: (0, j + col0))],
        out_specs=pl.BlockSpec((tm, D_MODEL), lambda j, i: (i, j)),
        out_shape=jax.ShapeDtypeStruct((t, ncol * D_MODEL), out_dtype),
        compiler_params=_params("parallel", "parallel"), name=name)(h, w)


def _sb_block(l, mask, tri, carry):
    e = jnp.exp(-jnp.abs(l))
    sp = jnp.log(1.0 + e)
    lnb = jnp.where(mask, jnp.minimum(-l, 0.0) - sp, 0.0)
    lsig = jnp.minimum(l, 0.0) - sp
    hi, lo = _split_bf16(lnb)
    suffix = _dot(hi, tri) + _dot(lo, tri)
    later = suffix - lnb + carry
    a = jnp.where(mask, jnp.exp(lsig + later), 0.0)
    return a, suffix[:, 0:1], e


def _attn_fwd(qkv, nb, seq):
    nq = seq // Q_BLOCK

    def body(q_ref, k_ref, v_ref, o_ref):
        qi = pl.program_id(2)
        lane = lax.broadcasted_iota(jnp.int32, (Q_BLOCK, 2 * HEAD_DIM), 1)
        row = lax.broadcasted_iota(jnp.int32, (Q_BLOCK, Q_BLOCK), 0)
        col = lax.broadcasted_iota(jnp.int32, (Q_BLOCK, Q_BLOCK), 1)
        tri = (row >= col).astype(BF16)
        q = q_ref[...]
        outs = []
        for hd in range(2):
            in_head = (lane >= HEAD_DIM * hd) & (lane < HEAD_DIM * (hd + 1))
            qh = jnp.where(in_head, q, jnp.zeros_like(q))

            def step(jj, state, qh=qh):
                carry, acc = state
                ks = pl.multiple_of((qi - jj) * Q_BLOCK, Q_BLOCK)
                kblk = k_ref[pl.ds(ks, Q_BLOCK), :]
                vblk = v_ref[pl.ds(ks, Q_BLOCK), :]
                l = _dot_nt(qh, kblk)
                mask = (col < row) | (jj > 0)
                a, blk_sum, _ = _sb_block(l, mask, tri, carry)
                return carry + blk_sum, acc + _dot(a.astype(BF16), vblk)

            init = (jnp.zeros((Q_BLOCK, 1), F32), jnp.zeros((Q_BLOCK, 2 * HEAD_DIM), F32))
            outs.append(lax.fori_loop(0, qi + 1, step, init)[1])
        o_ref[...] = jnp.where(lane < HEAD_DIM, outs[0], outs[1])

    w = 2 * HEAD_DIM
    return pl.pallas_call(
        body, grid=(nb, HEAD_PAIRS, nq),
        in_specs=[pl.BlockSpec((Q_BLOCK, w), lambda b, h, i: (b * nq + i, h)),
                  pl.BlockSpec((seq, w), lambda b, h, i: (b, HEAD_PAIRS + h)),
                  pl.BlockSpec((seq, w), lambda b, h, i: (b, 2 * HEAD_PAIRS + h))],
        out_specs=pl.BlockSpec((Q_BLOCK, w), lambda b, h, i: (b * nq + i, h)),
        out_shape=jax.ShapeDtypeStruct((nb * seq, D_MODEL), F32),
        compiler_params=_params("parallel", "parallel", "arbitrary"), name="attn_fwd")(qkv, qkv, qkv)


def _attn_bwd(qkv, do, nb, seq):
    nq = seq // Q_BLOCK

    def body(q_ref, k_ref, v_ref, do_ref, dq_ref, dk_ref, dv_ref, g_s, b_s, dk_acc, dv_acc):
        qi = pl.program_id(2)

        @pl.when(qi == 0)
        def _():
            dk_acc[...] = jnp.zeros_like(dk_acc)
            dv_acc[...] = jnp.zeros_like(dv_acc)

        lane = lax.broadcasted_iota(jnp.int32, (Q_BLOCK, 2 * HEAD_DIM), 1)
        row = lax.broadcasted_iota(jnp.int32, (Q_BLOCK, Q_BLOCK), 0)
        col = lax.broadcasted_iota(jnp.int32, (Q_BLOCK, Q_BLOCK), 1)
        tri_suffix = (row >= col).astype(BF16)
        tri_prefix = (row <= col).astype(BF16)
        q = q_ref[...]
        dout = do_ref[...]
        dqs = []
        for hd in range(2):
            in_head = (lane >= HEAD_DIM * hd) & (lane < HEAD_DIM * (hd + 1))
            qh = jnp.where(in_head, q, jnp.zeros_like(q))
            doh = jnp.where(in_head, dout, jnp.zeros_like(dout))

            def sweep_a(jj, carry, qh=qh, doh=doh):
                kb = qi - jj
                ks = pl.multiple_of(kb * Q_BLOCK, Q_BLOCK)
                kblk = k_ref[pl.ds(ks, Q_BLOCK), :]
                vblk = v_ref[pl.ds(ks, Q_BLOCK), :]
                l = _dot_nt(qh, kblk)
                mask = (col < row) | (jj > 0)
                a, blk_sum, e = _sb_block(l, mask, tri_suffix, carry)
                r = 1.0 / (1.0 + e)
                beta = jnp.where(l >= 0.0, r, e * r)
                g_s[kb] = a * _dot_nt(doh, vblk)
                b_s[kb] = beta
                dv_acc[pl.ds(ks, Q_BLOCK), :] += _dot_tn(a.astype(BF16), doh)
                return carry + blk_sum

            lax.fori_loop(0, qi + 1, sweep_a, jnp.zeros((Q_BLOCK, 1), F32))

            def sweep_b(kb, state, qh=qh):
                prefix, dq = state
                ks = pl.multiple_of(kb * Q_BLOCK, Q_BLOCK)
                kblk = k_ref[pl.ds(ks, Q_BLOCK), :]
                g = g_s[kb]
                beta = b_s[kb]
                hi, lo = _split_bf16(g)
                incl = _dot(hi, tri_prefix) + _dot(lo, tri_prefix)
                before = incl - g + prefix
                mask = (col < row) | (kb < qi)
                dl = jnp.where(mask, g * (1.0 - beta) - beta * before, 0.0).astype(BF16)
                dk_acc[pl.ds(ks, Q_BLOCK), :] += _dot_tn(dl, qh)
                return prefix + incl[:, Q_BLOCK - 1:Q_BLOCK], dq + _dot(dl, kblk)

            init = (jnp.zeros((Q_BLOCK, 1), F32), jnp.zeros((Q_BLOCK, 2 * HEAD_DIM), F32))
            dqs.append(lax.fori_loop(0, qi + 1, sweep_b, init)[1])

        dq_ref[...] = (jnp.where(lane < HEAD_DIM, dqs[0], dqs[1]) * Q_SCALE).astype(BF16)

        @pl.when(qi == nq - 1)
        def _():
            dk_ref[...] = dk_acc[...].astype(BF16)
            dv_ref[...] = dv_acc[...].astype(BF16)

    w = 2 * HEAD_DIM
    t = nb * seq
    qspec = pl.BlockSpec((Q_BLOCK, w), lambda b, h, i: (b * nq + i, h))
    kvout = pl.BlockSpec((seq, w), lambda b, h, i: (b, h))
    return pl.pallas_call(
        body, grid=(nb, HEAD_PAIRS, nq),
        in_specs=[qspec,
                  pl.BlockSpec((seq, w), lambda b, h, i: (b, HEAD_PAIRS + h)),
                  pl.BlockSpec((seq, w), lambda b, h, i: (b, 2 * HEAD_PAIRS + h)),
                  qspec],
        out_specs=[qspec, kvout, kvout],
        out_shape=[jax.ShapeDtypeStruct((t, D_MODEL), BF16)] * 3,
        scratch_shapes=[pltpu.VMEM((nq, Q_BLOCK, Q_BLOCK), F32), pltpu.VMEM((nq, Q_BLOCK, Q_BLOCK), F32),
                        pltpu.VMEM((seq, w), F32), pltpu.VMEM((seq, w), F32)],
        compiler_params=_params("parallel", "parallel", "arbitrary"), name="attn_bwd")(qkv, qkv, qkv, do)


def _conv_block_rows(seq):
    return min(256, seq)


def _conv_fwd(hrest, conv_w, conv_b, ln_g, ln_b, nb, seq):
    ts = _conv_block_rows(seq)
    nblk = seq // ts
    lanes = 128

    def body(cv_ref, cg_ref, cvh_ref, cgh_ref, z_ref, w_ref, cb_ref, g_ref, b_ref, u1_ref, cvin_ref, ubuf):
        i = pl.program_id(1)
        halo = cvh_ref[...] * _sigmoid(cgh_ref[...])
        ubuf[0:HALO, :] = jnp.where(i > 0, halo, 0.0)
        ubuf[HALO:HALO + ts, :] = cv_ref[...] * _sigmoid(cg_ref[...])
        for cc in range(D_MODEL // lanes):
            cs = slice(cc * lanes, (cc + 1) * lanes)
            acc = jnp.zeros((ts, lanes), F32) + cb_ref[:, cs]
            for k in range(CONV_K):
                acc = acc + w_ref[k:k + 1, cs] * ubuf[pl.ds(HALO - CONV_K + 1 + k, ts), cs]
            u1_ref[:, cs] = acc
        xhat, _ = _ln_stats(u1_ref[...])
        u2 = xhat * g_ref[...] + b_ref[...]
        z = z_ref[...]
        cvin_ref[...] = (u2 * _sigmoid(u2) * z * _sigmoid(z)).astype(BF16)

    def main(colblk):
        return pl.BlockSpec((ts, D_MODEL), lambda b, i: (b * nblk + i, colblk))

    def halo(colblk):
        return pl.BlockSpec((HALO, D_MODEL),
                            lambda b, i: (jnp.maximum((b * seq + i * ts) // HALO - 1, 0), colblk))

    vec = pl.BlockSpec((1, D_MODEL), lambda b, i: (0, 0))
    t = nb * seq
    return pl.pallas_call(
        body, grid=(nb, nblk),
        in_specs=[main(1), main(2), halo(1), halo(2), main(3),
                  pl.BlockSpec((CONV_ROWS, D_MODEL), lambda b, i: (0, 0)), vec, vec, vec],
        out_specs=[main(0), main(0)],
        out_shape=[jax.ShapeDtypeStruct((t, D_MODEL), F32), jax.ShapeDtypeStruct((t, D_MODEL), BF16)],
        scratch_shapes=[pltpu.VMEM((HALO + ts, D_MODEL), F32)],
        compiler_params=_params("parallel", "parallel"), name="conv_fwd")(
            hrest, hrest, hrest, hrest, hrest, conv_w, conv_b, ln_g, ln_b)


def _tail(o, hrest, cvin, h0, tgt, w_sb, w_cv, w_out, ln_g, ln_b):
    t = o.shape[0]
    tm = min(256, t)

    def body(o_ref, z_ref, gs_ref, gc_ref, cvin_ref, h0_ref, tgt_ref, wsb_ref, wcv_ref, wout_ref, g_ref, b_ref,
             dr_ref, drb_ref, a_ref, mg_ref, dysb_ref, dycv_ref, do_ref, dz_ref, dgs_ref, dgc_ref, dcvin_ref,
             st_ref):
        @pl.when(pl.program_id(0) == 0)
        def _():
            st_ref[...] = jnp.zeros_like(st_ref)

        o = o_ref[...]
        z = z_ref[...]
        sz = _sigmoid(z)
        a = (o * z * sz).astype(BF16)
        a_ref[...] = a
        y_sb = _dot(a, wsb_ref[...])
        y_cv = _dot(cvin_ref[...], wcv_ref[...])
        s_sb = _sigmoid(gs_ref[...])
        s_cv = _sigmoid(gc_ref[...])
        merged = (s_sb * y_sb + s_cv * y_cv).astype(BF16)
        mg_ref[...] = merged
        r = ALPHA * h0_ref[...] + _dot(merged, wout_ref[...])
        xhat, rstd = _ln_stats(r)
        g = g_ref[...]
        err = xhat * g + b_ref[...] - tgt_ref[...]
        dy = err * (1.0 / D_MODEL)
        st_ref[0:1, :] += jnp.sum(dy * xhat, axis=0, keepdims=True)
        st_ref[1:2, :] += jnp.sum(dy, axis=0, keepdims=True)
        st_ref[2:3, :] += (0.5 / D_MODEL) * jnp.sum(err * err, axis=0, keepdims=True)
        dr = _ln_bwd(dy, xhat, rstd, g)
        dr_ref[...] = dr
        drb = dr.astype(BF16)
        drb_ref[...] = drb
        dm = _dot_nt(drb, wout_ref[...])
        dy_sb = (dm * s_sb).astype(BF16)
        dy_cv = (dm * s_cv).astype(BF16)
        dysb_ref[...] = dy_sb
        dycv_ref[...] = dy_cv
        dgs_ref[...] = (dm * y_sb * s_sb * (1.0 - s_sb)).astype(BF16)
        dgc_ref[...] = (dm * y_cv * s_cv * (1.0 - s_cv)).astype(BF16)
        da = _dot_nt(dy_sb, wsb_ref[...])
        dcvin_ref[...] = _dot_nt(dy_cv, wcv_ref[...])
        do_ref[...] = (da * z * sz).astype(BF16)
        dz_ref[...] = (da * o * sz * (1.0 + z * (1.0 - sz))).astype(BF16)

    def tok(colblk=0):
        return pl.BlockSpec((tm, D_MODEL), lambda i: (i, colblk))

    wspec = pl.BlockSpec((D_MODEL, D_MODEL), lambda i: (0, 0), pipeline_mode=pl.Buffered(1))
    vec = pl.BlockSpec((1, D_MODEL), lambda i: (0, 0))
    bf = jax.ShapeDtypeStruct((t, D_MODEL), BF16)
    f32 = jax.ShapeDtypeStruct((t, D_MODEL), F32)
    return pl.pallas_call(
        body, grid=(t // tm,),
        in_specs=[tok(), tok(0), tok(4), tok(5), tok(), tok(), tok(), wspec, wspec, wspec, vec, vec],
        out_specs=[tok()] * 11 + [pl.BlockSpec((8, D_MODEL), lambda i: (0, 0))],
        out_shape=[f32] + [bf] * 9 + [f32, jax.ShapeDtypeStruct((8, D_MODEL), F32)],
        compiler_params=_params("arbitrary"), name="tail")(
            o, hrest, hrest, hrest, cvin, h0, tgt, w_sb, w_cv, w_out, ln_g, ln_b)


def _conv_branch_bwd(dcvin, u1, hrest, ln_g, ln_b):
    t = u1.shape[0]
    tm = min(512, t)

    def body(dc_ref, u1_ref, z_ref, g_ref, b_ref, du1_ref, dz_ref, st_ref):
        @pl.when(pl.program_id(0) == 0)
        def _():
            st_ref[...] = jnp.zeros_like(st_ref)

        xhat, rstd = _ln_stats(u1_ref[...])
        g = g_ref[...]
        u2 = xhat * g + b_ref[...]
        s2 = _sigmoid(u2)
        z = z_ref[...]
        sz = _sigmoid(z)
        dc = dc_ref[...]
        dz_ref[...] = (dc * (u2 * s2) * sz * (1.0 + z * (1.0 - sz))).astype(BF16)
        du2 = dc * (z * sz) * s2 * (1.0 + u2 * (1.0 - s2))
        st_ref[0:1, :] += jnp.sum(du2 * xhat, axis=0, keepdims=True)
        st_ref[1:2, :] += jnp.sum(du2, axis=0, keepdims=True)
        du1 = _ln_bwd(du2, xhat, rstd, g)
        du1_ref[...] = du1
        st_ref[2:3, :] += jnp.sum(du1, axis=0, keepdims=True)

    tok = pl.BlockSpec((tm, D_MODEL), lambda i: (i, 0))
    vec = pl.BlockSpec((1, D_MODEL), lambda i: (0, 0))
    return pl.pallas_call(
        body, grid=(t // tm,),
        in_specs=[tok, tok, pl.BlockSpec((tm, D_MODEL), lambda i: (i, 3)), vec, vec],
        out_specs=[tok, tok, pl.BlockSpec((8, D_MODEL), lambda i: (0, 0))],
        out_shape=[jax.ShapeDtypeStruct((t, D_MODEL), F32), jax.ShapeDtypeStruct((t, D_MODEL), BF16),
                   jax.ShapeDtypeStruct((8, D_MODEL), F32)],
        compiler_params=_params("arbitrary"), name="conv_branch_bwd")(dcvin, u1, hrest, ln_g, ln_b)


def _conv_bwd(du1, hrest, conv_w, nb, seq):
    ts = _conv_block_rows(seq)
    nblk = seq // ts
    lanes = 128
    last_halo = nb * seq // HALO - 1

    def body(du_ref, duh_ref, cv_ref, cg_ref, cvh_ref, cgh_ref, w_ref, dcv_ref, dcg_ref, dw_ref, ubuf, dbuf):
        b = pl.program_id(0)
        i = pl.program_id(1)

        @pl.when((b == 0) & (i == 0))
        def _():
            dw_ref[...] = jnp.zeros_like(dw_ref)

        cv = cv_ref[...]
        sg = _sigmoid(cg_ref[...])
        halo = cvh_ref[...] * _sigmoid(cgh_ref[...])
        ubuf[0:HALO, :] = jnp.where(i > 0, halo, 0.0)
        ubuf[HALO:HALO + ts, :] = cv * sg
        dbuf[0:ts, :] = du_ref[...]
        dbuf[ts:ts + HALO, :] = jnp.where(i < nblk - 1, duh_ref[...], 0.0)
        for cc in range(D_MODEL // lanes):
            cs = slice(cc * lanes, (cc + 1) * lanes)
            dmain = dbuf[0:ts, cs]
            acc = jnp.zeros((ts, lanes), F32)
            for k in range(CONV_K):
                acc = acc + w_ref[k:k + 1, cs] * dbuf[pl.ds(CONV_K - 1 - k, ts), cs]
                dw_ref[k:k + 1, cs] += jnp.sum(dmain * ubuf[pl.ds(HALO - CONV_K + 1 + k, ts), cs],
                                               axis=0, keepdims=True)
            dcv_ref[:, cs] = (acc * sg[:, cs]).astype(BF16)
            dcg_ref[:, cs] = (acc * cv[:, cs] * sg[:, cs] * (1.0 - sg[:, cs])).astype(BF16)

    def main(colblk):
        return pl.BlockSpec((ts, D_MODEL), lambda b, i: (b * nblk + i, colblk))

    def halo_before(colblk):
        return pl.BlockSpec((HALO, D_MODEL),
                            lambda b, i: (jnp.maximum((b * seq + i * ts) // HALO - 1, 0), colblk))

    halo_after = pl.BlockSpec((HALO, D_MODEL),
                              lambda b, i: (jnp.minimum((b * seq + (i + 1) * ts) // HALO, last_halo), 0))
    t = nb * seq
    return pl.pallas_call(
        body, grid=(nb, nblk),
        in_specs=[main(0), halo_after, main(1), main(2), halo_before(1), halo_before(2),
                  pl.BlockSpec((CONV_ROWS, D_MODEL), lambda b, i: (0, 0))],
        out_specs=[main(0), main(0), pl.BlockSpec((CONV_ROWS, D_MODEL), lambda b, i: (0, 0))],
        out_shape=[jax.ShapeDtypeStruct((t, D_MODEL), BF16), jax.ShapeDtypeStruct((t, D_MODEL), BF16),
                   jax.ShapeDtypeStruct((CONV_ROWS, D_MODEL), F32)],
        scratch_shapes=[pltpu.VMEM((HALO + ts, D_MODEL), F32), pltpu.VMEM((ts + HALO, D_MODEL), F32)],
        compiler_params=_params("arbitrary", "arbitrary"), name="conv_bwd")(
            du1, du1, hrest, hrest, hrest, hrest, conv_w)


def _weight_grad(x, dys, name):
    n = len(dys)
    t = x.shape[0]
    tm = min(512, t)

    def body(x_ref, *refs):
        o_ref = refs[n]
        j = pl.program_id(0)

        @pl.when(pl.program_id(1) == 0)
        def _():
            o_ref[...] = jnp.zeros_like(o_ref)

        for p in range(n):
            @pl.when(j == p)
            def _(p=p):
                o_ref[...] += _dot_tn(x_ref[...], refs[p][...])

    def dy_spec(p):
        return pl.BlockSpec((tm, D_MODEL), lambda j, i: (jnp.where(j == p, i, 0), 0))

    return pl.pallas_call(
        body, grid=(n, t // tm),
        in_specs=[pl.BlockSpec((tm, D_MODEL), lambda j, i: (i, 0))] + [dy_spec(p) for p in range(n)],
        out_specs=pl.BlockSpec((D_MODEL, D_MODEL), lambda j, i: (0, j)),
        out_shape=jax.ShapeDtypeStruct((D_MODEL, n * D_MODEL), F32),
        compiler_params=_params("parallel", "arbitrary"), name=name)(x, *dys)


def _in_proj_bwd(dhs, w_in, dr, x, ln_g):
    n = len(dhs)
    t = x.shape[0]
    tm = min(512, t)

    def body(*refs):
        dh_refs = refs[:n]
        w_ref, dr_ref, x_ref, g_ref, gx_ref, st_ref, acc = refs[n:]
        i = pl.program_id(0)
        p = pl.program_id(1)

        @pl.when((i == 0) & (p == 0))
        def _():
            st_ref[...] = jnp.zeros_like(st_ref)

        @pl.when(p == 0)
        def _():
            acc[...] = ALPHA * dr_ref[...]

        for q in range(n):
            @pl.when(p == q)
            def _(q=q):
                acc[...] += _dot_nt(dh_refs[q][...], w_ref[...])

        @pl.when(p == n - 1)
        def _():
            xhat, rstd = _ln_stats(x_ref[...])
            dh0 = acc[...]
            st_ref[0:1, :] += jnp.sum(dh0 * xhat, axis=0, keepdims=True)
            st_ref[1:2, :] += jnp.sum(dh0, axis=0, keepdims=True)
            gx_ref[...] = _ln_bwd(dh0, xhat, rstd, g_ref[...])

    tok = pl.BlockSpec((tm, D_MODEL), lambda i, p: (i, 0))
    return pl.pallas_call(
        body, grid=(t // tm, n),
        in_specs=[tok] * n + [pl.BlockSpec((D_MODEL, D_MODEL), lambda i, p: (0, p)), tok, tok,
                              pl.BlockSpec((1, D_MODEL), lambda i, p: (0, 0))],
        out_specs=[tok, pl.BlockSpec((8, D_MODEL), lambda i, p: (0, 0))],
        out_shape=[jax.ShapeDtypeStruct((t, D_MODEL), F32), jax.ShapeDtypeStruct((8, D_MODEL), F32)],
        scratch_shapes=[pltpu.VMEM((tm, D_MODEL), F32)],
        compiler_params=_params("arbitrary", "arbitrary"), name="in_proj_bwd")(*dhs, w_in, dr, x, ln_g)


def _local_step(x, tgt, nb, seq, w_in, w_sb, w_cv, w_out, conv_w, vecs):
    h0, h0b = _ln_in(x, vecs["ln_in_g"], vecs["ln_in_b"])
    qkv = _in_proj(h0b, w_in, 0, 3, BF16, True, "in_proj_qkv")
    hrest = _in_proj(h0b, w_in, 3, 6, F32, False, "in_proj_rest")
    o = _attn_fwd(qkv, nb, seq)
    u1, cvin = _conv_fwd(hrest, conv_w, vecs["conv_b"], vecs["conv_ln_g"], vecs["conv_ln_b"], nb, seq)
    (dr, drb, a, merged, dy_sb, dy_cv, do, dz_sb, dg_sb, dg_cv, dcvin, st_tail) = _tail(
        o, hrest, cvin, h0, tgt, w_sb, w_cv, w_out, vecs["ln_post_g"], vecs["ln_post_b"])
    d_w_sb = _weight_grad(a, [dy_sb], "grad_w_sb")
    d_w_cv = _weight_grad(cvin, [dy_cv], "grad_w_cv")
    d_w_out = _weight_grad(merged, [drb], "grad_w_out")
    du1, dz_cv, st_conv = _conv_branch_bwd(dcvin, u1, hrest, vecs["conv_ln_g"], vecs["conv_ln_b"])
    dc_val, dc_gate, d_conv_w = _conv_bwd(du1, hrest, conv_w, nb, seq)
    dq, dk, dv = _attn_bwd(qkv, do, nb, seq)
    dhs = [dq, dk, dv, dz_sb, dc_val, dc_gate, dz_cv, dg_sb, dg_cv]
    d_w_in = _weight_grad(h0b, dhs, "grad_w_in")
    grad_x, st_in = _in_proj_bwd(dhs, w_in, dr, x, vecs["ln_in_g"])
    small = jnp.concatenate([st_in[0:2], st_conv[2:3], st_conv[0:2], st_tail[0:3], d_conv_w], axis=0)
    return grad_x, (d_w_in, d_w_sb, d_w_cv, d_w_out), small


def _place():
    x, y, c = lax.axis_index("x"), lax.axis_index("y"), lax.axis_index("c")
    chips = [(1 - x, y), (x, 1 - y), (1 - x, 1 - y)]
    return x, y, c, chips


def _gather_weights(w_in, w_sb, w_cv, w_out, conv_w):
    n_arr = 5

    def body(i0, i1, i2, i3, i4, o0, o1, o2, o3, o4, send_sems, recv_sems, local_sems):
        x, y, c, chips = _place()
        srcs = [i0, i1, i2, i3, i4]
        dsts = [o0, o1, o2, o3, o4]

        def slot(a, px, py):
            chip = 2 * px + py
            if a == 0:
                return dsts[a].at[:, pl.ds(pl.multiple_of(chip * SHARD_IN, 128), SHARD_IN)]
            if a == 4:
                return dsts[a].at[:, pl.ds(pl.multiple_of(chip * SHARD_SQ, 128), SHARD_SQ)]
            return dsts[a].at[pl.ds(pl.multiple_of(chip * SHARD_SQ, 8), SHARD_SQ), :]

        def remote(a, j):
            px, py = chips[j]
            return pltpu.make_async_remote_copy(
                src_ref=srcs[a], dst_ref=slot(a, x, y), send_sem=send_sems.at[3 * a + j],
                recv_sem=recv_sems.at[3 * a + j], device_id=(px, py, c), device_id_type=MESH)

        def arrival(a, j):
            px, py = chips[j]
            return pltpu.make_async_remote_copy(
                src_ref=srcs[a], dst_ref=slot(a, px, py), send_sem=send_sems.at[3 * a + j],
                recv_sem=recv_sems.at[3 * a + j], device_id=(px, py, c), device_id_type=MESH)

        local = [pltpu.make_async_copy(srcs[a], slot(a, x, y), local_sems.at[a]) for a in range(n_arr)]
        for cp in local:
            cp.start()
        sends = [remote(a, j) for a in range(n_arr) for j in range(3)]
        for cp in sends:
            cp.start()
        for a in range(n_arr):
            for j in range(3):
                arrival(a, j).wait_recv()
        for cp in sends:
            cp.wait_send()
        for cp in local:
            cp.wait()

    return pl.pallas_call(
        body, in_specs=[ANY] * n_arr, out_specs=[ANY] * n_arr,
        out_shape=[jax.ShapeDtypeStruct((D_MODEL, IN_WIDTH), BF16)] + [jax.ShapeDtypeStruct((D_MODEL, D_MODEL), BF16)] * 3
        + [jax.ShapeDtypeStruct((CONV_ROWS, D_MODEL), F32)],
        scratch_shapes=[pltpu.SemaphoreType.DMA((3 * n_arr,)), pltpu.SemaphoreType.DMA((3 * n_arr,)),
                        pltpu.SemaphoreType.DMA((n_arr,))],
        name="gather_weights")(w_in, w_sb, w_cv, w_out, conv_w)


def _pair_exchange(d_in, d_sb, d_cv, d_out):
    def body(i0, i1, i2, i3, m0, m1, m2, m3, r0, r1, r2, r3, send_sems, recv_sems, local_sems):
        x, y, c, _ = _place()
        srcs = [i0, i1, i2, i3]
        mine = [m0, m1, m2, m3]
        theirs = [r0, r1, r2, r3]

        def half(a, which):
            start = pl.multiple_of(which * HALF, 128)
            if a == 0:
                return srcs[a].at[pl.ds(start, HALF), :]
            return srcs[a].at[:, pl.ds(start, HALF)]

        local = [pltpu.make_async_copy(half(a, c), mine[a], local_sems.at[a]) for a in range(4)]
        sends = [pltpu.make_async_remote_copy(
            src_ref=half(a, 1 - c), dst_ref=theirs[a], send_sem=send_sems.at[a], recv_sem=recv_sems.at[a],
            device_id=(x, y, 1 - c), device_id_type=MESH) for a in range(4)]
        for cp in local + sends:
            cp.start()
        for cp in sends:
            cp.wait_recv()
        for cp in sends:
            cp.wait_send()
        for cp in local:
            cp.wait()

    shapes = [jax.ShapeDtypeStruct((HALF, IN_WIDTH), F32)] + [jax.ShapeDtypeStruct((D_MODEL, HALF), F32)] * 3
    return pl.pallas_call(
        body, in_specs=[ANY] * 4, out_specs=[ANY] * 8, out_shape=shapes + shapes,
        scratch_shapes=[pltpu.SemaphoreType.DMA((4,)), pltpu.SemaphoreType.DMA((4,)), pltpu.SemaphoreType.DMA((4,))],
        name="pair_exchange")(d_in, d_sb, d_cv, d_out)


def _chip_exchange(p_in, p_sb, p_cv, p_out):
    def body(i0, i1, i2, i3, m0, m1, m2, m3, r0, r1, r2, r3, send_sems, recv_sems, local_sems):
        x, y, c, chips = _place()
        srcs = [i0, i1, i2, i3]
        mine = [m0, m1, m2, m3]
        theirs = [r0, r1, r2, r3]

        def shard(a, px, py):
            chip = 2 * px + py
            if a == 0:
                return srcs[a].at[:, pl.ds(pl.multiple_of(chip * SHARD_IN, 128), SHARD_IN)]
            return srcs[a].at[pl.ds(pl.multiple_of(chip * SHARD_SQ, 8), SHARD_SQ), :]

        local = [pltpu.make_async_copy(shard(a, x, y), mine[a], local_sems.at[a]) for a in range(4)]
        sends = []
        for a in range(4):
            for j, (px, py) in enumerate(chips):
                sends.append(pltpu.make_async_remote_copy(
                    src_ref=shard(a, px, py), dst_ref=theirs[a].at[j], send_sem=send_sems.at[3 * a + j],
                    recv_sem=recv_sems.at[3 * a + j], device_id=(px, py, c), device_id_type=MESH))
        for cp in local + sends:
            cp.start()
        for cp in sends:
            cp.wait_recv()
        for cp in sends:
            cp.wait_send()
        for cp in local:
            cp.wait()

    own = [jax.ShapeDtypeStruct((HALF, SHARD_IN), F32)] + [jax.ShapeDtypeStruct((SHARD_SQ, HALF), F32)] * 3
    got = [jax.ShapeDtypeStruct((3, HALF, SHARD_IN), F32)] + [jax.ShapeDtypeStruct((3, SHARD_SQ, HALF), F32)] * 3
    return pl.pallas_call(
        body, in_specs=[ANY] * 4, out_specs=[ANY] * 8, out_shape=own + got,
        scratch_shapes=[pltpu.SemaphoreType.DMA((12,)), pltpu.SemaphoreType.DMA((12,)), pltpu.SemaphoreType.DMA((4,))],
        name="chip_exchange")(p_in, p_sb, p_cv, p_out)


def _pair_share(f_in, f_sb, f_cv, f_out):
    def body(i0, i1, i2, i3, o0, o1, o2, o3, send_sems, recv_sems, local_sems):
        x, y, c, _ = _place()
        srcs = [i0, i1, i2, i3]
        dsts = [o0, o1, o2, o3]

        def half(a, which):
            start = pl.multiple_of(which * HALF, 128)
            if a == 0:
                return dsts[a].at[pl.ds(start, HALF), :]
            return dsts[a].at[:, pl.ds(start, HALF)]

        local = [pltpu.make_async_copy(srcs[a], half(a, c), local_sems.at[a]) for a in range(4)]
        sends = [pltpu.make_async_remote_copy(
            src_ref=srcs[a], dst_ref=half(a, c), send_sem=send_sems.at[a], recv_sem=recv_sems.at[a],
            device_id=(x, y, 1 - c), device_id_type=MESH) for a in range(4)]
        arrivals = [pltpu.make_async_remote_copy(
            src_ref=srcs[a], dst_ref=half(a, 1 - c), send_sem=send_sems.at[a], recv_sem=recv_sems.at[a],
            device_id=(x, y, 1 - c), device_id_type=MESH) for a in range(4)]
        for cp in local + sends:
            cp.start()
        for cp in arrivals:
            cp.wait_recv()
        for cp in sends:
            cp.wait_send()
        for cp in local:
            cp.wait()

    return pl.pallas_call(
        body, in_specs=[ANY] * 4, out_specs=[ANY] * 4,
        out_shape=[jax.ShapeDtypeStruct((D_MODEL, SHARD_IN), F32)] + [jax.ShapeDtypeStruct((SHARD_SQ, D_MODEL), F32)] * 3,
        scratch_shapes=[pltpu.SemaphoreType.DMA((4,)), pltpu.SemaphoreType.DMA((4,)), pltpu.SemaphoreType.DMA((4,))],
        name="pair_share")(f_in, f_sb, f_cv, f_out)


def _small_allreduce(small):
    def body(s_ref, o_ref, slots, send_sems, recv_sems):
        x, y, c, _ = _place()
        me = 4 * x + 2 * y + c
        slots[me] = s_ref[...]
        sends = []
        for k in range(1, N_DEV):
            px, py, pc = (x + (k >> 2)) % 2, (y + ((k >> 1) & 1)) % 2, (c + (k & 1)) % 2
            sends.append(pltpu.make_async_remote_copy(
                src_ref=s_ref, dst_ref=slots.at[me], send_sem=send_sems.at[k - 1], recv_sem=recv_sems.at[k - 1],
                device_id=(px, py, pc), device_id_type=MESH))
        for cp in sends:
            cp.start()
        for k in range(1, N_DEV):
            px, py, pc = (x + (k >> 2)) % 2, (y + ((k >> 1) & 1)) % 2, (c + (k & 1)) % 2
            pltpu.make_async_remote_copy(
                src_ref=s_ref, dst_ref=slots.at[4 * px + 2 * py + pc], send_sem=send_sems.at[k - 1],
                recv_sem=recv_sems.at[k - 1], device_id=(px, py, pc), device_id_type=MESH).wait_recv()
        for cp in sends:
            cp.wait_send()
        total = slots[0]
        for d in range(1, N_DEV):
            total = total + slots[d]
        o_ref[...] = total
        o_ref[7:8, :] = jnp.zeros((1, D_MODEL), F32) + jnp.sum(total[7:8, :], axis=1, keepdims=True)

    return pl.pallas_call(
        body, in_specs=[pl.BlockSpec(memory_space=pltpu.VMEM)], out_specs=pl.BlockSpec(memory_space=pltpu.VMEM),
        out_shape=jax.ShapeDtypeStruct((SMALL_ROWS, D_MODEL), F32),
        scratch_shapes=[pltpu.VMEM((N_DEV, SMALL_ROWS, D_MODEL), F32), pltpu.SemaphoreType.DMA((N_DEV - 1,)),
                        pltpu.SemaphoreType.DMA((N_DEV - 1,))],
        name="small_allreduce")(small)


def _add2(a, b, name):
    rows, cols = a.shape
    tr = rows // 4

    def body(a_ref, b_ref, o_ref):
        o_ref[...] = a_ref[...] + b_ref[...]

    spec = pl.BlockSpec((tr, cols), lambda i: (i, 0))
    return pl.pallas_call(body, grid=(4,), in_specs=[spec, spec], out_specs=spec,
                          out_shape=jax.ShapeDtypeStruct(a.shape, F32),
                          compiler_params=_params("parallel"), name=name)(a, b)


def _add4(own, got, name):
    rows, cols = own.shape
    tr = rows // 4

    def body(a_ref, g_ref, o_ref):
        o_ref[...] = ((a_ref[...] + g_ref[0]) + g_ref[1]) + g_ref[2]

    return pl.pallas_call(
        body, grid=(4,),
        in_specs=[pl.BlockSpec((tr, cols), lambda i: (i, 0)), pl.BlockSpec((3, tr, cols), lambda i: (0, i, 0))],
        out_specs=pl.BlockSpec((tr, cols), lambda i: (i, 0)), out_shape=jax.ShapeDtypeStruct(own.shape, F32),
        compiler_params=_params("parallel"), name=name)(own, got)


def _adamw(w, g, m, v, name):
    rows, cols = w.shape
    tr = rows // 4 if rows % 32 == 0 else rows
    c1 = 1.0 - ADAM_B1 ** ADAM_STEP
    c2 = 1.0 - ADAM_B2 ** ADAM_STEP

    def body(w_ref, g_ref, m_ref, v_ref, d_ref, mo_ref, vo_ref):
        g = g_ref[...]
        mn = ADAM_B1 * m_ref[...] + (1.0 - ADAM_B1) * g
        vn = ADAM_B2 * v_ref[...] + (1.0 - ADAM_B2) * (g * g)
        mo_ref[...] = mn
        vo_ref[...] = vn
        d_ref[...] = -ADAM_LR * ((mn / c1) / (jnp.sqrt(vn / c2) + ADAM_EPS) + ADAM_WD * w_ref[...])

    spec = pl.BlockSpec((tr, cols), lambda i: (i, 0))
    shape = jax.ShapeDtypeStruct(w.shape, F32)
    return pl.pallas_call(body, grid=(rows // tr,), in_specs=[spec] * 4, out_specs=[spec] * 3,
                          out_shape=[shape] * 3, compiler_params=_params("parallel"), name=name)(w, g, m, v)


def _reduce_weight_grads(grads):
    names = ("in", "sb", "cv", "out")
    ex = _pair_exchange(*grads)
    pair = [_add2(ex[a], ex[4 + a], "pair_sum_" + names[a]) for a in range(4)]
    ex = _chip_exchange(*pair)
    done = [_add4(ex[a], ex[4 + a], "chip_sum_" + names[a]) for a in range(4)]
    return _pair_share(*done)


def kernel(x, ln_in_g, ln_in_b, w_in, w_sb_proj, conv_w, conv_b, conv_ln_g, conv_ln_b, w_cv_proj, w_out, ln_post_g, ln_post_b, loss_target, m_ln_in_g, m_ln_in_b, m_w_in, m_w_sb_proj, m_conv_w, m_conv_b, m_conv_ln_g, m_conv_ln_b, m_w_cv_proj, m_w_out, m_ln_post_g, m_ln_post_b, v_ln_in_g, v_ln_in_b, v_w_in, v_w_sb_proj, v_conv_w, v_conv_b, v_conv_ln_g, v_conv_ln_b, v_w_cv_proj, v_w_out, v_ln_post_g, v_ln_post_b):
    nb, seq, _ = x.shape
    t = nb * seq
    vec_names = ("ln_in_g", "ln_in_b", "conv_b", "conv_ln_g", "conv_ln_b", "ln_post_g", "ln_post_b")
    vec_w = dict(zip(vec_names, (ln_in_g, ln_in_b, conv_b, conv_ln_g, conv_ln_b, ln_post_g, ln_post_b)))
    vec_m = dict(zip(vec_names, (m_ln_in_g, m_ln_in_b, m_conv_b, m_conv_ln_g, m_conv_ln_b, m_ln_post_g, m_ln_post_b)))
    vec_v = dict(zip(vec_names, (v_ln_in_g, v_ln_in_b, v_conv_b, v_conv_ln_g, v_conv_ln_b, v_ln_post_g, v_ln_post_b)))
    vecs = {k: a.reshape(1, D_MODEL) for k, a in vec_w.items()}

    pad_taps = lambda a: jnp.pad(a.reshape(CONV_K, SHARD_SQ), ((0, CONV_ROWS - CONV_K), (0, 0)))
    gathered = _gather_weights(w_in[0].astype(BF16), w_sb_proj[0].astype(BF16), w_cv_proj[0].astype(BF16),
                               w_out[0].astype(BF16), pad_taps(conv_w))
    full_in, full_sb, full_cv, full_out, full_conv = gathered

    grad_x, big, small = _local_step(x.reshape(t, D_MODEL), loss_target.reshape(t, D_MODEL), nb, seq,
                                     full_in, full_sb, full_cv, full_out, full_conv, vecs)
    g_in, g_sb, g_cv, g_out = _reduce_weight_grads(big)
    small = _small_allreduce(small)

    chip = 2 * lax.axis_index("x") + lax.axis_index("y")
    g_conv = lax.dynamic_slice(small, (8, chip * SHARD_SQ), (CONV_ROWS, SHARD_SQ))
    loss = small[7, 0]

    grads, deltas, new_m, new_v = {}, {}, {}, {}
    stack = lambda d: jnp.concatenate([d[k].reshape(1, D_MODEL) for k in vec_names] + [jnp.zeros((1, D_MODEL), F32)])
    vd, vm, vv = _adamw(stack(vec_w), small[0:8], stack(vec_m), stack(vec_v), "adamw_vectors")
    for r, k in enumerate(vec_names):
        shape = vec_w[k].shape
        grads[k] = small[r].reshape(shape)
        deltas[k], new_m[k], new_v[k] = vd[r].reshape(shape), vm[r].reshape(shape), vv[r].reshape(shape)

    big_w = {"w_in": (w_in, g_in, m_w_in, v_w_in), "w_sb_proj": (w_sb_proj, g_sb, m_w_sb_proj, v_w_sb_proj),
             "w_cv_proj": (w_cv_proj, g_cv, m_w_cv_proj, v_w_cv_proj), "w_out": (w_out, g_out, m_w_out, v_w_out)}
    for k, (w, g, m, v) in big_w.items():
        d, mn, vn = _adamw(w[0], g, m[0], v[0], "adamw_" + k)
        grads[k], deltas[k], new_m[k], new_v[k] = g[None], d[None], mn[None], vn[None]
    d, mn, vn = _adamw(pad_taps(conv_w), g_conv, pad_taps(m_conv_w), pad_taps(v_conv_w), "adamw_conv_w")
    grads["conv_w"] = g_conv[None, :CONV_K]
    deltas["conv_w"], new_m["conv_w"], new_v["conv_w"] = d[None, :CONV_K], mn[None, :CONV_K], vn[None, :CONV_K]

    order = ("ln_in_g", "ln_in_b", "w_in", "w_sb_proj", "conv_w", "conv_b", "conv_ln_g", "conv_ln_b",
             "w_cv_proj", "w_out", "ln_post_g", "ln_post_b")
    return (loss, grad_x.reshape(x.shape), *[grads[k] for k in order], *[deltas[k] for k in order],
            *[new_m[k] for k in order], *[new_v[k] for k in order])
```

```python
import functools

import jax
import jax.numpy as jnp
from jax import lax
from jax.experimental import pallas as pl
from jax.experimental.pallas import tpu as pltpu

F32 = jnp.float32
BF16 = jnp.bfloat16

D_MODEL = 1024
N_HEADS = 16
HEAD_DIM = 64
HEAD_PAIRS = N_HEADS // 2
N_PIECES = 9
IN_WIDTH = N_PIECES * D_MODEL
Q_BLOCK = 256
CONV_K = 31
CONV_ROWS = 32
HALO = 32
LN_EPS = 1e-5
ALPHA = 2.0 ** 0.25
Q_SCALE = 0.125
N_CHIPS = 4
N_DEV = 8
SHARD_IN = IN_WIDTH // N_CHIPS
SHARD_SQ = D_MODEL // N_CHIPS
HALF = D_MODEL // 2
SMALL_ROWS = 40
N_CHUNKS = 16

ADAM_LR = 0.001
ADAM_B1 = 0.9
ADAM_B2 = 0.999
ADAM_EPS = 1e-08
ADAM_WD = 0.01
ADAM_STEP = 10

MESH = pl.DeviceIdType.MESH
ANY = pl.BlockSpec(memory_space=pl.ANY)
VMEM_LIMIT = 60 * 1024 * 1024

NT = (((1,), (1,)), ((), ()))
TN = (((0,), (0,)), ((), ()))


def _sigmoid(x):
    return 1.0 / (1.0 + jnp.exp(-x))


def _dot(a, b):
    return jnp.dot(a, b, preferred_element_type=F32)


def _dot_nt(a, b):
    return lax.dot_general(a, b, NT, preferred_element_type=F32)


def _dot_tn(a, b):
    return lax.dot_general(a, b, TN, preferred_element_type=F32)


def _split_bf16(x):
    hi = x.astype(BF16)
    lo = (x - hi.astype(F32)).astype(BF16)
    return hi, lo


def _ln_stats(x):
    mu = jnp.mean(x, axis=-1, keepdims=True)
    xc = x - mu
    var = jnp.mean(xc * xc, axis=-1, keepdims=True)
    rstd = lax.rsqrt(var + LN_EPS)
    return xc * rstd, rstd


def _ln_bwd(dy, xhat, rstd, g):
    dxh = dy * g
    m1 = jnp.mean(dxh, axis=-1, keepdims=True)
    m2 = jnp.mean(dxh * xhat, axis=-1, keepdims=True)
    return rstd * (dxh - m1 - xhat * m2)


def _params(*sem):
    return pltpu.CompilerParams(dimension_semantics=sem, vmem_limit_bytes=VMEM_LIMIT)


def _ln_in(x, g, b):
    t = x.shape[0]
    tm = min(512, t)

    def body(x_ref, g_ref, b_ref, hf_ref, hb_ref):
        xhat, _ = _ln_stats(x_ref[...])
        y = xhat * g_ref[...] + b_ref[...]
        hf_ref[...] = y
        hb_ref[...] = y.astype(BF16)

    row = pl.BlockSpec((tm, D_MODEL), lambda i: (i, 0))
    vec = pl.BlockSpec((1, D_MODEL), lambda i: (0, 0))
    return pl.pallas_call(
        body, grid=(t // tm,), in_specs=[row, vec, vec], out_specs=[row, row],
        out_shape=[jax.ShapeDtypeStruct((t, D_MODEL), F32), jax.ShapeDtypeStruct((t, D_MODEL), BF16)],
        compiler_params=_params("parallel"), name="ln_in")(x, g, b)


def _in_proj(h, w, col0, ncol, out_dtype, scale_first, name):
    t = h.shape[0]
    tm = min(512, t)

    def body(h_ref, w_ref, o_ref):
        res = _dot(h_ref[...], w_ref[...])
        if scale_first:
            res = res * jnp.where(pl.program_id(0) == 0, Q_SCALE, 1.0)
        o_ref[...] = res.astype(out_dtype)

    return pl.pallas_call(
        body, grid=(ncol, t // tm),
        in_specs=[pl.BlockSpec((tm, D_MODEL), lambda j, i: (i, 0)),
                  pl.BlockSpec((D_MODEL, D_MODEL), lambda j, i: (0, j + col0))],
        out_specs=pl.BlockSpec((tm, D_MODEL), lambda j, i: (i, j)),
        out_shape=jax.ShapeDtypeStruct((t, ncol * D_MODEL), out_dtype),
        compiler_params=_params("parallel", "parallel"), name=name)(h, w)


def _sb_block(l, diag, tri, carry):
    lnb = jnp.minimum(-l, 0.0) - jnp.log(1.0 + jnp.exp(-jnp.abs(l)))
    if diag is not None:
        lnb = jnp.where(diag, lnb, 0.0)
    hi, lo = _split_bf16(lnb)
    suffix = _dot(hi, tri) + _dot(lo, tri)
    a = jnp.exp(l + suffix + carry)
    if diag is not None:
        a = jnp.where(diag, a, 0.0)
    return a, lnb, suffix[:, 0:1]


def _attn_consts():
    lane = lax.broadcasted_iota(jnp.int32, (Q_BLOCK, 2 * HEAD_DIM), 1)
    row = lax.broadcasted_iota(jnp.int32, (Q_BLOCK, Q_BLOCK), 0)
    col = lax.broadcasted_iota(jnp.int32, (Q_BLOCK, Q_BLOCK), 1)
    return lane, col < row, (row >= col).astype(BF16), (row <= col).astype(BF16)


def _split_heads(t, lane):
    zero = jnp.zeros_like(t)
    return jnp.where(lane < HEAD_DIM, t, zero), jnp.where(lane >= HEAD_DIM, t, zero)


def _attn_fwd(qkv, nb, seq):
    nq = seq // Q_BLOCK

    def body(q_ref, k_ref, v_ref, o_ref):
        qi = pl.program_id(2)
        lane, diag, tri, _ = _attn_consts()
        q = q_ref[...]
        qh = _split_heads(q, lane)

        def block(ks, state, masked):
            kblk = k_ref[pl.ds(ks, Q_BLOCK), :]
            vblk = v_ref[pl.ds(ks, Q_BLOCK), :]
            out = []
            for hd in range(2):
                carry, acc = state[hd]
                a, _, blk_sum = _sb_block(_dot_nt(qh[hd], kblk), diag if masked else None, tri, carry)
                out.append((carry + blk_sum, acc + _dot(a.astype(BF16), vblk)))
            return tuple(out)

        def step(jj, state):
            return block(pl.multiple_of((qi - jj) * Q_BLOCK, Q_BLOCK), state, False)

        init = (jnp.zeros((Q_BLOCK, 1), F32), jnp.zeros((Q_BLOCK, 2 * HEAD_DIM), F32))
        state = block(pl.multiple_of(qi * Q_BLOCK, Q_BLOCK), (init, init), True)
        state = lax.fori_loop(1, qi + 1, step, state)
        o_ref[...] = jnp.where(lane < HEAD_DIM, state[0][1], state[1][1])

    w = 2 * HEAD_DIM
    return pl.pallas_call(
        body, grid=(nb, HEAD_PAIRS, nq),
        in_specs=[pl.BlockSpec((Q_BLOCK, w), lambda b, h, i: (b * nq + i, h)),
                  pl.BlockSpec((seq, w), lambda b, h, i: (b, HEAD_PAIRS + h)),
                  pl.BlockSpec((seq, w), lambda b, h, i: (b, 2 * HEAD_PAIRS + h))],
        out_specs=pl.BlockSpec((Q_BLOCK, w), lambda b, h, i: (b * nq + i, h)),
        out_shape=jax.ShapeDtypeStruct((nb * seq, D_MODEL), F32),
        compiler_params=_params("parallel", "parallel", "arbitrary"), name="attn_fwd")(qkv, qkv, qkv)


def _attn_bwd(qkv, do, nb, seq):
    nq = seq // Q_BLOCK

    def body(q_ref, k_ref, v_ref, do_ref, dq_ref, dk_ref, dv_ref, g_s, nb_s, dk_acc, dv_acc):
        qi = pl.program_id(2)

        @pl.when(qi == 0)
        def _():
            dk_acc[...] = jnp.zeros_like(dk_acc)
            dv_acc[...] = jnp.zeros_like(dv_acc)

        lane, diag, tri_suffix, tri_prefix = _attn_consts()
        qh = _split_heads(q_ref[...], lane)
        doh = _split_heads(do_ref[...], lane)

        def block_a(kb, carries, masked):
            ks = pl.multiple_of(kb * Q_BLOCK, Q_BLOCK)
            kblk = k_ref[pl.ds(ks, Q_BLOCK), :]
            vblk = v_ref[pl.ds(ks, Q_BLOCK), :]
            out = []
            dv = None
            for hd in range(2):
                a, lnb, blk_sum = _sb_block(_dot_nt(qh[hd], kblk), diag if masked else None, tri_suffix, carries[hd])
                g_s[hd, kb] = a * _dot_nt(doh[hd], vblk)
                nb_s[hd, kb] = jnp.exp(lnb)
                part = _dot_tn(a.astype(BF16), doh[hd])
                dv = part if dv is None else dv + part
                out.append(carries[hd] + blk_sum)
            dv_acc[pl.ds(ks, Q_BLOCK), :] += dv
            return tuple(out)

        zero_col = jnp.zeros((Q_BLOCK, 1), F32)
        carries = block_a(qi, (zero_col, zero_col), True)
        lax.fori_loop(1, qi + 1, lambda jj, cs: block_a(qi - jj, cs, False), carries)

        def block_b(kb, state, masked):
            ks = pl.multiple_of(kb * Q_BLOCK, Q_BLOCK)
            kblk = k_ref[pl.ds(ks, Q_BLOCK), :]
            out = []
            dk = None
            for hd in range(2):
                prefix, dq = state[hd]
                g = g_s[hd, kb]
                not_beta = nb_s[hd, kb]
                hi, lo = _split_bf16(g)
                incl = _dot(hi, tri_prefix) + _dot(lo, tri_prefix)
                before = incl - g + prefix
                dl = not_beta * (g + before) - before
                if masked:
                    dl = jnp.where(diag, dl, 0.0)
                dl = dl.astype(BF16)
                part = _dot_tn(dl, qh[hd])
                dk = part if dk is None else dk + part
                out.append((prefix + incl[:, Q_BLOCK - 1:Q_BLOCK], dq + _dot(dl, kblk)))
            dk_acc[pl.ds(ks, Q_BLOCK), :] += dk
            return tuple(out)

        init = (zero_col, jnp.zeros((Q_BLOCK, 2 * HEAD_DIM), F32))
        state = lax.fori_loop(0, qi, lambda kb, st: block_b(kb, st, False), (init, init))
        state = block_b(qi, state, True)
        dq_ref[...] = (jnp.where(lane < HEAD_DIM, state[0][1], state[1][1]) * Q_SCALE).astype(BF16)

        @pl.when(qi == nq - 1)
        def _():
            dk_ref[...] = dk_acc[...].astype(BF16)
            dv_ref[...] = dv_acc[...].astype(BF16)

    w = 2 * HEAD_DIM
    t = nb * seq
    qspec = pl.BlockSpec((Q_BLOCK, w), lambda b, h, i: (b * nq + i, h))
    kvout = pl.BlockSpec((seq, w), lambda b, h, i: (b, h))
    return pl.pallas_call(
        body, grid=(nb, HEAD_PAIRS, nq),
        in_specs=[qspec,
                  pl.BlockSpec((seq, w), lambda b, h, i: (b, HEAD_PAIRS + h)),
                  pl.BlockSpec((seq, w), lambda b, h, i: (b, 2 * HEAD_PAIRS + h)),
                  qspec],
        out_specs=[qspec, kvout, kvout],
        out_shape=[jax.ShapeDtypeStruct((t, D_MODEL), BF16)] * 3,
        scratch_shapes=[pltpu.VMEM((2, nq, Q_BLOCK, Q_BLOCK), F32), pltpu.VMEM((2, nq, Q_BLOCK, Q_BLOCK), F32),
                        pltpu.VMEM((seq, w), F32), pltpu.VMEM((seq, w), F32)],
        compiler_params=_params("parallel", "parallel", "arbitrary"), name="attn_bwd")(qkv, qkv, qkv, do)


def _conv_block_rows(seq):
    return min(256, seq)


def _conv_fwd(hrest, conv_w, conv_b, ln_g, ln_b, nb, seq):
    ts = _conv_block_rows(seq)
    nblk = seq // ts
    lanes = 128

    def body(cv_ref, cg_ref, cvh_ref, cgh_ref, z_ref, w_ref, cb_ref, g_ref, b_ref, u1_ref, cvin_ref, ubuf):
        i = pl.program_id(1)
        halo = cvh_ref[...] * _sigmoid(cgh_ref[...])
        ubuf[0:HALO, :] = jnp.where(i > 0, halo, 0.0)
        ubuf[HALO:HALO + ts, :] = cv_ref[...] * _sigmoid(cg_ref[...])
        for cc in range(D_MODEL // lanes):
            cs = slice(cc * lanes, (cc + 1) * lanes)
            acc = jnp.zeros((ts, lanes), F32) + cb_ref[:, cs]
            for k in range(CONV_K):
                acc = acc + w_ref[k:k + 1, cs] * ubuf[pl.ds(HALO - CONV_K + 1 + k, ts), cs]
            u1_ref[:, cs] = acc
        xhat, _ = _ln_stats(u1_ref[...])
        u2 = xhat * g_ref[...] + b_ref[...]
        z = z_ref[...]
        cvin_ref[...] = (u2 * _sigmoid(u2) * z * _sigmoid(z)).astype(BF16)

    def main(colblk):
        return pl.BlockSpec((ts, D_MODEL), lambda b, i: (b * nblk + i, colblk))

    def halo(colblk):
        return pl.BlockSpec((HALO, D_MODEL),
                            lambda b, i: (jnp.maximum((b * seq + i * ts) // HALO - 1, 0), colblk))

    vec = pl.BlockSpec((1, D_MODEL), lambda b, i: (0, 0))
    t = nb * seq
    return pl.pallas_call(
        body, grid=(nb, nblk),
        in_specs=[main(1), main(2), halo(1), halo(2), main(3),
                  pl.BlockSpec((CONV_ROWS, D_MODEL), lambda b, i: (0, 0)), vec, vec, vec],
        out_specs=[main(0), main(0)],
        out_shape=[jax.ShapeDtypeStruct((t, D_MODEL), F32), jax.ShapeDtypeStruct((t, D_MODEL), BF16)],
        scratch_shapes=[pltpu.VMEM((HALO + ts, D_MODEL), F32)],
        compiler_params=_params("parallel", "parallel"), name="conv_fwd")(
            hrest, hrest, hrest, hrest, hrest, conv_w, conv_b, ln_g, ln_b)


def _tail(o, hrest, cvin, h0, tgt, w_sb, w_cv, w_out, ln_g, ln_b):
    t = o.shape[0]
    tm = min(256, t)

    def body(o_ref, z_ref, gs_ref, gc_ref, cvin_ref, h0_ref, tgt_ref, wsb_ref, wcv_ref, wout_ref, g_ref, b_ref,
             dr_ref, drb_ref, a_ref, mg_ref, dysb_ref, dycv_ref, do_ref, dz_ref, dgs_ref, dgc_ref, dcvin_ref,
             st_ref):
        @pl.when(pl.program_id(0) == 0)
        def _():
            st_ref[...] = jnp.zeros_like(st_ref)

        o = o_ref[...]
        z = z_ref[...]
        sz = _sigmoid(z)
        a = (o * z * sz).astype(BF16)
        a_ref[...] = a
        y_sb = _dot(a, wsb_ref[...])
        y_cv = _dot(cvin_ref[...], wcv_ref[...])
        s_sb = _sigmoid(gs_ref[...])
        s_cv = _sigmoid(gc_ref[...])
        merged = (s_sb * y_sb + s_cv * y_cv).astype(BF16)
        mg_ref[...] = merged
        r = ALPHA * h0_ref[...] + _dot(merged, wout_ref[...])
        xhat, rstd = _ln_stats(r)
        g = g_ref[...]
        err = xhat * g + b_ref[...] - tgt_ref[...]
        dy = err * (1.0 / D_MODEL)
        st_ref[0:1, :] += jnp.sum(dy * xhat, axis=0, keepdims=True)
        st_ref[1:2, :] += jnp.sum(dy, axis=0, keepdims=True)
        st_ref[2:3, :] += (0.5 / D_MODEL) * jnp.sum(err * err, axis=0, keepdims=True)
        dr = _ln_bwd(dy, xhat, rstd, g)
        dr_ref[...] = dr
        drb = dr.astype(BF16)
        drb_ref[...] = drb
        dm = _dot_nt(drb, wout_ref[...])
        dy_sb = (dm * s_sb).astype(BF16)
        dy_cv = (dm * s_cv).astype(BF16)
        dysb_ref[...] = dy_sb
        dycv_ref[...] = dy_cv
        dgs_ref[...] = (dm * y_sb * s_sb * (1.0 - s_sb)).astype(BF16)
        dgc_ref[...] = (dm * y_cv * s_cv * (1.0 - s_cv)).astype(BF16)
        da = _dot_nt(dy_sb, wsb_ref[...])
        dcvin_ref[...] = _dot_nt(dy_cv, wcv_ref[...])
        do_ref[...] = (da * z * sz).astype(BF16)
        dz_ref[...] = (da * o * sz * (1.0 + z * (1.0 - sz))).astype(BF16)

    def tok(colblk=0):
        return pl.BlockSpec((tm, D_MODEL), lambda i: (i, colblk))

    wspec = pl.BlockSpec((D_MODEL, D_MODEL), lambda i: (0, 0), pipeline_mode=pl.Buffered(1))
    vec = pl.BlockSpec((1, D_MODEL), lambda i: (0, 0))
    bf = jax.ShapeDtypeStruct((t, D_MODEL), BF16)
    f32 = jax.ShapeDtypeStruct((t, D_MODEL), F32)
    return pl.pallas_call(
        body, grid=(t // tm,),
        in_specs=[tok(), tok(0), tok(4), tok(5), tok(), tok(), tok(), wspec, wspec, wspec, vec, vec],
        out_specs=[tok()] * 11 + [pl.BlockSpec((8, D_MODEL), lambda i: (0, 0))],
        out_shape=[f32] + [bf] * 9 + [f32, jax.ShapeDtypeStruct((8, D_MODEL), F32)],
        compiler_params=_params("arbitrary"), name="tail")(
            o, hrest, hrest, hrest, cvin, h0, tgt, w_sb, w_cv, w_out, ln_g, ln_b)


def _conv_branch_bwd(dcvin, u1, hrest, ln_g, ln_b):
    t = u1.shape[0]
    tm = min(512, t)

    def body(dc_ref, u1_ref, z_ref, g_ref, b_ref, du1_ref, dz_ref, st_ref):
        @pl.when(pl.program_id(0) == 0)
        def _():
            st_ref[...] = jnp.zeros_like(st_ref)

        xhat, rstd = _ln_stats(u1_ref[...])
        g = g_ref[...]
        u2 = xhat * g + b_ref[...]
        s2 = _sigmoid(u2)
        z = z_ref[...]
        sz = _sigmoid(z)
        dc = dc_ref[...]
        dz_ref[...] = (dc * (u2 * s2) * sz * (1.0 + z * (1.0 - sz))).astype(BF16)
        du2 = dc * (z * sz) * s2 * (1.0 + u2 * (1.0 - s2))
        st_ref[0:1, :] += jnp.sum(du2 * xhat, axis=0, keepdims=True)
        st_ref[1:2, :] += jnp.sum(du2, axis=0, keepdims=True)
        du1 = _ln_bwd(du2, xhat, rstd, g)
        du1_ref[...] = du1
        st_ref[2:3, :] += jnp.sum(du1, axis=0, keepdims=True)

    tok = pl.BlockSpec((tm, D_MODEL), lambda i: (i, 0))
    vec = pl.BlockSpec((1, D_MODEL), lambda i: (0, 0))
    return pl.pallas_call(
        body, grid=(t // tm,),
        in_specs=[tok, tok, pl.BlockSpec((tm, D_MODEL), lambda i: (i, 3)), vec, vec],
        out_specs=[tok, tok, pl.BlockSpec((8, D_MODEL), lambda i: (0, 0))],
        out_shape=[jax.ShapeDtypeStruct((t, D_MODEL), F32), jax.ShapeDtypeStruct((t, D_MODEL), BF16),
                   jax.ShapeDtypeStruct((8, D_MODEL), F32)],
        compiler_params=_params("arbitrary"), name="conv_branch_bwd")(dcvin, u1, hrest, ln_g, ln_b)


def _conv_bwd(du1, hrest, conv_w, nb, seq):
    ts = _conv_block_rows(seq)
    nblk = seq // ts
    lanes = 128
    last_halo = nb * seq // HALO - 1

    def body(du_ref, duh_ref, cv_ref, cg_ref, cvh_ref, cgh_ref, w_ref, dcv_ref, dcg_ref, dw_ref, ubuf, dbuf):
        b = pl.program_id(0)
        i = pl.program_id(1)

        @pl.when((b == 0) & (i == 0))
        def _():
            dw_ref[...] = jnp.zeros_like(dw_ref)

        cv = cv_ref[...]
        sg = _sigmoid(cg_ref[...])
        halo = cvh_ref[...] * _sigmoid(cgh_ref[...])
        ubuf[0:HALO, :] = jnp.where(i > 0, halo, 0.0)
        ubuf[HALO:HALO + ts, :] = cv * sg
        dbuf[0:ts, :] = du_ref[...]
        dbuf[ts:ts + HALO, :] = jnp.where(i < nblk - 1, duh_ref[...], 0.0)
        for cc in range(D_MODEL // lanes):
            cs = slice(cc * lanes, (cc + 1) * lanes)
            dmain = dbuf[0:ts, cs]
            acc = jnp.zeros((ts, lanes), F32)
            for k in range(CONV_K):
                acc = acc + w_ref[k:k + 1, cs] * dbuf[pl.ds(CONV_K - 1 - k, ts), cs]
                dw_ref[k:k + 1, cs] += jnp.sum(dmain * ubuf[pl.ds(HALO - CONV_K + 1 + k, ts), cs],
                                               axis=0, keepdims=True)
            dcv_ref[:, cs] = (acc * sg[:, cs]).astype(BF16)
            dcg_ref[:, cs] = (acc * cv[:, cs] * sg[:, cs] * (1.0 - sg[:, cs])).astype(BF16)

    def main(colblk):
        return pl.BlockSpec((ts, D_MODEL), lambda b, i: (b * nblk + i, colblk))

    def halo_before(colblk):
        return pl.BlockSpec((HALO, D_MODEL),
                            lambda b, i: (jnp.maximum((b * seq + i * ts) // HALO - 1, 0), colblk))

    halo_after = pl.BlockSpec((HALO, D_MODEL),
                              lambda b, i: (jnp.minimum((b * seq + (i + 1) * ts) // HALO, last_halo), 0))
    t = nb * seq
    return pl.pallas_call(
        body, grid=(nb, nblk),
        in_specs=[main(0), halo_after, main(1), main(2), halo_before(1), halo_before(2),
                  pl.BlockSpec((CONV_ROWS, D_MODEL), lambda b, i: (0, 0))],
        out_specs=[main(0), main(0), pl.BlockSpec((CONV_ROWS, D_MODEL), lambda b, i: (0, 0))],
        out_shape=[jax.ShapeDtypeStruct((t, D_MODEL), BF16), jax.ShapeDtypeStruct((t, D_MODEL), BF16),
                   jax.ShapeDtypeStruct((CONV_ROWS, D_MODEL), F32)],
        scratch_shapes=[pltpu.VMEM((HALO + ts, D_MODEL), F32), pltpu.VMEM((ts + HALO, D_MODEL), F32)],
        compiler_params=_params("arbitrary", "arbitrary"), name="conv_bwd")(
            du1, du1, hrest, hrest, hrest, hrest, conv_w)


def _weight_grad(x, dys, name):
    n = len(dys)
    t = x.shape[0]
    tm = min(512, t)

    def body(x_ref, *refs):
        o_ref = refs[n]
        j = pl.program_id(0)

        @pl.when(pl.program_id(1) == 0)
        def _():
            o_ref[...] = jnp.zeros_like(o_ref)

        for p in range(n):
            @pl.when(j == p)
            def _(p=p):
                o_ref[...] += _dot_tn(x_ref[...], refs[p][...])

    def dy_spec(p):
        return pl.BlockSpec((tm, D_MODEL), lambda j, i: (jnp.where(j == p, i, 0), 0))

    return pl.pallas_call(
        body, grid=(n, t // tm),
        in_specs=[pl.BlockSpec((tm, D_MODEL), lambda j, i: (i, 0))] + [dy_spec(p) for p in range(n)],
        out_specs=pl.BlockSpec((D_MODEL, D_MODEL), lambda j, i: (0, j)),
        out_shape=jax.ShapeDtypeStruct((D_MODEL, n * D_MODEL), F32),
        compiler_params=_params("parallel", "arbitrary"), name=name)(x, *dys)


def _in_proj_bwd(dhs, w_in, dr, x, ln_g):
    n = len(dhs)
    t = x.shape[0]
    tm = min(512, t)

    def body(*refs):
        dh_refs = refs[:n]
        w_ref, dr_ref, x_ref, g_ref, gx_ref, st_ref, acc = refs[n:]
        i = pl.program_id(0)
        p = pl.program_id(1)

        @pl.when((i == 0) & (p == 0))
        def _():
            st_ref[...] = jnp.zeros_like(st_ref)

        @pl.when(p == 0)
        def _():
            acc[...] = ALPHA * dr_ref[...]

        for q in range(n):
            @pl.when(p == q)
            def _(q=q):
                acc[...] += _dot_nt(dh_refs[q][...], w_ref[...])

        @pl.when(p == n - 1)
        def _():
            xhat, rstd = _ln_stats(x_ref[...])
            dh0 = acc[...]
            st_ref[0:1, :] += jnp.sum(dh0 * xhat, axis=0, keepdims=True)
            st_ref[1:2, :] += jnp.sum(dh0, axis=0, keepdims=True)
            gx_ref[...] = _ln_bwd(dh0, xhat, rstd, g_ref[...])

    tok = pl.BlockSpec((tm, D_MODEL), lambda i, p: (i, 0))
    return pl.pallas_call(
        body, grid=(t // tm, n),
        in_specs=[tok] * n + [pl.BlockSpec((D_MODEL, D_MODEL), lambda i, p: (0, p)), tok, tok,
                              pl.BlockSpec((1, D_MODEL), lambda i, p: (0, 0))],
        out_specs=[tok, pl.BlockSpec((8, D_MODEL), lambda i, p: (0, 0))],
        out_shape=[jax.ShapeDtypeStruct((t, D_MODEL), F32), jax.ShapeDtypeStruct((8, D_MODEL), F32)],
        scratch_shapes=[pltpu.VMEM((tm, D_MODEL), F32)],
        compiler_params=_params("arbitrary", "arbitrary"), name="in_proj_bwd")(*dhs, w_in, dr, x, ln_g)


def _local_step(x, tgt, nb, seq, w_in, w_sb, w_cv, w_out, conv_w, vecs):
    h0, h0b = _ln_in(x, vecs["ln_in_g"], vecs["ln_in_b"])
    qkv = _in_proj(h0b, w_in, 0, 3, BF16, True, "in_proj_qkv")
    hrest = _in_proj(h0b, w_in, 3, 6, F32, False, "in_proj_rest")
    o = _attn_fwd(qkv, nb, seq)
    u1, cvin = _conv_fwd(hrest, conv_w, vecs["conv_b"], vecs["conv_ln_g"], vecs["conv_ln_b"], nb, seq)
    (dr, drb, a, merged, dy_sb, dy_cv, do, dz_sb, dg_sb, dg_cv, dcvin, st_tail) = _tail(
        o, hrest, cvin, h0, tgt, w_sb, w_cv, w_out, vecs["ln_post_g"], vecs["ln_post_b"])
    d_w_sb = _weight_grad(a, [dy_sb], "grad_w_sb")
    d_w_cv = _weight_grad(cvin, [dy_cv], "grad_w_cv")
    d_w_out = _weight_grad(merged, [drb], "grad_w_out")
    du1, dz_cv, st_conv = _conv_branch_bwd(dcvin, u1, hrest, vecs["conv_ln_g"], vecs["conv_ln_b"])
    dc_val, dc_gate, d_conv_w = _conv_bwd(du1, hrest, conv_w, nb, seq)
    dq, dk, dv = _attn_bwd(qkv, do, nb, seq)
    dhs = [dq, dk, dv, dz_sb, dc_val, dc_gate, dz_cv, dg_sb, dg_cv]
    d_w_in = _weight_grad(h0b, dhs, "grad_w_in")
    grad_x, st_in = _in_proj_bwd(dhs, w_in, dr, x, vecs["ln_in_g"])
    small = jnp.concatenate([st_in[0:2], st_conv[2:3], st_conv[0:2], st_tail[0:3], d_conv_w], axis=0)
    return grad_x, (d_w_in, d_w_sb, d_w_cv, d_w_out), small


def _place():
    x, y, c = lax.axis_index("x"), lax.axis_index("y"), lax.axis_index("c")
    chips = [(1 - x, y), (x, 1 - y), (1 - x, 1 - y)]
    return x, y, c, chips


def _gather_weights(w_in, w_sb, w_cv, w_out, conv_w):
    n_arr = 5

    def body(i0, i1, i2, i3, i4, o0, o1, o2, o3, o4, send_sems, recv_sems, local_sems):
        x, y, c, chips = _place()
        srcs = [i0, i1, i2, i3, i4]
        dsts = [o0, o1, o2, o3, o4]

        def slot(a, px, py):
            chip = 2 * px + py
            if a == 0:
                return dsts[a].at[:, pl.ds(pl.multiple_of(chip * SHARD_IN, 128), SHARD_IN)]
            if a == 4:
                return dsts[a].at[:, pl.ds(pl.multiple_of(chip * SHARD_SQ, 128), SHARD_SQ)]
            return dsts[a].at[pl.ds(pl.multiple_of(chip * SHARD_SQ, 8), SHARD_SQ), :]

        def remote(a, j):
            px, py = chips[j]
            return pltpu.make_async_remote_copy(
                src_ref=srcs[a], dst_ref=slot(a, x, y), send_sem=send_sems.at[3 * a + j],
                recv_sem=recv_sems.at[3 * a + j], device_id=(px, py, c), device_id_type=MESH)

        def arrival(a, j):
            px, py = chips[j]
            return pltpu.make_async_remote_copy(
                src_ref=srcs[a], dst_ref=slot(a, px, py), send_sem=send_sems.at[3 * a + j],
                recv_sem=recv_sems.at[3 * a + j], device_id=(px, py, c), device_id_type=MESH)

        local = [pltpu.make_async_copy(srcs[a], slot(a, x, y), local_sems.at[a]) for a in range(n_arr)]
        for cp in local:
            cp.start()
        sends = [remote(a, j) for a in range(n_arr) for j in range(3)]
        for cp in sends:
            cp.start()
        for a in range(n_arr):
            for j in range(3):
                arrival(a, j).wait_recv()
        for cp in sends:
            cp.wait_send()
        for cp in local:
            cp.wait()

    return pl.pallas_call(
        body, in_specs=[ANY] * n_arr, out_specs=[ANY] * n_arr,
        out_shape=[jax.ShapeDtypeStruct((D_MODEL, IN_WIDTH), BF16)] + [jax.ShapeDtypeStruct((D_MODEL, D_MODEL), BF16)] * 3
        + [jax.ShapeDtypeStruct((CONV_ROWS, D_MODEL), F32)],
        scratch_shapes=[pltpu.SemaphoreType.DMA((3 * n_arr,)), pltpu.SemaphoreType.DMA((3 * n_arr,)),
                        pltpu.SemaphoreType.DMA((n_arr,))],
        name="gather_weights")(w_in, w_sb, w_cv, w_out, conv_w)


def _row_chunks(src, dst, n, send_sem, recv_sem, device):
    rows = src.shape[0] // n

    def copy(s, d):
        return pltpu.make_async_remote_copy(src_ref=s, dst_ref=d, send_sem=send_sem, recv_sem=recv_sem,
                                            device_id=device, device_id_type=MESH)

    chunks = [copy(src.at[pl.ds(k * rows, rows)], dst.at[pl.ds(k * rows, rows)]) for k in range(n)]
    return chunks, copy(src, dst)


def _arrival(dst, send_sem, recv_sem, device):
    return pltpu.make_async_remote_copy(src_ref=dst, dst_ref=dst, send_sem=send_sem, recv_sem=recv_sem,
                                        device_id=device, device_id_type=MESH)


def _pair_exchange(d_in, d_sb, d_cv, d_out):
    def body(i0, i1, i2, i3, r0, r1, r2, r3, send_sems, recv_sems):
        x, y, c, _ = _place()
        srcs = [i0, i1, i2, i3]
        theirs = [r0, r1, r2, r3]
        sibling = (x, y, 1 - c)
        start = pl.multiple_of((1 - c) * HALF, 128)
        views = [srcs[0].at[pl.ds(start, HALF), :]] + [srcs[a].at[:, pl.ds(start, HALF)] for a in range(1, 4)]
        pieces, wholes = [], []
        for a in range(4):
            cps, whole = _row_chunks(views[a], theirs[a], N_CHUNKS, send_sems.at[a], recv_sems.at[a], sibling)
            pieces += cps
            wholes.append(whole)
        for cp in pieces:
            cp.start()
        for cp in wholes:
            cp.wait_recv()
        for cp in wholes:
            cp.wait_send()

    shapes = [jax.ShapeDtypeStruct((HALF, IN_WIDTH), F32)] + [jax.ShapeDtypeStruct((D_MODEL, HALF), F32)] * 3
    return pl.pallas_call(
        body, in_specs=[ANY] * 4, out_specs=[ANY] * 4, out_shape=shapes,
        scratch_shapes=[pltpu.SemaphoreType.DMA((4,)), pltpu.SemaphoreType.DMA((4,))],
        name="pair_exchange")(d_in, d_sb, d_cv, d_out)


def _chip_exchange(p_in, p_sb, p_cv, p_out):
    def body(i0, i1, i2, i3, r0, r1, r2, r3, send_sems, recv_sems):
        x, y, c, chips = _place()
        srcs = [i0, i1, i2, i3]
        theirs = [r0, r1, r2, r3]

        def shard(a, px, py):
            chip = 2 * px + py
            if a == 0:
                return srcs[a].at[:, pl.ds(pl.multiple_of(chip * SHARD_IN, 128), SHARD_IN)]
            return srcs[a].at[pl.ds(pl.multiple_of(chip * SHARD_SQ, 8), SHARD_SQ), :]

        pieces, wholes = [], []
        for j, (px, py) in enumerate(chips):
            for a in range(4):
                cps, whole = _row_chunks(shard(a, px, py), theirs[a].at[j], 4, send_sems.at[3 * a + j],
                                         recv_sems.at[3 * a + j], (px, py, c))
                pieces += cps
                wholes.append(whole)
        for cp in pieces:
            cp.start()
        for cp in wholes:
            cp.wait_recv()
        for cp in wholes:
            cp.wait_send()

    got = [jax.ShapeDtypeStruct((3, HALF, SHARD_IN), F32)] + [jax.ShapeDtypeStruct((3, SHARD_SQ, HALF), F32)] * 3
    return pl.pallas_call(
        body, in_specs=[ANY] * 4, out_specs=[ANY] * 4, out_shape=got,
        scratch_shapes=[pltpu.SemaphoreType.DMA((12,)), pltpu.SemaphoreType.DMA((12,))],
        name="chip_exchange")(p_in, p_sb, p_cv, p_out)


def _pair_share(f_in, f_sb, f_cv, f_out):
    def body(i0, i1, i2, i3, o0, o1, o2, o3, send_sems, recv_sems):
        del i0, i1, i2, i3
        x, y, c, _ = _place()
        dsts = [o0, o1, o2, o3]
        sibling = (x, y, 1 - c)

        def half(a, which):
            start = pl.multiple_of(which * HALF, 128)
            if a == 0:
                return dsts[a].at[pl.ds(start, HALF), :]
            return dsts[a].at[:, pl.ds(start, HALF)]

        pieces, sent, arrived = [], [], []
        for a in range(4):
            cps, whole = _row_chunks(half(a, c), half(a, c), N_CHUNKS, send_sems.at[a], recv_sems.at[a], sibling)
            pieces += cps
            sent.append(whole)
            arrived.append(_arrival(half(a, 1 - c), send_sems.at[a], recv_sems.at[a], sibling))
        for cp in pieces:
            cp.start()
        for cp in arrived:
            cp.wait_recv()
        for cp in sent:
            cp.wait_send()

    return pl.pallas_call(
        body, in_specs=[ANY] * 4, out_specs=[ANY] * 4,
        out_shape=[jax.ShapeDtypeStruct((D_MODEL, SHARD_IN), F32)] + [jax.ShapeDtypeStruct((SHARD_SQ, D_MODEL), F32)] * 3,
        input_output_aliases={0: 0, 1: 1, 2: 2, 3: 3},
        scratch_shapes=[pltpu.SemaphoreType.DMA((4,)), pltpu.SemaphoreType.DMA((4,))],
        name="pair_share")(f_in, f_sb, f_cv, f_out)


def _small_allreduce(small):
    def body(s_ref, o_ref, slots, send_sems, recv_sems):
        x, y, c, _ = _place()
        me = 4 * x + 2 * y + c
        slots[me] = s_ref[...]
        sends = []
        for k in range(1, N_DEV):
            px, py, pc = (x + (k >> 2)) % 2, (y + ((k >> 1) & 1)) % 2, (c + (k & 1)) % 2
            sends.append(pltpu.make_async_remote_copy(
                src_ref=s_ref, dst_ref=slots.at[me], send_sem=send_sems.at[k - 1], recv_sem=recv_sems.at[k - 1],
                device_id=(px, py, pc), device_id_type=MESH))
        for cp in sends:
            cp.start()
        for k in range(1, N_DEV):
            px, py, pc = (x + (k >> 2)) % 2, (y + ((k >> 1) & 1)) % 2, (c + (k & 1)) % 2
            pltpu.make_async_remote_copy(
                src_ref=s_ref, dst_ref=slots.at[4 * px + 2 * py + pc], send_sem=send_sems.at[k - 1],
                recv_sem=recv_sems.at[k - 1], device_id=(px, py, pc), device_id_type=MESH).wait_recv()
        for cp in sends:
            cp.wait_send()
        total = slots[0]
        for d in range(1, N_DEV):
            total = total + slots[d]
        o_ref[...] = total
        o_ref[7:8, :] = jnp.zeros((1, D_MODEL), F32) + jnp.sum(total[7:8, :], axis=1, keepdims=True)

    return pl.pallas_call(
        body, in_specs=[pl.BlockSpec(memory_space=pltpu.VMEM)], out_specs=pl.BlockSpec(memory_space=pltpu.VMEM),
        out_shape=jax.ShapeDtypeStruct((SMALL_ROWS, D_MODEL), F32),
        scratch_shapes=[pltpu.VMEM((N_DEV, SMALL_ROWS, D_MODEL), F32), pltpu.SemaphoreType.DMA((N_DEV - 1,)),
                        pltpu.SemaphoreType.DMA((N_DEV - 1,))],
        name="small_allreduce")(small)


SUM_BLOCKS = 8


def _pair_sum(place, full, theirs, by_rows, name):
    rows, cols = theirs.shape
    tr = rows // SUM_BLOCKS

    def body(p_ref, a_ref, b_ref, o_ref):
        del p_ref
        o_ref[...] = a_ref[...] + b_ref[...]

    mine = (lambda i, p: (p[1] * SUM_BLOCKS + i, 0)) if by_rows else (lambda i, p: (i, p[1]))
    spec = pl.BlockSpec((tr, cols), lambda i, p: (i, 0))
    return pl.pallas_call(
        body, out_shape=jax.ShapeDtypeStruct(theirs.shape, F32),
        grid_spec=pltpu.PrefetchScalarGridSpec(num_scalar_prefetch=1, grid=(SUM_BLOCKS,),
                                               in_specs=[pl.BlockSpec((tr, cols), mine), spec], out_specs=spec),
        compiler_params=_params("parallel"), name=name)(place, full, theirs)


def _chip_sum(place, pair, got, by_rows, name):
    _, rows, cols = got.shape
    tr = rows // SUM_BLOCKS

    def body(p_ref, a_ref, g_ref, o_ref):
        del p_ref
        o_ref[...] = ((a_ref[...] + g_ref[0]) + g_ref[1]) + g_ref[2]

    if by_rows:
        own = lambda i, p: (i, p[0])
        out = lambda i, p: (p[1] * SUM_BLOCKS + i, 0)
        full = (2 * rows, cols)
    else:
        own = lambda i, p: (p[0] * SUM_BLOCKS + i, 0)
        out = lambda i, p: (i, p[1])
        full = (rows, 2 * cols)
    return pl.pallas_call(
        body, out_shape=jax.ShapeDtypeStruct(full, F32),
        grid_spec=pltpu.PrefetchScalarGridSpec(
            num_scalar_prefetch=1, grid=(SUM_BLOCKS,),
            in_specs=[pl.BlockSpec((tr, cols), own), pl.BlockSpec((3, tr, cols), lambda i, p: (0, i, 0))],
            out_specs=pl.BlockSpec((tr, cols), out)),
        compiler_params=_params("parallel"), name=name)(place, pair, got)


def _adamw(w, g, m, v, name):
    rows, cols = w.shape
    tr = rows // 4 if rows % 32 == 0 else rows
    c1 = 1.0 - ADAM_B1 ** ADAM_STEP
    c2 = 1.0 - ADAM_B2 ** ADAM_STEP

    def body(w_ref, g_ref, m_ref, v_ref, d_ref, mo_ref, vo_ref):
        g = g_ref[...]
        mn = ADAM_B1 * m_ref[...] + (1.0 - ADAM_B1) * g
        vn = ADAM_B2 * v_ref[...] + (1.0 - ADAM_B2) * (g * g)
        mo_ref[...] = mn
        vo_ref[...] = vn
        d_ref[...] = -ADAM_LR * ((mn / c1) / (jnp.sqrt(vn / c2) + ADAM_EPS) + ADAM_WD * w_ref[...])

    spec = pl.BlockSpec((tr, cols), lambda i: (i, 0))
    shape = jax.ShapeDtypeStruct(w.shape, F32)
    return pl.pallas_call(body, grid=(rows // tr,), in_specs=[spec] * 4, out_specs=[spec] * 3,
                          out_shape=[shape] * 3, compiler_params=_params("parallel"), name=name)(w, g, m, v)


def _reduce_weight_grads(grads):
    names = ("in", "sb", "cv", "out")
    place = jnp.stack([2 * lax.axis_index("x") + lax.axis_index("y"), lax.axis_index("c")]).astype(jnp.int32)
    theirs = _pair_exchange(*grads)
    pair = [_pair_sum(place, grads[a], theirs[a], a == 0, "pair_sum_" + names[a]) for a in range(4)]
    got = _chip_exchange(*pair)
    done = [_chip_sum(place, pair[a], got[a], a == 0, "chip_sum_" + names[a]) for a in range(4)]
    return _pair_share(*done)


def kernel(x, ln_in_g, ln_in_b, w_in, w_sb_proj, conv_w, conv_b, conv_ln_g, conv_ln_b, w_cv_proj, w_out, ln_post_g, ln_post_b, loss_target, m_ln_in_g, m_ln_in_b, m_w_in, m_w_sb_proj, m_conv_w, m_conv_b, m_conv_ln_g, m_conv_ln_b, m_w_cv_proj, m_w_out, m_ln_post_g, m_ln_post_b, v_ln_in_g, v_ln_in_b, v_w_in, v_w_sb_proj, v_conv_w, v_conv_b, v_conv_ln_g, v_conv_ln_b, v_w_cv_proj, v_w_out, v_ln_post_g, v_ln_post_b):
    nb, seq, _ = x.shape
    t = nb * seq
    vec_names = ("ln_in_g", "ln_in_b", "conv_b", "conv_ln_g", "conv_ln_b", "ln_post_g", "ln_post_b")
    vec_w = dict(zip(vec_names, (ln_in_g, ln_in_b, conv_b, conv_ln_g, conv_ln_b, ln_post_g, ln_post_b)))
    vec_m = dict(zip(vec_names, (m_ln_in_g, m_ln_in_b, m_conv_b, m_conv_ln_g, m_conv_ln_b, m_ln_post_g, m_ln_post_b)))
    vec_v = dict(zip(vec_names, (v_ln_in_g, v_ln_in_b, v_conv_b, v_conv_ln_g, v_conv_ln_b, v_ln_post_g, v_ln_post_b)))
    vecs = {k: a.reshape(1, D_MODEL) for k, a in vec_w.items()}

    pad_taps = lambda a: jnp.pad(a.reshape(CONV_K, SHARD_SQ), ((0, CONV_ROWS - CONV_K), (0, 0)))
    gathered = _gather_weights(w_in[0].astype(BF16), w_sb_proj[0].astype(BF16), w_cv_proj[0].astype(BF16),
                               w_out[0].astype(BF16), pad_taps(conv_w))
    full_in, full_sb, full_cv, full_out, full_conv = gathered

    grad_x, big, small = _local_step(x.reshape(t, D_MODEL), loss_target.reshape(t, D_MODEL), nb, seq,
                                     full_in, full_sb, full_cv, full_out, full_conv, vecs)
    g_in, g_sb, g_cv, g_out = _reduce_weight_grads(big)
    small = _small_allreduce(small)

    chip = 2 * lax.axis_index("x") + lax.axis_index("y")
    g_conv = lax.dynamic_slice(small, (8, chip * SHARD_SQ), (CONV_ROWS, SHARD_SQ))
    loss = small[7, 0]

    grads, deltas, new_m, new_v = {}, {}, {}, {}
    stack = lambda d: jnp.concatenate([d[k].reshape(1, D_MODEL) for k in vec_names] + [jnp.zeros((1, D_MODEL), F32)])
    vd, vm, vv = _adamw(stack(vec_w), small[0:8], stack(vec_m), stack(vec_v), "adamw_vectors")
    for r, k in enumerate(vec_names):
        shape = vec_w[k].shape
        grads[k] = small[r].reshape(shape)
        deltas[k], new_m[k], new_v[k] = vd[r].reshape(shape), vm[r].reshape(shape), vv[r].reshape(shape)

    big_w = {"w_in": (w_in, g_in, m_w_in, v_w_in), "w_sb_proj": (w_sb_proj, g_sb, m_w_sb_proj, v_w_sb_proj),
             "w_cv_proj": (w_cv_proj, g_cv, m_w_cv_proj, v_w_cv_proj), "w_out": (w_out, g_out, m_w_out, v_w_out)}
    for k, (w, g, m, v) in big_w.items():
        d, mn, vn = _adamw(w[0], g, m[0], v[0], "adamw_" + k)
        grads[k], deltas[k], new_m[k], new_v[k] = g[None], d[None], mn[None], vn[None]
    d, mn, vn = _adamw(pad_taps(conv_w), g_conv, pad_taps(m_conv_w), pad_taps(v_conv_w), "adamw_conv_w")
    grads["conv_w"] = g_conv[None, :CONV_K]
    deltas["conv_w"], new_m["conv_w"], new_v["conv_w"] = d[None, :CONV_K], mn[None, :CONV_K], vn[None, :CONV_K]

    order = ("ln_in_g", "ln_in_b", "w_in", "w_sb_proj", "conv_w", "conv_b", "conv_ln_g", "conv_ln_b",
             "w_cv_proj", "w_out", "ln_post_g", "ln_post_b")
    return (loss, grad_x.reshape(x.shape), *[grads[k] for k in order], *[deltas[k] for k in order],
            *[new_m[k] for k in order], *[new_v[k] for k in order])
```

```python
import functools

import jax
import jax.numpy as jnp
from jax import lax
from jax.experimental import pallas as pl
from jax.experimental.pallas import tpu as pltpu

F32 = jnp.float32
BF16 = jnp.bfloat16

D_MODEL = 1024
N_HEADS = 16
HEAD_DIM = 64
HEAD_PAIRS = N_HEADS // 2
N_PIECES = 9
IN_WIDTH = N_PIECES * D_MODEL
Q_BLOCK = 256
CONV_K = 31
CONV_ROWS = 32
HALO = 32
LN_EPS = 1e-5
ALPHA = 2.0 ** 0.25
Q_SCALE = 0.125
N_CHIPS = 4
N_DEV = 8
SHARD_IN = IN_WIDTH // N_CHIPS
SHARD_SQ = D_MODEL // N_CHIPS
HALF = D_MODEL // 2
SMALL_ROWS = 40
N_CHUNKS = 16

ADAM_LR = 0.001
ADAM_B1 = 0.9
ADAM_B2 = 0.999
ADAM_EPS = 1e-08
ADAM_WD = 0.01
ADAM_STEP = 10

MESH = pl.DeviceIdType.MESH
ANY = pl.BlockSpec(memory_space=pl.ANY)
VMEM_LIMIT = 60 * 1024 * 1024

NT = (((1,), (1,)), ((), ()))
TN = (((0,), (0,)), ((), ()))


def _sigmoid(x):
    return 1.0 / (1.0 + jnp.exp(-x))


def _dot(a, b):
    return jnp.dot(a, b, preferred_element_type=F32)


def _dot_nt(a, b):
    return lax.dot_general(a, b, NT, preferred_element_type=F32)


def _dot_tn(a, b):
    return lax.dot_general(a, b, TN, preferred_element_type=F32)


def _split_bf16(x):
    hi = x.astype(BF16)
    lo = (x - hi.astype(F32)).astype(BF16)
    return hi, lo


def _ln_stats(x):
    mu = jnp.mean(x, axis=-1, keepdims=True)
    xc = x - mu
    var = jnp.mean(xc * xc, axis=-1, keepdims=True)
    rstd = lax.rsqrt(var + LN_EPS)
    return xc * rstd, rstd


def _ln_bwd(dy, xhat, rstd, g):
    dxh = dy * g
    m1 = jnp.mean(dxh, axis=-1, keepdims=True)
    m2 = jnp.mean(dxh * xhat, axis=-1, keepdims=True)
    return rstd * (dxh - m1 - xhat * m2)


def _params(*sem):
    return pltpu.CompilerParams(dimension_semantics=sem, vmem_limit_bytes=VMEM_LIMIT)


def _ln_in(x, g, b):
    t = x.shape[0]
    tm = min(512, t)

    def body(x_ref, g_ref, b_ref, hf_ref, hb_ref):
        xhat, _ = _ln_stats(x_ref[...])
        y = xhat * g_ref[...] + b_ref[...]
        hf_ref[...] = y
        hb_ref[...] = y.astype(BF16)

    row = pl.BlockSpec((tm, D_MODEL), lambda i: (i, 0))
    vec = pl.BlockSpec((1, D_MODEL), lambda i: (0, 0))
    return pl.pallas_call(
        body, grid=(t // tm,), in_specs=[row, vec, vec], out_specs=[row, row],
        out_shape=[jax.ShapeDtypeStruct((t, D_MODEL), F32), jax.ShapeDtypeStruct((t, D_MODEL), BF16)],
        compiler_params=_params("parallel"), name="ln_in")(x, g, b)


def _in_proj(h, w, col0, ncol, out_dtype, scale_first, name):
    t = h.shape[0]
    tm = min(512, t)

    def body(h_ref, w_ref, o_ref):
        res = _dot(h_ref[...], w_ref[...])
        if scale_first:
            res = res * jnp.where(pl.program_id(0) == 0, Q_SCALE, 1.0)
        o_ref[...] = res.astype(out_dtype)

    return pl.pallas_call(
        body, grid=(ncol, t // tm),
        in_specs=[pl.BlockSpec((tm, D_MODEL), lambda j, i: (i, 0)),
                  pl.BlockSpec((D_MODEL, D_MODEL), lambda j, i: (0, j + col0))],
        out_specs=pl.BlockSpec((tm, D_MODEL), lambda j, i: (i, j)),
        out_shape=jax.ShapeDtypeStruct((t, ncol * D_MODEL), out_dtype),
        compiler_params=_params("parallel", "parallel"), name=name)(h, w)


def _sb_block(l, diag, tri, carry):
    lnb = jnp.minimum(-l, 0.0) - jnp.log(1.0 + jnp.exp(-jnp.abs(l)))
    if diag is not None:
        lnb = jnp.where(diag, lnb, 0.0)
    hi, lo = _split_bf16(lnb)
    suffix = _dot(hi, tri) + _dot(lo, tri)
    a = jnp.exp(l + suffix + carry)
    if diag is not None:
        a = jnp.where(diag, a, 0.0)
    return a, lnb, suffix[:, 0:1]


def _attn_consts():
    lane = lax.broadcasted_iota(jnp.int32, (Q_BLOCK, 2 * HEAD_DIM), 1)
    row = lax.broadcasted_iota(jnp.int32, (Q_BLOCK, Q_BLOCK), 0)
    col = lax.broadcasted_iota(jnp.int32, (Q_BLOCK, Q_BLOCK), 1)
    return lane, col < row, (row >= col).astype(BF16), (row <= col).astype(BF16)


def _split_heads(t, lane):
    zero = jnp.zeros_like(t)
    return jnp.where(lane < HEAD_DIM, t, zero), jnp.where(lane >= HEAD_DIM, t, zero)


def _attn_fwd(qkv, nb, seq):
    nq = seq // Q_BLOCK

    def body(q_ref, k_ref, v_ref, o_ref):
        qi = pl.program_id(2)
        lane, diag, tri, _ = _attn_consts()
        q = q_ref[...]
        qh = _split_heads(q, lane)

        def block(ks, state, masked):
            kblk = k_ref[pl.ds(ks, Q_BLOCK), :]
            vblk = v_ref[pl.ds(ks, Q_BLOCK), :]
            out = []
            for hd in range(2):
                carry, acc = state[hd]
                a, _, blk_sum = _sb_block(_dot_nt(qh[hd], kblk), diag if masked else None, tri, carry)
                out.append((carry + blk_sum, acc + _dot(a.astype(BF16), vblk)))
            return tuple(out)

        def step(jj, state):
            return block(pl.multiple_of((qi - jj) * Q_BLOCK, Q_BLOCK), state, False)

        init = (jnp.zeros((Q_BLOCK, 1), F32), jnp.zeros((Q_BLOCK, 2 * HEAD_DIM), F32))
        state = block(pl.multiple_of(qi * Q_BLOCK, Q_BLOCK), (init, init), True)
        state = lax.fori_loop(1, qi + 1, step, state)
        o_ref[...] = jnp.where(lane < HEAD_DIM, state[0][1], state[1][1])

    w = 2 * HEAD_DIM
    return pl.pallas_call(
        body, grid=(nb, HEAD_PAIRS, nq),
        in_specs=[pl.BlockSpec((Q_BLOCK, w), lambda b, h, i: (b * nq + i, h)),
                  pl.BlockSpec((seq, w), lambda b, h, i: (b, HEAD_PAIRS + h)),
                  pl.BlockSpec((seq, w), lambda b, h, i: (b, 2 * HEAD_PAIRS + h))],
        out_specs=pl.BlockSpec((Q_BLOCK, w), lambda b, h, i: (b * nq + i, h)),
        out_shape=jax.ShapeDtypeStruct((nb * seq, D_MODEL), F32),
        compiler_params=_params("parallel", "parallel", "arbitrary"), name="attn_fwd")(qkv, qkv, qkv)


def _attn_bwd(qkv, do, nb, seq):
    nq = seq // Q_BLOCK

    def body(q_ref, k_ref, v_ref, do_ref, dq_ref, dk_ref, dv_ref, g_s, nb_s, dk_acc, dv_acc):
        qi = pl.program_id(2)

        @pl.when(qi == 0)
        def _():
            dk_acc[...] = jnp.zeros_like(dk_acc)
            dv_acc[...] = jnp.zeros_like(dv_acc)

        lane, diag, tri_suffix, tri_prefix = _attn_consts()
        qh = _split_heads(q_ref[...], lane)
        doh = _split_heads(do_ref[...], lane)

        def block_a(kb, carries, masked):
            ks = pl.multiple_of(kb * Q_BLOCK, Q_BLOCK)
            kblk = k_ref[pl.ds(ks, Q_BLOCK), :]
            vblk = v_ref[pl.ds(ks, Q_BLOCK), :]
            out = []
            dv = None
            for hd in range(2):
                a, lnb, blk_sum = _sb_block(_dot_nt(qh[hd], kblk), diag if masked else None, tri_suffix, carries[hd])
                g_s[hd, kb] = a * _dot_nt(doh[hd], vblk)
                nb_s[hd, kb] = jnp.exp(lnb)
                part = _dot_tn(a.astype(BF16), doh[hd])
                dv = part if dv is None else dv + part
                out.append(carries[hd] + blk_sum)
            dv_acc[pl.ds(ks, Q_BLOCK), :] += dv
            return tuple(out)

        zero_col = jnp.zeros((Q_BLOCK, 1), F32)
        carries = block_a(qi, (zero_col, zero_col), True)
        lax.fori_loop(1, qi + 1, lambda jj, cs: block_a(qi - jj, cs, False), carries)

        def block_b(kb, state, masked):
            ks = pl.multiple_of(kb * Q_BLOCK, Q_BLOCK)
            kblk = k_ref[pl.ds(ks, Q_BLOCK), :]
            out = []
            dk = None
            for hd in range(2):
                prefix, dq = state[hd]
                g = g_s[hd, kb]
                not_beta = nb_s[hd, kb]
                hi, lo = _split_bf16(g)
                incl = _dot(hi, tri_prefix) + _dot(lo, tri_prefix)
                before = incl - g + prefix
                dl = not_beta * (g + before) - before
                if masked:
                    dl = jnp.where(diag, dl, 0.0)
                dl = dl.astype(BF16)
                part = _dot_tn(dl, qh[hd])
                dk = part if dk is None else dk + part
                out.append((prefix + incl[:, Q_BLOCK - 1:Q_BLOCK], dq + _dot(dl, kblk)))
            dk_acc[pl.ds(ks, Q_BLOCK), :] += dk
            return tuple(out)

        init = (zero_col, jnp.zeros((Q_BLOCK, 2 * HEAD_DIM), F32))
        state = lax.fori_loop(0, qi, lambda kb, st: block_b(kb, st, False), (init, init))
        state = block_b(qi, state, True)
        dq_ref[...] = (jnp.where(lane < HEAD_DIM, state[0][1], state[1][1]) * Q_SCALE).astype(BF16)

        @pl.when(qi == nq - 1)
        def _():
            dk_ref[...] = dk_acc[...].astype(BF16)
            dv_ref[...] = dv_acc[...].astype(BF16)

    w = 2 * HEAD_DIM
    t = nb * seq
    qspec = pl.BlockSpec((Q_BLOCK, w), lambda b, h, i: (b * nq + i, h))
    kvout = pl.BlockSpec((seq, w), lambda b, h, i: (b, h))
    return pl.pallas_call(
        body, grid=(nb, HEAD_PAIRS, nq),
        in_specs=[qspec,
                  pl.BlockSpec((seq, w), lambda b, h, i: (b, HEAD_PAIRS + h)),
                  pl.BlockSpec((seq, w), lambda b, h, i: (b, 2 * HEAD_PAIRS + h)),
                  qspec],
        out_specs=[qspec, kvout, kvout],
        out_shape=[jax.ShapeDtypeStruct((t, D_MODEL), BF16)] * 3,
        scratch_shapes=[pltpu.VMEM((2, nq, Q_BLOCK, Q_BLOCK), F32), pltpu.VMEM((2, nq, Q_BLOCK, Q_BLOCK), F32),
                        pltpu.VMEM((seq, w), F32), pltpu.VMEM((seq, w), F32)],
        compiler_params=_params("parallel", "parallel", "arbitrary"), name="attn_bwd")(qkv, qkv, qkv, do)


def _conv_block_rows(seq):
    return min(256, seq)


def _conv_fwd(hrest, conv_w, conv_b, ln_g, ln_b, nb, seq):
    ts = _conv_block_rows(seq)
    nblk = seq // ts
    lanes = 128

    def body(cv_ref, cg_ref, cvh_ref, cgh_ref, z_ref, w_ref, cb_ref, g_ref, b_ref, u1_ref, cvin_ref, ubuf):
        i = pl.program_id(1)
        halo = cvh_ref[...] * _sigmoid(cgh_ref[...])
        ubuf[0:HALO, :] = jnp.where(i > 0, halo, 0.0)
        ubuf[HALO:HALO + ts, :] = cv_ref[...] * _sigmoid(cg_ref[...])
        for cc in range(D_MODEL // lanes):
            cs = slice(cc * lanes, (cc + 1) * lanes)
            acc = jnp.zeros((ts, lanes), F32) + cb_ref[:, cs]
            for k in range(CONV_K):
                acc = acc + w_ref[k:k + 1, cs] * ubuf[pl.ds(HALO - CONV_K + 1 + k, ts), cs]
            u1_ref[:, cs] = acc
        xhat, _ = _ln_stats(u1_ref[...])
        u2 = xhat * g_ref[...] + b_ref[...]
        z = z_ref[...]
        cvin_ref[...] = (u2 * _sigmoid(u2) * z * _sigmoid(z)).astype(BF16)

    def main(colblk):
        return pl.BlockSpec((ts, D_MODEL), lambda b, i: (b * nblk + i, colblk))

    def halo(colblk):
        return pl.BlockSpec((HALO, D_MODEL),
                            lambda b, i: (jnp.maximum((b * seq + i * ts) // HALO - 1, 0), colblk))

    vec = pl.BlockSpec((1, D_MODEL), lambda b, i: (0, 0))
    t = nb * seq
    return pl.pallas_call(
        body, grid=(nb, nblk),
        in_specs=[main(1), main(2), halo(1), halo(2), main(3),
                  pl.BlockSpec((CONV_ROWS, D_MODEL), lambda b, i: (0, 0)), vec, vec, vec],
        out_specs=[main(0), main(0)],
        out_shape=[jax.ShapeDtypeStruct((t, D_MODEL), F32), jax.ShapeDtypeStruct((t, D_MODEL), BF16)],
        scratch_shapes=[pltpu.VMEM((HALO + ts, D_MODEL), F32)],
        compiler_params=_params("parallel", "parallel"), name="conv_fwd")(
            hrest, hrest, hrest, hrest, hrest, conv_w, conv_b, ln_g, ln_b)


def _tail(o, hrest, cvin, h0, tgt, w_sb, w_cv, w_out, ln_g, ln_b):
    t = o.shape[0]
    tm = min(256, t)

    def body(o_ref, z_ref, gs_ref, gc_ref, cvin_ref, h0_ref, tgt_ref, wsb_ref, wcv_ref, wout_ref, g_ref, b_ref,
             dr_ref, drb_ref, a_ref, mg_ref, dysb_ref, dycv_ref, do_ref, dz_ref, dgs_ref, dgc_ref, dcvin_ref,
             st_ref):
        @pl.when(pl.program_id(0) == 0)
        def _():
            st_ref[...] = jnp.zeros_like(st_ref)

        o = o_ref[...]
        z = z_ref[...]
        sz = _sigmoid(z)
        a = (o * z * sz).astype(BF16)
        a_ref[...] = a
        y_sb = _dot(a, wsb_ref[...])
        y_cv = _dot(cvin_ref[...], wcv_ref[...])
        s_sb = _sigmoid(gs_ref[...])
        s_cv = _sigmoid(gc_ref[...])
        merged = (s_sb * y_sb + s_cv * y_cv).astype(BF16)
        mg_ref[...] = merged
        r = ALPHA * h0_ref[...] + _dot(merged, wout_ref[...])
        xhat, rstd = _ln_stats(r)
        g = g_ref[...]
        err = xhat * g + b_ref[...] - tgt_ref[...]
        dy = err * (1.0 / D_MODEL)
        st_ref[0:1, :] += jnp.sum(dy * xhat, axis=0, keepdims=True)
        st_ref[1:2, :] += jnp.sum(dy, axis=0, keepdims=True)
        st_ref[2:3, :] += (0.5 / D_MODEL) * jnp.sum(err * err, axis=0, keepdims=True)
        dr = _ln_bwd(dy, xhat, rstd, g)
        dr_ref[...] = dr
        drb = dr.astype(BF16)
        drb_ref[...] = drb
        dm = _dot_nt(drb, wout_ref[...])
        dy_sb = (dm * s_sb).astype(BF16)
        dy_cv = (dm * s_cv).astype(BF16)
        dysb_ref[...] = dy_sb
        dycv_ref[...] = dy_cv
        dgs_ref[...] = (dm * y_sb * s_sb * (1.0 - s_sb)).astype(BF16)
        dgc_ref[...] = (dm * y_cv * s_cv * (1.0 - s_cv)).astype(BF16)
        da = _dot_nt(dy_sb, wsb_ref[...])
        dcvin_ref[...] = _dot_nt(dy_cv, wcv_ref[...])
        do_ref[...] = (da * z * sz).astype(BF16)
        dz_ref[...] = (da * o * sz * (1.0 + z * (1.0 - sz))).astype(BF16)

    def tok(colblk=0):
        return pl.BlockSpec((tm, D_MODEL), lambda i: (i, colblk))

    wspec = pl.BlockSpec((D_MODEL, D_MODEL), lambda i: (0, 0), pipeline_mode=pl.Buffered(1))
    vec = pl.BlockSpec((1, D_MODEL), lambda i: (0, 0))
    bf = jax.ShapeDtypeStruct((t, D_MODEL), BF16)
    f32 = jax.ShapeDtypeStruct((t, D_MODEL), F32)
    return pl.pallas_call(
        body, grid=(t // tm,),
        in_specs=[tok(), tok(0), tok(4), tok(5), tok(), tok(), tok(), wspec, wspec, wspec, vec, vec],
        out_specs=[tok()] * 11 + [pl.BlockSpec((8, D_MODEL), lambda i: (0, 0))],
        out_shape=[f32] + [bf] * 9 + [f32, jax.ShapeDtypeStruct((8, D_MODEL), F32)],
        compiler_params=_params("arbitrary"), name="tail")(
            o, hrest, hrest, hrest, cvin, h0, tgt, w_sb, w_cv, w_out, ln_g, ln_b)


def _conv_branch_bwd(dcvin, u1, hrest, ln_g, ln_b):
    t = u1.shape[0]
    tm = min(512, t)

    def body(dc_ref, u1_ref, z_ref, g_ref, b_ref, du1_ref, dz_ref, st_ref):
        @pl.when(pl.program_id(0) == 0)
        def _():
            st_ref[...] = jnp.zeros_like(st_ref)

        xhat, rstd = _ln_stats(u1_ref[...])
        g = g_ref[...]
        u2 = xhat * g + b_ref[...]
        s2 = _sigmoid(u2)
        z = z_ref[...]
        sz = _sigmoid(z)
        dc = dc_ref[...]
        dz_ref[...] = (dc * (u2 * s2) * sz * (1.0 + z * (1.0 - sz))).astype(BF16)
        du2 = dc * (z * sz) * s2 * (1.0 + u2 * (1.0 - s2))
        st_ref[0:1, :] += jnp.sum(du2 * xhat, axis=0, keepdims=True)
        st_ref[1:2, :] += jnp.sum(du2, axis=0, keepdims=True)
        du1 = _ln_bwd(du2, xhat, rstd, g)
        du1_ref[...] = du1
        st_ref[2:3, :] += jnp.sum(du1, axis=0, keepdims=True)

    tok = pl.BlockSpec((tm, D_MODEL), lambda i: (i, 0))
    vec = pl.BlockSpec((1, D_MODEL), lambda i: (0, 0))
    return pl.pallas_call(
        body, grid=(t // tm,),
        in_specs=[tok, tok, pl.BlockSpec((tm, D_MODEL), lambda i: (i, 3)), vec, vec],
        out_specs=[tok, tok, pl.BlockSpec((8, D_MODEL), lambda i: (0, 0))],
        out_shape=[jax.ShapeDtypeStruct((t, D_MODEL), F32), jax.ShapeDtypeStruct((t, D_MODEL), BF16),
                   jax.ShapeDtypeStruct((8, D_MODEL), F32)],
        compiler_params=_params("arbitrary"), name="conv_branch_bwd")(dcvin, u1, hrest, ln_g, ln_b)


def _conv_bwd(du1, hrest, conv_w, nb, seq):
    ts = _conv_block_rows(seq)
    nblk = seq // ts
    lanes = 128
    last_halo = nb * seq // HALO - 1

    def body(du_ref, duh_ref, cv_ref, cg_ref, cvh_ref, cgh_ref, w_ref, dcv_ref, dcg_ref, dw_ref, ubuf, dbuf):
        b = pl.program_id(0)
        i = pl.program_id(1)

        @pl.when((b == 0) & (i == 0))
        def _():
            dw_ref[...] = jnp.zeros_like(dw_ref)

        cv = cv_ref[...]
        sg = _sigmoid(cg_ref[...])
        halo = cvh_ref[...] * _sigmoid(cgh_ref[...])
        ubuf[0:HALO, :] = jnp.where(i > 0, halo, 0.0)
        ubuf[HALO:HALO + ts, :] = cv * sg
        dbuf[0:ts, :] = du_ref[...]
        dbuf[ts:ts + HALO, :] = jnp.where(i < nblk - 1, duh_ref[...], 0.0)
        for cc in range(D_MODEL // lanes):
            cs = slice(cc * lanes, (cc + 1) * lanes)
            dmain = dbuf[0:ts, cs]
            acc = jnp.zeros((ts, lanes), F32)
            for k in range(CONV_K):
                acc = acc + w_ref[k:k + 1, cs] * dbuf[pl.ds(CONV_K - 1 - k, ts), cs]
                dw_ref[k:k + 1, cs] += jnp.sum(dmain * ubuf[pl.ds(HALO - CONV_K + 1 + k, ts), cs],
                                               axis=0, keepdims=True)
            dcv_ref[:, cs] = (acc * sg[:, cs]).astype(BF16)
            dcg_ref[:, cs] = (acc * cv[:, cs] * sg[:, cs] * (1.0 - sg[:, cs])).astype(BF16)

    def main(colblk):
        return pl.BlockSpec((ts, D_MODEL), lambda b, i: (b * nblk + i, colblk))

    def halo_before(colblk):
        return pl.BlockSpec((HALO, D_MODEL),
                            lambda b, i: (jnp.maximum((b * seq + i * ts) // HALO - 1, 0), colblk))

    halo_after = pl.BlockSpec((HALO, D_MODEL),
                              lambda b, i: (jnp.minimum((b * seq + (i + 1) * ts) // HALO, last_halo), 0))
    t = nb * seq
    return pl.pallas_call(
        body, grid=(nb, nblk),
        in_specs=[main(0), halo_after, main(1), main(2), halo_before(1), halo_before(2),
                  pl.BlockSpec((CONV_ROWS, D_MODEL), lambda b, i: (0, 0))],
        out_specs=[main(0), main(0), pl.BlockSpec((CONV_ROWS, D_MODEL), lambda b, i: (0, 0))],
        out_shape=[jax.ShapeDtypeStruct((t, D_MODEL), BF16), jax.ShapeDtypeStruct((t, D_MODEL), BF16),
                   jax.ShapeDtypeStruct((CONV_ROWS, D_MODEL), F32)],
        scratch_shapes=[pltpu.VMEM((HALO + ts, D_MODEL), F32), pltpu.VMEM((ts + HALO, D_MODEL), F32)],
        compiler_params=_params("arbitrary", "arbitrary"), name="conv_bwd")(
            du1, du1, hrest, hrest, hrest, hrest, conv_w)


def _weight_grad(x, dys, name):
    n = len(dys)
    t = x.shape[0]
    tm = min(512, t)

    def body(x_ref, *refs):
        o_ref = refs[n]
        j = pl.program_id(0)

        @pl.when(pl.program_id(1) == 0)
        def _():
            o_ref[...] = jnp.zeros_like(o_ref)

        for p in range(n):
            @pl.when(j == p)
            def _(p=p):
                o_ref[...] += _dot_tn(x_ref[...], refs[p][...])

    def dy_spec(p):
        return pl.BlockSpec((tm, D_MODEL), lambda j, i: (jnp.where(j == p, i, 0), 0))

    return pl.pallas_call(
        body, grid=(n, t // tm),
        in_specs=[pl.BlockSpec((tm, D_MODEL), lambda j, i: (i, 0))] + [dy_spec(p) for p in range(n)],
        out_specs=pl.BlockSpec((D_MODEL, D_MODEL), lambda j, i: (0, j)),
        out_shape=jax.ShapeDtypeStruct((D_MODEL, n * D_MODEL), F32),
        compiler_params=_params("parallel", "arbitrary"), name=name)(x, *dys)


def _in_proj_bwd(dhs, w_in, dr, x, ln_g, pair):
    n = len(dhs)
    t = x.shape[0]
    tm = min(512, t)
    n_i = t // tm

    def body(*refs):
        dh_refs = refs[:n]
        w_ref, dr_ref, x_ref, g_ref = refs[n:n + 4]
        pair_refs = refs[n + 4:n + 8]
        gx_ref, st_ref = refs[n + 8:n + 10]
        got_refs = refs[n + 10:n + 14]
        acc, send_sems, recv_sems = refs[n + 14:]
        i = pl.program_id(0)
        p = pl.program_id(1)
        pieces, wholes = _chip_exchange_copies(pair_refs, got_refs, send_sems, recv_sems)

        @pl.when((i == 0) & (p == 0))
        def _():
            st_ref[...] = jnp.zeros_like(st_ref)
            for cp in pieces:
                cp.start()

        @pl.when((i == n_i - 1) & (p == n - 1))
        def _():
            for cp in wholes:
                cp.wait_recv()
            for cp in wholes:
                cp.wait_send()

        @pl.when(p == 0)
        def _():
            acc[...] = ALPHA * dr_ref[...]

        for q in range(n):
            @pl.when(p == q)
            def _(q=q):
                acc[...] += _dot_nt(dh_refs[q][...], w_ref[...])

        @pl.when(p == n - 1)
        def _():
            xhat, rstd = _ln_stats(x_ref[...])
            dh0 = acc[...]
            st_ref[0:1, :] += jnp.sum(dh0 * xhat, axis=0, keepdims=True)
            st_ref[1:2, :] += jnp.sum(dh0, axis=0, keepdims=True)
            gx_ref[...] = _ln_bwd(dh0, xhat, rstd, g_ref[...])

    tok = pl.BlockSpec((tm, D_MODEL), lambda i, p: (i, 0))
    got = [jax.ShapeDtypeStruct((3, HALF, SHARD_IN), BF16)] + [jax.ShapeDtypeStruct((3, SHARD_SQ, HALF), BF16)] * 3
    res = pl.pallas_call(
        body, grid=(n_i, n),
        in_specs=[tok] * n + [pl.BlockSpec((D_MODEL, D_MODEL), lambda i, p: (0, p)), tok, tok,
                              pl.BlockSpec((1, D_MODEL), lambda i, p: (0, 0))] + [ANY] * 4,
        out_specs=[tok, pl.BlockSpec((8, D_MODEL), lambda i, p: (0, 0))] + [ANY] * 4,
        out_shape=[jax.ShapeDtypeStruct((t, D_MODEL), F32), jax.ShapeDtypeStruct((8, D_MODEL), F32)] + got,
        scratch_shapes=[pltpu.VMEM((tm, D_MODEL), F32), pltpu.SemaphoreType.DMA((12,)), pltpu.SemaphoreType.DMA((12,))],
        compiler_params=_params("arbitrary", "arbitrary"), name="in_proj_bwd")(*dhs, w_in, dr, x, ln_g, *pair)
    return res[0], res[1], res[2:]


def _forward_backward(x, tgt, nb, seq, w_in, w_sb, w_cv, w_out, conv_w, vecs):
    h0, h0b = _ln_in(x, vecs["ln_in_g"], vecs["ln_in_b"])
    qkv = _in_proj(h0b, w_in, 0, 3, BF16, True, "in_proj_qkv")
    hrest = _in_proj(h0b, w_in, 3, 6, F32, False, "in_proj_rest")
    o = _attn_fwd(qkv, nb, seq)
    u1, cvin = _conv_fwd(hrest, conv_w, vecs["conv_b"], vecs["conv_ln_g"], vecs["conv_ln_b"], nb, seq)
    (dr, drb, a, merged, dy_sb, dy_cv, do, dz_sb, dg_sb, dg_cv, dcvin, st_tail) = _tail(
        o, hrest, cvin, h0, tgt, w_sb, w_cv, w_out, vecs["ln_post_g"], vecs["ln_post_b"])
    d_w_sb = _weight_grad(a, [dy_sb], "grad_w_sb")
    d_w_cv = _weight_grad(cvin, [dy_cv], "grad_w_cv")
    d_w_out = _weight_grad(merged, [drb], "grad_w_out")
    du1, dz_cv, st_conv = _conv_branch_bwd(dcvin, u1, hrest, vecs["conv_ln_g"], vecs["conv_ln_b"])
    dc_val, dc_gate, d_conv_w = _conv_bwd(du1, hrest, conv_w, nb, seq)
    dq, dk, dv = _attn_bwd(qkv, do, nb, seq)
    dhs = [dq, dk, dv, dz_sb, dc_val, dc_gate, dz_cv, dg_sb, dg_cv]
    d_w_in = _weight_grad(h0b, dhs, "grad_w_in")
    small = jnp.concatenate([st_conv[2:3], st_conv[0:2], st_tail[0:3], d_conv_w], axis=0)
    return dhs, dr, (d_w_in, d_w_sb, d_w_cv, d_w_out), small


def _place():
    x, y, c = lax.axis_index("x"), lax.axis_index("y"), lax.axis_index("c")
    chips = [(1 - x, y), (x, 1 - y), (1 - x, 1 - y)]
    return x, y, c, chips


def _gather_weights(w_in, w_sb, w_cv, w_out, conv_w):
    n_arr = 5
    n_part = 4
    parts = [(0, k) for k in range(n_part)] + [(1, 0), (2, 0), (3, 0)]
    n_fwd = len(parts)
    n_ici = n_fwd + 1
    part_rows = HALF // n_part
    half_sq = SHARD_SQ // 2

    def body(i0, i1, i2, i3, i4, o0, o1, o2, o3, o4, ici_send, ici_recv, fwd_send, fwd_recv, local_sems):
        x, y, c, chips = _place()
        srcs = [i0, i1, i2, i3, i4]
        dsts = [o0, o1, o2, o3, o4]
        me = 2 * x + y

        def slot(a, chip):
            if a == 0:
                return dsts[a].at[:, pl.ds(pl.multiple_of(chip * SHARD_IN, 128), SHARD_IN)]
            if a == 4:
                return dsts[a].at[:, pl.ds(pl.multiple_of(chip * SHARD_SQ, 128), SHARD_SQ)]
            return dsts[a].at[pl.ds(pl.multiple_of(chip * SHARD_SQ, 16), SHARD_SQ), :]

        def part(ref, a, k, half):
            if a == 0:
                return ref.at[pl.ds(pl.multiple_of(half * HALF + k * part_rows, 16), part_rows), :]
            return ref.at[pl.ds(pl.multiple_of(half * half_sq, 16), half_sq), :]

        def copy(src, dst, send_sem, recv_sem, device):
            return pltpu.make_async_remote_copy(src_ref=src, dst_ref=dst, send_sem=send_sem, recv_sem=recv_sem,
                                                device_id=device, device_id_type=MESH)

        local = [pltpu.make_async_copy(srcs[a], slot(a, me), local_sems.at[a]) for a in range(n_arr)]
        for cp in local:
            cp.start()
        sends = []
        for j, (px, py) in enumerate(chips):
            for t, (a, k) in enumerate(parts):
                sends.append(copy(part(srcs[a], a, k, c), part(slot(a, me), a, k, c),
                                  ici_send.at[j * n_ici + t], ici_recv.at[j * n_ici + t], (px, py, c)))
            sends.append(copy(srcs[4], slot(4, me), ici_send.at[j * n_ici + n_fwd], ici_recv.at[j * n_ici + n_fwd],
                              (px, py, c)))
        for cp in sends:
            cp.start()
        for j, (px, py) in enumerate(chips):
            landed = 2 * px + py
            for t, (a, k) in enumerate(parts):
                here = part(slot(a, landed), a, k, c)
                copy(here, here, ici_send.at[j * n_ici + t], ici_recv.at[j * n_ici + t], (px, py, c)).wait_recv()
                passed = copy(here, here, fwd_send.at[j * n_fwd + t], fwd_recv.at[j * n_fwd + t], (x, y, 1 - c))
                passed.start()
                sends.append(passed)
            copy(srcs[4], slot(4, landed), ici_send.at[j * n_ici + n_fwd], ici_recv.at[j * n_ici + n_fwd],
                 (px, py, c)).wait_recv()
        for j, (px, py) in enumerate(chips):
            for t, (a, k) in enumerate(parts):
                there = part(slot(a, 2 * px + py), a, k, 1 - c)
                copy(there, there, fwd_send.at[j * n_fwd + t], fwd_recv.at[j * n_fwd + t], (x, y, 1 - c)).wait_recv()
        for cp in sends:
            cp.wait_send()
        for cp in local:
            cp.wait()

    return pl.pallas_call(
        body, in_specs=[ANY] * n_arr, out_specs=[ANY] * n_arr,
        out_shape=[jax.ShapeDtypeStruct((D_MODEL, IN_WIDTH), BF16)] + [jax.ShapeDtypeStruct((D_MODEL, D_MODEL), BF16)] * 3
        + [jax.ShapeDtypeStruct((CONV_ROWS, D_MODEL), F32)],
        scratch_shapes=[pltpu.SemaphoreType.DMA((3 * n_ici,)), pltpu.SemaphoreType.DMA((3 * n_ici,)),
                        pltpu.SemaphoreType.DMA((3 * n_fwd,)), pltpu.SemaphoreType.DMA((3 * n_fwd,)),
                        pltpu.SemaphoreType.DMA((n_arr,))],
        name="gather_weights")(w_in, w_sb, w_cv, w_out, conv_w)


def _row_chunks(src, dst, n, send_sem, recv_sem, device):
    rows = src.shape[0] // n

    def copy(s, d):
        return pltpu.make_async_remote_copy(src_ref=s, dst_ref=d, send_sem=send_sem, recv_sem=recv_sem,
                                            device_id=device, device_id_type=MESH)

    chunks = [copy(src.at[pl.ds(k * rows, rows)], dst.at[pl.ds(k * rows, rows)]) for k in range(n)]
    return chunks, copy(src, dst)


def _arrival(dst, send_sem, recv_sem, device):
    return pltpu.make_async_remote_copy(src_ref=dst, dst_ref=dst, send_sem=send_sem, recv_sem=recv_sem,
                                        device_id=device, device_id_type=MESH)


def _pair_exchange(d_in, d_sb, d_cv, d_out):
    def body(i0, i1, i2, i3, r0, r1, r2, r3, send_sems, recv_sems):
        x, y, c, _ = _place()
        srcs = [i0, i1, i2, i3]
        theirs = [r0, r1, r2, r3]
        sibling = (x, y, 1 - c)
        start = pl.multiple_of((1 - c) * HALF, 128)
        views = [srcs[0].at[pl.ds(start, HALF), :]] + [srcs[a].at[:, pl.ds(start, HALF)] for a in range(1, 4)]
        pieces, wholes = [], []
        for a in range(4):
            cps, whole = _row_chunks(views[a], theirs[a], N_CHUNKS, send_sems.at[a], recv_sems.at[a], sibling)
            pieces += cps
            wholes.append(whole)
        for cp in pieces:
            cp.start()
        for cp in wholes:
            cp.wait_recv()
        for cp in wholes:
            cp.wait_send()

    shapes = [jax.ShapeDtypeStruct((HALF, IN_WIDTH), F32)] + [jax.ShapeDtypeStruct((D_MODEL, HALF), F32)] * 3
    return pl.pallas_call(
        body, in_specs=[ANY] * 4, out_specs=[ANY] * 4, out_shape=shapes,
        scratch_shapes=[pltpu.SemaphoreType.DMA((4,)), pltpu.SemaphoreType.DMA((4,))],
        name="pair_exchange")(d_in, d_sb, d_cv, d_out)


def _chip_exchange_copies(srcs, theirs, send_sems, recv_sems):
    _, _, c, chips = _place()

    def shard(a, px, py):
        chip = 2 * px + py
        if a == 0:
            return srcs[a].at[:, pl.ds(pl.multiple_of(chip * SHARD_IN, 128), SHARD_IN)]
        return srcs[a].at[pl.ds(pl.multiple_of(chip * SHARD_SQ, 16), SHARD_SQ), :]

    pieces, wholes = [], []
    for j, (px, py) in enumerate(chips):
        for a in range(4):
            cps, whole = _row_chunks(shard(a, px, py), theirs[a].at[j], 4, send_sems.at[3 * a + j],
                                     recv_sems.at[3 * a + j], (px, py, c))
            pieces += cps
            wholes.append(whole)
    return pieces, wholes


def _pair_share(f_in, f_sb, f_cv, f_out):
    def body(i0, i1, i2, i3, o0, o1, o2, o3, send_sems, recv_sems):
        del i0, i1, i2, i3
        x, y, c, _ = _place()
        dsts = [o0, o1, o2, o3]
        sibling = (x, y, 1 - c)

        def half(a, which):
            start = pl.multiple_of(which * HALF, 128)
            if a == 0:
                return dsts[a].at[pl.ds(start, HALF), :]
            return dsts[a].at[:, pl.ds(start, HALF)]

        pieces, sent, arrived = [], [], []
        for a in range(4):
            cps, whole = _row_chunks(half(a, c), half(a, c), N_CHUNKS, send_sems.at[a], recv_sems.at[a], sibling)
            pieces += cps
            sent.append(whole)
            arrived.append(_arrival(half(a, 1 - c), send_sems.at[a], recv_sems.at[a], sibling))
        for cp in pieces:
            cp.start()
        for cp in arrived:
            cp.wait_recv()
        for cp in sent:
            cp.wait_send()

    return pl.pallas_call(
        body, in_specs=[ANY] * 4, out_specs=[ANY] * 4,
        out_shape=[jax.ShapeDtypeStruct((D_MODEL, SHARD_IN), F32)] + [jax.ShapeDtypeStruct((SHARD_SQ, D_MODEL), F32)] * 3,
        input_output_aliases={0: 0, 1: 1, 2: 2, 3: 3},
        scratch_shapes=[pltpu.SemaphoreType.DMA((4,)), pltpu.SemaphoreType.DMA((4,))],
        name="pair_share")(f_in, f_sb, f_cv, f_out)


def _small_allreduce(small):
    def body(s_ref, o_ref, slots, send_sems, recv_sems):
        x, y, c, _ = _place()
        me = 4 * x + 2 * y + c
        slots[me] = s_ref[...]
        sends = []
        for k in range(1, N_DEV):
            px, py, pc = (x + (k >> 2)) % 2, (y + ((k >> 1) & 1)) % 2, (c + (k & 1)) % 2
            sends.append(pltpu.make_async_remote_copy(
                src_ref=s_ref, dst_ref=slots.at[me], send_sem=send_sems.at[k - 1], recv_sem=recv_sems.at[k - 1],
                device_id=(px, py, pc), device_id_type=MESH))
        for cp in sends:
            cp.start()
        for k in range(1, N_DEV):
            px, py, pc = (x + (k >> 2)) % 2, (y + ((k >> 1) & 1)) % 2, (c + (k & 1)) % 2
            pltpu.make_async_remote_copy(
                src_ref=s_ref, dst_ref=slots.at[4 * px + 2 * py + pc], send_sem=send_sems.at[k - 1],
                recv_sem=recv_sems.at[k - 1], device_id=(px, py, pc), device_id_type=MESH).wait_recv()
        for cp in sends:
            cp.wait_send()
        total = slots[0]
        for d in range(1, N_DEV):
            total = total + slots[d]
        o_ref[...] = total
        o_ref[7:8, :] = jnp.zeros((1, D_MODEL), F32) + jnp.sum(total[7:8, :], axis=1, keepdims=True)

    return pl.pallas_call(
        body, in_specs=[pl.BlockSpec(memory_space=pltpu.VMEM)], out_specs=pl.BlockSpec(memory_space=pltpu.VMEM),
        out_shape=jax.ShapeDtypeStruct((SMALL_ROWS, D_MODEL), F32),
        scratch_shapes=[pltpu.VMEM((N_DEV, SMALL_ROWS, D_MODEL), F32), pltpu.SemaphoreType.DMA((N_DEV - 1,)),
                        pltpu.SemaphoreType.DMA((N_DEV - 1,))],
        name="small_allreduce")(small)


SUM_BLOCKS = 8


def _pair_sum(place, full, theirs, by_rows, name):
    rows, cols = theirs.shape
    tr = rows // SUM_BLOCKS

    def body(p_ref, a_ref, b_ref, o_ref):
        del p_ref
        o_ref[...] = (a_ref[...] + b_ref[...]).astype(BF16)

    mine = (lambda i, p: (p[1] * SUM_BLOCKS + i, 0)) if by_rows else (lambda i, p: (i, p[1]))
    spec = pl.BlockSpec((tr, cols), lambda i, p: (i, 0))
    return pl.pallas_call(
        body, out_shape=jax.ShapeDtypeStruct(theirs.shape, BF16),
        grid_spec=pltpu.PrefetchScalarGridSpec(num_scalar_prefetch=1, grid=(SUM_BLOCKS,),
                                               in_specs=[pl.BlockSpec((tr, cols), mine), spec], out_specs=spec),
        compiler_params=_params("parallel"), name=name)(place, full, theirs)


def _chip_sum(place, pair, got, by_rows, name):
    _, rows, cols = got.shape
    tr = rows // SUM_BLOCKS

    def body(p_ref, a_ref, g_ref, o_ref):
        del p_ref
        o_ref[...] = ((a_ref[...].astype(F32) + g_ref[0].astype(F32)) + g_ref[1].astype(F32)) + g_ref[2].astype(F32)

    if by_rows:
        own = lambda i, p: (i, p[0])
        out = lambda i, p: (p[1] * SUM_BLOCKS + i, 0)
        full = (2 * rows, cols)
    else:
        own = lambda i, p: (p[0] * SUM_BLOCKS + i, 0)
        out = lambda i, p: (i, p[1])
        full = (rows, 2 * cols)
    return pl.pallas_call(
        body, out_shape=jax.ShapeDtypeStruct(full, F32),
        grid_spec=pltpu.PrefetchScalarGridSpec(
            num_scalar_prefetch=1, grid=(SUM_BLOCKS,),
            in_specs=[pl.BlockSpec((tr, cols), own), pl.BlockSpec((3, tr, cols), lambda i, p: (0, i, 0))],
            out_specs=pl.BlockSpec((tr, cols), out)),
        compiler_params=_params("parallel"), name=name)(place, pair, got)


def _adamw(w, g, m, v, name):
    rows, cols = w.shape
    tr = rows // 4 if rows % 32 == 0 else rows
    c1 = 1.0 - ADAM_B1 ** ADAM_STEP
    c2 = 1.0 - ADAM_B2 ** ADAM_STEP

    def body(w_ref, g_ref, m_ref, v_ref, d_ref, mo_ref, vo_ref):
        g = g_ref[...]
        mn = ADAM_B1 * m_ref[...] + (1.0 - ADAM_B1) * g
        vn = ADAM_B2 * v_ref[...] + (1.0 - ADAM_B2) * (g * g)
        mo_ref[...] = mn
        vo_ref[...] = vn
        d_ref[...] = -ADAM_LR * ((mn / c1) / (jnp.sqrt(vn / c2) + ADAM_EPS) + ADAM_WD * w_ref[...])

    spec = pl.BlockSpec((tr, cols), lambda i: (i, 0))
    shape = jax.ShapeDtypeStruct(w.shape, F32)
    return pl.pallas_call(body, grid=(rows // tr,), in_specs=[spec] * 4, out_specs=[spec] * 3,
                          out_shape=[shape] * 3, compiler_params=_params("parallel"), name=name)(w, g, m, v)


GRAD_NAMES = ("in", "sb", "cv", "out")


def _place_scalars():
    return jnp.stack([2 * lax.axis_index("x") + lax.axis_index("y"), lax.axis_index("c")]).astype(jnp.int32)


def _pair_sums(grads):
    place = _place_scalars()
    theirs = _pair_exchange(*grads)
    return [_pair_sum(place, grads[a], theirs[a], a == 0, "pair_sum_" + GRAD_NAMES[a]) for a in range(4)]


def _finish_weight_grads(pair, got):
    place = _place_scalars()
    done = [_chip_sum(place, pair[a], got[a], a == 0, "chip_sum_" + GRAD_NAMES[a]) for a in range(4)]
    return _pair_share(*done)


def kernel(x, ln_in_g, ln_in_b, w_in, w_sb_proj, conv_w, conv_b, conv_ln_g, conv_ln_b, w_cv_proj, w_out, ln_post_g, ln_post_b, loss_target, m_ln_in_g, m_ln_in_b, m_w_in, m_w_sb_proj, m_conv_w, m_conv_b, m_conv_ln_g, m_conv_ln_b, m_w_cv_proj, m_w_out, m_ln_post_g, m_ln_post_b, v_ln_in_g, v_ln_in_b, v_w_in, v_w_sb_proj, v_conv_w, v_conv_b, v_conv_ln_g, v_conv_ln_b, v_w_cv_proj, v_w_out, v_ln_post_g, v_ln_post_b):
    nb, seq, _ = x.shape
    t = nb * seq
    vec_names = ("ln_in_g", "ln_in_b", "conv_b", "conv_ln_g", "conv_ln_b", "ln_post_g", "ln_post_b")
    vec_w = dict(zip(vec_names, (ln_in_g, ln_in_b, conv_b, conv_ln_g, conv_ln_b, ln_post_g, ln_post_b)))
    vec_m = dict(zip(vec_names, (m_ln_in_g, m_ln_in_b, m_conv_b, m_conv_ln_g, m_conv_ln_b, m_ln_post_g, m_ln_post_b)))
    vec_v = dict(zip(vec_names, (v_ln_in_g, v_ln_in_b, v_conv_b, v_conv_ln_g, v_conv_ln_b, v_ln_post_g, v_ln_post_b)))
    vecs = {k: a.reshape(1, D_MODEL) for k, a in vec_w.items()}

    pad_taps = lambda a: jnp.pad(a.reshape(CONV_K, SHARD_SQ), ((0, CONV_ROWS - CONV_K), (0, 0)))
    gathered = _gather_weights(w_in[0].astype(BF16), w_sb_proj[0].astype(BF16), w_cv_proj[0].astype(BF16),
                               w_out[0].astype(BF16), pad_taps(conv_w))
    full_in, full_sb, full_cv, full_out, full_conv = gathered

    x2 = x.reshape(t, D_MODEL)
    dhs, dr, big, small = _forward_backward(x2, loss_target.reshape(t, D_MODEL), nb, seq,
                                            full_in, full_sb, full_cv, full_out, full_conv, vecs)
    pair = _pair_sums(big)
    grad_x, st_in, got = _in_proj_bwd(dhs, full_in, dr, x2, vecs["ln_in_g"], pair)
    g_in, g_sb, g_cv, g_out = _finish_weight_grads(pair, got)
    small = _small_allreduce(jnp.concatenate([st_in[0:2], small], axis=0))

    chip = 2 * lax.axis_index("x") + lax.axis_index("y")
    g_conv = lax.dynamic_slice(small, (8, chip * SHARD_SQ), (CONV_ROWS, SHARD_SQ))
    loss = small[7, 0]

    grads, deltas, new_m, new_v = {}, {}, {}, {}
    stack = lambda d: jnp.concatenate([d[k].reshape(1, D_MODEL) for k in vec_names] + [jnp.zeros((1, D_MODEL), F32)])
    vd, vm, vv = _adamw(stack(vec_w), small[0:8], stack(vec_m), stack(vec_v), "adamw_vectors")
    for r, k in enumerate(vec_names):
        shape = vec_w[k].shape
        grads[k] = small[r].reshape(shape)
        deltas[k], new_m[k], new_v[k] = vd[r].reshape(shape), vm[r].reshape(shape), vv[r].reshape(shape)

    big_w = {"w_in": (w_in, g_in, m_w_in, v_w_in), "w_sb_proj": (w_sb_proj, g_sb, m_w_sb_proj, v_w_sb_proj),
             "w_cv_proj": (w_cv_proj, g_cv, m_w_cv_proj, v_w_cv_proj), "w_out": (w_out, g_out, m_w_out, v_w_out)}
    for k, (w, g, m, v) in big_w.items():
        d, mn, vn = _adamw(w[0], g, m[0], v[0], "adamw_" + k)
        grads[k], deltas[k], new_m[k], new_v[k] = g[None], d[None], mn[None], vn[None]
    d, mn, vn = _adamw(pad_taps(conv_w), g_conv, pad_taps(m_conv_w), pad_taps(v_conv_w), "adamw_conv_w")
    grads["conv_w"] = g_conv[None, :CONV_K]
    deltas["conv_w"], new_m["conv_w"], new_v["conv_w"] = d[None, :CONV_K], mn[None, :CONV_K], vn[None, :CONV_K]

    order = ("ln_in_g", "ln_in_b", "w_in", "w_sb_proj", "conv_w", "conv_b", "conv_ln_g", "conv_ln_b",
             "w_cv_proj", "w_out", "ln_post_g", "ln_post_b")
    return (loss, grad_x.reshape(x.shape), *[grads[k] for k in order], *[deltas[k] for k in order],
            *[new_m[k] for k in order], *[new_v[k] for k in order])
```

```python
import functools

import jax
import jax.numpy as jnp
from jax import lax
from jax.experimental import pallas as pl
from jax.experimental.pallas import tpu as pltpu

F32 = jnp.float32
BF16 = jnp.bfloat16

D_MODEL = 1024
N_HEADS = 16
HEAD_DIM = 64
HEAD_GROUP = 4
GROUP_W = HEAD_GROUP * HEAD_DIM
N_GROUPS = N_HEADS // HEAD_GROUP
N_PIECES = 9
IN_WIDTH = N_PIECES * D_MODEL
Q_BLOCK = 256
CONV_K = 31
CONV_ROWS = 32
HALO = 32
LN_EPS = 1e-5
ALPHA = 2.0 ** 0.25
Q_SCALE = 0.125
N_CHIPS = 4
N_DEV = 8
SHARD_IN = IN_WIDTH // N_CHIPS
SHARD_SQ = D_MODEL // N_CHIPS
HALF = D_MODEL // 2
SMALL_ROWS = 40
N_CHUNKS = 16

ADAM_LR = 0.001
ADAM_B1 = 0.9
ADAM_B2 = 0.999
ADAM_EPS = 1e-08
ADAM_WD = 0.01
ADAM_STEP = 10

MESH = pl.DeviceIdType.MESH
ANY = pl.BlockSpec(memory_space=pl.ANY)
VMEM_LIMIT = 60 * 1024 * 1024

NT = (((1,), (1,)), ((), ()))
TN = (((0,), (0,)), ((), ()))


def _sigmoid(x):
    return 1.0 / (1.0 + jnp.exp(-x))


def _dot(a, b):
    return jnp.dot(a, b, preferred_element_type=F32)


def _dot_nt(a, b):
    return lax.dot_general(a, b, NT, preferred_element_type=F32)


def _dot_tn(a, b):
    return lax.dot_general(a, b, TN, preferred_element_type=F32)


def _split_bf16(x):
    hi = x.astype(BF16)
    lo = (x - hi.astype(F32)).astype(BF16)
    return hi, lo


def _ln_stats(x):
    mu = jnp.mean(x, axis=-1, keepdims=True)
    xc = x - mu
    var = jnp.mean(xc * xc, axis=-1, keepdims=True)
    rstd = lax.rsqrt(var + LN_EPS)
    return xc * rstd, rstd


def _ln_bwd(dy, xhat, rstd, g):
    dxh = dy * g
    m1 = jnp.mean(dxh, axis=-1, keepdims=True)
    m2 = jnp.mean(dxh * xhat, axis=-1, keepdims=True)
    return rstd * (dxh - m1 - xhat * m2)


def _params(*sem):
    return pltpu.CompilerParams(dimension_semantics=sem, vmem_limit_bytes=VMEM_LIMIT)


def _ln_in(x, g, b):
    t = x.shape[0]
    tm = min(512, t)

    def body(x_ref, g_ref, b_ref, hf_ref, hb_ref):
        xhat, _ = _ln_stats(x_ref[...])
        y = xhat * g_ref[...] + b_ref[...]
        hf_ref[...] = y
        hb_ref[...] = y.astype(BF16)

    row = pl.BlockSpec((tm, D_MODEL), lambda i: (i, 0))
    vec = pl.BlockSpec((1, D_MODEL), lambda i: (0, 0))
    return pl.pallas_call(
        body, grid=(t // tm,), in_specs=[row, vec, vec], out_specs=[row, row],
        out_shape=[jax.ShapeDtypeStruct((t, D_MODEL), F32), jax.ShapeDtypeStruct((t, D_MODEL), BF16)],
        compiler_params=_params("parallel"), name="ln_in")(x, g, b)


def _in_proj(h, w, col0, ncol, out_dtype, scale_first, name):
    t = h.shape[0]
    tm = min(512, t)

    def body(h_ref, w_ref, o_ref):
        res = _dot(h_ref[...], w_ref[...])
        if scale_first:
            res = res * jnp.where(pl.program_id(0) == 0, Q_SCALE, 1.0)
        o_ref[...] = res.astype(out_dtype)

    return pl.pallas_call(
        body, grid=(ncol, t // tm),
        in_specs=[pl.BlockSpec((tm, D_MODEL), lambda j, i: (i, 0)),
                  pl.BlockSpec((D_MODEL, D_MODEL), lambda j, i: (0, j + col0))],
        out_specs=pl.BlockSpec((tm, D_MODEL), lambda j, i: (i, j)),
        out_shape=jax.ShapeDtypeStruct((t, ncol * D_MODEL), out_dtype),
        compiler_params=_params("parallel", "parallel"), name=name)(h, w)


def _sb_block(l, diag, tri, carry):
    lnb = jnp.minimum(-l, 0.0) - jnp.log(1.0 + jnp.exp(-jnp.abs(l)))
    if diag is not None:
        lnb = jnp.where(diag, lnb, 0.0)
    hi, lo = _split_bf16(lnb)
    suffix = _dot(hi, tri) + _dot(lo, tri)
    a = jnp.exp(l + suffix + carry)
    if diag is not None:
        a = jnp.where(diag, a, 0.0)
    return a, lnb, suffix[:, 0:1]


def _attn_consts():
    head = lax.broadcasted_iota(jnp.int32, (Q_BLOCK, GROUP_W), 1) // HEAD_DIM
    row = lax.broadcasted_iota(jnp.int32, (Q_BLOCK, Q_BLOCK), 0)
    col = lax.broadcasted_iota(jnp.int32, (Q_BLOCK, Q_BLOCK), 1)
    return head, col < row, (row >= col).astype(BF16), (row <= col).astype(BF16)


def _split_heads(t, head):
    zero = jnp.zeros_like(t)
    return [jnp.where(head == h, t, zero) for h in range(HEAD_GROUP)]


def _attn_fwd(qkv, nb, seq):
    nq = seq // Q_BLOCK

    def body(q_ref, k_ref, v_ref, o_ref):
        qi = pl.program_id(2)
        head, diag, tri, _ = _attn_consts()
        qh = _split_heads(q_ref[...], head)
        o_ref[...] = jnp.zeros_like(o_ref)

        def block(ks, carries, masked):
            kblk = k_ref[pl.ds(ks, Q_BLOCK), :]
            vstack = jnp.concatenate(_split_heads(v_ref[pl.ds(ks, Q_BLOCK), :], head), axis=0)
            weights, out = [], []
            for h in range(HEAD_GROUP):
                a, _, blk_sum = _sb_block(_dot_nt(qh[h], kblk), diag if masked else None, tri, carries[h])
                weights.append(a.astype(BF16))
                out.append(carries[h] + blk_sum)
            o_ref[...] += _dot(jnp.concatenate(weights, axis=1), vstack)
            return tuple(out)

        zero_col = jnp.zeros((Q_BLOCK, 1), F32)
        carries = block(pl.multiple_of(qi * Q_BLOCK, Q_BLOCK), (zero_col,) * HEAD_GROUP, True)
        lax.fori_loop(1, qi + 1, lambda jj, cs: block(pl.multiple_of((qi - jj) * Q_BLOCK, Q_BLOCK), cs, False), carries)

    return pl.pallas_call(
        body, grid=(nb, N_GROUPS, nq),
        in_specs=[pl.BlockSpec((Q_BLOCK, GROUP_W), lambda b, g, i: (b * nq + i, g)),
                  pl.BlockSpec((seq, GROUP_W), lambda b, g, i: (b, N_GROUPS + g)),
                  pl.BlockSpec((seq, GROUP_W), lambda b, g, i: (b, 2 * N_GROUPS + g))],
        out_specs=pl.BlockSpec((Q_BLOCK, GROUP_W), lambda b, g, i: (b * nq + i, g)),
        out_shape=jax.ShapeDtypeStruct((nb * seq, D_MODEL), F32),
        compiler_params=_params("parallel", "parallel", "arbitrary"), name="attn_fwd")(qkv, qkv, qkv)


def _attn_bwd(qkv, do, nb, seq):
    nq = seq // Q_BLOCK

    def body(q_ref, k_ref, v_ref, do_ref, dq_ref, dk_ref, dv_ref, g_s, nb_s, dq_acc, dk_acc, dv_acc):
        qi = pl.program_id(2)

        @pl.when(qi == 0)
        def _():
            dk_acc[...] = jnp.zeros_like(dk_acc)
            dv_acc[...] = jnp.zeros_like(dv_acc)

        head, diag, tri_suffix, tri_prefix = _attn_consts()
        qh = _split_heads(q_ref[...], head)
        doh = _split_heads(do_ref[...], head)
        qstack = jnp.concatenate(qh, axis=0)
        dostack = jnp.concatenate(doh, axis=0)
        dq_acc[...] = jnp.zeros_like(dq_acc)

        def block_a(kb, carries, masked):
            ks = pl.multiple_of(kb * Q_BLOCK, Q_BLOCK)
            kblk = k_ref[pl.ds(ks, Q_BLOCK), :]
            vblk = v_ref[pl.ds(ks, Q_BLOCK), :]
            weights, out = [], []
            for h in range(HEAD_GROUP):
                a, lnb, blk_sum = _sb_block(_dot_nt(qh[h], kblk), diag if masked else None, tri_suffix, carries[h])
                g_s[h, kb] = a * _dot_nt(doh[h], vblk)
                nb_s[h, kb] = jnp.exp(lnb)
                weights.append(a.astype(BF16))
                out.append(carries[h] + blk_sum)
            dv_acc[pl.ds(ks, Q_BLOCK), :] += _dot_tn(jnp.concatenate(weights, axis=0), dostack)
            return tuple(out)

        zero_col = jnp.zeros((Q_BLOCK, 1), F32)
        carries = block_a(qi, (zero_col,) * HEAD_GROUP, True)
        lax.fori_loop(1, qi + 1, lambda jj, cs: block_a(qi - jj, cs, False), carries)

        def block_b(kb, prefixes, masked):
            ks = pl.multiple_of(kb * Q_BLOCK, Q_BLOCK)
            kstack = jnp.concatenate(_split_heads(k_ref[pl.ds(ks, Q_BLOCK), :], head), axis=0)
            dls, out = [], []
            for h in range(HEAD_GROUP):
                g = g_s[h, kb]
                not_beta = nb_s[h, kb]
                hi, lo = _split_bf16(g)
                incl = _dot(hi, tri_prefix) + _dot(lo, tri_prefix)
                before = incl - g + prefixes[h]
                dl = not_beta * (g + before) - before
                if masked:
                    dl = jnp.where(diag, dl, 0.0)
                dls.append(dl.astype(BF16))
                out.append(prefixes[h] + incl[:, Q_BLOCK - 1:Q_BLOCK])
            dq_acc[...] += _dot(jnp.concatenate(dls, axis=1), kstack)
            dk_acc[pl.ds(ks, Q_BLOCK), :] += _dot_tn(jnp.concatenate(dls, axis=0), qstack)
            return tuple(out)

        prefixes = lax.fori_loop(0, qi, lambda kb, ps: block_b(kb, ps, False), (zero_col,) * HEAD_GROUP)
        block_b(qi, prefixes, True)
        dq_ref[...] = (dq_acc[...] * Q_SCALE).astype(BF16)

        @pl.when(qi == nq - 1)
        def _():
            dk_ref[...] = dk_acc[...].astype(BF16)
            dv_ref[...] = dv_acc[...].astype(BF16)

    t = nb * seq
    qspec = pl.BlockSpec((Q_BLOCK, GROUP_W), lambda b, g, i: (b * nq + i, g))
    kvout = pl.BlockSpec((seq, GROUP_W), lambda b, g, i: (b, g))
    strip = pltpu.VMEM((HEAD_GROUP, nq, Q_BLOCK, Q_BLOCK), F32)
    return pl.pallas_call(
        body, grid=(nb, N_GROUPS, nq),
        in_specs=[qspec,
                  pl.BlockSpec((seq, GROUP_W), lambda b, g, i: (b, N_GROUPS + g)),
                  pl.BlockSpec((seq, GROUP_W), lambda b, g, i: (b, 2 * N_GROUPS + g)),
                  qspec],
        out_specs=[qspec, kvout, kvout],
        out_shape=[jax.ShapeDtypeStruct((t, D_MODEL), BF16)] * 3,
        scratch_shapes=[strip, strip, pltpu.VMEM((Q_BLOCK, GROUP_W), F32),
                        pltpu.VMEM((seq, GROUP_W), F32), pltpu.VMEM((seq, GROUP_W), F32)],
        compiler_params=_params("parallel", "parallel", "arbitrary"), name="attn_bwd")(qkv, qkv, qkv, do)


def _conv_block_rows(seq):
    return min(256, seq)


def _conv_fwd(hrest, conv_w, conv_b, ln_g, ln_b, nb, seq):
    ts = _conv_block_rows(seq)
    nblk = seq // ts
    lanes = 128

    def body(cv_ref, cg_ref, cvh_ref, cgh_ref, z_ref, w_ref, cb_ref, g_ref, b_ref, u1_ref, cvin_ref, ubuf):
        i = pl.program_id(1)
        halo = cvh_ref[...] * _sigmoid(cgh_ref[...])
        ubuf[0:HALO, :] = jnp.where(i > 0, halo, 0.0)
        ubuf[HALO:HALO + ts, :] = cv_ref[...] * _sigmoid(cg_ref[...])
        for cc in range(D_MODEL // lanes):
            cs = slice(cc * lanes, (cc + 1) * lanes)
            acc = jnp.zeros((ts, lanes), F32) + cb_ref[:, cs]
            for k in range(CONV_K):
                acc = acc + w_ref[k:k + 1, cs] * ubuf[pl.ds(HALO - CONV_K + 1 + k, ts), cs]
            u1_ref[:, cs] = acc
        xhat, _ = _ln_stats(u1_ref[...])
        u2 = xhat * g_ref[...] + b_ref[...]
        z = z_ref[...]
        cvin_ref[...] = (u2 * _sigmoid(u2) * z * _sigmoid(z)).astype(BF16)

    def main(colblk):
        return pl.BlockSpec((ts, D_MODEL), lambda b, i: (b * nblk + i, colblk))

    def halo(colblk):
        return pl.BlockSpec((HALO, D_MODEL),
                            lambda b, i: (jnp.maximum((b * seq + i * ts) // HALO - 1, 0), colblk))

    vec = pl.BlockSpec((1, D_MODEL), lambda b, i: (0, 0))
    t = nb * seq
    return pl.pallas_call(
        body, grid=(nb, nblk),
        in_specs=[main(1), main(2), halo(1), halo(2), main(3),
                  pl.BlockSpec((CONV_ROWS, D_MODEL), lambda b, i: (0, 0)), vec, vec, vec],
        out_specs=[main(0), main(0)],
        out_shape=[jax.ShapeDtypeStruct((t, D_MODEL), F32), jax.ShapeDtypeStruct((t, D_MODEL), BF16)],
        scratch_shapes=[pltpu.VMEM((HALO + ts, D_MODEL), F32)],
        compiler_params=_params("parallel", "parallel"), name="conv_fwd")(
            hrest, hrest, hrest, hrest, hrest, conv_w, conv_b, ln_g, ln_b)


def _tail(o, hrest, cvin, h0, tgt, w_sb, w_cv, w_out, ln_g, ln_b):
    t = o.shape[0]
    tm = min(256, t)

    def body(o_ref, z_ref, gs_ref, gc_ref, cvin_ref, h0_ref, tgt_ref, wsb_ref, wcv_ref, wout_ref, g_ref, b_ref,
             dr_ref, drb_ref, a_ref, mg_ref, dysb_ref, dycv_ref, do_ref, dz_ref, dgs_ref, dgc_ref, dcvin_ref,
             st_ref):
        @pl.when(pl.program_id(0) == 0)
        def _():
            st_ref[...] = jnp.zeros_like(st_ref)

        o = o_ref[...]
        z = z_ref[...]
        sz = _sigmoid(z)
        a = (o * z * sz).astype(BF16)
        a_ref[...] = a
        y_sb = _dot(a, wsb_ref[...])
        y_cv = _dot(cvin_ref[...], wcv_ref[...])
        s_sb = _sigmoid(gs_ref[...])
        s_cv = _sigmoid(gc_ref[...])
        merged = (s_sb * y_sb + s_cv * y_cv).astype(BF16)
        mg_ref[...] = merged
        r = ALPHA * h0_ref[...] + _dot(merged, wout_ref[...])
        xhat, rstd = _ln_stats(r)
        g = g_ref[...]
        err = xhat * g + b_ref[...] - tgt_ref[...]
        dy = err * (1.0 / D_MODEL)
        st_ref[0:1, :] += jnp.sum(dy * xhat, axis=0, keepdims=True)
        st_ref[1:2, :] += jnp.sum(dy, axis=0, keepdims=True)
        st_ref[2:3, :] += (0.5 / D_MODEL) * jnp.sum(err * err, axis=0, keepdims=True)
        dr = _ln_bwd(dy, xhat, rstd, g)
        dr_ref[...] = dr
        drb = dr.astype(BF16)
        drb_ref[...] = drb
        dm = _dot_nt(drb, wout_ref[...])
        dy_sb = (dm * s_sb).astype(BF16)
        dy_cv = (dm * s_cv).astype(BF16)
        dysb_ref[...] = dy_sb
        dycv_ref[...] = dy_cv
        dgs_ref[...] = (dm * y_sb * s_sb * (1.0 - s_sb)).astype(BF16)
        dgc_ref[...] = (dm * y_cv * s_cv * (1.0 - s_cv)).astype(BF16)
        da = _dot_nt(dy_sb, wsb_ref[...])
        dcvin_ref[...] = _dot_nt(dy_cv, wcv_ref[...])
        do_ref[...] = (da * z * sz).astype(BF16)
        dz_ref[...] = (da * o * sz * (1.0 + z * (1.0 - sz))).astype(BF16)

    def tok(colblk=0):
        return pl.BlockSpec((tm, D_MODEL), lambda i: (i, colblk))

    wspec = pl.BlockSpec((D_MODEL, D_MODEL), lambda i: (0, 0), pipeline_mode=pl.Buffered(1))
    vec = pl.BlockSpec((1, D_MODEL), lambda i: (0, 0))
    bf = jax.ShapeDtypeStruct((t, D_MODEL), BF16)
    f32 = jax.ShapeDtypeStruct((t, D_MODEL), F32)
    return pl.pallas_call(
        body, grid=(t // tm,),
        in_specs=[tok(), tok(0), tok(4), tok(5), tok(), tok(), tok(), wspec, wspec, wspec, vec, vec],
        out_specs=[tok()] * 11 + [pl.BlockSpec((8, D_MODEL), lambda i: (0, 0))],
        out_shape=[f32] + [bf] * 9 + [f32, jax.ShapeDtypeStruct((8, D_MODEL), F32)],
        compiler_params=_params("arbitrary"), name="tail")(
            o, hrest, hrest, hrest, cvin, h0, tgt, w_sb, w_cv, w_out, ln_g, ln_b)


def _conv_branch_bwd(dcvin, u1, hrest, ln_g, ln_b):
    t = u1.shape[0]
    tm = min(512, t)

    def body(dc_ref, u1_ref, z_ref, g_ref, b_ref, du1_ref, dz_ref, st_ref):
        @pl.when(pl.program_id(0) == 0)
        def _():
            st_ref[...] = jnp.zeros_like(st_ref)

        xhat, rstd = _ln_stats(u1_ref[...])
        g = g_ref[...]
        u2 = xhat * g + b_ref[...]
        s2 = _sigmoid(u2)
        z = z_ref[...]
        sz = _sigmoid(z)
        dc = dc_ref[...]
        dz_ref[...] = (dc * (u2 * s2) * sz * (1.0 + z * (1.0 - sz))).astype(BF16)
        du2 = dc * (z * sz) * s2 * (1.0 + u2 * (1.0 - s2))
        st_ref[0:1, :] += jnp.sum(du2 * xhat, axis=0, keepdims=True)
        st_ref[1:2, :] += jnp.sum(du2, axis=0, keepdims=True)
        du1 = _ln_bwd(du2, xhat, rstd, g)
        du1_ref[...] = du1
        st_ref[2:3, :] += jnp.sum(du1, axis=0, keepdims=True)

    tok = pl.BlockSpec((tm, D_MODEL), lambda i: (i, 0))
    vec = pl.BlockSpec((1, D_MODEL), lambda i: (0, 0))
    return pl.pallas_call(
        body, grid=(t // tm,),
        in_specs=[tok, tok, pl.BlockSpec((tm, D_MODEL), lambda i: (i, 3)), vec, vec],
        out_specs=[tok, tok, pl.BlockSpec((8, D_MODEL), lambda i: (0, 0))],
        out_shape=[jax.ShapeDtypeStruct((t, D_MODEL), F32), jax.ShapeDtypeStruct((t, D_MODEL), BF16),
                   jax.ShapeDtypeStruct((8, D_MODEL), F32)],
        compiler_params=_params("arbitrary"), name="conv_branch_bwd")(dcvin, u1, hrest, ln_g, ln_b)


def _conv_bwd(du1, hrest, conv_w, nb, seq):
    ts = _conv_block_rows(seq)
    nblk = seq // ts
    lanes = 128
    last_halo = nb * seq // HALO - 1

    def body(du_ref, duh_ref, cv_ref, cg_ref, cvh_ref, cgh_ref, w_ref, dcv_ref, dcg_ref, dw_ref, ubuf, dbuf):
        b = pl.program_id(0)
        i = pl.program_id(1)

        @pl.when((b == 0) & (i == 0))
        def _():
            dw_ref[...] = jnp.zeros_like(dw_ref)

        cv = cv_ref[...]
        sg = _sigmoid(cg_ref[...])
        halo = cvh_ref[...] * _sigmoid(cgh_ref[...])
        ubuf[0:HALO, :] = jnp.where(i > 0, halo, 0.0)
        ubuf[HALO:HALO + ts, :] = cv * sg
        dbuf[0:ts, :] = du_ref[...]
        dbuf[ts:ts + HALO, :] = jnp.where(i < nblk - 1, duh_ref[...], 0.0)
        for cc in range(D_MODEL // lanes):
            cs = slice(cc * lanes, (cc + 1) * lanes)
            dmain = dbuf[0:ts, cs]
            acc = jnp.zeros((ts, lanes), F32)
            for k in range(CONV_K):
                acc = acc + w_ref[k:k + 1, cs] * dbuf[pl.ds(CONV_K - 1 - k, ts), cs]
                dw_ref[k:k + 1, cs] += jnp.sum(dmain * ubuf[pl.ds(HALO - CONV_K + 1 + k, ts), cs],
                                               axis=0, keepdims=True)
            dcv_ref[:, cs] = (acc * sg[:, cs]).astype(BF16)
            dcg_ref[:, cs] = (acc * cv[:, cs] * sg[:, cs] * (1.0 - sg[:, cs])).astype(BF16)

    def main(colblk):
        return pl.BlockSpec((ts, D_MODEL), lambda b, i: (b * nblk + i, colblk))

    def halo_before(colblk):
        return pl.BlockSpec((HALO, D_MODEL),
                            lambda b, i: (jnp.maximum((b * seq + i * ts) // HALO - 1, 0), colblk))

    halo_after = pl.BlockSpec((HALO, D_MODEL),
                              lambda b, i: (jnp.minimum((b * seq + (i + 1) * ts) // HALO, last_halo), 0))
    t = nb * seq
    return pl.pallas_call(
        body, grid=(nb, nblk),
        in_specs=[main(0), halo_after, main(1), main(2), halo_before(1), halo_before(2),
                  pl.BlockSpec((CONV_ROWS, D_MODEL), lambda b, i: (0, 0))],
        out_specs=[main(0), main(0), pl.BlockSpec((CONV_ROWS, D_MODEL), lambda b, i: (0, 0))],
        out_shape=[jax.ShapeDtypeStruct((t, D_MODEL), BF16), jax.ShapeDtypeStruct((t, D_MODEL), BF16),
                   jax.ShapeDtypeStruct((CONV_ROWS, D_MODEL), F32)],
        scratch_shapes=[pltpu.VMEM((HALO + ts, D_MODEL), F32), pltpu.VMEM((ts + HALO, D_MODEL), F32)],
        compiler_params=_params("arbitrary", "arbitrary"), name="conv_bwd")(
            du1, du1, hrest, hrest, hrest, hrest, conv_w)


def _weight_grad(x, dys, name):
    n = len(dys)
    t = x.shape[0]
    tm = min(512, t)

    def body(x_ref, *refs):
        o_ref = refs[n]
        j = pl.program_id(0)

        @pl.when(pl.program_id(1) == 0)
        def _():
            o_ref[...] = jnp.zeros_like(o_ref)

        for p in range(n):
            @pl.when(j == p)
            def _(p=p):
                o_ref[...] += _dot_tn(x_ref[...], refs[p][...])

    def dy_spec(p):
        return pl.BlockSpec((tm, D_MODEL), lambda j, i: (jnp.where(j == p, i, 0), 0))

    return pl.pallas_call(
        body, grid=(n, t // tm),
        in_specs=[pl.BlockSpec((tm, D_MODEL), lambda j, i: (i, 0))] + [dy_spec(p) for p in range(n)],
        out_specs=pl.BlockSpec((D_MODEL, D_MODEL), lambda j, i: (0, j)),
        out_shape=jax.ShapeDtypeStruct((D_MODEL, n * D_MODEL), F32),
        compiler_params=_params("parallel", "arbitrary"), name=name)(x, *dys)


def _in_proj_bwd(dhs, w_in, dr, x, ln_g, pair):
    n = len(dhs)
    t = x.shape[0]
    tm = min(512, t)
    n_i = t // tm

    def body(*refs):
        dh_refs = refs[:n]
        w_ref, dr_ref, x_ref, g_ref = refs[n:n + 4]
        pair_refs = refs[n + 4:n + 8]
        gx_ref, st_ref = refs[n + 8:n + 10]
        got_refs = refs[n + 10:n + 14]
        acc, send_sems, recv_sems = refs[n + 14:]
        i = pl.program_id(0)
        p = pl.program_id(1)
        pieces, wholes = _chip_exchange_copies(pair_refs, got_refs, send_sems, recv_sems)

        @pl.when((i == 0) & (p == 0))
        def _():
            st_ref[...] = jnp.zeros_like(st_ref)
            for cp in pieces:
                cp.start()

        @pl.when((i == n_i - 1) & (p == n - 1))
        def _():
            for cp in wholes:
                cp.wait_recv()
            for cp in wholes:
                cp.wait_send()

        @pl.when(p == 0)
        def _():
            acc[...] = ALPHA * dr_ref[...]

        for q in range(n):
            @pl.when(p == q)
            def _(q=q):
                acc[...] += _dot_nt(dh_refs[q][...], w_ref[...])

        @pl.when(p == n - 1)
        def _():
            xhat, rstd = _ln_stats(x_ref[...])
            dh0 = acc[...]
            st_ref[0:1, :] += jnp.sum(dh0 * xhat, axis=0, keepdims=True)
            st_ref[1:2, :] += jnp.sum(dh0, axis=0, keepdims=True)
            gx_ref[...] = _ln_bwd(dh0, xhat, rstd, g_ref[...])

    tok = pl.BlockSpec((tm, D_MODEL), lambda i, p: (i, 0))
    got = [jax.ShapeDtypeStruct((3, HALF, SHARD_IN), BF16)] + [jax.ShapeDtypeStruct((3, SHARD_SQ, HALF), BF16)] * 3
    res = pl.pallas_call(
        body, grid=(n_i, n),
        in_specs=[tok] * n + [pl.BlockSpec((D_MODEL, D_MODEL), lambda i, p: (0, p)), tok, tok,
                              pl.BlockSpec((1, D_MODEL), lambda i, p: (0, 0))] + [ANY] * 4,
        out_specs=[tok, pl.BlockSpec((8, D_MODEL), lambda i, p: (0, 0))] + [ANY] * 4,
        out_shape=[jax.ShapeDtypeStruct((t, D_MODEL), F32), jax.ShapeDtypeStruct((8, D_MODEL), F32)] + got,
        scratch_shapes=[pltpu.VMEM((tm, D_MODEL), F32), pltpu.SemaphoreType.DMA((12,)), pltpu.SemaphoreType.DMA((12,))],
        compiler_params=_params("arbitrary", "arbitrary"), name="in_proj_bwd")(*dhs, w_in, dr, x, ln_g, *pair)
    return res[0], res[1], res[2:]


def _forward_backward(x, tgt, nb, seq, w_in, w_sb, w_cv, w_out, conv_w, vecs):
    h0, h0b = _ln_in(x, vecs["ln_in_g"], vecs["ln_in_b"])
    qkv = _in_proj(h0b, w_in, 0, 3, BF16, True, "in_proj_qkv")
    hrest = _in_proj(h0b, w_in, 3, 6, F32, False, "in_proj_rest")
    o = _attn_fwd(qkv, nb, seq)
    u1, cvin = _conv_fwd(hrest, conv_w, vecs["conv_b"], vecs["conv_ln_g"], vecs["conv_ln_b"], nb, seq)
    (dr, drb, a, merged, dy_sb, dy_cv, do, dz_sb, dg_sb, dg_cv, dcvin, st_tail) = _tail(
        o, hrest, cvin, h0, tgt, w_sb, w_cv, w_out, vecs["ln_post_g"], vecs["ln_post_b"])
    d_w_sb = _weight_grad(a, [dy_sb], "grad_w_sb")
    d_w_cv = _weight_grad(cvin, [dy_cv], "grad_w_cv")
    d_w_out = _weight_grad(merged, [drb], "grad_w_out")
    du1, dz_cv, st_conv = _conv_branch_bwd(dcvin, u1, hrest, vecs["conv_ln_g"], vecs["conv_ln_b"])
    dc_val, dc_gate, d_conv_w = _conv_bwd(du1, hrest, conv_w, nb, seq)
    dq, dk, dv = _attn_bwd(qkv, do, nb, seq)
    dhs = [dq, dk, dv, dz_sb, dc_val, dc_gate, dz_cv, dg_sb, dg_cv]
    d_w_in = _weight_grad(h0b, dhs, "grad_w_in")
    small = jnp.concatenate([st_conv[2:3], st_conv[0:2], st_tail[0:3], d_conv_w], axis=0)
    return dhs, dr, (d_w_in, d_w_sb, d_w_cv, d_w_out), small


def _place():
    x, y, c = lax.axis_index("x"), lax.axis_index("y"), lax.axis_index("c")
    chips = [(1 - x, y), (x, 1 - y), (1 - x, 1 - y)]
    return x, y, c, chips


def _gather_weights(w_in, w_sb, w_cv, w_out, conv_w):
    n_arr = 5
    n_part = 4
    parts = [(0, k) for k in range(n_part)] + [(1, 0), (2, 0), (3, 0)]
    n_fwd = len(parts)
    n_ici = n_fwd + 1
    part_rows = HALF // n_part
    half_sq = SHARD_SQ // 2

    def body(i0, i1, i2, i3, i4, o0, o1, o2, o3, o4, ici_send, ici_recv, fwd_send, fwd_recv, local_sems):
        x, y, c, chips = _place()
        srcs = [i0, i1, i2, i3, i4]
        dsts = [o0, o1, o2, o3, o4]
        me = 2 * x + y

        def slot(a, chip):
            if a == 0:
                return dsts[a].at[:, pl.ds(pl.multiple_of(chip * SHARD_IN, 128), SHARD_IN)]
            if a == 4:
                return dsts[a].at[:, pl.ds(pl.multiple_of(chip * SHARD_SQ, 128), SHARD_SQ)]
            return dsts[a].at[pl.ds(pl.multiple_of(chip * SHARD_SQ, 16), SHARD_SQ), :]

        def part(ref, a, k, half):
            if a == 0:
                return ref.at[pl.ds(pl.multiple_of(half * HALF + k * part_rows, 16), part_rows), :]
            return ref.at[pl.ds(pl.multiple_of(half * half_sq, 16), half_sq), :]

        def copy(src, dst, send_sem, recv_sem, device):
            return pltpu.make_async_remote_copy(src_ref=src, dst_ref=dst, send_sem=send_sem, recv_sem=recv_sem,
                                                device_id=device, device_id_type=MESH)

        local = [pltpu.make_async_copy(srcs[a], slot(a, me), local_sems.at[a]) for a in range(n_arr)]
        for cp in local:
            cp.start()
        sends = []
        for j, (px, py) in enumerate(chips):
            for t, (a, k) in enumerate(parts):
                sends.append(copy(part(srcs[a], a, k, c), part(slot(a, me), a, k, c),
                                  ici_send.at[j * n_ici + t], ici_recv.at[j * n_ici + t], (px, py, c)))
            sends.append(copy(srcs[4], slot(4, me), ici_send.at[j * n_ici + n_fwd], ici_recv.at[j * n_ici + n_fwd],
                              (px, py, c)))
        for cp in sends:
            cp.start()
        for j, (px, py) in enumerate(chips):
            landed = 2 * px + py
            for t, (a, k) in enumerate(parts):
                here = part(slot(a, landed), a, k, c)
                copy(here, here, ici_send.at[j * n_ici + t], ici_recv.at[j * n_ici + t], (px, py, c)).wait_recv()
                passed = copy(here, here, fwd_send.at[j * n_fwd + t], fwd_recv.at[j * n_fwd + t], (x, y, 1 - c))
                passed.start()
                sends.append(passed)
            copy(srcs[4], slot(4, landed), ici_send.at[j * n_ici + n_fwd], ici_recv.at[j * n_ici + n_fwd],
                 (px, py, c)).wait_recv()
        for j, (px, py) in enumerate(chips):
            for t, (a, k) in enumerate(parts):
                there = part(slot(a, 2 * px + py), a, k, 1 - c)
                copy(there, there, fwd_send.at[j * n_fwd + t], fwd_recv.at[j * n_fwd + t], (x, y, 1 - c)).wait_recv()
        for cp in sends:
            cp.wait_send()
        for cp in local:
            cp.wait()

    return pl.pallas_call(
        body, in_specs=[ANY] * n_arr, out_specs=[ANY] * n_arr,
        out_shape=[jax.ShapeDtypeStruct((D_MODEL, IN_WIDTH), BF16)] + [jax.ShapeDtypeStruct((D_MODEL, D_MODEL), BF16)] * 3
        + [jax.ShapeDtypeStruct((CONV_ROWS, D_MODEL), F32)],
        scratch_shapes=[pltpu.SemaphoreType.DMA((3 * n_ici,)), pltpu.SemaphoreType.DMA((3 * n_ici,)),
                        pltpu.SemaphoreType.DMA((3 * n_fwd,)), pltpu.SemaphoreType.DMA((3 * n_fwd,)),
                        pltpu.SemaphoreType.DMA((n_arr,))],
        name="gather_weights")(w_in, w_sb, w_cv, w_out, conv_w)


def _row_chunks(src, dst, n, send_sem, recv_sem, device):
    rows = src.shape[0] // n

    def copy(s, d):
        return pltpu.make_async_remote_copy(src_ref=s, dst_ref=d, send_sem=send_sem, recv_sem=recv_sem,
                                            device_id=device, device_id_type=MESH)

    chunks = [copy(src.at[pl.ds(k * rows, rows)], dst.at[pl.ds(k * rows, rows)]) for k in range(n)]
    return chunks, copy(src, dst)


def _arrival(dst, send_sem, recv_sem, device):
    return pltpu.make_async_remote_copy(src_ref=dst, dst_ref=dst, send_sem=send_sem, recv_sem=recv_sem,
                                        device_id=device, device_id_type=MESH)


def _pair_exchange(d_in, d_sb, d_cv, d_out):
    def body(i0, i1, i2, i3, r0, r1, r2, r3, send_sems, recv_sems):
        x, y, c, _ = _place()
        srcs = [i0, i1, i2, i3]
        theirs = [r0, r1, r2, r3]
        sibling = (x, y, 1 - c)
        start = pl.multiple_of((1 - c) * HALF, 128)
        views = [srcs[0].at[pl.ds(start, HALF), :]] + [srcs[a].at[:, pl.ds(start, HALF)] for a in range(1, 4)]
        pieces, wholes = [], []
        for a in range(4):
            cps, whole = _row_chunks(views[a], theirs[a], N_CHUNKS, send_sems.at[a], recv_sems.at[a], sibling)
            pieces += cps
            wholes.append(whole)
        for cp in pieces:
            cp.start()
        for cp in wholes:
            cp.wait_recv()
        for cp in wholes:
            cp.wait_send()

    shapes = [jax.ShapeDtypeStruct((HALF, IN_WIDTH), F32)] + [jax.ShapeDtypeStruct((D_MODEL, HALF), F32)] * 3
    return pl.pallas_call(
        body, in_specs=[ANY] * 4, out_specs=[ANY] * 4, out_shape=shapes,
        scratch_shapes=[pltpu.SemaphoreType.DMA((4,)), pltpu.SemaphoreType.DMA((4,))],
        name="pair_exchange")(d_in, d_sb, d_cv, d_out)


def _chip_exchange_copies(srcs, theirs, send_sems, recv_sems):
    _, _, c, chips = _place()

    def shard(a, px, py):
        chip = 2 * px + py
        if a == 0:
            return srcs[a].at[:, pl.ds(pl.multiple_of(chip * SHARD_IN, 128), SHARD_IN)]
        return srcs[a].at[pl.ds(pl.multiple_of(chip * SHARD_SQ, 16), SHARD_SQ), :]

    pieces, wholes = [], []
    for j, (px, py) in enumerate(chips):
        for a in range(4):
            cps, whole = _row_chunks(shard(a, px, py), theirs[a].at[j], 4, send_sems.at[3 * a + j],
                                     recv_sems.at[3 * a + j], (px, py, c))
            pieces += cps
            wholes.append(whole)
    return pieces, wholes


def _pair_share(f_in, f_sb, f_cv, f_out):
    def body(i0, i1, i2, i3, o0, o1, o2, o3, send_sems, recv_sems):
        del i0, i1, i2, i3
        x, y, c, _ = _place()
        dsts = [o0, o1, o2, o3]
        sibling = (x, y, 1 - c)

        def half(a, which):
            start = pl.multiple_of(which * HALF, 128)
            if a == 0:
                return dsts[a].at[pl.ds(start, HALF), :]
            return dsts[a].at[:, pl.ds(start, HALF)]

        pieces, sent, arrived = [], [], []
        for a in range(4):
            cps, whole = _row_chunks(half(a, c), half(a, c), N_CHUNKS, send_sems.at[a], recv_sems.at[a], sibling)
            pieces += cps
            sent.append(whole)
            arrived.append(_arrival(half(a, 1 - c), send_sems.at[a], recv_sems.at[a], sibling))
        for cp in pieces:
            cp.start()
        for cp in arrived:
            cp.wait_recv()
        for cp in sent:
            cp.wait_send()

    return pl.pallas_call(
        body, in_specs=[ANY] * 4, out_specs=[ANY] * 4,
        out_shape=[jax.ShapeDtypeStruct((D_MODEL, SHARD_IN), F32)] + [jax.ShapeDtypeStruct((SHARD_SQ, D_MODEL), F32)] * 3,
        input_output_aliases={0: 0, 1: 1, 2: 2, 3: 3},
        scratch_shapes=[pltpu.SemaphoreType.DMA((4,)), pltpu.SemaphoreType.DMA((4,))],
        name="pair_share")(f_in, f_sb, f_cv, f_out)


def _small_allreduce(small):
    def body(s_ref, o_ref, slots, send_sems, recv_sems):
        x, y, c, _ = _place()
        me = 4 * x + 2 * y + c
        slots[me] = s_ref[...]
        sends = []
        for k in range(1, N_DEV):
            px, py, pc = (x + (k >> 2)) % 2, (y + ((k >> 1) & 1)) % 2, (c + (k & 1)) % 2
            sends.append(pltpu.make_async_remote_copy(
                src_ref=s_ref, dst_ref=slots.at[me], send_sem=send_sems.at[k - 1], recv_sem=recv_sems.at[k - 1],
                device_id=(px, py, pc), device_id_type=MESH))
        for cp in sends:
            cp.start()
        for k in range(1, N_DEV):
            px, py, pc = (x + (k >> 2)) % 2, (y + ((k >> 1) & 1)) % 2, (c + (k & 1)) % 2
            pltpu.make_async_remote_copy(
                src_ref=s_ref, dst_ref=slots.at[4 * px + 2 * py + pc], send_sem=send_sems.at[k - 1],
                recv_sem=recv_sems.at[k - 1], device_id=(px, py, pc), device_id_type=MESH).wait_recv()
        for cp in sends:
            cp.wait_send()
        total = slots[0]
        for d in range(1, N_DEV):
            total = total + slots[d]
        o_ref[...] = total
        o_ref[7:8, :] = jnp.zeros((1, D_MODEL), F32) + jnp.sum(total[7:8, :], axis=1, keepdims=True)

    return pl.pallas_call(
        body, in_specs=[pl.BlockSpec(memory_space=pltpu.VMEM)], out_specs=pl.BlockSpec(memory_space=pltpu.VMEM),
        out_shape=jax.ShapeDtypeStruct((SMALL_ROWS, D_MODEL), F32),
        scratch_shapes=[pltpu.VMEM((N_DEV, SMALL_ROWS, D_MODEL), F32), pltpu.SemaphoreType.DMA((N_DEV - 1,)),
                        pltpu.SemaphoreType.DMA((N_DEV - 1,))],
        name="small_allreduce")(small)


SUM_BLOCKS = 8


def _pair_sum(place, full, theirs, by_rows, name):
    rows, cols = theirs.shape
    tr = rows // SUM_BLOCKS

    def body(p_ref, a_ref, b_ref, o_ref):
        del p_ref
        o_ref[...] = (a_ref[...] + b_ref[...]).astype(BF16)

    mine = (lambda i, p: (p[1] * SUM_BLOCKS + i, 0)) if by_rows else (lambda i, p: (i, p[1]))
    spec = pl.BlockSpec((tr, cols), lambda i, p: (i, 0))
    return pl.pallas_call(
        body, out_shape=jax.ShapeDtypeStruct(theirs.shape, BF16),
        grid_spec=pltpu.PrefetchScalarGridSpec(num_scalar_prefetch=1, grid=(SUM_BLOCKS,),
                                               in_specs=[pl.BlockSpec((tr, cols), mine), spec], out_specs=spec),
        compiler_params=_params("parallel"), name=name)(place, full, theirs)


def _chip_sum(place, pair, got, by_rows, name):
    _, rows, cols = got.shape
    tr = rows // SUM_BLOCKS

    def body(p_ref, a_ref, g_ref, o_ref):
        del p_ref
        o_ref[...] = ((a_ref[...].astype(F32) + g_ref[0].astype(F32)) + g_ref[1].astype(F32)) + g_ref[2].astype(F32)

    if by_rows:
        own = lambda i, p: (i, p[0])
        out = lambda i, p: (p[1] * SUM_BLOCKS + i, 0)
        full = (2 * rows, cols)
    else:
        own = lambda i, p: (p[0] * SUM_BLOCKS + i, 0)
        out = lambda i, p: (i, p[1])
        full = (rows, 2 * cols)
    return pl.pallas_call(
        body, out_shape=jax.ShapeDtypeStruct(full, F32),
        grid_spec=pltpu.PrefetchScalarGridSpec(
            num_scalar_prefetch=1, grid=(SUM_BLOCKS,),
            in_specs=[pl.BlockSpec((tr, cols), own), pl.BlockSpec((3, tr, cols), lambda i, p: (0, i, 0))],
            out_specs=pl.BlockSpec((tr, cols), out)),
        compiler_params=_params("parallel"), name=name)(place, pair, got)


def _adamw(w, g, m, v, name):
    rows, cols = w.shape
    tr = rows // 4 if rows % 32 == 0 else rows
    c1 = 1.0 - ADAM_B1 ** ADAM_STEP
    c2 = 1.0 - ADAM_B2 ** ADAM_STEP

    def body(w_ref, g_ref, m_ref, v_ref, d_ref, mo_ref, vo_ref):
        g = g_ref[...]
        mn = ADAM_B1 * m_ref[...] + (1.0 - ADAM_B1) * g
        vn = ADAM_B2 * v_ref[...] + (1.0 - ADAM_B2) * (g * g)
        mo_ref[...] = mn
        vo_ref[...] = vn
        d_ref[...] = -ADAM_LR * ((mn / c1) / (jnp.sqrt(vn / c2) + ADAM_EPS) + ADAM_WD * w_ref[...])

    spec = pl.BlockSpec((tr, cols), lambda i: (i, 0))
    shape = jax.ShapeDtypeStruct(w.shape, F32)
    return pl.pallas_call(body, grid=(rows // tr,), in_specs=[spec] * 4, out_specs=[spec] * 3,
                          out_shape=[shape] * 3, compiler_params=_params("parallel"), name=name)(w, g, m, v)


GRAD_NAMES = ("in", "sb", "cv", "out")


def _place_scalars():
    return jnp.stack([2 * lax.axis_index("x") + lax.axis_index("y"), lax.axis_index("c")]).astype(jnp.int32)


def _pair_sums(grads):
    place = _place_scalars()
    theirs = _pair_exchange(*grads)
    return [_pair_sum(place, grads[a], theirs[a], a == 0, "pair_sum_" + GRAD_NAMES[a]) for a in range(4)]


def _finish_weight_grads(pair, got):
    place = _place_scalars()
    done = [_chip_sum(place, pair[a], got[a], a == 0, "chip_sum_" + GRAD_NAMES[a]) for a in range(4)]
    return _pair_share(*done)


def kernel(x, ln_in_g, ln_in_b, w_in, w_sb_proj, conv_w, conv_b, conv_ln_g, conv_ln_b, w_cv_proj, w_out, ln_post_g, ln_post_b, loss_target, m_ln_in_g, m_ln_in_b, m_w_in, m_w_sb_proj, m_conv_w, m_conv_b, m_conv_ln_g, m_conv_ln_b, m_w_cv_proj, m_w_out, m_ln_post_g, m_ln_post_b, v_ln_in_g, v_ln_in_b, v_w_in, v_w_sb_proj, v_conv_w, v_conv_b, v_conv_ln_g, v_conv_ln_b, v_w_cv_proj, v_w_out, v_ln_post_g, v_ln_post_b):
    nb, seq, _ = x.shape
    t = nb * seq
    vec_names = ("ln_in_g", "ln_in_b", "conv_b", "conv_ln_g", "conv_ln_b", "ln_post_g", "ln_post_b")
    vec_w = dict(zip(vec_names, (ln_in_g, ln_in_b, conv_b, conv_ln_g, conv_ln_b, ln_post_g, ln_post_b)))
    vec_m = dict(zip(vec_names, (m_ln_in_g, m_ln_in_b, m_conv_b, m_conv_ln_g, m_conv_ln_b, m_ln_post_g, m_ln_post_b)))
    vec_v = dict(zip(vec_names, (v_ln_in_g, v_ln_in_b, v_conv_b, v_conv_ln_g, v_conv_ln_b, v_ln_post_g, v_ln_post_b)))
    vecs = {k: a.reshape(1, D_MODEL) for k, a in vec_w.items()}

    pad_taps = lambda a: jnp.pad(a.reshape(CONV_K, SHARD_SQ), ((0, CONV_ROWS - CONV_K), (0, 0)))
    gathered = _gather_weights(w_in[0].astype(BF16), w_sb_proj[0].astype(BF16), w_cv_proj[0].astype(BF16),
                               w_out[0].astype(BF16), pad_taps(conv_w))
    full_in, full_sb, full_cv, full_out, full_conv = gathered

    x2 = x.reshape(t, D_MODEL)
    dhs, dr, big, small = _forward_backward(x2, loss_target.reshape(t, D_MODEL), nb, seq,
                                            full_in, full_sb, full_cv, full_out, full_conv, vecs)
    pair = _pair_sums(big)
    grad_x, st_in, got = _in_proj_bwd(dhs, full_in, dr, x2, vecs["ln_in_g"], pair)
    g_in, g_sb, g_cv, g_out = _finish_weight_grads(pair, got)
    small = _small_allreduce(jnp.concatenate([st_in[0:2], small], axis=0))

    chip = 2 * lax.axis_index("x") + lax.axis_index("y")
    g_conv = lax.dynamic_slice(small, (8, chip * SHARD_SQ), (CONV_ROWS, SHARD_SQ))
    loss = small[7, 0]

    grads, deltas, new_m, new_v = {}, {}, {}, {}
    stack = lambda d: jnp.concatenate([d[k].reshape(1, D_MODEL) for k in vec_names] + [jnp.zeros((1, D_MODEL), F32)])
    vd, vm, vv = _adamw(stack(vec_w), small[0:8], stack(vec_m), stack(vec_v), "adamw_vectors")
    for r, k in enumerate(vec_names):
        shape = vec_w[k].shape
        grads[k] = small[r].reshape(shape)
        deltas[k], new_m[k], new_v[k] = vd[r].reshape(shape), vm[r].reshape(shape), vv[r].reshape(shape)

    big_w = {"w_in": (w_in, g_in, m_w_in, v_w_in), "w_sb_proj": (w_sb_proj, g_sb, m_w_sb_proj, v_w_sb_proj),
             "w_cv_proj": (w_cv_proj, g_cv, m_w_cv_proj, v_w_cv_proj), "w_out": (w_out, g_out, m_w_out, v_w_out)}
    for k, (w, g, m, v) in big_w.items():
        d, mn, vn = _adamw(w[0], g, m[0], v[0], "adamw_" + k)
        grads[k], deltas[k], new_m[k], new_v[k] = g[None], d[None], mn[None], vn[None]
    d, mn, vn = _adamw(pad_taps(conv_w), g_conv, pad_taps(m_conv_w), pad_taps(v_conv_w), "adamw_conv_w")
    grads["conv_w"] = g_conv[None, :CONV_K]
    deltas["conv_w"], new_m["conv_w"], new_v["conv_w"] = d[None, :CONV_K], mn[None, :CONV_K], vn[None, :CONV_K]

    order = ("ln_in_g", "ln_in_b", "w_in", "w_sb_proj", "conv_w", "conv_b", "conv_ln_g", "conv_ln_b",
             "w_cv_proj", "w_out", "ln_post_g", "ln_post_b")
    return (loss, grad_x.reshape(x.shape), *[grads[k] for k in order], *[deltas[k] for k in order],
            *[new_m[k] for k in order], *[new_v[k] for k in order])
```

```python
import functools

import jax
import jax.numpy as jnp
from jax import lax
from jax.experimental import pallas as pl
from jax.experimental.pallas import tpu as pltpu

F32 = jnp.float32
BF16 = jnp.bfloat16

D_MODEL = 1024
N_HEADS = 16
HEAD_DIM = 64
HEAD_GROUP = 4
GROUP_W = HEAD_GROUP * HEAD_DIM
N_GROUPS = N_HEADS // HEAD_GROUP
N_PIECES = 9
IN_WIDTH = N_PIECES * D_MODEL
Q_BLOCK = 256
CONV_K = 31
CONV_ROWS = 32
HALO = 32
LN_EPS = 1e-5
ALPHA = 2.0 ** 0.25
Q_SCALE = 0.125
N_CHIPS = 4
N_DEV = 8
SHARD_IN = IN_WIDTH // N_CHIPS
SHARD_SQ = D_MODEL // N_CHIPS
HALF = D_MODEL // 2
SMALL_ROWS = 40
N_CHUNKS = 16

ADAM_LR = 0.001
ADAM_B1 = 0.9
ADAM_B2 = 0.999
ADAM_EPS = 1e-08
ADAM_WD = 0.01
ADAM_STEP = 10

MESH = pl.DeviceIdType.MESH
ANY = pl.BlockSpec(memory_space=pl.ANY)
VMEM_LIMIT = 60 * 1024 * 1024

NT = (((1,), (1,)), ((), ()))
TN = (((0,), (0,)), ((), ()))


def _sigmoid(x):
    return 1.0 / (1.0 + jnp.exp(-x))


def _dot(a, b):
    return jnp.dot(a, b, preferred_element_type=F32)


def _dot_nt(a, b):
    return lax.dot_general(a, b, NT, preferred_element_type=F32)


def _dot_tn(a, b):
    return lax.dot_general(a, b, TN, preferred_element_type=F32)


def _split_bf16(x):
    hi = x.astype(BF16)
    lo = (x - hi.astype(F32)).astype(BF16)
    return hi, lo


def _ln_stats(x):
    mu = jnp.mean(x, axis=-1, keepdims=True)
    xc = x - mu
    var = jnp.mean(xc * xc, axis=-1, keepdims=True)
    rstd = lax.rsqrt(var + LN_EPS)
    return xc * rstd, rstd


def _ln_bwd(dy, xhat, rstd, g):
    dxh = dy * g
    m1 = jnp.mean(dxh, axis=-1, keepdims=True)
    m2 = jnp.mean(dxh * xhat, axis=-1, keepdims=True)
    return rstd * (dxh - m1 - xhat * m2)


def _params(*sem):
    return pltpu.CompilerParams(dimension_semantics=sem, vmem_limit_bytes=VMEM_LIMIT)


def _ln_in(x, g, b):
    t = x.shape[0]
    tm = min(512, t)

    def body(x_ref, g_ref, b_ref, hf_ref, hb_ref):
        xhat, _ = _ln_stats(x_ref[...])
        y = xhat * g_ref[...] + b_ref[...]
        hf_ref[...] = y
        hb_ref[...] = y.astype(BF16)

    row = pl.BlockSpec((tm, D_MODEL), lambda i: (i, 0))
    vec = pl.BlockSpec((1, D_MODEL), lambda i: (0, 0))
    return pl.pallas_call(
        body, grid=(t // tm,), in_specs=[row, vec, vec], out_specs=[row, row],
        out_shape=[jax.ShapeDtypeStruct((t, D_MODEL), F32), jax.ShapeDtypeStruct((t, D_MODEL), BF16)],
        compiler_params=_params("parallel"), name="ln_in")(x, g, b)


def _in_proj(h, w, col0, ncol, out_dtype, scale_first, name):
    t = h.shape[0]
    tm = min(1024, t)

    def body(h_ref, w_ref, o_ref):
        res = _dot(h_ref[...], w_ref[...])
        if scale_first:
            res = res * jnp.where(pl.program_id(0) == 0, Q_SCALE, 1.0)
        o_ref[...] = res.astype(out_dtype)

    return pl.pallas_call(
        body, grid=(ncol, t // tm),
        in_specs=[pl.BlockSpec((tm, D_MODEL), lambda j, i: (i, 0)),
                  pl.BlockSpec((D_MODEL, D_MODEL), lambda j, i: (0, j + col0))],
        out_specs=pl.BlockSpec((tm, D_MODEL), lambda j, i: (i, j)),
        out_shape=jax.ShapeDtypeStruct((t, ncol * D_MODEL), out_dtype),
        compiler_params=_params("parallel", "parallel"), name=name)(h, w)


def _sb_block(l, diag, tri, carry):
    lnb = jnp.minimum(-l, 0.0) - jnp.log(1.0 + jnp.exp(-jnp.abs(l)))
    if diag is not None:
        lnb = jnp.where(diag, lnb, 0.0)
    hi, lo = _split_bf16(lnb)
    suffix = _dot(hi, tri) + _dot(lo, tri)
    a = jnp.exp(l + suffix + carry)
    if diag is not None:
        a = jnp.where(diag, a, 0.0)
    return a, lnb, suffix[:, 0:1]


def _attn_consts():
    head = lax.broadcasted_iota(jnp.int32, (Q_BLOCK, GROUP_W), 1) // HEAD_DIM
    row = lax.broadcasted_iota(jnp.int32, (Q_BLOCK, Q_BLOCK), 0)
    col = lax.broadcasted_iota(jnp.int32, (Q_BLOCK, Q_BLOCK), 1)
    return head, col < row, (row >= col).astype(BF16), (row <= col).astype(BF16)


def _split_heads(t, head):
    zero = jnp.zeros_like(t)
    return [jnp.where(head == h, t, zero) for h in range(HEAD_GROUP)]


def _attn_fwd(qkv, nb, seq):
    nq = seq // Q_BLOCK

    def body(q_ref, k_ref, v_ref, o_ref):
        qi = pl.program_id(2)
        head, diag, tri, _ = _attn_consts()
        qh = _split_heads(q_ref[...], head)
        o_ref[...] = jnp.zeros_like(o_ref)

        def block(ks, carries, masked):
            kblk = k_ref[pl.ds(ks, Q_BLOCK), :]
            vstack = jnp.concatenate(_split_heads(v_ref[pl.ds(ks, Q_BLOCK), :], head), axis=0)
            weights, out = [], []
            for h in range(HEAD_GROUP):
                a, _, blk_sum = _sb_block(_dot_nt(qh[h], kblk), diag if masked else None, tri, carries[h])
                weights.append(a.astype(BF16))
                out.append(carries[h] + blk_sum)
            o_ref[...] += _dot(jnp.concatenate(weights, axis=1), vstack)
            return tuple(out)

        zero_col = jnp.zeros((Q_BLOCK, 1), F32)
        carries = block(pl.multiple_of(qi * Q_BLOCK, Q_BLOCK), (zero_col,) * HEAD_GROUP, True)
        lax.fori_loop(1, qi + 1, lambda jj, cs: block(pl.multiple_of((qi - jj) * Q_BLOCK, Q_BLOCK), cs, False), carries)

    return pl.pallas_call(
        body, grid=(nb, N_GROUPS, nq),
        in_specs=[pl.BlockSpec((Q_BLOCK, GROUP_W), lambda b, g, i: (b * nq + i, g)),
                  pl.BlockSpec((seq, GROUP_W), lambda b, g, i: (b, N_GROUPS + g)),
                  pl.BlockSpec((seq, GROUP_W), lambda b, g, i: (b, 2 * N_GROUPS + g))],
        out_specs=pl.BlockSpec((Q_BLOCK, GROUP_W), lambda b, g, i: (b * nq + i, g)),
        out_shape=jax.ShapeDtypeStruct((nb * seq, D_MODEL), F32),
        compiler_params=_params("parallel", "parallel", "arbitrary"), name="attn_fwd")(qkv, qkv, qkv)


def _attn_bwd(qkv, do, nb, seq):
    nq = seq // Q_BLOCK

    def body(q_ref, k_ref, v_ref, do_ref, dq_ref, dk_ref, dv_ref, g_s, nb_s, dq_acc, dk_acc, dv_acc):
        qi = pl.program_id(2)

        @pl.when(qi == 0)
        def _():
            dk_acc[...] = jnp.zeros_like(dk_acc)
            dv_acc[...] = jnp.zeros_like(dv_acc)

        head, diag, tri_suffix, tri_prefix = _attn_consts()
        qh = _split_heads(q_ref[...], head)
        doh = _split_heads(do_ref[...], head)
        qstack = jnp.concatenate(qh, axis=0)
        dostack = jnp.concatenate(doh, axis=0)
        dq_acc[...] = jnp.zeros_like(dq_acc)

        def block_a(kb, carries, masked):
            ks = pl.multiple_of(kb * Q_BLOCK, Q_BLOCK)
            kblk = k_ref[pl.ds(ks, Q_BLOCK), :]
            vblk = v_ref[pl.ds(ks, Q_BLOCK), :]
            weights, out = [], []
            for h in range(HEAD_GROUP):
                a, lnb, blk_sum = _sb_block(_dot_nt(qh[h], kblk), diag if masked else None, tri_suffix, carries[h])
                g_s[h, kb] = a * _dot_nt(doh[h], vblk)
                nb_s[h, kb] = jnp.exp(lnb)
                weights.append(a.astype(BF16))
                out.append(carries[h] + blk_sum)
            dv_acc[pl.ds(ks, Q_BLOCK), :] += _dot_tn(jnp.concatenate(weights, axis=0), dostack)
            return tuple(out)

        zero_col = jnp.zeros((Q_BLOCK, 1), F32)
        carries = block_a(qi, (zero_col,) * HEAD_GROUP, True)
        lax.fori_loop(1, qi + 1, lambda jj, cs: block_a(qi - jj, cs, False), carries)

        def block_b(kb, prefixes, masked):
            ks = pl.multiple_of(kb * Q_BLOCK, Q_BLOCK)
            kstack = jnp.concatenate(_split_heads(k_ref[pl.ds(ks, Q_BLOCK), :], head), axis=0)
            dls, out = [], []
            for h in range(HEAD_GROUP):
                g = g_s[h, kb]
                not_beta = nb_s[h, kb]
                hi, lo = _split_bf16(g)
                incl = _dot(hi, tri_prefix) + _dot(lo, tri_prefix)
                before = incl - g + prefixes[h]
                dl = not_beta * (g + before) - before
                if masked:
                    dl = jnp.where(diag, dl, 0.0)
                dls.append(dl.astype(BF16))
                out.append(prefixes[h] + incl[:, Q_BLOCK - 1:Q_BLOCK])
            dq_acc[...] += _dot(jnp.concatenate(dls, axis=1), kstack)
            dk_acc[pl.ds(ks, Q_BLOCK), :] += _dot_tn(jnp.concatenate(dls, axis=0), qstack)
            return tuple(out)

        prefixes = lax.fori_loop(0, qi, lambda kb, ps: block_b(kb, ps, False), (zero_col,) * HEAD_GROUP)
        block_b(qi, prefixes, True)
        dq_ref[...] = (dq_acc[...] * Q_SCALE).astype(BF16)

        @pl.when(qi == nq - 1)
        def _():
            dk_ref[...] = dk_acc[...].astype(BF16)
            dv_ref[...] = dv_acc[...].astype(BF16)

    t = nb * seq
    qspec = pl.BlockSpec((Q_BLOCK, GROUP_W), lambda b, g, i: (b * nq + i, g))
    kvout = pl.BlockSpec((seq, GROUP_W), lambda b, g, i: (b, g))
    strip = pltpu.VMEM((HEAD_GROUP, nq, Q_BLOCK, Q_BLOCK), F32)
    return pl.pallas_call(
        body, grid=(nb, N_GROUPS, nq),
        in_specs=[qspec,
                  pl.BlockSpec((seq, GROUP_W), lambda b, g, i: (b, N_GROUPS + g)),
                  pl.BlockSpec((seq, GROUP_W), lambda b, g, i: (b, 2 * N_GROUPS + g)),
                  qspec],
        out_specs=[qspec, kvout, kvout],
        out_shape=[jax.ShapeDtypeStruct((t, D_MODEL), BF16)] * 3,
        scratch_shapes=[strip, strip, pltpu.VMEM((Q_BLOCK, GROUP_W), F32),
                        pltpu.VMEM((seq, GROUP_W), F32), pltpu.VMEM((seq, GROUP_W), F32)],
        compiler_params=_params("parallel", "parallel", "arbitrary"), name="attn_bwd")(qkv, qkv, qkv, do)


def _conv_block_rows(seq):
    return min(256, seq)


SUBLANES = 8
CONV_SUB = 32
CONV_GROUPS = CONV_SUB // SUBLANES


def _fill_shifts(sh_ref, n):
    for r in range(1, SUBLANES):
        sh_ref[r, 0:n, :] = sh_ref[0, pl.ds(r, n), :]


def _shifted(sh_ref, start, cs):
    r = start % SUBLANES
    return sh_ref[r, start - r:start - r + CONV_SUB, cs].reshape(CONV_GROUPS, SUBLANES, -1)


def _tap(w_ref, k, cs):
    row = w_ref[k:k + 1, cs]
    return jnp.broadcast_to(row, (SUBLANES, row.shape[1]))[None]


def _tree_sum(x):
    parts = [x[i] for i in range(x.shape[0])]
    while len(parts) > 1:
        parts = [parts[i] + parts[i + 1] for i in range(0, len(parts), 2)]
    return parts[0]


def _conv_fwd(hrest, conv_w, conv_b, ln_g, ln_b, nb, seq):
    ts = _conv_block_rows(seq)
    nblk = seq // ts
    lanes = 128

    def body(cv_ref, cg_ref, cvh_ref, cgh_ref, z_ref, w_ref, cb_ref, g_ref, b_ref, u1_ref, cvin_ref, ush):
        i = pl.program_id(1)
        halo = cvh_ref[...] * _sigmoid(cgh_ref[...])
        ush[0, 0:HALO, :] = jnp.where(i > 0, halo, 0.0)
        ush[0, HALO:HALO + ts, :] = cv_ref[...] * _sigmoid(cg_ref[...])
        _fill_shifts(ush, HALO + ts - SUBLANES)
        for cc in range(D_MODEL // lanes):
            cs = slice(cc * lanes, (cc + 1) * lanes)
            for r0 in range(0, ts, CONV_SUB):
                acc = jnp.zeros((CONV_GROUPS, SUBLANES, lanes), F32) + _tap(cb_ref, 0, cs)
                for k in range(CONV_K):
                    acc = acc + _tap(w_ref, k, cs) * _shifted(ush, HALO - CONV_K + 1 + k + r0, cs)
                u1_ref[r0:r0 + CONV_SUB, cs] = acc.reshape(CONV_SUB, lanes)
        xhat, _ = _ln_stats(u1_ref[...])
        u2 = xhat * g_ref[...] + b_ref[...]
        z = z_ref[...]
        cvin_ref[...] = (u2 * _sigmoid(u2) * z * _sigmoid(z)).astype(BF16)

    def main(colblk):
        return pl.BlockSpec((ts, D_MODEL), lambda b, i: (b * nblk + i, colblk))

    def halo(colblk):
        return pl.BlockSpec((HALO, D_MODEL),
                            lambda b, i: (jnp.maximum((b * seq + i * ts) // HALO - 1, 0), colblk))

    vec = pl.BlockSpec((1, D_MODEL), lambda b, i: (0, 0))
    t = nb * seq
    return pl.pallas_call(
        body, grid=(nb, nblk),
        in_specs=[main(1), main(2), halo(1), halo(2), main(3),
                  pl.BlockSpec((CONV_ROWS, D_MODEL), lambda b, i: (0, 0)), vec, vec, vec],
        out_specs=[main(0), main(0)],
        out_shape=[jax.ShapeDtypeStruct((t, D_MODEL), F32), jax.ShapeDtypeStruct((t, D_MODEL), BF16)],
        scratch_shapes=[pltpu.VMEM((SUBLANES, HALO + ts, D_MODEL), F32)],
        compiler_params=_params("parallel", "parallel"), name="conv_fwd")(
            hrest, hrest, hrest, hrest, hrest, conv_w, conv_b, ln_g, ln_b)


def _tail(o, hrest, cvin, h0, tgt, w_sb, w_cv, w_out, ln_g, ln_b):
    t = o.shape[0]
    tm = min(256, t)

    def body(o_ref, z_ref, gs_ref, gc_ref, cvin_ref, h0_ref, tgt_ref, wsb_ref, wcv_ref, wout_ref, g_ref, b_ref,
             dr_ref, drb_ref, a_ref, mg_ref, dysb_ref, dycv_ref, do_ref, dz_ref, dgs_ref, dgc_ref, dcvin_ref,
             st_ref):
        @pl.when(pl.program_id(0) == 0)
        def _():
            st_ref[...] = jnp.zeros_like(st_ref)

        o = o_ref[...]
        z = z_ref[...]
        sz = _sigmoid(z)
        a = (o * z * sz).astype(BF16)
        a_ref[...] = a
        y_sb = _dot(a, wsb_ref[...])
        y_cv = _dot(cvin_ref[...], wcv_ref[...])
        s_sb = _sigmoid(gs_ref[...])
        s_cv = _sigmoid(gc_ref[...])
        merged = (s_sb * y_sb + s_cv * y_cv).astype(BF16)
        mg_ref[...] = merged
        r = ALPHA * h0_ref[...] + _dot(merged, wout_ref[...])
        xhat, rstd = _ln_stats(r)
        g = g_ref[...]
        err = xhat * g + b_ref[...] - tgt_ref[...]
        dy = err * (1.0 / D_MODEL)
        st_ref[0:1, :] += jnp.sum(dy * xhat, axis=0, keepdims=True)
        st_ref[1:2, :] += jnp.sum(dy, axis=0, keepdims=True)
        st_ref[2:3, :] += (0.5 / D_MODEL) * jnp.sum(err * err, axis=0, keepdims=True)
        dr = _ln_bwd(dy, xhat, rstd, g)
        dr_ref[...] = dr
        drb = dr.astype(BF16)
        drb_ref[...] = drb
        dm = _dot_nt(drb, wout_ref[...])
        dy_sb = (dm * s_sb).astype(BF16)
        dy_cv = (dm * s_cv).astype(BF16)
        dysb_ref[...] = dy_sb
        dycv_ref[...] = dy_cv
        dgs_ref[...] = (dm * y_sb * s_sb * (1.0 - s_sb)).astype(BF16)
        dgc_ref[...] = (dm * y_cv * s_cv * (1.0 - s_cv)).astype(BF16)
        da = _dot_nt(dy_sb, wsb_ref[...])
        dcvin_ref[...] = _dot_nt(dy_cv, wcv_ref[...])
        do_ref[...] = (da * z * sz).astype(BF16)
        dz_ref[...] = (da * o * sz * (1.0 + z * (1.0 - sz))).astype(BF16)

    def tok(colblk=0):
        return pl.BlockSpec((tm, D_MODEL), lambda i: (i, colblk))

    wspec = pl.BlockSpec((D_MODEL, D_MODEL), lambda i: (0, 0), pipeline_mode=pl.Buffered(1))
    vec = pl.BlockSpec((1, D_MODEL), lambda i: (0, 0))
    bf = jax.ShapeDtypeStruct((t, D_MODEL), BF16)
    f32 = jax.ShapeDtypeStruct((t, D_MODEL), F32)
    return pl.pallas_call(
        body, grid=(t // tm,),
        in_specs=[tok(), tok(0), tok(4), tok(5), tok(), tok(), tok(), wspec, wspec, wspec, vec, vec],
        out_specs=[tok()] * 11 + [pl.BlockSpec((8, D_MODEL), lambda i: (0, 0))],
        out_shape=[f32] + [bf] * 9 + [f32, jax.ShapeDtypeStruct((8, D_MODEL), F32)],
        compiler_params=_params("arbitrary"), name="tail")(
            o, hrest, hrest, hrest, cvin, h0, tgt, w_sb, w_cv, w_out, ln_g, ln_b)


def _conv_branch_bwd(dcvin, u1, hrest, ln_g, ln_b):
    t = u1.shape[0]
    tm = min(512, t)

    def body(dc_ref, u1_ref, z_ref, g_ref, b_ref, du1_ref, dz_ref, st_ref):
        @pl.when(pl.program_id(0) == 0)
        def _():
            st_ref[...] = jnp.zeros_like(st_ref)

        xhat, rstd = _ln_stats(u1_ref[...])
        g = g_ref[...]
        u2 = xhat * g + b_ref[...]
        s2 = _sigmoid(u2)
        z = z_ref[...]
        sz = _sigmoid(z)
        dc = dc_ref[...]
        dz_ref[...] = (dc * (u2 * s2) * sz * (1.0 + z * (1.0 - sz))).astype(BF16)
        du2 = dc * (z * sz) * s2 * (1.0 + u2 * (1.0 - s2))
        st_ref[0:1, :] += jnp.sum(du2 * xhat, axis=0, keepdims=True)
        st_ref[1:2, :] += jnp.sum(du2, axis=0, keepdims=True)
        du1 = _ln_bwd(du2, xhat, rstd, g)
        du1_ref[...] = du1
        st_ref[2:3, :] += jnp.sum(du1, axis=0, keepdims=True)

    tok = pl.BlockSpec((tm, D_MODEL), lambda i: (i, 0))
    vec = pl.BlockSpec((1, D_MODEL), lambda i: (0, 0))
    return pl.pallas_call(
        body, grid=(t // tm,),
        in_specs=[tok, tok, pl.BlockSpec((tm, D_MODEL), lambda i: (i, 3)), vec, vec],
        out_specs=[tok, tok, pl.BlockSpec((8, D_MODEL), lambda i: (0, 0))],
        out_shape=[jax.ShapeDtypeStruct((t, D_MODEL), F32), jax.ShapeDtypeStruct((t, D_MODEL), BF16),
                   jax.ShapeDtypeStruct((8, D_MODEL), F32)],
        compiler_params=_params("arbitrary"), name="conv_branch_bwd")(dcvin, u1, hrest, ln_g, ln_b)


def _conv_bwd(du1, hrest, conv_w, nb, seq):
    ts = _conv_block_rows(seq)
    nblk = seq // ts
    lanes = 128
    last_halo = nb * seq // HALO - 1

    def body(du_ref, duh_ref, cv_ref, cg_ref, cvh_ref, cgh_ref, w_ref, dcv_ref, dcg_ref, dw_ref, ush, dsh, dw_part):
        b = pl.program_id(0)
        i = pl.program_id(1)

        @pl.when((b == 0) & (i == 0))
        def _():
            dw_part[...] = jnp.zeros_like(dw_part)

        cv = cv_ref[...]
        sg = _sigmoid(cg_ref[...])
        halo = cvh_ref[...] * _sigmoid(cgh_ref[...])
        ush[0, 0:HALO, :] = jnp.where(i > 0, halo, 0.0)
        ush[0, HALO:HALO + ts, :] = cv * sg
        dsh[0, 0:ts, :] = du_ref[...]
        dsh[0, ts:ts + HALO, :] = jnp.where(i < nblk - 1, duh_ref[...], 0.0)
        _fill_shifts(ush, HALO + ts - SUBLANES)
        _fill_shifts(dsh, HALO + ts - SUBLANES)
        for cc in range(D_MODEL // lanes):
            cs = slice(cc * lanes, (cc + 1) * lanes)
            for r0 in range(0, ts, CONV_SUB):
                rows = slice(r0, r0 + CONV_SUB)
                dmain = _shifted(dsh, r0, cs)
                acc = jnp.zeros((CONV_GROUPS, SUBLANES, lanes), F32)
                for k in range(CONV_K):
                    acc = acc + _tap(w_ref, k, cs) * _shifted(dsh, CONV_K - 1 - k + r0, cs)
                    dw_part[k, :, cs] += _tree_sum(dmain * _shifted(ush, HALO - CONV_K + 1 + k + r0, cs))
                acc = acc.reshape(CONV_SUB, lanes)
                dcv_ref[rows, cs] = (acc * sg[rows, cs]).astype(BF16)
                dcg_ref[rows, cs] = (acc * cv[rows, cs] * sg[rows, cs] * (1.0 - sg[rows, cs])).astype(BF16)

        @pl.when((b == nb - 1) & (i == nblk - 1))
        def _():
            for k in range(CONV_ROWS):
                dw_ref[k:k + 1, :] = jnp.sum(dw_part[k], axis=0, keepdims=True)

    def main(colblk):
        return pl.BlockSpec((ts, D_MODEL), lambda b, i: (b * nblk + i, colblk))

    def halo_before(colblk):
        return pl.BlockSpec((HALO, D_MODEL),
                            lambda b, i: (jnp.maximum((b * seq + i * ts) // HALO - 1, 0), colblk))

    halo_after = pl.BlockSpec((HALO, D_MODEL),
                              lambda b, i: (jnp.minimum((b * seq + (i + 1) * ts) // HALO, last_halo), 0))
    t = nb * seq
    return pl.pallas_call(
        body, grid=(nb, nblk),
        in_specs=[main(0), halo_after, main(1), main(2), halo_before(1), halo_before(2),
                  pl.BlockSpec((CONV_ROWS, D_MODEL), lambda b, i: (0, 0))],
        out_specs=[main(0), main(0), pl.BlockSpec((CONV_ROWS, D_MODEL), lambda b, i: (0, 0))],
        out_shape=[jax.ShapeDtypeStruct((t, D_MODEL), BF16), jax.ShapeDtypeStruct((t, D_MODEL), BF16),
                   jax.ShapeDtypeStruct((CONV_ROWS, D_MODEL), F32)],
        scratch_shapes=[pltpu.VMEM((SUBLANES, HALO + ts, D_MODEL), F32), pltpu.VMEM((SUBLANES, ts + HALO, D_MODEL), F32),
                        pltpu.VMEM((CONV_ROWS, SUBLANES, D_MODEL), F32)],
        compiler_params=_params("arbitrary", "arbitrary"), name="conv_bwd")(
            du1, du1, hrest, hrest, hrest, hrest, conv_w)


def _weight_grad(x, dys, name):
    n = len(dys)
    t = x.shape[0]
    tm = min(512, t)

    def body(x_ref, *refs):
        o_ref = refs[n]
        j = pl.program_id(0)

        @pl.when(pl.program_id(1) == 0)
        def _():
            o_ref[...] = jnp.zeros_like(o_ref)

        for p in range(n):
            @pl.when(j == p)
            def _(p=p):
                o_ref[...] += _dot_tn(x_ref[...], refs[p][...])

    def dy_spec(p):
        return pl.BlockSpec((tm, D_MODEL), lambda j, i: (jnp.where(j == p, i, 0), 0))

    return pl.pallas_call(
        body, grid=(n, t // tm),
        in_specs=[pl.BlockSpec((tm, D_MODEL), lambda j, i: (i, 0))] + [dy_spec(p) for p in range(n)],
        out_specs=pl.BlockSpec((D_MODEL, D_MODEL), lambda j, i: (0, j)),
        out_shape=jax.ShapeDtypeStruct((D_MODEL, n * D_MODEL), F32),
        compiler_params=_params("parallel", "arbitrary"), name=name)(x, *dys)


def _in_proj_bwd(dhs, w_in, dr, x, ln_g, pair):
    n = len(dhs)
    t = x.shape[0]
    tm = min(512, t)
    n_i = t // tm

    def body(*refs):
        dh_refs = refs[:n]
        w_ref, dr_ref, x_ref, g_ref = refs[n:n + 4]
        pair_refs = refs[n + 4:n + 8]
        gx_ref, st_ref = refs[n + 8:n + 10]
        got_refs = refs[n + 10:n + 14]
        acc, send_sems, recv_sems = refs[n + 14:]
        i = pl.program_id(0)
        p = pl.program_id(1)
        pieces, wholes = _chip_exchange_copies(pair_refs, got_refs, send_sems, recv_sems)

        @pl.when((i == 0) & (p == 0))
        def _():
            st_ref[...] = jnp.zeros_like(st_ref)
            for cp in pieces:
                cp.start()

        @pl.when((i == n_i - 1) & (p == n - 1))
        def _():
            for cp in wholes:
                cp.wait_recv()
            for cp in wholes:
                cp.wait_send()

        @pl.when(p == 0)
        def _():
            acc[...] = ALPHA * dr_ref[...]

        for q in range(n):
            @pl.when(p == q)
            def _(q=q):
                acc[...] += _dot_nt(dh_refs[q][...], w_ref[...])

        @pl.when(p == n - 1)
        def _():
            xhat, rstd = _ln_stats(x_ref[...])
            dh0 = acc[...]
            st_ref[0:1, :] += jnp.sum(dh0 * xhat, axis=0, keepdims=True)
            st_ref[1:2, :] += jnp.sum(dh0, axis=0, keepdims=True)
            gx_ref[...] = _ln_bwd(dh0, xhat, rstd, g_ref[...])

    tok = pl.BlockSpec((tm, D_MODEL), lambda i, p: (i, 0))
    got = [jax.ShapeDtypeStruct((3, HALF, SHARD_IN), BF16)] + [jax.ShapeDtypeStruct((3, SHARD_SQ, HALF), BF16)] * 3
    res = pl.pallas_call(
        body, grid=(n_i, n),
        in_specs=[tok] * n + [pl.BlockSpec((D_MODEL, D_MODEL), lambda i, p: (0, p)), tok, tok,
                              pl.BlockSpec((1, D_MODEL), lambda i, p: (0, 0))] + [ANY] * 4,
        out_specs=[tok, pl.BlockSpec((8, D_MODEL), lambda i, p: (0, 0))] + [ANY] * 4,
        out_shape=[jax.ShapeDtypeStruct((t, D_MODEL), F32), jax.ShapeDtypeStruct((8, D_MODEL), F32)] + got,
        scratch_shapes=[pltpu.VMEM((tm, D_MODEL), F32), pltpu.SemaphoreType.DMA((12,)), pltpu.SemaphoreType.DMA((12,))],
        compiler_params=_params("arbitrary", "arbitrary"), name="in_proj_bwd")(*dhs, w_in, dr, x, ln_g, *pair)
    return res[0], res[1], res[2:]


def _forward_backward(x, tgt, nb, seq, w_in, w_sb, w_cv, w_out, conv_w, vecs):
    h0, h0b = _ln_in(x, vecs["ln_in_g"], vecs["ln_in_b"])
    qkv = _in_proj(h0b, w_in, 0, 3, BF16, True, "in_proj_qkv")
    hrest = _in_proj(h0b, w_in, 3, 6, F32, False, "in_proj_rest")
    o = _attn_fwd(qkv, nb, seq)
    u1, cvin = _conv_fwd(hrest, conv_w, vecs["conv_b"], vecs["conv_ln_g"], vecs["conv_ln_b"], nb, seq)
    (dr, drb, a, merged, dy_sb, dy_cv, do, dz_sb, dg_sb, dg_cv, dcvin, st_tail) = _tail(
        o, hrest, cvin, h0, tgt, w_sb, w_cv, w_out, vecs["ln_post_g"], vecs["ln_post_b"])
    d_w_sb = _weight_grad(a, [dy_sb], "grad_w_sb")
    d_w_cv = _weight_grad(cvin, [dy_cv], "grad_w_cv")
    d_w_out = _weight_grad(merged, [drb], "grad_w_out")
    du1, dz_cv, st_conv = _conv_branch_bwd(dcvin, u1, hrest, vecs["conv_ln_g"], vecs["conv_ln_b"])
    dc_val, dc_gate, d_conv_w = _conv_bwd(du1, hrest, conv_w, nb, seq)
    dq, dk, dv = _attn_bwd(qkv, do, nb, seq)
    dhs = [dq, dk, dv, dz_sb, dc_val, dc_gate, dz_cv, dg_sb, dg_cv]
    d_w_in = _weight_grad(h0b, dhs, "grad_w_in")
    small = jnp.concatenate([st_conv[2:3], st_conv[0:2], st_tail[0:3], d_conv_w], axis=0)
    return dhs, dr, (d_w_in, d_w_sb, d_w_cv, d_w_out), small


def _place():
    x, y, c = lax.axis_index("x"), lax.axis_index("y"), lax.axis_index("c")
    chips = [(1 - x, y), (x, 1 - y), (1 - x, 1 - y)]
    return x, y, c, chips


def _gather_weights(w_in, w_sb, w_cv, w_out, conv_w):
    n_arr = 5
    n_part = 4
    parts = [(0, k) for k in range(n_part)] + [(1, 0), (2, 0), (3, 0)]
    n_fwd = len(parts)
    n_ici = n_fwd + 1
    part_rows = HALF // n_part
    half_sq = SHARD_SQ // 2

    def body(i0, i1, i2, i3, i4, o0, o1, o2, o3, o4, ici_send, ici_recv, fwd_send, fwd_recv, local_sems):
        x, y, c, chips = _place()
        srcs = [i0, i1, i2, i3, i4]
        dsts = [o0, o1, o2, o3, o4]
        me = 2 * x + y

        def slot(a, chip):
            if a == 0:
                return dsts[a].at[:, pl.ds(pl.multiple_of(chip * SHARD_IN, 128), SHARD_IN)]
            if a == 4:
                return dsts[a].at[:, pl.ds(pl.multiple_of(chip * SHARD_SQ, 128), SHARD_SQ)]
            return dsts[a].at[pl.ds(pl.multiple_of(chip * SHARD_SQ, 16), SHARD_SQ), :]

        def part(ref, a, k, half):
            if a == 0:
                return ref.at[pl.ds(pl.multiple_of(half * HALF + k * part_rows, 16), part_rows), :]
            return ref.at[pl.ds(pl.multiple_of(half * half_sq, 16), half_sq), :]

        def copy(src, dst, send_sem, recv_sem, device):
            return pltpu.make_async_remote_copy(src_ref=src, dst_ref=dst, send_sem=send_sem, recv_sem=recv_sem,
                                                device_id=device, device_id_type=MESH)

        local = [pltpu.make_async_copy(srcs[a], slot(a, me), local_sems.at[a]) for a in range(n_arr)]
        for cp in local:
            cp.start()
        sends = []
        for j, (px, py) in enumerate(chips):
            for t, (a, k) in enumerate(parts):
                sends.append(copy(part(srcs[a], a, k, c), part(slot(a, me), a, k, c),
                                  ici_send.at[j * n_ici + t], ici_recv.at[j * n_ici + t], (px, py, c)))
            sends.append(copy(srcs[4], slot(4, me), ici_send.at[j * n_ici + n_fwd], ici_recv.at[j * n_ici + n_fwd],
                              (px, py, c)))
        for cp in sends:
            cp.start()
        for j, (px, py) in enumerate(chips):
            landed = 2 * px + py
            for t, (a, k) in enumerate(parts):
                here = part(slot(a, landed), a, k, c)
                copy(here, here, ici_send.at[j * n_ici + t], ici_recv.at[j * n_ici + t], (px, py, c)).wait_recv()
                passed = copy(here, here, fwd_send.at[j * n_fwd + t], fwd_recv.at[j * n_fwd + t], (x, y, 1 - c))
                passed.start()
                sends.append(passed)
            copy(srcs[4], slot(4, landed), ici_send.at[j * n_ici + n_fwd], ici_recv.at[j * n_ici + n_fwd],
                 (px, py, c)).wait_recv()
        for j, (px, py) in enumerate(chips):
            for t, (a, k) in enumerate(parts):
                there = part(slot(a, 2 * px + py), a, k, 1 - c)
                copy(there, there, fwd_send.at[j * n_fwd + t], fwd_recv.at[j * n_fwd + t], (x, y, 1 - c)).wait_recv()
        for cp in sends:
            cp.wait_send()
        for cp in local:
            cp.wait()

    return pl.pallas_call(
        body, in_specs=[ANY] * n_arr, out_specs=[ANY] * n_arr,
        out_shape=[jax.ShapeDtypeStruct((D_MODEL, IN_WIDTH), BF16)] + [jax.ShapeDtypeStruct((D_MODEL, D_MODEL), BF16)] * 3
        + [jax.ShapeDtypeStruct((CONV_ROWS, D_MODEL), F32)],
        scratch_shapes=[pltpu.SemaphoreType.DMA((3 * n_ici,)), pltpu.SemaphoreType.DMA((3 * n_ici,)),
                        pltpu.SemaphoreType.DMA((3 * n_fwd,)), pltpu.SemaphoreType.DMA((3 * n_fwd,)),
                        pltpu.SemaphoreType.DMA((n_arr,))],
        name="gather_weights")(w_in, w_sb, w_cv, w_out, conv_w)


def _row_chunks(src, dst, n, send_sem, recv_sem, device):
    rows = src.shape[0] // n

    def copy(s, d):
        return pltpu.make_async_remote_copy(src_ref=s, dst_ref=d, send_sem=send_sem, recv_sem=recv_sem,
                                            device_id=device, device_id_type=MESH)

    chunks = [copy(src.at[pl.ds(k * rows, rows)], dst.at[pl.ds(k * rows, rows)]) for k in range(n)]
    return chunks, copy(src, dst)


def _arrival(dst, send_sem, recv_sem, device):
    return pltpu.make_async_remote_copy(src_ref=dst, dst_ref=dst, send_sem=send_sem, recv_sem=recv_sem,
                                        device_id=device, device_id_type=MESH)


def _pair_exchange(d_in, d_sb, d_cv, d_out):
    def body(i0, i1, i2, i3, r0, r1, r2, r3, send_sems, recv_sems):
        x, y, c, _ = _place()
        srcs = [i0, i1, i2, i3]
        theirs = [r0, r1, r2, r3]
        sibling = (x, y, 1 - c)
        start = pl.multiple_of((1 - c) * HALF, 128)
        views = [srcs[0].at[pl.ds(start, HALF), :]] + [srcs[a].at[:, pl.ds(start, HALF)] for a in range(1, 4)]
        pieces, wholes = [], []
        for a in range(4):
            cps, whole = _row_chunks(views[a], theirs[a], N_CHUNKS, send_sems.at[a], recv_sems.at[a], sibling)
            pieces += cps
            wholes.append(whole)
        for cp in pieces:
            cp.start()
        for cp in wholes:
            cp.wait_recv()
        for cp in wholes:
            cp.wait_send()

    shapes = [jax.ShapeDtypeStruct((HALF, IN_WIDTH), F32)] + [jax.ShapeDtypeStruct((D_MODEL, HALF), F32)] * 3
    return pl.pallas_call(
        body, in_specs=[ANY] * 4, out_specs=[ANY] * 4, out_shape=shapes,
        scratch_shapes=[pltpu.SemaphoreType.DMA((4,)), pltpu.SemaphoreType.DMA((4,))],
        name="pair_exchange")(d_in, d_sb, d_cv, d_out)


def _chip_exchange_copies(srcs, theirs, send_sems, recv_sems):
    _, _, c, chips = _place()

    def shard(a, px, py):
        chip = 2 * px + py
        if a == 0:
            return srcs[a].at[:, pl.ds(pl.multiple_of(chip * SHARD_IN, 128), SHARD_IN)]
        return srcs[a].at[pl.ds(pl.multiple_of(chip * SHARD_SQ, 16), SHARD_SQ), :]

    pieces, wholes = [], []
    for j, (px, py) in enumerate(chips):
        for a in range(4):
            cps, whole = _row_chunks(shard(a, px, py), theirs[a].at[j], 4, send_sems.at[3 * a + j],
                                     recv_sems.at[3 * a + j], (px, py, c))
            pieces += cps
            wholes.append(whole)
    return pieces, wholes


def _pair_share(f_in, f_sb, f_cv, f_out):
    def body(i0, i1, i2, i3, o0, o1, o2, o3, send_sems, recv_sems):
        del i0, i1, i2, i3
        x, y, c, _ = _place()
        dsts = [o0, o1, o2, o3]
        sibling = (x, y, 1 - c)

        def half(a, which):
            start = pl.multiple_of(which * HALF, 128)
            if a == 0:
                return dsts[a].at[pl.ds(start, HALF), :]
            return dsts[a].at[:, pl.ds(start, HALF)]

        pieces, sent, arrived = [], [], []
        for a in range(4):
            cps, whole = _row_chunks(half(a, c), half(a, c), N_CHUNKS, send_sems.at[a], recv_sems.at[a], sibling)
            pieces += cps
            sent.append(whole)
            arrived.append(_arrival(half(a, 1 - c), send_sems.at[a], recv_sems.at[a], sibling))
        for cp in pieces:
            cp.start()
        for cp in arrived:
            cp.wait_recv()
        for cp in sent:
            cp.wait_send()

    return pl.pallas_call(
        body, in_specs=[ANY] * 4, out_specs=[ANY] * 4,
        out_shape=[jax.ShapeDtypeStruct((D_MODEL, SHARD_IN), F32)] + [jax.ShapeDtypeStruct((SHARD_SQ, D_MODEL), F32)] * 3,
        input_output_aliases={0: 0, 1: 1, 2: 2, 3: 3},
        scratch_shapes=[pltpu.SemaphoreType.DMA((4,)), pltpu.SemaphoreType.DMA((4,))],
        name="pair_share")(f_in, f_sb, f_cv, f_out)


def _small_allreduce(small):
    def body(s_ref, o_ref, slots, send_sems, recv_sems):
        x, y, c, _ = _place()
        me = 4 * x + 2 * y + c
        slots[me] = s_ref[...]
        sends = []
        for k in range(1, N_DEV):
            px, py, pc = (x + (k >> 2)) % 2, (y + ((k >> 1) & 1)) % 2, (c + (k & 1)) % 2
            sends.append(pltpu.make_async_remote_copy(
                src_ref=s_ref, dst_ref=slots.at[me], send_sem=send_sems.at[k - 1], recv_sem=recv_sems.at[k - 1],
                device_id=(px, py, pc), device_id_type=MESH))
        for cp in sends:
            cp.start()
        for k in range(1, N_DEV):
            px, py, pc = (x + (k >> 2)) % 2, (y + ((k >> 1) & 1)) % 2, (c + (k & 1)) % 2
            pltpu.make_async_remote_copy(
                src_ref=s_ref, dst_ref=slots.at[4 * px + 2 * py + pc], send_sem=send_sems.at[k - 1],
                recv_sem=recv_sems.at[k - 1], device_id=(px, py, pc), device_id_type=MESH).wait_recv()
        for cp in sends:
            cp.wait_send()
        total = slots[0]
        for d in range(1, N_DEV):
            total = total + slots[d]
        o_ref[...] = total
        o_ref[7:8, :] = jnp.zeros((1, D_MODEL), F32) + jnp.sum(total[7:8, :], axis=1, keepdims=True)

    return pl.pallas_call(
        body, in_specs=[pl.BlockSpec(memory_space=pltpu.VMEM)], out_specs=pl.BlockSpec(memory_space=pltpu.VMEM),
        out_shape=jax.ShapeDtypeStruct((SMALL_ROWS, D_MODEL), F32),
        scratch_shapes=[pltpu.VMEM((N_DEV, SMALL_ROWS, D_MODEL), F32), pltpu.SemaphoreType.DMA((N_DEV - 1,)),
                        pltpu.SemaphoreType.DMA((N_DEV - 1,))],
        name="small_allreduce")(small)


SUM_BLOCKS = 8


def _pair_sum(place, full, theirs, by_rows, name):
    rows, cols = theirs.shape
    tr = rows // SUM_BLOCKS

    def body(p_ref, a_ref, b_ref, o_ref):
        del p_ref
        o_ref[...] = (a_ref[...] + b_ref[...]).astype(BF16)

    mine = (lambda i, p: (p[1] * SUM_BLOCKS + i, 0)) if by_rows else (lambda i, p: (i, p[1]))
    spec = pl.BlockSpec((tr, cols), lambda i, p: (i, 0))
    return pl.pallas_call(
        body, out_shape=jax.ShapeDtypeStruct(theirs.shape, BF16),
        grid_spec=pltpu.PrefetchScalarGridSpec(num_scalar_prefetch=1, grid=(SUM_BLOCKS,),
                                               in_specs=[pl.BlockSpec((tr, cols), mine), spec], out_specs=spec),
        compiler_params=_params("parallel"), name=name)(place, full, theirs)


def _chip_sum(place, pair, got, by_rows, name):
    _, rows, cols = got.shape
    tr = rows // SUM_BLOCKS

    def body(p_ref, a_ref, g_ref, o_ref):
        del p_ref
        o_ref[...] = ((a_ref[...].astype(F32) + g_ref[0].astype(F32)) + g_ref[1].astype(F32)) + g_ref[2].astype(F32)

    if by_rows:
        own = lambda i, p: (i, p[0])
        out = lambda i, p: (p[1] * SUM_BLOCKS + i, 0)
        full = (2 * rows, cols)
    else:
        own = lambda i, p: (p[0] * SUM_BLOCKS + i, 0)
        out = lambda i, p: (i, p[1])
        full = (rows, 2 * cols)
    return pl.pallas_call(
        body, out_shape=jax.ShapeDtypeStruct(full, F32),
        grid_spec=pltpu.PrefetchScalarGridSpec(
            num_scalar_prefetch=1, grid=(SUM_BLOCKS,),
            in_specs=[pl.BlockSpec((tr, cols), own), pl.BlockSpec((3, tr, cols), lambda i, p: (0, i, 0))],
            out_specs=pl.BlockSpec((tr, cols), out)),
        compiler_params=_params("parallel"), name=name)(place, pair, got)


def _adamw(w, g, m, v, name):
    rows, cols = w.shape
    tr = rows // 4 if rows % 32 == 0 else rows
    c1 = 1.0 - ADAM_B1 ** ADAM_STEP
    c2 = 1.0 - ADAM_B2 ** ADAM_STEP

    def body(w_ref, g_ref, m_ref, v_ref, d_ref, mo_ref, vo_ref):
        g = g_ref[...]
        mn = ADAM_B1 * m_ref[...] + (1.0 - ADAM_B1) * g
        vn = ADAM_B2 * v_ref[...] + (1.0 - ADAM_B2) * (g * g)
        mo_ref[...] = mn
        vo_ref[...] = vn
        d_ref[...] = -ADAM_LR * ((mn / c1) / (jnp.sqrt(vn / c2) + ADAM_EPS) + ADAM_WD * w_ref[...])

    spec = pl.BlockSpec((tr, cols), lambda i: (i, 0))
    shape = jax.ShapeDtypeStruct(w.shape, F32)
    return pl.pallas_call(body, grid=(rows // tr,), in_specs=[spec] * 4, out_specs=[spec] * 3,
                          out_shape=[shape] * 3, compiler_params=_params("parallel"), name=name)(w, g, m, v)


GRAD_NAMES = ("in", "sb", "cv", "out")


def _place_scalars():
    return jnp.stack([2 * lax.axis_index("x") + lax.axis_index("y"), lax.axis_index("c")]).astype(jnp.int32)


def _pair_sums(grads):
    place = _place_scalars()
    theirs = _pair_exchange(*grads)
    return [_pair_sum(place, grads[a], theirs[a], a == 0, "pair_sum_" + GRAD_NAMES[a]) for a in range(4)]


def _finish_weight_grads(pair, got):
    place = _place_scalars()
    done = [_chip_sum(place, pair[a], got[a], a == 0, "chip_sum_" + GRAD_NAMES[a]) for a in range(4)]
    return _pair_share(*done)


def kernel(x, ln_in_g, ln_in_b, w_in, w_sb_proj, conv_w, conv_b, conv_ln_g, conv_ln_b, w_cv_proj, w_out, ln_post_g, ln_post_b, loss_target, m_ln_in_g, m_ln_in_b, m_w_in, m_w_sb_proj, m_conv_w, m_conv_b, m_conv_ln_g, m_conv_ln_b, m_w_cv_proj, m_w_out, m_ln_post_g, m_ln_post_b, v_ln_in_g, v_ln_in_b, v_w_in, v_w_sb_proj, v_conv_w, v_conv_b, v_conv_ln_g, v_conv_ln_b, v_w_cv_proj, v_w_out, v_ln_post_g, v_ln_post_b):
    nb, seq, _ = x.shape
    t = nb * seq
    vec_names = ("ln_in_g", "ln_in_b", "conv_b", "conv_ln_g", "conv_ln_b", "ln_post_g", "ln_post_b")
    vec_w = dict(zip(vec_names, (ln_in_g, ln_in_b, conv_b, conv_ln_g, conv_ln_b, ln_post_g, ln_post_b)))
    vec_m = dict(zip(vec_names, (m_ln_in_g, m_ln_in_b, m_conv_b, m_conv_ln_g, m_conv_ln_b, m_ln_post_g, m_ln_post_b)))
    vec_v = dict(zip(vec_names, (v_ln_in_g, v_ln_in_b, v_conv_b, v_conv_ln_g, v_conv_ln_b, v_ln_post_g, v_ln_post_b)))
    vecs = {k: a.reshape(1, D_MODEL) for k, a in vec_w.items()}

    pad_taps = lambda a: jnp.pad(a.reshape(CONV_K, SHARD_SQ), ((0, CONV_ROWS - CONV_K), (0, 0)))
    gathered = _gather_weights(w_in[0].astype(BF16), w_sb_proj[0].astype(BF16), w_cv_proj[0].astype(BF16),
                               w_out[0].astype(BF16), pad_taps(conv_w))
    full_in, full_sb, full_cv, full_out, full_conv = gathered

    x2 = x.reshape(t, D_MODEL)
    dhs, dr, big, small = _forward_backward(x2, loss_target.reshape(t, D_MODEL), nb, seq,
                                            full_in, full_sb, full_cv, full_out, full_conv, vecs)
    pair = _pair_sums(big)
    grad_x, st_in, got = _in_proj_bwd(dhs, full_in, dr, x2, vecs["ln_in_g"], pair)
    g_in, g_sb, g_cv, g_out = _finish_weight_grads(pair, got)
    small = _small_allreduce(jnp.concatenate([st_in[0:2], small], axis=0))

    chip = 2 * lax.axis_index("x") + lax.axis_index("y")
    g_conv = lax.dynamic_slice(small, (8, chip * SHARD_SQ), (CONV_ROWS, SHARD_SQ))
    loss = small[7, 0]

    grads, deltas, new_m, new_v = {}, {}, {}, {}
    stack = lambda d: jnp.concatenate([d[k].reshape(1, D_MODEL) for k in vec_names] + [jnp.zeros((1, D_MODEL), F32)])
    vd, vm, vv = _adamw(stack(vec_w), small[0:8], stack(vec_m), stack(vec_v), "adamw_vectors")
    for r, k in enumerate(vec_names):
        shape = vec_w[k].shape
        grads[k] = small[r].reshape(shape)
        deltas[k], new_m[k], new_v[k] = vd[r].reshape(shape), vm[r].reshape(shape), vv[r].reshape(shape)

    big_w = {"w_in": (w_in, g_in, m_w_in, v_w_in), "w_sb_proj": (w_sb_proj, g_sb, m_w_sb_proj, v_w_sb_proj),
             "w_cv_proj": (w_cv_proj, g_cv, m_w_cv_proj, v_w_cv_proj), "w_out": (w_out, g_out, m_w_out, v_w_out)}
    for k, (w, g, m, v) in big_w.items():
        d, mn, vn = _adamw(w[0], g, m[0], v[0], "adamw_" + k)
        grads[k], deltas[k], new_m[k], new_v[k] = g[None], d[None], mn[None], vn[None]
    d, mn, vn = _adamw(pad_taps(conv_w), g_conv, pad_taps(m_conv_w), pad_taps(v_conv_w), "adamw_conv_w")
    grads["conv_w"] = g_conv[None, :CONV_K]
    deltas["conv_w"], new_m["conv_w"], new_v["conv_w"] = d[None, :CONV_K], mn[None, :CONV_K], vn[None, :CONV_K]

    order = ("ln_in_g", "ln_in_b", "w_in", "w_sb_proj", "conv_w", "conv_b", "conv_ln_g", "conv_ln_b",
             "w_cv_proj", "w_out", "ln_post_g", "ln_post_b")
    return (loss, grad_x.reshape(x.shape), *[grads[k] for k in order], *[deltas[k] for k in order],
            *[new_m[k] for k in order], *[new_v[k] for k in order])
```

```python
import functools

import jax
import jax.numpy as jnp
from jax import lax
from jax.experimental import pallas as pl
from jax.experimental.pallas import tpu as pltpu

F32 = jnp.float32
BF16 = jnp.bfloat16

D_MODEL = 1024
N_HEADS = 16
HEAD_DIM = 64
HEAD_GROUP = 4
GROUP_W = HEAD_GROUP * HEAD_DIM
N_GROUPS = N_HEADS // HEAD_GROUP
N_PIECES = 9
IN_WIDTH = N_PIECES * D_MODEL
Q_BLOCK = 256
Q_TILE = 2 * Q_BLOCK
CONV_K = 31
CONV_ROWS = 32
HALO = 32
LN_EPS = 1e-5
ALPHA = 2.0 ** 0.25
Q_SCALE = 0.125
N_CHIPS = 4
N_DEV = 8
SHARD_IN = IN_WIDTH // N_CHIPS
SHARD_SQ = D_MODEL // N_CHIPS
HALF = D_MODEL // 2
SMALL_ROWS = 40
N_CHUNKS = 16

ADAM_LR = 0.001
ADAM_B1 = 0.9
ADAM_B2 = 0.999
ADAM_EPS = 1e-08
ADAM_WD = 0.01
ADAM_STEP = 10

MESH = pl.DeviceIdType.MESH
ANY = pl.BlockSpec(memory_space=pl.ANY)
VMEM_LIMIT = 60 * 1024 * 1024

NT = (((1,), (1,)), ((), ()))
TN = (((0,), (0,)), ((), ()))


def _sigmoid(x):
    return 1.0 / (1.0 + jnp.exp(-x))


def _dot(a, b):
    return jnp.dot(a, b, preferred_element_type=F32)


def _dot_nt(a, b):
    return lax.dot_general(a, b, NT, preferred_element_type=F32)


def _dot_tn(a, b):
    return lax.dot_general(a, b, TN, preferred_element_type=F32)


def _split_bf16(x):
    hi = x.astype(BF16)
    lo = (x - hi.astype(F32)).astype(BF16)
    return hi, lo


def _ln_stats(x):
    mu = jnp.mean(x, axis=-1, keepdims=True)
    xc = x - mu
    var = jnp.mean(xc * xc, axis=-1, keepdims=True)
    rstd = lax.rsqrt(var + LN_EPS)
    return xc * rstd, rstd


def _ln_bwd(dy, xhat, rstd, g):
    dxh = dy * g
    m1 = jnp.mean(dxh, axis=-1, keepdims=True)
    m2 = jnp.mean(dxh * xhat, axis=-1, keepdims=True)
    return rstd * (dxh - m1 - xhat * m2)


def _params(*sem):
    return pltpu.CompilerParams(dimension_semantics=sem, vmem_limit_bytes=VMEM_LIMIT)


def _ln_in(x, g, b):
    t = x.shape[0]
    tm = min(512, t)

    def body(x_ref, g_ref, b_ref, hf_ref, hb_ref):
        xhat, _ = _ln_stats(x_ref[...])
        y = xhat * g_ref[...] + b_ref[...]
        hf_ref[...] = y
        hb_ref[...] = y.astype(BF16)

    row = pl.BlockSpec((tm, D_MODEL), lambda i: (i, 0))
    vec = pl.BlockSpec((1, D_MODEL), lambda i: (0, 0))
    return pl.pallas_call(
        body, grid=(t // tm,), in_specs=[row, vec, vec], out_specs=[row, row],
        out_shape=[jax.ShapeDtypeStruct((t, D_MODEL), F32), jax.ShapeDtypeStruct((t, D_MODEL), BF16)],
        compiler_params=_params("parallel"), name="ln_in")(x, g, b)


def _in_proj(h, w, col0, ncol, out_dtype, scale_first, name):
    t = h.shape[0]
    tm = min(1024, t)

    def body(h_ref, w_ref, o_ref):
        res = _dot(h_ref[...], w_ref[...])
        if scale_first:
            res = res * jnp.where(pl.program_id(0) == 0, Q_SCALE, 1.0)
        o_ref[...] = res.astype(out_dtype)

    return pl.pallas_call(
        body, grid=(ncol, t // tm),
        in_specs=[pl.BlockSpec((tm, D_MODEL), lambda j, i: (i, 0)),
                  pl.BlockSpec((D_MODEL, D_MODEL), lambda j, i: (0, j + col0))],
        out_specs=pl.BlockSpec((tm, D_MODEL), lambda j, i: (i, j)),
        out_shape=jax.ShapeDtypeStruct((t, ncol * D_MODEL), out_dtype),
        compiler_params=_params("parallel", "parallel"), name=name)(h, w)


def _sb_block(l, diag, tri, carry):
    lnb = jnp.minimum(-l, 0.0) - jnp.log(1.0 + jnp.exp(-jnp.abs(l)))
    if diag is not None:
        lnb = jnp.where(diag, lnb, 0.0)
    hi, lo = _split_bf16(lnb)
    suffix = _dot(hi, tri) + _dot(lo, tri)
    a = jnp.exp(l + suffix + carry)
    if diag is not None:
        a = jnp.where(diag, a, 0.0)
    return a, lnb, suffix[:, 0:1]


def _attn_consts():
    head = lax.broadcasted_iota(jnp.int32, (Q_TILE, GROUP_W), 1) // HEAD_DIM
    row = lax.broadcasted_iota(jnp.int32, (Q_TILE, Q_BLOCK), 0)
    col = lax.broadcasted_iota(jnp.int32, (Q_TILE, Q_BLOCK), 1)
    square = slice(0, Q_BLOCK)
    return head, col < row, (row[square] >= col[square]).astype(BF16), (row[square] <= col[square]).astype(BF16)


def _split_heads(t, head):
    zero = jnp.zeros_like(t)
    head = head[:t.shape[0]]
    return [jnp.where(head == h, t, zero) for h in range(HEAD_GROUP)]


def _key_block(ref, kb):
    return ref[pl.ds(pl.multiple_of(kb * Q_BLOCK, Q_BLOCK), Q_BLOCK), :]


def _attn_fwd(qkv, nb, seq):
    nq = seq // Q_TILE

    def body(q_ref, k_ref, v_ref, o_ref):
        qi = pl.program_id(2)
        head, diag, tri, _ = _attn_consts()
        qh = _split_heads(q_ref[...], head)
        o_ref[...] = jnp.zeros_like(o_ref)
        zero_col = jnp.zeros((Q_BLOCK, 1), F32)

        def block(kb, carries, rows, masked):
            kblk = _key_block(k_ref, kb)
            vstack = jnp.concatenate(_split_heads(_key_block(v_ref, kb), head), axis=0)
            mask = diag[:Q_TILE - rows.start] if masked else None
            weights, out = [], []
            for h in range(HEAD_GROUP):
                a, _, blk_sum = _sb_block(_dot_nt(qh[h][rows], kblk), mask, tri, carries[h][rows])
                weights.append(a.astype(BF16))
                out.append(carries[h][rows] + blk_sum)
            o_ref[rows, :] += _dot(jnp.concatenate(weights, axis=1), vstack)
            return out

        second = block(2 * qi + 1, [jnp.zeros((Q_TILE, 1), F32)] * HEAD_GROUP, slice(Q_BLOCK, Q_TILE), True)
        carries = tuple(jnp.concatenate([zero_col, c], axis=0) for c in second)
        everything = slice(0, Q_TILE)
        carries = tuple(block(2 * qi, carries, everything, True))
        lax.fori_loop(1, 2 * qi + 1, lambda jj, cs: tuple(block(2 * qi - jj, cs, everything, False)), carries)

    return pl.pallas_call(
        body, grid=(nb, N_GROUPS, nq),
        in_specs=[pl.BlockSpec((Q_TILE, GROUP_W), lambda b, g, i: (b * nq + i, g)),
                  pl.BlockSpec((seq, GROUP_W), lambda b, g, i: (b, N_GROUPS + g)),
                  pl.BlockSpec((seq, GROUP_W), lambda b, g, i: (b, 2 * N_GROUPS + g))],
        out_specs=pl.BlockSpec((Q_TILE, GROUP_W), lambda b, g, i: (b * nq + i, g)),
        out_shape=jax.ShapeDtypeStruct((nb * seq, D_MODEL), F32),
        compiler_params=_params("parallel", "parallel", "arbitrary"), name="attn_fwd")(qkv, qkv, qkv)


def _attn_bwd(qkv, do, nb, seq):
    nq = seq // Q_TILE
    n_kb = seq // Q_BLOCK

    def body(q_ref, k_ref, v_ref, do_ref, dq_ref, dk_ref, dv_ref, g_s, beta_s, dq_acc, dk_acc, dv_acc):
        qi = pl.program_id(2)

        @pl.when(qi == 0)
        def _():
            dk_acc[...] = jnp.zeros_like(dk_acc)
            dv_acc[...] = jnp.zeros_like(dv_acc)

        head, diag, tri_suffix, tri_prefix = _attn_consts()
        qh = _split_heads(q_ref[...], head)
        doh = _split_heads(do_ref[...], head)
        dq_acc[...] = jnp.zeros_like(dq_acc)
        zero_col = jnp.zeros((Q_BLOCK, 1), F32)
        everything = slice(0, Q_TILE)
        second = slice(Q_BLOCK, Q_TILE)

        def block_a(kb, carries, rows, masked):
            ks = pl.multiple_of(kb * Q_BLOCK, Q_BLOCK)
            kblk = _key_block(k_ref, kb)
            vblk = _key_block(v_ref, kb)
            mask = diag[:Q_TILE - rows.start] if masked else None
            weights, out = [], []
            for h in range(HEAD_GROUP):
                l = _dot_nt(qh[h][rows], kblk)
                a, lnb, blk_sum = _sb_block(l, mask, tri_suffix, carries[h][rows])
                beta = jnp.exp(l + lnb)
                if masked:
                    beta = jnp.where(mask, beta, 0.0)
                g_s[h, kb, rows, :] = a * _dot_nt(doh[h][rows], vblk)
                beta_s[h, kb, rows, :] = beta.astype(BF16)
                weights.append(a.astype(BF16))
                out.append(carries[h][rows] + blk_sum)
            dv_acc[pl.ds(ks, Q_BLOCK), :] += _dot_tn(jnp.concatenate(weights, axis=0),
                                                     jnp.concatenate([t[rows] for t in doh], axis=0))
            return out

        part = block_a(2 * qi + 1, [jnp.zeros((Q_TILE, 1), F32)] * HEAD_GROUP, second, True)
        carries = tuple(jnp.concatenate([zero_col, c], axis=0) for c in part)
        carries = tuple(block_a(2 * qi, carries, everything, True))
        lax.fori_loop(1, 2 * qi + 1, lambda jj, cs: tuple(block_a(2 * qi - jj, cs, everything, False)), carries)

        def block_b(kb, prefixes, rows, masked):
            ks = pl.multiple_of(kb * Q_BLOCK, Q_BLOCK)
            kstack = jnp.concatenate(_split_heads(_key_block(k_ref, kb), head), axis=0)
            dls, out = [], []
            for h in range(HEAD_GROUP):
                g = g_s[h, kb, rows, :]
                beta = beta_s[h, kb, rows, :].astype(F32)
                hi, lo = _split_bf16(g)
                incl = _dot(hi, tri_prefix) + _dot(lo, tri_prefix)
                before = incl - g + prefixes[h][rows]
                dl = g - beta * (g + before)
                if masked:
                    dl = jnp.where(diag[:Q_TILE - rows.start], dl, 0.0)
                dls.append(dl.astype(BF16))
                out.append(prefixes[h][rows] + incl[:, Q_BLOCK - 1:Q_BLOCK])
            dq_acc[rows, :] += _dot(jnp.concatenate(dls, axis=1), kstack)
            dk_acc[pl.ds(ks, Q_BLOCK), :] += _dot_tn(jnp.concatenate(dls, axis=0),
                                                     jnp.concatenate([t[rows] for t in qh], axis=0))
            return out

        zeros = (jnp.zeros((Q_TILE, 1), F32),) * HEAD_GROUP
        prefixes = lax.fori_loop(0, 2 * qi, lambda kb, ps: tuple(block_b(kb, ps, everything, False)), zeros)
        prefixes = block_b(2 * qi, prefixes, everything, True)
        block_b(2 * qi + 1, prefixes, second, True)
        dq_ref[...] = (dq_acc[...] * Q_SCALE).astype(BF16)

        @pl.when(qi == nq - 1)
        def _():
            dk_ref[...] = dk_acc[...].astype(BF16)
            dv_ref[...] = dv_acc[...].astype(BF16)

    t = nb * seq
    qspec = pl.BlockSpec((Q_TILE, GROUP_W), lambda b, g, i: (b * nq + i, g))
    kvout = pl.BlockSpec((seq, GROUP_W), lambda b, g, i: (b, g))
    return pl.pallas_call(
        body, grid=(nb, N_GROUPS, nq),
        in_specs=[qspec,
                  pl.BlockSpec((seq, GROUP_W), lambda b, g, i: (b, N_GROUPS + g)),
                  pl.BlockSpec((seq, GROUP_W), lambda b, g, i: (b, 2 * N_GROUPS + g)),
                  qspec],
        out_specs=[qspec, kvout, kvout],
        out_shape=[jax.ShapeDtypeStruct((t, D_MODEL), BF16)] * 3,
        scratch_shapes=[pltpu.VMEM((HEAD_GROUP, n_kb, Q_TILE, Q_BLOCK), F32),
                        pltpu.VMEM((HEAD_GROUP, n_kb, Q_TILE, Q_BLOCK), BF16),
                        pltpu.VMEM((Q_TILE, GROUP_W), F32),
                        pltpu.VMEM((seq, GROUP_W), F32), pltpu.VMEM((seq, GROUP_W), F32)],
        compiler_params=_params("parallel", "parallel", "arbitrary"), name="attn_bwd")(qkv, qkv, qkv, do)


def _conv_block_rows(seq):
    return min(256, seq)


SUBLANES = 8
CONV_SUB = 32
CONV_GROUPS = CONV_SUB // SUBLANES


def _fill_shifts(sh_ref, n):
    for r in range(1, SUBLANES):
        sh_ref[r, 0:n, :] = sh_ref[0, pl.ds(r, n), :]


def _shifted(sh_ref, start, cs):
    r = start % SUBLANES
    return sh_ref[r, start - r:start - r + CONV_SUB, cs].reshape(CONV_GROUPS, SUBLANES, -1)


def _tap(w_ref, k, cs):
    row = w_ref[k:k + 1, cs]
    return jnp.broadcast_to(row, (SUBLANES, row.shape[1]))[None]


def _tree_sum(x):
    parts = [x[i] for i in range(x.shape[0])]
    while len(parts) > 1:
        parts = [parts[i] + parts[i + 1] for i in range(0, len(parts), 2)]
    return parts[0]


def _conv_fwd(hrest, conv_w, conv_b, ln_g, ln_b, nb, seq):
    ts = _conv_block_rows(seq)
    nblk = seq // ts
    lanes = 128

    def body(cv_ref, cg_ref, cvh_ref, cgh_ref, z_ref, w_ref, cb_ref, g_ref, b_ref, u1_ref, cvin_ref, ush):
        i = pl.program_id(1)
        halo = cvh_ref[...] * _sigmoid(cgh_ref[...])
        ush[0, 0:HALO, :] = jnp.where(i > 0, halo, 0.0)
        ush[0, HALO:HALO + ts, :] = cv_ref[...] * _sigmoid(cg_ref[...])
        _fill_shifts(ush, HALO + ts - SUBLANES)
        for cc in range(D_MODEL // lanes):
            cs = slice(cc * lanes, (cc + 1) * lanes)
            for r0 in range(0, ts, CONV_SUB):
                acc = jnp.zeros((CONV_GROUPS, SUBLANES, lanes), F32) + _tap(cb_ref, 0, cs)
                for k in range(CONV_K):
                    acc = acc + _tap(w_ref, k, cs) * _shifted(ush, HALO - CONV_K + 1 + k + r0, cs)
                u1_ref[r0:r0 + CONV_SUB, cs] = acc.reshape(CONV_SUB, lanes)
        xhat, _ = _ln_stats(u1_ref[...])
        u2 = xhat * g_ref[...] + b_ref[...]
        z = z_ref[...]
        cvin_ref[...] = (u2 * _sigmoid(u2) * z * _sigmoid(z)).astype(BF16)

    def main(colblk):
        return pl.BlockSpec((ts, D_MODEL), lambda b, i: (b * nblk + i, colblk))

    def halo(colblk):
        return pl.BlockSpec((HALO, D_MODEL),
                            lambda b, i: (jnp.maximum((b * seq + i * ts) // HALO - 1, 0), colblk))

    vec = pl.BlockSpec((1, D_MODEL), lambda b, i: (0, 0))
    t = nb * seq
    return pl.pallas_call(
        body, grid=(nb, nblk),
        in_specs=[main(1), main(2), halo(1), halo(2), main(3),
                  pl.BlockSpec((CONV_ROWS, D_MODEL), lambda b, i: (0, 0)), vec, vec, vec],
        out_specs=[main(0), main(0)],
        out_shape=[jax.ShapeDtypeStruct((t, D_MODEL), F32), jax.ShapeDtypeStruct((t, D_MODEL), BF16)],
        scratch_shapes=[pltpu.VMEM((SUBLANES, HALO + ts, D_MODEL), F32)],
        compiler_params=_params("parallel", "parallel"), name="conv_fwd")(
            hrest, hrest, hrest, hrest, hrest, conv_w, conv_b, ln_g, ln_b)


def _tail(o, hrest, cvin, h0, tgt, w_sb, w_cv, w_out, ln_g, ln_b):
    t = o.shape[0]
    tm = min(256, t)

    def body(o_ref, z_ref, gs_ref, gc_ref, cvin_ref, h0_ref, tgt_ref, wsb_ref, wcv_ref, wout_ref, g_ref, b_ref,
             dr_ref, drb_ref, a_ref, mg_ref, dysb_ref, dycv_ref, do_ref, dz_ref, dgs_ref, dgc_ref, dcvin_ref,
             st_ref):
        @pl.when(pl.program_id(0) == 0)
        def _():
            st_ref[...] = jnp.zeros_like(st_ref)

        o = o_ref[...]
        z = z_ref[...]
        sz = _sigmoid(z)
        a = (o * z * sz).astype(BF16)
        a_ref[...] = a
        y_sb = _dot(a, wsb_ref[...])
        y_cv = _dot(cvin_ref[...], wcv_ref[...])
        s_sb = _sigmoid(gs_ref[...])
        s_cv = _sigmoid(gc_ref[...])
        merged = (s_sb * y_sb + s_cv * y_cv).astype(BF16)
        mg_ref[...] = merged
        r = ALPHA * h0_ref[...] + _dot(merged, wout_ref[...])
        xhat, rstd = _ln_stats(r)
        g = g_ref[...]
        err = xhat * g + b_ref[...] - tgt_ref[...]
        dy = err * (1.0 / D_MODEL)
        st_ref[0:1, :] += jnp.sum(dy * xhat, axis=0, keepdims=True)
        st_ref[1:2, :] += jnp.sum(dy, axis=0, keepdims=True)
        st_ref[2:3, :] += (0.5 / D_MODEL) * jnp.sum(err * err, axis=0, keepdims=True)
        dr = _ln_bwd(dy, xhat, rstd, g)
        dr_ref[...] = dr
        drb = dr.astype(BF16)
        drb_ref[...] = drb
        dm = _dot_nt(drb, wout_ref[...])
        dy_sb = (dm * s_sb).astype(BF16)
        dy_cv = (dm * s_cv).astype(BF16)
        dysb_ref[...] = dy_sb
        dycv_ref[...] = dy_cv
        dgs_ref[...] = (dm * y_sb * s_sb * (1.0 - s_sb)).astype(BF16)
        dgc_ref[...] = (dm * y_cv * s_cv * (1.0 - s_cv)).astype(BF16)
        da = _dot_nt(dy_sb, wsb_ref[...])
        dcvin_ref[...] = _dot_nt(dy_cv, wcv_ref[...])
        do_ref[...] = (da * z * sz).astype(BF16)
        dz_ref[...] = (da * o * sz * (1.0 + z * (1.0 - sz))).astype(BF16)

    def tok(colblk=0):
        return pl.BlockSpec((tm, D_MODEL), lambda i: (i, colblk))

    wspec = pl.BlockSpec((D_MODEL, D_MODEL), lambda i: (0, 0), pipeline_mode=pl.Buffered(1))
    vec = pl.BlockSpec((1, D_MODEL), lambda i: (0, 0))
    bf = jax.ShapeDtypeStruct((t, D_MODEL), BF16)
    f32 = jax.ShapeDtypeStruct((t, D_MODEL), F32)
    return pl.pallas_call(
        body, grid=(t // tm,),
        in_specs=[tok(), tok(0), tok(4), tok(5), tok(), tok(), tok(), wspec, wspec, wspec, vec, vec],
        out_specs=[tok()] * 11 + [pl.BlockSpec((8, D_MODEL), lambda i: (0, 0))],
        out_shape=[f32] + [bf] * 9 + [f32, jax.ShapeDtypeStruct((8, D_MODEL), F32)],
        compiler_params=_params("arbitrary"), name="tail")(
            o, hrest, hrest, hrest, cvin, h0, tgt, w_sb, w_cv, w_out, ln_g, ln_b)


def _conv_branch_bwd(dcvin, u1, hrest, ln_g, ln_b):
    t = u1.shape[0]
    tm = min(512, t)

    def body(dc_ref, u1_ref, z_ref, g_ref, b_ref, du1_ref, dz_ref, st_ref):
        @pl.when(pl.program_id(0) == 0)
        def _():
            st_ref[...] = jnp.zeros_like(st_ref)

        xhat, rstd = _ln_stats(u1_ref[...])
        g = g_ref[...]
        u2 = xhat * g + b_ref[...]
        s2 = _sigmoid(u2)
        z = z_ref[...]
        sz = _sigmoid(z)
        dc = dc_ref[...]
        dz_ref[...] = (dc * (u2 * s2) * sz * (1.0 + z * (1.0 - sz))).astype(BF16)
        du2 = dc * (z * sz) * s2 * (1.0 + u2 * (1.0 - s2))
        st_ref[0:1, :] += jnp.sum(du2 * xhat, axis=0, keepdims=True)
        st_ref[1:2, :] += jnp.sum(du2, axis=0, keepdims=True)
        du1 = _ln_bwd(du2, xhat, rstd, g)
        du1_ref[...] = du1
        st_ref[2:3, :] += jnp.sum(du1, axis=0, keepdims=True)

    tok = pl.BlockSpec((tm, D_MODEL), lambda i: (i, 0))
    vec = pl.BlockSpec((1, D_MODEL), lambda i: (0, 0))
    return pl.pallas_call(
        body, grid=(t // tm,),
        in_specs=[tok, tok, pl.BlockSpec((tm, D_MODEL), lambda i: (i, 3)), vec, vec],
        out_specs=[tok, tok, pl.BlockSpec((8, D_MODEL), lambda i: (0, 0))],
        out_shape=[jax.ShapeDtypeStruct((t, D_MODEL), F32), jax.ShapeDtypeStruct((t, D_MODEL), BF16),
                   jax.ShapeDtypeStruct((8, D_MODEL), F32)],
        compiler_params=_params("arbitrary"), name="conv_branch_bwd")(dcvin, u1, hrest, ln_g, ln_b)


def _conv_bwd(du1, hrest, conv_w, nb, seq):
    ts = _conv_block_rows(seq)
    nblk = seq // ts
    lanes = 128
    last_halo = nb * seq // HALO - 1

    def body(du_ref, duh_ref, cv_ref, cg_ref, cvh_ref, cgh_ref, w_ref, dcv_ref, dcg_ref, dw_ref, ush, dsh, dw_part):
        b = pl.program_id(0)
        i = pl.program_id(1)

        @pl.when((b == 0) & (i == 0))
        def _():
            dw_part[...] = jnp.zeros_like(dw_part)

        cv = cv_ref[...]
        sg = _sigmoid(cg_ref[...])
        halo = cvh_ref[...] * _sigmoid(cgh_ref[...])
        ush[0, 0:HALO, :] = jnp.where(i > 0, halo, 0.0)
        ush[0, HALO:HALO + ts, :] = cv * sg
        dsh[0, 0:ts, :] = du_ref[...]
        dsh[0, ts:ts + HALO, :] = jnp.where(i < nblk - 1, duh_ref[...], 0.0)
        _fill_shifts(ush, HALO + ts - SUBLANES)
        _fill_shifts(dsh, HALO + ts - SUBLANES)
        for cc in range(D_MODEL // lanes):
            cs = slice(cc * lanes, (cc + 1) * lanes)
            for r0 in range(0, ts, CONV_SUB):
                rows = slice(r0, r0 + CONV_SUB)
                dmain = _shifted(dsh, r0, cs)
                acc = jnp.zeros((CONV_GROUPS, SUBLANES, lanes), F32)
                for k in range(CONV_K):
                    acc = acc + _tap(w_ref, k, cs) * _shifted(dsh, CONV_K - 1 - k + r0, cs)
                    dw_part[k, :, cs] += _tree_sum(dmain * _shifted(ush, HALO - CONV_K + 1 + k + r0, cs))
                acc = acc.reshape(CONV_SUB, lanes)
                dcv_ref[rows, cs] = (acc * sg[rows, cs]).astype(BF16)
                dcg_ref[rows, cs] = (acc * cv[rows, cs] * sg[rows, cs] * (1.0 - sg[rows, cs])).astype(BF16)

        @pl.when((b == nb - 1) & (i == nblk - 1))
        def _():
            for k in range(CONV_ROWS):
                dw_ref[k:k + 1, :] = jnp.sum(dw_part[k], axis=0, keepdims=True)

    def main(colblk):
        return pl.BlockSpec((ts, D_MODEL), lambda b, i: (b * nblk + i, colblk))

    def halo_before(colblk):
        return pl.BlockSpec((HALO, D_MODEL),
                            lambda b, i: (jnp.maximum((b * seq + i * ts) // HALO - 1, 0), colblk))

    halo_after = pl.BlockSpec((HALO, D_MODEL),
                              lambda b, i: (jnp.minimum((b * seq + (i + 1) * ts) // HALO, last_halo), 0))
    t = nb * seq
    return pl.pallas_call(
        body, grid=(nb, nblk),
        in_specs=[main(0), halo_after, main(1), main(2), halo_before(1), halo_before(2),
                  pl.BlockSpec((CONV_ROWS, D_MODEL), lambda b, i: (0, 0))],
        out_specs=[main(0), main(0), pl.BlockSpec((CONV_ROWS, D_MODEL), lambda b, i: (0, 0))],
        out_shape=[jax.ShapeDtypeStruct((t, D_MODEL), BF16), jax.ShapeDtypeStruct((t, D_MODEL), BF16),
                   jax.ShapeDtypeStruct((CONV_ROWS, D_MODEL), F32)],
        scratch_shapes=[pltpu.VMEM((SUBLANES, HALO + ts, D_MODEL), F32), pltpu.VMEM((SUBLANES, ts + HALO, D_MODEL), F32),
                        pltpu.VMEM((CONV_ROWS, SUBLANES, D_MODEL), F32)],
        compiler_params=_params("arbitrary", "arbitrary"), name="conv_bwd")(
            du1, du1, hrest, hrest, hrest, hrest, conv_w)


def _weight_grad(x, dys, name):
    n = len(dys)
    t = x.shape[0]
    tm = min(512, t)

    def body(x_ref, *refs):
        o_ref = refs[n]
        j = pl.program_id(0)

        @pl.when(pl.program_id(1) == 0)
        def _():
            o_ref[...] = jnp.zeros_like(o_ref)

        for p in range(n):
            @pl.when(j == p)
            def _(p=p):
                o_ref[...] += _dot_tn(x_ref[...], refs[p][...])

    def dy_spec(p):
        return pl.BlockSpec((tm, D_MODEL), lambda j, i: (jnp.where(j == p, i, 0), 0))

    return pl.pallas_call(
        body, grid=(n, t // tm),
        in_specs=[pl.BlockSpec((tm, D_MODEL), lambda j, i: (i, 0))] + [dy_spec(p) for p in range(n)],
        out_specs=pl.BlockSpec((D_MODEL, D_MODEL), lambda j, i: (0, j)),
        out_shape=jax.ShapeDtypeStruct((D_MODEL, n * D_MODEL), F32),
        compiler_params=_params("parallel", "arbitrary"), name=name)(x, *dys)


def _in_proj_bwd(dhs, w_in, dr, x, ln_g, pair):
    n = len(dhs)
    t = x.shape[0]
    tm = min(512, t)
    n_i = t // tm

    def body(*refs):
        dh_refs = refs[:n]
        w_ref, dr_ref, x_ref, g_ref = refs[n:n + 4]
        pair_refs = refs[n + 4:n + 8]
        gx_ref, st_ref = refs[n + 8:n + 10]
        got_refs = refs[n + 10:n + 14]
        acc, send_sems, recv_sems = refs[n + 14:]
        i = pl.program_id(0)
        p = pl.program_id(1)
        pieces, wholes = _chip_exchange_copies(pair_refs, got_refs, send_sems, recv_sems)

        @pl.when((i == 0) & (p == 0))
        def _():
            st_ref[...] = jnp.zeros_like(st_ref)
            for cp in pieces:
                cp.start()

        @pl.when((i == n_i - 1) & (p == n - 1))
        def _():
            for cp in wholes:
                cp.wait_recv()
            for cp in wholes:
                cp.wait_send()

        @pl.when(p == 0)
        def _():
            acc[...] = ALPHA * dr_ref[...]

        for q in range(n):
            @pl.when(p == q)
            def _(q=q):
                acc[...] += _dot_nt(dh_refs[q][...], w_ref[...])

        @pl.when(p == n - 1)
        def _():
            xhat, rstd = _ln_stats(x_ref[...])
            dh0 = acc[...]
            st_ref[0:1, :] += jnp.sum(dh0 * xhat, axis=0, keepdims=True)
            st_ref[1:2, :] += jnp.sum(dh0, axis=0, keepdims=True)
            gx_ref[...] = _ln_bwd(dh0, xhat, rstd, g_ref[...])

    tok = pl.BlockSpec((tm, D_MODEL), lambda i, p: (i, 0))
    got = [jax.ShapeDtypeStruct((3, HALF, SHARD_IN), BF16)] + [jax.ShapeDtypeStruct((3, SHARD_SQ, HALF), BF16)] * 3
    res = pl.pallas_call(
        body, grid=(n_i, n),
        in_specs=[tok] * n + [pl.BlockSpec((D_MODEL, D_MODEL), lambda i, p: (0, p)), tok, tok,
                              pl.BlockSpec((1, D_MODEL), lambda i, p: (0, 0))] + [ANY] * 4,
        out_specs=[tok, pl.BlockSpec((8, D_MODEL), lambda i, p: (0, 0))] + [ANY] * 4,
        out_shape=[jax.ShapeDtypeStruct((t, D_MODEL), F32), jax.ShapeDtypeStruct((8, D_MODEL), F32)] + got,
        scratch_shapes=[pltpu.VMEM((tm, D_MODEL), F32), pltpu.SemaphoreType.DMA((12,)), pltpu.SemaphoreType.DMA((12,))],
        compiler_params=_params("arbitrary", "arbitrary"), name="in_proj_bwd")(*dhs, w_in, dr, x, ln_g, *pair)
    return res[0], res[1], res[2:]


def _forward_backward(x, tgt, nb, seq, w_in, w_sb, w_cv, w_out, conv_w, vecs):
    h0, h0b = _ln_in(x, vecs["ln_in_g"], vecs["ln_in_b"])
    qkv = _in_proj(h0b, w_in, 0, 3, BF16, True, "in_proj_qkv")
    hrest = _in_proj(h0b, w_in, 3, 6, F32, False, "in_proj_rest")
    o = _attn_fwd(qkv, nb, seq)
    u1, cvin = _conv_fwd(hrest, conv_w, vecs["conv_b"], vecs["conv_ln_g"], vecs["conv_ln_b"], nb, seq)
    (dr, drb, a, merged, dy_sb, dy_cv, do, dz_sb, dg_sb, dg_cv, dcvin, st_tail) = _tail(
        o, hrest, cvin, h0, tgt, w_sb, w_cv, w_out, vecs["ln_post_g"], vecs["ln_post_b"])
    d_w_sb = _weight_grad(a, [dy_sb], "grad_w_sb")
    d_w_cv = _weight_grad(cvin, [dy_cv], "grad_w_cv")
    d_w_out = _weight_grad(merged, [drb], "grad_w_out")
    du1, dz_cv, st_conv = _conv_branch_bwd(dcvin, u1, hrest, vecs["conv_ln_g"], vecs["conv_ln_b"])
    dc_val, dc_gate, d_conv_w = _conv_bwd(du1, hrest, conv_w, nb, seq)
    dq, dk, dv = _attn_bwd(qkv, do, nb, seq)
    dhs = [dq, dk, dv, dz_sb, dc_val, dc_gate, dz_cv, dg_sb, dg_cv]
    d_w_in = _weight_grad(h0b, dhs, "grad_w_in")
    small = jnp.concatenate([st_conv[2:3], st_conv[0:2], st_tail[0:3], d_conv_w], axis=0)
    return dhs, dr, (d_w_in, d_w_sb, d_w_cv, d_w_out), small


def _place():
    x, y, c = lax.axis_index("x"), lax.axis_index("y"), lax.axis_index("c")
    chips = [(1 - x, y), (x, 1 - y), (1 - x, 1 - y)]
    return x, y, c, chips


def _gather_weights(w_in, w_sb, w_cv, w_out, conv_w):
    n_arr = 5
    n_part = 4
    parts = [(0, k) for k in range(n_part)] + [(1, 0), (2, 0), (3, 0)]
    n_fwd = len(parts)
    n_ici = n_fwd + 1
    part_rows = HALF // n_part
    half_sq = SHARD_SQ // 2

    def body(i0, i1, i2, i3, i4, o0, o1, o2, o3, o4, ici_send, ici_recv, fwd_send, fwd_recv, local_sems):
        x, y, c, chips = _place()
        srcs = [i0, i1, i2, i3, i4]
        dsts = [o0, o1, o2, o3, o4]
        me = 2 * x + y

        def slot(a, chip):
            if a == 0:
                return dsts[a].at[:, pl.ds(pl.multiple_of(chip * SHARD_IN, 128), SHARD_IN)]
            if a == 4:
                return dsts[a].at[:, pl.ds(pl.multiple_of(chip * SHARD_SQ, 128), SHARD_SQ)]
            return dsts[a].at[pl.ds(pl.multiple_of(chip * SHARD_SQ, 16), SHARD_SQ), :]

        def part(ref, a, k, half):
            if a == 0:
                return ref.at[pl.ds(pl.multiple_of(half * HALF + k * part_rows, 16), part_rows), :]
            return ref.at[pl.ds(pl.multiple_of(half * half_sq, 16), half_sq), :]

        def copy(src, dst, send_sem, recv_sem, device):
            return pltpu.make_async_remote_copy(src_ref=src, dst_ref=dst, send_sem=send_sem, recv_sem=recv_sem,
                                                device_id=device, device_id_type=MESH)

        local = [pltpu.make_async_copy(srcs[a], slot(a, me), local_sems.at[a]) for a in range(n_arr)]
        for cp in local:
            cp.start()
        sends = []
        for j, (px, py) in enumerate(chips):
            for t, (a, k) in enumerate(parts):
                sends.append(copy(part(srcs[a], a, k, c), part(slot(a, me), a, k, c),
                                  ici_send.at[j * n_ici + t], ici_recv.at[j * n_ici + t], (px, py, c)))
            sends.append(copy(srcs[4], slot(4, me), ici_send.at[j * n_ici + n_fwd], ici_recv.at[j * n_ici + n_fwd],
                              (px, py, c)))
        for cp in sends:
            cp.start()
        for j, (px, py) in enumerate(chips):
            landed = 2 * px + py
            for t, (a, k) in enumerate(parts):
                here = part(slot(a, landed), a, k, c)
                copy(here, here, ici_send.at[j * n_ici + t], ici_recv.at[j * n_ici + t], (px, py, c)).wait_recv()
                passed = copy(here, here, fwd_send.at[j * n_fwd + t], fwd_recv.at[j * n_fwd + t], (x, y, 1 - c))
                passed.start()
                sends.append(passed)
            copy(srcs[4], slot(4, landed), ici_send.at[j * n_ici + n_fwd], ici_recv.at[j * n_ici + n_fwd],
                 (px, py, c)).wait_recv()
        for j, (px, py) in enumerate(chips):
            for t, (a, k) in enumerate(parts):
                there = part(slot(a, 2 * px + py), a, k, 1 - c)
                copy(there, there, fwd_send.at[j * n_fwd + t], fwd_recv.at[j * n_fwd + t], (x, y, 1 - c)).wait_recv()
        for cp in sends:
            cp.wait_send()
        for cp in local:
            cp.wait()

    return pl.pallas_call(
        body, in_specs=[ANY] * n_arr, out_specs=[ANY] * n_arr,
        out_shape=[jax.ShapeDtypeStruct((D_MODEL, IN_WIDTH), BF16)] + [jax.ShapeDtypeStruct((D_MODEL, D_MODEL), BF16)] * 3
        + [jax.ShapeDtypeStruct((CONV_ROWS, D_MODEL), F32)],
        scratch_shapes=[pltpu.SemaphoreType.DMA((3 * n_ici,)), pltpu.SemaphoreType.DMA((3 * n_ici,)),
                        pltpu.SemaphoreType.DMA((3 * n_fwd,)), pltpu.SemaphoreType.DMA((3 * n_fwd,)),
                        pltpu.SemaphoreType.DMA((n_arr,))],
        name="gather_weights")(w_in, w_sb, w_cv, w_out, conv_w)


def _row_chunks(src, dst, n, send_sem, recv_sem, device):
    rows = src.shape[0] // n

    def copy(s, d):
        return pltpu.make_async_remote_copy(src_ref=s, dst_ref=d, send_sem=send_sem, recv_sem=recv_sem,
                                            device_id=device, device_id_type=MESH)

    chunks = [copy(src.at[pl.ds(k * rows, rows)], dst.at[pl.ds(k * rows, rows)]) for k in range(n)]
    return chunks, copy(src, dst)


def _arrival(dst, send_sem, recv_sem, device):
    return pltpu.make_async_remote_copy(src_ref=dst, dst_ref=dst, send_sem=send_sem, recv_sem=recv_sem,
                                        device_id=device, device_id_type=MESH)


def _pair_exchange(d_in, d_sb, d_cv, d_out):
    def body(i0, i1, i2, i3, r0, r1, r2, r3, send_sems, recv_sems):
        x, y, c, _ = _place()
        srcs = [i0, i1, i2, i3]
        theirs = [r0, r1, r2, r3]
        sibling = (x, y, 1 - c)
        start = pl.multiple_of((1 - c) * HALF, 128)
        views = [srcs[0].at[pl.ds(start, HALF), :]] + [srcs[a].at[:, pl.ds(start, HALF)] for a in range(1, 4)]
        pieces, wholes = [], []
        for a in range(4):
            cps, whole = _row_chunks(views[a], theirs[a], N_CHUNKS, send_sems.at[a], recv_sems.at[a], sibling)
            pieces += cps
            wholes.append(whole)
        for cp in pieces:
            cp.start()
        for cp in wholes:
            cp.wait_recv()
        for cp in wholes:
            cp.wait_send()

    shapes = [jax.ShapeDtypeStruct((HALF, IN_WIDTH), F32)] + [jax.ShapeDtypeStruct((D_MODEL, HALF), F32)] * 3
    return pl.pallas_call(
        body, in_specs=[ANY] * 4, out_specs=[ANY] * 4, out_shape=shapes,
        scratch_shapes=[pltpu.SemaphoreType.DMA((4,)), pltpu.SemaphoreType.DMA((4,))],
        name="pair_exchange")(d_in, d_sb, d_cv, d_out)


def _chip_exchange_copies(srcs, theirs, send_sems, recv_sems):
    _, _, c, chips = _place()

    def shard(a, px, py):
        chip = 2 * px + py
        if a == 0:
            return srcs[a].at[:, pl.ds(pl.multiple_of(chip * SHARD_IN, 128), SHARD_IN)]
        return srcs[a].at[pl.ds(pl.multiple_of(chip * SHARD_SQ, 16), SHARD_SQ), :]

    pieces, wholes = [], []
    for j, (px, py) in enumerate(chips):
        for a in range(4):
            cps, whole = _row_chunks(shard(a, px, py), theirs[a].at[j], 4, send_sems.at[3 * a + j],
                                     recv_sems.at[3 * a + j], (px, py, c))
            pieces += cps
            wholes.append(whole)
    return pieces, wholes


def _pair_share(f_in, f_sb, f_cv, f_out):
    def body(i0, i1, i2, i3, o0, o1, o2, o3, send_sems, recv_sems):
        del i0, i1, i2, i3
        x, y, c, _ = _place()
        dsts = [o0, o1, o2, o3]
        sibling = (x, y, 1 - c)

        def half(a, which):
            start = pl.multiple_of(which * HALF, 128)
            if a == 0:
                return dsts[a].at[pl.ds(start, HALF), :]
            return dsts[a].at[:, pl.ds(start, HALF)]

        pieces, sent, arrived = [], [], []
        for a in range(4):
            cps, whole = _row_chunks(half(a, c), half(a, c), N_CHUNKS, send_sems.at[a], recv_sems.at[a], sibling)
            pieces += cps
            sent.append(whole)
            arrived.append(_arrival(half(a, 1 - c), send_sems.at[a], recv_sems.at[a], sibling))
        for cp in pieces:
            cp.start()
        for cp in arrived:
            cp.wait_recv()
        for cp in sent:
            cp.wait_send()

    return pl.pallas_call(
        body, in_specs=[ANY] * 4, out_specs=[ANY] * 4,
        out_shape=[jax.ShapeDtypeStruct((D_MODEL, SHARD_IN), F32)] + [jax.ShapeDtypeStruct((SHARD_SQ, D_MODEL), F32)] * 3,
        input_output_aliases={0: 0, 1: 1, 2: 2, 3: 3},
        scratch_shapes=[pltpu.SemaphoreType.DMA((4,)), pltpu.SemaphoreType.DMA((4,))],
        name="pair_share")(f_in, f_sb, f_cv, f_out)


def _small_allreduce(small):
    def body(s_ref, o_ref, slots, send_sems, recv_sems):
        x, y, c, _ = _place()
        me = 4 * x + 2 * y + c
        slots[me] = s_ref[...]
        sends = []
        for k in range(1, N_DEV):
            px, py, pc = (x + (k >> 2)) % 2, (y + ((k >> 1) & 1)) % 2, (c + (k & 1)) % 2
            sends.append(pltpu.make_async_remote_copy(
                src_ref=s_ref, dst_ref=slots.at[me], send_sem=send_sems.at[k - 1], recv_sem=recv_sems.at[k - 1],
                device_id=(px, py, pc), device_id_type=MESH))
        for cp in sends:
            cp.start()
        for k in range(1, N_DEV):
            px, py, pc = (x + (k >> 2)) % 2, (y + ((k >> 1) & 1)) % 2, (c + (k & 1)) % 2
            pltpu.make_async_remote_copy(
                src_ref=s_ref, dst_ref=slots.at[4 * px + 2 * py + pc], send_sem=send_sems.at[k - 1],
                recv_sem=recv_sems.at[k - 1], device_id=(px, py, pc), device_id_type=MESH).wait_recv()
        for cp in sends:
            cp.wait_send()
        total = slots[0]
        for d in range(1, N_DEV):
            total = total + slots[d]
        o_ref[...] = total
        o_ref[7:8, :] = jnp.zeros((1, D_MODEL), F32) + jnp.sum(total[7:8, :], axis=1, keepdims=True)

    return pl.pallas_call(
        body, in_specs=[pl.BlockSpec(memory_space=pltpu.VMEM)], out_specs=pl.BlockSpec(memory_space=pltpu.VMEM),
        out_shape=jax.ShapeDtypeStruct((SMALL_ROWS, D_MODEL), F32),
        scratch_shapes=[pltpu.VMEM((N_DEV, SMALL_ROWS, D_MODEL), F32), pltpu.SemaphoreType.DMA((N_DEV - 1,)),
                        pltpu.SemaphoreType.DMA((N_DEV - 1,))],
        name="small_allreduce")(small)


SUM_BLOCKS = 8


def _pair_sum(place, full, theirs, by_rows, name):
    rows, cols = theirs.shape
    tr = rows // SUM_BLOCKS

    def body(p_ref, a_ref, b_ref, o_ref):
        del p_ref
        o_ref[...] = (a_ref[...] + b_ref[...]).astype(BF16)

    mine = (lambda i, p: (p[1] * SUM_BLOCKS + i, 0)) if by_rows else (lambda i, p: (i, p[1]))
    spec = pl.BlockSpec((tr, cols), lambda i, p: (i, 0))
    return pl.pallas_call(
        body, out_shape=jax.ShapeDtypeStruct(theirs.shape, BF16),
        grid_spec=pltpu.PrefetchScalarGridSpec(num_scalar_prefetch=1, grid=(SUM_BLOCKS,),
                                               in_specs=[pl.BlockSpec((tr, cols), mine), spec], out_specs=spec),
        compiler_params=_params("parallel"), name=name)(place, full, theirs)


def _chip_sum(place, pair, got, by_rows, name):
    _, rows, cols = got.shape
    tr = rows // SUM_BLOCKS

    def body(p_ref, a_ref, g_ref, o_ref):
        del p_ref
        o_ref[...] = ((a_ref[...].astype(F32) + g_ref[0].astype(F32)) + g_ref[1].astype(F32)) + g_ref[2].astype(F32)

    if by_rows:
        own = lambda i, p: (i, p[0])
        out = lambda i, p: (p[1] * SUM_BLOCKS + i, 0)
        full = (2 * rows, cols)
    else:
        own = lambda i, p: (p[0] * SUM_BLOCKS + i, 0)
        out = lambda i, p: (i, p[1])
        full = (rows, 2 * cols)
    return pl.pallas_call(
        body, out_shape=jax.ShapeDtypeStruct(full, F32),
        grid_spec=pltpu.PrefetchScalarGridSpec(
            num_scalar_prefetch=1, grid=(SUM_BLOCKS,),
            in_specs=[pl.BlockSpec((tr, cols), own), pl.BlockSpec((3, tr, cols), lambda i, p: (0, i, 0))],
            out_specs=pl.BlockSpec((tr, cols), out)),
        compiler_params=_params("parallel"), name=name)(place, pair, got)


def _adamw(w, g, m, v, name):
    rows, cols = w.shape
    tr = rows // 4 if rows % 32 == 0 else rows
    c1 = 1.0 - ADAM_B1 ** ADAM_STEP
    c2 = 1.0 - ADAM_B2 ** ADAM_STEP

    def body(w_ref, g_ref, m_ref, v_ref, d_ref, mo_ref, vo_ref):
        g = g_ref[...]
        mn = ADAM_B1 * m_ref[...] + (1.0 - ADAM_B1) * g
        vn = ADAM_B2 * v_ref[...] + (1.0 - ADAM_B2) * (g * g)
        mo_ref[...] = mn
        vo_ref[...] = vn
        d_ref[...] = -ADAM_LR * ((mn / c1) / (jnp.sqrt(vn / c2) + ADAM_EPS) + ADAM_WD * w_ref[...])

    spec = pl.BlockSpec((tr, cols), lambda i: (i, 0))
    shape = jax.ShapeDtypeStruct(w.shape, F32)
    return pl.pallas_call(body, grid=(rows // tr,), in_specs=[spec] * 4, out_specs=[spec] * 3,
                          out_shape=[shape] * 3, compiler_params=_params("parallel"), name=name)(w, g, m, v)


GRAD_NAMES = ("in", "sb", "cv", "out")


def _place_scalars():
    return jnp.stack([2 * lax.axis_index("x") + lax.axis_index("y"), lax.axis_index("c")]).astype(jnp.int32)


def _pair_sums(grads):
    place = _place_scalars()
    theirs = _pair_exchange(*grads)
    return [_pair_sum(place, grads[a], theirs[a], a == 0, "pair_sum_" + GRAD_NAMES[a]) for a in range(4)]


def _finish_weight_grads(pair, got):
    place = _place_scalars()
    done = [_chip_sum(place, pair[a], got[a], a == 0, "chip_sum_" + GRAD_NAMES[a]) for a in range(4)]
    return _pair_share(*done)


def kernel(x, ln_in_g, ln_in_b, w_in, w_sb_proj, conv_w, conv_b, conv_ln_g, conv_ln_b, w_cv_proj, w_out, ln_post_g, ln_post_b, loss_target, m_ln_in_g, m_ln_in_b, m_w_in, m_w_sb_proj, m_conv_w, m_conv_b, m_conv_ln_g, m_conv_ln_b, m_w_cv_proj, m_w_out, m_ln_post_g, m_ln_post_b, v_ln_in_g, v_ln_in_b, v_w_in, v_w_sb_proj, v_conv_w, v_conv_b, v_conv_ln_g, v_conv_ln_b, v_w_cv_proj, v_w_out, v_ln_post_g, v_ln_post_b):
    nb, seq, _ = x.shape
    t = nb * seq
    vec_names = ("ln_in_g", "ln_in_b", "conv_b", "conv_ln_g", "conv_ln_b", "ln_post_g", "ln_post_b")
    vec_w = dict(zip(vec_names, (ln_in_g, ln_in_b, conv_b, conv_ln_g, conv_ln_b, ln_post_g, ln_post_b)))
    vec_m = dict(zip(vec_names, (m_ln_in_g, m_ln_in_b, m_conv_b, m_conv_ln_g, m_conv_ln_b, m_ln_post_g, m_ln_post_b)))
    vec_v = dict(zip(vec_names, (v_ln_in_g, v_ln_in_b, v_conv_b, v_conv_ln_g, v_conv_ln_b, v_ln_post_g, v_ln_post_b)))
    vecs = {k: a.reshape(1, D_MODEL) for k, a in vec_w.items()}

    pad_taps = lambda a: jnp.pad(a.reshape(CONV_K, SHARD_SQ), ((0, CONV_ROWS - CONV_K), (0, 0)))
    gathered = _gather_weights(w_in[0].astype(BF16), w_sb_proj[0].astype(BF16), w_cv_proj[0].astype(BF16),
                               w_out[0].astype(BF16), pad_taps(conv_w))
    full_in, full_sb, full_cv, full_out, full_conv = gathered

    x2 = x.reshape(t, D_MODEL)
    dhs, dr, big, small = _forward_backward(x2, loss_target.reshape(t, D_MODEL), nb, seq,
                                            full_in, full_sb, full_cv, full_out, full_conv, vecs)
    pair = _pair_sums(big)
    grad_x, st_in, got = _in_proj_bwd(dhs, full_in, dr, x2, vecs["ln_in_g"], pair)
    g_in, g_sb, g_cv, g_out = _finish_weight_grads(pair, got)
    small = _small_allreduce(jnp.concatenate([st_in[0:2], small], axis=0))

    chip = 2 * lax.axis_index("x") + lax.axis_index("y")
    g_conv = lax.dynamic_slice(small, (8, chip * SHARD_SQ), (CONV_ROWS, SHARD_SQ))
    loss = small[7, 0]

    grads, deltas, new_m, new_v = {}, {}, {}, {}
    stack = lambda d: jnp.concatenate([d[k].reshape(1, D_MODEL) for k in vec_names] + [jnp.zeros((1, D_MODEL), F32)])
    vd, vm, vv = _adamw(stack(vec_w), small[0:8], stack(vec_m), stack(vec_v), "adamw_vectors")
    for r, k in enumerate(vec_names):
        shape = vec_w[k].shape
        grads[k] = small[r].reshape(shape)
        deltas[k], new_m[k], new_v[k] = vd[r].reshape(shape), vm[r].reshape(shape), vv[r].reshape(shape)

    big_w = {"w_in": (w_in, g_in, m_w_in, v_w_in), "w_sb_proj": (w_sb_proj, g_sb, m_w_sb_proj, v_w_sb_proj),
             "w_cv_proj": (w_cv_proj, g_cv, m_w_cv_proj, v_w_cv_proj), "w_out": (w_out, g_out, m_w_out, v_w_out)}
    for k, (w, g, m, v) in big_w.items():
        d, mn, vn = _adamw(w[0], g, m[0], v[0], "adamw_" + k)
        grads[k], deltas[k], new_m[k], new_v[k] = g[None], d[None], mn[None], vn[None]
    d, mn, vn = _adamw(pad_taps(conv_w), g_conv, pad_taps(m_conv_w), pad_taps(v_conv_w), "adamw_conv_w")
    grads["conv_w"] = g_conv[None, :CONV_K]
    deltas["conv_w"], new_m["conv_w"], new_v["conv_w"] = d[None, :CONV_K], mn[None, :CONV_K], vn[None, :CONV_K]

    order = ("ln_in_g", "ln_in_b", "w_in", "w_sb_proj", "conv_w", "conv_b", "conv_ln_g", "conv_ln_b",
             "w_cv_proj", "w_out", "ln_post_g", "ln_post_b")
    return (loss, grad_x.reshape(x.shape), *[grads[k] for k in order], *[deltas[k] for k in order],
            *[new_m[k] for k in order], *[new_v[k] for k in order])
```

```python
import functools

import jax
import jax.numpy as jnp
from jax import lax
from jax.experimental import pallas as pl
from jax.experimental.pallas import tpu as pltpu

F32 = jnp.float32
BF16 = jnp.bfloat16

D_MODEL = 1024
N_HEADS = 16
HEAD_DIM = 64
HEAD_GROUP = 4
GROUP_W = HEAD_GROUP * HEAD_DIM
N_GROUPS = N_HEADS // HEAD_GROUP
N_PIECES = 9
IN_WIDTH = N_PIECES * D_MODEL
Q_BLOCK = 256
Q_TILE = 2 * Q_BLOCK
CONV_K = 31
CONV_ROWS = 32
HALO = 32
LN_EPS = 1e-5
ALPHA = 2.0 ** 0.25
Q_SCALE = 0.125
N_CHIPS = 4
N_DEV = 8
SHARD_IN = IN_WIDTH // N_CHIPS
SHARD_SQ = D_MODEL // N_CHIPS
HALF = D_MODEL // 2
SMALL_ROWS = 40
N_CHUNKS = 16

ADAM_LR = 0.001
ADAM_B1 = 0.9
ADAM_B2 = 0.999
ADAM_EPS = 1e-08
ADAM_WD = 0.01
ADAM_STEP = 10

MESH = pl.DeviceIdType.MESH
ANY = pl.BlockSpec(memory_space=pl.ANY)
VMEM_LIMIT = 60 * 1024 * 1024

NT = (((1,), (1,)), ((), ()))
TN = (((0,), (0,)), ((), ()))


def _sigmoid(x):
    return 1.0 / (1.0 + jnp.exp(-x))


def _dot(a, b):
    return jnp.dot(a, b, preferred_element_type=F32)


def _dot_nt(a, b):
    return lax.dot_general(a, b, NT, preferred_element_type=F32)


def _dot_tn(a, b):
    return lax.dot_general(a, b, TN, preferred_element_type=F32)


def _split_bf16(x):
    hi = x.astype(BF16)
    lo = (x - hi.astype(F32)).astype(BF16)
    return hi, lo


def _ln_stats(x):
    mu = jnp.mean(x, axis=-1, keepdims=True)
    xc = x - mu
    var = jnp.mean(xc * xc, axis=-1, keepdims=True)
    rstd = lax.rsqrt(var + LN_EPS)
    return xc * rstd, rstd


def _ln_bwd(dy, xhat, rstd, g):
    dxh = dy * g
    m1 = jnp.mean(dxh, axis=-1, keepdims=True)
    m2 = jnp.mean(dxh * xhat, axis=-1, keepdims=True)
    return rstd * (dxh - m1 - xhat * m2)


def _params(*sem):
    return pltpu.CompilerParams(dimension_semantics=sem, vmem_limit_bytes=VMEM_LIMIT)


def _ln_in(x, g, b):
    t = x.shape[0]
    tm = min(512, t)

    def body(x_ref, g_ref, b_ref, hf_ref, hb_ref):
        xhat, _ = _ln_stats(x_ref[...])
        y = xhat * g_ref[...] + b_ref[...]
        hf_ref[...] = y
        hb_ref[...] = y.astype(BF16)

    row = pl.BlockSpec((tm, D_MODEL), lambda i: (i, 0))
    vec = pl.BlockSpec((1, D_MODEL), lambda i: (0, 0))
    return pl.pallas_call(
        body, grid=(t // tm,), in_specs=[row, vec, vec], out_specs=[row, row],
        out_shape=[jax.ShapeDtypeStruct((t, D_MODEL), F32), jax.ShapeDtypeStruct((t, D_MODEL), BF16)],
        compiler_params=_params("parallel"), name="ln_in")(x, g, b)


def _in_proj(h, w, col0, ncol, out_dtype, scale_first, name):
    t = h.shape[0]
    tm = min(1024, t)

    def body(h_ref, w_ref, o_ref):
        res = _dot(h_ref[...], w_ref[...])
        if scale_first:
            res = res * jnp.where(pl.program_id(0) == 0, Q_SCALE, 1.0)
        o_ref[...] = res.astype(out_dtype)

    return pl.pallas_call(
        body, grid=(ncol, t // tm),
        in_specs=[pl.BlockSpec((tm, D_MODEL), lambda j, i: (i, 0)),
                  pl.BlockSpec((D_MODEL, D_MODEL), lambda j, i: (0, j + col0))],
        out_specs=pl.BlockSpec((tm, D_MODEL), lambda j, i: (i, j)),
        out_shape=jax.ShapeDtypeStruct((t, ncol * D_MODEL), out_dtype),
        compiler_params=_params("parallel", "parallel"), name=name)(h, w)


def _sb_block(l, diag, tri, carry):
    lnb = jnp.minimum(-l, 0.0) - jnp.log(1.0 + jnp.exp(-jnp.abs(l)))
    if diag is not None:
        lnb = jnp.where(diag, lnb, 0.0)
    hi, lo = _split_bf16(lnb)
    suffix = _dot(hi, tri) + _dot(lo, tri)
    a = jnp.exp(l + suffix + carry)
    if diag is not None:
        a = jnp.where(diag, a, 0.0)
    return a, lnb, suffix[:, 0:1]


def _attn_consts():
    head = lax.broadcasted_iota(jnp.int32, (Q_TILE, GROUP_W), 1) // HEAD_DIM
    row = lax.broadcasted_iota(jnp.int32, (Q_TILE, Q_BLOCK), 0)
    col = lax.broadcasted_iota(jnp.int32, (Q_TILE, Q_BLOCK), 1)
    square = slice(0, Q_BLOCK)
    return head, col < row, (row[square] >= col[square]).astype(BF16), (row[square] <= col[square]).astype(BF16)


def _split_heads(t, head):
    zero = jnp.zeros_like(t)
    head = head[:t.shape[0]]
    return [jnp.where(head == h, t, zero) for h in range(HEAD_GROUP)]


def _key_block(ref, kb):
    return ref[pl.ds(pl.multiple_of(kb * Q_BLOCK, Q_BLOCK), Q_BLOCK), :]


def _attn_fwd(qkv, nb, seq, w_in_early, shards):
    nq = seq // Q_TILE

    def body(q_ref, k_ref, v_ref, full_in, s0, s1, s2, s3, s4, o_ref, d0, d1, d2, d3, d4, send_sems, recv_sems, local_sems):
        del full_in
        qi = pl.program_id(2)
        first = (pl.program_id(0) == 0) & (pl.program_id(1) == 0) & (qi == 0)
        last = (pl.program_id(0) == nb - 1) & (pl.program_id(1) == N_GROUPS - 1) & (qi == nq - 1)
        start_gather, finish_gather = _late_gather([s0, s1, s2, s3, s4], [d0, d1, d2, d3, d4],
                                                   send_sems, recv_sems, local_sems)
        pl.when(first)(start_gather)
        head, diag, tri, _ = _attn_consts()
        qh = _split_heads(q_ref[...], head)
        o_ref[...] = jnp.zeros_like(o_ref)
        zero_col = jnp.zeros((Q_BLOCK, 1), F32)

        def block(kb, carries, rows, masked):
            kblk = _key_block(k_ref, kb)
            vstack = jnp.concatenate(_split_heads(_key_block(v_ref, kb), head), axis=0)
            mask = diag[:Q_TILE - rows.start] if masked else None
            weights, out = [], []
            for h in range(HEAD_GROUP):
                a, _, blk_sum = _sb_block(_dot_nt(qh[h][rows], kblk), mask, tri, carries[h][rows])
                weights.append(a.astype(BF16))
                out.append(carries[h][rows] + blk_sum)
            o_ref[rows, :] += _dot(jnp.concatenate(weights, axis=1), vstack)
            return out

        second = block(2 * qi + 1, [jnp.zeros((Q_TILE, 1), F32)] * HEAD_GROUP, slice(Q_BLOCK, Q_TILE), True)
        carries = tuple(jnp.concatenate([zero_col, c], axis=0) for c in second)
        everything = slice(0, Q_TILE)
        carries = tuple(block(2 * qi, carries, everything, True))
        lax.fori_loop(1, 2 * qi + 1, lambda jj, cs: tuple(block(2 * qi - jj, cs, everything, False)), carries)
        pl.when(last)(finish_gather)

    n_shards = len(shards)
    full = [jax.ShapeDtypeStruct((D_MODEL, IN_WIDTH), BF16)] + [jax.ShapeDtypeStruct((D_MODEL, D_MODEL), BF16)] * 3 \
        + [jax.ShapeDtypeStruct((CONV_ROWS, D_MODEL), F32)]
    res = pl.pallas_call(
        body, grid=(nb, N_GROUPS, nq),
        in_specs=[pl.BlockSpec((Q_TILE, GROUP_W), lambda b, g, i: (b * nq + i, g)),
                  pl.BlockSpec((seq, GROUP_W), lambda b, g, i: (b, N_GROUPS + g)),
                  pl.BlockSpec((seq, GROUP_W), lambda b, g, i: (b, 2 * N_GROUPS + g))] + [ANY] * (1 + n_shards),
        out_specs=[pl.BlockSpec((Q_TILE, GROUP_W), lambda b, g, i: (b * nq + i, g))] + [ANY] * n_shards,
        out_shape=[jax.ShapeDtypeStruct((nb * seq, D_MODEL), F32)] + full,
        input_output_aliases={3: 1},
        scratch_shapes=[pltpu.SemaphoreType.DMA((3 * n_shards,)), pltpu.SemaphoreType.DMA((3 * n_shards,)),
                        pltpu.SemaphoreType.DMA((n_shards - 1,))],
        compiler_params=_params("arbitrary", "arbitrary", "arbitrary"), name="attn_fwd")(
            qkv, qkv, qkv, w_in_early, *shards)
    return res[0], res[1:]


def _attn_bwd(qkv, do, nb, seq):
    nq = seq // Q_TILE
    n_kb = seq // Q_BLOCK

    def body(q_ref, k_ref, v_ref, do_ref, dq_ref, dk_ref, dv_ref, g_s, beta_s, dq_acc, dk_acc, dv_acc):
        qi = pl.program_id(2)

        @pl.when(qi == 0)
        def _():
            dk_acc[...] = jnp.zeros_like(dk_acc)
            dv_acc[...] = jnp.zeros_like(dv_acc)

        head, diag, tri_suffix, tri_prefix = _attn_consts()
        qh = _split_heads(q_ref[...], head)
        doh = _split_heads(do_ref[...], head)
        dq_acc[...] = jnp.zeros_like(dq_acc)
        zero_col = jnp.zeros((Q_BLOCK, 1), F32)
        everything = slice(0, Q_TILE)
        second = slice(Q_BLOCK, Q_TILE)

        def block_a(kb, carries, rows, masked):
            ks = pl.multiple_of(kb * Q_BLOCK, Q_BLOCK)
            kblk = _key_block(k_ref, kb)
            vblk = _key_block(v_ref, kb)
            mask = diag[:Q_TILE - rows.start] if masked else None
            weights, out = [], []
            for h in range(HEAD_GROUP):
                l = _dot_nt(qh[h][rows], kblk)
                a, lnb, blk_sum = _sb_block(l, mask, tri_suffix, carries[h][rows])
                beta = jnp.exp(l + lnb)
                if masked:
                    beta = jnp.where(mask, beta, 0.0)
                g_s[h, kb, rows, :] = a * _dot_nt(doh[h][rows], vblk)
                beta_s[h, kb, rows, :] = beta.astype(BF16)
                weights.append(a.astype(BF16))
                out.append(carries[h][rows] + blk_sum)
            dv_acc[pl.ds(ks, Q_BLOCK), :] += _dot_tn(jnp.concatenate(weights, axis=0),
                                                     jnp.concatenate([t[rows] for t in doh], axis=0))
            return out

        part = block_a(2 * qi + 1, [jnp.zeros((Q_TILE, 1), F32)] * HEAD_GROUP, second, True)
        carries = tuple(jnp.concatenate([zero_col, c], axis=0) for c in part)
        carries = tuple(block_a(2 * qi, carries, everything, True))
        lax.fori_loop(1, 2 * qi + 1, lambda jj, cs: tuple(block_a(2 * qi - jj, cs, everything, False)), carries)

        def block_b(kb, prefixes, rows, masked):
            ks = pl.multiple_of(kb * Q_BLOCK, Q_BLOCK)
            kstack = jnp.concatenate(_split_heads(_key_block(k_ref, kb), head), axis=0)
            dls, out = [], []
            for h in range(HEAD_GROUP):
                g = g_s[h, kb, rows, :]
                beta = beta_s[h, kb, rows, :].astype(F32)
                hi, lo = _split_bf16(g)
                incl = _dot(hi, tri_prefix) + _dot(lo, tri_prefix)
                before = incl - g + prefixes[h][rows]
                dl = g - beta * (g + before)
                if masked:
                    dl = jnp.where(diag[:Q_TILE - rows.start], dl, 0.0)
                dls.append(dl.astype(BF16))
                out.append(prefixes[h][rows] + incl[:, Q_BLOCK - 1:Q_BLOCK])
            dq_acc[rows, :] += _dot(jnp.concatenate(dls, axis=1), kstack)
            dk_acc[pl.ds(ks, Q_BLOCK), :] += _dot_tn(jnp.concatenate(dls, axis=0),
                                                     jnp.concatenate([t[rows] for t in qh], axis=0))
            return out

        zeros = (jnp.zeros((Q_TILE, 1), F32),) * HEAD_GROUP
        prefixes = lax.fori_loop(0, 2 * qi, lambda kb, ps: tuple(block_b(kb, ps, everything, False)), zeros)
        prefixes = block_b(2 * qi, prefixes, everything, True)
        block_b(2 * qi + 1, prefixes, second, True)
        dq_ref[...] = (dq_acc[...] * Q_SCALE).astype(BF16)

        @pl.when(qi == nq - 1)
        def _():
            dk_ref[...] = dk_acc[...].astype(BF16)
            dv_ref[...] = dv_acc[...].astype(BF16)

    t = nb * seq
    qspec = pl.BlockSpec((Q_TILE, GROUP_W), lambda b, g, i: (b * nq + i, g))
    kvout = pl.BlockSpec((seq, GROUP_W), lambda b, g, i: (b, g))
    return pl.pallas_call(
        body, grid=(nb, N_GROUPS, nq),
        in_specs=[qspec,
                  pl.BlockSpec((seq, GROUP_W), lambda b, g, i: (b, N_GROUPS + g)),
                  pl.BlockSpec((seq, GROUP_W), lambda b, g, i: (b, 2 * N_GROUPS + g)),
                  qspec],
        out_specs=[qspec, kvout, kvout],
        out_shape=[jax.ShapeDtypeStruct((t, D_MODEL), BF16)] * 3,
        scratch_shapes=[pltpu.VMEM((HEAD_GROUP, n_kb, Q_TILE, Q_BLOCK), F32),
                        pltpu.VMEM((HEAD_GROUP, n_kb, Q_TILE, Q_BLOCK), BF16),
                        pltpu.VMEM((Q_TILE, GROUP_W), F32),
                        pltpu.VMEM((seq, GROUP_W), F32), pltpu.VMEM((seq, GROUP_W), F32)],
        compiler_params=_params("parallel", "parallel", "arbitrary"), name="attn_bwd")(qkv, qkv, qkv, do)


def _conv_block_rows(seq):
    return min(256, seq)


SUBLANES = 8
CONV_SUB = 32
CONV_GROUPS = CONV_SUB // SUBLANES


def _fill_shifts(sh_ref, n):
    for r in range(1, SUBLANES):
        sh_ref[r, 0:n, :] = sh_ref[0, pl.ds(r, n), :]


def _shifted(sh_ref, start, cs):
    r = start % SUBLANES
    return sh_ref[r, start - r:start - r + CONV_SUB, cs].reshape(CONV_GROUPS, SUBLANES, -1)


def _tap(w_ref, k, cs):
    row = w_ref[k:k + 1, cs]
    return jnp.broadcast_to(row, (SUBLANES, row.shape[1]))[None]


def _tree_sum(x):
    parts = [x[i] for i in range(x.shape[0])]
    while len(parts) > 1:
        parts = [parts[i] + parts[i + 1] for i in range(0, len(parts), 2)]
    return parts[0]


def _conv_fwd(hrest, conv_w, conv_b, ln_g, ln_b, nb, seq):
    ts = _conv_block_rows(seq)
    nblk = seq // ts
    lanes = 128

    def body(cv_ref, cg_ref, cvh_ref, cgh_ref, z_ref, w_ref, cb_ref, g_ref, b_ref, u1_ref, cvin_ref, ush):
        i = pl.program_id(1)
        halo = cvh_ref[...] * _sigmoid(cgh_ref[...])
        ush[0, 0:HALO, :] = jnp.where(i > 0, halo, 0.0)
        ush[0, HALO:HALO + ts, :] = cv_ref[...] * _sigmoid(cg_ref[...])
        _fill_shifts(ush, HALO + ts - SUBLANES)
        for cc in range(D_MODEL // lanes):
            cs = slice(cc * lanes, (cc + 1) * lanes)
            for r0 in range(0, ts, CONV_SUB):
                acc = jnp.zeros((CONV_GROUPS, SUBLANES, lanes), F32) + _tap(cb_ref, 0, cs)
                for k in range(CONV_K):
                    acc = acc + _tap(w_ref, k, cs) * _shifted(ush, HALO - CONV_K + 1 + k + r0, cs)
                u1_ref[r0:r0 + CONV_SUB, cs] = acc.reshape(CONV_SUB, lanes)
        xhat, _ = _ln_stats(u1_ref[...])
        u2 = xhat * g_ref[...] + b_ref[...]
        z = z_ref[...]
        cvin_ref[...] = (u2 * _sigmoid(u2) * z * _sigmoid(z)).astype(BF16)

    def main(colblk):
        return pl.BlockSpec((ts, D_MODEL), lambda b, i: (b * nblk + i, colblk))

    def halo(colblk):
        return pl.BlockSpec((HALO, D_MODEL),
                            lambda b, i: (jnp.maximum((b * seq + i * ts) // HALO - 1, 0), colblk))

    vec = pl.BlockSpec((1, D_MODEL), lambda b, i: (0, 0))
    t = nb * seq
    return pl.pallas_call(
        body, grid=(nb, nblk),
        in_specs=[main(1), main(2), halo(1), halo(2), main(3),
                  pl.BlockSpec((CONV_ROWS, D_MODEL), lambda b, i: (0, 0)), vec, vec, vec],
        out_specs=[main(0), main(0)],
        out_shape=[jax.ShapeDtypeStruct((t, D_MODEL), F32), jax.ShapeDtypeStruct((t, D_MODEL), BF16)],
        scratch_shapes=[pltpu.VMEM((SUBLANES, HALO + ts, D_MODEL), F32)],
        compiler_params=_params("parallel", "parallel"), name="conv_fwd")(
            hrest, hrest, hrest, hrest, hrest, conv_w, conv_b, ln_g, ln_b)


def _tail(o, hrest, cvin, h0, tgt, w_sb, w_cv, w_out, ln_g, ln_b):
    t = o.shape[0]
    tm = min(256, t)

    def body(o_ref, z_ref, gs_ref, gc_ref, cvin_ref, h0_ref, tgt_ref, wsb_ref, wcv_ref, wout_ref, g_ref, b_ref,
             dr_ref, drb_ref, a_ref, mg_ref, dysb_ref, dycv_ref, do_ref, dz_ref, dgs_ref, dgc_ref, dcvin_ref,
             st_ref):
        @pl.when(pl.program_id(0) == 0)
        def _():
            st_ref[...] = jnp.zeros_like(st_ref)

        o = o_ref[...]
        z = z_ref[...]
        sz = _sigmoid(z)
        a = (o * z * sz).astype(BF16)
        a_ref[...] = a
        y_sb = _dot(a, wsb_ref[...])
        y_cv = _dot(cvin_ref[...], wcv_ref[...])
        s_sb = _sigmoid(gs_ref[...])
        s_cv = _sigmoid(gc_ref[...])
        merged = (s_sb * y_sb + s_cv * y_cv).astype(BF16)
        mg_ref[...] = merged
        r = ALPHA * h0_ref[...] + _dot(merged, wout_ref[...])
        xhat, rstd = _ln_stats(r)
        g = g_ref[...]
        err = xhat * g + b_ref[...] - tgt_ref[...]
        dy = err * (1.0 / D_MODEL)
        st_ref[0:1, :] += jnp.sum(dy * xhat, axis=0, keepdims=True)
        st_ref[1:2, :] += jnp.sum(dy, axis=0, keepdims=True)
        st_ref[2:3, :] += (0.5 / D_MODEL) * jnp.sum(err * err, axis=0, keepdims=True)
        dr = _ln_bwd(dy, xhat, rstd, g)
        dr_ref[...] = dr
        drb = dr.astype(BF16)
        drb_ref[...] = drb
        dm = _dot_nt(drb, wout_ref[...])
        dy_sb = (dm * s_sb).astype(BF16)
        dy_cv = (dm * s_cv).astype(BF16)
        dysb_ref[...] = dy_sb
        dycv_ref[...] = dy_cv
        dgs_ref[...] = (dm * y_sb * s_sb * (1.0 - s_sb)).astype(BF16)
        dgc_ref[...] = (dm * y_cv * s_cv * (1.0 - s_cv)).astype(BF16)
        da = _dot_nt(dy_sb, wsb_ref[...])
        dcvin_ref[...] = _dot_nt(dy_cv, wcv_ref[...])
        do_ref[...] = (da * z * sz).astype(BF16)
        dz_ref[...] = (da * o * sz * (1.0 + z * (1.0 - sz))).astype(BF16)

    def tok(colblk=0):
        return pl.BlockSpec((tm, D_MODEL), lambda i: (i, colblk))

    wspec = pl.BlockSpec((D_MODEL, D_MODEL), lambda i: (0, 0), pipeline_mode=pl.Buffered(1))
    vec = pl.BlockSpec((1, D_MODEL), lambda i: (0, 0))
    bf = jax.ShapeDtypeStruct((t, D_MODEL), BF16)
    f32 = jax.ShapeDtypeStruct((t, D_MODEL), F32)
    return pl.pallas_call(
        body, grid=(t // tm,),
        in_specs=[tok(), tok(0), tok(4), tok(5), tok(), tok(), tok(), wspec, wspec, wspec, vec, vec],
        out_specs=[tok()] * 11 + [pl.BlockSpec((8, D_MODEL), lambda i: (0, 0))],
        out_shape=[f32] + [bf] * 9 + [f32, jax.ShapeDtypeStruct((8, D_MODEL), F32)],
        compiler_params=_params("arbitrary"), name="tail")(
            o, hrest, hrest, hrest, cvin, h0, tgt, w_sb, w_cv, w_out, ln_g, ln_b)


def _conv_branch_bwd(dcvin, u1, hrest, ln_g, ln_b):
    t = u1.shape[0]
    tm = min(512, t)

    def body(dc_ref, u1_ref, z_ref, g_ref, b_ref, du1_ref, dz_ref, st_ref):
        @pl.when(pl.program_id(0) == 0)
        def _():
            st_ref[...] = jnp.zeros_like(st_ref)

        xhat, rstd = _ln_stats(u1_ref[...])
        g = g_ref[...]
        u2 = xhat * g + b_ref[...]
        s2 = _sigmoid(u2)
        z = z_ref[...]
        sz = _sigmoid(z)
        dc = dc_ref[...]
        dz_ref[...] = (dc * (u2 * s2) * sz * (1.0 + z * (1.0 - sz))).astype(BF16)
        du2 = dc * (z * sz) * s2 * (1.0 + u2 * (1.0 - s2))
        st_ref[0:1, :] += jnp.sum(du2 * xhat, axis=0, keepdims=True)
        st_ref[1:2, :] += jnp.sum(du2, axis=0, keepdims=True)
        du1 = _ln_bwd(du2, xhat, rstd, g)
        du1_ref[...] = du1
        st_ref[2:3, :] += jnp.sum(du1, axis=0, keepdims=True)

    tok = pl.BlockSpec((tm, D_MODEL), lambda i: (i, 0))
    vec = pl.BlockSpec((1, D_MODEL), lambda i: (0, 0))
    return pl.pallas_call(
        body, grid=(t // tm,),
        in_specs=[tok, tok, pl.BlockSpec((tm, D_MODEL), lambda i: (i, 3)), vec, vec],
        out_specs=[tok, tok, pl.BlockSpec((8, D_MODEL), lambda i: (0, 0))],
        out_shape=[jax.ShapeDtypeStruct((t, D_MODEL), F32), jax.ShapeDtypeStruct((t, D_MODEL), BF16),
                   jax.ShapeDtypeStruct((8, D_MODEL), F32)],
        compiler_params=_params("arbitrary"), name="conv_branch_bwd")(dcvin, u1, hrest, ln_g, ln_b)


def _conv_bwd(du1, hrest, conv_w, nb, seq):
    ts = _conv_block_rows(seq)
    nblk = seq // ts
    lanes = 128
    last_halo = nb * seq // HALO - 1

    def body(du_ref, duh_ref, cv_ref, cg_ref, cvh_ref, cgh_ref, w_ref, dcv_ref, dcg_ref, dw_ref, ush, dsh, dw_part):
        b = pl.program_id(0)
        i = pl.program_id(1)

        @pl.when((b == 0) & (i == 0))
        def _():
            dw_part[...] = jnp.zeros_like(dw_part)

        cv = cv_ref[...]
        sg = _sigmoid(cg_ref[...])
        halo = cvh_ref[...] * _sigmoid(cgh_ref[...])
        ush[0, 0:HALO, :] = jnp.where(i > 0, halo, 0.0)
        ush[0, HALO:HALO + ts, :] = cv * sg
        dsh[0, 0:ts, :] = du_ref[...]
        dsh[0, ts:ts + HALO, :] = jnp.where(i < nblk - 1, duh_ref[...], 0.0)
        _fill_shifts(ush, HALO + ts - SUBLANES)
        _fill_shifts(dsh, HALO + ts - SUBLANES)
        for cc in range(D_MODEL // lanes):
            cs = slice(cc * lanes, (cc + 1) * lanes)
            for r0 in range(0, ts, CONV_SUB):
                rows = slice(r0, r0 + CONV_SUB)
                dmain = _shifted(dsh, r0, cs)
                acc = jnp.zeros((CONV_GROUPS, SUBLANES, lanes), F32)
                for k in range(CONV_K):
                    acc = acc + _tap(w_ref, k, cs) * _shifted(dsh, CONV_K - 1 - k + r0, cs)
                    dw_part[k, :, cs] += _tree_sum(dmain * _shifted(ush, HALO - CONV_K + 1 + k + r0, cs))
                acc = acc.reshape(CONV_SUB, lanes)
                dcv_ref[rows, cs] = (acc * sg[rows, cs]).astype(BF16)
                dcg_ref[rows, cs] = (acc * cv[rows, cs] * sg[rows, cs] * (1.0 - sg[rows, cs])).astype(BF16)

        @pl.when((b == nb - 1) & (i == nblk - 1))
        def _():
            for k in range(CONV_ROWS):
                dw_ref[k:k + 1, :] = jnp.sum(dw_part[k], axis=0, keepdims=True)

    def main(colblk):
        return pl.BlockSpec((ts, D_MODEL), lambda b, i: (b * nblk + i, colblk))

    def halo_before(colblk):
        return pl.BlockSpec((HALO, D_MODEL),
                            lambda b, i: (jnp.maximum((b * seq + i * ts) // HALO - 1, 0), colblk))

    halo_after = pl.BlockSpec((HALO, D_MODEL),
                              lambda b, i: (jnp.minimum((b * seq + (i + 1) * ts) // HALO, last_halo), 0))
    t = nb * seq
    return pl.pallas_call(
        body, grid=(nb, nblk),
        in_specs=[main(0), halo_after, main(1), main(2), halo_before(1), halo_before(2),
                  pl.BlockSpec((CONV_ROWS, D_MODEL), lambda b, i: (0, 0))],
        out_specs=[main(0), main(0), pl.BlockSpec((CONV_ROWS, D_MODEL), lambda b, i: (0, 0))],
        out_shape=[jax.ShapeDtypeStruct((t, D_MODEL), BF16), jax.ShapeDtypeStruct((t, D_MODEL), BF16),
                   jax.ShapeDtypeStruct((CONV_ROWS, D_MODEL), F32)],
        scratch_shapes=[pltpu.VMEM((SUBLANES, HALO + ts, D_MODEL), F32), pltpu.VMEM((SUBLANES, ts + HALO, D_MODEL), F32),
                        pltpu.VMEM((CONV_ROWS, SUBLANES, D_MODEL), F32)],
        compiler_params=_params("arbitrary", "arbitrary"), name="conv_bwd")(
            du1, du1, hrest, hrest, hrest, hrest, conv_w)


def _weight_grad(x, dys, name):
    n = len(dys)
    t = x.shape[0]
    tm = min(512, t)

    def body(x_ref, *refs):
        o_ref = refs[n]
        j = pl.program_id(0)

        @pl.when(pl.program_id(1) == 0)
        def _():
            o_ref[...] = jnp.zeros_like(o_ref)

        for p in range(n):
            @pl.when(j == p)
            def _(p=p):
                o_ref[...] += _dot_tn(x_ref[...], refs[p][...])

    def dy_spec(p):
        return pl.BlockSpec((tm, D_MODEL), lambda j, i: (jnp.where(j == p, i, 0), 0))

    return pl.pallas_call(
        body, grid=(n, t // tm),
        in_specs=[pl.BlockSpec((tm, D_MODEL), lambda j, i: (i, 0))] + [dy_spec(p) for p in range(n)],
        out_specs=pl.BlockSpec((D_MODEL, D_MODEL), lambda j, i: (0, j)),
        out_shape=jax.ShapeDtypeStruct((D_MODEL, n * D_MODEL), F32),
        compiler_params=_params("parallel", "arbitrary"), name=name)(x, *dys)


def _in_proj_bwd(dhs, w_in, dr, x, ln_g, pair):
    n = len(dhs)
    t = x.shape[0]
    tm = min(512, t)
    n_i = t // tm

    def body(*refs):
        dh_refs = refs[:n]
        w_ref, dr_ref, x_ref, g_ref = refs[n:n + 4]
        pair_refs = refs[n + 4:n + 8]
        gx_ref, st_ref = refs[n + 8:n + 10]
        got_refs = refs[n + 10:n + 14]
        acc, send_sems, recv_sems = refs[n + 14:]
        i = pl.program_id(0)
        p = pl.program_id(1)
        pieces, wholes = _chip_exchange_copies(pair_refs, got_refs, send_sems, recv_sems)

        @pl.when((i == 0) & (p == 0))
        def _():
            st_ref[...] = jnp.zeros_like(st_ref)
            for cp in pieces:
                cp.start()

        @pl.when((i == n_i - 1) & (p == n - 1))
        def _():
            for cp in wholes:
                cp.wait_recv()
            for cp in wholes:
                cp.wait_send()

        @pl.when(p == 0)
        def _():
            acc[...] = ALPHA * dr_ref[...]

        for q in range(n):
            @pl.when(p == q)
            def _(q=q):
                acc[...] += _dot_nt(dh_refs[q][...], w_ref[...])

        @pl.when(p == n - 1)
        def _():
            xhat, rstd = _ln_stats(x_ref[...])
            dh0 = acc[...]
            st_ref[0:1, :] += jnp.sum(dh0 * xhat, axis=0, keepdims=True)
            st_ref[1:2, :] += jnp.sum(dh0, axis=0, keepdims=True)
            gx_ref[...] = _ln_bwd(dh0, xhat, rstd, g_ref[...])

    tok = pl.BlockSpec((tm, D_MODEL), lambda i, p: (i, 0))
    got = [jax.ShapeDtypeStruct((3, HALF, SHARD_IN), BF16)] + [jax.ShapeDtypeStruct((3, SHARD_SQ, HALF), BF16)] * 3
    res = pl.pallas_call(
        body, grid=(n_i, n),
        in_specs=[tok] * n + [pl.BlockSpec((D_MODEL, D_MODEL), lambda i, p: (0, p)), tok, tok,
                              pl.BlockSpec((1, D_MODEL), lambda i, p: (0, 0))] + [ANY] * 4,
        out_specs=[tok, pl.BlockSpec((8, D_MODEL), lambda i, p: (0, 0))] + [ANY] * 4,
        out_shape=[jax.ShapeDtypeStruct((t, D_MODEL), F32), jax.ShapeDtypeStruct((8, D_MODEL), F32)] + got,
        scratch_shapes=[pltpu.VMEM((tm, D_MODEL), F32), pltpu.SemaphoreType.DMA((12,)), pltpu.SemaphoreType.DMA((12,))],
        compiler_params=_params("arbitrary", "arbitrary"), name="in_proj_bwd")(*dhs, w_in, dr, x, ln_g, *pair)
    return res[0], res[1], res[2:]


def _forward_backward(x, tgt, nb, seq, shards, vecs):
    w_in_early = _gather_first(shards[0])
    h0, h0b = _ln_in(x, vecs["ln_in_g"], vecs["ln_in_b"])
    qkv = _in_proj(h0b, w_in_early, 0, 3, BF16, True, "in_proj_qkv")
    o, (w_in, w_sb, w_cv, w_out, conv_w) = _attn_fwd(qkv, nb, seq, w_in_early, shards)
    hrest = _in_proj(h0b, w_in, 3, 6, F32, False, "in_proj_rest")
    u1, cvin = _conv_fwd(hrest, conv_w, vecs["conv_b"], vecs["conv_ln_g"], vecs["conv_ln_b"], nb, seq)
    (dr, drb, a, merged, dy_sb, dy_cv, do, dz_sb, dg_sb, dg_cv, dcvin, st_tail) = _tail(
        o, hrest, cvin, h0, tgt, w_sb, w_cv, w_out, vecs["ln_post_g"], vecs["ln_post_b"])
    d_w_sb = _weight_grad(a, [dy_sb], "grad_w_sb")
    d_w_cv = _weight_grad(cvin, [dy_cv], "grad_w_cv")
    d_w_out = _weight_grad(merged, [drb], "grad_w_out")
    du1, dz_cv, st_conv = _conv_branch_bwd(dcvin, u1, hrest, vecs["conv_ln_g"], vecs["conv_ln_b"])
    dc_val, dc_gate, d_conv_w = _conv_bwd(du1, hrest, conv_w, nb, seq)
    dq, dk, dv = _attn_bwd(qkv, do, nb, seq)
    dhs = [dq, dk, dv, dz_sb, dc_val, dc_gate, dz_cv, dg_sb, dg_cv]
    d_w_in = _weight_grad(h0b, dhs, "grad_w_in")
    small = jnp.concatenate([st_conv[2:3], st_conv[0:2], st_tail[0:3], d_conv_w], axis=0)
    return w_in, dhs, dr, (d_w_in, d_w_sb, d_w_cv, d_w_out), small


def _place():
    x, y, c = lax.axis_index("x"), lax.axis_index("y"), lax.axis_index("c")
    chips = [(1 - x, y), (x, 1 - y), (1 - x, 1 - y)]
    return x, y, c, chips


N_EARLY = -(-3 * D_MODEL // SHARD_IN)
GATHER_PARTS = 4


def _w_in_slot(full, chip):
    return full.at[:, pl.ds(pl.multiple_of(chip * SHARD_IN, 128), SHARD_IN)]


def _gather_first(w_in):
    part_rows = HALF // GATHER_PARTS
    n_sem = 3 * GATHER_PARTS

    def body(src, dst, ici_send, ici_recv, fwd_send, fwd_recv, local_sem):
        x, y, c, chips = _place()
        me = 2 * x + y
        sibling = (x, y, 1 - c)

        def part(ref, k, half):
            return ref.at[pl.ds(pl.multiple_of(half * HALF + k * part_rows, 16), part_rows), :]

        def sent(j, k):
            px, py = chips[j]
            return _remote(part(src, k, c), part(_w_in_slot(dst, me), k, c), ici_send.at[j * GATHER_PARTS + k],
                           ici_recv.at[j * GATHER_PARTS + k], (px, py, c))

        def landed(j, k):
            px, py = chips[j]
            here = part(_w_in_slot(dst, 2 * px + py), k, c)
            return _remote(here, here, ici_send.at[j * GATHER_PARTS + k], ici_recv.at[j * GATHER_PARTS + k], (px, py, c))

        def passed(j, k, half):
            px, py = chips[j]
            here = part(_w_in_slot(dst, 2 * px + py), k, half)
            return _remote(here, here, fwd_send.at[j * GATHER_PARTS + k], fwd_recv.at[j * GATHER_PARTS + k], sibling)

        local = pltpu.make_async_copy(src, _w_in_slot(dst, me), local_sem)
        local.start()

        @pl.when(me < N_EARLY)
        def _():
            for j in range(3):
                for k in range(GATHER_PARTS):
                    sent(j, k).start()

        for j, (px, py) in enumerate(chips):
            @pl.when(2 * px + py < N_EARLY)
            def _(j=j):
                for k in range(GATHER_PARTS):
                    landed(j, k).wait_recv()
                    passed(j, k, c).start()

        for j, (px, py) in enumerate(chips):
            @pl.when(2 * px + py < N_EARLY)
            def _(j=j):
                for k in range(GATHER_PARTS):
                    passed(j, k, 1 - c).wait_recv()
                for k in range(GATHER_PARTS):
                    passed(j, k, c).wait_send()

        @pl.when(me < N_EARLY)
        def _():
            for j in range(3):
                for k in range(GATHER_PARTS):
                    sent(j, k).wait_send()

        local.wait()

    return pl.pallas_call(
        body, in_specs=[ANY], out_specs=ANY, out_shape=jax.ShapeDtypeStruct((D_MODEL, IN_WIDTH), BF16),
        scratch_shapes=[pltpu.SemaphoreType.DMA((n_sem,)), pltpu.SemaphoreType.DMA((n_sem,)),
                        pltpu.SemaphoreType.DMA((n_sem,)), pltpu.SemaphoreType.DMA((n_sem,)),
                        pltpu.SemaphoreType.DMA],
        name="gather_first")(w_in)


def _late_gather(srcs, dsts, send_sems, recv_sems, local_sems):
    x, y, c, chips = _place()
    me = 2 * x + y
    small = range(1, 5)

    def slot(a, chip):
        if a == 0:
            return _w_in_slot(dsts[0], chip)
        if a == 4:
            return dsts[a].at[:, pl.ds(pl.multiple_of(chip * SHARD_SQ, 128), SHARD_SQ)]
        return dsts[a].at[pl.ds(pl.multiple_of(chip * SHARD_SQ, 16), SHARD_SQ), :]

    def sems(a, j):
        return send_sems.at[3 * a + j], recv_sems.at[3 * a + j]

    def whole(a, j):
        px, py = chips[j]
        return _remote(srcs[a], slot(a, me), *sems(a, j), (px, py, c))

    def arrival(a, j):
        px, py = chips[j]
        return _arrival(slot(a, 2 * px + py), *sems(a, j), (px, py, c))

    def local(a):
        return pltpu.make_async_copy(srcs[a], slot(a, me), local_sems.at[a - 1])

    def start():
        for a in small:
            local(a).start()
            for j in range(3):
                whole(a, j).start()

        @pl.when(me >= N_EARLY)
        def _():
            for j, (px, py) in enumerate(chips):
                for cp in _chunk_copies(srcs[0], slot(0, me), GATHER_PARTS, *sems(0, j), (px, py, c)):
                    cp.start()

    def finish():
        for j, (px, py) in enumerate(chips):
            for a in small:
                arrival(a, j).wait_recv()

            @pl.when(2 * px + py >= N_EARLY)
            def _(j=j):
                arrival(0, j).wait_recv()

        for a in small:
            for j in range(3):
                whole(a, j).wait_send()
            local(a).wait()

        @pl.when(me >= N_EARLY)
        def _():
            for j in range(3):
                whole(0, j).wait_send()

    return start, finish


def _remote(src, dst, send_sem, recv_sem, device):
    return pltpu.make_async_remote_copy(src_ref=src, dst_ref=dst, send_sem=send_sem, recv_sem=recv_sem,
                                        device_id=device, device_id_type=MESH)


def _chunk_copies(src, dst, n, send_sem, recv_sem, device):
    rows = src.shape[0] // n
    return [_remote(src.at[pl.ds(k * rows, rows)], dst.at[pl.ds(k * rows, rows)], send_sem, recv_sem, device)
            for k in range(n)]


def _row_chunks(src, dst, n, send_sem, recv_sem, device):
    return _chunk_copies(src, dst, n, send_sem, recv_sem, device), _remote(src, dst, send_sem, recv_sem, device)


def _arrival(dst, send_sem, recv_sem, device):
    return _remote(dst, dst, send_sem, recv_sem, device)


def _pair_exchange(d_in, d_sb, d_cv, d_out):
    def body(i0, i1, i2, i3, r0, r1, r2, r3, send_sems, recv_sems):
        x, y, c, _ = _place()
        srcs = [i0, i1, i2, i3]
        theirs = [r0, r1, r2, r3]
        sibling = (x, y, 1 - c)
        start = pl.multiple_of((1 - c) * HALF, 128)
        views = [srcs[0].at[pl.ds(start, HALF), :]] + [srcs[a].at[:, pl.ds(start, HALF)] for a in range(1, 4)]
        pieces, wholes = [], []
        for a in range(4):
            cps, whole = _row_chunks(views[a], theirs[a], N_CHUNKS, send_sems.at[a], recv_sems.at[a], sibling)
            pieces += cps
            wholes.append(whole)
        for cp in pieces:
            cp.start()
        for cp in wholes:
            cp.wait_recv()
        for cp in wholes:
            cp.wait_send()

    shapes = [jax.ShapeDtypeStruct((HALF, IN_WIDTH), F32)] + [jax.ShapeDtypeStruct((D_MODEL, HALF), F32)] * 3
    return pl.pallas_call(
        body, in_specs=[ANY] * 4, out_specs=[ANY] * 4, out_shape=shapes,
        scratch_shapes=[pltpu.SemaphoreType.DMA((4,)), pltpu.SemaphoreType.DMA((4,))],
        name="pair_exchange")(d_in, d_sb, d_cv, d_out)


def _chip_exchange_copies(srcs, theirs, send_sems, recv_sems):
    _, _, c, chips = _place()

    def shard(a, px, py):
        chip = 2 * px + py
        if a == 0:
            return srcs[a].at[:, pl.ds(pl.multiple_of(chip * SHARD_IN, 128), SHARD_IN)]
        return srcs[a].at[pl.ds(pl.multiple_of(chip * SHARD_SQ, 16), SHARD_SQ), :]

    pieces, wholes = [], []
    for j, (px, py) in enumerate(chips):
        for a in range(4):
            cps, whole = _row_chunks(shard(a, px, py), theirs[a].at[j], 4, send_sems.at[3 * a + j],
                                     recv_sems.at[3 * a + j], (px, py, c))
            pieces += cps
            wholes.append(whole)
    return pieces, wholes


def _pair_share(f_in, f_sb, f_cv, f_out):
    def body(i0, i1, i2, i3, o0, o1, o2, o3, send_sems, recv_sems):
        del i0, i1, i2, i3
        x, y, c, _ = _place()
        dsts = [o0, o1, o2, o3]
        sibling = (x, y, 1 - c)

        def half(a, which):
            start = pl.multiple_of(which * HALF, 128)
            if a == 0:
                return dsts[a].at[pl.ds(start, HALF), :]
            return dsts[a].at[:, pl.ds(start, HALF)]

        pieces, sent, arrived = [], [], []
        for a in range(4):
            cps, whole = _row_chunks(half(a, c), half(a, c), N_CHUNKS, send_sems.at[a], recv_sems.at[a], sibling)
            pieces += cps
            sent.append(whole)
            arrived.append(_arrival(half(a, 1 - c), send_sems.at[a], recv_sems.at[a], sibling))
        for cp in pieces:
            cp.start()
        for cp in arrived:
            cp.wait_recv()
        for cp in sent:
            cp.wait_send()

    return pl.pallas_call(
        body, in_specs=[ANY] * 4, out_specs=[ANY] * 4,
        out_shape=[jax.ShapeDtypeStruct((D_MODEL, SHARD_IN), F32)] + [jax.ShapeDtypeStruct((SHARD_SQ, D_MODEL), F32)] * 3,
        input_output_aliases={0: 0, 1: 1, 2: 2, 3: 3},
        scratch_shapes=[pltpu.SemaphoreType.DMA((4,)), pltpu.SemaphoreType.DMA((4,))],
        name="pair_share")(f_in, f_sb, f_cv, f_out)


def _small_allreduce(small):
    def body(s_ref, o_ref, slots, send_sems, recv_sems):
        x, y, c, _ = _place()
        me = 4 * x + 2 * y + c
        slots[me] = s_ref[...]
        sends = []
        for k in range(1, N_DEV):
            px, py, pc = (x + (k >> 2)) % 2, (y + ((k >> 1) & 1)) % 2, (c + (k & 1)) % 2
            sends.append(pltpu.make_async_remote_copy(
                src_ref=s_ref, dst_ref=slots.at[me], send_sem=send_sems.at[k - 1], recv_sem=recv_sems.at[k - 1],
                device_id=(px, py, pc), device_id_type=MESH))
        for cp in sends:
            cp.start()
        for k in range(1, N_DEV):
            px, py, pc = (x + (k >> 2)) % 2, (y + ((k >> 1) & 1)) % 2, (c + (k & 1)) % 2
            pltpu.make_async_remote_copy(
                src_ref=s_ref, dst_ref=slots.at[4 * px + 2 * py + pc], send_sem=send_sems.at[k - 1],
                recv_sem=recv_sems.at[k - 1], device_id=(px, py, pc), device_id_type=MESH).wait_recv()
        for cp in sends:
            cp.wait_send()
        total = slots[0]
        for d in range(1, N_DEV):
            total = total + slots[d]
        o_ref[...] = total
        o_ref[7:8, :] = jnp.zeros((1, D_MODEL), F32) + jnp.sum(total[7:8, :], axis=1, keepdims=True)

    return pl.pallas_call(
        body, in_specs=[pl.BlockSpec(memory_space=pltpu.VMEM)], out_specs=pl.BlockSpec(memory_space=pltpu.VMEM),
        out_shape=jax.ShapeDtypeStruct((SMALL_ROWS, D_MODEL), F32),
        scratch_shapes=[pltpu.VMEM((N_DEV, SMALL_ROWS, D_MODEL), F32), pltpu.SemaphoreType.DMA((N_DEV - 1,)),
                        pltpu.SemaphoreType.DMA((N_DEV - 1,))],
        name="small_allreduce")(small)


SUM_BLOCKS = 8


def _pair_sum(place, full, theirs, by_rows, name):
    rows, cols = theirs.shape
    tr = rows // SUM_BLOCKS

    def body(p_ref, a_ref, b_ref, o_ref):
        del p_ref
        o_ref[...] = (a_ref[...] + b_ref[...]).astype(BF16)

    mine = (lambda i, p: (p[1] * SUM_BLOCKS + i, 0)) if by_rows else (lambda i, p: (i, p[1]))
    spec = pl.BlockSpec((tr, cols), lambda i, p: (i, 0))
    return pl.pallas_call(
        body, out_shape=jax.ShapeDtypeStruct(theirs.shape, BF16),
        grid_spec=pltpu.PrefetchScalarGridSpec(num_scalar_prefetch=1, grid=(SUM_BLOCKS,),
                                               in_specs=[pl.BlockSpec((tr, cols), mine), spec], out_specs=spec),
        compiler_params=_params("parallel"), name=name)(place, full, theirs)


def _chip_sum(place, pair, got, by_rows, name):
    _, rows, cols = got.shape
    tr = rows // SUM_BLOCKS

    def body(p_ref, a_ref, g_ref, o_ref):
        del p_ref
        o_ref[...] = ((a_ref[...].astype(F32) + g_ref[0].astype(F32)) + g_ref[1].astype(F32)) + g_ref[2].astype(F32)

    if by_rows:
        own = lambda i, p: (i, p[0])
        out = lambda i, p: (p[1] * SUM_BLOCKS + i, 0)
        full = (2 * rows, cols)
    else:
        own = lambda i, p: (p[0] * SUM_BLOCKS + i, 0)
        out = lambda i, p: (i, p[1])
        full = (rows, 2 * cols)
    return pl.pallas_call(
        body, out_shape=jax.ShapeDtypeStruct(full, F32),
        grid_spec=pltpu.PrefetchScalarGridSpec(
            num_scalar_prefetch=1, grid=(SUM_BLOCKS,),
            in_specs=[pl.BlockSpec((tr, cols), own), pl.BlockSpec((3, tr, cols), lambda i, p: (0, i, 0))],
            out_specs=pl.BlockSpec((tr, cols), out)),
        compiler_params=_params("parallel"), name=name)(place, pair, got)


def _adamw(w, g, m, v, name):
    rows, cols = w.shape
    tr = rows // 4 if rows % 32 == 0 else rows
    c1 = 1.0 - ADAM_B1 ** ADAM_STEP
    c2 = 1.0 - ADAM_B2 ** ADAM_STEP

    def body(w_ref, g_ref, m_ref, v_ref, d_ref, mo_ref, vo_ref):
        g = g_ref[...]
        mn = ADAM_B1 * m_ref[...] + (1.0 - ADAM_B1) * g
        vn = ADAM_B2 * v_ref[...] + (1.0 - ADAM_B2) * (g * g)
        mo_ref[...] = mn
        vo_ref[...] = vn
        d_ref[...] = -ADAM_LR * ((mn / c1) / (jnp.sqrt(vn / c2) + ADAM_EPS) + ADAM_WD * w_ref[...])

    spec = pl.BlockSpec((tr, cols), lambda i: (i, 0))
    shape = jax.ShapeDtypeStruct(w.shape, F32)
    return pl.pallas_call(body, grid=(rows // tr,), in_specs=[spec] * 4, out_specs=[spec] * 3,
                          out_shape=[shape] * 3, compiler_params=_params("parallel"), name=name)(w, g, m, v)


GRAD_NAMES = ("in", "sb", "cv", "out")


def _place_scalars():
    return jnp.stack([2 * lax.axis_index("x") + lax.axis_index("y"), lax.axis_index("c")]).astype(jnp.int32)


def _pair_sums(grads):
    place = _place_scalars()
    theirs = _pair_exchange(*grads)
    return [_pair_sum(place, grads[a], theirs[a], a == 0, "pair_sum_" + GRAD_NAMES[a]) for a in range(4)]


def _finish_weight_grads(pair, got):
    place = _place_scalars()
    done = [_chip_sum(place, pair[a], got[a], a == 0, "chip_sum_" + GRAD_NAMES[a]) for a in range(4)]
    return _pair_share(*done)


def kernel(x, ln_in_g, ln_in_b, w_in, w_sb_proj, conv_w, conv_b, conv_ln_g, conv_ln_b, w_cv_proj, w_out, ln_post_g, ln_post_b, loss_target, m_ln_in_g, m_ln_in_b, m_w_in, m_w_sb_proj, m_conv_w, m_conv_b, m_conv_ln_g, m_conv_ln_b, m_w_cv_proj, m_w_out, m_ln_post_g, m_ln_post_b, v_ln_in_g, v_ln_in_b, v_w_in, v_w_sb_proj, v_conv_w, v_conv_b, v_conv_ln_g, v_conv_ln_b, v_w_cv_proj, v_w_out, v_ln_post_g, v_ln_post_b):
    nb, seq, _ = x.shape
    t = nb * seq
    vec_names = ("ln_in_g", "ln_in_b", "conv_b", "conv_ln_g", "conv_ln_b", "ln_post_g", "ln_post_b")
    vec_w = dict(zip(vec_names, (ln_in_g, ln_in_b, conv_b, conv_ln_g, conv_ln_b, ln_post_g, ln_post_b)))
    vec_m = dict(zip(vec_names, (m_ln_in_g, m_ln_in_b, m_conv_b, m_conv_ln_g, m_conv_ln_b, m_ln_post_g, m_ln_post_b)))
    vec_v = dict(zip(vec_names, (v_ln_in_g, v_ln_in_b, v_conv_b, v_conv_ln_g, v_conv_ln_b, v_ln_post_g, v_ln_post_b)))
    vecs = {k: a.reshape(1, D_MODEL) for k, a in vec_w.items()}

    pad_taps = lambda a: jnp.pad(a.reshape(CONV_K, SHARD_SQ), ((0, CONV_ROWS - CONV_K), (0, 0)))
    shards = (w_in[0].astype(BF16), w_sb_proj[0].astype(BF16), w_cv_proj[0].astype(BF16), w_out[0].astype(BF16),
              pad_taps(conv_w))
    x2 = x.reshape(t, D_MODEL)
    full_in, dhs, dr, big, small = _forward_backward(x2, loss_target.reshape(t, D_MODEL), nb, seq, shards, vecs)
    pair = _pair_sums(big)
    grad_x, st_in, got = _in_proj_bwd(dhs, full_in, dr, x2, vecs["ln_in_g"], pair)
    g_in, g_sb, g_cv, g_out = _finish_weight_grads(pair, got)
    small = _small_allreduce(jnp.concatenate([st_in[0:2], small], axis=0))

    chip = 2 * lax.axis_index("x") + lax.axis_index("y")
    g_conv = lax.dynamic_slice(small, (8, chip * SHARD_SQ), (CONV_ROWS, SHARD_SQ))
    loss = small[7, 0]

    grads, deltas, new_m, new_v = {}, {}, {}, {}
    stack = lambda d: jnp.concatenate([d[k].reshape(1, D_MODEL) for k in vec_names] + [jnp.zeros((1, D_MODEL), F32)])
    vd, vm, vv = _adamw(stack(vec_w), small[0:8], stack(vec_m), stack(vec_v), "adamw_vectors")
    for r, k in enumerate(vec_names):
        shape = vec_w[k].shape
        grads[k] = small[r].reshape(shape)
        deltas[k], new_m[k], new_v[k] = vd[r].reshape(shape), vm[r].reshape(shape), vv[r].reshape(shape)

    big_w = {"w_in": (w_in, g_in, m_w_in, v_w_in), "w_sb_proj": (w_sb_proj, g_sb, m_w_sb_proj, v_w_sb_proj),
             "w_cv_proj": (w_cv_proj, g_cv, m_w_cv_proj, v_w_cv_proj), "w_out": (w_out, g_out, m_w_out, v_w_out)}
    for k, (w, g, m, v) in big_w.items():
        d, mn, vn = _adamw(w[0], g, m[0], v[0], "adamw_" + k)
        grads[k], deltas[k], new_m[k], new_v[k] = g[None], d[None], mn[None], vn[None]
    d, mn, vn = _adamw(pad_taps(conv_w), g_conv, pad_taps(m_conv_w), pad_taps(v_conv_w), "adamw_conv_w")
    grads["conv_w"] = g_conv[None, :CONV_K]
    deltas["conv_w"], new_m["conv_w"], new_v["conv_w"] = d[None, :CONV_K], mn[None, :CONV_K], vn[None, :CONV_K]

    order = ("ln_in_g", "ln_in_b", "w_in", "w_sb_proj", "conv_w", "conv_b", "conv_ln_g", "conv_ln_b",
             "w_cv_proj", "w_out", "ln_post_g", "ln_post_b")
    return (loss, grad_x.reshape(x.shape), *[grads[k] for k in order], *[deltas[k] for k in order],
            *[new_m[k] for k in order], *[new_v[k] for k in order])
```

```python
import functools

import jax
import jax.numpy as jnp
from jax import lax
from jax.experimental import pallas as pl
from jax.experimental.pallas import tpu as pltpu

F32 = jnp.float32
BF16 = jnp.bfloat16

D_MODEL = 1024
N_HEADS = 16
HEAD_DIM = 64
HEAD_GROUP = 4
GROUP_W = HEAD_GROUP * HEAD_DIM
N_GROUPS = N_HEADS // HEAD_GROUP
N_PIECES = 9
IN_WIDTH = N_PIECES * D_MODEL
Q_BLOCK = 256
Q_TILE = 2 * Q_BLOCK
CONV_K = 31
CONV_ROWS = 32
HALO = 32
LN_EPS = 1e-5
ALPHA = 2.0 ** 0.25
Q_SCALE = 0.125
N_CHIPS = 4
N_DEV = 8
SHARD_IN = IN_WIDTH // N_CHIPS
SHARD_SQ = D_MODEL // N_CHIPS
HALF = D_MODEL // 2
SMALL_ROWS = 40
N_CHUNKS = 16

ADAM_LR = 0.001
ADAM_B1 = 0.9
ADAM_B2 = 0.999
ADAM_EPS = 1e-08
ADAM_WD = 0.01
ADAM_STEP = 10

MESH = pl.DeviceIdType.MESH
ANY = pl.BlockSpec(memory_space=pl.ANY)
VMEM_LIMIT = 60 * 1024 * 1024

NT = (((1,), (1,)), ((), ()))
TN = (((0,), (0,)), ((), ()))


def _sigmoid(x):
    return 1.0 / (1.0 + jnp.exp(-x))


def _dot(a, b):
    return jnp.dot(a, b, preferred_element_type=F32)


def _sum_along_keys(x, tri_twice):
    return _dot(jnp.concatenate(_split_bf16(x), axis=1), tri_twice)


def _dot_nt(a, b):
    return lax.dot_general(a, b, NT, preferred_element_type=F32)


def _dot_tn(a, b):
    return lax.dot_general(a, b, TN, preferred_element_type=F32)


def _split_bf16(x):
    hi = x.astype(BF16)
    lo = (x - hi.astype(F32)).astype(BF16)
    return hi, lo


def _ln_stats(x):
    mu = jnp.mean(x, axis=-1, keepdims=True)
    xc = x - mu
    var = jnp.mean(xc * xc, axis=-1, keepdims=True)
    rstd = lax.rsqrt(var + LN_EPS)
    return xc * rstd, rstd


def _ln_bwd(dy, xhat, rstd, g):
    dxh = dy * g
    m1 = jnp.mean(dxh, axis=-1, keepdims=True)
    m2 = jnp.mean(dxh * xhat, axis=-1, keepdims=True)
    return rstd * (dxh - m1 - xhat * m2)


def _params(*sem):
    return pltpu.CompilerParams(dimension_semantics=sem, vmem_limit_bytes=VMEM_LIMIT)


def _ln_in(x, g, b):
    t = x.shape[0]
    tm = min(512, t)

    def body(x_ref, g_ref, b_ref, hf_ref, hb_ref):
        xhat, _ = _ln_stats(x_ref[...])
        y = xhat * g_ref[...] + b_ref[...]
        hf_ref[...] = y
        hb_ref[...] = y.astype(BF16)

    row = pl.BlockSpec((tm, D_MODEL), lambda i: (i, 0))
    vec = pl.BlockSpec((1, D_MODEL), lambda i: (0, 0))
    return pl.pallas_call(
        body, grid=(t // tm,), in_specs=[row, vec, vec], out_specs=[row, row],
        out_shape=[jax.ShapeDtypeStruct((t, D_MODEL), F32), jax.ShapeDtypeStruct((t, D_MODEL), BF16)],
        compiler_params=_params("parallel"), name="ln_in")(x, g, b)


def _in_proj(h, w, col0, ncol, out_dtype, scale_first, name):
    t = h.shape[0]
    tm = min(1024, t)

    def body(h_ref, w_ref, o_ref):
        res = _dot(h_ref[...], w_ref[...])
        if scale_first:
            res = res * jnp.where(pl.program_id(0) == 0, Q_SCALE, 1.0)
        o_ref[...] = res.astype(out_dtype)

    return pl.pallas_call(
        body, grid=(ncol, t // tm),
        in_specs=[pl.BlockSpec((tm, D_MODEL), lambda j, i: (i, 0)),
                  pl.BlockSpec((D_MODEL, D_MODEL), lambda j, i: (0, j + col0))],
        out_specs=pl.BlockSpec((tm, D_MODEL), lambda j, i: (i, j)),
        out_shape=jax.ShapeDtypeStruct((t, ncol * D_MODEL), out_dtype),
        compiler_params=_params("parallel", "parallel"), name=name)(h, w)


def _sb_block(l, diag, tri, carry):
    lnb = jnp.minimum(-l, 0.0) - jnp.log(1.0 + jnp.exp(-jnp.abs(l)))
    if diag is not None:
        lnb = jnp.where(diag, lnb, 0.0)
    suffix = _sum_along_keys(lnb, tri)
    a = jnp.exp(l + suffix + carry)
    if diag is not None:
        a = jnp.where(diag, a, 0.0)
    return a, lnb, suffix[:, 0:1]


def _attn_consts():
    head = lax.broadcasted_iota(jnp.int32, (Q_TILE, GROUP_W), 1) // HEAD_DIM
    row = lax.broadcasted_iota(jnp.int32, (Q_TILE, Q_BLOCK), 0)
    col = lax.broadcasted_iota(jnp.int32, (Q_TILE, Q_BLOCK), 1)
    key = row % Q_BLOCK
    return head, col < row, (key >= col).astype(BF16), (key <= col).astype(BF16)


def _split_heads(t, head):
    zero = jnp.zeros_like(t)
    head = head[:t.shape[0]]
    return [jnp.where(head == h, t, zero) for h in range(HEAD_GROUP)]


def _key_block(ref, kb):
    return ref[pl.ds(pl.multiple_of(kb * Q_BLOCK, Q_BLOCK), Q_BLOCK), :]


def _attn_fwd(qkv, nb, seq, w_in_early, shards):
    nq = seq // Q_TILE

    def body(q_ref, k_ref, v_ref, full_in, s0, s1, s2, s3, s4, o_ref, d0, d1, d2, d3, d4, send_sems, recv_sems, local_sems):
        del full_in
        qi = pl.program_id(2)
        first = (pl.program_id(0) == 0) & (pl.program_id(1) == 0) & (qi == 0)
        last = (pl.program_id(0) == nb - 1) & (pl.program_id(1) == N_GROUPS - 1) & (qi == nq - 1)
        start_gather, finish_gather = _late_gather([s0, s1, s2, s3, s4], [d0, d1, d2, d3, d4],
                                                   send_sems, recv_sems, local_sems)
        pl.when(first)(start_gather)
        head, diag, tri, _ = _attn_consts()
        qh = _split_heads(q_ref[...], head)
        o_ref[...] = jnp.zeros_like(o_ref)
        zero_col = jnp.zeros((Q_BLOCK, 1), F32)

        def block(kb, carries, rows, masked):
            kblk = _key_block(k_ref, kb)
            vstack = jnp.concatenate(_split_heads(_key_block(v_ref, kb), head), axis=0)
            mask = diag[:Q_TILE - rows.start] if masked else None
            weights, out = [], []
            for h in range(HEAD_GROUP):
                a, _, blk_sum = _sb_block(_dot_nt(qh[h][rows], kblk), mask, tri, carries[h][rows])
                weights.append(a.astype(BF16))
                out.append(carries[h][rows] + blk_sum)
            o_ref[rows, :] += _dot(jnp.concatenate(weights, axis=1), vstack)
            return out

        second = block(2 * qi + 1, [jnp.zeros((Q_TILE, 1), F32)] * HEAD_GROUP, slice(Q_BLOCK, Q_TILE), True)
        carries = tuple(jnp.concatenate([zero_col, c], axis=0) for c in second)
        everything = slice(0, Q_TILE)
        carries = tuple(block(2 * qi, carries, everything, True))
        lax.fori_loop(1, 2 * qi + 1, lambda jj, cs: tuple(block(2 * qi - jj, cs, everything, False)), carries)
        pl.when(last)(finish_gather)

    n_shards = len(shards)
    full = [jax.ShapeDtypeStruct((D_MODEL, IN_WIDTH), BF16)] + [jax.ShapeDtypeStruct((D_MODEL, D_MODEL), BF16)] * 3 \
        + [jax.ShapeDtypeStruct((CONV_ROWS, D_MODEL), F32)]
    res = pl.pallas_call(
        body, grid=(nb, N_GROUPS, nq),
        in_specs=[pl.BlockSpec((Q_TILE, GROUP_W), lambda b, g, i: (b * nq + i, g)),
                  pl.BlockSpec((seq, GROUP_W), lambda b, g, i: (b, N_GROUPS + g)),
                  pl.BlockSpec((seq, GROUP_W), lambda b, g, i: (b, 2 * N_GROUPS + g))] + [ANY] * (1 + n_shards),
        out_specs=[pl.BlockSpec((Q_TILE, GROUP_W), lambda b, g, i: (b * nq + i, g))] + [ANY] * n_shards,
        out_shape=[jax.ShapeDtypeStruct((nb * seq, D_MODEL), F32)] + full,
        input_output_aliases={3: 1},
        scratch_shapes=[pltpu.SemaphoreType.DMA((3 * n_shards,)), pltpu.SemaphoreType.DMA((3 * n_shards,)),
                        pltpu.SemaphoreType.DMA((n_shards - 1,))],
        compiler_params=_params("arbitrary", "arbitrary", "arbitrary"), name="attn_fwd")(
            qkv, qkv, qkv, w_in_early, *shards)
    return res[0], res[1:]


def _attn_bwd(qkv, do, nb, seq):
    nq = seq // Q_TILE
    n_kb = seq // Q_BLOCK

    def body(q_ref, k_ref, v_ref, do_ref, dq_ref, dk_ref, dv_ref, g_s, beta_s, dq_acc, dk_acc, dv_acc):
        qi = pl.program_id(2)

        @pl.when(qi == 0)
        def _():
            dk_acc[...] = jnp.zeros_like(dk_acc)
            dv_acc[...] = jnp.zeros_like(dv_acc)

        head, diag, tri_suffix, tri_prefix = _attn_consts()
        qh = _split_heads(q_ref[...], head)
        doh = _split_heads(do_ref[...], head)
        dq_acc[...] = jnp.zeros_like(dq_acc)
        zero_col = jnp.zeros((Q_BLOCK, 1), F32)
        everything = slice(0, Q_TILE)
        second = slice(Q_BLOCK, Q_TILE)

        def block_a(kb, carries, rows, masked):
            ks = pl.multiple_of(kb * Q_BLOCK, Q_BLOCK)
            kblk = _key_block(k_ref, kb)
            vblk = _key_block(v_ref, kb)
            mask = diag[:Q_TILE - rows.start] if masked else None
            weights, out = [], []
            for h in range(HEAD_GROUP):
                l = _dot_nt(qh[h][rows], kblk)
                a, lnb, blk_sum = _sb_block(l, mask, tri_suffix, carries[h][rows])
                beta = jnp.exp(l + lnb)
                if masked:
                    beta = jnp.where(mask, beta, 0.0)
                g_s[h, kb, rows, :] = a * _dot_nt(doh[h][rows], vblk)
                beta_s[h, kb, rows, :] = beta.astype(BF16)
                weights.append(a.astype(BF16))
                out.append(carries[h][rows] + blk_sum)
            dv_acc[pl.ds(ks, Q_BLOCK), :] += _dot_tn(jnp.concatenate(weights, axis=0),
                                                     jnp.concatenate([t[rows] for t in doh], axis=0))
            return out

        part = block_a(2 * qi + 1, [jnp.zeros((Q_TILE, 1), F32)] * HEAD_GROUP, second, True)
        carries = tuple(jnp.concatenate([zero_col, c], axis=0) for c in part)
        carries = tuple(block_a(2 * qi, carries, everything, True))
        lax.fori_loop(1, 2 * qi + 1, lambda jj, cs: tuple(block_a(2 * qi - jj, cs, everything, False)), carries)

        def block_b(kb, prefixes, rows, masked):
            ks = pl.multiple_of(kb * Q_BLOCK, Q_BLOCK)
            kstack = jnp.concatenate(_split_heads(_key_block(k_ref, kb), head), axis=0)
            dls, out = [], []
            for h in range(HEAD_GROUP):
                g = g_s[h, kb, rows, :]
                beta = beta_s[h, kb, rows, :].astype(F32)
                incl = _sum_along_keys(g, tri_prefix)
                before = incl - g + prefixes[h][rows]
                dl = g - beta * (g + before)
                if masked:
                    dl = jnp.where(diag[:Q_TILE - rows.start], dl, 0.0)
                dls.append(dl.astype(BF16))
                out.append(prefixes[h][rows] + incl[:, Q_BLOCK - 1:Q_BLOCK])
            dq_acc[rows, :] += _dot(jnp.concatenate(dls, axis=1), kstack)
            dk_acc[pl.ds(ks, Q_BLOCK), :] += _dot_tn(jnp.concatenate(dls, axis=0),
                                                     jnp.concatenate([t[rows] for t in qh], axis=0))
            return out

        zeros = (jnp.zeros((Q_TILE, 1), F32),) * HEAD_GROUP
        prefixes = lax.fori_loop(0, 2 * qi, lambda kb, ps: tuple(block_b(kb, ps, everything, False)), zeros)
        prefixes = block_b(2 * qi, prefixes, everything, True)
        block_b(2 * qi + 1, prefixes, second, True)
        dq_ref[...] = (dq_acc[...] * Q_SCALE).astype(BF16)

        @pl.when(qi == nq - 1)
        def _():
            dk_ref[...] = dk_acc[...].astype(BF16)
            dv_ref[...] = dv_acc[...].astype(BF16)

    t = nb * seq
    qspec = pl.BlockSpec((Q_TILE, GROUP_W), lambda b, g, i: (b * nq + i, g))
    kvout = pl.BlockSpec((seq, GROUP_W), lambda b, g, i: (b, g))
    return pl.pallas_call(
        body, grid=(nb, N_GROUPS, nq),
        in_specs=[qspec,
                  pl.BlockSpec((seq, GROUP_W), lambda b, g, i: (b, N_GROUPS + g)),
                  pl.BlockSpec((seq, GROUP_W), lambda b, g, i: (b, 2 * N_GROUPS + g)),
                  qspec],
        out_specs=[qspec, kvout, kvout],
        out_shape=[jax.ShapeDtypeStruct((t, D_MODEL), BF16)] * 3,
        scratch_shapes=[pltpu.VMEM((HEAD_GROUP, n_kb, Q_TILE, Q_BLOCK), F32),
                        pltpu.VMEM((HEAD_GROUP, n_kb, Q_TILE, Q_BLOCK), BF16),
                        pltpu.VMEM((Q_TILE, GROUP_W), F32),
                        pltpu.VMEM((seq, GROUP_W), F32), pltpu.VMEM((seq, GROUP_W), F32)],
        compiler_params=_params("parallel", "parallel", "arbitrary"), name="attn_bwd")(qkv, qkv, qkv, do)


def _conv_block_rows(seq):
    return min(256, seq)


SUBLANES = 8
CONV_SUB = 32
CONV_GROUPS = CONV_SUB // SUBLANES


def _fill_shifts(sh_ref, n):
    for r in range(1, SUBLANES):
        sh_ref[r, 0:n, :] = sh_ref[0, pl.ds(r, n), :]


def _shifted(sh_ref, start, cs):
    r = start % SUBLANES
    return sh_ref[r, start - r:start - r + CONV_SUB, cs].reshape(CONV_GROUPS, SUBLANES, -1)


def _tap(w_ref, k, cs):
    row = w_ref[k:k + 1, cs]
    return jnp.broadcast_to(row, (SUBLANES, row.shape[1]))[None]


def _tree_sum(x):
    parts = [x[i] for i in range(x.shape[0])]
    while len(parts) > 1:
        parts = [parts[i] + parts[i + 1] for i in range(0, len(parts), 2)]
    return parts[0]


def _conv_fwd(hrest, conv_w, conv_b, ln_g, ln_b, nb, seq):
    ts = _conv_block_rows(seq)
    nblk = seq // ts
    lanes = 128

    def body(cv_ref, cg_ref, cvh_ref, cgh_ref, z_ref, w_ref, cb_ref, g_ref, b_ref, u1_ref, cvin_ref, ush):
        i = pl.program_id(1)
        halo = cvh_ref[...] * _sigmoid(cgh_ref[...])
        ush[0, 0:HALO, :] = jnp.where(i > 0, halo, 0.0)
        ush[0, HALO:HALO + ts, :] = cv_ref[...] * _sigmoid(cg_ref[...])
        _fill_shifts(ush, HALO + ts - SUBLANES)
        for cc in range(D_MODEL // lanes):
            cs = slice(cc * lanes, (cc + 1) * lanes)
            for r0 in range(0, ts, CONV_SUB):
                acc = jnp.zeros((CONV_GROUPS, SUBLANES, lanes), F32) + _tap(cb_ref, 0, cs)
                for k in range(CONV_K):
                    acc = acc + _tap(w_ref, k, cs) * _shifted(ush, HALO - CONV_K + 1 + k + r0, cs)
                u1_ref[r0:r0 + CONV_SUB, cs] = acc.reshape(CONV_SUB, lanes)
        xhat, _ = _ln_stats(u1_ref[...])
        u2 = xhat * g_ref[...] + b_ref[...]
        z = z_ref[...]
        cvin_ref[...] = (u2 * _sigmoid(u2) * z * _sigmoid(z)).astype(BF16)

    def main(colblk):
        return pl.BlockSpec((ts, D_MODEL), lambda b, i: (b * nblk + i, colblk))

    def halo(colblk):
        return pl.BlockSpec((HALO, D_MODEL),
                            lambda b, i: (jnp.maximum((b * seq + i * ts) // HALO - 1, 0), colblk))

    vec = pl.BlockSpec((1, D_MODEL), lambda b, i: (0, 0))
    t = nb * seq
    return pl.pallas_call(
        body, grid=(nb, nblk),
        in_specs=[main(1), main(2), halo(1), halo(2), main(3),
                  pl.BlockSpec((CONV_ROWS, D_MODEL), lambda b, i: (0, 0)), vec, vec, vec],
        out_specs=[main(0), main(0)],
        out_shape=[jax.ShapeDtypeStruct((t, D_MODEL), F32), jax.ShapeDtypeStruct((t, D_MODEL), BF16)],
        scratch_shapes=[pltpu.VMEM((SUBLANES, HALO + ts, D_MODEL), F32)],
        compiler_params=_params("parallel", "parallel"), name="conv_fwd")(
            hrest, hrest, hrest, hrest, hrest, conv_w, conv_b, ln_g, ln_b)


def _tail(o, hrest, cvin, h0, tgt, w_sb, w_cv, w_out, ln_g, ln_b):
    t = o.shape[0]
    tm = min(256, t)

    def body(o_ref, z_ref, gs_ref, gc_ref, cvin_ref, h0_ref, tgt_ref, wsb_ref, wcv_ref, wout_ref, g_ref, b_ref,
             dr_ref, drb_ref, a_ref, mg_ref, dysb_ref, dycv_ref, do_ref, dz_ref, dgs_ref, dgc_ref, dcvin_ref,
             st_ref):
        @pl.when(pl.program_id(0) == 0)
        def _():
            st_ref[...] = jnp.zeros_like(st_ref)

        o = o_ref[...]
        z = z_ref[...]
        sz = _sigmoid(z)
        a = (o * z * sz).astype(BF16)
        a_ref[...] = a
        y_sb = _dot(a, wsb_ref[...])
        y_cv = _dot(cvin_ref[...], wcv_ref[...])
        s_sb = _sigmoid(gs_ref[...])
        s_cv = _sigmoid(gc_ref[...])
        merged = (s_sb * y_sb + s_cv * y_cv).astype(BF16)
        mg_ref[...] = merged
        r = ALPHA * h0_ref[...] + _dot(merged, wout_ref[...])
        xhat, rstd = _ln_stats(r)
        g = g_ref[...]
        err = xhat * g + b_ref[...] - tgt_ref[...]
        dy = err * (1.0 / D_MODEL)
        st_ref[0:1, :] += jnp.sum(dy * xhat, axis=0, keepdims=True)
        st_ref[1:2, :] += jnp.sum(dy, axis=0, keepdims=True)
        st_ref[2:3, :] += (0.5 / D_MODEL) * jnp.sum(err * err, axis=0, keepdims=True)
        dr = _ln_bwd(dy, xhat, rstd, g)
        dr_ref[...] = dr
        drb = dr.astype(BF16)
        drb_ref[...] = drb
        dm = _dot_nt(drb, wout_ref[...])
        dy_sb = (dm * s_sb).astype(BF16)
        dy_cv = (dm * s_cv).astype(BF16)
        dysb_ref[...] = dy_sb
        dycv_ref[...] = dy_cv
        dgs_ref[...] = (dm * y_sb * s_sb * (1.0 - s_sb)).astype(BF16)
        dgc_ref[...] = (dm * y_cv * s_cv * (1.0 - s_cv)).astype(BF16)
        da = _dot_nt(dy_sb, wsb_ref[...])
        dcvin_ref[...] = _dot_nt(dy_cv, wcv_ref[...])
        do_ref[...] = (da * z * sz).astype(BF16)
        dz_ref[...] = (da * o * sz * (1.0 + z * (1.0 - sz))).astype(BF16)

    def tok(colblk=0):
        return pl.BlockSpec((tm, D_MODEL), lambda i: (i, colblk))

    wspec = pl.BlockSpec((D_MODEL, D_MODEL), lambda i: (0, 0), pipeline_mode=pl.Buffered(1))
    vec = pl.BlockSpec((1, D_MODEL), lambda i: (0, 0))
    bf = jax.ShapeDtypeStruct((t, D_MODEL), BF16)
    f32 = jax.ShapeDtypeStruct((t, D_MODEL), F32)
    return pl.pallas_call(
        body, grid=(t // tm,),
        in_specs=[tok(), tok(0), tok(4), tok(5), tok(), tok(), tok(), wspec, wspec, wspec, vec, vec],
        out_specs=[tok()] * 11 + [pl.BlockSpec((8, D_MODEL), lambda i: (0, 0))],
        out_shape=[f32] + [bf] * 9 + [f32, jax.ShapeDtypeStruct((8, D_MODEL), F32)],
        compiler_params=_params("arbitrary"), name="tail")(
            o, hrest, hrest, hrest, cvin, h0, tgt, w_sb, w_cv, w_out, ln_g, ln_b)


def _conv_branch_bwd(dcvin, u1, hrest, ln_g, ln_b):
    t = u1.shape[0]
    tm = min(512, t)

    def body(dc_ref, u1_ref, z_ref, g_ref, b_ref, du1_ref, dz_ref, st_ref):
        @pl.when(pl.program_id(0) == 0)
        def _():
            st_ref[...] = jnp.zeros_like(st_ref)

        xhat, rstd = _ln_stats(u1_ref[...])
        g = g_ref[...]
        u2 = xhat * g + b_ref[...]
        s2 = _sigmoid(u2)
        z = z_ref[...]
        sz = _sigmoid(z)
        dc = dc_ref[...]
        dz_ref[...] = (dc * (u2 * s2) * sz * (1.0 + z * (1.0 - sz))).astype(BF16)
        du2 = dc * (z * sz) * s2 * (1.0 + u2 * (1.0 - s2))
        st_ref[0:1, :] += jnp.sum(du2 * xhat, axis=0, keepdims=True)
        st_ref[1:2, :] += jnp.sum(du2, axis=0, keepdims=True)
        du1 = _ln_bwd(du2, xhat, rstd, g)
        du1_ref[...] = du1
        st_ref[2:3, :] += jnp.sum(du1, axis=0, keepdims=True)

    tok = pl.BlockSpec((tm, D_MODEL), lambda i: (i, 0))
    vec = pl.BlockSpec((1, D_MODEL), lambda i: (0, 0))
    return pl.pallas_call(
        body, grid=(t // tm,),
        in_specs=[tok, tok, pl.BlockSpec((tm, D_MODEL), lambda i: (i, 3)), vec, vec],
        out_specs=[tok, tok, pl.BlockSpec((8, D_MODEL), lambda i: (0, 0))],
        out_shape=[jax.ShapeDtypeStruct((t, D_MODEL), F32), jax.ShapeDtypeStruct((t, D_MODEL), BF16),
                   jax.ShapeDtypeStruct((8, D_MODEL), F32)],
        compiler_params=_params("arbitrary"), name="conv_branch_bwd")(dcvin, u1, hrest, ln_g, ln_b)


def _conv_bwd(du1, hrest, conv_w, nb, seq):
    ts = _conv_block_rows(seq)
    nblk = seq // ts
    lanes = 128
    last_halo = nb * seq // HALO - 1

    def body(du_ref, duh_ref, cv_ref, cg_ref, cvh_ref, cgh_ref, w_ref, dcv_ref, dcg_ref, dw_ref, ush, dsh, dw_part):
        b = pl.program_id(0)
        i = pl.program_id(1)

        @pl.when((b == 0) & (i == 0))
        def _():
            dw_part[...] = jnp.zeros_like(dw_part)

        cv = cv_ref[...]
        sg = _sigmoid(cg_ref[...])
        halo = cvh_ref[...] * _sigmoid(cgh_ref[...])
        ush[0, 0:HALO, :] = jnp.where(i > 0, halo, 0.0)
        ush[0, HALO:HALO + ts, :] = cv * sg
        dsh[0, 0:ts, :] = du_ref[...]
        dsh[0, ts:ts + HALO, :] = jnp.where(i < nblk - 1, duh_ref[...], 0.0)
        _fill_shifts(ush, HALO + ts - SUBLANES)
        _fill_shifts(dsh, HALO + ts - SUBLANES)
        for cc in range(D_MODEL // lanes):
            cs = slice(cc * lanes, (cc + 1) * lanes)
            for r0 in range(0, ts, CONV_SUB):
                rows = slice(r0, r0 + CONV_SUB)
                dmain = _shifted(dsh, r0, cs)
                acc = jnp.zeros((CONV_GROUPS, SUBLANES, lanes), F32)
                for k in range(CONV_K):
                    acc = acc + _tap(w_ref, k, cs) * _shifted(dsh, CONV_K - 1 - k + r0, cs)
                    dw_part[k, :, cs] += _tree_sum(dmain * _shifted(ush, HALO - CONV_K + 1 + k + r0, cs))
                acc = acc.reshape(CONV_SUB, lanes)
                dcv_ref[rows, cs] = (acc * sg[rows, cs]).astype(BF16)
                dcg_ref[rows, cs] = (acc * cv[rows, cs] * sg[rows, cs] * (1.0 - sg[rows, cs])).astype(BF16)

        @pl.when((b == nb - 1) & (i == nblk - 1))
        def _():
            for k in range(CONV_ROWS):
                dw_ref[k:k + 1, :] = jnp.sum(dw_part[k], axis=0, keepdims=True)

    def main(colblk):
        return pl.BlockSpec((ts, D_MODEL), lambda b, i: (b * nblk + i, colblk))

    def halo_before(colblk):
        return pl.BlockSpec((HALO, D_MODEL),
                            lambda b, i: (jnp.maximum((b * seq + i * ts) // HALO - 1, 0), colblk))

    halo_after = pl.BlockSpec((HALO, D_MODEL),
                              lambda b, i: (jnp.minimum((b * seq + (i + 1) * ts) // HALO, last_halo), 0))
    t = nb * seq
    return pl.pallas_call(
        body, grid=(nb, nblk),
        in_specs=[main(0), halo_after, main(1), main(2), halo_before(1), halo_before(2),
                  pl.BlockSpec((CONV_ROWS, D_MODEL), lambda b, i: (0, 0))],
        out_specs=[main(0), main(0), pl.BlockSpec((CONV_ROWS, D_MODEL), lambda b, i: (0, 0))],
        out_shape=[jax.ShapeDtypeStruct((t, D_MODEL), BF16), jax.ShapeDtypeStruct((t, D_MODEL), BF16),
                   jax.ShapeDtypeStruct((CONV_ROWS, D_MODEL), F32)],
        scratch_shapes=[pltpu.VMEM((SUBLANES, HALO + ts, D_MODEL), F32), pltpu.VMEM((SUBLANES, ts + HALO, D_MODEL), F32),
                        pltpu.VMEM((CONV_ROWS, SUBLANES, D_MODEL), F32)],
        compiler_params=_params("arbitrary", "arbitrary"), name="conv_bwd")(
            du1, du1, hrest, hrest, hrest, hrest, conv_w)


def _weight_grad(x, dys, name):
    n = len(dys)
    t = x.shape[0]
    tm = min(512, t)

    def body(x_ref, *refs):
        o_ref = refs[n]
        j = pl.program_id(0)

        @pl.when(pl.program_id(1) == 0)
        def _():
            o_ref[...] = jnp.zeros_like(o_ref)

        for p in range(n):
            @pl.when(j == p)
            def _(p=p):
                o_ref[...] += _dot_tn(x_ref[...], refs[p][...])

    def dy_spec(p):
        return pl.BlockSpec((tm, D_MODEL), lambda j, i: (jnp.where(j == p, i, 0), 0))

    return pl.pallas_call(
        body, grid=(n, t // tm),
        in_specs=[pl.BlockSpec((tm, D_MODEL), lambda j, i: (i, 0))] + [dy_spec(p) for p in range(n)],
        out_specs=pl.BlockSpec((D_MODEL, D_MODEL), lambda j, i: (0, j)),
        out_shape=jax.ShapeDtypeStruct((D_MODEL, n * D_MODEL), F32),
        compiler_params=_params("parallel", "arbitrary"), name=name)(x, *dys)


def _in_proj_bwd(dhs, w_in, dr, x, ln_g, pair):
    n = len(dhs)
    t = x.shape[0]
    tm = min(512, t)
    n_i = t // tm

    def body(*refs):
        dh_refs = refs[:n]
        w_ref, dr_ref, x_ref, g_ref = refs[n:n + 4]
        pair_refs = refs[n + 4:n + 8]
        gx_ref, st_ref = refs[n + 8:n + 10]
        got_refs = refs[n + 10:n + 14]
        acc, send_sems, recv_sems = refs[n + 14:]
        i = pl.program_id(0)
        p = pl.program_id(1)
        pieces, wholes = _chip_exchange_copies(pair_refs, got_refs, send_sems, recv_sems)

        @pl.when((i == 0) & (p == 0))
        def _():
            st_ref[...] = jnp.zeros_like(st_ref)
            for cp in pieces:
                cp.start()

        @pl.when((i == n_i - 1) & (p == n - 1))
        def _():
            for cp in wholes:
                cp.wait_recv()
            for cp in wholes:
                cp.wait_send()

        @pl.when(p == 0)
        def _():
            acc[...] = ALPHA * dr_ref[...]

        for q in range(n):
            @pl.when(p == q)
            def _(q=q):
                acc[...] += _dot_nt(dh_refs[q][...], w_ref[...])

        @pl.when(p == n - 1)
        def _():
            xhat, rstd = _ln_stats(x_ref[...])
            dh0 = acc[...]
            st_ref[0:1, :] += jnp.sum(dh0 * xhat, axis=0, keepdims=True)
            st_ref[1:2, :] += jnp.sum(dh0, axis=0, keepdims=True)
            gx_ref[...] = _ln_bwd(dh0, xhat, rstd, g_ref[...])

    tok = pl.BlockSpec((tm, D_MODEL), lambda i, p: (i, 0))
    got = [jax.ShapeDtypeStruct((3, HALF, SHARD_IN), BF16)] + [jax.ShapeDtypeStruct((3, SHARD_SQ, HALF), BF16)] * 3
    res = pl.pallas_call(
        body, grid=(n_i, n),
        in_specs=[tok] * n + [pl.BlockSpec((D_MODEL, D_MODEL), lambda i, p: (0, p)), tok, tok,
                              pl.BlockSpec((1, D_MODEL), lambda i, p: (0, 0))] + [ANY] * 4,
        out_specs=[tok, pl.BlockSpec((8, D_MODEL), lambda i, p: (0, 0))] + [ANY] * 4,
        out_shape=[jax.ShapeDtypeStruct((t, D_MODEL), F32), jax.ShapeDtypeStruct((8, D_MODEL), F32)] + got,
        scratch_shapes=[pltpu.VMEM((tm, D_MODEL), F32), pltpu.SemaphoreType.DMA((12,)), pltpu.SemaphoreType.DMA((12,))],
        compiler_params=_params("arbitrary", "arbitrary"), name="in_proj_bwd")(*dhs, w_in, dr, x, ln_g, *pair)
    return res[0], res[1], res[2:]


def _forward_backward(x, tgt, nb, seq, shards, vecs):
    w_in_early = _gather_first(shards[0])
    h0, h0b = _ln_in(x, vecs["ln_in_g"], vecs["ln_in_b"])
    qkv = _in_proj(h0b, w_in_early, 0, 3, BF16, True, "in_proj_qkv")
    o, (w_in, w_sb, w_cv, w_out, conv_w) = _attn_fwd(qkv, nb, seq, w_in_early, shards)
    hrest = _in_proj(h0b, w_in, 3, 6, F32, False, "in_proj_rest")
    u1, cvin = _conv_fwd(hrest, conv_w, vecs["conv_b"], vecs["conv_ln_g"], vecs["conv_ln_b"], nb, seq)
    (dr, drb, a, merged, dy_sb, dy_cv, do, dz_sb, dg_sb, dg_cv, dcvin, st_tail) = _tail(
        o, hrest, cvin, h0, tgt, w_sb, w_cv, w_out, vecs["ln_post_g"], vecs["ln_post_b"])
    d_w_sb = _weight_grad(a, [dy_sb], "grad_w_sb")
    d_w_cv = _weight_grad(cvin, [dy_cv], "grad_w_cv")
    d_w_out = _weight_grad(merged, [drb], "grad_w_out")
    du1, dz_cv, st_conv = _conv_branch_bwd(dcvin, u1, hrest, vecs["conv_ln_g"], vecs["conv_ln_b"])
    dc_val, dc_gate, d_conv_w = _conv_bwd(du1, hrest, conv_w, nb, seq)
    dq, dk, dv = _attn_bwd(qkv, do, nb, seq)
    dhs = [dq, dk, dv, dz_sb, dc_val, dc_gate, dz_cv, dg_sb, dg_cv]
    d_w_in = _weight_grad(h0b, dhs, "grad_w_in")
    small = jnp.concatenate([st_conv[2:3], st_conv[0:2], st_tail[0:3], d_conv_w], axis=0)
    return w_in, dhs, dr, (d_w_in, d_w_sb, d_w_cv, d_w_out), small


def _place():
    x, y, c = lax.axis_index("x"), lax.axis_index("y"), lax.axis_index("c")
    chips = [(1 - x, y), (x, 1 - y), (1 - x, 1 - y)]
    return x, y, c, chips


N_EARLY = -(-3 * D_MODEL // SHARD_IN)
GATHER_PARTS = 4


def _w_in_slot(full, chip):
    return full.at[:, pl.ds(pl.multiple_of(chip * SHARD_IN, 128), SHARD_IN)]


def _gather_first(w_in):
    part_rows = HALF // GATHER_PARTS
    n_sem = 3 * GATHER_PARTS
    across_x, across_y, diagonal = 0, 1, 2

    def body(src, dst, ici_send, ici_recv, fwd_send, fwd_recv, local_sem):
        x, y, c, chips = _place()
        me = 2 * x + y
        sibling = (x, y, 1 - c)
        owner = [2 * px + py for px, py in chips]

        def part(ref, k, half):
            return ref.at[pl.ds(pl.multiple_of(half * HALF + k * part_rows, 16), part_rows), :]

        def held(j, k, half):
            return part(_w_in_slot(dst, owner[j]), k, half)

        def sent(j, k):
            px, py = chips[j]
            return _remote(part(src, k, c), part(_w_in_slot(dst, me), k, c), ici_send.at[j * GATHER_PARTS + k],
                           ici_recv.at[j * GATHER_PARTS + k], (px, py, c))

        def relayed(k):
            px, py = chips[across_y]
            here = held(across_x, k, c)
            return _remote(here, here, ici_send.at[diagonal * GATHER_PARTS + k],
                           ici_recv.at[diagonal * GATHER_PARTS + k], (px, py, c))

        def landed(j, k):
            px, py = chips[across_y if j == diagonal else j]
            return _remote(held(j, k, c), held(j, k, c), ici_send.at[j * GATHER_PARTS + k],
                           ici_recv.at[j * GATHER_PARTS + k], (px, py, c))

        def passed(j, k, half):
            return _remote(held(j, k, half), held(j, k, half), fwd_send.at[j * GATHER_PARTS + k],
                           fwd_recv.at[j * GATHER_PARTS + k], sibling)

        local = pltpu.make_async_copy(src, _w_in_slot(dst, me), local_sem)
        local.start()

        @pl.when(me < N_EARLY)
        def _():
            for j in (across_x, across_y):
                for k in range(GATHER_PARTS):
                    sent(j, k).start()

        for j in (across_x, across_y, diagonal):
            @pl.when(owner[j] < N_EARLY)
            def _(j=j):
                for k in range(GATHER_PARTS):
                    landed(j, k).wait_recv()
                    passed(j, k, c).start()
                    if j == across_x:
                        relayed(k).start()

        for j in (across_x, across_y, diagonal):
            @pl.when(owner[j] < N_EARLY)
            def _(j=j):
                for k in range(GATHER_PARTS):
                    passed(j, k, 1 - c).wait_recv()
                for k in range(GATHER_PARTS):
                    passed(j, k, c).wait_send()
                    if j == across_x:
                        relayed(k).wait_send()

        @pl.when(me < N_EARLY)
        def _():
            for j in (across_x, across_y):
                for k in range(GATHER_PARTS):
                    sent(j, k).wait_send()

        local.wait()

    return pl.pallas_call(
        body, in_specs=[ANY], out_specs=ANY, out_shape=jax.ShapeDtypeStruct((D_MODEL, IN_WIDTH), BF16),
        scratch_shapes=[pltpu.SemaphoreType.DMA((n_sem,)), pltpu.SemaphoreType.DMA((n_sem,)),
                        pltpu.SemaphoreType.DMA((n_sem,)), pltpu.SemaphoreType.DMA((n_sem,)),
                        pltpu.SemaphoreType.DMA],
        name="gather_first")(w_in)


def _late_gather(srcs, dsts, send_sems, recv_sems, local_sems):
    x, y, c, chips = _place()
    me = 2 * x + y
    small = range(1, 5)

    def slot(a, chip):
        if a == 0:
            return _w_in_slot(dsts[0], chip)
        if a == 4:
            return dsts[a].at[:, pl.ds(pl.multiple_of(chip * SHARD_SQ, 128), SHARD_SQ)]
        return dsts[a].at[pl.ds(pl.multiple_of(chip * SHARD_SQ, 16), SHARD_SQ), :]

    def sems(a, j):
        return send_sems.at[3 * a + j], recv_sems.at[3 * a + j]

    def whole(a, j):
        px, py = chips[j]
        return _remote(srcs[a], slot(a, me), *sems(a, j), (px, py, c))

    def arrival(a, j):
        px, py = chips[j]
        return _arrival(slot(a, 2 * px + py), *sems(a, j), (px, py, c))

    def local(a):
        return pltpu.make_async_copy(srcs[a], slot(a, me), local_sems.at[a - 1])

    def start():
        for a in small:
            local(a).start()
            for j in range(3):
                whole(a, j).start()

        @pl.when(me >= N_EARLY)
        def _():
            for j, (px, py) in enumerate(chips):
                for cp in _chunk_copies(srcs[0], slot(0, me), GATHER_PARTS, *sems(0, j), (px, py, c)):
                    cp.start()

    def finish():
        for j, (px, py) in enumerate(chips):
            for a in small:
                arrival(a, j).wait_recv()

            @pl.when(2 * px + py >= N_EARLY)
            def _(j=j):
                arrival(0, j).wait_recv()

        for a in small:
            for j in range(3):
                whole(a, j).wait_send()
            local(a).wait()

        @pl.when(me >= N_EARLY)
        def _():
            for j in range(3):
                whole(0, j).wait_send()

    return start, finish


def _remote(src, dst, send_sem, recv_sem, device):
    return pltpu.make_async_remote_copy(src_ref=src, dst_ref=dst, send_sem=send_sem, recv_sem=recv_sem,
                                        device_id=device, device_id_type=MESH)


def _chunk_copies(src, dst, n, send_sem, recv_sem, device):
    rows = src.shape[0] // n
    return [_remote(src.at[pl.ds(k * rows, rows)], dst.at[pl.ds(k * rows, rows)], send_sem, recv_sem, device)
            for k in range(n)]


def _row_chunks(src, dst, n, send_sem, recv_sem, device):
    return _chunk_copies(src, dst, n, send_sem, recv_sem, device), _remote(src, dst, send_sem, recv_sem, device)


def _arrival(dst, send_sem, recv_sem, device):
    return _remote(dst, dst, send_sem, recv_sem, device)


def _pair_exchange(d_in, d_sb, d_cv, d_out):
    def body(i0, i1, i2, i3, r0, r1, r2, r3, send_sems, recv_sems):
        x, y, c, _ = _place()
        srcs = [i0, i1, i2, i3]
        theirs = [r0, r1, r2, r3]
        sibling = (x, y, 1 - c)
        start = pl.multiple_of((1 - c) * HALF, 128)
        views = [srcs[0].at[pl.ds(start, HALF), :]] + [srcs[a].at[:, pl.ds(start, HALF)] for a in range(1, 4)]
        pieces, wholes = [], []
        for a in range(4):
            cps, whole = _row_chunks(views[a], theirs[a], N_CHUNKS, send_sems.at[a], recv_sems.at[a], sibling)
            pieces += cps
            wholes.append(whole)
        for cp in pieces:
            cp.start()
        for cp in wholes:
            cp.wait_recv()
        for cp in wholes:
            cp.wait_send()

    shapes = [jax.ShapeDtypeStruct((HALF, IN_WIDTH), F32)] + [jax.ShapeDtypeStruct((D_MODEL, HALF), F32)] * 3
    return pl.pallas_call(
        body, in_specs=[ANY] * 4, out_specs=[ANY] * 4, out_shape=shapes,
        scratch_shapes=[pltpu.SemaphoreType.DMA((4,)), pltpu.SemaphoreType.DMA((4,))],
        name="pair_exchange")(d_in, d_sb, d_cv, d_out)


def _chip_exchange_copies(srcs, theirs, send_sems, recv_sems):
    _, _, c, chips = _place()

    def shard(a, px, py):
        chip = 2 * px + py
        if a == 0:
            return srcs[a].at[:, pl.ds(pl.multiple_of(chip * SHARD_IN, 128), SHARD_IN)]
        return srcs[a].at[pl.ds(pl.multiple_of(chip * SHARD_SQ, 16), SHARD_SQ), :]

    pieces, wholes = [], []
    for j, (px, py) in enumerate(chips):
        for a in range(4):
            cps, whole = _row_chunks(shard(a, px, py), theirs[a].at[j], 4, send_sems.at[3 * a + j],
                                     recv_sems.at[3 * a + j], (px, py, c))
            pieces += cps
            wholes.append(whole)
    return pieces, wholes


def _pair_share(f_in, f_sb, f_cv, f_out):
    def body(i0, i1, i2, i3, o0, o1, o2, o3, send_sems, recv_sems):
        del i0, i1, i2, i3
        x, y, c, _ = _place()
        dsts = [o0, o1, o2, o3]
        sibling = (x, y, 1 - c)

        def half(a, which):
            start = pl.multiple_of(which * HALF, 128)
            if a == 0:
                return dsts[a].at[pl.ds(start, HALF), :]
            return dsts[a].at[:, pl.ds(start, HALF)]

        pieces, sent, arrived = [], [], []
        for a in range(4):
            cps, whole = _row_chunks(half(a, c), half(a, c), N_CHUNKS, send_sems.at[a], recv_sems.at[a], sibling)
            pieces += cps
            sent.append(whole)
            arrived.append(_arrival(half(a, 1 - c), send_sems.at[a], recv_sems.at[a], sibling))
        for cp in pieces:
            cp.start()
        for cp in arrived:
            cp.wait_recv()
        for cp in sent:
            cp.wait_send()

    return pl.pallas_call(
        body, in_specs=[ANY] * 4, out_specs=[ANY] * 4,
        out_shape=[jax.ShapeDtypeStruct((D_MODEL, SHARD_IN), F32)] + [jax.ShapeDtypeStruct((SHARD_SQ, D_MODEL), F32)] * 3,
        input_output_aliases={0: 0, 1: 1, 2: 2, 3: 3},
        scratch_shapes=[pltpu.SemaphoreType.DMA((4,)), pltpu.SemaphoreType.DMA((4,))],
        name="pair_share")(f_in, f_sb, f_cv, f_out)


def _small_allreduce(small):
    def body(s_ref, o_ref, slots, send_sems, recv_sems):
        x, y, c, _ = _place()
        me = 4 * x + 2 * y + c
        slots[me] = s_ref[...]
        sends = []
        for k in range(1, N_DEV):
            px, py, pc = (x + (k >> 2)) % 2, (y + ((k >> 1) & 1)) % 2, (c + (k & 1)) % 2
            sends.append(pltpu.make_async_remote_copy(
                src_ref=s_ref, dst_ref=slots.at[me], send_sem=send_sems.at[k - 1], recv_sem=recv_sems.at[k - 1],
                device_id=(px, py, pc), device_id_type=MESH))
        for cp in sends:
            cp.start()
        for k in range(1, N_DEV):
            px, py, pc = (x + (k >> 2)) % 2, (y + ((k >> 1) & 1)) % 2, (c + (k & 1)) % 2
            pltpu.make_async_remote_copy(
                src_ref=s_ref, dst_ref=slots.at[4 * px + 2 * py + pc], send_sem=send_sems.at[k - 1],
                recv_sem=recv_sems.at[k - 1], device_id=(px, py, pc), device_id_type=MESH).wait_recv()
        for cp in sends:
            cp.wait_send()
        total = slots[0]
        for d in range(1, N_DEV):
            total = total + slots[d]
        o_ref[...] = total
        o_ref[7:8, :] = jnp.zeros((1, D_MODEL), F32) + jnp.sum(total[7:8, :], axis=1, keepdims=True)

    return pl.pallas_call(
        body, in_specs=[pl.BlockSpec(memory_space=pltpu.VMEM)], out_specs=pl.BlockSpec(memory_space=pltpu.VMEM),
        out_shape=jax.ShapeDtypeStruct((SMALL_ROWS, D_MODEL), F32),
        scratch_shapes=[pltpu.VMEM((N_DEV, SMALL_ROWS, D_MODEL), F32), pltpu.SemaphoreType.DMA((N_DEV - 1,)),
                        pltpu.SemaphoreType.DMA((N_DEV - 1,))],
        name="small_allreduce")(small)


SUM_BLOCKS = 8


def _pair_sum(place, full, theirs, by_rows, name):
    rows, cols = theirs.shape
    tr = rows // SUM_BLOCKS

    def body(p_ref, a_ref, b_ref, o_ref):
        del p_ref
        o_ref[...] = (a_ref[...] + b_ref[...]).astype(BF16)

    mine = (lambda i, p: (p[1] * SUM_BLOCKS + i, 0)) if by_rows else (lambda i, p: (i, p[1]))
    spec = pl.BlockSpec((tr, cols), lambda i, p: (i, 0))
    return pl.pallas_call(
        body, out_shape=jax.ShapeDtypeStruct(theirs.shape, BF16),
        grid_spec=pltpu.PrefetchScalarGridSpec(num_scalar_prefetch=1, grid=(SUM_BLOCKS,),
                                               in_specs=[pl.BlockSpec((tr, cols), mine), spec], out_specs=spec),
        compiler_params=_params("parallel"), name=name)(place, full, theirs)


def _chip_sum(place, pair, got, by_rows, name):
    _, rows, cols = got.shape
    tr = rows // SUM_BLOCKS

    def body(p_ref, a_ref, g_ref, o_ref):
        del p_ref
        o_ref[...] = ((a_ref[...].astype(F32) + g_ref[0].astype(F32)) + g_ref[1].astype(F32)) + g_ref[2].astype(F32)

    if by_rows:
        own = lambda i, p: (i, p[0])
        out = lambda i, p: (p[1] * SUM_BLOCKS + i, 0)
        full = (2 * rows, cols)
    else:
        own = lambda i, p: (p[0] * SUM_BLOCKS + i, 0)
        out = lambda i, p: (i, p[1])
        full = (rows, 2 * cols)
    return pl.pallas_call(
        body, out_shape=jax.ShapeDtypeStruct(full, F32),
        grid_spec=pltpu.PrefetchScalarGridSpec(
            num_scalar_prefetch=1, grid=(SUM_BLOCKS,),
            in_specs=[pl.BlockSpec((tr, cols), own), pl.BlockSpec((3, tr, cols), lambda i, p: (0, i, 0))],
            out_specs=pl.BlockSpec((tr, cols), out)),
        compiler_params=_params("parallel"), name=name)(place, pair, got)


def _adamw(w, g, m, v, name):
    rows, cols = w.shape
    tr = rows // 4 if rows % 32 == 0 else rows
    c1 = 1.0 - ADAM_B1 ** ADAM_STEP
    c2 = 1.0 - ADAM_B2 ** ADAM_STEP

    def body(w_ref, g_ref, m_ref, v_ref, d_ref, mo_ref, vo_ref):
        g = g_ref[...]
        mn = ADAM_B1 * m_ref[...] + (1.0 - ADAM_B1) * g
        vn = ADAM_B2 * v_ref[...] + (1.0 - ADAM_B2) * (g * g)
        mo_ref[...] = mn
        vo_ref[...] = vn
        d_ref[...] = -ADAM_LR * ((mn / c1) / (jnp.sqrt(vn / c2) + ADAM_EPS) + ADAM_WD * w_ref[...])

    spec = pl.BlockSpec((tr, cols), lambda i: (i, 0))
    shape = jax.ShapeDtypeStruct(w.shape, F32)
    return pl.pallas_call(body, grid=(rows // tr,), in_specs=[spec] * 4, out_specs=[spec] * 3,
                          out_shape=[shape] * 3, compiler_params=_params("parallel"), name=name)(w, g, m, v)


GRAD_NAMES = ("in", "sb", "cv", "out")


def _place_scalars():
    return jnp.stack([2 * lax.axis_index("x") + lax.axis_index("y"), lax.axis_index("c")]).astype(jnp.int32)


def _pair_sums(grads):
    place = _place_scalars()
    theirs = _pair_exchange(*grads)
    return [_pair_sum(place, grads[a], theirs[a], a == 0, "pair_sum_" + GRAD_NAMES[a]) for a in range(4)]


def _finish_weight_grads(pair, got):
    place = _place_scalars()
    done = [_chip_sum(place, pair[a], got[a], a == 0, "chip_sum_" + GRAD_NAMES[a]) for a in range(4)]
    return _pair_share(*done)


def kernel(x, ln_in_g, ln_in_b, w_in, w_sb_proj, conv_w, conv_b, conv_ln_g, conv_ln_b, w_cv_proj, w_out, ln_post_g, ln_post_b, loss_target, m_ln_in_g, m_ln_in_b, m_w_in, m_w_sb_proj, m_conv_w, m_conv_b, m_conv_ln_g, m_conv_ln_b, m_w_cv_proj, m_w_out, m_ln_post_g, m_ln_post_b, v_ln_in_g, v_ln_in_b, v_w_in, v_w_sb_proj, v_conv_w, v_conv_b, v_conv_ln_g, v_conv_ln_b, v_w_cv_proj, v_w_out, v_ln_post_g, v_ln_post_b):
    nb, seq, _ = x.shape
    t = nb * seq
    vec_names = ("ln_in_g", "ln_in_b", "conv_b", "conv_ln_g", "conv_ln_b", "ln_post_g", "ln_post_b")
    vec_w = dict(zip(vec_names, (ln_in_g, ln_in_b, conv_b, conv_ln_g, conv_ln_b, ln_post_g, ln_post_b)))
    vec_m = dict(zip(vec_names, (m_ln_in_g, m_ln_in_b, m_conv_b, m_conv_ln_g, m_conv_ln_b, m_ln_post_g, m_ln_post_b)))
    vec_v = dict(zip(vec_names, (v_ln_in_g, v_ln_in_b, v_conv_b, v_conv_ln_g, v_conv_ln_b, v_ln_post_g, v_ln_post_b)))
    vecs = {k: a.reshape(1, D_MODEL) for k, a in vec_w.items()}

    pad_taps = lambda a: jnp.pad(a.reshape(CONV_K, SHARD_SQ), ((0, CONV_ROWS - CONV_K), (0, 0)))
    shards = (w_in[0].astype(BF16), w_sb_proj[0].astype(BF16), w_cv_proj[0].astype(BF16), w_out[0].astype(BF16),
              pad_taps(conv_w))
    x2 = x.reshape(t, D_MODEL)
    full_in, dhs, dr, big, small = _forward_backward(x2, loss_target.reshape(t, D_MODEL), nb, seq, shards, vecs)
    pair = _pair_sums(big)
    grad_x, st_in, got = _in_proj_bwd(dhs, full_in, dr, x2, vecs["ln_in_g"], pair)
    g_in, g_sb, g_cv, g_out = _finish_weight_grads(pair, got)
    small = _small_allreduce(jnp.concatenate([st_in[0:2], small], axis=0))

    chip = 2 * lax.axis_index("x") + lax.axis_index("y")
    g_conv = lax.dynamic_slice(small, (8, chip * SHARD_SQ), (CONV_ROWS, SHARD_SQ))
    loss = small[7, 0]

    grads, deltas, new_m, new_v = {}, {}, {}, {}
    stack = lambda d: jnp.concatenate([d[k].reshape(1, D_MODEL) for k in vec_names] + [jnp.zeros((1, D_MODEL), F32)])
    vd, vm, vv = _adamw(stack(vec_w), small[0:8], stack(vec_m), stack(vec_v), "adamw_vectors")
    for r, k in enumerate(vec_names):
        shape = vec_w[k].shape
        grads[k] = small[r].reshape(shape)
        deltas[k], new_m[k], new_v[k] = vd[r].reshape(shape), vm[r].reshape(shape), vv[r].reshape(shape)

    big_w = {"w_in": (w_in, g_in, m_w_in, v_w_in), "w_sb_proj": (w_sb_proj, g_sb, m_w_sb_proj, v_w_sb_proj),
             "w_cv_proj": (w_cv_proj, g_cv, m_w_cv_proj, v_w_cv_proj), "w_out": (w_out, g_out, m_w_out, v_w_out)}
    for k, (w, g, m, v) in big_w.items():
        d, mn, vn = _adamw(w[0], g, m[0], v[0], "adamw_" + k)
        grads[k], deltas[k], new_m[k], new_v[k] = g[None], d[None], mn[None], vn[None]
    d, mn, vn = _adamw(pad_taps(conv_w), g_conv, pad_taps(m_conv_w), pad_taps(v_conv_w), "adamw_conv_w")
    grads["conv_w"] = g_conv[None, :CONV_K]
    deltas["conv_w"], new_m["conv_w"], new_v["conv_w"] = d[None, :CONV_K], mn[None, :CONV_K], vn[None, :CONV_K]

    order = ("ln_in_g", "ln_in_b", "w_in", "w_sb_proj", "conv_w", "conv_b", "conv_ln_g", "conv_ln_b",
             "w_cv_proj", "w_out", "ln_post_g", "ln_post_b")
    return (loss, grad_x.reshape(x.shape), *[grads[k] for k in order], *[deltas[k] for k in order],
            *[new_m[k] for k in order], *[new_v[k] for k in order])
```

```python
import functools

import jax
import jax.numpy as jnp
from jax import lax
from jax.experimental import pallas as pl
from jax.experimental.pallas import tpu as pltpu

F32 = jnp.float32
BF16 = jnp.bfloat16

D_MODEL = 1024
N_HEADS = 16
HEAD_DIM = 64
HEAD_GROUP = 4
GROUP_W = HEAD_GROUP * HEAD_DIM
N_GROUPS = N_HEADS // HEAD_GROUP
N_PIECES = 9
IN_WIDTH = N_PIECES * D_MODEL
Q_BLOCK = 256
Q_TILE = 2 * Q_BLOCK
CONV_K = 31
CONV_ROWS = 32
HALO = 32
LN_EPS = 1e-5
ALPHA = 2.0 ** 0.25
Q_SCALE = 0.125
N_CHIPS = 4
N_DEV = 8
SHARD_IN = IN_WIDTH // N_CHIPS
SHARD_SQ = D_MODEL // N_CHIPS
HALF = D_MODEL // 2
SMALL_ROWS = 40
N_CHUNKS = 16

ADAM_LR = 0.001
ADAM_B1 = 0.9
ADAM_B2 = 0.999
ADAM_EPS = 1e-08
ADAM_WD = 0.01
ADAM_STEP = 10

MESH = pl.DeviceIdType.MESH
ANY = pl.BlockSpec(memory_space=pl.ANY)
VMEM_LIMIT = 60 * 1024 * 1024

NT = (((1,), (1,)), ((), ()))
TN = (((0,), (0,)), ((), ()))


def _sigmoid(x):
    return 1.0 / (1.0 + jnp.exp(-x))


def _dot(a, b):
    return jnp.dot(a, b, preferred_element_type=F32)


def _sum_along_keys(x, tri_twice):
    return _dot(jnp.concatenate(_split_bf16(x), axis=1), tri_twice)


def _dot_nt(a, b):
    return lax.dot_general(a, b, NT, preferred_element_type=F32)


def _dot_tn(a, b):
    return lax.dot_general(a, b, TN, preferred_element_type=F32)


def _split_bf16(x):
    hi = x.astype(BF16)
    lo = (x - hi.astype(F32)).astype(BF16)
    return hi, lo


def _ln_stats(x):
    mu = jnp.mean(x, axis=-1, keepdims=True)
    xc = x - mu
    var = jnp.mean(xc * xc, axis=-1, keepdims=True)
    rstd = lax.rsqrt(var + LN_EPS)
    return xc * rstd, rstd


def _ln_bwd(dy, xhat, rstd, g):
    dxh = dy * g
    m1 = jnp.mean(dxh, axis=-1, keepdims=True)
    m2 = jnp.mean(dxh * xhat, axis=-1, keepdims=True)
    return rstd * (dxh - m1 - xhat * m2)


def _params(*sem):
    return pltpu.CompilerParams(dimension_semantics=sem, vmem_limit_bytes=VMEM_LIMIT)


def _ln_in(x, g, b):
    t = x.shape[0]
    tm = min(512, t)

    def body(x_ref, g_ref, b_ref, hf_ref, hb_ref):
        xhat, _ = _ln_stats(x_ref[...])
        y = xhat * g_ref[...] + b_ref[...]
        hf_ref[...] = y
        hb_ref[...] = y.astype(BF16)

    row = pl.BlockSpec((tm, D_MODEL), lambda i: (i, 0))
    vec = pl.BlockSpec((1, D_MODEL), lambda i: (0, 0))
    return pl.pallas_call(
        body, grid=(t // tm,), in_specs=[row, vec, vec], out_specs=[row, row],
        out_shape=[jax.ShapeDtypeStruct((t, D_MODEL), F32), jax.ShapeDtypeStruct((t, D_MODEL), BF16)],
        compiler_params=_params("parallel"), name="ln_in")(x, g, b)


def _in_proj(h, w, col0, ncol, out_dtype, scale_first, name):
    t = h.shape[0]
    tm = min(1024, t)

    def body(h_ref, w_ref, o_ref):
        res = _dot(h_ref[...], w_ref[...])
        if scale_first:
            res = res * jnp.where(pl.program_id(0) == 0, Q_SCALE, 1.0)
        o_ref[...] = res.astype(out_dtype)

    return pl.pallas_call(
        body, grid=(ncol, t // tm),
        in_specs=[pl.BlockSpec((tm, D_MODEL), lambda j, i: (i, 0)),
                  pl.BlockSpec((D_MODEL, D_MODEL), lambda j, i: (0, j + col0))],
        out_specs=pl.BlockSpec((tm, D_MODEL), lambda j, i: (i, j)),
        out_shape=jax.ShapeDtypeStruct((t, ncol * D_MODEL), out_dtype),
        compiler_params=_params("parallel", "parallel"), name=name)(h, w)


def _sb_block(l, diag, tri, carry):
    lnb = jnp.minimum(-l, 0.0) - jnp.log(1.0 + jnp.exp(-jnp.abs(l)))
    if diag is not None:
        lnb = jnp.where(diag, lnb, 0.0)
    suffix = _sum_along_keys(lnb, tri)
    a = jnp.exp(l + suffix + carry)
    if diag is not None:
        a = jnp.where(diag, a, 0.0)
    return a, lnb, suffix[:, 0:1]


def _attn_consts():
    head = lax.broadcasted_iota(jnp.int32, (Q_TILE, GROUP_W), 1) // HEAD_DIM
    row = lax.broadcasted_iota(jnp.int32, (Q_TILE, Q_BLOCK), 0)
    col = lax.broadcasted_iota(jnp.int32, (Q_TILE, Q_BLOCK), 1)
    key = row % Q_BLOCK
    return head, col < row, (key >= col).astype(BF16), (key <= col).astype(BF16)


def _split_heads(t, head):
    zero = jnp.zeros_like(t)
    head = head[:t.shape[0]]
    return [jnp.where(head == h, t, zero) for h in range(HEAD_GROUP)]


def _key_block(ref, kb):
    return ref[pl.ds(pl.multiple_of(kb * Q_BLOCK, Q_BLOCK), Q_BLOCK), :]


def _attn_fwd(qkv, nb, seq, w_in_early, shards):
    nq = seq // Q_TILE

    def body(q_ref, k_ref, v_ref, full_in, s0, s1, s2, s3, s4, o_ref, d0, d1, d2, d3, d4, send_sems, recv_sems, local_sems):
        del full_in
        qi = pl.program_id(2)
        first = (pl.program_id(0) == 0) & (pl.program_id(1) == 0) & (qi == 0)
        last = (pl.program_id(0) == nb - 1) & (pl.program_id(1) == N_GROUPS - 1) & (qi == nq - 1)
        start_gather, finish_gather = _late_gather([s0, s1, s2, s3, s4], [d0, d1, d2, d3, d4],
                                                   send_sems, recv_sems, local_sems)
        pl.when(first)(start_gather)
        head, diag, tri, _ = _attn_consts()
        qh = _split_heads(q_ref[...], head)
        o_ref[...] = jnp.zeros_like(o_ref)
        zero_col = jnp.zeros((Q_BLOCK, 1), F32)

        def block(kb, carries, rows, masked):
            kblk = _key_block(k_ref, kb)
            vstack = jnp.concatenate(_split_heads(_key_block(v_ref, kb), head), axis=0)
            mask = diag[:Q_TILE - rows.start] if masked else None
            weights, out = [], []
            for h in range(HEAD_GROUP):
                a, _, blk_sum = _sb_block(_dot_nt(qh[h][rows], kblk), mask, tri, carries[h][rows])
                weights.append(a.astype(BF16))
                out.append(carries[h][rows] + blk_sum)
            o_ref[rows, :] += _dot(jnp.concatenate(weights, axis=1), vstack)
            return out

        second = block(2 * qi + 1, [jnp.zeros((Q_TILE, 1), F32)] * HEAD_GROUP, slice(Q_BLOCK, Q_TILE), True)
        carries = tuple(jnp.concatenate([zero_col, c], axis=0) for c in second)
        everything = slice(0, Q_TILE)
        carries = tuple(block(2 * qi, carries, everything, True))
        lax.fori_loop(1, 2 * qi + 1, lambda jj, cs: tuple(block(2 * qi - jj, cs, everything, False)), carries)
        pl.when(last)(finish_gather)

    n_shards = len(shards)
    full = [jax.ShapeDtypeStruct((D_MODEL, IN_WIDTH), BF16)] + [jax.ShapeDtypeStruct((D_MODEL, D_MODEL), BF16)] * 3 \
        + [jax.ShapeDtypeStruct((CONV_ROWS, D_MODEL), F32)]
    res = pl.pallas_call(
        body, grid=(nb, N_GROUPS, nq),
        in_specs=[pl.BlockSpec((Q_TILE, GROUP_W), lambda b, g, i: (b * nq + i, g)),
                  pl.BlockSpec((seq, GROUP_W), lambda b, g, i: (b, N_GROUPS + g)),
                  pl.BlockSpec((seq, GROUP_W), lambda b, g, i: (b, 2 * N_GROUPS + g))] + [ANY] * (1 + n_shards),
        out_specs=[pl.BlockSpec((Q_TILE, GROUP_W), lambda b, g, i: (b * nq + i, g))] + [ANY] * n_shards,
        out_shape=[jax.ShapeDtypeStruct((nb * seq, D_MODEL), F32)] + full,
        input_output_aliases={3: 1},
        scratch_shapes=[pltpu.SemaphoreType.DMA((3 * n_shards,)), pltpu.SemaphoreType.DMA((3 * n_shards,)),
                        pltpu.SemaphoreType.DMA((n_shards - 1,))],
        compiler_params=_params("arbitrary", "arbitrary", "arbitrary"), name="attn_fwd")(
            qkv, qkv, qkv, w_in_early, *shards)
    return res[0], res[1:]


def _attn_bwd(qkv, do, nb, seq):
    nq = seq // Q_TILE
    n_kb = seq // Q_BLOCK

    def body(q_ref, k_ref, v_ref, do_ref, dq_ref, dk_ref, dv_ref, g_s, beta_s, dq_acc, dk_acc, dv_acc):
        qi = pl.program_id(2)

        @pl.when(qi == 0)
        def _():
            dk_acc[...] = jnp.zeros_like(dk_acc)
            dv_acc[...] = jnp.zeros_like(dv_acc)

        head, diag, tri_suffix, tri_prefix = _attn_consts()
        qh = _split_heads(q_ref[...], head)
        doh = _split_heads(do_ref[...], head)
        dq_acc[...] = jnp.zeros_like(dq_acc)
        zero_col = jnp.zeros((Q_BLOCK, 1), F32)
        everything = slice(0, Q_TILE)
        second = slice(Q_BLOCK, Q_TILE)

        def block_a(kb, carries, rows, masked):
            ks = pl.multiple_of(kb * Q_BLOCK, Q_BLOCK)
            kblk = _key_block(k_ref, kb)
            vblk = _key_block(v_ref, kb)
            mask = diag[:Q_TILE - rows.start] if masked else None
            weights, out = [], []
            for h in range(HEAD_GROUP):
                l = _dot_nt(qh[h][rows], kblk)
                a, lnb, blk_sum = _sb_block(l, mask, tri_suffix, carries[h][rows])
                beta = jnp.exp(l + lnb)
                if masked:
                    beta = jnp.where(mask, beta, 0.0)
                g_s[h, kb, rows, :] = a * _dot_nt(doh[h][rows], vblk)
                beta_s[h, kb, rows, :] = beta.astype(BF16)
                weights.append(a.astype(BF16))
                out.append(carries[h][rows] + blk_sum)
            dv_acc[pl.ds(ks, Q_BLOCK), :] += _dot_tn(jnp.concatenate(weights, axis=0),
                                                     jnp.concatenate([t[rows] for t in doh], axis=0))
            return out

        part = block_a(2 * qi + 1, [jnp.zeros((Q_TILE, 1), F32)] * HEAD_GROUP, second, True)
        carries = tuple(jnp.concatenate([zero_col, c], axis=0) for c in part)
        carries = tuple(block_a(2 * qi, carries, everything, True))
        lax.fori_loop(1, 2 * qi + 1, lambda jj, cs: tuple(block_a(2 * qi - jj, cs, everything, False)), carries)

        def block_b(kb, prefixes, rows, masked):
            ks = pl.multiple_of(kb * Q_BLOCK, Q_BLOCK)
            kstack = jnp.concatenate(_split_heads(_key_block(k_ref, kb), head), axis=0)
            dls, out = [], []
            for h in range(HEAD_GROUP):
                g = g_s[h, kb, rows, :]
                beta = beta_s[h, kb, rows, :].astype(F32)
                incl = _sum_along_keys(g, tri_prefix)
                before = incl - g + prefixes[h][rows]
                dl = g - beta * (g + before)
                if masked:
                    dl = jnp.where(diag[:Q_TILE - rows.start], dl, 0.0)
                dls.append(dl.astype(BF16))
                out.append(prefixes[h][rows] + incl[:, Q_BLOCK - 1:Q_BLOCK])
            dq_acc[rows, :] += _dot(jnp.concatenate(dls, axis=1), kstack)
            dk_acc[pl.ds(ks, Q_BLOCK), :] += _dot_tn(jnp.concatenate(dls, axis=0),
                                                     jnp.concatenate([t[rows] for t in qh], axis=0))
            return out

        zeros = (jnp.zeros((Q_TILE, 1), F32),) * HEAD_GROUP
        prefixes = lax.fori_loop(0, 2 * qi, lambda kb, ps: tuple(block_b(kb, ps, everything, False)), zeros)
        prefixes = block_b(2 * qi, prefixes, everything, True)
        block_b(2 * qi + 1, prefixes, second, True)
        dq_ref[...] = (dq_acc[...] * Q_SCALE).astype(BF16)

        @pl.when(qi == nq - 1)
        def _():
            dk_ref[...] = dk_acc[...].astype(BF16)
            dv_ref[...] = dv_acc[...].astype(BF16)

    t = nb * seq
    qspec = pl.BlockSpec((Q_TILE, GROUP_W), lambda b, g, i: (b * nq + i, g))
    kvout = pl.BlockSpec((seq, GROUP_W), lambda b, g, i: (b, g))
    return pl.pallas_call(
        body, grid=(nb, N_GROUPS, nq),
        in_specs=[qspec,
                  pl.BlockSpec((seq, GROUP_W), lambda b, g, i: (b, N_GROUPS + g)),
                  pl.BlockSpec((seq, GROUP_W), lambda b, g, i: (b, 2 * N_GROUPS + g)),
                  qspec],
        out_specs=[qspec, kvout, kvout],
        out_shape=[jax.ShapeDtypeStruct((t, D_MODEL), BF16)] * 3,
        scratch_shapes=[pltpu.VMEM((HEAD_GROUP, n_kb, Q_TILE, Q_BLOCK), F32),
                        pltpu.VMEM((HEAD_GROUP, n_kb, Q_TILE, Q_BLOCK), BF16),
                        pltpu.VMEM((Q_TILE, GROUP_W), F32),
                        pltpu.VMEM((seq, GROUP_W), F32), pltpu.VMEM((seq, GROUP_W), F32)],
        compiler_params=_params("parallel", "parallel", "arbitrary"), name="attn_bwd")(qkv, qkv, qkv, do)


def _conv_block_rows(seq):
    return min(256, seq)


SUBLANES = 8
CONV_SUB = 32
CONV_GROUPS = CONV_SUB // SUBLANES


def _fill_shifts(sh_ref, n):
    for r in range(1, SUBLANES):
        sh_ref[r, 0:n, :] = sh_ref[0, pl.ds(r, n), :]


def _shifted(sh_ref, start, cs):
    r = start % SUBLANES
    return sh_ref[r, start - r:start - r + CONV_SUB, cs].reshape(CONV_GROUPS, SUBLANES, -1)


def _tap(w_ref, k, cs):
    row = w_ref[k:k + 1, cs]
    return jnp.broadcast_to(row, (SUBLANES, row.shape[1]))[None]


def _tree_sum(x):
    parts = [x[i] for i in range(x.shape[0])]
    while len(parts) > 1:
        parts = [parts[i] + parts[i + 1] for i in range(0, len(parts), 2)]
    return parts[0]


def _conv_fwd(hrest, conv_w, conv_b, ln_g, ln_b, nb, seq):
    ts = _conv_block_rows(seq)
    nblk = seq // ts
    lanes = 128

    def body(cv_ref, cg_ref, cvh_ref, cgh_ref, z_ref, w_ref, cb_ref, g_ref, b_ref, u1_ref, cvin_ref, ush):
        i = pl.program_id(1)
        halo = cvh_ref[...] * _sigmoid(cgh_ref[...])
        ush[0, 0:HALO, :] = jnp.where(i > 0, halo, 0.0)
        ush[0, HALO:HALO + ts, :] = cv_ref[...] * _sigmoid(cg_ref[...])
        _fill_shifts(ush, HALO + ts - SUBLANES)
        for cc in range(D_MODEL // lanes):
            cs = slice(cc * lanes, (cc + 1) * lanes)
            for r0 in range(0, ts, CONV_SUB):
                acc = jnp.zeros((CONV_GROUPS, SUBLANES, lanes), F32) + _tap(cb_ref, 0, cs)
                for k in range(CONV_K):
                    acc = acc + _tap(w_ref, k, cs) * _shifted(ush, HALO - CONV_K + 1 + k + r0, cs)
                u1_ref[r0:r0 + CONV_SUB, cs] = acc.reshape(CONV_SUB, lanes)
        xhat, _ = _ln_stats(u1_ref[...])
        u2 = xhat * g_ref[...] + b_ref[...]
        z = z_ref[...]
        cvin_ref[...] = (u2 * _sigmoid(u2) * z * _sigmoid(z)).astype(BF16)

    def main(colblk):
        return pl.BlockSpec((ts, D_MODEL), lambda b, i: (b * nblk + i, colblk))

    def halo(colblk):
        return pl.BlockSpec((HALO, D_MODEL),
                            lambda b, i: (jnp.maximum((b * seq + i * ts) // HALO - 1, 0), colblk))

    vec = pl.BlockSpec((1, D_MODEL), lambda b, i: (0, 0))
    t = nb * seq
    return pl.pallas_call(
        body, grid=(nb, nblk),
        in_specs=[main(1), main(2), halo(1), halo(2), main(3),
                  pl.BlockSpec((CONV_ROWS, D_MODEL), lambda b, i: (0, 0)), vec, vec, vec],
        out_specs=[main(0), main(0)],
        out_shape=[jax.ShapeDtypeStruct((t, D_MODEL), F32), jax.ShapeDtypeStruct((t, D_MODEL), BF16)],
        scratch_shapes=[pltpu.VMEM((SUBLANES, HALO + ts, D_MODEL), F32)],
        compiler_params=_params("parallel", "parallel"), name="conv_fwd")(
            hrest, hrest, hrest, hrest, hrest, conv_w, conv_b, ln_g, ln_b)


def _tail(o, hrest, cvin, h0, tgt, w_sb, w_cv, w_out, ln_g, ln_b):
    t = o.shape[0]
    tm = min(256, t)

    def body(o_ref, z_ref, gs_ref, gc_ref, cvin_ref, h0_ref, tgt_ref, wsb_ref, wcv_ref, wout_ref, g_ref, b_ref,
             dr_ref, drb_ref, a_ref, mg_ref, dysb_ref, dycv_ref, do_ref, dz_ref, dgs_ref, dgc_ref, dcvin_ref,
             st_ref):
        @pl.when(pl.program_id(0) == 0)
        def _():
            st_ref[...] = jnp.zeros_like(st_ref)

        o = o_ref[...]
        z = z_ref[...]
        sz = _sigmoid(z)
        a = (o * z * sz).astype(BF16)
        a_ref[...] = a
        y_sb = _dot(a, wsb_ref[...])
        y_cv = _dot(cvin_ref[...], wcv_ref[...])
        s_sb = _sigmoid(gs_ref[...])
        s_cv = _sigmoid(gc_ref[...])
        merged = (s_sb * y_sb + s_cv * y_cv).astype(BF16)
        mg_ref[...] = merged
        r = ALPHA * h0_ref[...] + _dot(merged, wout_ref[...])
        xhat, rstd = _ln_stats(r)
        g = g_ref[...]
        err = xhat * g + b_ref[...] - tgt_ref[...]
        dy = err * (1.0 / D_MODEL)
        st_ref[0:1, :] += jnp.sum(dy * xhat, axis=0, keepdims=True)
        st_ref[1:2, :] += jnp.sum(dy, axis=0, keepdims=True)
        st_ref[2:3, :] += (0.5 / D_MODEL) * jnp.sum(err * err, axis=0, keepdims=True)
        dr = _ln_bwd(dy, xhat, rstd, g)
        dr_ref[...] = dr
        drb = dr.astype(BF16)
        drb_ref[...] = drb
        dm = _dot_nt(drb, wout_ref[...])
        dy_sb = (dm * s_sb).astype(BF16)
        dy_cv = (dm * s_cv).astype(BF16)
        dysb_ref[...] = dy_sb
        dycv_ref[...] = dy_cv
        dgs_ref[...] = (dm * y_sb * s_sb * (1.0 - s_sb)).astype(BF16)
        dgc_ref[...] = (dm * y_cv * s_cv * (1.0 - s_cv)).astype(BF16)
        da = _dot_nt(dy_sb, wsb_ref[...])
        dcvin_ref[...] = _dot_nt(dy_cv, wcv_ref[...])
        do_ref[...] = (da * z * sz).astype(BF16)
        dz_ref[...] = (da * o * sz * (1.0 + z * (1.0 - sz))).astype(BF16)

    def tok(colblk=0):
        return pl.BlockSpec((tm, D_MODEL), lambda i: (i, colblk))

    wspec = pl.BlockSpec((D_MODEL, D_MODEL), lambda i: (0, 0), pipeline_mode=pl.Buffered(1))
    vec = pl.BlockSpec((1, D_MODEL), lambda i: (0, 0))
    bf = jax.ShapeDtypeStruct((t, D_MODEL), BF16)
    f32 = jax.ShapeDtypeStruct((t, D_MODEL), F32)
    return pl.pallas_call(
        body, grid=(t // tm,),
        in_specs=[tok(), tok(0), tok(4), tok(5), tok(), tok(), tok(), wspec, wspec, wspec, vec, vec],
        out_specs=[tok()] * 11 + [pl.BlockSpec((8, D_MODEL), lambda i: (0, 0))],
        out_shape=[f32] + [bf] * 9 + [f32, jax.ShapeDtypeStruct((8, D_MODEL), F32)],
        compiler_params=_params("arbitrary"), name="tail")(
            o, hrest, hrest, hrest, cvin, h0, tgt, w_sb, w_cv, w_out, ln_g, ln_b)


def _conv_branch_bwd(dcvin, u1, hrest, ln_g, ln_b):
    t = u1.shape[0]
    tm = min(512, t)

    def body(dc_ref, u1_ref, z_ref, g_ref, b_ref, du1_ref, dz_ref, st_ref):
        @pl.when(pl.program_id(0) == 0)
        def _():
            st_ref[...] = jnp.zeros_like(st_ref)

        xhat, rstd = _ln_stats(u1_ref[...])
        g = g_ref[...]
        u2 = xhat * g + b_ref[...]
        s2 = _sigmoid(u2)
        z = z_ref[...]
        sz = _sigmoid(z)
        dc = dc_ref[...]
        dz_ref[...] = (dc * (u2 * s2) * sz * (1.0 + z * (1.0 - sz))).astype(BF16)
        du2 = dc * (z * sz) * s2 * (1.0 + u2 * (1.0 - s2))
        st_ref[0:1, :] += jnp.sum(du2 * xhat, axis=0, keepdims=True)
        st_ref[1:2, :] += jnp.sum(du2, axis=0, keepdims=True)
        du1 = _ln_bwd(du2, xhat, rstd, g)
        du1_ref[...] = du1
        st_ref[2:3, :] += jnp.sum(du1, axis=0, keepdims=True)

    tok = pl.BlockSpec((tm, D_MODEL), lambda i: (i, 0))
    vec = pl.BlockSpec((1, D_MODEL), lambda i: (0, 0))
    return pl.pallas_call(
        body, grid=(t // tm,),
        in_specs=[tok, tok, pl.BlockSpec((tm, D_MODEL), lambda i: (i, 3)), vec, vec],
        out_specs=[tok, tok, pl.BlockSpec((8, D_MODEL), lambda i: (0, 0))],
        out_shape=[jax.ShapeDtypeStruct((t, D_MODEL), F32), jax.ShapeDtypeStruct((t, D_MODEL), BF16),
                   jax.ShapeDtypeStruct((8, D_MODEL), F32)],
        compiler_params=_params("arbitrary"), name="conv_branch_bwd")(dcvin, u1, hrest, ln_g, ln_b)


def _conv_bwd(du1, hrest, conv_w, nb, seq):
    ts = _conv_block_rows(seq)
    nblk = seq // ts
    lanes = 128
    last_halo = nb * seq // HALO - 1

    def body(du_ref, duh_ref, cv_ref, cg_ref, cvh_ref, cgh_ref, w_ref, dcv_ref, dcg_ref, dw_ref, ush, dsh, dw_part):
        b = pl.program_id(0)
        i = pl.program_id(1)

        @pl.when((b == 0) & (i == 0))
        def _():
            dw_part[...] = jnp.zeros_like(dw_part)

        cv = cv_ref[...]
        sg = _sigmoid(cg_ref[...])
        halo = cvh_ref[...] * _sigmoid(cgh_ref[...])
        ush[0, 0:HALO, :] = jnp.where(i > 0, halo, 0.0)
        ush[0, HALO:HALO + ts, :] = cv * sg
        dsh[0, 0:ts, :] = du_ref[...]
        dsh[0, ts:ts + HALO, :] = jnp.where(i < nblk - 1, duh_ref[...], 0.0)
        _fill_shifts(ush, HALO + ts - SUBLANES)
        _fill_shifts(dsh, HALO + ts - SUBLANES)
        for cc in range(D_MODEL // lanes):
            cs = slice(cc * lanes, (cc + 1) * lanes)
            for r0 in range(0, ts, CONV_SUB):
                rows = slice(r0, r0 + CONV_SUB)
                dmain = _shifted(dsh, r0, cs)
                acc = jnp.zeros((CONV_GROUPS, SUBLANES, lanes), F32)
                for k in range(CONV_K):
                    acc = acc + _tap(w_ref, k, cs) * _shifted(dsh, CONV_K - 1 - k + r0, cs)
                    dw_part[k, :, cs] += _tree_sum(dmain * _shifted(ush, HALO - CONV_K + 1 + k + r0, cs))
                acc = acc.reshape(CONV_SUB, lanes)
                dcv_ref[rows, cs] = (acc * sg[rows, cs]).astype(BF16)
                dcg_ref[rows, cs] = (acc * cv[rows, cs] * sg[rows, cs] * (1.0 - sg[rows, cs])).astype(BF16)

        @pl.when((b == nb - 1) & (i == nblk - 1))
        def _():
            for k in range(CONV_ROWS):
                dw_ref[k:k + 1, :] = jnp.sum(dw_part[k], axis=0, keepdims=True)

    def main(colblk):
        return pl.BlockSpec((ts, D_MODEL), lambda b, i: (b * nblk + i, colblk))

    def halo_before(colblk):
        return pl.BlockSpec((HALO, D_MODEL),
                            lambda b, i: (jnp.maximum((b * seq + i * ts) // HALO - 1, 0), colblk))

    halo_after = pl.BlockSpec((HALO, D_MODEL),
                              lambda b, i: (jnp.minimum((b * seq + (i + 1) * ts) // HALO, last_halo), 0))
    t = nb * seq
    return pl.pallas_call(
        body, grid=(nb, nblk),
        in_specs=[main(0), halo_after, main(1), main(2), halo_before(1), halo_before(2),
                  pl.BlockSpec((CONV_ROWS, D_MODEL), lambda b, i: (0, 0))],
        out_specs=[main(0), main(0), pl.BlockSpec((CONV_ROWS, D_MODEL), lambda b, i: (0, 0))],
        out_shape=[jax.ShapeDtypeStruct((t, D_MODEL), BF16), jax.ShapeDtypeStruct((t, D_MODEL), BF16),
                   jax.ShapeDtypeStruct((CONV_ROWS, D_MODEL), F32)],
        scratch_shapes=[pltpu.VMEM((SUBLANES, HALO + ts, D_MODEL), F32), pltpu.VMEM((SUBLANES, ts + HALO, D_MODEL), F32),
                        pltpu.VMEM((CONV_ROWS, SUBLANES, D_MODEL), F32)],
        compiler_params=_params("arbitrary", "arbitrary"), name="conv_bwd")(
            du1, du1, hrest, hrest, hrest, hrest, conv_w)


def _weight_grad(x, dys, name):
    n = len(dys)
    t = x.shape[0]
    tm = min(512, t)

    def body(x_ref, *refs):
        o_ref = refs[n]
        j = pl.program_id(0)

        @pl.when(pl.program_id(1) == 0)
        def _():
            o_ref[...] = jnp.zeros_like(o_ref)

        for p in range(n):
            @pl.when(j == p)
            def _(p=p):
                o_ref[...] += _dot_tn(x_ref[...], refs[p][...])

    def dy_spec(p):
        return pl.BlockSpec((tm, D_MODEL), lambda j, i: (jnp.where(j == p, i, 0), 0))

    return pl.pallas_call(
        body, grid=(n, t // tm),
        in_specs=[pl.BlockSpec((tm, D_MODEL), lambda j, i: (i, 0))] + [dy_spec(p) for p in range(n)],
        out_specs=pl.BlockSpec((D_MODEL, D_MODEL), lambda j, i: (0, j)),
        out_shape=jax.ShapeDtypeStruct((D_MODEL, n * D_MODEL), F32),
        compiler_params=_params("parallel", "arbitrary"), name=name)(x, *dys)


def _in_proj_bwd(dhs, w_in, dr, x, ln_g, pair):
    n = len(dhs)
    t = x.shape[0]
    tm = min(512, t)
    n_i = t // tm

    def body(*refs):
        dh_refs = refs[:n]
        w_ref, dr_ref, x_ref, g_ref = refs[n:n + 4]
        pair_refs = refs[n + 4:n + 8]
        gx_ref, st_ref = refs[n + 8:n + 10]
        got_refs = refs[n + 10:n + 14]
        acc, send_sems, recv_sems = refs[n + 14:]
        i = pl.program_id(0)
        p = pl.program_id(1)
        pieces, wholes = _chip_exchange_copies(pair_refs, got_refs, send_sems, recv_sems)

        @pl.when((i == 0) & (p == 0))
        def _():
            st_ref[...] = jnp.zeros_like(st_ref)
            for cp in pieces:
                cp.start()

        @pl.when((i == n_i - 1) & (p == n - 1))
        def _():
            for cp in wholes:
                cp.wait_recv()
            for cp in wholes:
                cp.wait_send()

        @pl.when(p == 0)
        def _():
            acc[...] = ALPHA * dr_ref[...]

        for q in range(n):
            @pl.when(p == q)
            def _(q=q):
                acc[...] += _dot_nt(dh_refs[q][...], w_ref[...])

        @pl.when(p == n - 1)
        def _():
            xhat, rstd = _ln_stats(x_ref[...])
            dh0 = acc[...]
            st_ref[0:1, :] += jnp.sum(dh0 * xhat, axis=0, keepdims=True)
            st_ref[1:2, :] += jnp.sum(dh0, axis=0, keepdims=True)
            gx_ref[...] = _ln_bwd(dh0, xhat, rstd, g_ref[...])

    tok = pl.BlockSpec((tm, D_MODEL), lambda i, p: (i, 0))
    got = [jax.ShapeDtypeStruct((3, HALF, SHARD_IN), BF16)] + [jax.ShapeDtypeStruct((3, SHARD_SQ, HALF), BF16)] * 3
    res = pl.pallas_call(
        body, grid=(n_i, n),
        in_specs=[tok] * n + [pl.BlockSpec((D_MODEL, D_MODEL), lambda i, p: (0, p)), tok, tok,
                              pl.BlockSpec((1, D_MODEL), lambda i, p: (0, 0))] + [ANY] * 4,
        out_specs=[tok, pl.BlockSpec((8, D_MODEL), lambda i, p: (0, 0))] + [ANY] * 4,
        out_shape=[jax.ShapeDtypeStruct((t, D_MODEL), F32), jax.ShapeDtypeStruct((8, D_MODEL), F32)] + got,
        scratch_shapes=[pltpu.VMEM((tm, D_MODEL), F32), pltpu.SemaphoreType.DMA((12,)), pltpu.SemaphoreType.DMA((12,))],
        compiler_params=_params("arbitrary", "arbitrary"), name="in_proj_bwd")(*dhs, w_in, dr, x, ln_g, *pair)
    return res[0], res[1], res[2:]


def _forward_backward(x, tgt, nb, seq, shards, vecs):
    w_in_early = _gather_first(shards[0])
    h0, h0b = _ln_in(x, vecs["ln_in_g"], vecs["ln_in_b"])
    qkv = _in_proj(h0b, w_in_early, 0, 3, BF16, True, "in_proj_qkv")
    o, (w_in, w_sb, w_cv, w_out, conv_w) = _attn_fwd(qkv, nb, seq, w_in_early, shards)
    hrest = _in_proj(h0b, w_in, 3, 6, F32, False, "in_proj_rest")
    u1, cvin = _conv_fwd(hrest, conv_w, vecs["conv_b"], vecs["conv_ln_g"], vecs["conv_ln_b"], nb, seq)
    (dr, drb, a, merged, dy_sb, dy_cv, do, dz_sb, dg_sb, dg_cv, dcvin, st_tail) = _tail(
        o, hrest, cvin, h0, tgt, w_sb, w_cv, w_out, vecs["ln_post_g"], vecs["ln_post_b"])
    d_w_sb = _weight_grad(a, [dy_sb], "grad_w_sb")
    d_w_cv = _weight_grad(cvin, [dy_cv], "grad_w_cv")
    d_w_out = _weight_grad(merged, [drb], "grad_w_out")
    du1, dz_cv, st_conv = _conv_branch_bwd(dcvin, u1, hrest, vecs["conv_ln_g"], vecs["conv_ln_b"])
    dc_val, dc_gate, d_conv_w = _conv_bwd(du1, hrest, conv_w, nb, seq)
    dq, dk, dv = _attn_bwd(qkv, do, nb, seq)
    dhs = [dq, dk, dv, dz_sb, dc_val, dc_gate, dz_cv, dg_sb, dg_cv]
    d_w_in = _weight_grad(h0b, dhs, "grad_w_in")
    small = jnp.concatenate([st_conv[2:3], st_conv[0:2], st_tail[0:3], d_conv_w], axis=0)
    return w_in, dhs, dr, (d_w_in, d_w_sb, d_w_cv, d_w_out), small


def _place():
    x, y, c = lax.axis_index("x"), lax.axis_index("y"), lax.axis_index("c")
    chips = [(1 - x, y), (x, 1 - y), (1 - x, 1 - y)]
    return x, y, c, chips


N_EARLY = -(-3 * D_MODEL // SHARD_IN)
GATHER_PARTS = 16


def _w_in_slot(full, chip):
    return full.at[:, pl.ds(pl.multiple_of(chip * SHARD_IN, 128), SHARD_IN)]


def _gather_first(w_in):
    part_rows = HALF // GATHER_PARTS
    n_sem = 3 * GATHER_PARTS
    across_x, across_y, diagonal = 0, 1, 2

    def body(src, dst, ici_send, ici_recv, fwd_send, fwd_recv, local_sem):
        x, y, c, chips = _place()
        me = 2 * x + y
        sibling = (x, y, 1 - c)
        owner = [2 * px + py for px, py in chips]

        def part(ref, k, half):
            return ref.at[pl.ds(pl.multiple_of(half * HALF + k * part_rows, 16), part_rows), :]

        def held(j, k, half):
            return part(_w_in_slot(dst, owner[j]), k, half)

        def sent(j, k):
            px, py = chips[j]
            return _remote(part(src, k, c), part(_w_in_slot(dst, me), k, c), ici_send.at[j * GATHER_PARTS + k],
                           ici_recv.at[j * GATHER_PARTS + k], (px, py, c))

        def relayed(k):
            px, py = chips[across_y]
            here = held(across_x, k, c)
            return _remote(here, here, ici_send.at[diagonal * GATHER_PARTS + k],
                           ici_recv.at[diagonal * GATHER_PARTS + k], (px, py, c))

        def landed(j, k):
            px, py = chips[across_y if j == diagonal else j]
            return _remote(held(j, k, c), held(j, k, c), ici_send.at[j * GATHER_PARTS + k],
                           ici_recv.at[j * GATHER_PARTS + k], (px, py, c))

        def passed(j, k, half):
            return _remote(held(j, k, half), held(j, k, half), fwd_send.at[j * GATHER_PARTS + k],
                           fwd_recv.at[j * GATHER_PARTS + k], sibling)

        local = pltpu.make_async_copy(src, _w_in_slot(dst, me), local_sem)
        local.start()

        @pl.when(me < N_EARLY)
        def _():
            for j in (across_x, across_y):
                for k in range(GATHER_PARTS):
                    sent(j, k).start()

        for j in (across_x, across_y, diagonal):
            @pl.when(owner[j] < N_EARLY)
            def _(j=j):
                for k in range(GATHER_PARTS):
                    landed(j, k).wait_recv()
                    passed(j, k, c).start()
                    if j == across_x:
                        relayed(k).start()

        for j in (across_x, across_y, diagonal):
            @pl.when(owner[j] < N_EARLY)
            def _(j=j):
                for k in range(GATHER_PARTS):
                    passed(j, k, 1 - c).wait_recv()
                for k in range(GATHER_PARTS):
                    passed(j, k, c).wait_send()
                    if j == across_x:
                        relayed(k).wait_send()

        @pl.when(me < N_EARLY)
        def _():
            for j in (across_x, across_y):
                for k in range(GATHER_PARTS):
                    sent(j, k).wait_send()

        local.wait()

    return pl.pallas_call(
        body, in_specs=[ANY], out_specs=ANY, out_shape=jax.ShapeDtypeStruct((D_MODEL, IN_WIDTH), BF16),
        scratch_shapes=[pltpu.SemaphoreType.DMA((n_sem,)), pltpu.SemaphoreType.DMA((n_sem,)),
                        pltpu.SemaphoreType.DMA((n_sem,)), pltpu.SemaphoreType.DMA((n_sem,)),
                        pltpu.SemaphoreType.DMA],
        name="gather_first")(w_in)


def _late_gather(srcs, dsts, send_sems, recv_sems, local_sems):
    x, y, c, chips = _place()
    me = 2 * x + y
    small = range(1, 5)

    def slot(a, chip):
        if a == 0:
            return _w_in_slot(dsts[0], chip)
        if a == 4:
            return dsts[a].at[:, pl.ds(pl.multiple_of(chip * SHARD_SQ, 128), SHARD_SQ)]
        return dsts[a].at[pl.ds(pl.multiple_of(chip * SHARD_SQ, 16), SHARD_SQ), :]

    def sems(a, j):
        return send_sems.at[3 * a + j], recv_sems.at[3 * a + j]

    def whole(a, j):
        px, py = chips[j]
        return _remote(srcs[a], slot(a, me), *sems(a, j), (px, py, c))

    def arrival(a, j):
        px, py = chips[j]
        return _arrival(slot(a, 2 * px + py), *sems(a, j), (px, py, c))

    def local(a):
        return pltpu.make_async_copy(srcs[a], slot(a, me), local_sems.at[a - 1])

    def start():
        for a in small:
            local(a).start()
            for j in range(3):
                whole(a, j).start()

        @pl.when(me >= N_EARLY)
        def _():
            for j, (px, py) in enumerate(chips):
                for cp in _chunk_copies(srcs[0], slot(0, me), GATHER_PARTS, *sems(0, j), (px, py, c)):
                    cp.start()

    def finish():
        for j, (px, py) in enumerate(chips):
            for a in small:
                arrival(a, j).wait_recv()

            @pl.when(2 * px + py >= N_EARLY)
            def _(j=j):
                arrival(0, j).wait_recv()

        for a in small:
            for j in range(3):
                whole(a, j).wait_send()
            local(a).wait()

        @pl.when(me >= N_EARLY)
        def _():
            for j in range(3):
                whole(0, j).wait_send()

    return start, finish


def _remote(src, dst, send_sem, recv_sem, device):
    return pltpu.make_async_remote_copy(src_ref=src, dst_ref=dst, send_sem=send_sem, recv_sem=recv_sem,
                                        device_id=device, device_id_type=MESH)


def _chunk_copies(src, dst, n, send_sem, recv_sem, device):
    rows = src.shape[0] // n
    return [_remote(src.at[pl.ds(k * rows, rows)], dst.at[pl.ds(k * rows, rows)], send_sem, recv_sem, device)
            for k in range(n)]


def _row_chunks(src, dst, n, send_sem, recv_sem, device):
    return _chunk_copies(src, dst, n, send_sem, recv_sem, device), _remote(src, dst, send_sem, recv_sem, device)


def _arrival(dst, send_sem, recv_sem, device):
    return _remote(dst, dst, send_sem, recv_sem, device)


def _pair_exchange(d_in, d_sb, d_cv, d_out):
    def body(i0, i1, i2, i3, r0, r1, r2, r3, send_sems, recv_sems):
        x, y, c, _ = _place()
        srcs = [i0, i1, i2, i3]
        theirs = [r0, r1, r2, r3]
        sibling = (x, y, 1 - c)
        start = pl.multiple_of((1 - c) * HALF, 128)
        views = [srcs[0].at[pl.ds(start, HALF), :]] + [srcs[a].at[:, pl.ds(start, HALF)] for a in range(1, 4)]
        pieces, wholes = [], []
        for a in range(4):
            cps, whole = _row_chunks(views[a], theirs[a], N_CHUNKS, send_sems.at[a], recv_sems.at[a], sibling)
            pieces += cps
            wholes.append(whole)
        for cp in pieces:
            cp.start()
        for cp in wholes:
            cp.wait_recv()
        for cp in wholes:
            cp.wait_send()

    shapes = [jax.ShapeDtypeStruct((HALF, IN_WIDTH), F32)] + [jax.ShapeDtypeStruct((D_MODEL, HALF), F32)] * 3
    return pl.pallas_call(
        body, in_specs=[ANY] * 4, out_specs=[ANY] * 4, out_shape=shapes,
        scratch_shapes=[pltpu.SemaphoreType.DMA((4,)), pltpu.SemaphoreType.DMA((4,))],
        name="pair_exchange")(d_in, d_sb, d_cv, d_out)


def _chip_exchange_copies(srcs, theirs, send_sems, recv_sems):
    _, _, c, chips = _place()

    def shard(a, px, py):
        chip = 2 * px + py
        if a == 0:
            return srcs[a].at[:, pl.ds(pl.multiple_of(chip * SHARD_IN, 128), SHARD_IN)]
        return srcs[a].at[pl.ds(pl.multiple_of(chip * SHARD_SQ, 16), SHARD_SQ), :]

    pieces, wholes = [], []
    for j, (px, py) in enumerate(chips):
        for a in range(4):
            cps, whole = _row_chunks(shard(a, px, py), theirs[a].at[j], 4, send_sems.at[3 * a + j],
                                     recv_sems.at[3 * a + j], (px, py, c))
            pieces += cps
            wholes.append(whole)
    return pieces, wholes


def _pair_share(f_in, f_sb, f_cv, f_out):
    def body(i0, i1, i2, i3, o0, o1, o2, o3, send_sems, recv_sems):
        del i0, i1, i2, i3
        x, y, c, _ = _place()
        dsts = [o0, o1, o2, o3]
        sibling = (x, y, 1 - c)

        def half(a, which):
            start = pl.multiple_of(which * HALF, 128)
            if a == 0:
                return dsts[a].at[pl.ds(start, HALF), :]
            return dsts[a].at[:, pl.ds(start, HALF)]

        pieces, sent, arrived = [], [], []
        for a in range(4):
            cps, whole = _row_chunks(half(a, c), half(a, c), N_CHUNKS, send_sems.at[a], recv_sems.at[a], sibling)
            pieces += cps
            sent.append(whole)
            arrived.append(_arrival(half(a, 1 - c), send_sems.at[a], recv_sems.at[a], sibling))
        for cp in pieces:
            cp.start()
        for cp in arrived:
            cp.wait_recv()
        for cp in sent:
            cp.wait_send()

    return pl.pallas_call(
        body, in_specs=[ANY] * 4, out_specs=[ANY] * 4,
        out_shape=[jax.ShapeDtypeStruct((D_MODEL, SHARD_IN), F32)] + [jax.ShapeDtypeStruct((SHARD_SQ, D_MODEL), F32)] * 3,
        input_output_aliases={0: 0, 1: 1, 2: 2, 3: 3},
        scratch_shapes=[pltpu.SemaphoreType.DMA((4,)), pltpu.SemaphoreType.DMA((4,))],
        name="pair_share")(f_in, f_sb, f_cv, f_out)


def _small_allreduce(small):
    def body(s_ref, o_ref, slots, send_sems, recv_sems):
        x, y, c, _ = _place()
        me = 4 * x + 2 * y + c
        slots[me] = s_ref[...]
        sends = []
        for k in range(1, N_DEV):
            px, py, pc = (x + (k >> 2)) % 2, (y + ((k >> 1) & 1)) % 2, (c + (k & 1)) % 2
            sends.append(pltpu.make_async_remote_copy(
                src_ref=s_ref, dst_ref=slots.at[me], send_sem=send_sems.at[k - 1], recv_sem=recv_sems.at[k - 1],
                device_id=(px, py, pc), device_id_type=MESH))
        for cp in sends:
            cp.start()
        for k in range(1, N_DEV):
            px, py, pc = (x + (k >> 2)) % 2, (y + ((k >> 1) & 1)) % 2, (c + (k & 1)) % 2
            pltpu.make_async_remote_copy(
                src_ref=s_ref, dst_ref=slots.at[4 * px + 2 * py + pc], send_sem=send_sems.at[k - 1],
                recv_sem=recv_sems.at[k - 1], device_id=(px, py, pc), device_id_type=MESH).wait_recv()
        for cp in sends:
            cp.wait_send()
        total = slots[0]
        for d in range(1, N_DEV):
            total = total + slots[d]
        o_ref[...] = total
        o_ref[7:8, :] = jnp.zeros((1, D_MODEL), F32) + jnp.sum(total[7:8, :], axis=1, keepdims=True)

    return pl.pallas_call(
        body, in_specs=[pl.BlockSpec(memory_space=pltpu.VMEM)], out_specs=pl.BlockSpec(memory_space=pltpu.VMEM),
        out_shape=jax.ShapeDtypeStruct((SMALL_ROWS, D_MODEL), F32),
        scratch_shapes=[pltpu.VMEM((N_DEV, SMALL_ROWS, D_MODEL), F32), pltpu.SemaphoreType.DMA((N_DEV - 1,)),
                        pltpu.SemaphoreType.DMA((N_DEV - 1,))],
        name="small_allreduce")(small)


SUM_BLOCKS = 8


def _pair_sum(place, full, theirs, by_rows, name):
    rows, cols = theirs.shape
    tr = rows // SUM_BLOCKS

    def body(p_ref, a_ref, b_ref, o_ref):
        del p_ref
        o_ref[...] = (a_ref[...] + b_ref[...]).astype(BF16)

    mine = (lambda i, p: (p[1] * SUM_BLOCKS + i, 0)) if by_rows else (lambda i, p: (i, p[1]))
    spec = pl.BlockSpec((tr, cols), lambda i, p: (i, 0))
    return pl.pallas_call(
        body, out_shape=jax.ShapeDtypeStruct(theirs.shape, BF16),
        grid_spec=pltpu.PrefetchScalarGridSpec(num_scalar_prefetch=1, grid=(SUM_BLOCKS,),
                                               in_specs=[pl.BlockSpec((tr, cols), mine), spec], out_specs=spec),
        compiler_params=_params("parallel"), name=name)(place, full, theirs)


def _chip_sum(place, pair, got, by_rows, name):
    _, rows, cols = got.shape
    tr = rows // SUM_BLOCKS

    def body(p_ref, a_ref, g_ref, o_ref):
        del p_ref
        o_ref[...] = ((a_ref[...].astype(F32) + g_ref[0].astype(F32)) + g_ref[1].astype(F32)) + g_ref[2].astype(F32)

    if by_rows:
        own = lambda i, p: (i, p[0])
        out = lambda i, p: (p[1] * SUM_BLOCKS + i, 0)
        full = (2 * rows, cols)
    else:
        own = lambda i, p: (p[0] * SUM_BLOCKS + i, 0)
        out = lambda i, p: (i, p[1])
        full = (rows, 2 * cols)
    return pl.pallas_call(
        body, out_shape=jax.ShapeDtypeStruct(full, F32),
        grid_spec=pltpu.PrefetchScalarGridSpec(
            num_scalar_prefetch=1, grid=(SUM_BLOCKS,),
            in_specs=[pl.BlockSpec((tr, cols), own), pl.BlockSpec((3, tr, cols), lambda i, p: (0, i, 0))],
            out_specs=pl.BlockSpec((tr, cols), out)),
        compiler_params=_params("parallel"), name=name)(place, pair, got)


def _adamw(w, g, m, v, name):
    rows, cols = w.shape
    tr = rows // 4 if rows % 32 == 0 else rows
    c1 = 1.0 - ADAM_B1 ** ADAM_STEP
    c2 = 1.0 - ADAM_B2 ** ADAM_STEP

    def body(w_ref, g_ref, m_ref, v_ref, d_ref, mo_ref, vo_ref):
        g = g_ref[...]
        mn = ADAM_B1 * m_ref[...] + (1.0 - ADAM_B1) * g
        vn = ADAM_B2 * v_ref[...] + (1.0 - ADAM_B2) * (g * g)
        mo_ref[...] = mn
        vo_ref[...] = vn
        d_ref[...] = -ADAM_LR * ((mn / c1) / (jnp.sqrt(vn / c2) + ADAM_EPS) + ADAM_WD * w_ref[...])

    spec = pl.BlockSpec((tr, cols), lambda i: (i, 0))
    shape = jax.ShapeDtypeStruct(w.shape, F32)
    return pl.pallas_call(body, grid=(rows // tr,), in_specs=[spec] * 4, out_specs=[spec] * 3,
                          out_shape=[shape] * 3, compiler_params=_params("parallel"), name=name)(w, g, m, v)


GRAD_NAMES = ("in", "sb", "cv", "out")


def _place_scalars():
    return jnp.stack([2 * lax.axis_index("x") + lax.axis_index("y"), lax.axis_index("c")]).astype(jnp.int32)


def _pair_sums(grads):
    place = _place_scalars()
    theirs = _pair_exchange(*grads)
    return [_pair_sum(place, grads[a], theirs[a], a == 0, "pair_sum_" + GRAD_NAMES[a]) for a in range(4)]


def _finish_weight_grads(pair, got):
    place = _place_scalars()
    done = [_chip_sum(place, pair[a], got[a], a == 0, "chip_sum_" + GRAD_NAMES[a]) for a in range(4)]
    return _pair_share(*done)


def kernel(x, ln_in_g, ln_in_b, w_in, w_sb_proj, conv_w, conv_b, conv_ln_g, conv_ln_b, w_cv_proj, w_out, ln_post_g, ln_post_b, loss_target, m_ln_in_g, m_ln_in_b, m_w_in, m_w_sb_proj, m_conv_w, m_conv_b, m_conv_ln_g, m_conv_ln_b, m_w_cv_proj, m_w_out, m_ln_post_g, m_ln_post_b, v_ln_in_g, v_ln_in_b, v_w_in, v_w_sb_proj, v_conv_w, v_conv_b, v_conv_ln_g, v_conv_ln_b, v_w_cv_proj, v_w_out, v_ln_post_g, v_ln_post_b):
    nb, seq, _ = x.shape
    t = nb * seq
    vec_names = ("ln_in_g", "ln_in_b", "conv_b", "conv_ln_g", "conv_ln_b", "ln_post_g", "ln_post_b")
    vec_w = dict(zip(vec_names, (ln_in_g, ln_in_b, conv_b, conv_ln_g, conv_ln_b, ln_post_g, ln_post_b)))
    vec_m = dict(zip(vec_names, (m_ln_in_g, m_ln_in_b, m_conv_b, m_conv_ln_g, m_conv_ln_b, m_ln_post_g, m_ln_post_b)))
    vec_v = dict(zip(vec_names, (v_ln_in_g, v_ln_in_b, v_conv_b, v_conv_ln_g, v_conv_ln_b, v_ln_post_g, v_ln_post_b)))
    vecs = {k: a.reshape(1, D_MODEL) for k, a in vec_w.items()}

    pad_taps = lambda a: jnp.pad(a.reshape(CONV_K, SHARD_SQ), ((0, CONV_ROWS - CONV_K), (0, 0)))
    shards = (w_in[0].astype(BF16), w_sb_proj[0].astype(BF16), w_cv_proj[0].astype(BF16), w_out[0].astype(BF16),
              pad_taps(conv_w))
    x2 = x.reshape(t, D_MODEL)
    full_in, dhs, dr, big, small = _forward_backward(x2, loss_target.reshape(t, D_MODEL), nb, seq, shards, vecs)
    pair = _pair_sums(big)
    grad_x, st_in, got = _in_proj_bwd(dhs, full_in, dr, x2, vecs["ln_in_g"], pair)
    g_in, g_sb, g_cv, g_out = _finish_weight_grads(pair, got)
    small = _small_allreduce(jnp.concatenate([st_in[0:2], small], axis=0))

    chip = 2 * lax.axis_index("x") + lax.axis_index("y")
    g_conv = lax.dynamic_slice(small, (8, chip * SHARD_SQ), (CONV_ROWS, SHARD_SQ))
    loss = small[7, 0]

    grads, deltas, new_m, new_v = {}, {}, {}, {}
    stack = lambda d: jnp.concatenate([d[k].reshape(1, D_MODEL) for k in vec_names] + [jnp.zeros((1, D_MODEL), F32)])
    vd, vm, vv = _adamw(stack(vec_w), small[0:8], stack(vec_m), stack(vec_v), "adamw_vectors")
    for r, k in enumerate(vec_names):
        shape = vec_w[k].shape
        grads[k] = small[r].reshape(shape)
        deltas[k], new_m[k], new_v[k] = vd[r].reshape(shape), vm[r].reshape(shape), vv[r].reshape(shape)

    big_w = {"w_in": (w_in, g_in, m_w_in, v_w_in), "w_sb_proj": (w_sb_proj, g_sb, m_w_sb_proj, v_w_sb_proj),
             "w_cv_proj": (w_cv_proj, g_cv, m_w_cv_proj, v_w_cv_proj), "w_out": (w_out, g_out, m_w_out, v_w_out)}
    for k, (w, g, m, v) in big_w.items():
        d, mn, vn = _adamw(w[0], g, m[0], v[0], "adamw_" + k)
        grads[k], deltas[k], new_m[k], new_v[k] = g[None], d[None], mn[None], vn[None]
    d, mn, vn = _adamw(pad_taps(conv_w), g_conv, pad_taps(m_conv_w), pad_taps(v_conv_w), "adamw_conv_w")
    grads["conv_w"] = g_conv[None, :CONV_K]
    deltas["conv_w"], new_m["conv_w"], new_v["conv_w"] = d[None, :CONV_K], mn[None, :CONV_K], vn[None, :CONV_K]

    order = ("ln_in_g", "ln_in_b", "w_in", "w_sb_proj", "conv_w", "conv_b", "conv_ln_g", "conv_ln_b",
             "w_cv_proj", "w_out", "ln_post_g", "ln_post_b")
    return (loss, grad_x.reshape(x.shape), *[grads[k] for k in order], *[deltas[k] for k in order],
            *[new_m[k] for k in order], *[new_v[k] for k in order])
```

```python
import functools

import jax
import jax.numpy as jnp
from jax import lax
from jax.experimental import pallas as pl
from jax.experimental.pallas import tpu as pltpu

F32 = jnp.float32
BF16 = jnp.bfloat16

D_MODEL = 1024
N_HEADS = 16
HEAD_DIM = 64
HEAD_GROUP = 4
GROUP_W = HEAD_GROUP * HEAD_DIM
N_GROUPS = N_HEADS // HEAD_GROUP
N_PIECES = 9
IN_WIDTH = N_PIECES * D_MODEL
Q_BLOCK = 256
Q_TILE = 2 * Q_BLOCK
CONV_K = 31
CONV_ROWS = 32
HALO = 32
LN_EPS = 1e-5
ALPHA = 2.0 ** 0.25
Q_SCALE = 0.125
N_CHIPS = 4
N_DEV = 8
SHARD_IN = IN_WIDTH // N_CHIPS
SHARD_SQ = D_MODEL // N_CHIPS
HALF = D_MODEL // 2
SMALL_ROWS = 40
N_CHUNKS = 16

ADAM_LR = 0.001
ADAM_B1 = 0.9
ADAM_B2 = 0.999
ADAM_EPS = 1e-08
ADAM_WD = 0.01
ADAM_STEP = 10

MESH = pl.DeviceIdType.MESH
ANY = pl.BlockSpec(memory_space=pl.ANY)
VMEM_LIMIT = 60 * 1024 * 1024

NT = (((1,), (1,)), ((), ()))
TN = (((0,), (0,)), ((), ()))


def _sigmoid(x):
    return 1.0 / (1.0 + jnp.exp(-x))


def _dot(a, b):
    return jnp.dot(a, b, preferred_element_type=F32)


def _sum_along_keys(x, tri_twice):
    return _dot(jnp.concatenate(_split_bf16(x), axis=1), tri_twice)


def _dot_nt(a, b):
    return lax.dot_general(a, b, NT, preferred_element_type=F32)


def _dot_tn(a, b):
    return lax.dot_general(a, b, TN, preferred_element_type=F32)


def _split_bf16(x):
    hi = x.astype(BF16)
    lo = (x - hi.astype(F32)).astype(BF16)
    return hi, lo


def _ln_stats(x):
    mu = jnp.mean(x, axis=-1, keepdims=True)
    xc = x - mu
    var = jnp.mean(xc * xc, axis=-1, keepdims=True)
    rstd = lax.rsqrt(var + LN_EPS)
    return xc * rstd, rstd


def _ln_bwd(dy, xhat, rstd, g):
    dxh = dy * g
    m1 = jnp.mean(dxh, axis=-1, keepdims=True)
    m2 = jnp.mean(dxh * xhat, axis=-1, keepdims=True)
    return rstd * (dxh - m1 - xhat * m2)


def _params(*sem):
    return pltpu.CompilerParams(dimension_semantics=sem, vmem_limit_bytes=VMEM_LIMIT)


def _ln_in(x, g, b):
    t = x.shape[0]
    tm = min(512, t)

    def body(x_ref, g_ref, b_ref, hf_ref, hb_ref):
        xhat, _ = _ln_stats(x_ref[...])
        y = xhat * g_ref[...] + b_ref[...]
        hf_ref[...] = y
        hb_ref[...] = y.astype(BF16)

    row = pl.BlockSpec((tm, D_MODEL), lambda i: (i, 0))
    vec = pl.BlockSpec((1, D_MODEL), lambda i: (0, 0))
    return pl.pallas_call(
        body, grid=(t // tm,), in_specs=[row, vec, vec], out_specs=[row, row],
        out_shape=[jax.ShapeDtypeStruct((t, D_MODEL), F32), jax.ShapeDtypeStruct((t, D_MODEL), BF16)],
        compiler_params=_params("parallel"), name="ln_in")(x, g, b)


def _in_proj(h, w, col0, ncol, out_dtype, scale_first, name):
    t = h.shape[0]
    tm = min(1024, t)

    def body(h_ref, w_ref, o_ref):
        res = _dot(h_ref[...], w_ref[...])
        if scale_first:
            res = res * jnp.where(pl.program_id(0) == 0, Q_SCALE, 1.0)
        o_ref[...] = res.astype(out_dtype)

    return pl.pallas_call(
        body, grid=(ncol, t // tm),
        in_specs=[pl.BlockSpec((tm, D_MODEL), lambda j, i: (i, 0)),
                  pl.BlockSpec((D_MODEL, D_MODEL), lambda j, i: (0, j + col0))],
        out_specs=pl.BlockSpec((tm, D_MODEL), lambda j, i: (i, j)),
        out_shape=jax.ShapeDtypeStruct((t, ncol * D_MODEL), out_dtype),
        compiler_params=_params("parallel", "parallel"), name=name)(h, w)


def _sb_block(l, diag, tri, carry):
    lnb = jnp.minimum(-l, 0.0) - jnp.log(1.0 + jnp.exp(-jnp.abs(l)))
    if diag is not None:
        lnb = jnp.where(diag, lnb, 0.0)
    suffix = _sum_along_keys(lnb, tri)
    a = jnp.exp(l + suffix + carry)
    if diag is not None:
        a = jnp.where(diag, a, 0.0)
    return a, lnb, suffix[:, 0:1]


def _attn_consts():
    head = lax.broadcasted_iota(jnp.int32, (Q_TILE, GROUP_W), 1) // HEAD_DIM
    row = lax.broadcasted_iota(jnp.int32, (Q_TILE, Q_BLOCK), 0)
    col = lax.broadcasted_iota(jnp.int32, (Q_TILE, Q_BLOCK), 1)
    key = row % Q_BLOCK
    return head, col < row, (key >= col).astype(BF16), (key <= col).astype(BF16)


def _split_heads(t, head):
    zero = jnp.zeros_like(t)
    head = head[:t.shape[0]]
    return [jnp.where(head == h, t, zero) for h in range(HEAD_GROUP)]


def _key_block(ref, kb):
    return ref[pl.ds(pl.multiple_of(kb * Q_BLOCK, Q_BLOCK), Q_BLOCK), :]


def _attn_fwd(qkv, nb, seq, w_in_early, shards):
    nq = seq // Q_TILE

    def body(q_ref, k_ref, v_ref, full_in, s0, s1, s2, s3, s4, o_ref, d0, d1, d2, d3, d4, send_sems, recv_sems, local_sems):
        del full_in
        qi = pl.program_id(2)
        first = (pl.program_id(0) == 0) & (pl.program_id(1) == 0) & (qi == 0)
        last = (pl.program_id(0) == nb - 1) & (pl.program_id(1) == N_GROUPS - 1) & (qi == nq - 1)
        start_gather, finish_gather = _late_gather([s0, s1, s2, s3, s4], [d0, d1, d2, d3, d4],
                                                   send_sems, recv_sems, local_sems)
        pl.when(first)(start_gather)
        head, diag, tri, _ = _attn_consts()
        qh = _split_heads(q_ref[...], head)
        o_ref[...] = jnp.zeros_like(o_ref)
        zero_col = jnp.zeros((Q_BLOCK, 1), F32)

        def block(kb, carries, rows, masked):
            kblk = _key_block(k_ref, kb)
            vstack = jnp.concatenate(_split_heads(_key_block(v_ref, kb), head), axis=0)
            mask = diag[:Q_TILE - rows.start] if masked else None
            weights, out = [], []
            for h in range(HEAD_GROUP):
                a, _, blk_sum = _sb_block(_dot_nt(qh[h][rows], kblk), mask, tri, carries[h][rows])
                weights.append(a.astype(BF16))
                out.append(carries[h][rows] + blk_sum)
            o_ref[rows, :] += _dot(jnp.concatenate(weights, axis=1), vstack)
            return out

        second = block(2 * qi + 1, [jnp.zeros((Q_TILE, 1), F32)] * HEAD_GROUP, slice(Q_BLOCK, Q_TILE), True)
        carries = tuple(jnp.concatenate([zero_col, c], axis=0) for c in second)
        everything = slice(0, Q_TILE)
        carries = tuple(block(2 * qi, carries, everything, True))
        lax.fori_loop(1, 2 * qi + 1, lambda jj, cs: tuple(block(2 * qi - jj, cs, everything, False)), carries)
        pl.when(last)(finish_gather)

    n_shards = len(shards)
    full = [jax.ShapeDtypeStruct((D_MODEL, IN_WIDTH), BF16)] + [jax.ShapeDtypeStruct((D_MODEL, D_MODEL), BF16)] * 3 \
        + [jax.ShapeDtypeStruct((CONV_ROWS, D_MODEL), F32)]
    res = pl.pallas_call(
        body, grid=(nb, N_GROUPS, nq),
        in_specs=[pl.BlockSpec((Q_TILE, GROUP_W), lambda b, g, i: (b * nq + i, g)),
                  pl.BlockSpec((seq, GROUP_W), lambda b, g, i: (b, N_GROUPS + g)),
                  pl.BlockSpec((seq, GROUP_W), lambda b, g, i: (b, 2 * N_GROUPS + g))] + [ANY] * (1 + n_shards),
        out_specs=[pl.BlockSpec((Q_TILE, GROUP_W), lambda b, g, i: (b * nq + i, g))] + [ANY] * n_shards,
        out_shape=[jax.ShapeDtypeStruct((nb * seq, D_MODEL), F32)] + full,
        input_output_aliases={3: 1},
        scratch_shapes=[pltpu.SemaphoreType.DMA((3 * n_shards,)), pltpu.SemaphoreType.DMA((3 * n_shards,)),
                        pltpu.SemaphoreType.DMA((n_shards - 1,))],
        compiler_params=_params("arbitrary", "arbitrary", "arbitrary"), name="attn_fwd")(
            qkv, qkv, qkv, w_in_early, *shards)
    return res[0], res[1:]


def _attn_bwd(qkv, do, nb, seq):
    nq = seq // Q_TILE
    n_kb = seq // Q_BLOCK

    def body(q_ref, k_ref, v_ref, do_ref, dq_ref, dk_ref, dv_ref, g_s, beta_s, dq_acc, dk_acc, dv_acc):
        qi = pl.program_id(2)

        @pl.when(qi == 0)
        def _():
            dk_acc[...] = jnp.zeros_like(dk_acc)
            dv_acc[...] = jnp.zeros_like(dv_acc)

        head, diag, tri_suffix, tri_prefix = _attn_consts()
        qh = _split_heads(q_ref[...], head)
        doh = _split_heads(do_ref[...], head)
        dq_acc[...] = jnp.zeros_like(dq_acc)
        zero_col = jnp.zeros((Q_BLOCK, 1), F32)
        everything = slice(0, Q_TILE)
        second = slice(Q_BLOCK, Q_TILE)

        def block_a(kb, carries, rows, masked):
            ks = pl.multiple_of(kb * Q_BLOCK, Q_BLOCK)
            kblk = _key_block(k_ref, kb)
            vblk = _key_block(v_ref, kb)
            mask = diag[:Q_TILE - rows.start] if masked else None
            weights, out = [], []
            for h in range(HEAD_GROUP):
                l = _dot_nt(qh[h][rows], kblk)
                a, lnb, blk_sum = _sb_block(l, mask, tri_suffix, carries[h][rows])
                beta = jnp.exp(l + lnb)
                if masked:
                    beta = jnp.where(mask, beta, 0.0)
                g_s[h, kb, rows, :] = a * _dot_nt(doh[h][rows], vblk)
                beta_s[h, kb, rows, :] = beta.astype(BF16)
                weights.append(a.astype(BF16))
                out.append(carries[h][rows] + blk_sum)
            dv_acc[pl.ds(ks, Q_BLOCK), :] += _dot_tn(jnp.concatenate(weights, axis=0),
                                                     jnp.concatenate([t[rows] for t in doh], axis=0))
            return out

        part = block_a(2 * qi + 1, [jnp.zeros((Q_TILE, 1), F32)] * HEAD_GROUP, second, True)
        carries = tuple(jnp.concatenate([zero_col, c], axis=0) for c in part)
        carries = tuple(block_a(2 * qi, carries, everything, True))
        lax.fori_loop(1, 2 * qi + 1, lambda jj, cs: tuple(block_a(2 * qi - jj, cs, everything, False)), carries)

        def block_b(kb, prefixes, rows, masked):
            ks = pl.multiple_of(kb * Q_BLOCK, Q_BLOCK)
            kstack = jnp.concatenate(_split_heads(_key_block(k_ref, kb), head), axis=0)
            dls, out = [], []
            for h in range(HEAD_GROUP):
                g = g_s[h, kb, rows, :]
                beta = beta_s[h, kb, rows, :].astype(F32)
                incl = _sum_along_keys(g, tri_prefix)
                before = incl - g + prefixes[h][rows]
                dl = g - beta * (g + before)
                if masked:
                    dl = jnp.where(diag[:Q_TILE - rows.start], dl, 0.0)
                dls.append(dl.astype(BF16))
                out.append(prefixes[h][rows] + incl[:, Q_BLOCK - 1:Q_BLOCK])
            dq_acc[rows, :] += _dot(jnp.concatenate(dls, axis=1), kstack)
            dk_acc[pl.ds(ks, Q_BLOCK), :] += _dot_tn(jnp.concatenate(dls, axis=0),
                                                     jnp.concatenate([t[rows] for t in qh], axis=0))
            return out

        zeros = (jnp.zeros((Q_TILE, 1), F32),) * HEAD_GROUP
        prefixes = lax.fori_loop(0, 2 * qi, lambda kb, ps: tuple(block_b(kb, ps, everything, False)), zeros)
        prefixes = block_b(2 * qi, prefixes, everything, True)
        block_b(2 * qi + 1, prefixes, second, True)
        dq_ref[...] = (dq_acc[...] * Q_SCALE).astype(BF16)

        @pl.when(qi == nq - 1)
        def _():
            dk_ref[...] = dk_acc[...].astype(BF16)
            dv_ref[...] = dv_acc[...].astype(BF16)

    t = nb * seq
    qspec = pl.BlockSpec((Q_TILE, GROUP_W), lambda b, g, i: (b * nq + i, g))
    kvout = pl.BlockSpec((seq, GROUP_W), lambda b, g, i: (b, g))
    return pl.pallas_call(
        body, grid=(nb, N_GROUPS, nq),
        in_specs=[qspec,
                  pl.BlockSpec((seq, GROUP_W), lambda b, g, i: (b, N_GROUPS + g)),
                  pl.BlockSpec((seq, GROUP_W), lambda b, g, i: (b, 2 * N_GROUPS + g)),
                  qspec],
        out_specs=[qspec, kvout, kvout],
        out_shape=[jax.ShapeDtypeStruct((t, D_MODEL), BF16)] * 3,
        scratch_shapes=[pltpu.VMEM((HEAD_GROUP, n_kb, Q_TILE, Q_BLOCK), F32),
                        pltpu.VMEM((HEAD_GROUP, n_kb, Q_TILE, Q_BLOCK), BF16),
                        pltpu.VMEM((Q_TILE, GROUP_W), F32),
                        pltpu.VMEM((seq, GROUP_W), F32), pltpu.VMEM((seq, GROUP_W), F32)],
        compiler_params=_params("parallel", "parallel", "arbitrary"), name="attn_bwd")(qkv, qkv, qkv, do)


def _conv_block_rows(seq):
    return min(256, seq)


SUBLANES = 8
CONV_SUB = 32
CONV_GROUPS = CONV_SUB // SUBLANES


def _fill_shifts(sh_ref, n):
    for r in range(1, SUBLANES):
        sh_ref[r, 0:n, :] = sh_ref[0, pl.ds(r, n), :]


def _shifted(sh_ref, start, cs):
    r = start % SUBLANES
    return sh_ref[r, start - r:start - r + CONV_SUB, cs].reshape(CONV_GROUPS, SUBLANES, -1)


def _tap(w_ref, k, cs):
    row = w_ref[k:k + 1, cs]
    return jnp.broadcast_to(row, (SUBLANES, row.shape[1]))[None]


def _tree_sum(x):
    parts = [x[i] for i in range(x.shape[0])]
    while len(parts) > 1:
        parts = [parts[i] + parts[i + 1] for i in range(0, len(parts), 2)]
    return parts[0]


def _conv_fwd(hrest, conv_w, conv_b, ln_g, ln_b, nb, seq):
    ts = _conv_block_rows(seq)
    nblk = seq // ts
    lanes = 128

    def body(cv_ref, cg_ref, cvh_ref, cgh_ref, z_ref, w_ref, cb_ref, g_ref, b_ref, u1_ref, cvin_ref, ush):
        i = pl.program_id(1)
        halo = cvh_ref[...] * _sigmoid(cgh_ref[...])
        ush[0, 0:HALO, :] = jnp.where(i > 0, halo, 0.0)
        ush[0, HALO:HALO + ts, :] = cv_ref[...] * _sigmoid(cg_ref[...])
        _fill_shifts(ush, HALO + ts - SUBLANES)
        for cc in range(D_MODEL // lanes):
            cs = slice(cc * lanes, (cc + 1) * lanes)
            for r0 in range(0, ts, CONV_SUB):
                acc = jnp.zeros((CONV_GROUPS, SUBLANES, lanes), F32) + _tap(cb_ref, 0, cs)
                for k in range(CONV_K):
                    acc = acc + _tap(w_ref, k, cs) * _shifted(ush, HALO - CONV_K + 1 + k + r0, cs)
                u1_ref[r0:r0 + CONV_SUB, cs] = acc.reshape(CONV_SUB, lanes)
        xhat, _ = _ln_stats(u1_ref[...])
        u2 = xhat * g_ref[...] + b_ref[...]
        z = z_ref[...]
        cvin_ref[...] = (u2 * _sigmoid(u2) * z * _sigmoid(z)).astype(BF16)

    def main(colblk):
        return pl.BlockSpec((ts, D_MODEL), lambda b, i: (b * nblk + i, colblk))

    def halo(colblk):
        return pl.BlockSpec((HALO, D_MODEL),
                            lambda b, i: (jnp.maximum((b * seq + i * ts) // HALO - 1, 0), colblk))

    vec = pl.BlockSpec((1, D_MODEL), lambda b, i: (0, 0))
    t = nb * seq
    return pl.pallas_call(
        body, grid=(nb, nblk),
        in_specs=[main(1), main(2), halo(1), halo(2), main(3),
                  pl.BlockSpec((CONV_ROWS, D_MODEL), lambda b, i: (0, 0)), vec, vec, vec],
        out_specs=[main(0), main(0)],
        out_shape=[jax.ShapeDtypeStruct((t, D_MODEL), F32), jax.ShapeDtypeStruct((t, D_MODEL), BF16)],
        scratch_shapes=[pltpu.VMEM((SUBLANES, HALO + ts, D_MODEL), F32)],
        compiler_params=_params("parallel", "parallel"), name="conv_fwd")(
            hrest, hrest, hrest, hrest, hrest, conv_w, conv_b, ln_g, ln_b)


def _tail(o, hrest, cvin, h0, tgt, w_sb, w_cv, w_out, ln_g, ln_b):
    t = o.shape[0]
    tm = min(256, t)

    def body(o_ref, z_ref, gs_ref, gc_ref, cvin_ref, h0_ref, tgt_ref, wsb_ref, wcv_ref, wout_ref, g_ref, b_ref,
             dr_ref, drb_ref, a_ref, mg_ref, dysb_ref, dycv_ref, do_ref, dz_ref, dgs_ref, dgc_ref, dcvin_ref,
             st_ref):
        @pl.when(pl.program_id(0) == 0)
        def _():
            st_ref[...] = jnp.zeros_like(st_ref)

        o = o_ref[...]
        z = z_ref[...]
        sz = _sigmoid(z)
        a = (o * z * sz).astype(BF16)
        a_ref[...] = a
        y_sb = _dot(a, wsb_ref[...])
        y_cv = _dot(cvin_ref[...], wcv_ref[...])
        s_sb = _sigmoid(gs_ref[...])
        s_cv = _sigmoid(gc_ref[...])
        merged = (s_sb * y_sb + s_cv * y_cv).astype(BF16)
        mg_ref[...] = merged
        r = ALPHA * h0_ref[...] + _dot(merged, wout_ref[...])
        xhat, rstd = _ln_stats(r)
        g = g_ref[...]
        err = xhat * g + b_ref[...] - tgt_ref[...]
        dy = err * (1.0 / D_MODEL)
        st_ref[0:1, :] += jnp.sum(dy * xhat, axis=0, keepdims=True)
        st_ref[1:2, :] += jnp.sum(dy, axis=0, keepdims=True)
        st_ref[2:3, :] += (0.5 / D_MODEL) * jnp.sum(err * err, axis=0, keepdims=True)
        dr = _ln_bwd(dy, xhat, rstd, g)
        dr_ref[...] = dr
        drb = dr.astype(BF16)
        drb_ref[...] = drb
        dm = _dot_nt(drb, wout_ref[...])
        dy_sb = (dm * s_sb).astype(BF16)
        dy_cv = (dm * s_cv).astype(BF16)
        dysb_ref[...] = dy_sb
        dycv_ref[...] = dy_cv
        dgs_ref[...] = (dm * y_sb * s_sb * (1.0 - s_sb)).astype(BF16)
        dgc_ref[...] = (dm * y_cv * s_cv * (1.0 - s_cv)).astype(BF16)
        da = _dot_nt(dy_sb, wsb_ref[...])
        dcvin_ref[...] = _dot_nt(dy_cv, wcv_ref[...])
        do_ref[...] = (da * z * sz).astype(BF16)
        dz_ref[...] = (da * o * sz * (1.0 + z * (1.0 - sz))).astype(BF16)

    def tok(colblk=0):
        return pl.BlockSpec((tm, D_MODEL), lambda i: (i, colblk))

    wspec = pl.BlockSpec((D_MODEL, D_MODEL), lambda i: (0, 0), pipeline_mode=pl.Buffered(1))
    vec = pl.BlockSpec((1, D_MODEL), lambda i: (0, 0))
    bf = jax.ShapeDtypeStruct((t, D_MODEL), BF16)
    f32 = jax.ShapeDtypeStruct((t, D_MODEL), F32)
    return pl.pallas_call(
        body, grid=(t // tm,),
        in_specs=[tok(), tok(0), tok(4), tok(5), tok(), tok(), tok(), wspec, wspec, wspec, vec, vec],
        out_specs=[tok()] * 11 + [pl.BlockSpec((8, D_MODEL), lambda i: (0, 0))],
        out_shape=[f32] + [bf] * 9 + [f32, jax.ShapeDtypeStruct((8, D_MODEL), F32)],
        compiler_params=_params("arbitrary"), name="tail")(
            o, hrest, hrest, hrest, cvin, h0, tgt, w_sb, w_cv, w_out, ln_g, ln_b)


def _conv_branch_bwd(dcvin, u1, hrest, ln_g, ln_b):
    t = u1.shape[0]
    tm = min(512, t)

    def body(dc_ref, u1_ref, z_ref, g_ref, b_ref, du1_ref, dz_ref, st_ref):
        @pl.when(pl.program_id(0) == 0)
        def _():
            st_ref[...] = jnp.zeros_like(st_ref)

        xhat, rstd = _ln_stats(u1_ref[...])
        g = g_ref[...]
        u2 = xhat * g + b_ref[...]
        s2 = _sigmoid(u2)
        z = z_ref[...]
        sz = _sigmoid(z)
        dc = dc_ref[...]
        dz_ref[...] = (dc * (u2 * s2) * sz * (1.0 + z * (1.0 - sz))).astype(BF16)
        du2 = dc * (z * sz) * s2 * (1.0 + u2 * (1.0 - s2))
        st_ref[0:1, :] += jnp.sum(du2 * xhat, axis=0, keepdims=True)
        st_ref[1:2, :] += jnp.sum(du2, axis=0, keepdims=True)
        du1 = _ln_bwd(du2, xhat, rstd, g)
        du1_ref[...] = du1
        st_ref[2:3, :] += jnp.sum(du1, axis=0, keepdims=True)

    tok = pl.BlockSpec((tm, D_MODEL), lambda i: (i, 0))
    vec = pl.BlockSpec((1, D_MODEL), lambda i: (0, 0))
    return pl.pallas_call(
        body, grid=(t // tm,),
        in_specs=[tok, tok, pl.BlockSpec((tm, D_MODEL), lambda i: (i, 3)), vec, vec],
        out_specs=[tok, tok, pl.BlockSpec((8, D_MODEL), lambda i: (0, 0))],
        out_shape=[jax.ShapeDtypeStruct((t, D_MODEL), F32), jax.ShapeDtypeStruct((t, D_MODEL), BF16),
                   jax.ShapeDtypeStruct((8, D_MODEL), F32)],
        compiler_params=_params("arbitrary"), name="conv_branch_bwd")(dcvin, u1, hrest, ln_g, ln_b)


def _conv_bwd(du1, hrest, conv_w, nb, seq):
    ts = _conv_block_rows(seq)
    nblk = seq // ts
    lanes = 128
    last_halo = nb * seq // HALO - 1

    def body(du_ref, duh_ref, cv_ref, cg_ref, cvh_ref, cgh_ref, w_ref, dcv_ref, dcg_ref, dw_ref, ush, dsh, dw_part):
        b = pl.program_id(0)
        i = pl.program_id(1)

        @pl.when((b == 0) & (i == 0))
        def _():
            dw_part[...] = jnp.zeros_like(dw_part)

        cv = cv_ref[...]
        sg = _sigmoid(cg_ref[...])
        halo = cvh_ref[...] * _sigmoid(cgh_ref[...])
        ush[0, 0:HALO, :] = jnp.where(i > 0, halo, 0.0)
        ush[0, HALO:HALO + ts, :] = cv * sg
        dsh[0, 0:ts, :] = du_ref[...]
        dsh[0, ts:ts + HALO, :] = jnp.where(i < nblk - 1, duh_ref[...], 0.0)
        _fill_shifts(ush, HALO + ts - SUBLANES)
        _fill_shifts(dsh, HALO + ts - SUBLANES)
        for cc in range(D_MODEL // lanes):
            cs = slice(cc * lanes, (cc + 1) * lanes)
            for r0 in range(0, ts, CONV_SUB):
                rows = slice(r0, r0 + CONV_SUB)
                dmain = _shifted(dsh, r0, cs)
                acc = jnp.zeros((CONV_GROUPS, SUBLANES, lanes), F32)
                for k in range(CONV_K):
                    acc = acc + _tap(w_ref, k, cs) * _shifted(dsh, CONV_K - 1 - k + r0, cs)
                    dw_part[k, :, cs] += _tree_sum(dmain * _shifted(ush, HALO - CONV_K + 1 + k + r0, cs))
                acc = acc.reshape(CONV_SUB, lanes)
                dcv_ref[rows, cs] = (acc * sg[rows, cs]).astype(BF16)
                dcg_ref[rows, cs] = (acc * cv[rows, cs] * sg[rows, cs] * (1.0 - sg[rows, cs])).astype(BF16)

        @pl.when((b == nb - 1) & (i == nblk - 1))
        def _():
            for k in range(CONV_ROWS):
                dw_ref[k:k + 1, :] = jnp.sum(dw_part[k], axis=0, keepdims=True)

    def main(colblk):
        return pl.BlockSpec((ts, D_MODEL), lambda b, i: (b * nblk + i, colblk))

    def halo_before(colblk):
        return pl.BlockSpec((HALO, D_MODEL),
                            lambda b, i: (jnp.maximum((b * seq + i * ts) // HALO - 1, 0), colblk))

    halo_after = pl.BlockSpec((HALO, D_MODEL),
                              lambda b, i: (jnp.minimum((b * seq + (i + 1) * ts) // HALO, last_halo), 0))
    t = nb * seq
    return pl.pallas_call(
        body, grid=(nb, nblk),
        in_specs=[main(0), halo_after, main(1), main(2), halo_before(1), halo_before(2),
                  pl.BlockSpec((CONV_ROWS, D_MODEL), lambda b, i: (0, 0))],
        out_specs=[main(0), main(0), pl.BlockSpec((CONV_ROWS, D_MODEL), lambda b, i: (0, 0))],
        out_shape=[jax.ShapeDtypeStruct((t, D_MODEL), BF16), jax.ShapeDtypeStruct((t, D_MODEL), BF16),
                   jax.ShapeDtypeStruct((CONV_ROWS, D_MODEL), F32)],
        scratch_shapes=[pltpu.VMEM((SUBLANES, HALO + ts, D_MODEL), F32), pltpu.VMEM((SUBLANES, ts + HALO, D_MODEL), F32),
                        pltpu.VMEM((CONV_ROWS, SUBLANES, D_MODEL), F32)],
        compiler_params=_params("arbitrary", "arbitrary"), name="conv_bwd")(
            du1, du1, hrest, hrest, hrest, hrest, conv_w)


def _weight_grad(x, dys, name):
    n = len(dys)
    t = x.shape[0]
    tm = min(512, t)

    def body(x_ref, *refs):
        o_ref = refs[n]
        j = pl.program_id(0)

        @pl.when(pl.program_id(1) == 0)
        def _():
            o_ref[...] = jnp.zeros_like(o_ref)

        for p in range(n):
            @pl.when(j == p)
            def _(p=p):
                o_ref[...] += _dot_tn(x_ref[...], refs[p][...])

    def dy_spec(p):
        return pl.BlockSpec((tm, D_MODEL), lambda j, i: (jnp.where(j == p, i, 0), 0))

    return pl.pallas_call(
        body, grid=(n, t // tm),
        in_specs=[pl.BlockSpec((tm, D_MODEL), lambda j, i: (i, 0))] + [dy_spec(p) for p in range(n)],
        out_specs=pl.BlockSpec((D_MODEL, D_MODEL), lambda j, i: (0, j)),
        out_shape=jax.ShapeDtypeStruct((D_MODEL, n * D_MODEL), F32),
        compiler_params=_params("parallel", "arbitrary"), name=name)(x, *dys)


def _in_proj_bwd(dhs, w_in, dr, x, ln_g, pair):
    n = len(dhs)
    t = x.shape[0]
    tm = min(512, t)
    n_i = t // tm

    def body(*refs):
        dh_refs = refs[:n]
        w_ref, dr_ref, x_ref, g_ref = refs[n:n + 4]
        pair_refs = refs[n + 4:n + 8]
        gx_ref, st_ref = refs[n + 8:n + 10]
        got_refs = refs[n + 10:n + 14]
        acc, send_sems, recv_sems = refs[n + 14:]
        i = pl.program_id(0)
        p = pl.program_id(1)
        pieces, wholes = _chip_exchange_copies(pair_refs, got_refs, send_sems, recv_sems)

        @pl.when((i == 0) & (p == 0))
        def _():
            st_ref[...] = jnp.zeros_like(st_ref)
            for cp in pieces:
                cp.start()

        @pl.when((i == n_i - 1) & (p == n - 1))
        def _():
            for cp in wholes:
                cp.wait_recv()
            for cp in wholes:
                cp.wait_send()

        @pl.when(p == 0)
        def _():
            acc[...] = ALPHA * dr_ref[...]

        for q in range(n):
            @pl.when(p == q)
            def _(q=q):
                acc[...] += _dot_nt(dh_refs[q][...], w_ref[...])

        @pl.when(p == n - 1)
        def _():
            xhat, rstd = _ln_stats(x_ref[...])
            dh0 = acc[...]
            st_ref[0:1, :] += jnp.sum(dh0 * xhat, axis=0, keepdims=True)
            st_ref[1:2, :] += jnp.sum(dh0, axis=0, keepdims=True)
            gx_ref[...] = _ln_bwd(dh0, xhat, rstd, g_ref[...])

    tok = pl.BlockSpec((tm, D_MODEL), lambda i, p: (i, 0))
    got = [jax.ShapeDtypeStruct((3, HALF, SHARD_IN), BF16)] + [jax.ShapeDtypeStruct((3, SHARD_SQ, HALF), BF16)] * 3
    res = pl.pallas_call(
        body, grid=(n_i, n),
        in_specs=[tok] * n + [pl.BlockSpec((D_MODEL, D_MODEL), lambda i, p: (0, p)), tok, tok,
                              pl.BlockSpec((1, D_MODEL), lambda i, p: (0, 0))] + [ANY] * 4,
        out_specs=[tok, pl.BlockSpec((8, D_MODEL), lambda i, p: (0, 0))] + [ANY] * 4,
        out_shape=[jax.ShapeDtypeStruct((t, D_MODEL), F32), jax.ShapeDtypeStruct((8, D_MODEL), F32)] + got,
        scratch_shapes=[pltpu.VMEM((tm, D_MODEL), F32), pltpu.SemaphoreType.DMA((12,)), pltpu.SemaphoreType.DMA((12,))],
        compiler_params=_params("arbitrary", "arbitrary"), name="in_proj_bwd")(*dhs, w_in, dr, x, ln_g, *pair)
    return res[0], res[1], res[2:]


def _forward_backward(x, tgt, nb, seq, shards, vecs):
    w_in_early = _gather_first(shards[0])
    h0, h0b = _ln_in(x, vecs["ln_in_g"], vecs["ln_in_b"])
    qkv = _in_proj(h0b, w_in_early, 0, 3, BF16, True, "in_proj_qkv")
    o, (w_in, w_sb, w_cv, w_out, conv_w) = _attn_fwd(qkv, nb, seq, w_in_early, shards)
    hrest = _in_proj(h0b, w_in, 3, 6, F32, False, "in_proj_rest")
    u1, cvin = _conv_fwd(hrest, conv_w, vecs["conv_b"], vecs["conv_ln_g"], vecs["conv_ln_b"], nb, seq)
    (dr, drb, a, merged, dy_sb, dy_cv, do, dz_sb, dg_sb, dg_cv, dcvin, st_tail) = _tail(
        o, hrest, cvin, h0, tgt, w_sb, w_cv, w_out, vecs["ln_post_g"], vecs["ln_post_b"])
    d_w_sb = _weight_grad(a, [dy_sb], "grad_w_sb")
    d_w_cv = _weight_grad(cvin, [dy_cv], "grad_w_cv")
    d_w_out = _weight_grad(merged, [drb], "grad_w_out")
    du1, dz_cv, st_conv = _conv_branch_bwd(dcvin, u1, hrest, vecs["conv_ln_g"], vecs["conv_ln_b"])
    dc_val, dc_gate, d_conv_w = _conv_bwd(du1, hrest, conv_w, nb, seq)
    dq, dk, dv = _attn_bwd(qkv, do, nb, seq)
    dhs = [dq, dk, dv, dz_sb, dc_val, dc_gate, dz_cv, dg_sb, dg_cv]
    d_w_in = _weight_grad(h0b, dhs, "grad_w_in")
    small = jnp.concatenate([st_conv[2:3], st_conv[0:2], st_tail[0:3], d_conv_w], axis=0)
    return w_in, dhs, dr, (d_w_in, d_w_sb, d_w_cv, d_w_out), small


def _place():
    x, y, c = lax.axis_index("x"), lax.axis_index("y"), lax.axis_index("c")
    chips = [(1 - x, y), (x, 1 - y), (1 - x, 1 - y)]
    return x, y, c, chips


N_EARLY = -(-3 * D_MODEL // SHARD_IN)
GATHER_PARTS = 16


def _w_in_slot(full, chip):
    return full.at[:, pl.ds(pl.multiple_of(chip * SHARD_IN, 128), SHARD_IN)]


def _gather_first(w_in):
    part_rows = HALF // GATHER_PARTS
    n_sem = 3 * GATHER_PARTS
    across_x, across_y, diagonal = 0, 1, 2

    def body(src, dst, ici_send, ici_recv, fwd_send, fwd_recv, local_sem):
        x, y, c, chips = _place()
        me = 2 * x + y
        sibling = (x, y, 1 - c)
        owner = [2 * px + py for px, py in chips]

        def part(ref, k, half):
            return ref.at[pl.ds(pl.multiple_of(half * HALF + k * part_rows, 16), part_rows), :]

        def held(j, k, half):
            return part(_w_in_slot(dst, owner[j]), k, half)

        def sent(j, k):
            px, py = chips[j]
            return _remote(part(src, k, c), part(_w_in_slot(dst, me), k, c), ici_send.at[j * GATHER_PARTS + k],
                           ici_recv.at[j * GATHER_PARTS + k], (px, py, c))

        def relayed(k):
            px, py = chips[across_y]
            here = held(across_x, k, c)
            return _remote(here, here, ici_send.at[diagonal * GATHER_PARTS + k],
                           ici_recv.at[diagonal * GATHER_PARTS + k], (px, py, c))

        def landed(j, k):
            px, py = chips[across_y if j == diagonal else j]
            return _remote(held(j, k, c), held(j, k, c), ici_send.at[j * GATHER_PARTS + k],
                           ici_recv.at[j * GATHER_PARTS + k], (px, py, c))

        def passed(j, k, half):
            return _remote(held(j, k, half), held(j, k, half), fwd_send.at[j * GATHER_PARTS + k],
                           fwd_recv.at[j * GATHER_PARTS + k], sibling)

        own = _w_in_slot(dst, me)
        rows = D_MODEL // N_CHUNKS
        for k in range(N_CHUNKS):
            pltpu.make_async_copy(src.at[pl.ds(k * rows, rows)], own.at[pl.ds(k * rows, rows)], local_sem).start()
        local = pltpu.make_async_copy(src, own, local_sem)

        @pl.when(me < N_EARLY)
        def _():
            for j in (across_x, across_y):
                for k in range(GATHER_PARTS):
                    sent(j, k).start()

        for j in (across_x, across_y, diagonal):
            @pl.when(owner[j] < N_EARLY)
            def _(j=j):
                for k in range(GATHER_PARTS):
                    landed(j, k).wait_recv()
                    passed(j, k, c).start()
                    if j == across_x:
                        relayed(k).start()

        for j in (across_x, across_y, diagonal):
            @pl.when(owner[j] < N_EARLY)
            def _(j=j):
                for k in range(GATHER_PARTS):
                    passed(j, k, 1 - c).wait_recv()
                for k in range(GATHER_PARTS):
                    passed(j, k, c).wait_send()
                    if j == across_x:
                        relayed(k).wait_send()

        @pl.when(me < N_EARLY)
        def _():
            for j in (across_x, across_y):
                for k in range(GATHER_PARTS):
                    sent(j, k).wait_send()

        local.wait()

    return pl.pallas_call(
        body, in_specs=[ANY], out_specs=ANY, out_shape=jax.ShapeDtypeStruct((D_MODEL, IN_WIDTH), BF16),
        scratch_shapes=[pltpu.SemaphoreType.DMA((n_sem,)), pltpu.SemaphoreType.DMA((n_sem,)),
                        pltpu.SemaphoreType.DMA((n_sem,)), pltpu.SemaphoreType.DMA((n_sem,)),
                        pltpu.SemaphoreType.DMA],
        name="gather_first")(w_in)


def _late_gather(srcs, dsts, send_sems, recv_sems, local_sems):
    x, y, c, chips = _place()
    me = 2 * x + y
    small = range(1, 5)

    def slot(a, chip):
        if a == 0:
            return _w_in_slot(dsts[0], chip)
        if a == 4:
            return dsts[a].at[:, pl.ds(pl.multiple_of(chip * SHARD_SQ, 128), SHARD_SQ)]
        return dsts[a].at[pl.ds(pl.multiple_of(chip * SHARD_SQ, 16), SHARD_SQ), :]

    def sems(a, j):
        return send_sems.at[3 * a + j], recv_sems.at[3 * a + j]

    def whole(a, j):
        px, py = chips[j]
        return _remote(srcs[a], slot(a, me), *sems(a, j), (px, py, c))

    def arrival(a, j):
        px, py = chips[j]
        return _arrival(slot(a, 2 * px + py), *sems(a, j), (px, py, c))

    def local(a):
        return pltpu.make_async_copy(srcs[a], slot(a, me), local_sems.at[a - 1])

    def start():
        for a in small:
            local(a).start()
            for j in range(3):
                whole(a, j).start()

        @pl.when(me >= N_EARLY)
        def _():
            for j, (px, py) in enumerate(chips):
                for cp in _chunk_copies(srcs[0], slot(0, me), GATHER_PARTS, *sems(0, j), (px, py, c)):
                    cp.start()

    def finish():
        for j, (px, py) in enumerate(chips):
            for a in small:
                arrival(a, j).wait_recv()

            @pl.when(2 * px + py >= N_EARLY)
            def _(j=j):
                arrival(0, j).wait_recv()

        for a in small:
            for j in range(3):
                whole(a, j).wait_send()
            local(a).wait()

        @pl.when(me >= N_EARLY)
        def _():
            for j in range(3):
                whole(0, j).wait_send()

    return start, finish


def _remote(src, dst, send_sem, recv_sem, device):
    return pltpu.make_async_remote_copy(src_ref=src, dst_ref=dst, send_sem=send_sem, recv_sem=recv_sem,
                                        device_id=device, device_id_type=MESH)


def _chunk_copies(src, dst, n, send_sem, recv_sem, device):
    rows = src.shape[0] // n
    return [_remote(src.at[pl.ds(k * rows, rows)], dst.at[pl.ds(k * rows, rows)], send_sem, recv_sem, device)
            for k in range(n)]


def _row_chunks(src, dst, n, send_sem, recv_sem, device):
    return _chunk_copies(src, dst, n, send_sem, recv_sem, device), _remote(src, dst, send_sem, recv_sem, device)


def _arrival(dst, send_sem, recv_sem, device):
    return _remote(dst, dst, send_sem, recv_sem, device)


def _pair_exchange(d_in, d_sb, d_cv, d_out):
    def body(i0, i1, i2, i3, r0, r1, r2, r3, send_sems, recv_sems):
        x, y, c, _ = _place()
        srcs = [i0, i1, i2, i3]
        theirs = [r0, r1, r2, r3]
        sibling = (x, y, 1 - c)
        start = pl.multiple_of((1 - c) * HALF, 128)
        views = [srcs[0].at[pl.ds(start, HALF), :]] + [srcs[a].at[:, pl.ds(start, HALF)] for a in range(1, 4)]
        pieces, wholes = [], []
        for a in range(4):
            cps, whole = _row_chunks(views[a], theirs[a], N_CHUNKS, send_sems.at[a], recv_sems.at[a], sibling)
            pieces += cps
            wholes.append(whole)
        for cp in pieces:
            cp.start()
        for cp in wholes:
            cp.wait_recv()
        for cp in wholes:
            cp.wait_send()

    shapes = [jax.ShapeDtypeStruct((HALF, IN_WIDTH), F32)] + [jax.ShapeDtypeStruct((D_MODEL, HALF), F32)] * 3
    return pl.pallas_call(
        body, in_specs=[ANY] * 4, out_specs=[ANY] * 4, out_shape=shapes,
        scratch_shapes=[pltpu.SemaphoreType.DMA((4,)), pltpu.SemaphoreType.DMA((4,))],
        name="pair_exchange")(d_in, d_sb, d_cv, d_out)


def _chip_exchange_copies(srcs, theirs, send_sems, recv_sems):
    _, _, c, chips = _place()

    def shard(a, px, py):
        chip = 2 * px + py
        if a == 0:
            return srcs[a].at[:, pl.ds(pl.multiple_of(chip * SHARD_IN, 128), SHARD_IN)]
        return srcs[a].at[pl.ds(pl.multiple_of(chip * SHARD_SQ, 16), SHARD_SQ), :]

    pieces, wholes = [], []
    for j, (px, py) in enumerate(chips):
        for a in range(4):
            cps, whole = _row_chunks(shard(a, px, py), theirs[a].at[j], 4, send_sems.at[3 * a + j],
                                     recv_sems.at[3 * a + j], (px, py, c))
            pieces += cps
            wholes.append(whole)
    return pieces, wholes


def _pair_share(f_in, f_sb, f_cv, f_out):
    def body(i0, i1, i2, i3, o0, o1, o2, o3, send_sems, recv_sems):
        del i0, i1, i2, i3
        x, y, c, _ = _place()
        dsts = [o0, o1, o2, o3]
        sibling = (x, y, 1 - c)

        def half(a, which):
            start = pl.multiple_of(which * HALF, 128)
            if a == 0:
                return dsts[a].at[pl.ds(start, HALF), :]
            return dsts[a].at[:, pl.ds(start, HALF)]

        pieces, sent, arrived = [], [], []
        for a in range(4):
            cps, whole = _row_chunks(half(a, c), half(a, c), N_CHUNKS, send_sems.at[a], recv_sems.at[a], sibling)
            pieces += cps
            sent.append(whole)
            arrived.append(_arrival(half(a, 1 - c), send_sems.at[a], recv_sems.at[a], sibling))
        for cp in pieces:
            cp.start()
        for cp in arrived:
            cp.wait_recv()
        for cp in sent:
            cp.wait_send()

    return pl.pallas_call(
        body, in_specs=[ANY] * 4, out_specs=[ANY] * 4,
        out_shape=[jax.ShapeDtypeStruct((D_MODEL, SHARD_IN), F32)] + [jax.ShapeDtypeStruct((SHARD_SQ, D_MODEL), F32)] * 3,
        input_output_aliases={0: 0, 1: 1, 2: 2, 3: 3},
        scratch_shapes=[pltpu.SemaphoreType.DMA((4,)), pltpu.SemaphoreType.DMA((4,))],
        name="pair_share")(f_in, f_sb, f_cv, f_out)


def _small_allreduce(small):
    def body(s_ref, o_ref, slots, send_sems, recv_sems):
        x, y, c, _ = _place()
        me = 4 * x + 2 * y + c
        slots[me] = s_ref[...]
        sends = []
        for k in range(1, N_DEV):
            px, py, pc = (x + (k >> 2)) % 2, (y + ((k >> 1) & 1)) % 2, (c + (k & 1)) % 2
            sends.append(pltpu.make_async_remote_copy(
                src_ref=s_ref, dst_ref=slots.at[me], send_sem=send_sems.at[k - 1], recv_sem=recv_sems.at[k - 1],
                device_id=(px, py, pc), device_id_type=MESH))
        for cp in sends:
            cp.start()
        for k in range(1, N_DEV):
            px, py, pc = (x + (k >> 2)) % 2, (y + ((k >> 1) & 1)) % 2, (c + (k & 1)) % 2
            pltpu.make_async_remote_copy(
                src_ref=s_ref, dst_ref=slots.at[4 * px + 2 * py + pc], send_sem=send_sems.at[k - 1],
                recv_sem=recv_sems.at[k - 1], device_id=(px, py, pc), device_id_type=MESH).wait_recv()
        for cp in sends:
            cp.wait_send()
        total = slots[0]
        for d in range(1, N_DEV):
            total = total + slots[d]
        o_ref[...] = total
        o_ref[7:8, :] = jnp.zeros((1, D_MODEL), F32) + jnp.sum(total[7:8, :], axis=1, keepdims=True)

    return pl.pallas_call(
        body, in_specs=[pl.BlockSpec(memory_space=pltpu.VMEM)], out_specs=pl.BlockSpec(memory_space=pltpu.VMEM),
        out_shape=jax.ShapeDtypeStruct((SMALL_ROWS, D_MODEL), F32),
        scratch_shapes=[pltpu.VMEM((N_DEV, SMALL_ROWS, D_MODEL), F32), pltpu.SemaphoreType.DMA((N_DEV - 1,)),
                        pltpu.SemaphoreType.DMA((N_DEV - 1,))],
        name="small_allreduce")(small)


SUM_BLOCKS = 8


def _pair_sum(place, full, theirs, by_rows, name):
    rows, cols = theirs.shape
    tr = rows // SUM_BLOCKS

    def body(p_ref, a_ref, b_ref, o_ref):
        del p_ref
        o_ref[...] = (a_ref[...] + b_ref[...]).astype(BF16)

    mine = (lambda i, p: (p[1] * SUM_BLOCKS + i, 0)) if by_rows else (lambda i, p: (i, p[1]))
    spec = pl.BlockSpec((tr, cols), lambda i, p: (i, 0))
    return pl.pallas_call(
        body, out_shape=jax.ShapeDtypeStruct(theirs.shape, BF16),
        grid_spec=pltpu.PrefetchScalarGridSpec(num_scalar_prefetch=1, grid=(SUM_BLOCKS,),
                                               in_specs=[pl.BlockSpec((tr, cols), mine), spec], out_specs=spec),
        compiler_params=_params("parallel"), name=name)(place, full, theirs)


def _chip_sum(place, pair, got, by_rows, name):
    _, rows, cols = got.shape
    tr = rows // SUM_BLOCKS

    def body(p_ref, a_ref, g_ref, o_ref):
        del p_ref
        o_ref[...] = ((a_ref[...].astype(F32) + g_ref[0].astype(F32)) + g_ref[1].astype(F32)) + g_ref[2].astype(F32)

    if by_rows:
        own = lambda i, p: (i, p[0])
        out = lambda i, p: (p[1] * SUM_BLOCKS + i, 0)
        full = (2 * rows, cols)
    else:
        own = lambda i, p: (p[0] * SUM_BLOCKS + i, 0)
        out = lambda i, p: (i, p[1])
        full = (rows, 2 * cols)
    return pl.pallas_call(
        body, out_shape=jax.ShapeDtypeStruct(full, F32),
        grid_spec=pltpu.PrefetchScalarGridSpec(
            num_scalar_prefetch=1, grid=(SUM_BLOCKS,),
            in_specs=[pl.BlockSpec((tr, cols), own), pl.BlockSpec((3, tr, cols), lambda i, p: (0, i, 0))],
            out_specs=pl.BlockSpec((tr, cols), out)),
        compiler_params=_params("parallel"), name=name)(place, pair, got)


def _adamw(w, g, m, v, name):
    rows, cols = w.shape
    tr = rows // 4 if rows % 32 == 0 else rows
    c1 = 1.0 - ADAM_B1 ** ADAM_STEP
    c2 = 1.0 - ADAM_B2 ** ADAM_STEP

    def body(w_ref, g_ref, m_ref, v_ref, d_ref, mo_ref, vo_ref):
        g = g_ref[...]
        mn = ADAM_B1 * m_ref[...] + (1.0 - ADAM_B1) * g
        vn = ADAM_B2 * v_ref[...] + (1.0 - ADAM_B2) * (g * g)
        mo_ref[...] = mn
        vo_ref[...] = vn
        d_ref[...] = -ADAM_LR * ((mn / c1) / (jnp.sqrt(vn / c2) + ADAM_EPS) + ADAM_WD * w_ref[...])

    spec = pl.BlockSpec((tr, cols), lambda i: (i, 0))
    shape = jax.ShapeDtypeStruct(w.shape, F32)
    return pl.pallas_call(body, grid=(rows // tr,), in_specs=[spec] * 4, out_specs=[spec] * 3,
                          out_shape=[shape] * 3, compiler_params=_params("parallel"), name=name)(w, g, m, v)


GRAD_NAMES = ("in", "sb", "cv", "out")


def _place_scalars():
    return jnp.stack([2 * lax.axis_index("x") + lax.axis_index("y"), lax.axis_index("c")]).astype(jnp.int32)


def _pair_sums(grads):
    place = _place_scalars()
    theirs = _pair_exchange(*grads)
    return [_pair_sum(place, grads[a], theirs[a], a == 0, "pair_sum_" + GRAD_NAMES[a]) for a in range(4)]


def _finish_weight_grads(pair, got):
    place = _place_scalars()
    done = [_chip_sum(place, pair[a], got[a], a == 0, "chip_sum_" + GRAD_NAMES[a]) for a in range(4)]
    return _pair_share(*done)


def kernel(x, ln_in_g, ln_in_b, w_in, w_sb_proj, conv_w, conv_b, conv_ln_g, conv_ln_b, w_cv_proj, w_out, ln_post_g, ln_post_b, loss_target, m_ln_in_g, m_ln_in_b, m_w_in, m_w_sb_proj, m_conv_w, m_conv_b, m_conv_ln_g, m_conv_ln_b, m_w_cv_proj, m_w_out, m_ln_post_g, m_ln_post_b, v_ln_in_g, v_ln_in_b, v_w_in, v_w_sb_proj, v_conv_w, v_conv_b, v_conv_ln_g, v_conv_ln_b, v_w_cv_proj, v_w_out, v_ln_post_g, v_ln_post_b):
    nb, seq, _ = x.shape
    t = nb * seq
    vec_names = ("ln_in_g", "ln_in_b", "conv_b", "conv_ln_g", "conv_ln_b", "ln_post_g", "ln_post_b")
    vec_w = dict(zip(vec_names, (ln_in_g, ln_in_b, conv_b, conv_ln_g, conv_ln_b, ln_post_g, ln_post_b)))
    vec_m = dict(zip(vec_names, (m_ln_in_g, m_ln_in_b, m_conv_b, m_conv_ln_g, m_conv_ln_b, m_ln_post_g, m_ln_post_b)))
    vec_v = dict(zip(vec_names, (v_ln_in_g, v_ln_in_b, v_conv_b, v_conv_ln_g, v_conv_ln_b, v_ln_post_g, v_ln_post_b)))
    vecs = {k: a.reshape(1, D_MODEL) for k, a in vec_w.items()}

    pad_taps = lambda a: jnp.pad(a.reshape(CONV_K, SHARD_SQ), ((0, CONV_ROWS - CONV_K), (0, 0)))
    shards = (w_in[0].astype(BF16), w_sb_proj[0].astype(BF16), w_cv_proj[0].astype(BF16), w_out[0].astype(BF16),
              pad_taps(conv_w))
    x2 = x.reshape(t, D_MODEL)
    full_in, dhs, dr, big, small = _forward_backward(x2, loss_target.reshape(t, D_MODEL), nb, seq, shards, vecs)
    pair = _pair_sums(big)
    grad_x, st_in, got = _in_proj_bwd(dhs, full_in, dr, x2, vecs["ln_in_g"], pair)
    g_in, g_sb, g_cv, g_out = _finish_weight_grads(pair, got)
    small = _small_allreduce(jnp.concatenate([st_in[0:2], small], axis=0))

    chip = 2 * lax.axis_index("x") + lax.axis_index("y")
    g_conv = lax.dynamic_slice(small, (8, chip * SHARD_SQ), (CONV_ROWS, SHARD_SQ))
    loss = small[7, 0]

    grads, deltas, new_m, new_v = {}, {}, {}, {}
    stack = lambda d: jnp.concatenate([d[k].reshape(1, D_MODEL) for k in vec_names] + [jnp.zeros((1, D_MODEL), F32)])
    vd, vm, vv = _adamw(stack(vec_w), small[0:8], stack(vec_m), stack(vec_v), "adamw_vectors")
    for r, k in enumerate(vec_names):
        shape = vec_w[k].shape
        grads[k] = small[r].reshape(shape)
        deltas[k], new_m[k], new_v[k] = vd[r].reshape(shape), vm[r].reshape(shape), vv[r].reshape(shape)

    big_w = {"w_in": (w_in, g_in, m_w_in, v_w_in), "w_sb_proj": (w_sb_proj, g_sb, m_w_sb_proj, v_w_sb_proj),
             "w_cv_proj": (w_cv_proj, g_cv, m_w_cv_proj, v_w_cv_proj), "w_out": (w_out, g_out, m_w_out, v_w_out)}
    for k, (w, g, m, v) in big_w.items():
        d, mn, vn = _adamw(w[0], g, m[0], v[0], "adamw_" + k)
        grads[k], deltas[k], new_m[k], new_v[k] = g[None], d[None], mn[None], vn[None]
    d, mn, vn = _adamw(pad_taps(conv_w), g_conv, pad_taps(m_conv_w), pad_taps(v_conv_w), "adamw_conv_w")
    grads["conv_w"] = g_conv[None, :CONV_K]
    deltas["conv_w"], new_m["conv_w"], new_v["conv_w"] = d[None, :CONV_K], mn[None, :CONV_K], vn[None, :CONV_K]

    order = ("ln_in_g", "ln_in_b", "w_in", "w_sb_proj", "conv_w", "conv_b", "conv_ln_g", "conv_ln_b",
             "w_cv_proj", "w_out", "ln_post_g", "ln_post_b")
    return (loss, grad_x.reshape(x.shape), *[grads[k] for k in order], *[deltas[k] for k in order],
            *[new_m[k] for k in order], *[new_v[k] for k in order])
```

```python
import functools

import jax
import jax.numpy as jnp
from jax import lax
from jax.experimental import pallas as pl
from jax.experimental.pallas import tpu as pltpu

F32 = jnp.float32
BF16 = jnp.bfloat16

D_MODEL = 1024
N_HEADS = 16
HEAD_DIM = 64
HEAD_GROUP = 4
GROUP_W = HEAD_GROUP * HEAD_DIM
N_GROUPS = N_HEADS // HEAD_GROUP
N_PIECES = 9
IN_WIDTH = N_PIECES * D_MODEL
Q_BLOCK = 256
Q_TILE = 2 * Q_BLOCK
CONV_K = 31
CONV_ROWS = 32
HALO = 32
LN_EPS = 1e-5
ALPHA = 2.0 ** 0.25
Q_SCALE = 0.125
N_CHIPS = 4
N_DEV = 8
SHARD_IN = IN_WIDTH // N_CHIPS
SHARD_SQ = D_MODEL // N_CHIPS
HALF = D_MODEL // 2
SMALL_ROWS = 40
N_CHUNKS = 16

ADAM_LR = 0.001
ADAM_B1 = 0.9
ADAM_B2 = 0.999
ADAM_EPS = 1e-08
ADAM_WD = 0.01
ADAM_STEP = 10

MESH = pl.DeviceIdType.MESH
ANY = pl.BlockSpec(memory_space=pl.ANY)
VMEM_LIMIT = 60 * 1024 * 1024

NT = (((1,), (1,)), ((), ()))
TN = (((0,), (0,)), ((), ()))


def _sigmoid(x):
    return 1.0 / (1.0 + jnp.exp(-x))


def _dot(a, b):
    return jnp.dot(a, b, preferred_element_type=F32)


def _sum_along_keys(x, tri_twice):
    return _dot(jnp.concatenate(_split_bf16(x), axis=1), tri_twice)


def _dot_nt(a, b):
    return lax.dot_general(a, b, NT, preferred_element_type=F32)


def _dot_tn(a, b):
    return lax.dot_general(a, b, TN, preferred_element_type=F32)


def _split_bf16(x):
    hi = x.astype(BF16)
    lo = (x - hi.astype(F32)).astype(BF16)
    return hi, lo


def _ln_stats(x):
    mu = jnp.mean(x, axis=-1, keepdims=True)
    xc = x - mu
    var = jnp.mean(xc * xc, axis=-1, keepdims=True)
    rstd = lax.rsqrt(var + LN_EPS)
    return xc * rstd, rstd


def _ln_bwd(dy, xhat, rstd, g):
    dxh = dy * g
    m1 = jnp.mean(dxh, axis=-1, keepdims=True)
    m2 = jnp.mean(dxh * xhat, axis=-1, keepdims=True)
    return rstd * (dxh - m1 - xhat * m2)


def _params(*sem):
    return pltpu.CompilerParams(dimension_semantics=sem, vmem_limit_bytes=VMEM_LIMIT)


def _ln_in(x, g, b):
    t = x.shape[0]
    tm = min(512, t)

    def body(x_ref, g_ref, b_ref, hf_ref, hb_ref):
        xhat, _ = _ln_stats(x_ref[...])
        y = xhat * g_ref[...] + b_ref[...]
        hf_ref[...] = y
        hb_ref[...] = y.astype(BF16)

    row = pl.BlockSpec((tm, D_MODEL), lambda i: (i, 0))
    vec = pl.BlockSpec((1, D_MODEL), lambda i: (0, 0))
    return pl.pallas_call(
        body, grid=(t // tm,), in_specs=[row, vec, vec], out_specs=[row, row],
        out_shape=[jax.ShapeDtypeStruct((t, D_MODEL), F32), jax.ShapeDtypeStruct((t, D_MODEL), BF16)],
        compiler_params=_params("parallel"), name="ln_in")(x, g, b)


def _in_proj(h, w, col0, ncol, out_dtype, scale_first, name):
    t = h.shape[0]
    tm = min(1024, t)

    def body(h_ref, w_ref, o_ref):
        res = _dot(h_ref[...], w_ref[...])
        if scale_first:
            res = res * jnp.where(pl.program_id(0) == 0, Q_SCALE, 1.0)
        o_ref[...] = res.astype(out_dtype)

    return pl.pallas_call(
        body, grid=(ncol, t // tm),
        in_specs=[pl.BlockSpec((tm, D_MODEL), lambda j, i: (i, 0)),
                  pl.BlockSpec((D_MODEL, D_MODEL), lambda j, i: (0, j + col0))],
        out_specs=pl.BlockSpec((tm, D_MODEL), lambda j, i: (i, j)),
        out_shape=jax.ShapeDtypeStruct((t, ncol * D_MODEL), out_dtype),
        compiler_params=_params("parallel", "parallel"), name=name)(h, w)


def _sb_block(l, diag, tri, carry):
    lnb = jnp.minimum(-l, 0.0) - jnp.log(1.0 + jnp.exp(-jnp.abs(l)))
    if diag is not None:
        lnb = jnp.where(diag, lnb, 0.0)
    suffix = _sum_along_keys(lnb, tri)
    a = jnp.exp(l + suffix + carry)
    if diag is not None:
        a = jnp.where(diag, a, 0.0)
    return a, lnb, suffix[:, 0:1]


def _attn_consts():
    head = lax.broadcasted_iota(jnp.int32, (Q_TILE, GROUP_W), 1) // HEAD_DIM
    row = lax.broadcasted_iota(jnp.int32, (Q_TILE, Q_BLOCK), 0)
    col = lax.broadcasted_iota(jnp.int32, (Q_TILE, Q_BLOCK), 1)
    key = row % Q_BLOCK
    return head, col < row, (key >= col).astype(BF16), (key <= col).astype(BF16)


def _split_heads(t, head):
    zero = jnp.zeros_like(t)
    head = head[:t.shape[0]]
    return [jnp.where(head == h, t, zero) for h in range(HEAD_GROUP)]


def _key_block(ref, kb):
    return ref[pl.ds(pl.multiple_of(kb * Q_BLOCK, Q_BLOCK), Q_BLOCK), :]


def _attn_fwd(qkv, nb, seq, w_in_early, shards):
    nq = seq // Q_TILE

    def body(q_ref, k_ref, v_ref, full_in, s0, s1, s2, s3, s4, o_ref, d0, d1, d2, d3, d4, send_sems, recv_sems, local_sems):
        del full_in
        qi = pl.program_id(2)
        first = (pl.program_id(0) == 0) & (pl.program_id(1) == 0) & (qi == 0)
        last = (pl.program_id(0) == nb - 1) & (pl.program_id(1) == N_GROUPS - 1) & (qi == nq - 1)
        start_gather, finish_gather = _late_gather([s0, s1, s2, s3, s4], [d0, d1, d2, d3, d4],
                                                   send_sems, recv_sems, local_sems)
        pl.when(first)(start_gather)
        head, diag, tri, _ = _attn_consts()
        qh = _split_heads(q_ref[...], head)
        o_ref[...] = jnp.zeros_like(o_ref)
        zero_col = jnp.zeros((Q_BLOCK, 1), F32)

        def block(kb, carries, rows, masked):
            kblk = _key_block(k_ref, kb)
            vstack = jnp.concatenate(_split_heads(_key_block(v_ref, kb), head), axis=0)
            mask = diag[:Q_TILE - rows.start] if masked else None
            weights, out = [], []
            for h in range(HEAD_GROUP):
                a, _, blk_sum = _sb_block(_dot_nt(qh[h][rows], kblk), mask, tri, carries[h][rows])
                weights.append(a.astype(BF16))
                out.append(carries[h][rows] + blk_sum)
            o_ref[rows, :] += _dot(jnp.concatenate(weights, axis=1), vstack)
            return out

        second = block(2 * qi + 1, [jnp.zeros((Q_TILE, 1), F32)] * HEAD_GROUP, slice(Q_BLOCK, Q_TILE), True)
        carries = tuple(jnp.concatenate([zero_col, c], axis=0) for c in second)
        everything = slice(0, Q_TILE)
        carries = tuple(block(2 * qi, carries, everything, True))
        lax.fori_loop(1, 2 * qi + 1, lambda jj, cs: tuple(block(2 * qi - jj, cs, everything, False)), carries)
        pl.when(last)(finish_gather)

    n_shards = len(shards)
    full = [jax.ShapeDtypeStruct((D_MODEL, IN_WIDTH), BF16)] + [jax.ShapeDtypeStruct((D_MODEL, D_MODEL), BF16)] * 3 \
        + [jax.ShapeDtypeStruct((CONV_ROWS, D_MODEL), F32)]
    res = pl.pallas_call(
        body, grid=(nb, N_GROUPS, nq),
        in_specs=[pl.BlockSpec((Q_TILE, GROUP_W), lambda b, g, i: (b * nq + i, g)),
                  pl.BlockSpec((seq, GROUP_W), lambda b, g, i: (b, N_GROUPS + g)),
                  pl.BlockSpec((seq, GROUP_W), lambda b, g, i: (b, 2 * N_GROUPS + g))] + [ANY] * (1 + n_shards),
        out_specs=[pl.BlockSpec((Q_TILE, GROUP_W), lambda b, g, i: (b * nq + i, g))] + [ANY] * n_shards,
        out_shape=[jax.ShapeDtypeStruct((nb * seq, D_MODEL), F32)] + full,
        input_output_aliases={3: 1},
        scratch_shapes=[pltpu.SemaphoreType.DMA((3 * n_shards,)), pltpu.SemaphoreType.DMA((3 * n_shards,)),
                        pltpu.SemaphoreType.DMA((n_shards - 1,))],
        compiler_params=_params("arbitrary", "arbitrary", "arbitrary"), name="attn_fwd")(
            qkv, qkv, qkv, w_in_early, *shards)
    return res[0], res[1:]


def _attn_bwd(qkv, do, nb, seq):
    nq = seq // Q_TILE
    n_kb = seq // Q_BLOCK

    def body(q_ref, k_ref, v_ref, do_ref, dq_ref, dk_ref, dv_ref, g_s, beta_s, dq_acc, dk_acc, dv_acc):
        qi = pl.program_id(2)

        @pl.when(qi == 0)
        def _():
            dk_acc[...] = jnp.zeros_like(dk_acc)
            dv_acc[...] = jnp.zeros_like(dv_acc)

        head, diag, tri_suffix, tri_prefix = _attn_consts()
        qh = _split_heads(q_ref[...], head)
        doh = _split_heads(do_ref[...], head)
        dq_acc[...] = jnp.zeros_like(dq_acc)
        zero_col = jnp.zeros((Q_BLOCK, 1), F32)
        everything = slice(0, Q_TILE)
        second = slice(Q_BLOCK, Q_TILE)

        def block_a(kb, carries, rows, masked):
            ks = pl.multiple_of(kb * Q_BLOCK, Q_BLOCK)
            kblk = _key_block(k_ref, kb)
            vblk = _key_block(v_ref, kb)
            mask = diag[:Q_TILE - rows.start] if masked else None
            weights, out = [], []
            for h in range(HEAD_GROUP):
                l = _dot_nt(qh[h][rows], kblk)
                a, lnb, blk_sum = _sb_block(l, mask, tri_suffix, carries[h][rows])
                beta = jnp.exp(l + lnb)
                if masked:
                    beta = jnp.where(mask, beta, 0.0)
                g_s[h, kb, rows, :] = a * _dot_nt(doh[h][rows], vblk)
                beta_s[h, kb, rows, :] = beta.astype(BF16)
                weights.append(a.astype(BF16))
                out.append(carries[h][rows] + blk_sum)
            dv_acc[pl.ds(ks, Q_BLOCK), :] += _dot_tn(jnp.concatenate(weights, axis=0),
                                                     jnp.concatenate([t[rows] for t in doh], axis=0))
            return out

        part = block_a(2 * qi + 1, [jnp.zeros((Q_TILE, 1), F32)] * HEAD_GROUP, second, True)
        carries = tuple(jnp.concatenate([zero_col, c], axis=0) for c in part)
        carries = tuple(block_a(2 * qi, carries, everything, True))
        lax.fori_loop(1, 2 * qi + 1, lambda jj, cs: tuple(block_a(2 * qi - jj, cs, everything, False)), carries)

        def block_b(kb, prefixes, rows, masked):
            ks = pl.multiple_of(kb * Q_BLOCK, Q_BLOCK)
            kstack = jnp.concatenate(_split_heads(_key_block(k_ref, kb), head), axis=0)
            dls, out = [], []
            for h in range(HEAD_GROUP):
                g = g_s[h, kb, rows, :]
                beta = beta_s[h, kb, rows, :].astype(F32)
                incl = _sum_along_keys(g, tri_prefix)
                before = incl - g + prefixes[h][rows]
                dl = g - beta * (g + before)
                if masked:
                    dl = jnp.where(diag[:Q_TILE - rows.start], dl, 0.0)
                dls.append(dl.astype(BF16))
                out.append(prefixes[h][rows] + incl[:, Q_BLOCK - 1:Q_BLOCK])
            dq_acc[rows, :] += _dot(jnp.concatenate(dls, axis=1), kstack)
            dk_acc[pl.ds(ks, Q_BLOCK), :] += _dot_tn(jnp.concatenate(dls, axis=0),
                                                     jnp.concatenate([t[rows] for t in qh], axis=0))
            return out

        zeros = (jnp.zeros((Q_TILE, 1), F32),) * HEAD_GROUP
        prefixes = lax.fori_loop(0, 2 * qi, lambda kb, ps: tuple(block_b(kb, ps, everything, False)), zeros)
        prefixes = block_b(2 * qi, prefixes, everything, True)
        block_b(2 * qi + 1, prefixes, second, True)
        dq_ref[...] = (dq_acc[...] * Q_SCALE).astype(BF16)

        @pl.when(qi == nq - 1)
        def _():
            dk_ref[...] = dk_acc[...].astype(BF16)
            dv_ref[...] = dv_acc[...].astype(BF16)

    t = nb * seq
    qspec = pl.BlockSpec((Q_TILE, GROUP_W), lambda b, g, i: (b * nq + i, g))
    kvout = pl.BlockSpec((seq, GROUP_W), lambda b, g, i: (b, g))
    return pl.pallas_call(
        body, grid=(nb, N_GROUPS, nq),
        in_specs=[qspec,
                  pl.BlockSpec((seq, GROUP_W), lambda b, g, i: (b, N_GROUPS + g)),
                  pl.BlockSpec((seq, GROUP_W), lambda b, g, i: (b, 2 * N_GROUPS + g)),
                  qspec],
        out_specs=[qspec, kvout, kvout],
        out_shape=[jax.ShapeDtypeStruct((t, D_MODEL), BF16)] * 3,
        scratch_shapes=[pltpu.VMEM((HEAD_GROUP, n_kb, Q_TILE, Q_BLOCK), F32),
                        pltpu.VMEM((HEAD_GROUP, n_kb, Q_TILE, Q_BLOCK), BF16),
                        pltpu.VMEM((Q_TILE, GROUP_W), F32),
                        pltpu.VMEM((seq, GROUP_W), F32), pltpu.VMEM((seq, GROUP_W), F32)],
        compiler_params=_params("parallel", "parallel", "arbitrary"), name="attn_bwd")(qkv, qkv, qkv, do)


def _conv_block_rows(seq):
    return min(256, seq)


SUBLANES = 8
CONV_SUB = 32
CONV_GROUPS = CONV_SUB // SUBLANES


def _fill_shifts(sh_ref, n):
    for r in range(1, SUBLANES):
        sh_ref[r, 0:n, :] = sh_ref[0, pl.ds(r, n), :]


def _shifted(sh_ref, start, cs):
    r = start % SUBLANES
    return sh_ref[r, start - r:start - r + CONV_SUB, cs].reshape(CONV_GROUPS, SUBLANES, -1)


def _tap(w_ref, k, cs):
    row = w_ref[k:k + 1, cs]
    return jnp.broadcast_to(row, (SUBLANES, row.shape[1]))[None]


def _tree_sum(x):
    parts = [x[i] for i in range(x.shape[0])]
    while len(parts) > 1:
        parts = [parts[i] + parts[i + 1] for i in range(0, len(parts), 2)]
    return parts[0]


def _conv_fwd(hrest, conv_w, conv_b, ln_g, ln_b, nb, seq):
    ts = _conv_block_rows(seq)
    nblk = seq // ts
    lanes = 128

    def body(cv_ref, cg_ref, cvh_ref, cgh_ref, z_ref, w_ref, cb_ref, g_ref, b_ref, u1_ref, cvin_ref, ush):
        i = pl.program_id(1)
        halo = cvh_ref[...] * _sigmoid(cgh_ref[...])
        ush[0, 0:HALO, :] = jnp.where(i > 0, halo, 0.0)
        ush[0, HALO:HALO + ts, :] = cv_ref[...] * _sigmoid(cg_ref[...])
        _fill_shifts(ush, HALO + ts - SUBLANES)
        for cc in range(D_MODEL // lanes):
            cs = slice(cc * lanes, (cc + 1) * lanes)
            for r0 in range(0, ts, CONV_SUB):
                acc = jnp.zeros((CONV_GROUPS, SUBLANES, lanes), F32) + _tap(cb_ref, 0, cs)
                for k in range(CONV_K):
                    acc = acc + _tap(w_ref, k, cs) * _shifted(ush, HALO - CONV_K + 1 + k + r0, cs)
                u1_ref[r0:r0 + CONV_SUB, cs] = acc.reshape(CONV_SUB, lanes)
        xhat, _ = _ln_stats(u1_ref[...])
        u2 = xhat * g_ref[...] + b_ref[...]
        z = z_ref[...]
        cvin_ref[...] = (u2 * _sigmoid(u2) * z * _sigmoid(z)).astype(BF16)

    def main(colblk):
        return pl.BlockSpec((ts, D_MODEL), lambda b, i: (b * nblk + i, colblk))

    def halo(colblk):
        return pl.BlockSpec((HALO, D_MODEL),
                            lambda b, i: (jnp.maximum((b * seq + i * ts) // HALO - 1, 0), colblk))

    vec = pl.BlockSpec((1, D_MODEL), lambda b, i: (0, 0))
    t = nb * seq
    return pl.pallas_call(
        body, grid=(nb, nblk),
        in_specs=[main(1), main(2), halo(1), halo(2), main(3),
                  pl.BlockSpec((CONV_ROWS, D_MODEL), lambda b, i: (0, 0)), vec, vec, vec],
        out_specs=[main(0), main(0)],
        out_shape=[jax.ShapeDtypeStruct((t, D_MODEL), F32), jax.ShapeDtypeStruct((t, D_MODEL), BF16)],
        scratch_shapes=[pltpu.VMEM((SUBLANES, HALO + ts, D_MODEL), F32)],
        compiler_params=_params("parallel", "parallel"), name="conv_fwd")(
            hrest, hrest, hrest, hrest, hrest, conv_w, conv_b, ln_g, ln_b)


def _tail(o, hrest, cvin, h0, tgt, w_sb, w_cv, w_out, ln_g, ln_b):
    t = o.shape[0]
    tm = min(256, t)

    def body(o_ref, z_ref, gs_ref, gc_ref, cvin_ref, h0_ref, tgt_ref, wsb_ref, wcv_ref, wout_ref, g_ref, b_ref,
             dr_ref, drb_ref, a_ref, mg_ref, dysb_ref, dycv_ref, do_ref, dz_ref, dgs_ref, dgc_ref, dcvin_ref,
             st_ref):
        @pl.when(pl.program_id(0) == 0)
        def _():
            st_ref[...] = jnp.zeros_like(st_ref)

        o = o_ref[...]
        z = z_ref[...]
        sz = _sigmoid(z)
        a = (o * z * sz).astype(BF16)
        a_ref[...] = a
        y_sb = _dot(a, wsb_ref[...])
        y_cv = _dot(cvin_ref[...], wcv_ref[...])
        s_sb = _sigmoid(gs_ref[...])
        s_cv = _sigmoid(gc_ref[...])
        merged = (s_sb * y_sb + s_cv * y_cv).astype(BF16)
        mg_ref[...] = merged
        r = ALPHA * h0_ref[...] + _dot(merged, wout_ref[...])
        xhat, rstd = _ln_stats(r)
        g = g_ref[...]
        err = xhat * g + b_ref[...] - tgt_ref[...]
        dy = err * (1.0 / D_MODEL)
        st_ref[0:1, :] += jnp.sum(dy * xhat, axis=0, keepdims=True)
        st_ref[1:2, :] += jnp.sum(dy, axis=0, keepdims=True)
        st_ref[2:3, :] += (0.5 / D_MODEL) * jnp.sum(err * err, axis=0, keepdims=True)
        dr = _ln_bwd(dy, xhat, rstd, g)
        dr_ref[...] = dr
        drb = dr.astype(BF16)
        drb_ref[...] = drb
        dm = _dot_nt(drb, wout_ref[...])
        dy_sb = (dm * s_sb).astype(BF16)
        dy_cv = (dm * s_cv).astype(BF16)
        dysb_ref[...] = dy_sb
        dycv_ref[...] = dy_cv
        dgs_ref[...] = (dm * y_sb * s_sb * (1.0 - s_sb)).astype(BF16)
        dgc_ref[...] = (dm * y_cv * s_cv * (1.0 - s_cv)).astype(BF16)
        da = _dot_nt(dy_sb, wsb_ref[...])
        dcvin_ref[...] = _dot_nt(dy_cv, wcv_ref[...])
        do_ref[...] = (da * z * sz).astype(BF16)
        dz_ref[...] = (da * o * sz * (1.0 + z * (1.0 - sz))).astype(BF16)

    def tok(colblk=0):
        return pl.BlockSpec((tm, D_MODEL), lambda i: (i, colblk))

    wspec = pl.BlockSpec((D_MODEL, D_MODEL), lambda i: (0, 0), pipeline_mode=pl.Buffered(1))
    vec = pl.BlockSpec((1, D_MODEL), lambda i: (0, 0))
    bf = jax.ShapeDtypeStruct((t, D_MODEL), BF16)
    f32 = jax.ShapeDtypeStruct((t, D_MODEL), F32)
    return pl.pallas_call(
        body, grid=(t // tm,),
        in_specs=[tok(), tok(0), tok(4), tok(5), tok(), tok(), tok(), wspec, wspec, wspec, vec, vec],
        out_specs=[tok()] * 11 + [pl.BlockSpec((8, D_MODEL), lambda i: (0, 0))],
        out_shape=[f32] + [bf] * 9 + [f32, jax.ShapeDtypeStruct((8, D_MODEL), F32)],
        compiler_params=_params("arbitrary"), name="tail")(
            o, hrest, hrest, hrest, cvin, h0, tgt, w_sb, w_cv, w_out, ln_g, ln_b)


def _conv_branch_bwd(dcvin, u1, hrest, ln_g, ln_b):
    t = u1.shape[0]
    tm = min(512, t)

    def body(dc_ref, u1_ref, z_ref, g_ref, b_ref, du1_ref, dz_ref, st_ref):
        @pl.when(pl.program_id(0) == 0)
        def _():
            st_ref[...] = jnp.zeros_like(st_ref)

        xhat, rstd = _ln_stats(u1_ref[...])
        g = g_ref[...]
        u2 = xhat * g + b_ref[...]
        s2 = _sigmoid(u2)
        z = z_ref[...]
        sz = _sigmoid(z)
        dc = dc_ref[...]
        dz_ref[...] = (dc * (u2 * s2) * sz * (1.0 + z * (1.0 - sz))).astype(BF16)
        du2 = dc * (z * sz) * s2 * (1.0 + u2 * (1.0 - s2))
        st_ref[0:1, :] += jnp.sum(du2 * xhat, axis=0, keepdims=True)
        st_ref[1:2, :] += jnp.sum(du2, axis=0, keepdims=True)
        du1 = _ln_bwd(du2, xhat, rstd, g)
        du1_ref[...] = du1
        st_ref[2:3, :] += jnp.sum(du1, axis=0, keepdims=True)

    tok = pl.BlockSpec((tm, D_MODEL), lambda i: (i, 0))
    vec = pl.BlockSpec((1, D_MODEL), lambda i: (0, 0))
    return pl.pallas_call(
        body, grid=(t // tm,),
        in_specs=[tok, tok, pl.BlockSpec((tm, D_MODEL), lambda i: (i, 3)), vec, vec],
        out_specs=[tok, tok, pl.BlockSpec((8, D_MODEL), lambda i: (0, 0))],
        out_shape=[jax.ShapeDtypeStruct((t, D_MODEL), F32), jax.ShapeDtypeStruct((t, D_MODEL), BF16),
                   jax.ShapeDtypeStruct((8, D_MODEL), F32)],
        compiler_params=_params("arbitrary"), name="conv_branch_bwd")(dcvin, u1, hrest, ln_g, ln_b)


def _conv_bwd(du1, hrest, conv_w, nb, seq):
    ts = _conv_block_rows(seq)
    nblk = seq // ts
    lanes = 128
    last_halo = nb * seq // HALO - 1

    def body(du_ref, duh_ref, cv_ref, cg_ref, cvh_ref, cgh_ref, w_ref, dcv_ref, dcg_ref, dw_ref, ush, dsh, dw_part):
        b = pl.program_id(0)
        i = pl.program_id(1)

        @pl.when((b == 0) & (i == 0))
        def _():
            dw_part[...] = jnp.zeros_like(dw_part)

        cv = cv_ref[...]
        sg = _sigmoid(cg_ref[...])
        halo = cvh_ref[...] * _sigmoid(cgh_ref[...])
        ush[0, 0:HALO, :] = jnp.where(i > 0, halo, 0.0)
        ush[0, HALO:HALO + ts, :] = cv * sg
        dsh[0, 0:ts, :] = du_ref[...]
        dsh[0, ts:ts + HALO, :] = jnp.where(i < nblk - 1, duh_ref[...], 0.0)
        _fill_shifts(ush, HALO + ts - SUBLANES)
        _fill_shifts(dsh, HALO + ts - SUBLANES)
        for cc in range(D_MODEL // lanes):
            cs = slice(cc * lanes, (cc + 1) * lanes)
            for r0 in range(0, ts, CONV_SUB):
                rows = slice(r0, r0 + CONV_SUB)
                dmain = _shifted(dsh, r0, cs)
                acc = jnp.zeros((CONV_GROUPS, SUBLANES, lanes), F32)
                for k in range(CONV_K):
                    acc = acc + _tap(w_ref, k, cs) * _shifted(dsh, CONV_K - 1 - k + r0, cs)
                    dw_part[k, :, cs] += _tree_sum(dmain * _shifted(ush, HALO - CONV_K + 1 + k + r0, cs))
                acc = acc.reshape(CONV_SUB, lanes)
                dcv_ref[rows, cs] = (acc * sg[rows, cs]).astype(BF16)
                dcg_ref[rows, cs] = (acc * cv[rows, cs] * sg[rows, cs] * (1.0 - sg[rows, cs])).astype(BF16)

        @pl.when((b == nb - 1) & (i == nblk - 1))
        def _():
            for k in range(CONV_ROWS):
                dw_ref[k:k + 1, :] = jnp.sum(dw_part[k], axis=0, keepdims=True)

    def main(colblk):
        return pl.BlockSpec((ts, D_MODEL), lambda b, i: (b * nblk + i, colblk))

    def halo_before(colblk):
        return pl.BlockSpec((HALO, D_MODEL),
                            lambda b, i: (jnp.maximum((b * seq + i * ts) // HALO - 1, 0), colblk))

    halo_after = pl.BlockSpec((HALO, D_MODEL),
                              lambda b, i: (jnp.minimum((b * seq + (i + 1) * ts) // HALO, last_halo), 0))
    t = nb * seq
    return pl.pallas_call(
        body, grid=(nb, nblk),
        in_specs=[main(0), halo_after, main(1), main(2), halo_before(1), halo_before(2),
                  pl.BlockSpec((CONV_ROWS, D_MODEL), lambda b, i: (0, 0))],
        out_specs=[main(0), main(0), pl.BlockSpec((CONV_ROWS, D_MODEL), lambda b, i: (0, 0))],
        out_shape=[jax.ShapeDtypeStruct((t, D_MODEL), BF16), jax.ShapeDtypeStruct((t, D_MODEL), BF16),
                   jax.ShapeDtypeStruct((CONV_ROWS, D_MODEL), F32)],
        scratch_shapes=[pltpu.VMEM((SUBLANES, HALO + ts, D_MODEL), F32), pltpu.VMEM((SUBLANES, ts + HALO, D_MODEL), F32),
                        pltpu.VMEM((CONV_ROWS, SUBLANES, D_MODEL), F32)],
        compiler_params=_params("arbitrary", "arbitrary"), name="conv_bwd")(
            du1, du1, hrest, hrest, hrest, hrest, conv_w)


def _weight_grad(x, dys, name):
    n = len(dys)
    t = x.shape[0]
    tm = min(512, t)

    def body(x_ref, *refs):
        o_ref = refs[n]
        j = pl.program_id(0)

        @pl.when(pl.program_id(1) == 0)
        def _():
            o_ref[...] = jnp.zeros_like(o_ref)

        for p in range(n):
            @pl.when(j == p)
            def _(p=p):
                o_ref[...] += _dot_tn(x_ref[...], refs[p][...])

    def dy_spec(p):
        return pl.BlockSpec((tm, D_MODEL), lambda j, i: (jnp.where(j == p, i, 0), 0))

    return pl.pallas_call(
        body, grid=(n, t // tm),
        in_specs=[pl.BlockSpec((tm, D_MODEL), lambda j, i: (i, 0))] + [dy_spec(p) for p in range(n)],
        out_specs=pl.BlockSpec((D_MODEL, D_MODEL), lambda j, i: (0, j)),
        out_shape=jax.ShapeDtypeStruct((D_MODEL, n * D_MODEL), F32),
        compiler_params=_params("parallel", "arbitrary"), name=name)(x, *dys)


def _in_proj_bwd(dhs, w_in, dr, x, ln_g, pair):
    n = len(dhs)
    t = x.shape[0]
    tm = min(512, t)
    n_i = t // tm

    def body(*refs):
        dh_refs = refs[:n]
        w_ref, dr_ref, x_ref, g_ref = refs[n:n + 4]
        pair_refs = refs[n + 4:n + 8]
        gx_ref, st_ref = refs[n + 8:n + 10]
        got_refs = refs[n + 10:n + 14]
        acc, send_sems, recv_sems = refs[n + 14:]
        i = pl.program_id(0)
        p = pl.program_id(1)
        pieces, wholes = _chip_exchange_copies(pair_refs, got_refs, send_sems, recv_sems)

        @pl.when((i == 0) & (p == 0))
        def _():
            st_ref[...] = jnp.zeros_like(st_ref)
            for cp in pieces:
                cp.start()

        @pl.when((i == n_i - 1) & (p == n - 1))
        def _():
            for cp in wholes:
                cp.wait_recv()
            for cp in wholes:
                cp.wait_send()

        @pl.when(p == 0)
        def _():
            acc[...] = ALPHA * dr_ref[...]

        for q in range(n):
            @pl.when(p == q)
            def _(q=q):
                acc[...] += _dot_nt(dh_refs[q][...], w_ref[...])

        @pl.when(p == n - 1)
        def _():
            xhat, rstd = _ln_stats(x_ref[...])
            dh0 = acc[...]
            st_ref[0:1, :] += jnp.sum(dh0 * xhat, axis=0, keepdims=True)
            st_ref[1:2, :] += jnp.sum(dh0, axis=0, keepdims=True)
            gx_ref[...] = _ln_bwd(dh0, xhat, rstd, g_ref[...])

    tok = pl.BlockSpec((tm, D_MODEL), lambda i, p: (i, 0))
    got = [jax.ShapeDtypeStruct((3, HALF, SHARD_IN), BF16)] + [jax.ShapeDtypeStruct((3, SHARD_SQ, HALF), BF16)] * 3
    res = pl.pallas_call(
        body, grid=(n_i, n),
        in_specs=[tok] * n + [pl.BlockSpec((D_MODEL, D_MODEL), lambda i, p: (0, p)), tok, tok,
                              pl.BlockSpec((1, D_MODEL), lambda i, p: (0, 0))] + [ANY] * 4,
        out_specs=[tok, pl.BlockSpec((8, D_MODEL), lambda i, p: (0, 0))] + [ANY] * 4,
        out_shape=[jax.ShapeDtypeStruct((t, D_MODEL), F32), jax.ShapeDtypeStruct((8, D_MODEL), F32)] + got,
        scratch_shapes=[pltpu.VMEM((tm, D_MODEL), F32), pltpu.SemaphoreType.DMA((12,)), pltpu.SemaphoreType.DMA((12,))],
        compiler_params=_params("arbitrary", "arbitrary"), name="in_proj_bwd")(*dhs, w_in, dr, x, ln_g, *pair)
    return res[0], res[1], res[2:]


def _forward_backward(x, tgt, nb, seq, shards, vecs):
    w_in_early = _gather_first(shards[0])
    h0, h0b = _ln_in(x, vecs["ln_in_g"], vecs["ln_in_b"])
    qkv = _in_proj(h0b, w_in_early, 0, 3, BF16, True, "in_proj_qkv")
    o, (w_in, w_sb, w_cv, w_out, conv_w) = _attn_fwd(qkv, nb, seq, w_in_early, shards)
    hrest = _in_proj(h0b, w_in, 3, 6, F32, False, "in_proj_rest")
    u1, cvin = _conv_fwd(hrest, conv_w, vecs["conv_b"], vecs["conv_ln_g"], vecs["conv_ln_b"], nb, seq)
    (dr, drb, a, merged, dy_sb, dy_cv, do, dz_sb, dg_sb, dg_cv, dcvin, st_tail) = _tail(
        o, hrest, cvin, h0, tgt, w_sb, w_cv, w_out, vecs["ln_post_g"], vecs["ln_post_b"])
    d_w_sb = _weight_grad(a, [dy_sb], "grad_w_sb")
    d_w_cv = _weight_grad(cvin, [dy_cv], "grad_w_cv")
    d_w_out = _weight_grad(merged, [drb], "grad_w_out")
    du1, dz_cv, st_conv = _conv_branch_bwd(dcvin, u1, hrest, vecs["conv_ln_g"], vecs["conv_ln_b"])
    dc_val, dc_gate, d_conv_w = _conv_bwd(du1, hrest, conv_w, nb, seq)
    dq, dk, dv = _attn_bwd(qkv, do, nb, seq)
    dhs = [dq, dk, dv, dz_sb, dc_val, dc_gate, dz_cv, dg_sb, dg_cv]
    d_w_in = _weight_grad(h0b, dhs, "grad_w_in")
    small = jnp.concatenate([st_conv[2:3], st_conv[0:2], st_tail[0:3], d_conv_w], axis=0)
    return w_in, dhs, dr, (d_w_in, d_w_sb, d_w_cv, d_w_out), small


def _place():
    x, y, c = lax.axis_index("x"), lax.axis_index("y"), lax.axis_index("c")
    chips = [(1 - x, y), (x, 1 - y), (1 - x, 1 - y)]
    return x, y, c, chips


N_EARLY = -(-3 * D_MODEL // SHARD_IN)
GATHER_PARTS = 8
GATHER_WINDOW = 2


def _w_in_slot(full, chip):
    return full.at[:, pl.ds(pl.multiple_of(chip * SHARD_IN, 128), SHARD_IN)]


def _gather_first(w_in):
    part_rows = HALF // GATHER_PARTS
    n_sem = 3 * GATHER_PARTS
    across_x, across_y, diagonal = 0, 1, 2

    def body(src, dst, ici_send, ici_recv, fwd_send, fwd_recv, local_sem):
        x, y, c, chips = _place()
        me = 2 * x + y
        sibling = (x, y, 1 - c)
        owner = [2 * px + py for px, py in chips]

        def part(ref, k, half):
            return ref.at[pl.ds(pl.multiple_of(half * HALF + k * part_rows, 16), part_rows), :]

        def held(j, k, half):
            return part(_w_in_slot(dst, owner[j]), k, half)

        def sent(j, k):
            px, py = chips[j]
            return _remote(part(src, k, c), part(_w_in_slot(dst, me), k, c), ici_send.at[j * GATHER_PARTS + k],
                           ici_recv.at[j * GATHER_PARTS + k], (px, py, c))

        def relayed(k):
            px, py = chips[across_y]
            here = held(across_x, k, c)
            return _remote(here, here, ici_send.at[diagonal * GATHER_PARTS + k],
                           ici_recv.at[diagonal * GATHER_PARTS + k], (px, py, c))

        def landed(j, k):
            px, py = chips[across_y if j == diagonal else j]
            return _remote(held(j, k, c), held(j, k, c), ici_send.at[j * GATHER_PARTS + k],
                           ici_recv.at[j * GATHER_PARTS + k], (px, py, c))

        def passed(j, k, half):
            return _remote(held(j, k, half), held(j, k, half), fwd_send.at[j * GATHER_PARTS + k],
                           fwd_recv.at[j * GATHER_PARTS + k], sibling)

        own = _w_in_slot(dst, me)
        rows = D_MODEL // N_CHUNKS
        for k in range(N_CHUNKS):
            pltpu.make_async_copy(src.at[pl.ds(k * rows, rows)], own.at[pl.ds(k * rows, rows)], local_sem).start()
        local = pltpu.make_async_copy(src, own, local_sem)

        @pl.when(me < N_EARLY)
        def _():
            for k in range(GATHER_PARTS):
                for j in (across_x, across_y):
                    if k >= GATHER_WINDOW:
                        sent(j, k - GATHER_WINDOW).wait_send()
                    sent(j, k).start()

        for j in (across_x, across_y, diagonal):
            @pl.when(owner[j] < N_EARLY)
            def _(j=j):
                for k in range(GATHER_PARTS):
                    landed(j, k).wait_recv()
                    passed(j, k, c).start()
                    if j == across_x:
                        relayed(k).start()

        for j in (across_x, across_y, diagonal):
            @pl.when(owner[j] < N_EARLY)
            def _(j=j):
                for k in range(GATHER_PARTS):
                    passed(j, k, 1 - c).wait_recv()
                for k in range(GATHER_PARTS):
                    passed(j, k, c).wait_send()
                    if j == across_x:
                        relayed(k).wait_send()

        @pl.when(me < N_EARLY)
        def _():
            for j in (across_x, across_y):
                for k in range(GATHER_PARTS - GATHER_WINDOW, GATHER_PARTS):
                    sent(j, k).wait_send()

        local.wait()

    return pl.pallas_call(
        body, in_specs=[ANY], out_specs=ANY, out_shape=jax.ShapeDtypeStruct((D_MODEL, IN_WIDTH), BF16),
        scratch_shapes=[pltpu.SemaphoreType.DMA((n_sem,)), pltpu.SemaphoreType.DMA((n_sem,)),
                        pltpu.SemaphoreType.DMA((n_sem,)), pltpu.SemaphoreType.DMA((n_sem,)),
                        pltpu.SemaphoreType.DMA],
        name="gather_first")(w_in)


def _late_gather(srcs, dsts, send_sems, recv_sems, local_sems):
    x, y, c, chips = _place()
    me = 2 * x + y
    small = range(1, 5)

    def slot(a, chip):
        if a == 0:
            return _w_in_slot(dsts[0], chip)
        if a == 4:
            return dsts[a].at[:, pl.ds(pl.multiple_of(chip * SHARD_SQ, 128), SHARD_SQ)]
        return dsts[a].at[pl.ds(pl.multiple_of(chip * SHARD_SQ, 16), SHARD_SQ), :]

    def sems(a, j):
        return send_sems.at[3 * a + j], recv_sems.at[3 * a + j]

    def whole(a, j):
        px, py = chips[j]
        return _remote(srcs[a], slot(a, me), *sems(a, j), (px, py, c))

    def arrival(a, j):
        px, py = chips[j]
        return _arrival(slot(a, 2 * px + py), *sems(a, j), (px, py, c))

    def local(a):
        return pltpu.make_async_copy(srcs[a], slot(a, me), local_sems.at[a - 1])

    def start():
        for a in small:
            local(a).start()
            for j in range(3):
                whole(a, j).start()

        @pl.when(me >= N_EARLY)
        def _():
            for j, (px, py) in enumerate(chips):
                for cp in _chunk_copies(srcs[0], slot(0, me), GATHER_PARTS, *sems(0, j), (px, py, c)):
                    cp.start()

    def finish():
        for j, (px, py) in enumerate(chips):
            for a in small:
                arrival(a, j).wait_recv()

            @pl.when(2 * px + py >= N_EARLY)
            def _(j=j):
                arrival(0, j).wait_recv()

        for a in small:
            for j in range(3):
                whole(a, j).wait_send()
            local(a).wait()

        @pl.when(me >= N_EARLY)
        def _():
            for j in range(3):
                whole(0, j).wait_send()

    return start, finish


def _remote(src, dst, send_sem, recv_sem, device):
    return pltpu.make_async_remote_copy(src_ref=src, dst_ref=dst, send_sem=send_sem, recv_sem=recv_sem,
                                        device_id=device, device_id_type=MESH)


def _chunk_copies(src, dst, n, send_sem, recv_sem, device):
    rows = src.shape[0] // n
    return [_remote(src.at[pl.ds(k * rows, rows)], dst.at[pl.ds(k * rows, rows)], send_sem, recv_sem, device)
            for k in range(n)]


def _row_chunks(src, dst, n, send_sem, recv_sem, device):
    return _chunk_copies(src, dst, n, send_sem, recv_sem, device), _remote(src, dst, send_sem, recv_sem, device)


def _arrival(dst, send_sem, recv_sem, device):
    return _remote(dst, dst, send_sem, recv_sem, device)


def _pair_exchange(d_in, d_sb, d_cv, d_out):
    def body(i0, i1, i2, i3, r0, r1, r2, r3, send_sems, recv_sems):
        x, y, c, _ = _place()
        srcs = [i0, i1, i2, i3]
        theirs = [r0, r1, r2, r3]
        sibling = (x, y, 1 - c)
        start = pl.multiple_of((1 - c) * HALF, 128)
        views = [srcs[0].at[pl.ds(start, HALF), :]] + [srcs[a].at[:, pl.ds(start, HALF)] for a in range(1, 4)]
        pieces, wholes = [], []
        for a in range(4):
            cps, whole = _row_chunks(views[a], theirs[a], N_CHUNKS, send_sems.at[a], recv_sems.at[a], sibling)
            pieces += cps
            wholes.append(whole)
        for cp in pieces:
            cp.start()
        for cp in wholes:
            cp.wait_recv()
        for cp in wholes:
            cp.wait_send()

    shapes = [jax.ShapeDtypeStruct((HALF, IN_WIDTH), F32)] + [jax.ShapeDtypeStruct((D_MODEL, HALF), F32)] * 3
    return pl.pallas_call(
        body, in_specs=[ANY] * 4, out_specs=[ANY] * 4, out_shape=shapes,
        scratch_shapes=[pltpu.SemaphoreType.DMA((4,)), pltpu.SemaphoreType.DMA((4,))],
        name="pair_exchange")(d_in, d_sb, d_cv, d_out)


def _chip_exchange_copies(srcs, theirs, send_sems, recv_sems):
    _, _, c, chips = _place()

    def shard(a, px, py):
        chip = 2 * px + py
        if a == 0:
            return srcs[a].at[:, pl.ds(pl.multiple_of(chip * SHARD_IN, 128), SHARD_IN)]
        return srcs[a].at[pl.ds(pl.multiple_of(chip * SHARD_SQ, 16), SHARD_SQ), :]

    pieces, wholes = [], []
    for j, (px, py) in enumerate(chips):
        for a in range(4):
            cps, whole = _row_chunks(shard(a, px, py), theirs[a].at[j], 4, send_sems.at[3 * a + j],
                                     recv_sems.at[3 * a + j], (px, py, c))
            pieces += cps
            wholes.append(whole)
    return pieces, wholes


def _pair_share(f_in, f_sb, f_cv, f_out):
    def body(i0, i1, i2, i3, o0, o1, o2, o3, send_sems, recv_sems):
        del i0, i1, i2, i3
        x, y, c, _ = _place()
        dsts = [o0, o1, o2, o3]
        sibling = (x, y, 1 - c)

        def half(a, which):
            start = pl.multiple_of(which * HALF, 128)
            if a == 0:
                return dsts[a].at[pl.ds(start, HALF), :]
            return dsts[a].at[:, pl.ds(start, HALF)]

        pieces, sent, arrived = [], [], []
        for a in range(4):
            cps, whole = _row_chunks(half(a, c), half(a, c), N_CHUNKS, send_sems.at[a], recv_sems.at[a], sibling)
            pieces += cps
            sent.append(whole)
            arrived.append(_arrival(half(a, 1 - c), send_sems.at[a], recv_sems.at[a], sibling))
        for cp in pieces:
            cp.start()
        for cp in arrived:
            cp.wait_recv()
        for cp in sent:
            cp.wait_send()

    return pl.pallas_call(
        body, in_specs=[ANY] * 4, out_specs=[ANY] * 4,
        out_shape=[jax.ShapeDtypeStruct((D_MODEL, SHARD_IN), F32)] + [jax.ShapeDtypeStruct((SHARD_SQ, D_MODEL), F32)] * 3,
        input_output_aliases={0: 0, 1: 1, 2: 2, 3: 3},
        scratch_shapes=[pltpu.SemaphoreType.DMA((4,)), pltpu.SemaphoreType.DMA((4,))],
        name="pair_share")(f_in, f_sb, f_cv, f_out)


def _small_allreduce(small):
    def body(s_ref, o_ref, slots, send_sems, recv_sems):
        x, y, c, _ = _place()
        me = 4 * x + 2 * y + c
        slots[me] = s_ref[...]
        sends = []
        for k in range(1, N_DEV):
            px, py, pc = (x + (k >> 2)) % 2, (y + ((k >> 1) & 1)) % 2, (c + (k & 1)) % 2
            sends.append(pltpu.make_async_remote_copy(
                src_ref=s_ref, dst_ref=slots.at[me], send_sem=send_sems.at[k - 1], recv_sem=recv_sems.at[k - 1],
                device_id=(px, py, pc), device_id_type=MESH))
        for cp in sends:
            cp.start()
        for k in range(1, N_DEV):
            px, py, pc = (x + (k >> 2)) % 2, (y + ((k >> 1) & 1)) % 2, (c + (k & 1)) % 2
            pltpu.make_async_remote_copy(
                src_ref=s_ref, dst_ref=slots.at[4 * px + 2 * py + pc], send_sem=send_sems.at[k - 1],
                recv_sem=recv_sems.at[k - 1], device_id=(px, py, pc), device_id_type=MESH).wait_recv()
        for cp in sends:
            cp.wait_send()
        total = slots[0]
        for d in range(1, N_DEV):
            total = total + slots[d]
        o_ref[...] = total
        o_ref[7:8, :] = jnp.zeros((1, D_MODEL), F32) + jnp.sum(total[7:8, :], axis=1, keepdims=True)

    return pl.pallas_call(
        body, in_specs=[pl.BlockSpec(memory_space=pltpu.VMEM)], out_specs=pl.BlockSpec(memory_space=pltpu.VMEM),
        out_shape=jax.ShapeDtypeStruct((SMALL_ROWS, D_MODEL), F32),
        scratch_shapes=[pltpu.VMEM((N_DEV, SMALL_ROWS, D_MODEL), F32), pltpu.SemaphoreType.DMA((N_DEV - 1,)),
                        pltpu.SemaphoreType.DMA((N_DEV - 1,))],
        name="small_allreduce")(small)


SUM_BLOCKS = 8


def _pair_sum(place, full, theirs, by_rows, name):
    rows, cols = theirs.shape
    tr = rows // SUM_BLOCKS

    def body(p_ref, a_ref, b_ref, o_ref):
        del p_ref
        o_ref[...] = (a_ref[...] + b_ref[...]).astype(BF16)

    mine = (lambda i, p: (p[1] * SUM_BLOCKS + i, 0)) if by_rows else (lambda i, p: (i, p[1]))
    spec = pl.BlockSpec((tr, cols), lambda i, p: (i, 0))
    return pl.pallas_call(
        body, out_shape=jax.ShapeDtypeStruct(theirs.shape, BF16),
        grid_spec=pltpu.PrefetchScalarGridSpec(num_scalar_prefetch=1, grid=(SUM_BLOCKS,),
                                               in_specs=[pl.BlockSpec((tr, cols), mine), spec], out_specs=spec),
        compiler_params=_params("parallel"), name=name)(place, full, theirs)


def _chip_sum(place, pair, got, by_rows, name):
    _, rows, cols = got.shape
    tr = rows // SUM_BLOCKS

    def body(p_ref, a_ref, g_ref, o_ref):
        del p_ref
        o_ref[...] = ((a_ref[...].astype(F32) + g_ref[0].astype(F32)) + g_ref[1].astype(F32)) + g_ref[2].astype(F32)

    if by_rows:
        own = lambda i, p: (i, p[0])
        out = lambda i, p: (p[1] * SUM_BLOCKS + i, 0)
        full = (2 * rows, cols)
    else:
        own = lambda i, p: (p[0] * SUM_BLOCKS + i, 0)
        out = lambda i, p: (i, p[1])
        full = (rows, 2 * cols)
    return pl.pallas_call(
        body, out_shape=jax.ShapeDtypeStruct(full, F32),
        grid_spec=pltpu.PrefetchScalarGridSpec(
            num_scalar_prefetch=1, grid=(SUM_BLOCKS,),
            in_specs=[pl.BlockSpec((tr, cols), own), pl.BlockSpec((3, tr, cols), lambda i, p: (0, i, 0))],
            out_specs=pl.BlockSpec((tr, cols), out)),
        compiler_params=_params("parallel"), name=name)(place, pair, got)


def _adamw(w, g, m, v, name):
    rows, cols = w.shape
    tr = rows // 4 if rows % 32 == 0 else rows
    c1 = 1.0 - ADAM_B1 ** ADAM_STEP
    c2 = 1.0 - ADAM_B2 ** ADAM_STEP

    def body(w_ref, g_ref, m_ref, v_ref, d_ref, mo_ref, vo_ref):
        g = g_ref[...]
        mn = ADAM_B1 * m_ref[...] + (1.0 - ADAM_B1) * g
        vn = ADAM_B2 * v_ref[...] + (1.0 - ADAM_B2) * (g * g)
        mo_ref[...] = mn
        vo_ref[...] = vn
        d_ref[...] = -ADAM_LR * ((mn / c1) / (jnp.sqrt(vn / c2) + ADAM_EPS) + ADAM_WD * w_ref[...])

    spec = pl.BlockSpec((tr, cols), lambda i: (i, 0))
    shape = jax.ShapeDtypeStruct(w.shape, F32)
    return pl.pallas_call(body, grid=(rows // tr,), in_specs=[spec] * 4, out_specs=[spec] * 3,
                          out_shape=[shape] * 3, compiler_params=_params("parallel"), name=name)(w, g, m, v)


GRAD_NAMES = ("in", "sb", "cv", "out")


def _place_scalars():
    return jnp.stack([2 * lax.axis_index("x") + lax.axis_index("y"), lax.axis_index("c")]).astype(jnp.int32)


def _pair_sums(grads):
    place = _place_scalars()
    theirs = _pair_exchange(*grads)
    return [_pair_sum(place, grads[a], theirs[a], a == 0, "pair_sum_" + GRAD_NAMES[a]) for a in range(4)]


def _finish_weight_grads(pair, got):
    place = _place_scalars()
    done = [_chip_sum(place, pair[a], got[a], a == 0, "chip_sum_" + GRAD_NAMES[a]) for a in range(4)]
    return _pair_share(*done)


def kernel(x, ln_in_g, ln_in_b, w_in, w_sb_proj, conv_w, conv_b, conv_ln_g, conv_ln_b, w_cv_proj, w_out, ln_post_g, ln_post_b, loss_target, m_ln_in_g, m_ln_in_b, m_w_in, m_w_sb_proj, m_conv_w, m_conv_b, m_conv_ln_g, m_conv_ln_b, m_w_cv_proj, m_w_out, m_ln_post_g, m_ln_post_b, v_ln_in_g, v_ln_in_b, v_w_in, v_w_sb_proj, v_conv_w, v_conv_b, v_conv_ln_g, v_conv_ln_b, v_w_cv_proj, v_w_out, v_ln_post_g, v_ln_post_b):
    nb, seq, _ = x.shape
    t = nb * seq
    vec_names = ("ln_in_g", "ln_in_b", "conv_b", "conv_ln_g", "conv_ln_b", "ln_post_g", "ln_post_b")
    vec_w = dict(zip(vec_names, (ln_in_g, ln_in_b, conv_b, conv_ln_g, conv_ln_b, ln_post_g, ln_post_b)))
    vec_m = dict(zip(vec_names, (m_ln_in_g, m_ln_in_b, m_conv_b, m_conv_ln_g, m_conv_ln_b, m_ln_post_g, m_ln_post_b)))
    vec_v = dict(zip(vec_names, (v_ln_in_g, v_ln_in_b, v_conv_b, v_conv_ln_g, v_conv_ln_b, v_ln_post_g, v_ln_post_b)))
    vecs = {k: a.reshape(1, D_MODEL) for k, a in vec_w.items()}

    pad_taps = lambda a: jnp.pad(a.reshape(CONV_K, SHARD_SQ), ((0, CONV_ROWS - CONV_K), (0, 0)))
    shards = (w_in[0].astype(BF16), w_sb_proj[0].astype(BF16), w_cv_proj[0].astype(BF16), w_out[0].astype(BF16),
              pad_taps(conv_w))
    x2 = x.reshape(t, D_MODEL)
    full_in, dhs, dr, big, small = _forward_backward(x2, loss_target.reshape(t, D_MODEL), nb, seq, shards, vecs)
    pair = _pair_sums(big)
    grad_x, st_in, got = _in_proj_bwd(dhs, full_in, dr, x2, vecs["ln_in_g"], pair)
    g_in, g_sb, g_cv, g_out = _finish_weight_grads(pair, got)
    small = _small_allreduce(jnp.concatenate([st_in[0:2], small], axis=0))

    chip = 2 * lax.axis_index("x") + lax.axis_index("y")
    g_conv = lax.dynamic_slice(small, (8, chip * SHARD_SQ), (CONV_ROWS, SHARD_SQ))
    loss = small[7, 0]

    grads, deltas, new_m, new_v = {}, {}, {}, {}
    stack = lambda d: jnp.concatenate([d[k].reshape(1, D_MODEL) for k in vec_names] + [jnp.zeros((1, D_MODEL), F32)])
    vd, vm, vv = _adamw(stack(vec_w), small[0:8], stack(vec_m), stack(vec_v), "adamw_vectors")
    for r, k in enumerate(vec_names):
        shape = vec_w[k].shape
        grads[k] = small[r].reshape(shape)
        deltas[k], new_m[k], new_v[k] = vd[r].reshape(shape), vm[r].reshape(shape), vv[r].reshape(shape)

    big_w = {"w_in": (w_in, g_in, m_w_in, v_w_in), "w_sb_proj": (w_sb_proj, g_sb, m_w_sb_proj, v_w_sb_proj),
             "w_cv_proj": (w_cv_proj, g_cv, m_w_cv_proj, v_w_cv_proj), "w_out": (w_out, g_out, m_w_out, v_w_out)}
    for k, (w, g, m, v) in big_w.items():
        d, mn, vn = _adamw(w[0], g, m[0], v[0], "adamw_" + k)
        grads[k], deltas[k], new_m[k], new_v[k] = g[None], d[None], mn[None], vn[None]
    d, mn, vn = _adamw(pad_taps(conv_w), g_conv, pad_taps(m_conv_w), pad_taps(v_conv_w), "adamw_conv_w")
    grads["conv_w"] = g_conv[None, :CONV_K]
    deltas["conv_w"], new_m["conv_w"], new_v["conv_w"] = d[None, :CONV_K], mn[None, :CONV_K], vn[None, :CONV_K]

    order = ("ln_in_g", "ln_in_b", "w_in", "w_sb_proj", "conv_w", "conv_b", "conv_ln_g", "conv_ln_b",
             "w_cv_proj", "w_out", "ln_post_g", "ln_post_b")
    return (loss, grad_x.reshape(x.shape), *[grads[k] for k in order], *[deltas[k] for k in order],
            *[new_m[k] for k in order], *[new_v[k] for k in order])
```

```python
import functools

import jax
import jax.numpy as jnp
from jax import lax
from jax.experimental import pallas as pl
from jax.experimental.pallas import tpu as pltpu

F32 = jnp.float32
BF16 = jnp.bfloat16

D_MODEL = 1024
N_HEADS = 16
HEAD_DIM = 64
HEAD_GROUP = 4
GROUP_W = HEAD_GROUP * HEAD_DIM
N_GROUPS = N_HEADS // HEAD_GROUP
N_PIECES = 9
IN_WIDTH = N_PIECES * D_MODEL
Q_BLOCK = 256
Q_TILE = 2 * Q_BLOCK
CONV_K = 31
CONV_ROWS = 32
HALO = 32
LN_EPS = 1e-5
ALPHA = 2.0 ** 0.25
Q_SCALE = 0.125
N_CHIPS = 4
N_DEV = 8
SHARD_IN = IN_WIDTH // N_CHIPS
SHARD_SQ = D_MODEL // N_CHIPS
HALF = D_MODEL // 2
SMALL_ROWS = 40
N_CHUNKS = 16

ADAM_LR = 0.001
ADAM_B1 = 0.9
ADAM_B2 = 0.999
ADAM_EPS = 1e-08
ADAM_WD = 0.01
ADAM_STEP = 10

MESH = pl.DeviceIdType.MESH
ANY = pl.BlockSpec(memory_space=pl.ANY)
VMEM_LIMIT = 60 * 1024 * 1024

NT = (((1,), (1,)), ((), ()))
TN = (((0,), (0,)), ((), ()))


def _sigmoid(x):
    return 1.0 / (1.0 + jnp.exp(-x))


def _dot(a, b):
    return jnp.dot(a, b, preferred_element_type=F32)


def _sum_along_keys(x, tri_twice):
    return _dot(jnp.concatenate(_split_bf16(x), axis=1), tri_twice)


def _dot_nt(a, b):
    return lax.dot_general(a, b, NT, preferred_element_type=F32)


def _dot_tn(a, b):
    return lax.dot_general(a, b, TN, preferred_element_type=F32)


def _split_bf16(x):
    hi = x.astype(BF16)
    lo = (x - hi.astype(F32)).astype(BF16)
    return hi, lo


def _ln_stats(x):
    mu = jnp.mean(x, axis=-1, keepdims=True)
    xc = x - mu
    var = jnp.mean(xc * xc, axis=-1, keepdims=True)
    rstd = lax.rsqrt(var + LN_EPS)
    return xc * rstd, rstd


def _ln_bwd(dy, xhat, rstd, g):
    dxh = dy * g
    m1 = jnp.mean(dxh, axis=-1, keepdims=True)
    m2 = jnp.mean(dxh * xhat, axis=-1, keepdims=True)
    return rstd * (dxh - m1 - xhat * m2)


def _params(*sem):
    return pltpu.CompilerParams(dimension_semantics=sem, vmem_limit_bytes=VMEM_LIMIT)


def _ln_in(x, g, b):
    t = x.shape[0]
    tm = min(512, t)

    def body(x_ref, g_ref, b_ref, hf_ref, hb_ref):
        xhat, _ = _ln_stats(x_ref[...])
        y = xhat * g_ref[...] + b_ref[...]
        hf_ref[...] = y
        hb_ref[...] = y.astype(BF16)

    row = pl.BlockSpec((tm, D_MODEL), lambda i: (i, 0))
    vec = pl.BlockSpec((1, D_MODEL), lambda i: (0, 0))
    return pl.pallas_call(
        body, grid=(t // tm,), in_specs=[row, vec, vec], out_specs=[row, row],
        out_shape=[jax.ShapeDtypeStruct((t, D_MODEL), F32), jax.ShapeDtypeStruct((t, D_MODEL), BF16)],
        compiler_params=_params("parallel"), name="ln_in")(x, g, b)


def _in_proj(h, w, col0, ncol, out_dtype, scale_first, name):
    t = h.shape[0]
    tm = min(1024, t)

    def body(h_ref, w_ref, o_ref):
        res = _dot(h_ref[...], w_ref[...])
        if scale_first:
            res = res * jnp.where(pl.program_id(0) == 0, Q_SCALE, 1.0)
        o_ref[...] = res.astype(out_dtype)

    return pl.pallas_call(
        body, grid=(ncol, t // tm),
        in_specs=[pl.BlockSpec((tm, D_MODEL), lambda j, i: (i, 0)),
                  pl.BlockSpec((D_MODEL, D_MODEL), lambda j, i: (0, j + col0))],
        out_specs=pl.BlockSpec((tm, D_MODEL), lambda j, i: (i, j)),
        out_shape=jax.ShapeDtypeStruct((t, ncol * D_MODEL), out_dtype),
        compiler_params=_params("parallel", "parallel"), name=name)(h, w)


def _sb_block(l, diag, tri, carry):
    lnb = jnp.minimum(-l, 0.0) - jnp.log(1.0 + jnp.exp(-jnp.abs(l)))
    if diag is not None:
        lnb = jnp.where(diag, lnb, 0.0)
    suffix = _sum_along_keys(lnb, tri)
    a = jnp.exp(l + suffix + carry)
    if diag is not None:
        a = jnp.where(diag, a, 0.0)
    return a, lnb, suffix[:, 0:1]


def _attn_consts():
    head = lax.broadcasted_iota(jnp.int32, (Q_TILE, GROUP_W), 1) // HEAD_DIM
    row = lax.broadcasted_iota(jnp.int32, (Q_TILE, Q_BLOCK), 0)
    col = lax.broadcasted_iota(jnp.int32, (Q_TILE, Q_BLOCK), 1)
    key = row % Q_BLOCK
    return head, col < row, (key >= col).astype(BF16), (key <= col).astype(BF16)


def _split_heads(t, head):
    zero = jnp.zeros_like(t)
    head = head[:t.shape[0]]
    return [jnp.where(head == h, t, zero) for h in range(HEAD_GROUP)]


def _key_block(ref, kb):
    return ref[pl.ds(pl.multiple_of(kb * Q_BLOCK, Q_BLOCK), Q_BLOCK), :]


def _attn_fwd(qkv, nb, seq, w_in_early, shards):
    nq = seq // Q_TILE

    def body(q_ref, k_ref, v_ref, full_in, s0, s1, s2, s3, s4, o_ref, d0, d1, d2, d3, d4, send_sems, recv_sems, local_sems):
        del full_in
        qi = pl.program_id(2)
        first = (pl.program_id(0) == 0) & (pl.program_id(1) == 0) & (qi == 0)
        last = (pl.program_id(0) == nb - 1) & (pl.program_id(1) == N_GROUPS - 1) & (qi == nq - 1)
        start_gather, finish_gather = _late_gather([s0, s1, s2, s3, s4], [d0, d1, d2, d3, d4],
                                                   send_sems, recv_sems, local_sems)
        pl.when(first)(start_gather)
        head, diag, tri, _ = _attn_consts()
        qh = _split_heads(q_ref[...], head)
        o_ref[...] = jnp.zeros_like(o_ref)
        zero_col = jnp.zeros((Q_BLOCK, 1), F32)

        def block(kb, carries, rows, masked):
            kblk = _key_block(k_ref, kb)
            vstack = jnp.concatenate(_split_heads(_key_block(v_ref, kb), head), axis=0)
            mask = diag[:Q_TILE - rows.start] if masked else None
            weights, out = [], []
            for h in range(HEAD_GROUP):
                a, _, blk_sum = _sb_block(_dot_nt(qh[h][rows], kblk), mask, tri, carries[h][rows])
                weights.append(a.astype(BF16))
                out.append(carries[h][rows] + blk_sum)
            o_ref[rows, :] += _dot(jnp.concatenate(weights, axis=1), vstack)
            return out

        second = block(2 * qi + 1, [jnp.zeros((Q_TILE, 1), F32)] * HEAD_GROUP, slice(Q_BLOCK, Q_TILE), True)
        carries = tuple(jnp.concatenate([zero_col, c], axis=0) for c in second)
        everything = slice(0, Q_TILE)
        carries = tuple(block(2 * qi, carries, everything, True))
        lax.fori_loop(1, 2 * qi + 1, lambda jj, cs: tuple(block(2 * qi - jj, cs, everything, False)), carries)
        pl.when(last)(finish_gather)

    n_shards = len(shards)
    full = [jax.ShapeDtypeStruct((D_MODEL, IN_WIDTH), BF16)] + [jax.ShapeDtypeStruct((D_MODEL, D_MODEL), BF16)] * 3 \
        + [jax.ShapeDtypeStruct((CONV_ROWS, D_MODEL), F32)]
    res = pl.pallas_call(
        body, grid=(nb, N_GROUPS, nq),
        in_specs=[pl.BlockSpec((Q_TILE, GROUP_W), lambda b, g, i: (b * nq + i, g)),
                  pl.BlockSpec((seq, GROUP_W), lambda b, g, i: (b, N_GROUPS + g)),
                  pl.BlockSpec((seq, GROUP_W), lambda b, g, i: (b, 2 * N_GROUPS + g))] + [ANY] * (1 + n_shards),
        out_specs=[pl.BlockSpec((Q_TILE, GROUP_W), lambda b, g, i: (b * nq + i, g))] + [ANY] * n_shards,
        out_shape=[jax.ShapeDtypeStruct((nb * seq, D_MODEL), F32)] + full,
        input_output_aliases={3: 1},
        scratch_shapes=[pltpu.SemaphoreType.DMA((3 * n_shards,)), pltpu.SemaphoreType.DMA((3 * n_shards,)),
                        pltpu.SemaphoreType.DMA((n_shards - 1,))],
        compiler_params=_params("arbitrary", "arbitrary", "arbitrary"), name="attn_fwd")(
            qkv, qkv, qkv, w_in_early, *shards)
    return res[0], res[1:]


def _attn_bwd(qkv, do, nb, seq):
    nq = seq // Q_TILE
    n_kb = seq // Q_BLOCK

    def body(q_ref, k_ref, v_ref, do_ref, dq_ref, dk_ref, dv_ref, g_s, beta_s, dq_acc, dk_acc, dv_acc):
        qi = pl.program_id(2)

        @pl.when(qi == 0)
        def _():
            dk_acc[...] = jnp.zeros_like(dk_acc)
            dv_acc[...] = jnp.zeros_like(dv_acc)

        head, diag, tri_suffix, tri_prefix = _attn_consts()
        qh = _split_heads(q_ref[...], head)
        doh = _split_heads(do_ref[...], head)
        dq_acc[...] = jnp.zeros_like(dq_acc)
        zero_col = jnp.zeros((Q_BLOCK, 1), F32)
        everything = slice(0, Q_TILE)
        second = slice(Q_BLOCK, Q_TILE)

        def block_a(kb, carries, rows, masked):
            ks = pl.multiple_of(kb * Q_BLOCK, Q_BLOCK)
            kblk = _key_block(k_ref, kb)
            vblk = _key_block(v_ref, kb)
            mask = diag[:Q_TILE - rows.start] if masked else None
            weights, out = [], []
            for h in range(HEAD_GROUP):
                l = _dot_nt(qh[h][rows], kblk)
                a, lnb, blk_sum = _sb_block(l, mask, tri_suffix, carries[h][rows])
                beta = jnp.exp(l + lnb)
                if masked:
                    beta = jnp.where(mask, beta, 0.0)
                g_s[h, kb, rows, :] = a * _dot_nt(doh[h][rows], vblk)
                beta_s[h, kb, rows, :] = beta.astype(BF16)
                weights.append(a.astype(BF16))
                out.append(carries[h][rows] + blk_sum)
            dv_acc[pl.ds(ks, Q_BLOCK), :] += _dot_tn(jnp.concatenate(weights, axis=0),
                                                     jnp.concatenate([t[rows] for t in doh], axis=0))
            return out

        part = block_a(2 * qi + 1, [jnp.zeros((Q_TILE, 1), F32)] * HEAD_GROUP, second, True)
        carries = tuple(jnp.concatenate([zero_col, c], axis=0) for c in part)
        carries = tuple(block_a(2 * qi, carries, everything, True))
        lax.fori_loop(1, 2 * qi + 1, lambda jj, cs: tuple(block_a(2 * qi - jj, cs, everything, False)), carries)

        def block_b(kb, prefixes, rows, masked):
            ks = pl.multiple_of(kb * Q_BLOCK, Q_BLOCK)
            kstack = jnp.concatenate(_split_heads(_key_block(k_ref, kb), head), axis=0)
            dls, out = [], []
            for h in range(HEAD_GROUP):
                g = g_s[h, kb, rows, :]
                beta = beta_s[h, kb, rows, :].astype(F32)
                incl = _sum_along_keys(g, tri_prefix)
                before = incl - g + prefixes[h][rows]
                dl = g - beta * (g + before)
                if masked:
                    dl = jnp.where(diag[:Q_TILE - rows.start], dl, 0.0)
                dls.append(dl.astype(BF16))
                out.append(prefixes[h][rows] + incl[:, Q_BLOCK - 1:Q_BLOCK])
            dq_acc[rows, :] += _dot(jnp.concatenate(dls, axis=1), kstack)
            dk_acc[pl.ds(ks, Q_BLOCK), :] += _dot_tn(jnp.concatenate(dls, axis=0),
                                                     jnp.concatenate([t[rows] for t in qh], axis=0))
            return out

        zeros = (jnp.zeros((Q_TILE, 1), F32),) * HEAD_GROUP
        prefixes = lax.fori_loop(0, 2 * qi, lambda kb, ps: tuple(block_b(kb, ps, everything, False)), zeros)
        prefixes = block_b(2 * qi, prefixes, everything, True)
        block_b(2 * qi + 1, prefixes, second, True)
        dq_ref[...] = (dq_acc[...] * Q_SCALE).astype(BF16)

        @pl.when(qi == nq - 1)
        def _():
            dk_ref[...] = dk_acc[...].astype(BF16)
            dv_ref[...] = dv_acc[...].astype(BF16)

    t = nb * seq
    qspec = pl.BlockSpec((Q_TILE, GROUP_W), lambda b, g, i: (b * nq + i, g))
    kvout = pl.BlockSpec((seq, GROUP_W), lambda b, g, i: (b, g))
    return pl.pallas_call(
        body, grid=(nb, N_GROUPS, nq),
        in_specs=[qspec,
                  pl.BlockSpec((seq, GROUP_W), lambda b, g, i: (b, N_GROUPS + g)),
                  pl.BlockSpec((seq, GROUP_W), lambda b, g, i: (b, 2 * N_GROUPS + g)),
                  qspec],
        out_specs=[qspec, kvout, kvout],
        out_shape=[jax.ShapeDtypeStruct((t, D_MODEL), BF16)] * 3,
        scratch_shapes=[pltpu.VMEM((HEAD_GROUP, n_kb, Q_TILE, Q_BLOCK), F32),
                        pltpu.VMEM((HEAD_GROUP, n_kb, Q_TILE, Q_BLOCK), BF16),
                        pltpu.VMEM((Q_TILE, GROUP_W), F32),
                        pltpu.VMEM((seq, GROUP_W), F32), pltpu.VMEM((seq, GROUP_W), F32)],
        compiler_params=_params("parallel", "parallel", "arbitrary"), name="attn_bwd")(qkv, qkv, qkv, do)


def _conv_block_rows(seq):
    return min(256, seq)


SUBLANES = 8
CONV_SUB = 32
CONV_GROUPS = CONV_SUB // SUBLANES


def _fill_shifts(sh_ref, n):
    for r in range(1, SUBLANES):
        sh_ref[r, 0:n, :] = sh_ref[0, pl.ds(r, n), :]


def _shifted(sh_ref, start, cs):
    r = start % SUBLANES
    return sh_ref[r, start - r:start - r + CONV_SUB, cs].reshape(CONV_GROUPS, SUBLANES, -1)


def _tap(w_ref, k, cs):
    row = w_ref[k:k + 1, cs]
    return jnp.broadcast_to(row, (SUBLANES, row.shape[1]))[None]


def _tree_sum(x):
    parts = [x[i] for i in range(x.shape[0])]
    while len(parts) > 1:
        parts = [parts[i] + parts[i + 1] for i in range(0, len(parts), 2)]
    return parts[0]


def _conv_fwd(hrest, conv_w, conv_b, ln_g, ln_b, nb, seq):
    ts = _conv_block_rows(seq)
    nblk = seq // ts
    lanes = 128

    def body(cv_ref, cg_ref, cvh_ref, cgh_ref, z_ref, w_ref, cb_ref, g_ref, b_ref, u1_ref, cvin_ref, ush):
        i = pl.program_id(1)
        halo = cvh_ref[...] * _sigmoid(cgh_ref[...])
        ush[0, 0:HALO, :] = jnp.where(i > 0, halo, 0.0)
        ush[0, HALO:HALO + ts, :] = cv_ref[...] * _sigmoid(cg_ref[...])
        _fill_shifts(ush, HALO + ts - SUBLANES)
        for cc in range(D_MODEL // lanes):
            cs = slice(cc * lanes, (cc + 1) * lanes)
            for r0 in range(0, ts, CONV_SUB):
                acc = jnp.zeros((CONV_GROUPS, SUBLANES, lanes), F32) + _tap(cb_ref, 0, cs)
                for k in range(CONV_K):
                    acc = acc + _tap(w_ref, k, cs) * _shifted(ush, HALO - CONV_K + 1 + k + r0, cs)
                u1_ref[r0:r0 + CONV_SUB, cs] = acc.reshape(CONV_SUB, lanes)
        xhat, _ = _ln_stats(u1_ref[...])
        u2 = xhat * g_ref[...] + b_ref[...]
        z = z_ref[...]
        cvin_ref[...] = (u2 * _sigmoid(u2) * z * _sigmoid(z)).astype(BF16)

    def main(colblk):
        return pl.BlockSpec((ts, D_MODEL), lambda b, i: (b * nblk + i, colblk))

    def halo(colblk):
        return pl.BlockSpec((HALO, D_MODEL),
                            lambda b, i: (jnp.maximum((b * seq + i * ts) // HALO - 1, 0), colblk))

    vec = pl.BlockSpec((1, D_MODEL), lambda b, i: (0, 0))
    t = nb * seq
    return pl.pallas_call(
        body, grid=(nb, nblk),
        in_specs=[main(1), main(2), halo(1), halo(2), main(3),
                  pl.BlockSpec((CONV_ROWS, D_MODEL), lambda b, i: (0, 0)), vec, vec, vec],
        out_specs=[main(0), main(0)],
        out_shape=[jax.ShapeDtypeStruct((t, D_MODEL), F32), jax.ShapeDtypeStruct((t, D_MODEL), BF16)],
        scratch_shapes=[pltpu.VMEM((SUBLANES, HALO + ts, D_MODEL), F32)],
        compiler_params=_params("parallel", "parallel"), name="conv_fwd")(
            hrest, hrest, hrest, hrest, hrest, conv_w, conv_b, ln_g, ln_b)


def _tail(o, hrest, cvin, h0, tgt, w_sb, w_cv, w_out, ln_g, ln_b):
    t = o.shape[0]
    tm = min(256, t)

    def body(o_ref, z_ref, gs_ref, gc_ref, cvin_ref, h0_ref, tgt_ref, wsb_ref, wcv_ref, wout_ref, g_ref, b_ref,
             dr_ref, drb_ref, a_ref, mg_ref, dysb_ref, dycv_ref, do_ref, dz_ref, dgs_ref, dgc_ref, dcvin_ref,
             st_ref):
        @pl.when(pl.program_id(0) == 0)
        def _():
            st_ref[...] = jnp.zeros_like(st_ref)

        o = o_ref[...]
        z = z_ref[...]
        sz = _sigmoid(z)
        a = (o * z * sz).astype(BF16)
        a_ref[...] = a
        y_sb = _dot(a, wsb_ref[...])
        y_cv = _dot(cvin_ref[...], wcv_ref[...])
        s_sb = _sigmoid(gs_ref[...])
        s_cv = _sigmoid(gc_ref[...])
        merged = (s_sb * y_sb + s_cv * y_cv).astype(BF16)
        mg_ref[...] = merged
        r = ALPHA * h0_ref[...] + _dot(merged, wout_ref[...])
        xhat, rstd = _ln_stats(r)
        g = g_ref[...]
        err = xhat * g + b_ref[...] - tgt_ref[...]
        dy = err * (1.0 / D_MODEL)
        st_ref[0:1, :] += jnp.sum(dy * xhat, axis=0, keepdims=True)
        st_ref[1:2, :] += jnp.sum(dy, axis=0, keepdims=True)
        st_ref[2:3, :] += (0.5 / D_MODEL) * jnp.sum(err * err, axis=0, keepdims=True)
        dr = _ln_bwd(dy, xhat, rstd, g)
        dr_ref[...] = dr
        drb = dr.astype(BF16)
        drb_ref[...] = drb
        dm = _dot_nt(drb, wout_ref[...])
        dy_sb = (dm * s_sb).astype(BF16)
        dy_cv = (dm * s_cv).astype(BF16)
        dysb_ref[...] = dy_sb
        dycv_ref[...] = dy_cv
        dgs_ref[...] = (dm * y_sb * s_sb * (1.0 - s_sb)).astype(BF16)
        dgc_ref[...] = (dm * y_cv * s_cv * (1.0 - s_cv)).astype(BF16)
        da = _dot_nt(dy_sb, wsb_ref[...])
        dcvin_ref[...] = _dot_nt(dy_cv, wcv_ref[...])
        do_ref[...] = (da * z * sz).astype(BF16)
        dz_ref[...] = (da * o * sz * (1.0 + z * (1.0 - sz))).astype(BF16)

    def tok(colblk=0):
        return pl.BlockSpec((tm, D_MODEL), lambda i: (i, colblk))

    wspec = pl.BlockSpec((D_MODEL, D_MODEL), lambda i: (0, 0), pipeline_mode=pl.Buffered(1))
    vec = pl.BlockSpec((1, D_MODEL), lambda i: (0, 0))
    bf = jax.ShapeDtypeStruct((t, D_MODEL), BF16)
    f32 = jax.ShapeDtypeStruct((t, D_MODEL), F32)
    return pl.pallas_call(
        body, grid=(t // tm,),
        in_specs=[tok(), tok(0), tok(4), tok(5), tok(), tok(), tok(), wspec, wspec, wspec, vec, vec],
        out_specs=[tok()] * 11 + [pl.BlockSpec((8, D_MODEL), lambda i: (0, 0))],
        out_shape=[f32] + [bf] * 9 + [f32, jax.ShapeDtypeStruct((8, D_MODEL), F32)],
        compiler_params=_params("arbitrary"), name="tail")(
            o, hrest, hrest, hrest, cvin, h0, tgt, w_sb, w_cv, w_out, ln_g, ln_b)


def _conv_branch_bwd(dcvin, u1, hrest, ln_g, ln_b):
    t = u1.shape[0]
    tm = min(512, t)

    def body(dc_ref, u1_ref, z_ref, g_ref, b_ref, du1_ref, dz_ref, st_ref):
        @pl.when(pl.program_id(0) == 0)
        def _():
            st_ref[...] = jnp.zeros_like(st_ref)

        xhat, rstd = _ln_stats(u1_ref[...])
        g = g_ref[...]
        u2 = xhat * g + b_ref[...]
        s2 = _sigmoid(u2)
        z = z_ref[...]
        sz = _sigmoid(z)
        dc = dc_ref[...]
        dz_ref[...] = (dc * (u2 * s2) * sz * (1.0 + z * (1.0 - sz))).astype(BF16)
        du2 = dc * (z * sz) * s2 * (1.0 + u2 * (1.0 - s2))
        st_ref[0:1, :] += jnp.sum(du2 * xhat, axis=0, keepdims=True)
        st_ref[1:2, :] += jnp.sum(du2, axis=0, keepdims=True)
        du1 = _ln_bwd(du2, xhat, rstd, g)
        du1_ref[...] = du1
        st_ref[2:3, :] += jnp.sum(du1, axis=0, keepdims=True)

    tok = pl.BlockSpec((tm, D_MODEL), lambda i: (i, 0))
    vec = pl.BlockSpec((1, D_MODEL), lambda i: (0, 0))
    return pl.pallas_call(
        body, grid=(t // tm,),
        in_specs=[tok, tok, pl.BlockSpec((tm, D_MODEL), lambda i: (i, 3)), vec, vec],
        out_specs=[tok, tok, pl.BlockSpec((8, D_MODEL), lambda i: (0, 0))],
        out_shape=[jax.ShapeDtypeStruct((t, D_MODEL), F32), jax.ShapeDtypeStruct((t, D_MODEL), BF16),
                   jax.ShapeDtypeStruct((8, D_MODEL), F32)],
        compiler_params=_params("arbitrary"), name="conv_branch_bwd")(dcvin, u1, hrest, ln_g, ln_b)


def _conv_bwd(du1, hrest, conv_w, nb, seq):
    ts = _conv_block_rows(seq)
    nblk = seq // ts
    lanes = 128
    last_halo = nb * seq // HALO - 1

    def body(du_ref, duh_ref, cv_ref, cg_ref, cvh_ref, cgh_ref, w_ref, dcv_ref, dcg_ref, dw_ref, ush, dsh, dw_part):
        b = pl.program_id(0)
        i = pl.program_id(1)

        @pl.when((b == 0) & (i == 0))
        def _():
            dw_part[...] = jnp.zeros_like(dw_part)

        cv = cv_ref[...]
        sg = _sigmoid(cg_ref[...])
        halo = cvh_ref[...] * _sigmoid(cgh_ref[...])
        ush[0, 0:HALO, :] = jnp.where(i > 0, halo, 0.0)
        ush[0, HALO:HALO + ts, :] = cv * sg
        dsh[0, 0:ts, :] = du_ref[...]
        dsh[0, ts:ts + HALO, :] = jnp.where(i < nblk - 1, duh_ref[...], 0.0)
        _fill_shifts(ush, HALO + ts - SUBLANES)
        _fill_shifts(dsh, HALO + ts - SUBLANES)
        for cc in range(D_MODEL // lanes):
            cs = slice(cc * lanes, (cc + 1) * lanes)
            for r0 in range(0, ts, CONV_SUB):
                rows = slice(r0, r0 + CONV_SUB)
                dmain = _shifted(dsh, r0, cs)
                acc = jnp.zeros((CONV_GROUPS, SUBLANES, lanes), F32)
                for k in range(CONV_K):
                    acc = acc + _tap(w_ref, k, cs) * _shifted(dsh, CONV_K - 1 - k + r0, cs)
                    dw_part[k, :, cs] += _tree_sum(dmain * _shifted(ush, HALO - CONV_K + 1 + k + r0, cs))
                acc = acc.reshape(CONV_SUB, lanes)
                dcv_ref[rows, cs] = (acc * sg[rows, cs]).astype(BF16)
                dcg_ref[rows, cs] = (acc * cv[rows, cs] * sg[rows, cs] * (1.0 - sg[rows, cs])).astype(BF16)

        @pl.when((b == nb - 1) & (i == nblk - 1))
        def _():
            for k in range(CONV_ROWS):
                dw_ref[k:k + 1, :] = jnp.sum(dw_part[k], axis=0, keepdims=True)

    def main(colblk):
        return pl.BlockSpec((ts, D_MODEL), lambda b, i: (b * nblk + i, colblk))

    def halo_before(colblk):
        return pl.BlockSpec((HALO, D_MODEL),
                            lambda b, i: (jnp.maximum((b * seq + i * ts) // HALO - 1, 0), colblk))

    halo_after = pl.BlockSpec((HALO, D_MODEL),
                              lambda b, i: (jnp.minimum((b * seq + (i + 1) * ts) // HALO, last_halo), 0))
    t = nb * seq
    return pl.pallas_call(
        body, grid=(nb, nblk),
        in_specs=[main(0), halo_after, main(1), main(2), halo_before(1), halo_before(2),
                  pl.BlockSpec((CONV_ROWS, D_MODEL), lambda b, i: (0, 0))],
        out_specs=[main(0), main(0), pl.BlockSpec((CONV_ROWS, D_MODEL), lambda b, i: (0, 0))],
        out_shape=[jax.ShapeDtypeStruct((t, D_MODEL), BF16), jax.ShapeDtypeStruct((t, D_MODEL), BF16),
                   jax.ShapeDtypeStruct((CONV_ROWS, D_MODEL), F32)],
        scratch_shapes=[pltpu.VMEM((SUBLANES, HALO + ts, D_MODEL), F32), pltpu.VMEM((SUBLANES, ts + HALO, D_MODEL), F32),
                        pltpu.VMEM((CONV_ROWS, SUBLANES, D_MODEL), F32)],
        compiler_params=_params("arbitrary", "arbitrary"), name="conv_bwd")(
            du1, du1, hrest, hrest, hrest, hrest, conv_w)


def _weight_grad(x, dys, name):
    n = len(dys)
    t = x.shape[0]
    tm = min(512, t)

    def body(x_ref, *refs):
        o_ref = refs[n]
        j = pl.program_id(0)

        @pl.when(pl.program_id(1) == 0)
        def _():
            o_ref[...] = jnp.zeros_like(o_ref)

        for p in range(n):
            @pl.when(j == p)
            def _(p=p):
                o_ref[...] += _dot_tn(x_ref[...], refs[p][...])

    def dy_spec(p):
        return pl.BlockSpec((tm, D_MODEL), lambda j, i: (jnp.where(j == p, i, 0), 0))

    return pl.pallas_call(
        body, grid=(n, t // tm),
        in_specs=[pl.BlockSpec((tm, D_MODEL), lambda j, i: (i, 0))] + [dy_spec(p) for p in range(n)],
        out_specs=pl.BlockSpec((D_MODEL, D_MODEL), lambda j, i: (0, j)),
        out_shape=jax.ShapeDtypeStruct((D_MODEL, n * D_MODEL), F32),
        compiler_params=_params("parallel", "arbitrary"), name=name)(x, *dys)


def _in_proj_bwd(dhs, w_in, dr, x, ln_g, pair):
    n = len(dhs)
    t = x.shape[0]
    tm = min(512, t)
    n_i = t // tm

    def body(*refs):
        dh_refs = refs[:n]
        w_ref, dr_ref, x_ref, g_ref = refs[n:n + 4]
        pair_refs = refs[n + 4:n + 8]
        gx_ref, st_ref = refs[n + 8:n + 10]
        got_refs = refs[n + 10:n + 14]
        acc, send_sems, recv_sems = refs[n + 14:]
        i = pl.program_id(0)
        p = pl.program_id(1)
        pieces, wholes = _chip_exchange_copies(pair_refs, got_refs, send_sems, recv_sems)

        @pl.when((i == 0) & (p == 0))
        def _():
            st_ref[...] = jnp.zeros_like(st_ref)
            for cp in pieces:
                cp.start()

        @pl.when((i == n_i - 1) & (p == n - 1))
        def _():
            for cp in wholes:
                cp.wait_recv()
            for cp in wholes:
                cp.wait_send()

        @pl.when(p == 0)
        def _():
            acc[...] = ALPHA * dr_ref[...]

        for q in range(n):
            @pl.when(p == q)
            def _(q=q):
                acc[...] += _dot_nt(dh_refs[q][...], w_ref[...])

        @pl.when(p == n - 1)
        def _():
            xhat, rstd = _ln_stats(x_ref[...])
            dh0 = acc[...]
            st_ref[0:1, :] += jnp.sum(dh0 * xhat, axis=0, keepdims=True)
            st_ref[1:2, :] += jnp.sum(dh0, axis=0, keepdims=True)
            gx_ref[...] = _ln_bwd(dh0, xhat, rstd, g_ref[...])

    tok = pl.BlockSpec((tm, D_MODEL), lambda i, p: (i, 0))
    got = [jax.ShapeDtypeStruct((3, HALF, SHARD_IN), BF16)] + [jax.ShapeDtypeStruct((3, SHARD_SQ, HALF), BF16)] * 3
    res = pl.pallas_call(
        body, grid=(n_i, n),
        in_specs=[tok] * n + [pl.BlockSpec((D_MODEL, D_MODEL), lambda i, p: (0, p)), tok, tok,
                              pl.BlockSpec((1, D_MODEL), lambda i, p: (0, 0))] + [ANY] * 4,
        out_specs=[tok, pl.BlockSpec((8, D_MODEL), lambda i, p: (0, 0))] + [ANY] * 4,
        out_shape=[jax.ShapeDtypeStruct((t, D_MODEL), F32), jax.ShapeDtypeStruct((8, D_MODEL), F32)] + got,
        scratch_shapes=[pltpu.VMEM((tm, D_MODEL), F32), pltpu.SemaphoreType.DMA((12,)), pltpu.SemaphoreType.DMA((12,))],
        compiler_params=_params("arbitrary", "arbitrary"), name="in_proj_bwd")(*dhs, w_in, dr, x, ln_g, *pair)
    return res[0], res[1], res[2:]


def _forward_backward(x, tgt, nb, seq, shards, vecs):
    w_in_early = _gather_first(shards[0])
    h0, h0b = _ln_in(x, vecs["ln_in_g"], vecs["ln_in_b"])
    qkv = _in_proj(h0b, w_in_early, 0, 3, BF16, True, "in_proj_qkv")
    o, (w_in, w_sb, w_cv, w_out, conv_w) = _attn_fwd(qkv, nb, seq, w_in_early, shards)
    hrest = _in_proj(h0b, w_in, 3, 6, F32, False, "in_proj_rest")
    u1, cvin = _conv_fwd(hrest, conv_w, vecs["conv_b"], vecs["conv_ln_g"], vecs["conv_ln_b"], nb, seq)
    (dr, drb, a, merged, dy_sb, dy_cv, do, dz_sb, dg_sb, dg_cv, dcvin, st_tail) = _tail(
        o, hrest, cvin, h0, tgt, w_sb, w_cv, w_out, vecs["ln_post_g"], vecs["ln_post_b"])
    d_w_sb = _weight_grad(a, [dy_sb], "grad_w_sb")
    d_w_cv = _weight_grad(cvin, [dy_cv], "grad_w_cv")
    d_w_out = _weight_grad(merged, [drb], "grad_w_out")
    du1, dz_cv, st_conv = _conv_branch_bwd(dcvin, u1, hrest, vecs["conv_ln_g"], vecs["conv_ln_b"])
    dc_val, dc_gate, d_conv_w = _conv_bwd(du1, hrest, conv_w, nb, seq)
    dq, dk, dv = _attn_bwd(qkv, do, nb, seq)
    dhs = [dq, dk, dv, dz_sb, dc_val, dc_gate, dz_cv, dg_sb, dg_cv]
    d_w_in = _weight_grad(h0b, dhs, "grad_w_in")
    small = jnp.concatenate([st_conv[2:3], st_conv[0:2], st_tail[0:3], d_conv_w], axis=0)
    return w_in, dhs, dr, (d_w_in, d_w_sb, d_w_cv, d_w_out), small


def _place():
    x, y, c = lax.axis_index("x"), lax.axis_index("y"), lax.axis_index("c")
    chips = [(1 - x, y), (x, 1 - y), (1 - x, 1 - y)]
    return x, y, c, chips


N_EARLY = -(-3 * D_MODEL // SHARD_IN)
GATHER_PARTS = 4


def _w_in_slot(full, chip):
    return full.at[:, pl.ds(pl.multiple_of(chip * SHARD_IN, 128), SHARD_IN)]


def _gather_first(w_in):
    part_rows = HALF // GATHER_PARTS
    n_sem = 3 * GATHER_PARTS

    def body(src, dst, ici_send, ici_recv, fwd_send, fwd_recv, local_sem):
        x, y, c, chips = _place()
        me = 2 * x + y
        sibling = (x, y, 1 - c)
        owner = [2 * px + py for px, py in chips]

        def part(ref, k, half):
            return ref.at[pl.ds(pl.multiple_of(half * HALF + k * part_rows, 16), part_rows), :]

        def held(j, k, half):
            return part(_w_in_slot(dst, owner[j]), k, half)

        def sent(j, k):
            px, py = chips[j]
            return _remote(part(src, k, c), part(_w_in_slot(dst, me), k, c), ici_send.at[j * GATHER_PARTS + k],
                           ici_recv.at[j * GATHER_PARTS + k], (px, py, c))

        def landed(j, k):
            px, py = chips[j]
            return _remote(held(j, k, c), held(j, k, c), ici_send.at[j * GATHER_PARTS + k],
                           ici_recv.at[j * GATHER_PARTS + k], (px, py, c))

        def passed(j, k, half):
            return _remote(held(j, k, half), held(j, k, half), fwd_send.at[j * GATHER_PARTS + k],
                           fwd_recv.at[j * GATHER_PARTS + k], sibling)

        own = _w_in_slot(dst, me)
        rows = D_MODEL // N_CHUNKS
        for k in range(N_CHUNKS):
            pltpu.make_async_copy(src.at[pl.ds(k * rows, rows)], own.at[pl.ds(k * rows, rows)], local_sem).start()
        local = pltpu.make_async_copy(src, own, local_sem)

        @pl.when(me < N_EARLY)
        def _():
            for j in range(3):
                for k in range(GATHER_PARTS):
                    sent(j, k).start()

        for j in range(3):
            @pl.when(owner[j] < N_EARLY)
            def _(j=j):
                for k in range(GATHER_PARTS):
                    landed(j, k).wait_recv()
                    passed(j, k, c).start()

        for j in range(3):
            @pl.when(owner[j] < N_EARLY)
            def _(j=j):
                for k in range(GATHER_PARTS):
                    passed(j, k, 1 - c).wait_recv()
                for k in range(GATHER_PARTS):
                    passed(j, k, c).wait_send()

        @pl.when(me < N_EARLY)
        def _():
            for j in range(3):
                for k in range(GATHER_PARTS):
                    sent(j, k).wait_send()

        local.wait()

    return pl.pallas_call(
        body, in_specs=[ANY], out_specs=ANY, out_shape=jax.ShapeDtypeStruct((D_MODEL, IN_WIDTH), BF16),
        scratch_shapes=[pltpu.SemaphoreType.DMA((n_sem,)), pltpu.SemaphoreType.DMA((n_sem,)),
                        pltpu.SemaphoreType.DMA((n_sem,)), pltpu.SemaphoreType.DMA((n_sem,)),
                        pltpu.SemaphoreType.DMA],
        name="gather_first")(w_in)


def _late_gather(srcs, dsts, send_sems, recv_sems, local_sems):
    x, y, c, chips = _place()
    me = 2 * x + y
    small = range(1, 5)

    def slot(a, chip):
        if a == 0:
            return _w_in_slot(dsts[0], chip)
        if a == 4:
            return dsts[a].at[:, pl.ds(pl.multiple_of(chip * SHARD_SQ, 128), SHARD_SQ)]
        return dsts[a].at[pl.ds(pl.multiple_of(chip * SHARD_SQ, 16), SHARD_SQ), :]

    def sems(a, j):
        return send_sems.at[3 * a + j], recv_sems.at[3 * a + j]

    def whole(a, j):
        px, py = chips[j]
        return _remote(srcs[a], slot(a, me), *sems(a, j), (px, py, c))

    def arrival(a, j):
        px, py = chips[j]
        return _arrival(slot(a, 2 * px + py), *sems(a, j), (px, py, c))

    def local(a):
        return pltpu.make_async_copy(srcs[a], slot(a, me), local_sems.at[a - 1])

    def start():
        for a in small:
            local(a).start()
            for j in range(3):
                whole(a, j).start()

        @pl.when(me >= N_EARLY)
        def _():
            for j, (px, py) in enumerate(chips):
                for cp in _chunk_copies(srcs[0], slot(0, me), GATHER_PARTS, *sems(0, j), (px, py, c)):
                    cp.start()

    def finish():
        for j, (px, py) in enumerate(chips):
            for a in small:
                arrival(a, j).wait_recv()

            @pl.when(2 * px + py >= N_EARLY)
            def _(j=j):
                arrival(0, j).wait_recv()

        for a in small:
            for j in range(3):
                whole(a, j).wait_send()
            local(a).wait()

        @pl.when(me >= N_EARLY)
        def _():
            for j in range(3):
                whole(0, j).wait_send()

    return start, finish


def _remote(src, dst, send_sem, recv_sem, device):
    return pltpu.make_async_remote_copy(src_ref=src, dst_ref=dst, send_sem=send_sem, recv_sem=recv_sem,
                                        device_id=device, device_id_type=MESH)


def _chunk_copies(src, dst, n, send_sem, recv_sem, device):
    rows = src.shape[0] // n
    return [_remote(src.at[pl.ds(k * rows, rows)], dst.at[pl.ds(k * rows, rows)], send_sem, recv_sem, device)
            for k in range(n)]


def _row_chunks(src, dst, n, send_sem, recv_sem, device):
    return _chunk_copies(src, dst, n, send_sem, recv_sem, device), _remote(src, dst, send_sem, recv_sem, device)


def _arrival(dst, send_sem, recv_sem, device):
    return _remote(dst, dst, send_sem, recv_sem, device)


def _pair_exchange(d_in, d_sb, d_cv, d_out):
    def body(i0, i1, i2, i3, r0, r1, r2, r3, send_sems, recv_sems):
        x, y, c, _ = _place()
        srcs = [i0, i1, i2, i3]
        theirs = [r0, r1, r2, r3]
        sibling = (x, y, 1 - c)
        start = pl.multiple_of((1 - c) * HALF, 128)
        views = [srcs[0].at[pl.ds(start, HALF), :]] + [srcs[a].at[:, pl.ds(start, HALF)] for a in range(1, 4)]
        pieces, wholes = [], []
        for a in range(4):
            cps, whole = _row_chunks(views[a], theirs[a], N_CHUNKS, send_sems.at[a], recv_sems.at[a], sibling)
            pieces += cps
            wholes.append(whole)
        for cp in pieces:
            cp.start()
        for cp in wholes:
            cp.wait_recv()
        for cp in wholes:
            cp.wait_send()

    shapes = [jax.ShapeDtypeStruct((HALF, IN_WIDTH), F32)] + [jax.ShapeDtypeStruct((D_MODEL, HALF), F32)] * 3
    return pl.pallas_call(
        body, in_specs=[ANY] * 4, out_specs=[ANY] * 4, out_shape=shapes,
        scratch_shapes=[pltpu.SemaphoreType.DMA((4,)), pltpu.SemaphoreType.DMA((4,))],
        name="pair_exchange")(d_in, d_sb, d_cv, d_out)


def _chip_exchange_copies(srcs, theirs, send_sems, recv_sems):
    _, _, c, chips = _place()

    def shard(a, px, py):
        chip = 2 * px + py
        if a == 0:
            return srcs[a].at[:, pl.ds(pl.multiple_of(chip * SHARD_IN, 128), SHARD_IN)]
        return srcs[a].at[pl.ds(pl.multiple_of(chip * SHARD_SQ, 16), SHARD_SQ), :]

    pieces, wholes = [], []
    for j, (px, py) in enumerate(chips):
        for a in range(4):
            cps, whole = _row_chunks(shard(a, px, py), theirs[a].at[j], 4, send_sems.at[3 * a + j],
                                     recv_sems.at[3 * a + j], (px, py, c))
            pieces += cps
            wholes.append(whole)
    return pieces, wholes


def _pair_share(f_in, f_sb, f_cv, f_out):
    def body(i0, i1, i2, i3, o0, o1, o2, o3, send_sems, recv_sems):
        del i0, i1, i2, i3
        x, y, c, _ = _place()
        dsts = [o0, o1, o2, o3]
        sibling = (x, y, 1 - c)

        def half(a, which):
            start = pl.multiple_of(which * HALF, 128)
            if a == 0:
                return dsts[a].at[pl.ds(start, HALF), :]
            return dsts[a].at[:, pl.ds(start, HALF)]

        pieces, sent, arrived = [], [], []
        for a in range(4):
            cps, whole = _row_chunks(half(a, c), half(a, c), N_CHUNKS, send_sems.at[a], recv_sems.at[a], sibling)
            pieces += cps
            sent.append(whole)
            arrived.append(_arrival(half(a, 1 - c), send_sems.at[a], recv_sems.at[a], sibling))
        for cp in pieces:
            cp.start()
        for cp in arrived:
            cp.wait_recv()
        for cp in sent:
            cp.wait_send()

    return pl.pallas_call(
        body, in_specs=[ANY] * 4, out_specs=[ANY] * 4,
        out_shape=[jax.ShapeDtypeStruct((D_MODEL, SHARD_IN), F32)] + [jax.ShapeDtypeStruct((SHARD_SQ, D_MODEL), F32)] * 3,
        input_output_aliases={0: 0, 1: 1, 2: 2, 3: 3},
        scratch_shapes=[pltpu.SemaphoreType.DMA((4,)), pltpu.SemaphoreType.DMA((4,))],
        name="pair_share")(f_in, f_sb, f_cv, f_out)


def _small_allreduce(small):
    def body(s_ref, o_ref, slots, send_sems, recv_sems):
        x, y, c, _ = _place()
        me = 4 * x + 2 * y + c
        slots[me] = s_ref[...]
        sends = []
        for k in range(1, N_DEV):
            px, py, pc = (x + (k >> 2)) % 2, (y + ((k >> 1) & 1)) % 2, (c + (k & 1)) % 2
            sends.append(pltpu.make_async_remote_copy(
                src_ref=s_ref, dst_ref=slots.at[me], send_sem=send_sems.at[k - 1], recv_sem=recv_sems.at[k - 1],
                device_id=(px, py, pc), device_id_type=MESH))
        for cp in sends:
            cp.start()
        for k in range(1, N_DEV):
            px, py, pc = (x + (k >> 2)) % 2, (y + ((k >> 1) & 1)) % 2, (c + (k & 1)) % 2
            pltpu.make_async_remote_copy(
                src_ref=s_ref, dst_ref=slots.at[4 * px + 2 * py + pc], send_sem=send_sems.at[k - 1],
                recv_sem=recv_sems.at[k - 1], device_id=(px, py, pc), device_id_type=MESH).wait_recv()
        for cp in sends:
            cp.wait_send()
        total = slots[0]
        for d in range(1, N_DEV):
            total = total + slots[d]
        o_ref[...] = total
        o_ref[7:8, :] = jnp.zeros((1, D_MODEL), F32) + jnp.sum(total[7:8, :], axis=1, keepdims=True)

    return pl.pallas_call(
        body, in_specs=[pl.BlockSpec(memory_space=pltpu.VMEM)], out_specs=pl.BlockSpec(memory_space=pltpu.VMEM),
        out_shape=jax.ShapeDtypeStruct((SMALL_ROWS, D_MODEL), F32),
        scratch_shapes=[pltpu.VMEM((N_DEV, SMALL_ROWS, D_MODEL), F32), pltpu.SemaphoreType.DMA((N_DEV - 1,)),
                        pltpu.SemaphoreType.DMA((N_DEV - 1,))],
        name="small_allreduce")(small)


SUM_BLOCKS = 8


def _pair_sum(place, full, theirs, by_rows, name):
    rows, cols = theirs.shape
    tr = rows // SUM_BLOCKS

    def body(p_ref, a_ref, b_ref, o_ref):
        del p_ref
        o_ref[...] = (a_ref[...] + b_ref[...]).astype(BF16)

    mine = (lambda i, p: (p[1] * SUM_BLOCKS + i, 0)) if by_rows else (lambda i, p: (i, p[1]))
    spec = pl.BlockSpec((tr, cols), lambda i, p: (i, 0))
    return pl.pallas_call(
        body, out_shape=jax.ShapeDtypeStruct(theirs.shape, BF16),
        grid_spec=pltpu.PrefetchScalarGridSpec(num_scalar_prefetch=1, grid=(SUM_BLOCKS,),
                                               in_specs=[pl.BlockSpec((tr, cols), mine), spec], out_specs=spec),
        compiler_params=_params("parallel"), name=name)(place, full, theirs)


def _chip_sum(place, pair, got, by_rows, name):
    _, rows, cols = got.shape
    tr = rows // SUM_BLOCKS

    def body(p_ref, a_ref, g_ref, o_ref):
        del p_ref
        o_ref[...] = ((a_ref[...].astype(F32) + g_ref[0].astype(F32)) + g_ref[1].astype(F32)) + g_ref[2].astype(F32)

    if by_rows:
        own = lambda i, p: (i, p[0])
        out = lambda i, p: (p[1] * SUM_BLOCKS + i, 0)
        full = (2 * rows, cols)
    else:
        own = lambda i, p: (p[0] * SUM_BLOCKS + i, 0)
        out = lambda i, p: (i, p[1])
        full = (rows, 2 * cols)
    return pl.pallas_call(
        body, out_shape=jax.ShapeDtypeStruct(full, F32),
        grid_spec=pltpu.PrefetchScalarGridSpec(
            num_scalar_prefetch=1, grid=(SUM_BLOCKS,),
            in_specs=[pl.BlockSpec((tr, cols), own), pl.BlockSpec((3, tr, cols), lambda i, p: (0, i, 0))],
            out_specs=pl.BlockSpec((tr, cols), out)),
        compiler_params=_params("parallel"), name=name)(place, pair, got)


def _adamw(w, g, m, v, name):
    rows, cols = w.shape
    tr = rows // 4 if rows % 32 == 0 else rows
    c1 = 1.0 - ADAM_B1 ** ADAM_STEP
    c2 = 1.0 - ADAM_B2 ** ADAM_STEP

    def body(w_ref, g_ref, m_ref, v_ref, d_ref, mo_ref, vo_ref):
        g = g_ref[...]
        mn = ADAM_B1 * m_ref[...] + (1.0 - ADAM_B1) * g
        vn = ADAM_B2 * v_ref[...] + (1.0 - ADAM_B2) * (g * g)
        mo_ref[...] = mn
        vo_ref[...] = vn
        d_ref[...] = -ADAM_LR * ((mn / c1) / (jnp.sqrt(vn / c2) + ADAM_EPS) + ADAM_WD * w_ref[...])

    spec = pl.BlockSpec((tr, cols), lambda i: (i, 0))
    shape = jax.ShapeDtypeStruct(w.shape, F32)
    return pl.pallas_call(body, grid=(rows // tr,), in_specs=[spec] * 4, out_specs=[spec] * 3,
                          out_shape=[shape] * 3, compiler_params=_params("parallel"), name=name)(w, g, m, v)


GRAD_NAMES = ("in", "sb", "cv", "out")


def _place_scalars():
    return jnp.stack([2 * lax.axis_index("x") + lax.axis_index("y"), lax.axis_index("c")]).astype(jnp.int32)


def _pair_sums(grads):
    place = _place_scalars()
    theirs = _pair_exchange(*grads)
    return [_pair_sum(place, grads[a], theirs[a], a == 0, "pair_sum_" + GRAD_NAMES[a]) for a in range(4)]


def _finish_weight_grads(pair, got):
    place = _place_scalars()
    done = [_chip_sum(place, pair[a], got[a], a == 0, "chip_sum_" + GRAD_NAMES[a]) for a in range(4)]
    return _pair_share(*done)


def kernel(x, ln_in_g, ln_in_b, w_in, w_sb_proj, conv_w, conv_b, conv_ln_g, conv_ln_b, w_cv_proj, w_out, ln_post_g, ln_post_b, loss_target, m_ln_in_g, m_ln_in_b, m_w_in, m_w_sb_proj, m_conv_w, m_conv_b, m_conv_ln_g, m_conv_ln_b, m_w_cv_proj, m_w_out, m_ln_post_g, m_ln_post_b, v_ln_in_g, v_ln_in_b, v_w_in, v_w_sb_proj, v_conv_w, v_conv_b, v_conv_ln_g, v_conv_ln_b, v_w_cv_proj, v_w_out, v_ln_post_g, v_ln_post_b):
    nb, seq, _ = x.shape
    t = nb * seq
    vec_names = ("ln_in_g", "ln_in_b", "conv_b", "conv_ln_g", "conv_ln_b", "ln_post_g", "ln_post_b")
    vec_w = dict(zip(vec_names, (ln_in_g, ln_in_b, conv_b, conv_ln_g, conv_ln_b, ln_post_g, ln_post_b)))
    vec_m = dict(zip(vec_names, (m_ln_in_g, m_ln_in_b, m_conv_b, m_conv_ln_g, m_conv_ln_b, m_ln_post_g, m_ln_post_b)))
    vec_v = dict(zip(vec_names, (v_ln_in_g, v_ln_in_b, v_conv_b, v_conv_ln_g, v_conv_ln_b, v_ln_post_g, v_ln_post_b)))
    vecs = {k: a.reshape(1, D_MODEL) for k, a in vec_w.items()}

    pad_taps = lambda a: jnp.pad(a.reshape(CONV_K, SHARD_SQ), ((0, CONV_ROWS - CONV_K), (0, 0)))
    shards = (w_in[0].astype(BF16), w_sb_proj[0].astype(BF16), w_cv_proj[0].astype(BF16), w_out[0].astype(BF16),
              pad_taps(conv_w))
    x2 = x.reshape(t, D_MODEL)
    full_in, dhs, dr, big, small = _forward_backward(x2, loss_target.reshape(t, D_MODEL), nb, seq, shards, vecs)
    pair = _pair_sums(big)
    grad_x, st_in, got = _in_proj_bwd(dhs, full_in, dr, x2, vecs["ln_in_g"], pair)
    g_in, g_sb, g_cv, g_out = _finish_weight_grads(pair, got)
    small = _small_allreduce(jnp.concatenate([st_in[0:2], small], axis=0))

    chip = 2 * lax.axis_index("x") + lax.axis_index("y")
    g_conv = lax.dynamic_slice(small, (8, chip * SHARD_SQ), (CONV_ROWS, SHARD_SQ))
    loss = small[7, 0]

    grads, deltas, new_m, new_v = {}, {}, {}, {}
    stack = lambda d: jnp.concatenate([d[k].reshape(1, D_MODEL) for k in vec_names] + [jnp.zeros((1, D_MODEL), F32)])
    vd, vm, vv = _adamw(stack(vec_w), small[0:8], stack(vec_m), stack(vec_v), "adamw_vectors")
    for r, k in enumerate(vec_names):
        shape = vec_w[k].shape
        grads[k] = small[r].reshape(shape)
        deltas[k], new_m[k], new_v[k] = vd[r].reshape(shape), vm[r].reshape(shape), vv[r].reshape(shape)

    big_w = {"w_in": (w_in, g_in, m_w_in, v_w_in), "w_sb_proj": (w_sb_proj, g_sb, m_w_sb_proj, v_w_sb_proj),
             "w_cv_proj": (w_cv_proj, g_cv, m_w_cv_proj, v_w_cv_proj), "w_out": (w_out, g_out, m_w_out, v_w_out)}
    for k, (w, g, m, v) in big_w.items():
        d, mn, vn = _adamw(w[0], g, m[0], v[0], "adamw_" + k)
        grads[k], deltas[k], new_m[k], new_v[k] = g[None], d[None], mn[None], vn[None]
    d, mn, vn = _adamw(pad_taps(conv_w), g_conv, pad_taps(m_conv_w), pad_taps(v_conv_w), "adamw_conv_w")
    grads["conv_w"] = g_conv[None, :CONV_K]
    deltas["conv_w"], new_m["conv_w"], new_v["conv_w"] = d[None, :CONV_K], mn[None, :CONV_K], vn[None, :CONV_K]

    order = ("ln_in_g", "ln_in_b", "w_in", "w_sb_proj", "conv_w", "conv_b", "conv_ln_g", "conv_ln_b",
             "w_cv_proj", "w_out", "ln_post_g", "ln_post_b")
    return (loss, grad_x.reshape(x.shape), *[grads[k] for k in order], *[deltas[k] for k in order],
            *[new_m[k] for k in order], *[new_v[k] for k in order])
```

```python
import functools

import jax
import jax.numpy as jnp
from jax import lax
from jax.experimental import pallas as pl
from jax.experimental.pallas import tpu as pltpu

F32 = jnp.float32
BF16 = jnp.bfloat16

D_MODEL = 1024
N_HEADS = 16
HEAD_DIM = 64
HEAD_GROUP = 4
GROUP_W = HEAD_GROUP * HEAD_DIM
N_GROUPS = N_HEADS // HEAD_GROUP
N_PIECES = 9
IN_WIDTH = N_PIECES * D_MODEL
Q_BLOCK = 256
Q_TILE = 2 * Q_BLOCK
CONV_K = 31
CONV_ROWS = 32
HALO = 32
LN_EPS = 1e-5
ALPHA = 2.0 ** 0.25
Q_SCALE = 0.125
N_CHIPS = 4
N_DEV = 8
SHARD_IN = IN_WIDTH // N_CHIPS
SHARD_SQ = D_MODEL // N_CHIPS
HALF = D_MODEL // 2
SMALL_ROWS = 40
N_CHUNKS = 16

ADAM_LR = 0.001
ADAM_B1 = 0.9
ADAM_B2 = 0.999
ADAM_EPS = 1e-08
ADAM_WD = 0.01
ADAM_STEP = 10

MESH = pl.DeviceIdType.MESH
ANY = pl.BlockSpec(memory_space=pl.ANY)
VMEM_LIMIT = 60 * 1024 * 1024

NT = (((1,), (1,)), ((), ()))
TN = (((0,), (0,)), ((), ()))


def _sigmoid(x):
    return 1.0 / (1.0 + jnp.exp(-x))


def _dot(a, b):
    return jnp.dot(a, b, preferred_element_type=F32)


def _sum_along_keys(x, tri_twice):
    return _dot(jnp.concatenate(_split_bf16(x), axis=1), tri_twice)


def _dot_nt(a, b):
    return lax.dot_general(a, b, NT, preferred_element_type=F32)


def _dot_tn(a, b):
    return lax.dot_general(a, b, TN, preferred_element_type=F32)


def _split_bf16(x):
    hi = x.astype(BF16)
    lo = (x - hi.astype(F32)).astype(BF16)
    return hi, lo


def _ln_stats(x):
    mu = jnp.mean(x, axis=-1, keepdims=True)
    xc = x - mu
    var = jnp.mean(xc * xc, axis=-1, keepdims=True)
    rstd = lax.rsqrt(var + LN_EPS)
    return xc * rstd, rstd


def _ln_bwd(dy, xhat, rstd, g):
    dxh = dy * g
    m1 = jnp.mean(dxh, axis=-1, keepdims=True)
    m2 = jnp.mean(dxh * xhat, axis=-1, keepdims=True)
    return rstd * (dxh - m1 - xhat * m2)


def _params(*sem):
    return pltpu.CompilerParams(dimension_semantics=sem, vmem_limit_bytes=VMEM_LIMIT)


def _ln_in(x, g, b):
    t = x.shape[0]
    tm = min(512, t)

    def body(x_ref, g_ref, b_ref, hf_ref, hb_ref):
        xhat, _ = _ln_stats(x_ref[...])
        y = xhat * g_ref[...] + b_ref[...]
        hf_ref[...] = y
        hb_ref[...] = y.astype(BF16)

    row = pl.BlockSpec((tm, D_MODEL), lambda i: (i, 0))
    vec = pl.BlockSpec((1, D_MODEL), lambda i: (0, 0))
    return pl.pallas_call(
        body, grid=(t // tm,), in_specs=[row, vec, vec], out_specs=[row, row],
        out_shape=[jax.ShapeDtypeStruct((t, D_MODEL), F32), jax.ShapeDtypeStruct((t, D_MODEL), BF16)],
        compiler_params=_params("parallel"), name="ln_in")(x, g, b)


def _in_proj(h, w, col0, ncol, out_dtype, scale_first, name):
    t = h.shape[0]
    tm = min(1024, t)

    def body(h_ref, w_ref, o_ref):
        res = _dot(h_ref[...], w_ref[...])
        if scale_first:
            res = res * jnp.where(pl.program_id(0) == 0, Q_SCALE, 1.0)
        o_ref[...] = res.astype(out_dtype)

    return pl.pallas_call(
        body, grid=(ncol, t // tm),
        in_specs=[pl.BlockSpec((tm, D_MODEL), lambda j, i: (i, 0)),
                  pl.BlockSpec((D_MODEL, D_MODEL), lambda j, i: (0, j + col0))],
        out_specs=pl.BlockSpec((tm, D_MODEL), lambda j, i: (i, j)),
        out_shape=jax.ShapeDtypeStruct((t, ncol * D_MODEL), out_dtype),
        compiler_params=_params("parallel", "parallel"), name=name)(h, w)


def _sb_block(l, diag, tri, carry):
    lnb = jnp.minimum(-l, 0.0) - jnp.log(1.0 + jnp.exp(-jnp.abs(l)))
    if diag is not None:
        lnb = jnp.where(diag, lnb, 0.0)
    suffix = _sum_along_keys(lnb, tri)
    a = jnp.exp(l + suffix + carry)
    if diag is not None:
        a = jnp.where(diag, a, 0.0)
    return a, lnb, suffix[:, 0:1]


def _attn_consts():
    head = lax.broadcasted_iota(jnp.int32, (Q_TILE, GROUP_W), 1) // HEAD_DIM
    row = lax.broadcasted_iota(jnp.int32, (Q_TILE, Q_BLOCK), 0)
    col = lax.broadcasted_iota(jnp.int32, (Q_TILE, Q_BLOCK), 1)
    key = row % Q_BLOCK
    return head, col < row, (key >= col).astype(BF16), (key <= col).astype(BF16)


def _split_heads(t, head):
    zero = jnp.zeros_like(t)
    head = head[:t.shape[0]]
    return [jnp.where(head == h, t, zero) for h in range(HEAD_GROUP)]


def _key_block(ref, kb):
    return ref[pl.ds(pl.multiple_of(kb * Q_BLOCK, Q_BLOCK), Q_BLOCK), :]


def _attn_fwd(qkv, nb, seq, w_in_early, shards):
    nq = seq // Q_TILE

    def body(q_ref, k_ref, v_ref, full_in, s0, s1, s2, s3, s4, o_ref, d0, d1, d2, d3, d4, send_sems, recv_sems, local_sems):
        del full_in
        qi = pl.program_id(2)
        first = (pl.program_id(0) == 0) & (pl.program_id(1) == 0) & (qi == 0)
        last = (pl.program_id(0) == nb - 1) & (pl.program_id(1) == N_GROUPS - 1) & (qi == nq - 1)
        start_gather, finish_gather = _late_gather([s0, s1, s2, s3, s4], [d0, d1, d2, d3, d4],
                                                   send_sems, recv_sems, local_sems)
        pl.when(first)(start_gather)
        head, diag, tri, _ = _attn_consts()
        qh = _split_heads(q_ref[...], head)
        o_ref[...] = jnp.zeros_like(o_ref)
        zero_col = jnp.zeros((Q_BLOCK, 1), F32)

        def block(kb, carries, rows, masked):
            kblk = _key_block(k_ref, kb)
            vstack = jnp.concatenate(_split_heads(_key_block(v_ref, kb), head), axis=0)
            mask = diag[:Q_TILE - rows.start] if masked else None
            weights, out = [], []
            for h in range(HEAD_GROUP):
                a, _, blk_sum = _sb_block(_dot_nt(qh[h][rows], kblk), mask, tri, carries[h][rows])
                weights.append(a.astype(BF16))
                out.append(carries[h][rows] + blk_sum)
            o_ref[rows, :] += _dot(jnp.concatenate(weights, axis=1), vstack)
            return out

        second = block(2 * qi + 1, [jnp.zeros((Q_TILE, 1), F32)] * HEAD_GROUP, slice(Q_BLOCK, Q_TILE), True)
        carries = tuple(jnp.concatenate([zero_col, c], axis=0) for c in second)
        everything = slice(0, Q_TILE)
        carries = tuple(block(2 * qi, carries, everything, True))
        lax.fori_loop(1, 2 * qi + 1, lambda jj, cs: tuple(block(2 * qi - jj, cs, everything, False)), carries)
        pl.when(last)(finish_gather)

    n_shards = len(shards)
    full = [jax.ShapeDtypeStruct((D_MODEL, IN_WIDTH), BF16)] + [jax.ShapeDtypeStruct((D_MODEL, D_MODEL), BF16)] * 3 \
        + [jax.ShapeDtypeStruct((CONV_ROWS, D_MODEL), F32)]
    res = pl.pallas_call(
        body, grid=(nb, N_GROUPS, nq),
        in_specs=[pl.BlockSpec((Q_TILE, GROUP_W), lambda b, g, i: (b * nq + i, g)),
                  pl.BlockSpec((seq, GROUP_W), lambda b, g, i: (b, N_GROUPS + g)),
                  pl.BlockSpec((seq, GROUP_W), lambda b, g, i: (b, 2 * N_GROUPS + g))] + [ANY] * (1 + n_shards),
        out_specs=[pl.BlockSpec((Q_TILE, GROUP_W), lambda b, g, i: (b * nq + i, g))] + [ANY] * n_shards,
        out_shape=[jax.ShapeDtypeStruct((nb * seq, D_MODEL), F32)] + full,
        input_output_aliases={3: 1},
        scratch_shapes=[pltpu.SemaphoreType.DMA((3 * n_shards,)), pltpu.SemaphoreType.DMA((3 * n_shards,)),
                        pltpu.SemaphoreType.DMA((n_shards - 1,))],
        compiler_params=_params("arbitrary", "arbitrary", "arbitrary"), name="attn_fwd")(
            qkv, qkv, qkv, w_in_early, *shards)
    return res[0], res[1:]


def _attn_bwd(qkv, do, nb, seq):
    nq = seq // Q_TILE
    n_kb = seq // Q_BLOCK

    def body(q_ref, k_ref, v_ref, do_ref, dq_ref, dk_ref, dv_ref, g_s, beta_s, dq_acc, dk_acc, dv_acc):
        qi = pl.program_id(2)

        @pl.when(qi == 0)
        def _():
            dk_acc[...] = jnp.zeros_like(dk_acc)
            dv_acc[...] = jnp.zeros_like(dv_acc)

        head, diag, tri_suffix, tri_prefix = _attn_consts()
        qh = _split_heads(q_ref[...], head)
        doh = _split_heads(do_ref[...], head)
        dq_acc[...] = jnp.zeros_like(dq_acc)
        zero_col = jnp.zeros((Q_BLOCK, 1), F32)
        everything = slice(0, Q_TILE)
        second = slice(Q_BLOCK, Q_TILE)

        def block_a(kb, carries, rows, masked):
            ks = pl.multiple_of(kb * Q_BLOCK, Q_BLOCK)
            kblk = _key_block(k_ref, kb)
            vblk = _key_block(v_ref, kb)
            mask = diag[:Q_TILE - rows.start] if masked else None
            weights, out = [], []
            for h in range(HEAD_GROUP):
                l = _dot_nt(qh[h][rows], kblk)
                a, lnb, blk_sum = _sb_block(l, mask, tri_suffix, carries[h][rows])
                beta = jnp.exp(l + lnb)
                if masked:
                    beta = jnp.where(mask, beta, 0.0)
                g_s[h, kb, rows, :] = a * _dot_nt(doh[h][rows], vblk)
                beta_s[h, kb, rows, :] = beta.astype(BF16)
                weights.append(a.astype(BF16))
                out.append(carries[h][rows] + blk_sum)
            dv_acc[pl.ds(ks, Q_BLOCK), :] += _dot_tn(jnp.concatenate(weights, axis=0),
                                                     jnp.concatenate([t[rows] for t in doh], axis=0))
            return out

        part = block_a(2 * qi + 1, [jnp.zeros((Q_TILE, 1), F32)] * HEAD_GROUP, second, True)
        carries = tuple(jnp.concatenate([zero_col, c], axis=0) for c in part)
        carries = tuple(block_a(2 * qi, carries, everything, True))
        lax.fori_loop(1, 2 * qi + 1, lambda jj, cs: tuple(block_a(2 * qi - jj, cs, everything, False)), carries)

        def block_b(kb, prefixes, rows, masked):
            ks = pl.multiple_of(kb * Q_BLOCK, Q_BLOCK)
            kstack = jnp.concatenate(_split_heads(_key_block(k_ref, kb), head), axis=0)
            dls, out = [], []
            for h in range(HEAD_GROUP):
                g = g_s[h, kb, rows, :]
                beta = beta_s[h, kb, rows, :].astype(F32)
                incl = _sum_along_keys(g, tri_prefix)
                before = incl - g + prefixes[h][rows]
                dl = g - beta * (g + before)
                if masked:
                    dl = jnp.where(diag[:Q_TILE - rows.start], dl, 0.0)
                dls.append(dl.astype(BF16))
                out.append(prefixes[h][rows] + incl[:, Q_BLOCK - 1:Q_BLOCK])
            dq_acc[rows, :] += _dot(jnp.concatenate(dls, axis=1), kstack)
            dk_acc[pl.ds(ks, Q_BLOCK), :] += _dot_tn(jnp.concatenate(dls, axis=0),
                                                     jnp.concatenate([t[rows] for t in qh], axis=0))
            return out

        zeros = (jnp.zeros((Q_TILE, 1), F32),) * HEAD_GROUP
        prefixes = lax.fori_loop(0, 2 * qi, lambda kb, ps: tuple(block_b(kb, ps, everything, False)), zeros)
        prefixes = block_b(2 * qi, prefixes, everything, True)
        block_b(2 * qi + 1, prefixes, second, True)
        dq_ref[...] = (dq_acc[...] * Q_SCALE).astype(BF16)

        @pl.when(qi == nq - 1)
        def _():
            dk_ref[...] = dk_acc[...].astype(BF16)
            dv_ref[...] = dv_acc[...].astype(BF16)

    t = nb * seq
    qspec = pl.BlockSpec((Q_TILE, GROUP_W), lambda b, g, i: (b * nq + i, g))
    kvout = pl.BlockSpec((seq, GROUP_W), lambda b, g, i: (b, g))
    return pl.pallas_call(
        body, grid=(nb, N_GROUPS, nq),
        in_specs=[qspec,
                  pl.BlockSpec((seq, GROUP_W), lambda b, g, i: (b, N_GROUPS + g)),
                  pl.BlockSpec((seq, GROUP_W), lambda b, g, i: (b, 2 * N_GROUPS + g)),
                  qspec],
        out_specs=[qspec, kvout, kvout],
        out_shape=[jax.ShapeDtypeStruct((t, D_MODEL), BF16)] * 3,
        scratch_shapes=[pltpu.VMEM((HEAD_GROUP, n_kb, Q_TILE, Q_BLOCK), F32),
                        pltpu.VMEM((HEAD_GROUP, n_kb, Q_TILE, Q_BLOCK), BF16),
                        pltpu.VMEM((Q_TILE, GROUP_W), F32),
                        pltpu.VMEM((seq, GROUP_W), F32), pltpu.VMEM((seq, GROUP_W), F32)],
        compiler_params=_params("parallel", "parallel", "arbitrary"), name="attn_bwd")(qkv, qkv, qkv, do)


def _conv_block_rows(seq):
    return min(256, seq)


SUBLANES = 8
CONV_SUB = 32
CONV_GROUPS = CONV_SUB // SUBLANES


def _fill_shifts(sh_ref, n):
    for r in range(1, SUBLANES):
        sh_ref[r, 0:n, :] = sh_ref[0, pl.ds(r, n), :]


def _shifted(sh_ref, start, cs):
    r = start % SUBLANES
    return sh_ref[r, start - r:start - r + CONV_SUB, cs].reshape(CONV_GROUPS, SUBLANES, -1)


def _tap(w_ref, k, cs):
    row = w_ref[k:k + 1, cs]
    return jnp.broadcast_to(row, (SUBLANES, row.shape[1]))[None]


def _tree_sum(x):
    parts = [x[i] for i in range(x.shape[0])]
    while len(parts) > 1:
        parts = [parts[i] + parts[i + 1] for i in range(0, len(parts), 2)]
    return parts[0]


def _conv_fwd(hrest, conv_w, conv_b, ln_g, ln_b, nb, seq):
    ts = _conv_block_rows(seq)
    nblk = seq // ts
    lanes = 128

    def body(cv_ref, cg_ref, cvh_ref, cgh_ref, z_ref, w_ref, cb_ref, g_ref, b_ref, u1_ref, cvin_ref, ush):
        i = pl.program_id(1)
        halo = cvh_ref[...] * _sigmoid(cgh_ref[...])
        ush[0, 0:HALO, :] = jnp.where(i > 0, halo, 0.0)
        ush[0, HALO:HALO + ts, :] = cv_ref[...] * _sigmoid(cg_ref[...])
        _fill_shifts(ush, HALO + ts - SUBLANES)
        for cc in range(D_MODEL // lanes):
            cs = slice(cc * lanes, (cc + 1) * lanes)
            for r0 in range(0, ts, CONV_SUB):
                acc = jnp.zeros((CONV_GROUPS, SUBLANES, lanes), F32) + _tap(cb_ref, 0, cs)
                for k in range(CONV_K):
                    acc = acc + _tap(w_ref, k, cs) * _shifted(ush, HALO - CONV_K + 1 + k + r0, cs)
                u1_ref[r0:r0 + CONV_SUB, cs] = acc.reshape(CONV_SUB, lanes)
        xhat, _ = _ln_stats(u1_ref[...])
        u2 = xhat * g_ref[...] + b_ref[...]
        z = z_ref[...]
        cvin_ref[...] = (u2 * _sigmoid(u2) * z * _sigmoid(z)).astype(BF16)

    def main(colblk):
        return pl.BlockSpec((ts, D_MODEL), lambda b, i: (b * nblk + i, colblk))

    def halo(colblk):
        return pl.BlockSpec((HALO, D_MODEL),
                            lambda b, i: (jnp.maximum((b * seq + i * ts) // HALO - 1, 0), colblk))

    vec = pl.BlockSpec((1, D_MODEL), lambda b, i: (0, 0))
    t = nb * seq
    return pl.pallas_call(
        body, grid=(nb, nblk),
        in_specs=[main(1), main(2), halo(1), halo(2), main(3),
                  pl.BlockSpec((CONV_ROWS, D_MODEL), lambda b, i: (0, 0)), vec, vec, vec],
        out_specs=[main(0), main(0)],
        out_shape=[jax.ShapeDtypeStruct((t, D_MODEL), F32), jax.ShapeDtypeStruct((t, D_MODEL), BF16)],
        scratch_shapes=[pltpu.VMEM((SUBLANES, HALO + ts, D_MODEL), F32)],
        compiler_params=_params("parallel", "parallel"), name="conv_fwd")(
            hrest, hrest, hrest, hrest, hrest, conv_w, conv_b, ln_g, ln_b)


def _tail(o, hrest, cvin, h0, tgt, w_sb, w_cv, w_out, ln_g, ln_b):
    t = o.shape[0]
    tm = min(256, t)

    def body(o_ref, z_ref, gs_ref, gc_ref, cvin_ref, h0_ref, tgt_ref, wsb_ref, wcv_ref, wout_ref, g_ref, b_ref,
             dr_ref, drb_ref, a_ref, mg_ref, dysb_ref, dycv_ref, do_ref, dz_ref, dgs_ref, dgc_ref, dcvin_ref,
             st_ref):
        @pl.when(pl.program_id(0) == 0)
        def _():
            st_ref[...] = jnp.zeros_like(st_ref)

        o = o_ref[...]
        z = z_ref[...]
        sz = _sigmoid(z)
        a = (o * z * sz).astype(BF16)
        a_ref[...] = a
        y_sb = _dot(a, wsb_ref[...])
        y_cv = _dot(cvin_ref[...], wcv_ref[...])
        s_sb = _sigmoid(gs_ref[...])
        s_cv = _sigmoid(gc_ref[...])
        merged = (s_sb * y_sb + s_cv * y_cv).astype(BF16)
        mg_ref[...] = merged
        r = ALPHA * h0_ref[...] + _dot(merged, wout_ref[...])
        xhat, rstd = _ln_stats(r)
        g = g_ref[...]
        err = xhat * g + b_ref[...] - tgt_ref[...]
        dy = err * (1.0 / D_MODEL)
        st_ref[0:1, :] += jnp.sum(dy * xhat, axis=0, keepdims=True)
        st_ref[1:2, :] += jnp.sum(dy, axis=0, keepdims=True)
        st_ref[2:3, :] += (0.5 / D_MODEL) * jnp.sum(err * err, axis=0, keepdims=True)
        dr = _ln_bwd(dy, xhat, rstd, g)
        dr_ref[...] = dr
        drb = dr.astype(BF16)
        drb_ref[...] = drb
        dm = _dot_nt(drb, wout_ref[...])
        dy_sb = (dm * s_sb).astype(BF16)
        dy_cv = (dm * s_cv).astype(BF16)
        dysb_ref[...] = dy_sb
        dycv_ref[...] = dy_cv
        dgs_ref[...] = (dm * y_sb * s_sb * (1.0 - s_sb)).astype(BF16)
        dgc_ref[...] = (dm * y_cv * s_cv * (1.0 - s_cv)).astype(BF16)
        da = _dot_nt(dy_sb, wsb_ref[...])
        dcvin_ref[...] = _dot_nt(dy_cv, wcv_ref[...])
        do_ref[...] = (da * z * sz).astype(BF16)
        dz_ref[...] = (da * o * sz * (1.0 + z * (1.0 - sz))).astype(BF16)

    def tok(colblk=0):
        return pl.BlockSpec((tm, D_MODEL), lambda i: (i, colblk))

    wspec = pl.BlockSpec((D_MODEL, D_MODEL), lambda i: (0, 0), pipeline_mode=pl.Buffered(1))
    vec = pl.BlockSpec((1, D_MODEL), lambda i: (0, 0))
    bf = jax.ShapeDtypeStruct((t, D_MODEL), BF16)
    f32 = jax.ShapeDtypeStruct((t, D_MODEL), F32)
    return pl.pallas_call(
        body, grid=(t // tm,),
        in_specs=[tok(), tok(0), tok(4), tok(5), tok(), tok(), tok(), wspec, wspec, wspec, vec, vec],
        out_specs=[tok()] * 11 + [pl.BlockSpec((8, D_MODEL), lambda i: (0, 0))],
        out_shape=[f32] + [bf] * 9 + [f32, jax.ShapeDtypeStruct((8, D_MODEL), F32)],
        compiler_params=_params("arbitrary"), name="tail")(
            o, hrest, hrest, hrest, cvin, h0, tgt, w_sb, w_cv, w_out, ln_g, ln_b)


def _conv_branch_bwd(dcvin, u1, hrest, ln_g, ln_b):
    t = u1.shape[0]
    tm = min(512, t)

    def body(dc_ref, u1_ref, z_ref, g_ref, b_ref, du1_ref, dz_ref, st_ref):
        @pl.when(pl.program_id(0) == 0)
        def _():
            st_ref[...] = jnp.zeros_like(st_ref)

        xhat, rstd = _ln_stats(u1_ref[...])
        g = g_ref[...]
        u2 = xhat * g + b_ref[...]
        s2 = _sigmoid(u2)
        z = z_ref[...]
        sz = _sigmoid(z)
        dc = dc_ref[...]
        dz_ref[...] = (dc * (u2 * s2) * sz * (1.0 + z * (1.0 - sz))).astype(BF16)
        du2 = dc * (z * sz) * s2 * (1.0 + u2 * (1.0 - s2))
        st_ref[0:1, :] += jnp.sum(du2 * xhat, axis=0, keepdims=True)
        st_ref[1:2, :] += jnp.sum(du2, axis=0, keepdims=True)
        du1 = _ln_bwd(du2, xhat, rstd, g)
        du1_ref[...] = du1
        st_ref[2:3, :] += jnp.sum(du1, axis=0, keepdims=True)

    tok = pl.BlockSpec((tm, D_MODEL), lambda i: (i, 0))
    vec = pl.BlockSpec((1, D_MODEL), lambda i: (0, 0))
    return pl.pallas_call(
        body, grid=(t // tm,),
        in_specs=[tok, tok, pl.BlockSpec((tm, D_MODEL), lambda i: (i, 3)), vec, vec],
        out_specs=[tok, tok, pl.BlockSpec((8, D_MODEL), lambda i: (0, 0))],
        out_shape=[jax.ShapeDtypeStruct((t, D_MODEL), F32), jax.ShapeDtypeStruct((t, D_MODEL), BF16),
                   jax.ShapeDtypeStruct((8, D_MODEL), F32)],
        compiler_params=_params("arbitrary"), name="conv_branch_bwd")(dcvin, u1, hrest, ln_g, ln_b)


def _conv_bwd(du1, hrest, conv_w, nb, seq):
    ts = _conv_block_rows(seq)
    nblk = seq // ts
    lanes = 128
    last_halo = nb * seq // HALO - 1

    def body(du_ref, duh_ref, cv_ref, cg_ref, cvh_ref, cgh_ref, w_ref, dcv_ref, dcg_ref, dw_ref, ush, dsh, dw_part):
        b = pl.program_id(0)
        i = pl.program_id(1)

        @pl.when((b == 0) & (i == 0))
        def _():
            dw_part[...] = jnp.zeros_like(dw_part)

        cv = cv_ref[...]
        sg = _sigmoid(cg_ref[...])
        halo = cvh_ref[...] * _sigmoid(cgh_ref[...])
        ush[0, 0:HALO, :] = jnp.where(i > 0, halo, 0.0)
        ush[0, HALO:HALO + ts, :] = cv * sg
        dsh[0, 0:ts, :] = du_ref[...]
        dsh[0, ts:ts + HALO, :] = jnp.where(i < nblk - 1, duh_ref[...], 0.0)
        _fill_shifts(ush, HALO + ts - SUBLANES)
        _fill_shifts(dsh, HALO + ts - SUBLANES)
        for cc in range(D_MODEL // lanes):
            cs = slice(cc * lanes, (cc + 1) * lanes)
            for r0 in range(0, ts, CONV_SUB):
                rows = slice(r0, r0 + CONV_SUB)
                dmain = _shifted(dsh, r0, cs)
                acc = jnp.zeros((CONV_GROUPS, SUBLANES, lanes), F32)
                for k in range(CONV_K):
                    acc = acc + _tap(w_ref, k, cs) * _shifted(dsh, CONV_K - 1 - k + r0, cs)
                    dw_part[k, :, cs] += _tree_sum(dmain * _shifted(ush, HALO - CONV_K + 1 + k + r0, cs))
                acc = acc.reshape(CONV_SUB, lanes)
                dcv_ref[rows, cs] = (acc * sg[rows, cs]).astype(BF16)
                dcg_ref[rows, cs] = (acc * cv[rows, cs] * sg[rows, cs] * (1.0 - sg[rows, cs])).astype(BF16)

        @pl.when((b == nb - 1) & (i == nblk - 1))
        def _():
            for k in range(CONV_ROWS):
                dw_ref[k:k + 1, :] = jnp.sum(dw_part[k], axis=0, keepdims=True)

    def main(colblk):
        return pl.BlockSpec((ts, D_MODEL), lambda b, i: (b * nblk + i, colblk))

    def halo_before(colblk):
        return pl.BlockSpec((HALO, D_MODEL),
                            lambda b, i: (jnp.maximum((b * seq + i * ts) // HALO - 1, 0), colblk))

    halo_after = pl.BlockSpec((HALO, D_MODEL),
                              lambda b, i: (jnp.minimum((b * seq + (i + 1) * ts) // HALO, last_halo), 0))
    t = nb * seq
    return pl.pallas_call(
        body, grid=(nb, nblk),
        in_specs=[main(0), halo_after, main(1), main(2), halo_before(1), halo_before(2),
                  pl.BlockSpec((CONV_ROWS, D_MODEL), lambda b, i: (0, 0))],
        out_specs=[main(0), main(0), pl.BlockSpec((CONV_ROWS, D_MODEL), lambda b, i: (0, 0))],
        out_shape=[jax.ShapeDtypeStruct((t, D_MODEL), BF16), jax.ShapeDtypeStruct((t, D_MODEL), BF16),
                   jax.ShapeDtypeStruct((CONV_ROWS, D_MODEL), F32)],
        scratch_shapes=[pltpu.VMEM((SUBLANES, HALO + ts, D_MODEL), F32), pltpu.VMEM((SUBLANES, ts + HALO, D_MODEL), F32),
                        pltpu.VMEM((CONV_ROWS, SUBLANES, D_MODEL), F32)],
        compiler_params=_params("arbitrary", "arbitrary"), name="conv_bwd")(
            du1, du1, hrest, hrest, hrest, hrest, conv_w)


def _weight_grad(x, dys, name):
    n = len(dys)
    t = x.shape[0]
    tm = min(1024, t)

    def body(x_ref, *refs):
        o_ref = refs[n]
        j = pl.program_id(0)

        @pl.when(pl.program_id(1) == 0)
        def _():
            o_ref[...] = jnp.zeros_like(o_ref)

        for p in range(n):
            @pl.when(j == p)
            def _(p=p):
                o_ref[...] += _dot_tn(x_ref[...], refs[p][...])

    def dy_spec(p):
        return pl.BlockSpec((tm, D_MODEL), lambda j, i: (jnp.where(j == p, i, 0), 0))

    return pl.pallas_call(
        body, grid=(n, t // tm),
        in_specs=[pl.BlockSpec((tm, D_MODEL), lambda j, i: (i, 0))] + [dy_spec(p) for p in range(n)],
        out_specs=pl.BlockSpec((D_MODEL, D_MODEL), lambda j, i: (0, j)),
        out_shape=jax.ShapeDtypeStruct((D_MODEL, n * D_MODEL), F32),
        compiler_params=_params("parallel", "arbitrary"), name=name)(x, *dys)


def _in_proj_bwd(dhs, w_in, dr, x, ln_g, pair):
    n = len(dhs)
    t = x.shape[0]
    tm = min(512, t)
    n_i = t // tm

    def body(*refs):
        dh_refs = refs[:n]
        w_ref, dr_ref, x_ref, g_ref = refs[n:n + 4]
        pair_refs = refs[n + 4:n + 8]
        gx_ref, st_ref = refs[n + 8:n + 10]
        got_refs = refs[n + 10:n + 14]
        acc, send_sems, recv_sems = refs[n + 14:]
        p = pl.program_id(0)
        i = pl.program_id(1)
        pieces, wholes = _chip_exchange_copies(pair_refs, got_refs, send_sems, recv_sems)

        @pl.when((p == 0) & (i == 0))
        def _():
            st_ref[...] = jnp.zeros_like(st_ref)
            for cp in pieces:
                cp.start()

        @pl.when((p == n - 1) & (i == n_i - 1))
        def _():
            for cp in wholes:
                cp.wait_recv()
            for cp in wholes:
                cp.wait_send()

        @pl.when(p == 0)
        def _():
            acc[i] = ALPHA * dr_ref[...]

        for q in range(n):
            @pl.when(p == q)
            def _(q=q):
                acc[i] += _dot_nt(dh_refs[q][...], w_ref[...])

        @pl.when(p == n - 1)
        def _():
            xhat, rstd = _ln_stats(x_ref[...])
            dh0 = acc[i]
            st_ref[0:1, :] += jnp.sum(dh0 * xhat, axis=0, keepdims=True)
            st_ref[1:2, :] += jnp.sum(dh0, axis=0, keepdims=True)
            gx_ref[...] = _ln_bwd(dh0, xhat, rstd, g_ref[...])

    def tok(piece):
        return pl.BlockSpec((tm, D_MODEL), lambda p, i: (jnp.where(p == piece, i, 0), 0))

    got = [jax.ShapeDtypeStruct((3, HALF, SHARD_IN), BF16)] + [jax.ShapeDtypeStruct((3, SHARD_SQ, HALF), BF16)] * 3
    res = pl.pallas_call(
        body, grid=(n, n_i),
        in_specs=[tok(q) for q in range(n)] + [pl.BlockSpec((D_MODEL, D_MODEL), lambda p, i: (0, p)), tok(0), tok(n - 1),
                                               pl.BlockSpec((1, D_MODEL), lambda p, i: (0, 0))] + [ANY] * 4,
        out_specs=[tok(n - 1), pl.BlockSpec((8, D_MODEL), lambda p, i: (0, 0))] + [ANY] * 4,
        out_shape=[jax.ShapeDtypeStruct((t, D_MODEL), F32), jax.ShapeDtypeStruct((8, D_MODEL), F32)] + got,
        scratch_shapes=[pltpu.VMEM((n_i, tm, D_MODEL), F32), pltpu.SemaphoreType.DMA((12,)),
                        pltpu.SemaphoreType.DMA((12,))],
        compiler_params=_params("arbitrary", "arbitrary"), name="in_proj_bwd")(*dhs, w_in, dr, x, ln_g, *pair)
    return res[0], res[1], res[2:]


def _forward_backward(x, tgt, nb, seq, shards, vecs):
    w_in_early = _gather_first(shards[0])
    h0, h0b = _ln_in(x, vecs["ln_in_g"], vecs["ln_in_b"])
    qkv = _in_proj(h0b, w_in_early, 0, 3, BF16, True, "in_proj_qkv")
    o, (w_in, w_sb, w_cv, w_out, conv_w) = _attn_fwd(qkv, nb, seq, w_in_early, shards)
    hrest = _in_proj(h0b, w_in, 3, 6, F32, False, "in_proj_rest")
    u1, cvin = _conv_fwd(hrest, conv_w, vecs["conv_b"], vecs["conv_ln_g"], vecs["conv_ln_b"], nb, seq)
    (dr, drb, a, merged, dy_sb, dy_cv, do, dz_sb, dg_sb, dg_cv, dcvin, st_tail) = _tail(
        o, hrest, cvin, h0, tgt, w_sb, w_cv, w_out, vecs["ln_post_g"], vecs["ln_post_b"])
    d_w_sb = _weight_grad(a, [dy_sb], "grad_w_sb")
    d_w_cv = _weight_grad(cvin, [dy_cv], "grad_w_cv")
    d_w_out = _weight_grad(merged, [drb], "grad_w_out")
    du1, dz_cv, st_conv = _conv_branch_bwd(dcvin, u1, hrest, vecs["conv_ln_g"], vecs["conv_ln_b"])
    dc_val, dc_gate, d_conv_w = _conv_bwd(du1, hrest, conv_w, nb, seq)
    dq, dk, dv = _attn_bwd(qkv, do, nb, seq)
    dhs = [dq, dk, dv, dz_sb, dc_val, dc_gate, dz_cv, dg_sb, dg_cv]
    d_w_in = _weight_grad(h0b, dhs, "grad_w_in")
    small = jnp.concatenate([st_conv[2:3], st_conv[0:2], st_tail[0:3], d_conv_w], axis=0)
    return w_in, dhs, dr, (d_w_in, d_w_sb, d_w_cv, d_w_out), small


def _place():
    x, y, c = lax.axis_index("x"), lax.axis_index("y"), lax.axis_index("c")
    chips = [(1 - x, y), (x, 1 - y), (1 - x, 1 - y)]
    return x, y, c, chips


N_EARLY = -(-3 * D_MODEL // SHARD_IN)
GATHER_PARTS = 4


def _w_in_slot(full, chip):
    return full.at[:, pl.ds(pl.multiple_of(chip * SHARD_IN, 128), SHARD_IN)]


def _gather_first(w_in):
    part_rows = HALF // GATHER_PARTS
    n_sem = 3 * GATHER_PARTS

    def body(src, dst, ici_send, ici_recv, fwd_send, fwd_recv, local_sem):
        x, y, c, chips = _place()
        me = 2 * x + y
        sibling = (x, y, 1 - c)
        owner = [2 * px + py for px, py in chips]

        def part(ref, k, half):
            return ref.at[pl.ds(pl.multiple_of(half * HALF + k * part_rows, 16), part_rows), :]

        def held(j, k, half):
            return part(_w_in_slot(dst, owner[j]), k, half)

        def sent(j, k):
            px, py = chips[j]
            return _remote(part(src, k, c), part(_w_in_slot(dst, me), k, c), ici_send.at[j * GATHER_PARTS + k],
                           ici_recv.at[j * GATHER_PARTS + k], (px, py, c))

        def landed(j, k):
            px, py = chips[j]
            return _remote(held(j, k, c), held(j, k, c), ici_send.at[j * GATHER_PARTS + k],
                           ici_recv.at[j * GATHER_PARTS + k], (px, py, c))

        def passed(j, k, half):
            return _remote(held(j, k, half), held(j, k, half), fwd_send.at[j * GATHER_PARTS + k],
                           fwd_recv.at[j * GATHER_PARTS + k], sibling)

        own = _w_in_slot(dst, me)
        rows = D_MODEL // N_CHUNKS
        for k in range(N_CHUNKS):
            pltpu.make_async_copy(src.at[pl.ds(k * rows, rows)], own.at[pl.ds(k * rows, rows)], local_sem).start()
        local = pltpu.make_async_copy(src, own, local_sem)

        @pl.when(me < N_EARLY)
        def _():
            for j in range(3):
                for k in range(GATHER_PARTS):
                    sent(j, k).start()

        for j in range(3):
            @pl.when(owner[j] < N_EARLY)
            def _(j=j):
                for k in range(GATHER_PARTS):
                    landed(j, k).wait_recv()
                    passed(j, k, c).start()

        for j in range(3):
            @pl.when(owner[j] < N_EARLY)
            def _(j=j):
                for k in range(GATHER_PARTS):
                    passed(j, k, 1 - c).wait_recv()
                for k in range(GATHER_PARTS):
                    passed(j, k, c).wait_send()

        @pl.when(me < N_EARLY)
        def _():
            for j in range(3):
                for k in range(GATHER_PARTS):
                    sent(j, k).wait_send()

        local.wait()

    return pl.pallas_call(
        body, in_specs=[ANY], out_specs=ANY, out_shape=jax.ShapeDtypeStruct((D_MODEL, IN_WIDTH), BF16),
        scratch_shapes=[pltpu.SemaphoreType.DMA((n_sem,)), pltpu.SemaphoreType.DMA((n_sem,)),
                        pltpu.SemaphoreType.DMA((n_sem,)), pltpu.SemaphoreType.DMA((n_sem,)),
                        pltpu.SemaphoreType.DMA],
        name="gather_first")(w_in)


def _late_gather(srcs, dsts, send_sems, recv_sems, local_sems):
    x, y, c, chips = _place()
    me = 2 * x + y
    small = range(1, 5)

    def slot(a, chip):
        if a == 0:
            return _w_in_slot(dsts[0], chip)
        if a == 4:
            return dsts[a].at[:, pl.ds(pl.multiple_of(chip * SHARD_SQ, 128), SHARD_SQ)]
        return dsts[a].at[pl.ds(pl.multiple_of(chip * SHARD_SQ, 16), SHARD_SQ), :]

    def sems(a, j):
        return send_sems.at[3 * a + j], recv_sems.at[3 * a + j]

    def whole(a, j):
        px, py = chips[j]
        return _remote(srcs[a], slot(a, me), *sems(a, j), (px, py, c))

    def arrival(a, j):
        px, py = chips[j]
        return _arrival(slot(a, 2 * px + py), *sems(a, j), (px, py, c))

    def local(a):
        return pltpu.make_async_copy(srcs[a], slot(a, me), local_sems.at[a - 1])

    def start():
        for a in small:
            local(a).start()
            for j in range(3):
                whole(a, j).start()

        @pl.when(me >= N_EARLY)
        def _():
            for j, (px, py) in enumerate(chips):
                for cp in _chunk_copies(srcs[0], slot(0, me), GATHER_PARTS, *sems(0, j), (px, py, c)):
                    cp.start()

    def finish():
        for j, (px, py) in enumerate(chips):
            for a in small:
                arrival(a, j).wait_recv()

            @pl.when(2 * px + py >= N_EARLY)
            def _(j=j):
                arrival(0, j).wait_recv()

        for a in small:
            for j in range(3):
                whole(a, j).wait_send()
            local(a).wait()

        @pl.when(me >= N_EARLY)
        def _():
            for j in range(3):
                whole(0, j).wait_send()

    return start, finish


def _remote(src, dst, send_sem, recv_sem, device):
    return pltpu.make_async_remote_copy(src_ref=src, dst_ref=dst, send_sem=send_sem, recv_sem=recv_sem,
                                        device_id=device, device_id_type=MESH)


def _chunk_copies(src, dst, n, send_sem, recv_sem, device):
    rows = src.shape[0] // n
    return [_remote(src.at[pl.ds(k * rows, rows)], dst.at[pl.ds(k * rows, rows)], send_sem, recv_sem, device)
            for k in range(n)]


def _row_chunks(src, dst, n, send_sem, recv_sem, device):
    return _chunk_copies(src, dst, n, send_sem, recv_sem, device), _remote(src, dst, send_sem, recv_sem, device)


def _arrival(dst, send_sem, recv_sem, device):
    return _remote(dst, dst, send_sem, recv_sem, device)


def _pair_exchange(d_in, d_sb, d_cv, d_out):
    def body(i0, i1, i2, i3, r0, r1, r2, r3, send_sems, recv_sems):
        x, y, c, _ = _place()
        srcs = [i0, i1, i2, i3]
        theirs = [r0, r1, r2, r3]
        sibling = (x, y, 1 - c)
        start = pl.multiple_of((1 - c) * HALF, 128)
        views = [srcs[0].at[pl.ds(start, HALF), :]] + [srcs[a].at[:, pl.ds(start, HALF)] for a in range(1, 4)]
        pieces, wholes = [], []
        for a in range(4):
            cps, whole = _row_chunks(views[a], theirs[a], N_CHUNKS, send_sems.at[a], recv_sems.at[a], sibling)
            pieces += cps
            wholes.append(whole)
        for cp in pieces:
            cp.start()
        for cp in wholes:
            cp.wait_recv()
        for cp in wholes:
            cp.wait_send()

    shapes = [jax.ShapeDtypeStruct((HALF, IN_WIDTH), F32)] + [jax.ShapeDtypeStruct((D_MODEL, HALF), F32)] * 3
    return pl.pallas_call(
        body, in_specs=[ANY] * 4, out_specs=[ANY] * 4, out_shape=shapes,
        scratch_shapes=[pltpu.SemaphoreType.DMA((4,)), pltpu.SemaphoreType.DMA((4,))],
        name="pair_exchange")(d_in, d_sb, d_cv, d_out)


def _chip_exchange_copies(srcs, theirs, send_sems, recv_sems):
    _, _, c, chips = _place()

    def shard(a, px, py):
        chip = 2 * px + py
        if a == 0:
            return srcs[a].at[:, pl.ds(pl.multiple_of(chip * SHARD_IN, 128), SHARD_IN)]
        return srcs[a].at[pl.ds(pl.multiple_of(chip * SHARD_SQ, 16), SHARD_SQ), :]

    pieces, wholes = [], []
    for j, (px, py) in enumerate(chips):
        for a in range(4):
            cps, whole = _row_chunks(shard(a, px, py), theirs[a].at[j], 4, send_sems.at[3 * a + j],
                                     recv_sems.at[3 * a + j], (px, py, c))
            pieces += cps
            wholes.append(whole)
    return pieces, wholes


def _pair_share(f_in, f_sb, f_cv, f_out):
    def body(i0, i1, i2, i3, o0, o1, o2, o3, send_sems, recv_sems):
        del i0, i1, i2, i3
        x, y, c, _ = _place()
        dsts = [o0, o1, o2, o3]
        sibling = (x, y, 1 - c)

        def half(a, which):
            start = pl.multiple_of(which * HALF, 128)
            if a == 0:
                return dsts[a].at[pl.ds(start, HALF), :]
            return dsts[a].at[:, pl.ds(start, HALF)]

        pieces, sent, arrived = [], [], []
        for a in range(4):
            cps, whole = _row_chunks(half(a, c), half(a, c), N_CHUNKS, send_sems.at[a], recv_sems.at[a], sibling)
            pieces += cps
            sent.append(whole)
            arrived.append(_arrival(half(a, 1 - c), send_sems.at[a], recv_sems.at[a], sibling))
        for cp in pieces:
            cp.start()
        for cp in arrived:
            cp.wait_recv()
        for cp in sent:
            cp.wait_send()

    return pl.pallas_call(
        body, in_specs=[ANY] * 4, out_specs=[ANY] * 4,
        out_shape=[jax.ShapeDtypeStruct((D_MODEL, SHARD_IN), F32)] + [jax.ShapeDtypeStruct((SHARD_SQ, D_MODEL), F32)] * 3,
        input_output_aliases={0: 0, 1: 1, 2: 2, 3: 3},
        scratch_shapes=[pltpu.SemaphoreType.DMA((4,)), pltpu.SemaphoreType.DMA((4,))],
        name="pair_share")(f_in, f_sb, f_cv, f_out)


def _small_allreduce(small):
    def body(s_ref, o_ref, slots, send_sems, recv_sems):
        x, y, c, _ = _place()
        me = 4 * x + 2 * y + c
        slots[me] = s_ref[...]
        sends = []
        for k in range(1, N_DEV):
            px, py, pc = (x + (k >> 2)) % 2, (y + ((k >> 1) & 1)) % 2, (c + (k & 1)) % 2
            sends.append(pltpu.make_async_remote_copy(
                src_ref=s_ref, dst_ref=slots.at[me], send_sem=send_sems.at[k - 1], recv_sem=recv_sems.at[k - 1],
                device_id=(px, py, pc), device_id_type=MESH))
        for cp in sends:
            cp.start()
        for k in range(1, N_DEV):
            px, py, pc = (x + (k >> 2)) % 2, (y + ((k >> 1) & 1)) % 2, (c + (k & 1)) % 2
            pltpu.make_async_remote_copy(
                src_ref=s_ref, dst_ref=slots.at[4 * px + 2 * py + pc], send_sem=send_sems.at[k - 1],
                recv_sem=recv_sems.at[k - 1], device_id=(px, py, pc), device_id_type=MESH).wait_recv()
        for cp in sends:
            cp.wait_send()
        total = slots[0]
        for d in range(1, N_DEV):
            total = total + slots[d]
        o_ref[...] = total
        o_ref[7:8, :] = jnp.zeros((1, D_MODEL), F32) + jnp.sum(total[7:8, :], axis=1, keepdims=True)

    return pl.pallas_call(
        body, in_specs=[pl.BlockSpec(memory_space=pltpu.VMEM)], out_specs=pl.BlockSpec(memory_space=pltpu.VMEM),
        out_shape=jax.ShapeDtypeStruct((SMALL_ROWS, D_MODEL), F32),
        scratch_shapes=[pltpu.VMEM((N_DEV, SMALL_ROWS, D_MODEL), F32), pltpu.SemaphoreType.DMA((N_DEV - 1,)),
                        pltpu.SemaphoreType.DMA((N_DEV - 1,))],
        name="small_allreduce")(small)


SUM_BLOCKS = 8


def _pair_sum(place, full, theirs, by_rows, name):
    rows, cols = theirs.shape
    tr = rows // SUM_BLOCKS

    def body(p_ref, a_ref, b_ref, o_ref):
        del p_ref
        o_ref[...] = (a_ref[...] + b_ref[...]).astype(BF16)

    mine = (lambda i, p: (p[1] * SUM_BLOCKS + i, 0)) if by_rows else (lambda i, p: (i, p[1]))
    spec = pl.BlockSpec((tr, cols), lambda i, p: (i, 0))
    return pl.pallas_call(
        body, out_shape=jax.ShapeDtypeStruct(theirs.shape, BF16),
        grid_spec=pltpu.PrefetchScalarGridSpec(num_scalar_prefetch=1, grid=(SUM_BLOCKS,),
                                               in_specs=[pl.BlockSpec((tr, cols), mine), spec], out_specs=spec),
        compiler_params=_params("parallel"), name=name)(place, full, theirs)


def _chip_sum(place, pair, got, by_rows, name):
    _, rows, cols = got.shape
    tr = rows // SUM_BLOCKS

    def body(p_ref, a_ref, g_ref, o_ref):
        del p_ref
        o_ref[...] = ((a_ref[...].astype(F32) + g_ref[0].astype(F32)) + g_ref[1].astype(F32)) + g_ref[2].astype(F32)

    if by_rows:
        own = lambda i, p: (i, p[0])
        out = lambda i, p: (p[1] * SUM_BLOCKS + i, 0)
        full = (2 * rows, cols)
    else:
        own = lambda i, p: (p[0] * SUM_BLOCKS + i, 0)
        out = lambda i, p: (i, p[1])
        full = (rows, 2 * cols)
    return pl.pallas_call(
        body, out_shape=jax.ShapeDtypeStruct(full, F32),
        grid_spec=pltpu.PrefetchScalarGridSpec(
            num_scalar_prefetch=1, grid=(SUM_BLOCKS,),
            in_specs=[pl.BlockSpec((tr, cols), own), pl.BlockSpec((3, tr, cols), lambda i, p: (0, i, 0))],
            out_specs=pl.BlockSpec((tr, cols), out)),
        compiler_params=_params("parallel"), name=name)(place, pair, got)


def _adamw(w, g, m, v, name):
    rows, cols = w.shape
    tr = rows // 4 if rows % 32 == 0 else rows
    c1 = 1.0 - ADAM_B1 ** ADAM_STEP
    c2 = 1.0 - ADAM_B2 ** ADAM_STEP

    def body(w_ref, g_ref, m_ref, v_ref, d_ref, mo_ref, vo_ref):
        g = g_ref[...]
        mn = ADAM_B1 * m_ref[...] + (1.0 - ADAM_B1) * g
        vn = ADAM_B2 * v_ref[...] + (1.0 - ADAM_B2) * (g * g)
        mo_ref[...] = mn
        vo_ref[...] = vn
        d_ref[...] = -ADAM_LR * ((mn / c1) / (jnp.sqrt(vn / c2) + ADAM_EPS) + ADAM_WD * w_ref[...])

    spec = pl.BlockSpec((tr, cols), lambda i: (i, 0))
    shape = jax.ShapeDtypeStruct(w.shape, F32)
    return pl.pallas_call(body, grid=(rows // tr,), in_specs=[spec] * 4, out_specs=[spec] * 3,
                          out_shape=[shape] * 3, compiler_params=_params("parallel"), name=name)(w, g, m, v)


GRAD_NAMES = ("in", "sb", "cv", "out")


def _place_scalars():
    return jnp.stack([2 * lax.axis_index("x") + lax.axis_index("y"), lax.axis_index("c")]).astype(jnp.int32)


def _pair_sums(grads):
    place = _place_scalars()
    theirs = _pair_exchange(*grads)
    return [_pair_sum(place, grads[a], theirs[a], a == 0, "pair_sum_" + GRAD_NAMES[a]) for a in range(4)]


def _finish_weight_grads(pair, got):
    place = _place_scalars()
    done = [_chip_sum(place, pair[a], got[a], a == 0, "chip_sum_" + GRAD_NAMES[a]) for a in range(4)]
    return _pair_share(*done)


def kernel(x, ln_in_g, ln_in_b, w_in, w_sb_proj, conv_w, conv_b, conv_ln_g, conv_ln_b, w_cv_proj, w_out, ln_post_g, ln_post_b, loss_target, m_ln_in_g, m_ln_in_b, m_w_in, m_w_sb_proj, m_conv_w, m_conv_b, m_conv_ln_g, m_conv_ln_b, m_w_cv_proj, m_w_out, m_ln_post_g, m_ln_post_b, v_ln_in_g, v_ln_in_b, v_w_in, v_w_sb_proj, v_conv_w, v_conv_b, v_conv_ln_g, v_conv_ln_b, v_w_cv_proj, v_w_out, v_ln_post_g, v_ln_post_b):
    nb, seq, _ = x.shape
    t = nb * seq
    vec_names = ("ln_in_g", "ln_in_b", "conv_b", "conv_ln_g", "conv_ln_b", "ln_post_g", "ln_post_b")
    vec_w = dict(zip(vec_names, (ln_in_g, ln_in_b, conv_b, conv_ln_g, conv_ln_b, ln_post_g, ln_post_b)))
    vec_m = dict(zip(vec_names, (m_ln_in_g, m_ln_in_b, m_conv_b, m_conv_ln_g, m_conv_ln_b, m_ln_post_g, m_ln_post_b)))
    vec_v = dict(zip(vec_names, (v_ln_in_g, v_ln_in_b, v_conv_b, v_conv_ln_g, v_conv_ln_b, v_ln_post_g, v_ln_post_b)))
    vecs = {k: a.reshape(1, D_MODEL) for k, a in vec_w.items()}

    pad_taps = lambda a: jnp.pad(a.reshape(CONV_K, SHARD_SQ), ((0, CONV_ROWS - CONV_K), (0, 0)))
    shards = (w_in[0].astype(BF16), w_sb_proj[0].astype(BF16), w_cv_proj[0].astype(BF16), w_out[0].astype(BF16),
              pad_taps(conv_w))
    x2 = x.reshape(t, D_MODEL)
    full_in, dhs, dr, big, small = _forward_backward(x2, loss_target.reshape(t, D_MODEL), nb, seq, shards, vecs)
    pair = _pair_sums(big)
    grad_x, st_in, got = _in_proj_bwd(dhs, full_in, dr, x2, vecs["ln_in_g"], pair)
    g_in, g_sb, g_cv, g_out = _finish_weight_grads(pair, got)
    small = _small_allreduce(jnp.concatenate([st_in[0:2], small], axis=0))

    chip = 2 * lax.axis_index("x") + lax.axis_index("y")
    g_conv = lax.dynamic_slice(small, (8, chip * SHARD_SQ), (CONV_ROWS, SHARD_SQ))
    loss = small[7, 0]

    grads, deltas, new_m, new_v = {}, {}, {}, {}
    stack = lambda d: jnp.concatenate([d[k].reshape(1, D_MODEL) for k in vec_names] + [jnp.zeros((1, D_MODEL), F32)])
    vd, vm, vv = _adamw(stack(vec_w), small[0:8], stack(vec_m), stack(vec_v), "adamw_vectors")
    for r, k in enumerate(vec_names):
        shape = vec_w[k].shape
        grads[k] = small[r].reshape(shape)
        deltas[k], new_m[k], new_v[k] = vd[r].reshape(shape), vm[r].reshape(shape), vv[r].reshape(shape)

    big_w = {"w_in": (w_in, g_in, m_w_in, v_w_in), "w_sb_proj": (w_sb_proj, g_sb, m_w_sb_proj, v_w_sb_proj),
             "w_cv_proj": (w_cv_proj, g_cv, m_w_cv_proj, v_w_cv_proj), "w_out": (w_out, g_out, m_w_out, v_w_out)}
    for k, (w, g, m, v) in big_w.items():
        d, mn, vn = _adamw(w[0], g, m[0], v[0], "adamw_" + k)
        grads[k], deltas[k], new_m[k], new_v[k] = g[None], d[None], mn[None], vn[None]
    d, mn, vn = _adamw(pad_taps(conv_w), g_conv, pad_taps(m_conv_w), pad_taps(v_conv_w), "adamw_conv_w")
    grads["conv_w"] = g_conv[None, :CONV_K]
    deltas["conv_w"], new_m["conv_w"], new_v["conv_w"] = d[None, :CONV_K], mn[None, :CONV_K], vn[None, :CONV_K]

    order = ("ln_in_g", "ln_in_b", "w_in", "w_sb_proj", "conv_w", "conv_b", "conv_ln_g", "conv_ln_b",
             "w_cv_proj", "w_out", "ln_post_g", "ln_post_b")
    return (loss, grad_x.reshape(x.shape), *[grads[k] for k in order], *[deltas[k] for k in order],
            *[new_m[k] for k in order], *[new_v[k] for k in order])
```

```python
import functools

import jax
import jax.numpy as jnp
from jax import lax
from jax.experimental import pallas as pl
from jax.experimental.pallas import tpu as pltpu

F32 = jnp.float32
BF16 = jnp.bfloat16

D_MODEL = 1024
N_HEADS = 16
HEAD_DIM = 64
HEAD_GROUP = 4
GROUP_W = HEAD_GROUP * HEAD_DIM
N_GROUPS = N_HEADS // HEAD_GROUP
N_PIECES = 9
IN_WIDTH = N_PIECES * D_MODEL
Q_BLOCK = 256
Q_TILE = 2 * Q_BLOCK
CONV_K = 31
CONV_ROWS = 32
HALO = 32
LN_EPS = 1e-5
ALPHA = 2.0 ** 0.25
Q_SCALE = 0.125
N_CHIPS = 4
N_DEV = 8
SHARD_IN = IN_WIDTH // N_CHIPS
SHARD_SQ = D_MODEL // N_CHIPS
HALF = D_MODEL // 2
SMALL_ROWS = 40
N_CHUNKS = 16

ADAM_LR = 0.001
ADAM_B1 = 0.9
ADAM_B2 = 0.999
ADAM_EPS = 1e-08
ADAM_WD = 0.01
ADAM_STEP = 10

MESH = pl.DeviceIdType.MESH
ANY = pl.BlockSpec(memory_space=pl.ANY)
VMEM_LIMIT = 60 * 1024 * 1024

NT = (((1,), (1,)), ((), ()))
TN = (((0,), (0,)), ((), ()))


def _sigmoid(x):
    return 1.0 / (1.0 + jnp.exp(-x))


def _dot(a, b):
    return jnp.dot(a, b, preferred_element_type=F32)


def _sum_along_keys(x, tri_twice):
    return _dot(jnp.concatenate(_split_bf16(x), axis=1), tri_twice)


def _dot_nt(a, b):
    return lax.dot_general(a, b, NT, preferred_element_type=F32)


def _dot_tn(a, b):
    return lax.dot_general(a, b, TN, preferred_element_type=F32)


def _split_bf16(x):
    hi = x.astype(BF16)
    lo = (x - hi.astype(F32)).astype(BF16)
    return hi, lo


def _ln_stats(x):
    mu = jnp.mean(x, axis=-1, keepdims=True)
    xc = x - mu
    var = jnp.mean(xc * xc, axis=-1, keepdims=True)
    rstd = lax.rsqrt(var + LN_EPS)
    return xc * rstd, rstd


def _ln_bwd(dy, xhat, rstd, g):
    dxh = dy * g
    m1 = jnp.mean(dxh, axis=-1, keepdims=True)
    m2 = jnp.mean(dxh * xhat, axis=-1, keepdims=True)
    return rstd * (dxh - m1 - xhat * m2)


def _params(*sem):
    return pltpu.CompilerParams(dimension_semantics=sem, vmem_limit_bytes=VMEM_LIMIT)


def _ln_in(x, g, b):
    t = x.shape[0]
    tm = min(512, t)

    def body(x_ref, g_ref, b_ref, hf_ref, hb_ref):
        xhat, _ = _ln_stats(x_ref[...])
        y = xhat * g_ref[...] + b_ref[...]
        hf_ref[...] = y
        hb_ref[...] = y.astype(BF16)

    row = pl.BlockSpec((tm, D_MODEL), lambda i: (i, 0))
    vec = pl.BlockSpec((1, D_MODEL), lambda i: (0, 0))
    return pl.pallas_call(
        body, grid=(t // tm,), in_specs=[row, vec, vec], out_specs=[row, row],
        out_shape=[jax.ShapeDtypeStruct((t, D_MODEL), F32), jax.ShapeDtypeStruct((t, D_MODEL), BF16)],
        compiler_params=_params("parallel"), name="ln_in")(x, g, b)


def _in_proj(h, w, col0, ncol, out_dtype, scale_first, name):
    t = h.shape[0]
    tm = min(1024, t)

    def body(h_ref, w_ref, o_ref):
        res = _dot(h_ref[...], w_ref[...])
        if scale_first:
            res = res * jnp.where(pl.program_id(0) == 0, Q_SCALE, 1.0)
        o_ref[...] = res.astype(out_dtype)

    return pl.pallas_call(
        body, grid=(ncol, t // tm),
        in_specs=[pl.BlockSpec((tm, D_MODEL), lambda j, i: (i, 0)),
                  pl.BlockSpec((D_MODEL, D_MODEL), lambda j, i: (0, j + col0))],
        out_specs=pl.BlockSpec((tm, D_MODEL), lambda j, i: (i, j)),
        out_shape=jax.ShapeDtypeStruct((t, ncol * D_MODEL), out_dtype),
        compiler_params=_params("parallel", "parallel"), name=name)(h, w)


def _sb_block(l, diag, tri, carry):
    lnb = jnp.minimum(-l, 0.0) - jnp.log(1.0 + jnp.exp(-jnp.abs(l)))
    if diag is not None:
        lnb = jnp.where(diag, lnb, 0.0)
    suffix = _sum_along_keys(lnb, tri)
    a = jnp.exp(l + suffix + carry)
    if diag is not None:
        a = jnp.where(diag, a, 0.0)
    return a, lnb, suffix[:, 0:1]


def _attn_consts():
    head = lax.broadcasted_iota(jnp.int32, (Q_TILE, GROUP_W), 1) // HEAD_DIM
    row = lax.broadcasted_iota(jnp.int32, (Q_TILE, Q_BLOCK), 0)
    col = lax.broadcasted_iota(jnp.int32, (Q_TILE, Q_BLOCK), 1)
    key = row % Q_BLOCK
    return head, col < row, (key >= col).astype(BF16), (key <= col).astype(BF16)


def _split_heads(t, head):
    zero = jnp.zeros_like(t)
    head = head[:t.shape[0]]
    return [jnp.where(head == h, t, zero) for h in range(HEAD_GROUP)]


def _key_block(ref, kb):
    return ref[pl.ds(pl.multiple_of(kb * Q_BLOCK, Q_BLOCK), Q_BLOCK), :]


def _attn_fwd(qkv, nb, seq, w_in_early, shards):
    nq = seq // Q_TILE

    def body(q_ref, k_ref, v_ref, full_in, s0, s1, s2, s3, s4, o_ref, d0, d1, d2, d3, d4, send_sems, recv_sems, local_sems):
        del full_in
        qi = pl.program_id(2)
        first = (pl.program_id(0) == 0) & (pl.program_id(1) == 0) & (qi == 0)
        last = (pl.program_id(0) == nb - 1) & (pl.program_id(1) == N_GROUPS - 1) & (qi == nq - 1)
        start_gather, finish_gather = _late_gather([s0, s1, s2, s3, s4], [d0, d1, d2, d3, d4],
                                                   send_sems, recv_sems, local_sems)
        pl.when(first)(start_gather)
        head, diag, tri, _ = _attn_consts()
        qh = _split_heads(q_ref[...], head)
        o_ref[...] = jnp.zeros_like(o_ref)
        zero_col = jnp.zeros((Q_BLOCK, 1), F32)

        def block(kb, carries, rows, masked):
            kblk = _key_block(k_ref, kb)
            vstack = jnp.concatenate(_split_heads(_key_block(v_ref, kb), head), axis=0)
            mask = diag[:Q_TILE - rows.start] if masked else None
            weights, out = [], []
            for h in range(HEAD_GROUP):
                a, _, blk_sum = _sb_block(_dot_nt(qh[h][rows], kblk), mask, tri, carries[h][rows])
                weights.append(a.astype(BF16))
                out.append(carries[h][rows] + blk_sum)
            o_ref[rows, :] += _dot(jnp.concatenate(weights, axis=1), vstack)
            return out

        second = block(2 * qi + 1, [jnp.zeros((Q_TILE, 1), F32)] * HEAD_GROUP, slice(Q_BLOCK, Q_TILE), True)
        carries = tuple(jnp.concatenate([zero_col, c], axis=0) for c in second)
        everything = slice(0, Q_TILE)
        carries = tuple(block(2 * qi, carries, everything, True))
        lax.fori_loop(1, 2 * qi + 1, lambda jj, cs: tuple(block(2 * qi - jj, cs, everything, False)), carries)
        pl.when(last)(finish_gather)

    n_shards = len(shards)
    full = [jax.ShapeDtypeStruct((D_MODEL, IN_WIDTH), BF16)] + [jax.ShapeDtypeStruct((D_MODEL, D_MODEL), BF16)] * 3 \
        + [jax.ShapeDtypeStruct((CONV_ROWS, D_MODEL), F32)]
    res = pl.pallas_call(
        body, grid=(nb, N_GROUPS, nq),
        in_specs=[pl.BlockSpec((Q_TILE, GROUP_W), lambda b, g, i: (b * nq + i, g)),
                  pl.BlockSpec((seq, GROUP_W), lambda b, g, i: (b, N_GROUPS + g)),
                  pl.BlockSpec((seq, GROUP_W), lambda b, g, i: (b, 2 * N_GROUPS + g))] + [ANY] * (1 + n_shards),
        out_specs=[pl.BlockSpec((Q_TILE, GROUP_W), lambda b, g, i: (b * nq + i, g))] + [ANY] * n_shards,
        out_shape=[jax.ShapeDtypeStruct((nb * seq, D_MODEL), F32)] + full,
        input_output_aliases={3: 1},
        scratch_shapes=[pltpu.SemaphoreType.DMA((3 * n_shards,)), pltpu.SemaphoreType.DMA((3 * n_shards,)),
                        pltpu.SemaphoreType.DMA((n_shards - 1,))],
        compiler_params=_params("arbitrary", "arbitrary", "arbitrary"), name="attn_fwd")(
            qkv, qkv, qkv, w_in_early, *shards)
    return res[0], res[1:]


def _attn_bwd(qkv, do, nb, seq, pair_sq):
    nq = seq // Q_TILE
    n_kb = seq // Q_BLOCK

    def body(q_ref, k_ref, v_ref, do_ref, p0, p1, p2, dq_ref, dk_ref, dv_ref, r0, r1, r2, g_s, beta_s, dq_acc, dk_acc,
             dv_acc, send_sems, recv_sems):
        qi = pl.program_id(2)
        first = (pl.program_id(0) == 0) & (pl.program_id(1) == 0) & (qi == 0)
        last = (pl.program_id(0) == nb - 1) & (pl.program_id(1) == N_GROUPS - 1) & (qi == nq - 1)
        pieces, wholes = _chip_exchange_copies([p0, p1, p2], [r0, r1, r2], [False] * 3, send_sems, recv_sems)

        @pl.when(first)
        def _():
            for cp in pieces:
                cp.start()

        @pl.when(qi == 0)
        def _():
            dk_acc[...] = jnp.zeros_like(dk_acc)
            dv_acc[...] = jnp.zeros_like(dv_acc)

        head, diag, tri_suffix, tri_prefix = _attn_consts()
        qh = _split_heads(q_ref[...], head)
        doh = _split_heads(do_ref[...], head)
        dq_acc[...] = jnp.zeros_like(dq_acc)
        zero_col = jnp.zeros((Q_BLOCK, 1), F32)
        everything = slice(0, Q_TILE)
        second = slice(Q_BLOCK, Q_TILE)

        def block_a(kb, carries, rows, masked):
            ks = pl.multiple_of(kb * Q_BLOCK, Q_BLOCK)
            kblk = _key_block(k_ref, kb)
            vblk = _key_block(v_ref, kb)
            mask = diag[:Q_TILE - rows.start] if masked else None
            weights, out = [], []
            for h in range(HEAD_GROUP):
                l = _dot_nt(qh[h][rows], kblk)
                a, lnb, blk_sum = _sb_block(l, mask, tri_suffix, carries[h][rows])
                beta = jnp.exp(l + lnb)
                if masked:
                    beta = jnp.where(mask, beta, 0.0)
                g_s[h, kb, rows, :] = a * _dot_nt(doh[h][rows], vblk)
                beta_s[h, kb, rows, :] = beta.astype(BF16)
                weights.append(a.astype(BF16))
                out.append(carries[h][rows] + blk_sum)
            dv_acc[pl.ds(ks, Q_BLOCK), :] += _dot_tn(jnp.concatenate(weights, axis=0),
                                                     jnp.concatenate([t[rows] for t in doh], axis=0))
            return out

        part = block_a(2 * qi + 1, [jnp.zeros((Q_TILE, 1), F32)] * HEAD_GROUP, second, True)
        carries = tuple(jnp.concatenate([zero_col, c], axis=0) for c in part)
        carries = tuple(block_a(2 * qi, carries, everything, True))
        lax.fori_loop(1, 2 * qi + 1, lambda jj, cs: tuple(block_a(2 * qi - jj, cs, everything, False)), carries)

        def block_b(kb, prefixes, rows, masked):
            ks = pl.multiple_of(kb * Q_BLOCK, Q_BLOCK)
            kstack = jnp.concatenate(_split_heads(_key_block(k_ref, kb), head), axis=0)
            dls, out = [], []
            for h in range(HEAD_GROUP):
                g = g_s[h, kb, rows, :]
                beta = beta_s[h, kb, rows, :].astype(F32)
                incl = _sum_along_keys(g, tri_prefix)
                before = incl - g + prefixes[h][rows]
                dl = g - beta * (g + before)
                if masked:
                    dl = jnp.where(diag[:Q_TILE - rows.start], dl, 0.0)
                dls.append(dl.astype(BF16))
                out.append(prefixes[h][rows] + incl[:, Q_BLOCK - 1:Q_BLOCK])
            dq_acc[rows, :] += _dot(jnp.concatenate(dls, axis=1), kstack)
            dk_acc[pl.ds(ks, Q_BLOCK), :] += _dot_tn(jnp.concatenate(dls, axis=0),
                                                     jnp.concatenate([t[rows] for t in qh], axis=0))
            return out

        zeros = (jnp.zeros((Q_TILE, 1), F32),) * HEAD_GROUP
        prefixes = lax.fori_loop(0, 2 * qi, lambda kb, ps: tuple(block_b(kb, ps, everything, False)), zeros)
        prefixes = block_b(2 * qi, prefixes, everything, True)
        block_b(2 * qi + 1, prefixes, second, True)
        dq_ref[...] = (dq_acc[...] * Q_SCALE).astype(BF16)

        @pl.when(qi == nq - 1)
        def _():
            dk_ref[...] = dk_acc[...].astype(BF16)
            dv_ref[...] = dv_acc[...].astype(BF16)

        @pl.when(last)
        def _():
            for cp in wholes:
                cp.wait_recv()
            for cp in wholes:
                cp.wait_send()

    t = nb * seq
    qspec = pl.BlockSpec((Q_TILE, GROUP_W), lambda b, g, i: (b * nq + i, g))
    kvout = pl.BlockSpec((seq, GROUP_W), lambda b, g, i: (b, g))
    res = pl.pallas_call(
        body, grid=(nb, N_GROUPS, nq),
        in_specs=[qspec,
                  pl.BlockSpec((seq, GROUP_W), lambda b, g, i: (b, N_GROUPS + g)),
                  pl.BlockSpec((seq, GROUP_W), lambda b, g, i: (b, 2 * N_GROUPS + g)),
                  qspec] + [ANY] * 3,
        out_specs=[qspec, kvout, kvout] + [ANY] * 3,
        out_shape=[jax.ShapeDtypeStruct((t, D_MODEL), BF16)] * 3 + _chip_exchange_shapes([False] * 3),
        scratch_shapes=[pltpu.VMEM((HEAD_GROUP, n_kb, Q_TILE, Q_BLOCK), F32),
                        pltpu.VMEM((HEAD_GROUP, n_kb, Q_TILE, Q_BLOCK), BF16),
                        pltpu.VMEM((Q_TILE, GROUP_W), F32),
                        pltpu.VMEM((seq, GROUP_W), F32), pltpu.VMEM((seq, GROUP_W), F32),
                        pltpu.SemaphoreType.DMA((9,)), pltpu.SemaphoreType.DMA((9,))],
        compiler_params=_params("arbitrary", "arbitrary", "arbitrary"), name="attn_bwd")(qkv, qkv, qkv, do, *pair_sq)
    return res[0], res[1], res[2], res[3:]


def _conv_block_rows(seq):
    return min(256, seq)


SUBLANES = 8
CONV_SUB = 32
CONV_GROUPS = CONV_SUB // SUBLANES


def _fill_shifts(sh_ref, n):
    for r in range(1, SUBLANES):
        sh_ref[r, 0:n, :] = sh_ref[0, pl.ds(r, n), :]


def _shifted(sh_ref, start, cs):
    r = start % SUBLANES
    return sh_ref[r, start - r:start - r + CONV_SUB, cs].reshape(CONV_GROUPS, SUBLANES, -1)


def _tap(w_ref, k, cs):
    row = w_ref[k:k + 1, cs]
    return jnp.broadcast_to(row, (SUBLANES, row.shape[1]))[None]


def _tree_sum(x):
    parts = [x[i] for i in range(x.shape[0])]
    while len(parts) > 1:
        parts = [parts[i] + parts[i + 1] for i in range(0, len(parts), 2)]
    return parts[0]


def _conv_fwd(hrest, conv_w, conv_b, ln_g, ln_b, nb, seq):
    ts = _conv_block_rows(seq)
    nblk = seq // ts
    lanes = 128

    def body(cv_ref, cg_ref, cvh_ref, cgh_ref, z_ref, w_ref, cb_ref, g_ref, b_ref, u1_ref, cvin_ref, ush):
        i = pl.program_id(1)
        halo = cvh_ref[...] * _sigmoid(cgh_ref[...])
        ush[0, 0:HALO, :] = jnp.where(i > 0, halo, 0.0)
        ush[0, HALO:HALO + ts, :] = cv_ref[...] * _sigmoid(cg_ref[...])
        _fill_shifts(ush, HALO + ts - SUBLANES)
        for cc in range(D_MODEL // lanes):
            cs = slice(cc * lanes, (cc + 1) * lanes)
            for r0 in range(0, ts, CONV_SUB):
                acc = jnp.zeros((CONV_GROUPS, SUBLANES, lanes), F32) + _tap(cb_ref, 0, cs)
                for k in range(CONV_K):
                    acc = acc + _tap(w_ref, k, cs) * _shifted(ush, HALO - CONV_K + 1 + k + r0, cs)
                u1_ref[r0:r0 + CONV_SUB, cs] = acc.reshape(CONV_SUB, lanes)
        xhat, _ = _ln_stats(u1_ref[...])
        u2 = xhat * g_ref[...] + b_ref[...]
        z = z_ref[...]
        cvin_ref[...] = (u2 * _sigmoid(u2) * z * _sigmoid(z)).astype(BF16)

    def main(colblk):
        return pl.BlockSpec((ts, D_MODEL), lambda b, i: (b * nblk + i, colblk))

    def halo(colblk):
        return pl.BlockSpec((HALO, D_MODEL),
                            lambda b, i: (jnp.maximum((b * seq + i * ts) // HALO - 1, 0), colblk))

    vec = pl.BlockSpec((1, D_MODEL), lambda b, i: (0, 0))
    t = nb * seq
    return pl.pallas_call(
        body, grid=(nb, nblk),
        in_specs=[main(1), main(2), halo(1), halo(2), main(3),
                  pl.BlockSpec((CONV_ROWS, D_MODEL), lambda b, i: (0, 0)), vec, vec, vec],
        out_specs=[main(0), main(0)],
        out_shape=[jax.ShapeDtypeStruct((t, D_MODEL), F32), jax.ShapeDtypeStruct((t, D_MODEL), BF16)],
        scratch_shapes=[pltpu.VMEM((SUBLANES, HALO + ts, D_MODEL), F32)],
        compiler_params=_params("parallel", "parallel"), name="conv_fwd")(
            hrest, hrest, hrest, hrest, hrest, conv_w, conv_b, ln_g, ln_b)


def _tail(o, hrest, cvin, h0, tgt, w_sb, w_cv, w_out, ln_g, ln_b):
    t = o.shape[0]
    tm = min(256, t)

    def body(o_ref, z_ref, gs_ref, gc_ref, cvin_ref, h0_ref, tgt_ref, wsb_ref, wcv_ref, wout_ref, g_ref, b_ref,
             dr_ref, drb_ref, a_ref, mg_ref, dysb_ref, dycv_ref, do_ref, dz_ref, dgs_ref, dgc_ref, dcvin_ref,
             st_ref):
        @pl.when(pl.program_id(0) == 0)
        def _():
            st_ref[...] = jnp.zeros_like(st_ref)

        o = o_ref[...]
        z = z_ref[...]
        sz = _sigmoid(z)
        a = (o * z * sz).astype(BF16)
        a_ref[...] = a
        y_sb = _dot(a, wsb_ref[...])
        y_cv = _dot(cvin_ref[...], wcv_ref[...])
        s_sb = _sigmoid(gs_ref[...])
        s_cv = _sigmoid(gc_ref[...])
        merged = (s_sb * y_sb + s_cv * y_cv).astype(BF16)
        mg_ref[...] = merged
        r = ALPHA * h0_ref[...] + _dot(merged, wout_ref[...])
        xhat, rstd = _ln_stats(r)
        g = g_ref[...]
        err = xhat * g + b_ref[...] - tgt_ref[...]
        dy = err * (1.0 / D_MODEL)
        st_ref[0:1, :] += jnp.sum(dy * xhat, axis=0, keepdims=True)
        st_ref[1:2, :] += jnp.sum(dy, axis=0, keepdims=True)
        st_ref[2:3, :] += (0.5 / D_MODEL) * jnp.sum(err * err, axis=0, keepdims=True)
        dr = _ln_bwd(dy, xhat, rstd, g)
        dr_ref[...] = dr
        drb = dr.astype(BF16)
        drb_ref[...] = drb
        dm = _dot_nt(drb, wout_ref[...])
        dy_sb = (dm * s_sb).astype(BF16)
        dy_cv = (dm * s_cv).astype(BF16)
        dysb_ref[...] = dy_sb
        dycv_ref[...] = dy_cv
        dgs_ref[...] = (dm * y_sb * s_sb * (1.0 - s_sb)).astype(BF16)
        dgc_ref[...] = (dm * y_cv * s_cv * (1.0 - s_cv)).astype(BF16)
        da = _dot_nt(dy_sb, wsb_ref[...])
        dcvin_ref[...] = _dot_nt(dy_cv, wcv_ref[...])
        do_ref[...] = (da * z * sz).astype(BF16)
        dz_ref[...] = (da * o * sz * (1.0 + z * (1.0 - sz))).astype(BF16)

    def tok(colblk=0):
        return pl.BlockSpec((tm, D_MODEL), lambda i: (i, colblk))

    wspec = pl.BlockSpec((D_MODEL, D_MODEL), lambda i: (0, 0), pipeline_mode=pl.Buffered(1))
    vec = pl.BlockSpec((1, D_MODEL), lambda i: (0, 0))
    bf = jax.ShapeDtypeStruct((t, D_MODEL), BF16)
    f32 = jax.ShapeDtypeStruct((t, D_MODEL), F32)
    return pl.pallas_call(
        body, grid=(t // tm,),
        in_specs=[tok(), tok(0), tok(4), tok(5), tok(), tok(), tok(), wspec, wspec, wspec, vec, vec],
        out_specs=[tok()] * 11 + [pl.BlockSpec((8, D_MODEL), lambda i: (0, 0))],
        out_shape=[f32] + [bf] * 9 + [f32, jax.ShapeDtypeStruct((8, D_MODEL), F32)],
        compiler_params=_params("arbitrary"), name="tail")(
            o, hrest, hrest, hrest, cvin, h0, tgt, w_sb, w_cv, w_out, ln_g, ln_b)


def _conv_branch_bwd(dcvin, u1, hrest, ln_g, ln_b):
    t = u1.shape[0]
    tm = min(512, t)

    def body(dc_ref, u1_ref, z_ref, g_ref, b_ref, du1_ref, dz_ref, st_ref):
        @pl.when(pl.program_id(0) == 0)
        def _():
            st_ref[...] = jnp.zeros_like(st_ref)

        xhat, rstd = _ln_stats(u1_ref[...])
        g = g_ref[...]
        u2 = xhat * g + b_ref[...]
        s2 = _sigmoid(u2)
        z = z_ref[...]
        sz = _sigmoid(z)
        dc = dc_ref[...]
        dz_ref[...] = (dc * (u2 * s2) * sz * (1.0 + z * (1.0 - sz))).astype(BF16)
        du2 = dc * (z * sz) * s2 * (1.0 + u2 * (1.0 - s2))
        st_ref[0:1, :] += jnp.sum(du2 * xhat, axis=0, keepdims=True)
        st_ref[1:2, :] += jnp.sum(du2, axis=0, keepdims=True)
        du1 = _ln_bwd(du2, xhat, rstd, g)
        du1_ref[...] = du1
        st_ref[2:3, :] += jnp.sum(du1, axis=0, keepdims=True)

    tok = pl.BlockSpec((tm, D_MODEL), lambda i: (i, 0))
    vec = pl.BlockSpec((1, D_MODEL), lambda i: (0, 0))
    return pl.pallas_call(
        body, grid=(t // tm,),
        in_specs=[tok, tok, pl.BlockSpec((tm, D_MODEL), lambda i: (i, 3)), vec, vec],
        out_specs=[tok, tok, pl.BlockSpec((8, D_MODEL), lambda i: (0, 0))],
        out_shape=[jax.ShapeDtypeStruct((t, D_MODEL), F32), jax.ShapeDtypeStruct((t, D_MODEL), BF16),
                   jax.ShapeDtypeStruct((8, D_MODEL), F32)],
        compiler_params=_params("arbitrary"), name="conv_branch_bwd")(dcvin, u1, hrest, ln_g, ln_b)


def _conv_bwd(du1, hrest, conv_w, nb, seq):
    ts = _conv_block_rows(seq)
    nblk = seq // ts
    lanes = 128
    last_halo = nb * seq // HALO - 1

    def body(du_ref, duh_ref, cv_ref, cg_ref, cvh_ref, cgh_ref, w_ref, dcv_ref, dcg_ref, dw_ref, ush, dsh, dw_part):
        b = pl.program_id(0)
        i = pl.program_id(1)

        @pl.when((b == 0) & (i == 0))
        def _():
            dw_part[...] = jnp.zeros_like(dw_part)

        cv = cv_ref[...]
        sg = _sigmoid(cg_ref[...])
        halo = cvh_ref[...] * _sigmoid(cgh_ref[...])
        ush[0, 0:HALO, :] = jnp.where(i > 0, halo, 0.0)
        ush[0, HALO:HALO + ts, :] = cv * sg
        dsh[0, 0:ts, :] = du_ref[...]
        dsh[0, ts:ts + HALO, :] = jnp.where(i < nblk - 1, duh_ref[...], 0.0)
        _fill_shifts(ush, HALO + ts - SUBLANES)
        _fill_shifts(dsh, HALO + ts - SUBLANES)
        for cc in range(D_MODEL // lanes):
            cs = slice(cc * lanes, (cc + 1) * lanes)
            for r0 in range(0, ts, CONV_SUB):
                rows = slice(r0, r0 + CONV_SUB)
                dmain = _shifted(dsh, r0, cs)
                acc = jnp.zeros((CONV_GROUPS, SUBLANES, lanes), F32)
                for k in range(CONV_K):
                    acc = acc + _tap(w_ref, k, cs) * _shifted(dsh, CONV_K - 1 - k + r0, cs)
                    dw_part[k, :, cs] += _tree_sum(dmain * _shifted(ush, HALO - CONV_K + 1 + k + r0, cs))
                acc = acc.reshape(CONV_SUB, lanes)
                dcv_ref[rows, cs] = (acc * sg[rows, cs]).astype(BF16)
                dcg_ref[rows, cs] = (acc * cv[rows, cs] * sg[rows, cs] * (1.0 - sg[rows, cs])).astype(BF16)

        @pl.when((b == nb - 1) & (i == nblk - 1))
        def _():
            for k in range(CONV_ROWS):
                dw_ref[k:k + 1, :] = jnp.sum(dw_part[k], axis=0, keepdims=True)

    def main(colblk):
        return pl.BlockSpec((ts, D_MODEL), lambda b, i: (b * nblk + i, colblk))

    def halo_before(colblk):
        return pl.BlockSpec((HALO, D_MODEL),
                            lambda b, i: (jnp.maximum((b * seq + i * ts) // HALO - 1, 0), colblk))

    halo_after = pl.BlockSpec((HALO, D_MODEL),
                              lambda b, i: (jnp.minimum((b * seq + (i + 1) * ts) // HALO, last_halo), 0))
    t = nb * seq
    return pl.pallas_call(
        body, grid=(nb, nblk),
        in_specs=[main(0), halo_after, main(1), main(2), halo_before(1), halo_before(2),
                  pl.BlockSpec((CONV_ROWS, D_MODEL), lambda b, i: (0, 0))],
        out_specs=[main(0), main(0), pl.BlockSpec((CONV_ROWS, D_MODEL), lambda b, i: (0, 0))],
        out_shape=[jax.ShapeDtypeStruct((t, D_MODEL), BF16), jax.ShapeDtypeStruct((t, D_MODEL), BF16),
                   jax.ShapeDtypeStruct((CONV_ROWS, D_MODEL), F32)],
        scratch_shapes=[pltpu.VMEM((SUBLANES, HALO + ts, D_MODEL), F32), pltpu.VMEM((SUBLANES, ts + HALO, D_MODEL), F32),
                        pltpu.VMEM((CONV_ROWS, SUBLANES, D_MODEL), F32)],
        compiler_params=_params("arbitrary", "arbitrary"), name="conv_bwd")(
            du1, du1, hrest, hrest, hrest, hrest, conv_w)


def _weight_grad(x, dys, name):
    n = len(dys)
    t = x.shape[0]
    tm = min(1024, t)

    def body(x_ref, *refs):
        o_ref = refs[n]
        j = pl.program_id(0)

        @pl.when(pl.program_id(1) == 0)
        def _():
            o_ref[...] = jnp.zeros_like(o_ref)

        for p in range(n):
            @pl.when(j == p)
            def _(p=p):
                o_ref[...] += _dot_tn(x_ref[...], refs[p][...])

    def dy_spec(p):
        return pl.BlockSpec((tm, D_MODEL), lambda j, i: (jnp.where(j == p, i, 0), 0))

    return pl.pallas_call(
        body, grid=(n, t // tm),
        in_specs=[pl.BlockSpec((tm, D_MODEL), lambda j, i: (i, 0))] + [dy_spec(p) for p in range(n)],
        out_specs=pl.BlockSpec((D_MODEL, D_MODEL), lambda j, i: (0, j)),
        out_shape=jax.ShapeDtypeStruct((D_MODEL, n * D_MODEL), F32),
        compiler_params=_params("parallel", "arbitrary"), name=name)(x, *dys)


def _in_proj_bwd(dhs, w_in, dr, x, ln_g, pair):
    n = len(dhs)
    t = x.shape[0]
    tm = min(512, t)
    n_i = t // tm

    def body(*refs):
        dh_refs = refs[:n]
        w_ref, dr_ref, x_ref, g_ref = refs[n:n + 4]
        pair_ref, gx_ref, st_ref, got_ref, acc, send_sems, recv_sems = refs[n + 4:]
        p = pl.program_id(0)
        i = pl.program_id(1)
        pieces, wholes = _chip_exchange_copies([pair_ref], [got_ref], [True], send_sems, recv_sems)

        @pl.when((p == 0) & (i == 0))
        def _():
            st_ref[...] = jnp.zeros_like(st_ref)
            for cp in pieces:
                cp.start()

        @pl.when((p == n - 1) & (i == n_i - 1))
        def _():
            for cp in wholes:
                cp.wait_recv()
            for cp in wholes:
                cp.wait_send()

        @pl.when(p == 0)
        def _():
            acc[i] = ALPHA * dr_ref[...]

        for q in range(n):
            @pl.when(p == q)
            def _(q=q):
                acc[i] += _dot_nt(dh_refs[q][...], w_ref[...])

        @pl.when(p == n - 1)
        def _():
            xhat, rstd = _ln_stats(x_ref[...])
            dh0 = acc[i]
            st_ref[0:1, :] += jnp.sum(dh0 * xhat, axis=0, keepdims=True)
            st_ref[1:2, :] += jnp.sum(dh0, axis=0, keepdims=True)
            gx_ref[...] = _ln_bwd(dh0, xhat, rstd, g_ref[...])

    def tok(piece):
        return pl.BlockSpec((tm, D_MODEL), lambda p, i: (jnp.where(p == piece, i, 0), 0))

    return pl.pallas_call(
        body, grid=(n, n_i),
        in_specs=[tok(q) for q in range(n)] + [pl.BlockSpec((D_MODEL, D_MODEL), lambda p, i: (0, p)), tok(0), tok(n - 1),
                                               pl.BlockSpec((1, D_MODEL), lambda p, i: (0, 0)), ANY],
        out_specs=[tok(n - 1), pl.BlockSpec((8, D_MODEL), lambda p, i: (0, 0)), ANY],
        out_shape=[jax.ShapeDtypeStruct((t, D_MODEL), F32), jax.ShapeDtypeStruct((8, D_MODEL), F32)]
        + _chip_exchange_shapes([True]),
        scratch_shapes=[pltpu.VMEM((n_i, tm, D_MODEL), F32), pltpu.SemaphoreType.DMA((3,)),
                        pltpu.SemaphoreType.DMA((3,))],
        compiler_params=_params("arbitrary", "arbitrary"), name="in_proj_bwd")(*dhs, w_in, dr, x, ln_g, pair)


def _forward_backward(x, tgt, nb, seq, shards, vecs):
    w_in_early = _gather_first(shards[0])
    h0, h0b = _ln_in(x, vecs["ln_in_g"], vecs["ln_in_b"])
    qkv = _in_proj(h0b, w_in_early, 0, 3, BF16, True, "in_proj_qkv")
    o, (w_in, w_sb, w_cv, w_out, conv_w) = _attn_fwd(qkv, nb, seq, w_in_early, shards)
    hrest = _in_proj(h0b, w_in, 3, 6, F32, False, "in_proj_rest")
    u1, cvin = _conv_fwd(hrest, conv_w, vecs["conv_b"], vecs["conv_ln_g"], vecs["conv_ln_b"], nb, seq)
    (dr, drb, a, merged, dy_sb, dy_cv, do, dz_sb, dg_sb, dg_cv, dcvin, st_tail) = _tail(
        o, hrest, cvin, h0, tgt, w_sb, w_cv, w_out, vecs["ln_post_g"], vecs["ln_post_b"])
    d_w_sb = _weight_grad(a, [dy_sb], "grad_w_sb")
    d_w_cv = _weight_grad(cvin, [dy_cv], "grad_w_cv")
    d_w_out = _weight_grad(merged, [drb], "grad_w_out")
    pair_sq = _pair_sums([d_w_sb, d_w_cv, d_w_out], [False] * 3, GRAD_NAMES[1:])
    du1, dz_cv, st_conv = _conv_branch_bwd(dcvin, u1, hrest, vecs["conv_ln_g"], vecs["conv_ln_b"])
    dc_val, dc_gate, d_conv_w = _conv_bwd(du1, hrest, conv_w, nb, seq)
    dq, dk, dv, got_sq = _attn_bwd(qkv, do, nb, seq, pair_sq)
    dhs = [dq, dk, dv, dz_sb, dc_val, dc_gate, dz_cv, dg_sb, dg_cv]
    d_w_in = _weight_grad(h0b, dhs, "grad_w_in")
    small = jnp.concatenate([st_conv[2:3], st_conv[0:2], st_tail[0:3], d_conv_w], axis=0)
    return w_in, dhs, dr, d_w_in, pair_sq, got_sq, small


def _place():
    x, y, c = lax.axis_index("x"), lax.axis_index("y"), lax.axis_index("c")
    chips = [(1 - x, y), (x, 1 - y), (1 - x, 1 - y)]
    return x, y, c, chips


N_EARLY = -(-3 * D_MODEL // SHARD_IN)
GATHER_PARTS = 4


def _w_in_slot(full, chip):
    return full.at[:, pl.ds(pl.multiple_of(chip * SHARD_IN, 128), SHARD_IN)]


def _gather_first(w_in):
    part_rows = HALF // GATHER_PARTS
    n_sem = 3 * GATHER_PARTS

    def body(src, dst, ici_send, ici_recv, fwd_send, fwd_recv, local_sem):
        x, y, c, chips = _place()
        me = 2 * x + y
        sibling = (x, y, 1 - c)
        owner = [2 * px + py for px, py in chips]

        def part(ref, k, half):
            return ref.at[pl.ds(pl.multiple_of(half * HALF + k * part_rows, 16), part_rows), :]

        def held(j, k, half):
            return part(_w_in_slot(dst, owner[j]), k, half)

        def sent(j, k):
            px, py = chips[j]
            return _remote(part(src, k, c), part(_w_in_slot(dst, me), k, c), ici_send.at[j * GATHER_PARTS + k],
                           ici_recv.at[j * GATHER_PARTS + k], (px, py, c))

        def landed(j, k):
            px, py = chips[j]
            return _remote(held(j, k, c), held(j, k, c), ici_send.at[j * GATHER_PARTS + k],
                           ici_recv.at[j * GATHER_PARTS + k], (px, py, c))

        def passed(j, k, half):
            return _remote(held(j, k, half), held(j, k, half), fwd_send.at[j * GATHER_PARTS + k],
                           fwd_recv.at[j * GATHER_PARTS + k], sibling)

        own = _w_in_slot(dst, me)
        rows = D_MODEL // N_CHUNKS
        for k in range(N_CHUNKS):
            pltpu.make_async_copy(src.at[pl.ds(k * rows, rows)], own.at[pl.ds(k * rows, rows)], local_sem).start()
        local = pltpu.make_async_copy(src, own, local_sem)

        @pl.when(me < N_EARLY)
        def _():
            for j in range(3):
                for k in range(GATHER_PARTS):
                    sent(j, k).start()

        for j in range(3):
            @pl.when(owner[j] < N_EARLY)
            def _(j=j):
                for k in range(GATHER_PARTS):
                    landed(j, k).wait_recv()
                    passed(j, k, c).start()

        for j in range(3):
            @pl.when(owner[j] < N_EARLY)
            def _(j=j):
                for k in range(GATHER_PARTS):
                    passed(j, k, 1 - c).wait_recv()
                for k in range(GATHER_PARTS):
                    passed(j, k, c).wait_send()

        @pl.when(me < N_EARLY)
        def _():
            for j in range(3):
                for k in range(GATHER_PARTS):
                    sent(j, k).wait_send()

        local.wait()

    return pl.pallas_call(
        body, in_specs=[ANY], out_specs=ANY, out_shape=jax.ShapeDtypeStruct((D_MODEL, IN_WIDTH), BF16),
        scratch_shapes=[pltpu.SemaphoreType.DMA((n_sem,)), pltpu.SemaphoreType.DMA((n_sem,)),
                        pltpu.SemaphoreType.DMA((n_sem,)), pltpu.SemaphoreType.DMA((n_sem,)),
                        pltpu.SemaphoreType.DMA],
        name="gather_first")(w_in)


def _late_gather(srcs, dsts, send_sems, recv_sems, local_sems):
    x, y, c, chips = _place()
    me = 2 * x + y
    small = range(1, 5)

    def slot(a, chip):
        if a == 0:
            return _w_in_slot(dsts[0], chip)
        if a == 4:
            return dsts[a].at[:, pl.ds(pl.multiple_of(chip * SHARD_SQ, 128), SHARD_SQ)]
        return dsts[a].at[pl.ds(pl.multiple_of(chip * SHARD_SQ, 16), SHARD_SQ), :]

    def sems(a, j):
        return send_sems.at[3 * a + j], recv_sems.at[3 * a + j]

    def whole(a, j):
        px, py = chips[j]
        return _remote(srcs[a], slot(a, me), *sems(a, j), (px, py, c))

    def arrival(a, j):
        px, py = chips[j]
        return _arrival(slot(a, 2 * px + py), *sems(a, j), (px, py, c))

    def local(a):
        return pltpu.make_async_copy(srcs[a], slot(a, me), local_sems.at[a - 1])

    def start():
        for a in small:
            local(a).start()
            for j in range(3):
                whole(a, j).start()

        @pl.when(me >= N_EARLY)
        def _():
            for j, (px, py) in enumerate(chips):
                for cp in _chunk_copies(srcs[0], slot(0, me), GATHER_PARTS, *sems(0, j), (px, py, c)):
                    cp.start()

    def finish():
        for j, (px, py) in enumerate(chips):
            for a in small:
                arrival(a, j).wait_recv()

            @pl.when(2 * px + py >= N_EARLY)
            def _(j=j):
                arrival(0, j).wait_recv()

        for a in small:
            for j in range(3):
                whole(a, j).wait_send()
            local(a).wait()

        @pl.when(me >= N_EARLY)
        def _():
            for j in range(3):
                whole(0, j).wait_send()

    return start, finish


def _remote(src, dst, send_sem, recv_sem, device):
    return pltpu.make_async_remote_copy(src_ref=src, dst_ref=dst, send_sem=send_sem, recv_sem=recv_sem,
                                        device_id=device, device_id_type=MESH)


def _chunk_copies(src, dst, n, send_sem, recv_sem, device):
    rows = src.shape[0] // n
    return [_remote(src.at[pl.ds(k * rows, rows)], dst.at[pl.ds(k * rows, rows)], send_sem, recv_sem, device)
            for k in range(n)]


def _row_chunks(src, dst, n, send_sem, recv_sem, device):
    return _chunk_copies(src, dst, n, send_sem, recv_sem, device), _remote(src, dst, send_sem, recv_sem, device)


def _arrival(dst, send_sem, recv_sem, device):
    return _remote(dst, dst, send_sem, recv_sem, device)


def _pair_exchange(grads, by_rows, name):
    n = len(grads)

    def body(*refs):
        srcs, theirs = refs[:n], refs[n:2 * n]
        send_sems, recv_sems = refs[2 * n:]
        x, y, c, _ = _place()
        sibling = (x, y, 1 - c)
        start = pl.multiple_of((1 - c) * HALF, 128)
        pieces, wholes = [], []
        for a in range(n):
            view = srcs[a].at[pl.ds(start, HALF), :] if by_rows[a] else srcs[a].at[:, pl.ds(start, HALF)]
            cps, whole = _row_chunks(view, theirs[a], N_CHUNKS, send_sems.at[a], recv_sems.at[a], sibling)
            pieces += cps
            wholes.append(whole)
        for cp in pieces:
            cp.start()
        for cp in wholes:
            cp.wait_recv()
        for cp in wholes:
            cp.wait_send()

    shapes = [jax.ShapeDtypeStruct((HALF, IN_WIDTH) if rows else (D_MODEL, HALF), F32) for rows in by_rows]
    return pl.pallas_call(
        body, in_specs=[ANY] * n, out_specs=[ANY] * n, out_shape=shapes,
        scratch_shapes=[pltpu.SemaphoreType.DMA((n,)), pltpu.SemaphoreType.DMA((n,))], name=name)(*grads)


def _chip_exchange_shapes(by_rows):
    return [jax.ShapeDtypeStruct((3, HALF, SHARD_IN) if rows else (3, SHARD_SQ, HALF), BF16) for rows in by_rows]


def _chip_exchange_copies(srcs, theirs, by_rows, send_sems, recv_sems):
    _, _, c, chips = _place()

    def shard(a, px, py):
        chip = 2 * px + py
        if by_rows[a]:
            return srcs[a].at[:, pl.ds(pl.multiple_of(chip * SHARD_IN, 128), SHARD_IN)]
        return srcs[a].at[pl.ds(pl.multiple_of(chip * SHARD_SQ, 16), SHARD_SQ), :]

    pieces, wholes = [], []
    for j, (px, py) in enumerate(chips):
        for a in range(len(srcs)):
            cps, whole = _row_chunks(shard(a, px, py), theirs[a].at[j], 4, send_sems.at[3 * a + j],
                                     recv_sems.at[3 * a + j], (px, py, c))
            pieces += cps
            wholes.append(whole)
    return pieces, wholes


def _pair_share(f_in, f_sb, f_cv, f_out):
    def body(i0, i1, i2, i3, o0, o1, o2, o3, send_sems, recv_sems):
        del i0, i1, i2, i3
        x, y, c, _ = _place()
        dsts = [o0, o1, o2, o3]
        sibling = (x, y, 1 - c)

        def half(a, which):
            start = pl.multiple_of(which * HALF, 128)
            if a == 0:
                return dsts[a].at[pl.ds(start, HALF), :]
            return dsts[a].at[:, pl.ds(start, HALF)]

        pieces, sent, arrived = [], [], []
        for a in range(4):
            cps, whole = _row_chunks(half(a, c), half(a, c), N_CHUNKS, send_sems.at[a], recv_sems.at[a], sibling)
            pieces += cps
            sent.append(whole)
            arrived.append(_arrival(half(a, 1 - c), send_sems.at[a], recv_sems.at[a], sibling))
        for cp in pieces:
            cp.start()
        for cp in arrived:
            cp.wait_recv()
        for cp in sent:
            cp.wait_send()

    return pl.pallas_call(
        body, in_specs=[ANY] * 4, out_specs=[ANY] * 4,
        out_shape=[jax.ShapeDtypeStruct((D_MODEL, SHARD_IN), F32)] + [jax.ShapeDtypeStruct((SHARD_SQ, D_MODEL), F32)] * 3,
        input_output_aliases={0: 0, 1: 1, 2: 2, 3: 3},
        scratch_shapes=[pltpu.SemaphoreType.DMA((4,)), pltpu.SemaphoreType.DMA((4,))],
        name="pair_share")(f_in, f_sb, f_cv, f_out)


def _small_allreduce(small):
    def body(s_ref, o_ref, slots, send_sems, recv_sems):
        x, y, c, _ = _place()
        me = 4 * x + 2 * y + c
        slots[me] = s_ref[...]
        sends = []
        for k in range(1, N_DEV):
            px, py, pc = (x + (k >> 2)) % 2, (y + ((k >> 1) & 1)) % 2, (c + (k & 1)) % 2
            sends.append(pltpu.make_async_remote_copy(
                src_ref=s_ref, dst_ref=slots.at[me], send_sem=send_sems.at[k - 1], recv_sem=recv_sems.at[k - 1],
                device_id=(px, py, pc), device_id_type=MESH))
        for cp in sends:
            cp.start()
        for k in range(1, N_DEV):
            px, py, pc = (x + (k >> 2)) % 2, (y + ((k >> 1) & 1)) % 2, (c + (k & 1)) % 2
            pltpu.make_async_remote_copy(
                src_ref=s_ref, dst_ref=slots.at[4 * px + 2 * py + pc], send_sem=send_sems.at[k - 1],
                recv_sem=recv_sems.at[k - 1], device_id=(px, py, pc), device_id_type=MESH).wait_recv()
        for cp in sends:
            cp.wait_send()
        total = slots[0]
        for d in range(1, N_DEV):
            total = total + slots[d]
        o_ref[...] = total
        o_ref[7:8, :] = jnp.zeros((1, D_MODEL), F32) + jnp.sum(total[7:8, :], axis=1, keepdims=True)

    return pl.pallas_call(
        body, in_specs=[pl.BlockSpec(memory_space=pltpu.VMEM)], out_specs=pl.BlockSpec(memory_space=pltpu.VMEM),
        out_shape=jax.ShapeDtypeStruct((SMALL_ROWS, D_MODEL), F32),
        scratch_shapes=[pltpu.VMEM((N_DEV, SMALL_ROWS, D_MODEL), F32), pltpu.SemaphoreType.DMA((N_DEV - 1,)),
                        pltpu.SemaphoreType.DMA((N_DEV - 1,))],
        name="small_allreduce")(small)


SUM_BLOCKS = 8


def _pair_sum(place, full, theirs, by_rows, name):
    rows, cols = theirs.shape
    tr = rows // SUM_BLOCKS

    def body(p_ref, a_ref, b_ref, o_ref):
        del p_ref
        o_ref[...] = (a_ref[...] + b_ref[...]).astype(BF16)

    mine = (lambda i, p: (p[1] * SUM_BLOCKS + i, 0)) if by_rows else (lambda i, p: (i, p[1]))
    spec = pl.BlockSpec((tr, cols), lambda i, p: (i, 0))
    return pl.pallas_call(
        body, out_shape=jax.ShapeDtypeStruct(theirs.shape, BF16),
        grid_spec=pltpu.PrefetchScalarGridSpec(num_scalar_prefetch=1, grid=(SUM_BLOCKS,),
                                               in_specs=[pl.BlockSpec((tr, cols), mine), spec], out_specs=spec),
        compiler_params=_params("parallel"), name=name)(place, full, theirs)


def _chip_sum(place, pair, got, by_rows, name):
    _, rows, cols = got.shape
    tr = rows // SUM_BLOCKS

    def body(p_ref, a_ref, g_ref, o_ref):
        del p_ref
        o_ref[...] = ((a_ref[...].astype(F32) + g_ref[0].astype(F32)) + g_ref[1].astype(F32)) + g_ref[2].astype(F32)

    if by_rows:
        own = lambda i, p: (i, p[0])
        out = lambda i, p: (p[1] * SUM_BLOCKS + i, 0)
        full = (2 * rows, cols)
    else:
        own = lambda i, p: (p[0] * SUM_BLOCKS + i, 0)
        out = lambda i, p: (i, p[1])
        full = (rows, 2 * cols)
    return pl.pallas_call(
        body, out_shape=jax.ShapeDtypeStruct(full, F32),
        grid_spec=pltpu.PrefetchScalarGridSpec(
            num_scalar_prefetch=1, grid=(SUM_BLOCKS,),
            in_specs=[pl.BlockSpec((tr, cols), own), pl.BlockSpec((3, tr, cols), lambda i, p: (0, i, 0))],
            out_specs=pl.BlockSpec((tr, cols), out)),
        compiler_params=_params("parallel"), name=name)(place, pair, got)


def _adamw(w, g, m, v, name):
    rows, cols = w.shape
    tr = rows // 4 if rows % 32 == 0 else rows
    c1 = 1.0 - ADAM_B1 ** ADAM_STEP
    c2 = 1.0 - ADAM_B2 ** ADAM_STEP

    def body(w_ref, g_ref, m_ref, v_ref, d_ref, mo_ref, vo_ref):
        g = g_ref[...]
        mn = ADAM_B1 * m_ref[...] + (1.0 - ADAM_B1) * g
        vn = ADAM_B2 * v_ref[...] + (1.0 - ADAM_B2) * (g * g)
        mo_ref[...] = mn
        vo_ref[...] = vn
        d_ref[...] = -ADAM_LR * ((mn / c1) / (jnp.sqrt(vn / c2) + ADAM_EPS) + ADAM_WD * w_ref[...])

    spec = pl.BlockSpec((tr, cols), lambda i: (i, 0))
    shape = jax.ShapeDtypeStruct(w.shape, F32)
    return pl.pallas_call(body, grid=(rows // tr,), in_specs=[spec] * 4, out_specs=[spec] * 3,
                          out_shape=[shape] * 3, compiler_params=_params("parallel"), name=name)(w, g, m, v)


GRAD_NAMES = ("in", "sb", "cv", "out")


def _place_scalars():
    return jnp.stack([2 * lax.axis_index("x") + lax.axis_index("y"), lax.axis_index("c")]).astype(jnp.int32)


def _pair_sums(grads, by_rows, names):
    place = _place_scalars()
    theirs = _pair_exchange(grads, by_rows, "pair_exchange_" + names[0])
    return [_pair_sum(place, g, t, rows, "pair_sum_" + nm) for g, t, rows, nm in zip(grads, theirs, by_rows, names)]


def _finish_weight_grads(pair, got):
    place = _place_scalars()
    done = [_chip_sum(place, pair[a], got[a], a == 0, "chip_sum_" + GRAD_NAMES[a]) for a in range(4)]
    return _pair_share(*done)


def kernel(x, ln_in_g, ln_in_b, w_in, w_sb_proj, conv_w, conv_b, conv_ln_g, conv_ln_b, w_cv_proj, w_out, ln_post_g, ln_post_b, loss_target, m_ln_in_g, m_ln_in_b, m_w_in, m_w_sb_proj, m_conv_w, m_conv_b, m_conv_ln_g, m_conv_ln_b, m_w_cv_proj, m_w_out, m_ln_post_g, m_ln_post_b, v_ln_in_g, v_ln_in_b, v_w_in, v_w_sb_proj, v_conv_w, v_conv_b, v_conv_ln_g, v_conv_ln_b, v_w_cv_proj, v_w_out, v_ln_post_g, v_ln_post_b):
    nb, seq, _ = x.shape
    t = nb * seq
    vec_names = ("ln_in_g", "ln_in_b", "conv_b", "conv_ln_g", "conv_ln_b", "ln_post_g", "ln_post_b")
    vec_w = dict(zip(vec_names, (ln_in_g, ln_in_b, conv_b, conv_ln_g, conv_ln_b, ln_post_g, ln_post_b)))
    vec_m = dict(zip(vec_names, (m_ln_in_g, m_ln_in_b, m_conv_b, m_conv_ln_g, m_conv_ln_b, m_ln_post_g, m_ln_post_b)))
    vec_v = dict(zip(vec_names, (v_ln_in_g, v_ln_in_b, v_conv_b, v_conv_ln_g, v_conv_ln_b, v_ln_post_g, v_ln_post_b)))
    vecs = {k: a.reshape(1, D_MODEL) for k, a in vec_w.items()}

    pad_taps = lambda a: jnp.pad(a.reshape(CONV_K, SHARD_SQ), ((0, CONV_ROWS - CONV_K), (0, 0)))
    shards = (w_in[0].astype(BF16), w_sb_proj[0].astype(BF16), w_cv_proj[0].astype(BF16), w_out[0].astype(BF16),
              pad_taps(conv_w))
    x2 = x.reshape(t, D_MODEL)
    full_in, dhs, dr, d_w_in, pair_sq, got_sq, small = _forward_backward(
        x2, loss_target.reshape(t, D_MODEL), nb, seq, shards, vecs)
    pair_in = _pair_sums([d_w_in], [True], GRAD_NAMES[:1])[0]
    grad_x, st_in, got_in = _in_proj_bwd(dhs, full_in, dr, x2, vecs["ln_in_g"], pair_in)
    g_in, g_sb, g_cv, g_out = _finish_weight_grads([pair_in] + pair_sq, [got_in] + list(got_sq))
    small = _small_allreduce(jnp.concatenate([st_in[0:2], small], axis=0))

    chip = 2 * lax.axis_index("x") + lax.axis_index("y")
    g_conv = lax.dynamic_slice(small, (8, chip * SHARD_SQ), (CONV_ROWS, SHARD_SQ))
    loss = small[7, 0]

    grads, deltas, new_m, new_v = {}, {}, {}, {}
    stack = lambda d: jnp.concatenate([d[k].reshape(1, D_MODEL) for k in vec_names] + [jnp.zeros((1, D_MODEL), F32)])
    vd, vm, vv = _adamw(stack(vec_w), small[0:8], stack(vec_m), stack(vec_v), "adamw_vectors")
    for r, k in enumerate(vec_names):
        shape = vec_w[k].shape
        grads[k] = small[r].reshape(shape)
        deltas[k], new_m[k], new_v[k] = vd[r].reshape(shape), vm[r].reshape(shape), vv[r].reshape(shape)

    big_w = {"w_in": (w_in, g_in, m_w_in, v_w_in), "w_sb_proj": (w_sb_proj, g_sb, m_w_sb_proj, v_w_sb_proj),
             "w_cv_proj": (w_cv_proj, g_cv, m_w_cv_proj, v_w_cv_proj), "w_out": (w_out, g_out, m_w_out, v_w_out)}
    for k, (w, g, m, v) in big_w.items():
        d, mn, vn = _adamw(w[0], g, m[0], v[0], "adamw_" + k)
        grads[k], deltas[k], new_m[k], new_v[k] = g[None], d[None], mn[None], vn[None]
    d, mn, vn = _adamw(pad_taps(conv_w), g_conv, pad_taps(m_conv_w), pad_taps(v_conv_w), "adamw_conv_w")
    grads["conv_w"] = g_conv[None, :CONV_K]
    deltas["conv_w"], new_m["conv_w"], new_v["conv_w"] = d[None, :CONV_K], mn[None, :CONV_K], vn[None, :CONV_K]

    order = ("ln_in_g", "ln_in_b", "w_in", "w_sb_proj", "conv_w", "conv_b", "conv_ln_g", "conv_ln_b",
             "w_cv_proj", "w_out", "ln_post_g", "ln_post_b")
    return (loss, grad_x.reshape(x.shape), *[grads[k] for k in order], *[deltas[k] for k in order],
            *[new_m[k] for k in order], *[new_v[k] for k in order])
```

```python
import functools

import jax
import jax.numpy as jnp
from jax import lax
from jax.experimental import pallas as pl
from jax.experimental.pallas import tpu as pltpu

F32 = jnp.float32
BF16 = jnp.bfloat16

D_MODEL = 1024
N_HEADS = 16
HEAD_DIM = 64
HEAD_GROUP = 4
GROUP_W = HEAD_GROUP * HEAD_DIM
N_GROUPS = N_HEADS // HEAD_GROUP
N_PIECES = 9
IN_WIDTH = N_PIECES * D_MODEL
Q_BLOCK = 256
Q_TILE = 2 * Q_BLOCK
CONV_K = 31
CONV_ROWS = 32
HALO = 32
LN_EPS = 1e-5
ALPHA = 2.0 ** 0.25
Q_SCALE = 0.125
N_CHIPS = 4
N_DEV = 8
SHARD_IN = IN_WIDTH // N_CHIPS
SHARD_SQ = D_MODEL // N_CHIPS
HALF = D_MODEL // 2
SMALL_ROWS = 40
N_CHUNKS = 16

ADAM_LR = 0.001
ADAM_B1 = 0.9
ADAM_B2 = 0.999
ADAM_EPS = 1e-08
ADAM_WD = 0.01
ADAM_STEP = 10

MESH = pl.DeviceIdType.MESH
ANY = pl.BlockSpec(memory_space=pl.ANY)
VMEM_LIMIT = 60 * 1024 * 1024

NT = (((1,), (1,)), ((), ()))
TN = (((0,), (0,)), ((), ()))


def _sigmoid(x):
    return 1.0 / (1.0 + jnp.exp(-x))


def _dot(a, b):
    return jnp.dot(a, b, preferred_element_type=F32)


def _sum_along_keys(x, tri_twice):
    return _dot(jnp.concatenate(_split_bf16(x), axis=1), tri_twice)


def _dot_nt(a, b):
    return lax.dot_general(a, b, NT, preferred_element_type=F32)


def _dot_tn(a, b):
    return lax.dot_general(a, b, TN, preferred_element_type=F32)


def _split_bf16(x):
    hi = x.astype(BF16)
    lo = (x - hi.astype(F32)).astype(BF16)
    return hi, lo


def _ln_stats(x):
    mu = jnp.mean(x, axis=-1, keepdims=True)
    xc = x - mu
    var = jnp.mean(xc * xc, axis=-1, keepdims=True)
    rstd = lax.rsqrt(var + LN_EPS)
    return xc * rstd, rstd


def _ln_bwd(dy, xhat, rstd, g):
    dxh = dy * g
    m1 = jnp.mean(dxh, axis=-1, keepdims=True)
    m2 = jnp.mean(dxh * xhat, axis=-1, keepdims=True)
    return rstd * (dxh - m1 - xhat * m2)


def _params(*sem):
    return pltpu.CompilerParams(dimension_semantics=sem, vmem_limit_bytes=VMEM_LIMIT)


def _ln_in(x, g, b):
    t = x.shape[0]
    tm = min(512, t)

    def body(x_ref, g_ref, b_ref, hf_ref, hb_ref):
        xhat, _ = _ln_stats(x_ref[...])
        y = xhat * g_ref[...] + b_ref[...]
        hf_ref[...] = y
        hb_ref[...] = y.astype(BF16)

    row = pl.BlockSpec((tm, D_MODEL), lambda i: (i, 0))
    vec = pl.BlockSpec((1, D_MODEL), lambda i: (0, 0))
    return pl.pallas_call(
        body, grid=(t // tm,), in_specs=[row, vec, vec], out_specs=[row, row],
        out_shape=[jax.ShapeDtypeStruct((t, D_MODEL), F32), jax.ShapeDtypeStruct((t, D_MODEL), BF16)],
        compiler_params=_params("parallel"), name="ln_in")(x, g, b)


def _in_proj(h, w, col0, ncol, out_dtype, scale_first, name):
    t = h.shape[0]
    tm = min(1024, t)

    def body(h_ref, w_ref, o_ref):
        res = _dot(h_ref[...], w_ref[...])
        if scale_first:
            res = res * jnp.where(pl.program_id(0) == 0, Q_SCALE, 1.0)
        o_ref[...] = res.astype(out_dtype)

    return pl.pallas_call(
        body, grid=(ncol, t // tm),
        in_specs=[pl.BlockSpec((tm, D_MODEL), lambda j, i: (i, 0)),
                  pl.BlockSpec((D_MODEL, D_MODEL), lambda j, i: (0, j + col0))],
        out_specs=pl.BlockSpec((tm, D_MODEL), lambda j, i: (i, j)),
        out_shape=jax.ShapeDtypeStruct((t, ncol * D_MODEL), out_dtype),
        compiler_params=_params("parallel", "parallel"), name=name)(h, w)


def _sb_block(l, diag, tri, carry):
    lnb = jnp.minimum(-l, 0.0) - jnp.log(1.0 + jnp.exp(-jnp.abs(l)))
    if diag is not None:
        lnb = jnp.where(diag, lnb, 0.0)
    suffix = _sum_along_keys(lnb, tri)
    a = jnp.exp(l + suffix + carry)
    if diag is not None:
        a = jnp.where(diag, a, 0.0)
    return a, lnb, suffix[:, 0:1]


def _attn_consts():
    head = lax.broadcasted_iota(jnp.int32, (Q_TILE, GROUP_W), 1) // HEAD_DIM
    row = lax.broadcasted_iota(jnp.int32, (Q_TILE, Q_BLOCK), 0)
    col = lax.broadcasted_iota(jnp.int32, (Q_TILE, Q_BLOCK), 1)
    key = row % Q_BLOCK
    return head, col < row, (key >= col).astype(BF16), (key <= col).astype(BF16)


def _split_heads(t, head):
    zero = jnp.zeros_like(t)
    head = head[:t.shape[0]]
    return [jnp.where(head == h, t, zero) for h in range(HEAD_GROUP)]


def _key_block(ref, kb):
    return ref[pl.ds(pl.multiple_of(kb * Q_BLOCK, Q_BLOCK), Q_BLOCK), :]


def _attn_fwd(qkv, nb, seq, w_in_early, shards):
    nq = seq // Q_TILE

    def body(q_ref, k_ref, v_ref, full_in, s0, s1, s2, s3, s4, o_ref, d0, d1, d2, d3, d4, send_sems, recv_sems, local_sems):
        del full_in
        qi = pl.program_id(2)
        first = (pl.program_id(0) == 0) & (pl.program_id(1) == 0) & (qi == 0)
        last = (pl.program_id(0) == nb - 1) & (pl.program_id(1) == N_GROUPS - 1) & (qi == nq - 1)
        start_gather, finish_gather = _late_gather([s0, s1, s2, s3, s4], [d0, d1, d2, d3, d4],
                                                   send_sems, recv_sems, local_sems)
        pl.when(first)(start_gather)
        head, diag, tri, _ = _attn_consts()
        qh = _split_heads(q_ref[...], head)
        o_ref[...] = jnp.zeros_like(o_ref)
        zero_col = jnp.zeros((Q_BLOCK, 1), F32)

        def block(kb, carries, rows, masked):
            kblk = _key_block(k_ref, kb)
            vstack = jnp.concatenate(_split_heads(_key_block(v_ref, kb), head), axis=0)
            mask = diag[:Q_TILE - rows.start] if masked else None
            weights, out = [], []
            for h in range(HEAD_GROUP):
                a, _, blk_sum = _sb_block(_dot_nt(qh[h][rows], kblk), mask, tri, carries[h][rows])
                weights.append(a.astype(BF16))
                out.append(carries[h][rows] + blk_sum)
            o_ref[rows, :] += _dot(jnp.concatenate(weights, axis=1), vstack)
            return out

        second = block(2 * qi + 1, [jnp.zeros((Q_TILE, 1), F32)] * HEAD_GROUP, slice(Q_BLOCK, Q_TILE), True)
        carries = tuple(jnp.concatenate([zero_col, c], axis=0) for c in second)
        everything = slice(0, Q_TILE)
        carries = tuple(block(2 * qi, carries, everything, True))
        lax.fori_loop(1, 2 * qi + 1, lambda jj, cs: tuple(block(2 * qi - jj, cs, everything, False)), carries)
        pl.when(last)(finish_gather)

    n_shards = len(shards)
    full = [jax.ShapeDtypeStruct((D_MODEL, IN_WIDTH), BF16)] + [jax.ShapeDtypeStruct((D_MODEL, D_MODEL), BF16)] * 3 \
        + [jax.ShapeDtypeStruct((CONV_ROWS, D_MODEL), F32)]
    res = pl.pallas_call(
        body, grid=(nb, N_GROUPS, nq),
        in_specs=[pl.BlockSpec((Q_TILE, GROUP_W), lambda b, g, i: (b * nq + i, g)),
                  pl.BlockSpec((seq, GROUP_W), lambda b, g, i: (b, N_GROUPS + g)),
                  pl.BlockSpec((seq, GROUP_W), lambda b, g, i: (b, 2 * N_GROUPS + g))] + [ANY] * (1 + n_shards),
        out_specs=[pl.BlockSpec((Q_TILE, GROUP_W), lambda b, g, i: (b * nq + i, g))] + [ANY] * n_shards,
        out_shape=[jax.ShapeDtypeStruct((nb * seq, D_MODEL), F32)] + full,
        input_output_aliases={3: 1},
        scratch_shapes=[pltpu.SemaphoreType.DMA((3 * n_shards,)), pltpu.SemaphoreType.DMA((3 * n_shards,)),
                        pltpu.SemaphoreType.DMA((n_shards - 1,))],
        compiler_params=_params("arbitrary", "arbitrary", "arbitrary"), name="attn_fwd")(
            qkv, qkv, qkv, w_in_early, *shards)
    return res[0], res[1:]


def _attn_bwd(qkv, do, nb, seq):
    nq = seq // Q_TILE
    n_kb = seq // Q_BLOCK

    def body(q_ref, k_ref, v_ref, do_ref, dq_ref, dk_ref, dv_ref, g_s, beta_s, dq_acc, dk_acc, dv_acc):
        qi = pl.program_id(2)

        @pl.when(qi == 0)
        def _():
            dk_acc[...] = jnp.zeros_like(dk_acc)
            dv_acc[...] = jnp.zeros_like(dv_acc)

        head, diag, tri_suffix, tri_prefix = _attn_consts()
        qh = _split_heads(q_ref[...], head)
        doh = _split_heads(do_ref[...], head)
        dq_acc[...] = jnp.zeros_like(dq_acc)
        zero_col = jnp.zeros((Q_BLOCK, 1), F32)
        everything = slice(0, Q_TILE)
        second = slice(Q_BLOCK, Q_TILE)

        def block_a(kb, carries, rows, masked):
            ks = pl.multiple_of(kb * Q_BLOCK, Q_BLOCK)
            kblk = _key_block(k_ref, kb)
            vblk = _key_block(v_ref, kb)
            mask = diag[:Q_TILE - rows.start] if masked else None
            weights, out = [], []
            for h in range(HEAD_GROUP):
                l = _dot_nt(qh[h][rows], kblk)
                a, lnb, blk_sum = _sb_block(l, mask, tri_suffix, carries[h][rows])
                beta = jnp.exp(l + lnb)
                if masked:
                    beta = jnp.where(mask, beta, 0.0)
                g_s[h, kb, rows, :] = a * _dot_nt(doh[h][rows], vblk)
                beta_s[h, kb, rows, :] = beta.astype(BF16)
                weights.append(a.astype(BF16))
                out.append(carries[h][rows] + blk_sum)
            dv_acc[pl.ds(ks, Q_BLOCK), :] += _dot_tn(jnp.concatenate(weights, axis=0),
                                                     jnp.concatenate([t[rows] for t in doh], axis=0))
            return out

        part = block_a(2 * qi + 1, [jnp.zeros((Q_TILE, 1), F32)] * HEAD_GROUP, second, True)
        carries = tuple(jnp.concatenate([zero_col, c], axis=0) for c in part)
        carries = tuple(block_a(2 * qi, carries, everything, True))
        lax.fori_loop(1, 2 * qi + 1, lambda jj, cs: tuple(block_a(2 * qi - jj, cs, everything, False)), carries)

        def block_b(kb, prefixes, rows, masked):
            ks = pl.multiple_of(kb * Q_BLOCK, Q_BLOCK)
            kstack = jnp.concatenate(_split_heads(_key_block(k_ref, kb), head), axis=0)
            dls, out = [], []
            for h in range(HEAD_GROUP):
                g = g_s[h, kb, rows, :]
                beta = beta_s[h, kb, rows, :].astype(F32)
                incl = _sum_along_keys(g, tri_prefix)
                before = incl - g + prefixes[h][rows]
                dl = g - beta * (g + before)
                if masked:
                    dl = jnp.where(diag[:Q_TILE - rows.start], dl, 0.0)
                dls.append(dl.astype(BF16))
                out.append(prefixes[h][rows] + incl[:, Q_BLOCK - 1:Q_BLOCK])
            dq_acc[rows, :] += _dot(jnp.concatenate(dls, axis=1), kstack)
            dk_acc[pl.ds(ks, Q_BLOCK), :] += _dot_tn(jnp.concatenate(dls, axis=0),
                                                     jnp.concatenate([t[rows] for t in qh], axis=0))
            return out

        zeros = (jnp.zeros((Q_TILE, 1), F32),) * HEAD_GROUP
        prefixes = lax.fori_loop(0, 2 * qi, lambda kb, ps: tuple(block_b(kb, ps, everything, False)), zeros)
        prefixes = block_b(2 * qi, prefixes, everything, True)
        block_b(2 * qi + 1, prefixes, second, True)
        dq_ref[...] = (dq_acc[...] * Q_SCALE).astype(BF16)

        @pl.when(qi == nq - 1)
        def _():
            dk_ref[...] = dk_acc[...].astype(BF16)
            dv_ref[...] = dv_acc[...].astype(BF16)

    t = nb * seq
    qspec = pl.BlockSpec((Q_TILE, GROUP_W), lambda b, g, i: (b * nq + i, g))
    kvout = pl.BlockSpec((seq, GROUP_W), lambda b, g, i: (b, g))
    return pl.pallas_call(
        body, grid=(nb, N_GROUPS, nq),
        in_specs=[qspec,
                  pl.BlockSpec((seq, GROUP_W), lambda b, g, i: (b, N_GROUPS + g)),
                  pl.BlockSpec((seq, GROUP_W), lambda b, g, i: (b, 2 * N_GROUPS + g)),
                  qspec],
        out_specs=[qspec, kvout, kvout],
        out_shape=[jax.ShapeDtypeStruct((t, D_MODEL), BF16)] * 3,
        scratch_shapes=[pltpu.VMEM((HEAD_GROUP, n_kb, Q_TILE, Q_BLOCK), F32),
                        pltpu.VMEM((HEAD_GROUP, n_kb, Q_TILE, Q_BLOCK), BF16),
                        pltpu.VMEM((Q_TILE, GROUP_W), F32),
                        pltpu.VMEM((seq, GROUP_W), F32), pltpu.VMEM((seq, GROUP_W), F32)],
        compiler_params=_params("parallel", "parallel", "arbitrary"), name="attn_bwd")(qkv, qkv, qkv, do)


def _conv_block_rows(seq):
    return min(256, seq)


SUBLANES = 8
CONV_SUB = 32
CONV_GROUPS = CONV_SUB // SUBLANES


def _fill_shifts(sh_ref, n):
    for r in range(1, SUBLANES):
        sh_ref[r, 0:n, :] = sh_ref[0, pl.ds(r, n), :]


def _shifted(sh_ref, start, cs):
    r = start % SUBLANES
    return sh_ref[r, start - r:start - r + CONV_SUB, cs].reshape(CONV_GROUPS, SUBLANES, -1)


def _tap(w_ref, k, cs):
    row = w_ref[k:k + 1, cs]
    return jnp.broadcast_to(row, (SUBLANES, row.shape[1]))[None]


def _tree_sum(x):
    parts = [x[i] for i in range(x.shape[0])]
    while len(parts) > 1:
        parts = [parts[i] + parts[i + 1] for i in range(0, len(parts), 2)]
    return parts[0]


def _conv_fwd(hrest, conv_w, conv_b, ln_g, ln_b, nb, seq):
    ts = _conv_block_rows(seq)
    nblk = seq // ts
    lanes = 128

    def body(cv_ref, cg_ref, cvh_ref, cgh_ref, z_ref, w_ref, cb_ref, g_ref, b_ref, u1_ref, cvin_ref, ush):
        i = pl.program_id(1)
        halo = cvh_ref[...] * _sigmoid(cgh_ref[...])
        ush[0, 0:HALO, :] = jnp.where(i > 0, halo, 0.0)
        ush[0, HALO:HALO + ts, :] = cv_ref[...] * _sigmoid(cg_ref[...])
        _fill_shifts(ush, HALO + ts - SUBLANES)
        for cc in range(D_MODEL // lanes):
            cs = slice(cc * lanes, (cc + 1) * lanes)
            for r0 in range(0, ts, CONV_SUB):
                acc = jnp.zeros((CONV_GROUPS, SUBLANES, lanes), F32) + _tap(cb_ref, 0, cs)
                for k in range(CONV_K):
                    acc = acc + _tap(w_ref, k, cs) * _shifted(ush, HALO - CONV_K + 1 + k + r0, cs)
                u1_ref[r0:r0 + CONV_SUB, cs] = acc.reshape(CONV_SUB, lanes)
        xhat, _ = _ln_stats(u1_ref[...])
        u2 = xhat * g_ref[...] + b_ref[...]
        z = z_ref[...]
        cvin_ref[...] = (u2 * _sigmoid(u2) * z * _sigmoid(z)).astype(BF16)

    def main(colblk):
        return pl.BlockSpec((ts, D_MODEL), lambda b, i: (b * nblk + i, colblk))

    def halo(colblk):
        return pl.BlockSpec((HALO, D_MODEL),
                            lambda b, i: (jnp.maximum((b * seq + i * ts) // HALO - 1, 0), colblk))

    vec = pl.BlockSpec((1, D_MODEL), lambda b, i: (0, 0))
    t = nb * seq
    return pl.pallas_call(
        body, grid=(nb, nblk),
        in_specs=[main(1), main(2), halo(1), halo(2), main(3),
                  pl.BlockSpec((CONV_ROWS, D_MODEL), lambda b, i: (0, 0)), vec, vec, vec],
        out_specs=[main(0), main(0)],
        out_shape=[jax.ShapeDtypeStruct((t, D_MODEL), F32), jax.ShapeDtypeStruct((t, D_MODEL), BF16)],
        scratch_shapes=[pltpu.VMEM((SUBLANES, HALO + ts, D_MODEL), F32)],
        compiler_params=_params("parallel", "parallel"), name="conv_fwd")(
            hrest, hrest, hrest, hrest, hrest, conv_w, conv_b, ln_g, ln_b)


def _tail(o, hrest, cvin, h0, tgt, w_sb, w_cv, w_out, ln_g, ln_b):
    t = o.shape[0]
    tm = min(256, t)

    def body(o_ref, z_ref, gs_ref, gc_ref, cvin_ref, h0_ref, tgt_ref, wsb_ref, wcv_ref, wout_ref, g_ref, b_ref,
             dr_ref, drb_ref, a_ref, mg_ref, dysb_ref, dycv_ref, do_ref, dz_ref, dgs_ref, dgc_ref, dcvin_ref,
             st_ref):
        @pl.when(pl.program_id(0) == 0)
        def _():
            st_ref[...] = jnp.zeros_like(st_ref)

        o = o_ref[...]
        z = z_ref[...]
        sz = _sigmoid(z)
        a = (o * z * sz).astype(BF16)
        a_ref[...] = a
        y_sb = _dot(a, wsb_ref[...])
        y_cv = _dot(cvin_ref[...], wcv_ref[...])
        s_sb = _sigmoid(gs_ref[...])
        s_cv = _sigmoid(gc_ref[...])
        merged = (s_sb * y_sb + s_cv * y_cv).astype(BF16)
        mg_ref[...] = merged
        r = ALPHA * h0_ref[...] + _dot(merged, wout_ref[...])
        xhat, rstd = _ln_stats(r)
        g = g_ref[...]
        err = xhat * g + b_ref[...] - tgt_ref[...]
        dy = err * (1.0 / D_MODEL)
        st_ref[0:1, :] += jnp.sum(dy * xhat, axis=0, keepdims=True)
        st_ref[1:2, :] += jnp.sum(dy, axis=0, keepdims=True)
        st_ref[2:3, :] += (0.5 / D_MODEL) * jnp.sum(err * err, axis=0, keepdims=True)
        dr = _ln_bwd(dy, xhat, rstd, g)
        dr_ref[...] = dr
        drb = dr.astype(BF16)
        drb_ref[...] = drb
        dm = _dot_nt(drb, wout_ref[...])
        dy_sb = (dm * s_sb).astype(BF16)
        dy_cv = (dm * s_cv).astype(BF16)
        dysb_ref[...] = dy_sb
        dycv_ref[...] = dy_cv
        dgs_ref[...] = (dm * y_sb * s_sb * (1.0 - s_sb)).astype(BF16)
        dgc_ref[...] = (dm * y_cv * s_cv * (1.0 - s_cv)).astype(BF16)
        da = _dot_nt(dy_sb, wsb_ref[...])
        dcvin_ref[...] = _dot_nt(dy_cv, wcv_ref[...])
        do_ref[...] = (da * z * sz).astype(BF16)
        dz_ref[...] = (da * o * sz * (1.0 + z * (1.0 - sz))).astype(BF16)

    def tok(colblk=0):
        return pl.BlockSpec((tm, D_MODEL), lambda i: (i, colblk))

    wspec = pl.BlockSpec((D_MODEL, D_MODEL), lambda i: (0, 0), pipeline_mode=pl.Buffered(1))
    vec = pl.BlockSpec((1, D_MODEL), lambda i: (0, 0))
    bf = jax.ShapeDtypeStruct((t, D_MODEL), BF16)
    f32 = jax.ShapeDtypeStruct((t, D_MODEL), F32)
    return pl.pallas_call(
        body, grid=(t // tm,),
        in_specs=[tok(), tok(0), tok(4), tok(5), tok(), tok(), tok(), wspec, wspec, wspec, vec, vec],
        out_specs=[tok()] * 11 + [pl.BlockSpec((8, D_MODEL), lambda i: (0, 0))],
        out_shape=[f32] + [bf] * 9 + [f32, jax.ShapeDtypeStruct((8, D_MODEL), F32)],
        compiler_params=_params("arbitrary"), name="tail")(
            o, hrest, hrest, hrest, cvin, h0, tgt, w_sb, w_cv, w_out, ln_g, ln_b)


def _conv_branch_bwd(dcvin, u1, hrest, ln_g, ln_b):
    t = u1.shape[0]
    tm = min(512, t)

    def body(dc_ref, u1_ref, z_ref, g_ref, b_ref, du1_ref, dz_ref, st_ref):
        @pl.when(pl.program_id(0) == 0)
        def _():
            st_ref[...] = jnp.zeros_like(st_ref)

        xhat, rstd = _ln_stats(u1_ref[...])
        g = g_ref[...]
        u2 = xhat * g + b_ref[...]
        s2 = _sigmoid(u2)
        z = z_ref[...]
        sz = _sigmoid(z)
        dc = dc_ref[...]
        dz_ref[...] = (dc * (u2 * s2) * sz * (1.0 + z * (1.0 - sz))).astype(BF16)
        du2 = dc * (z * sz) * s2 * (1.0 + u2 * (1.0 - s2))
        st_ref[0:1, :] += jnp.sum(du2 * xhat, axis=0, keepdims=True)
        st_ref[1:2, :] += jnp.sum(du2, axis=0, keepdims=True)
        du1 = _ln_bwd(du2, xhat, rstd, g)
        du1_ref[...] = du1
        st_ref[2:3, :] += jnp.sum(du1, axis=0, keepdims=True)

    tok = pl.BlockSpec((tm, D_MODEL), lambda i: (i, 0))
    vec = pl.BlockSpec((1, D_MODEL), lambda i: (0, 0))
    return pl.pallas_call(
        body, grid=(t // tm,),
        in_specs=[tok, tok, pl.BlockSpec((tm, D_MODEL), lambda i: (i, 3)), vec, vec],
        out_specs=[tok, tok, pl.BlockSpec((8, D_MODEL), lambda i: (0, 0))],
        out_shape=[jax.ShapeDtypeStruct((t, D_MODEL), F32), jax.ShapeDtypeStruct((t, D_MODEL), BF16),
                   jax.ShapeDtypeStruct((8, D_MODEL), F32)],
        compiler_params=_params("arbitrary"), name="conv_branch_bwd")(dcvin, u1, hrest, ln_g, ln_b)


def _conv_bwd(du1, hrest, conv_w, nb, seq):
    ts = _conv_block_rows(seq)
    nblk = seq // ts
    lanes = 128
    last_halo = nb * seq // HALO - 1

    def body(du_ref, duh_ref, cv_ref, cg_ref, cvh_ref, cgh_ref, w_ref, dcv_ref, dcg_ref, dw_ref, ush, dsh, dw_part):
        b = pl.program_id(0)
        i = pl.program_id(1)

        @pl.when((b == 0) & (i == 0))
        def _():
            dw_part[...] = jnp.zeros_like(dw_part)

        cv = cv_ref[...]
        sg = _sigmoid(cg_ref[...])
        halo = cvh_ref[...] * _sigmoid(cgh_ref[...])
        ush[0, 0:HALO, :] = jnp.where(i > 0, halo, 0.0)
        ush[0, HALO:HALO + ts, :] = cv * sg
        dsh[0, 0:ts, :] = du_ref[...]
        dsh[0, ts:ts + HALO, :] = jnp.where(i < nblk - 1, duh_ref[...], 0.0)
        _fill_shifts(ush, HALO + ts - SUBLANES)
        _fill_shifts(dsh, HALO + ts - SUBLANES)
        for cc in range(D_MODEL // lanes):
            cs = slice(cc * lanes, (cc + 1) * lanes)
            for r0 in range(0, ts, CONV_SUB):
                rows = slice(r0, r0 + CONV_SUB)
                dmain = _shifted(dsh, r0, cs)
                acc = jnp.zeros((CONV_GROUPS, SUBLANES, lanes), F32)
                for k in range(CONV_K):
                    acc = acc + _tap(w_ref, k, cs) * _shifted(dsh, CONV_K - 1 - k + r0, cs)
                    dw_part[k, :, cs] += _tree_sum(dmain * _shifted(ush, HALO - CONV_K + 1 + k + r0, cs))
                acc = acc.reshape(CONV_SUB, lanes)
                dcv_ref[rows, cs] = (acc * sg[rows, cs]).astype(BF16)
                dcg_ref[rows, cs] = (acc * cv[rows, cs] * sg[rows, cs] * (1.0 - sg[rows, cs])).astype(BF16)

        @pl.when((b == nb - 1) & (i == nblk - 1))
        def _():
            for k in range(CONV_ROWS):
                dw_ref[k:k + 1, :] = jnp.sum(dw_part[k], axis=0, keepdims=True)

    def main(colblk):
        return pl.BlockSpec((ts, D_MODEL), lambda b, i: (b * nblk + i, colblk))

    def halo_before(colblk):
        return pl.BlockSpec((HALO, D_MODEL),
                            lambda b, i: (jnp.maximum((b * seq + i * ts) // HALO - 1, 0), colblk))

    halo_after = pl.BlockSpec((HALO, D_MODEL),
                              lambda b, i: (jnp.minimum((b * seq + (i + 1) * ts) // HALO, last_halo), 0))
    t = nb * seq
    return pl.pallas_call(
        body, grid=(nb, nblk),
        in_specs=[main(0), halo_after, main(1), main(2), halo_before(1), halo_before(2),
                  pl.BlockSpec((CONV_ROWS, D_MODEL), lambda b, i: (0, 0))],
        out_specs=[main(0), main(0), pl.BlockSpec((CONV_ROWS, D_MODEL), lambda b, i: (0, 0))],
        out_shape=[jax.ShapeDtypeStruct((t, D_MODEL), BF16), jax.ShapeDtypeStruct((t, D_MODEL), BF16),
                   jax.ShapeDtypeStruct((CONV_ROWS, D_MODEL), F32)],
        scratch_shapes=[pltpu.VMEM((SUBLANES, HALO + ts, D_MODEL), F32), pltpu.VMEM((SUBLANES, ts + HALO, D_MODEL), F32),
                        pltpu.VMEM((CONV_ROWS, SUBLANES, D_MODEL), F32)],
        compiler_params=_params("arbitrary", "arbitrary"), name="conv_bwd")(
            du1, du1, hrest, hrest, hrest, hrest, conv_w)


def _weight_grad(x, dys, name):
    n = len(dys)
    t = x.shape[0]
    tm = min(1024, t)

    def body(x_ref, *refs):
        o_ref = refs[n]
        j = pl.program_id(0)

        @pl.when(pl.program_id(1) == 0)
        def _():
            o_ref[...] = jnp.zeros_like(o_ref)

        for p in range(n):
            @pl.when(j == p)
            def _(p=p):
                o_ref[...] += _dot_tn(x_ref[...], refs[p][...])

    def dy_spec(p):
        return pl.BlockSpec((tm, D_MODEL), lambda j, i: (jnp.where(j == p, i, 0), 0))

    return pl.pallas_call(
        body, grid=(n, t // tm),
        in_specs=[pl.BlockSpec((tm, D_MODEL), lambda j, i: (i, 0))] + [dy_spec(p) for p in range(n)],
        out_specs=pl.BlockSpec((D_MODEL, D_MODEL), lambda j, i: (0, j)),
        out_shape=jax.ShapeDtypeStruct((D_MODEL, n * D_MODEL), F32),
        compiler_params=_params("parallel", "arbitrary"), name=name)(x, *dys)


def _in_proj_bwd(dhs, w_in, dr, x, ln_g, pair):
    n = len(dhs)
    t = x.shape[0]
    tm = min(512, t)
    n_i = t // tm

    def body(*refs):
        dh_refs = refs[:n]
        w_ref, dr_ref, x_ref, g_ref = refs[n:n + 4]
        pair_refs = refs[n + 4:n + 8]
        gx_ref, st_ref = refs[n + 8:n + 10]
        got_refs = refs[n + 10:n + 14]
        acc, send_sems, recv_sems = refs[n + 14:]
        p = pl.program_id(0)
        i = pl.program_id(1)
        pieces, wholes = _chip_exchange_copies(pair_refs, got_refs, send_sems, recv_sems)

        @pl.when((p == 0) & (i == 0))
        def _():
            st_ref[...] = jnp.zeros_like(st_ref)
            for cp in pieces:
                cp.start()

        @pl.when((p == n - 1) & (i == n_i - 1))
        def _():
            for cp in wholes:
                cp.wait_recv()
            for cp in wholes:
                cp.wait_send()

        @pl.when(p == 0)
        def _():
            acc[i] = ALPHA * dr_ref[...]

        for q in range(n):
            @pl.when(p == q)
            def _(q=q):
                acc[i] += _dot_nt(dh_refs[q][...], w_ref[...])

        @pl.when(p == n - 1)
        def _():
            xhat, rstd = _ln_stats(x_ref[...])
            dh0 = acc[i]
            st_ref[0:1, :] += jnp.sum(dh0 * xhat, axis=0, keepdims=True)
            st_ref[1:2, :] += jnp.sum(dh0, axis=0, keepdims=True)
            gx_ref[...] = _ln_bwd(dh0, xhat, rstd, g_ref[...])

    def tok(piece):
        return pl.BlockSpec((tm, D_MODEL), lambda p, i: (jnp.where(p == piece, i, 0), 0))

    got = [jax.ShapeDtypeStruct((3, HALF, SHARD_IN), BF16)] + [jax.ShapeDtypeStruct((3, SHARD_SQ, HALF), BF16)] * 3
    res = pl.pallas_call(
        body, grid=(n, n_i),
        in_specs=[tok(q) for q in range(n)] + [pl.BlockSpec((D_MODEL, D_MODEL), lambda p, i: (0, p)), tok(0), tok(n - 1),
                                               pl.BlockSpec((1, D_MODEL), lambda p, i: (0, 0))] + [ANY] * 4,
        out_specs=[tok(n - 1), pl.BlockSpec((8, D_MODEL), lambda p, i: (0, 0))] + [ANY] * 4,
        out_shape=[jax.ShapeDtypeStruct((t, D_MODEL), F32), jax.ShapeDtypeStruct((8, D_MODEL), F32)] + got,
        scratch_shapes=[pltpu.VMEM((n_i, tm, D_MODEL), F32), pltpu.SemaphoreType.DMA((12,)),
                        pltpu.SemaphoreType.DMA((12,))],
        compiler_params=_params("arbitrary", "arbitrary"), name="in_proj_bwd")(*dhs, w_in, dr, x, ln_g, *pair)
    return res[0], res[1], res[2:]


def _forward_backward(x, tgt, nb, seq, shards, vecs):
    w_in_early = _gather_first(shards[0])
    h0, h0b = _ln_in(x, vecs["ln_in_g"], vecs["ln_in_b"])
    qkv = _in_proj(h0b, w_in_early, 0, 3, BF16, True, "in_proj_qkv")
    o, (w_in, w_sb, w_cv, w_out, conv_w) = _attn_fwd(qkv, nb, seq, w_in_early, shards)
    hrest = _in_proj(h0b, w_in, 3, 6, F32, False, "in_proj_rest")
    u1, cvin = _conv_fwd(hrest, conv_w, vecs["conv_b"], vecs["conv_ln_g"], vecs["conv_ln_b"], nb, seq)
    (dr, drb, a, merged, dy_sb, dy_cv, do, dz_sb, dg_sb, dg_cv, dcvin, st_tail) = _tail(
        o, hrest, cvin, h0, tgt, w_sb, w_cv, w_out, vecs["ln_post_g"], vecs["ln_post_b"])
    d_w_sb = _weight_grad(a, [dy_sb], "grad_w_sb")
    d_w_cv = _weight_grad(cvin, [dy_cv], "grad_w_cv")
    d_w_out = _weight_grad(merged, [drb], "grad_w_out")
    du1, dz_cv, st_conv = _conv_branch_bwd(dcvin, u1, hrest, vecs["conv_ln_g"], vecs["conv_ln_b"])
    dc_val, dc_gate, d_conv_w = _conv_bwd(du1, hrest, conv_w, nb, seq)
    dq, dk, dv = _attn_bwd(qkv, do, nb, seq)
    dhs = [dq, dk, dv, dz_sb, dc_val, dc_gate, dz_cv, dg_sb, dg_cv]
    d_w_in = _weight_grad(h0b, dhs, "grad_w_in")
    small = jnp.concatenate([st_conv[2:3], st_conv[0:2], st_tail[0:3], d_conv_w], axis=0)
    return w_in, dhs, dr, (d_w_in, d_w_sb, d_w_cv, d_w_out), small


def _place():
    x, y, c = lax.axis_index("x"), lax.axis_index("y"), lax.axis_index("c")
    chips = [(1 - x, y), (x, 1 - y), (1 - x, 1 - y)]
    return x, y, c, chips


EARLY_COLS = tuple(min(max(3 * D_MODEL - chip * SHARD_IN, 0), SHARD_IN) for chip in range(N_CHIPS))
EARLY = tuple((chip, cols) for chip, cols in enumerate(EARLY_COLS) if cols)
LATE = tuple((chip, cols) for chip, cols in enumerate(EARLY_COLS) if cols < SHARD_IN)
GATHER_PARTS = 4


def _w_in_slot(full, chip):
    return full.at[:, pl.ds(pl.multiple_of(chip * SHARD_IN, 128), SHARD_IN)]


def _gather_first(w_in):
    part_rows = HALF // GATHER_PARTS
    n_sem = 3 * GATHER_PARTS

    def body(src, dst, ici_send, ici_recv, fwd_send, fwd_recv, local_sem):
        x, y, c, chips = _place()
        me = 2 * x + y
        sibling = (x, y, 1 - c)
        owner = [2 * px + py for px, py in chips]

        def part(ref, k, half, cols):
            return ref.at[pl.ds(pl.multiple_of(half * HALF + k * part_rows, 16), part_rows), pl.ds(0, cols)]

        def held(j, k, half, cols):
            return part(_w_in_slot(dst, owner[j]), k, half, cols)

        def sent(j, k, cols):
            px, py = chips[j]
            return _remote(part(src, k, c, cols), part(_w_in_slot(dst, me), k, c, cols),
                           ici_send.at[j * GATHER_PARTS + k], ici_recv.at[j * GATHER_PARTS + k], (px, py, c))

        def landed(j, k, cols):
            px, py = chips[j]
            return _remote(held(j, k, c, cols), held(j, k, c, cols), ici_send.at[j * GATHER_PARTS + k],
                           ici_recv.at[j * GATHER_PARTS + k], (px, py, c))

        def passed(j, k, half, cols):
            return _remote(held(j, k, half, cols), held(j, k, half, cols), fwd_send.at[j * GATHER_PARTS + k],
                           fwd_recv.at[j * GATHER_PARTS + k], sibling)

        own = _w_in_slot(dst, me)
        rows = D_MODEL // N_CHUNKS
        for k in range(N_CHUNKS):
            pltpu.make_async_copy(src.at[pl.ds(k * rows, rows)], own.at[pl.ds(k * rows, rows)], local_sem).start()
        local = pltpu.make_async_copy(src, own, local_sem)

        for chip, cols in EARLY:
            @pl.when(me == chip)
            def _(cols=cols):
                for j in range(3):
                    for k in range(GATHER_PARTS):
                        sent(j, k, cols).start()

        for j in range(3):
            for chip, cols in EARLY:
                @pl.when(owner[j] == chip)
                def _(j=j, cols=cols):
                    for k in range(GATHER_PARTS):
                        landed(j, k, cols).wait_recv()
                        passed(j, k, c, cols).start()

        for j in range(3):
            for chip, cols in EARLY:
                @pl.when(owner[j] == chip)
                def _(j=j, cols=cols):
                    for k in range(GATHER_PARTS):
                        passed(j, k, 1 - c, cols).wait_recv()
                    for k in range(GATHER_PARTS):
                        passed(j, k, c, cols).wait_send()

        for chip, cols in EARLY:
            @pl.when(me == chip)
            def _(cols=cols):
                for j in range(3):
                    for k in range(GATHER_PARTS):
                        sent(j, k, cols).wait_send()

        local.wait()

    return pl.pallas_call(
        body, in_specs=[ANY], out_specs=ANY, out_shape=jax.ShapeDtypeStruct((D_MODEL, IN_WIDTH), BF16),
        scratch_shapes=[pltpu.SemaphoreType.DMA((n_sem,)), pltpu.SemaphoreType.DMA((n_sem,)),
                        pltpu.SemaphoreType.DMA((n_sem,)), pltpu.SemaphoreType.DMA((n_sem,)),
                        pltpu.SemaphoreType.DMA],
        name="gather_first")(w_in)


def _late_gather(srcs, dsts, send_sems, recv_sems, local_sems):
    x, y, c, chips = _place()
    me = 2 * x + y
    small = range(1, 5)

    def slot(a, chip):
        if a == 0:
            return _w_in_slot(dsts[0], chip)
        if a == 4:
            return dsts[a].at[:, pl.ds(pl.multiple_of(chip * SHARD_SQ, 128), SHARD_SQ)]
        return dsts[a].at[pl.ds(pl.multiple_of(chip * SHARD_SQ, 16), SHARD_SQ), :]

    def sems(a, j):
        return send_sems.at[3 * a + j], recv_sems.at[3 * a + j]

    def whole(a, j):
        px, py = chips[j]
        return _remote(srcs[a], slot(a, me), *sems(a, j), (px, py, c))

    def arrival(a, j):
        px, py = chips[j]
        return _arrival(slot(a, 2 * px + py), *sems(a, j), (px, py, c))

    def local(a):
        return pltpu.make_async_copy(srcs[a], slot(a, me), local_sems.at[a - 1])

    def late_src(first):
        return srcs[0].at[:, pl.ds(first, SHARD_IN - first)]

    def late_dst(chip, first):
        return _w_in_slot(dsts[0], chip).at[:, pl.ds(first, SHARD_IN - first)]

    def start():
        for a in small:
            local(a).start()
            for j in range(3):
                whole(a, j).start()

        for chip, first in LATE:
            @pl.when(me == chip)
            def _(chip=chip, first=first):
                for j, (px, py) in enumerate(chips):
                    for cp in _chunk_copies(late_src(first), late_dst(chip, first), GATHER_PARTS, *sems(0, j), (px, py, c)):
                        cp.start()

    def finish():
        for j, (px, py) in enumerate(chips):
            for a in small:
                arrival(a, j).wait_recv()

            for chip, first in LATE:
                @pl.when(2 * px + py == chip)
                def _(j=j, px=px, py=py, chip=chip, first=first):
                    _arrival(late_dst(chip, first), *sems(0, j), (px, py, c)).wait_recv()

        for a in small:
            for j in range(3):
                whole(a, j).wait_send()
            local(a).wait()

        for chip, first in LATE:
            @pl.when(me == chip)
            def _(chip=chip, first=first):
                for j, (px, py) in enumerate(chips):
                    _remote(late_src(first), late_dst(chip, first), *sems(0, j), (px, py, c)).wait_send()

    return start, finish


def _remote(src, dst, send_sem, recv_sem, device):
    return pltpu.make_async_remote_copy(src_ref=src, dst_ref=dst, send_sem=send_sem, recv_sem=recv_sem,
                                        device_id=device, device_id_type=MESH)


def _chunk_copies(src, dst, n, send_sem, recv_sem, device):
    rows = src.shape[0] // n
    return [_remote(src.at[pl.ds(k * rows, rows)], dst.at[pl.ds(k * rows, rows)], send_sem, recv_sem, device)
            for k in range(n)]


def _row_chunks(src, dst, n, send_sem, recv_sem, device):
    return _chunk_copies(src, dst, n, send_sem, recv_sem, device), _remote(src, dst, send_sem, recv_sem, device)


def _arrival(dst, send_sem, recv_sem, device):
    return _remote(dst, dst, send_sem, recv_sem, device)


def _pair_exchange(d_in, d_sb, d_cv, d_out):
    def body(i0, i1, i2, i3, r0, r1, r2, r3, send_sems, recv_sems):
        x, y, c, _ = _place()
        srcs = [i0, i1, i2, i3]
        theirs = [r0, r1, r2, r3]
        sibling = (x, y, 1 - c)
        start = pl.multiple_of((1 - c) * HALF, 128)
        views = [srcs[0].at[pl.ds(start, HALF), :]] + [srcs[a].at[:, pl.ds(start, HALF)] for a in range(1, 4)]
        pieces, wholes = [], []
        for a in range(4):
            cps, whole = _row_chunks(views[a], theirs[a], N_CHUNKS, send_sems.at[a], recv_sems.at[a], sibling)
            pieces += cps
            wholes.append(whole)
        for cp in pieces:
            cp.start()
        for cp in wholes:
            cp.wait_recv()
        for cp in wholes:
            cp.wait_send()

    shapes = [jax.ShapeDtypeStruct((HALF, IN_WIDTH), F32)] + [jax.ShapeDtypeStruct((D_MODEL, HALF), F32)] * 3
    return pl.pallas_call(
        body, in_specs=[ANY] * 4, out_specs=[ANY] * 4, out_shape=shapes,
        scratch_shapes=[pltpu.SemaphoreType.DMA((4,)), pltpu.SemaphoreType.DMA((4,))],
        name="pair_exchange")(d_in, d_sb, d_cv, d_out)


def _chip_exchange_copies(srcs, theirs, send_sems, recv_sems):
    _, _, c, chips = _place()

    def shard(a, px, py):
        chip = 2 * px + py
        if a == 0:
            return srcs[a].at[:, pl.ds(pl.multiple_of(chip * SHARD_IN, 128), SHARD_IN)]
        return srcs[a].at[pl.ds(pl.multiple_of(chip * SHARD_SQ, 16), SHARD_SQ), :]

    pieces, wholes = [], []
    for j, (px, py) in enumerate(chips):
        for a in range(4):
            cps, whole = _row_chunks(shard(a, px, py), theirs[a].at[j], 4, send_sems.at[3 * a + j],
                                     recv_sems.at[3 * a + j], (px, py, c))
            pieces += cps
            wholes.append(whole)
    return pieces, wholes


def _pair_share(f_in, f_sb, f_cv, f_out):
    def body(i0, i1, i2, i3, o0, o1, o2, o3, send_sems, recv_sems):
        del i0, i1, i2, i3
        x, y, c, _ = _place()
        dsts = [o0, o1, o2, o3]
        sibling = (x, y, 1 - c)

        def half(a, which):
            start = pl.multiple_of(which * HALF, 128)
            if a == 0:
                return dsts[a].at[pl.ds(start, HALF), :]
            return dsts[a].at[:, pl.ds(start, HALF)]

        pieces, sent, arrived = [], [], []
        for a in range(4):
            cps, whole = _row_chunks(half(a, c), half(a, c), N_CHUNKS, send_sems.at[a], recv_sems.at[a], sibling)
            pieces += cps
            sent.append(whole)
            arrived.append(_arrival(half(a, 1 - c), send_sems.at[a], recv_sems.at[a], sibling))
        for cp in pieces:
            cp.start()
        for cp in arrived:
            cp.wait_recv()
        for cp in sent:
            cp.wait_send()

    return pl.pallas_call(
        body, in_specs=[ANY] * 4, out_specs=[ANY] * 4,
        out_shape=[jax.ShapeDtypeStruct((D_MODEL, SHARD_IN), F32)] + [jax.ShapeDtypeStruct((SHARD_SQ, D_MODEL), F32)] * 3,
        input_output_aliases={0: 0, 1: 1, 2: 2, 3: 3},
        scratch_shapes=[pltpu.SemaphoreType.DMA((4,)), pltpu.SemaphoreType.DMA((4,))],
        name="pair_share")(f_in, f_sb, f_cv, f_out)


def _small_allreduce(small):
    def body(s_ref, o_ref, slots, send_sems, recv_sems):
        x, y, c, _ = _place()
        me = 4 * x + 2 * y + c
        slots[me] = s_ref[...]
        sends = []
        for k in range(1, N_DEV):
            px, py, pc = (x + (k >> 2)) % 2, (y + ((k >> 1) & 1)) % 2, (c + (k & 1)) % 2
            sends.append(pltpu.make_async_remote_copy(
                src_ref=s_ref, dst_ref=slots.at[me], send_sem=send_sems.at[k - 1], recv_sem=recv_sems.at[k - 1],
                device_id=(px, py, pc), device_id_type=MESH))
        for cp in sends:
            cp.start()
        for k in range(1, N_DEV):
            px, py, pc = (x + (k >> 2)) % 2, (y + ((k >> 1) & 1)) % 2, (c + (k & 1)) % 2
            pltpu.make_async_remote_copy(
                src_ref=s_ref, dst_ref=slots.at[4 * px + 2 * py + pc], send_sem=send_sems.at[k - 1],
                recv_sem=recv_sems.at[k - 1], device_id=(px, py, pc), device_id_type=MESH).wait_recv()
        for cp in sends:
            cp.wait_send()
        total = slots[0]
        for d in range(1, N_DEV):
            total = total + slots[d]
        o_ref[...] = total
        o_ref[7:8, :] = jnp.zeros((1, D_MODEL), F32) + jnp.sum(total[7:8, :], axis=1, keepdims=True)

    return pl.pallas_call(
        body, in_specs=[pl.BlockSpec(memory_space=pltpu.VMEM)], out_specs=pl.BlockSpec(memory_space=pltpu.VMEM),
        out_shape=jax.ShapeDtypeStruct((SMALL_ROWS, D_MODEL), F32),
        scratch_shapes=[pltpu.VMEM((N_DEV, SMALL_ROWS, D_MODEL), F32), pltpu.SemaphoreType.DMA((N_DEV - 1,)),
                        pltpu.SemaphoreType.DMA((N_DEV - 1,))],
        name="small_allreduce")(small)


SUM_BLOCKS = 8


def _pair_sum(place, full, theirs, by_rows, name):
    rows, cols = theirs.shape
    tr = rows // SUM_BLOCKS

    def body(p_ref, a_ref, b_ref, o_ref):
        del p_ref
        o_ref[...] = (a_ref[...] + b_ref[...]).astype(BF16)

    mine = (lambda i, p: (p[1] * SUM_BLOCKS + i, 0)) if by_rows else (lambda i, p: (i, p[1]))
    spec = pl.BlockSpec((tr, cols), lambda i, p: (i, 0))
    return pl.pallas_call(
        body, out_shape=jax.ShapeDtypeStruct(theirs.shape, BF16),
        grid_spec=pltpu.PrefetchScalarGridSpec(num_scalar_prefetch=1, grid=(SUM_BLOCKS,),
                                               in_specs=[pl.BlockSpec((tr, cols), mine), spec], out_specs=spec),
        compiler_params=_params("parallel"), name=name)(place, full, theirs)


def _chip_sum(place, pair, got, by_rows, name):
    _, rows, cols = got.shape
    tr = rows // SUM_BLOCKS

    def body(p_ref, a_ref, g_ref, o_ref):
        del p_ref
        o_ref[...] = ((a_ref[...].astype(F32) + g_ref[0].astype(F32)) + g_ref[1].astype(F32)) + g_ref[2].astype(F32)

    if by_rows:
        own = lambda i, p: (i, p[0])
        out = lambda i, p: (p[1] * SUM_BLOCKS + i, 0)
        full = (2 * rows, cols)
    else:
        own = lambda i, p: (p[0] * SUM_BLOCKS + i, 0)
        out = lambda i, p: (i, p[1])
        full = (rows, 2 * cols)
    return pl.pallas_call(
        body, out_shape=jax.ShapeDtypeStruct(full, F32),
        grid_spec=pltpu.PrefetchScalarGridSpec(
            num_scalar_prefetch=1, grid=(SUM_BLOCKS,),
            in_specs=[pl.BlockSpec((tr, cols), own), pl.BlockSpec((3, tr, cols), lambda i, p: (0, i, 0))],
            out_specs=pl.BlockSpec((tr, cols), out)),
        compiler_params=_params("parallel"), name=name)(place, pair, got)


def _adamw(w, g, m, v, name):
    rows, cols = w.shape
    tr = rows // 4 if rows % 32 == 0 else rows
    c1 = 1.0 - ADAM_B1 ** ADAM_STEP
    c2 = 1.0 - ADAM_B2 ** ADAM_STEP

    def body(w_ref, g_ref, m_ref, v_ref, d_ref, mo_ref, vo_ref):
        g = g_ref[...]
        mn = ADAM_B1 * m_ref[...] + (1.0 - ADAM_B1) * g
        vn = ADAM_B2 * v_ref[...] + (1.0 - ADAM_B2) * (g * g)
        mo_ref[...] = mn
        vo_ref[...] = vn
        d_ref[...] = -ADAM_LR * ((mn / c1) / (jnp.sqrt(vn / c2) + ADAM_EPS) + ADAM_WD * w_ref[...])

    spec = pl.BlockSpec((tr, cols), lambda i: (i, 0))
    shape = jax.ShapeDtypeStruct(w.shape, F32)
    return pl.pallas_call(body, grid=(rows // tr,), in_specs=[spec] * 4, out_specs=[spec] * 3,
                          out_shape=[shape] * 3, compiler_params=_params("parallel"), name=name)(w, g, m, v)


GRAD_NAMES = ("in", "sb", "cv", "out")


def _place_scalars():
    return jnp.stack([2 * lax.axis_index("x") + lax.axis_index("y"), lax.axis_index("c")]).astype(jnp.int32)


def _pair_sums(grads):
    place = _place_scalars()
    theirs = _pair_exchange(*grads)
    return [_pair_sum(place, grads[a], theirs[a], a == 0, "pair_sum_" + GRAD_NAMES[a]) for a in range(4)]


def _finish_weight_grads(pair, got):
    place = _place_scalars()
    done = [_chip_sum(place, pair[a], got[a], a == 0, "chip_sum_" + GRAD_NAMES[a]) for a in range(4)]
    return _pair_share(*done)


def kernel(x, ln_in_g, ln_in_b, w_in, w_sb_proj, conv_w, conv_b, conv_ln_g, conv_ln_b, w_cv_proj, w_out, ln_post_g, ln_post_b, loss_target, m_ln_in_g, m_ln_in_b, m_w_in, m_w_sb_proj, m_conv_w, m_conv_b, m_conv_ln_g, m_conv_ln_b, m_w_cv_proj, m_w_out, m_ln_post_g, m_ln_post_b, v_ln_in_g, v_ln_in_b, v_w_in, v_w_sb_proj, v_conv_w, v_conv_b, v_conv_ln_g, v_conv_ln_b, v_w_cv_proj, v_w_out, v_ln_post_g, v_ln_post_b):
    nb, seq, _ = x.shape
    t = nb * seq
    vec_names = ("ln_in_g", "ln_in_b", "conv_b", "conv_ln_g", "conv_ln_b", "ln_post_g", "ln_post_b")
    vec_w = dict(zip(vec_names, (ln_in_g, ln_in_b, conv_b, conv_ln_g, conv_ln_b, ln_post_g, ln_post_b)))
    vec_m = dict(zip(vec_names, (m_ln_in_g, m_ln_in_b, m_conv_b, m_conv_ln_g, m_conv_ln_b, m_ln_post_g, m_ln_post_b)))
    vec_v = dict(zip(vec_names, (v_ln_in_g, v_ln_in_b, v_conv_b, v_conv_ln_g, v_conv_ln_b, v_ln_post_g, v_ln_post_b)))
    vecs = {k: a.reshape(1, D_MODEL) for k, a in vec_w.items()}

    pad_taps = lambda a: jnp.pad(a.reshape(CONV_K, SHARD_SQ), ((0, CONV_ROWS - CONV_K), (0, 0)))
    shards = (w_in[0].astype(BF16), w_sb_proj[0].astype(BF16), w_cv_proj[0].astype(BF16), w_out[0].astype(BF16),
              pad_taps(conv_w))
    x2 = x.reshape(t, D_MODEL)
    full_in, dhs, dr, big, small = _forward_backward(x2, loss_target.reshape(t, D_MODEL), nb, seq, shards, vecs)
    pair = _pair_sums(big)
    grad_x, st_in, got = _in_proj_bwd(dhs, full_in, dr, x2, vecs["ln_in_g"], pair)
    g_in, g_sb, g_cv, g_out = _finish_weight_grads(pair, got)
    small = _small_allreduce(jnp.concatenate([st_in[0:2], small], axis=0))

    chip = 2 * lax.axis_index("x") + lax.axis_index("y")
    g_conv = lax.dynamic_slice(small, (8, chip * SHARD_SQ), (CONV_ROWS, SHARD_SQ))
    loss = small[7, 0]

    grads, deltas, new_m, new_v = {}, {}, {}, {}
    stack = lambda d: jnp.concatenate([d[k].reshape(1, D_MODEL) for k in vec_names] + [jnp.zeros((1, D_MODEL), F32)])
    vd, vm, vv = _adamw(stack(vec_w), small[0:8], stack(vec_m), stack(vec_v), "adamw_vectors")
    for r, k in enumerate(vec_names):
        shape = vec_w[k].shape
        grads[k] = small[r].reshape(shape)
        deltas[k], new_m[k], new_v[k] = vd[r].reshape(shape), vm[r].reshape(shape), vv[r].reshape(shape)

    big_w = {"w_in": (w_in, g_in, m_w_in, v_w_in), "w_sb_proj": (w_sb_proj, g_sb, m_w_sb_proj, v_w_sb_proj),
             "w_cv_proj": (w_cv_proj, g_cv, m_w_cv_proj, v_w_cv_proj), "w_out": (w_out, g_out, m_w_out, v_w_out)}
    for k, (w, g, m, v) in big_w.items():
        d, mn, vn = _adamw(w[0], g, m[0], v[0], "adamw_" + k)
        grads[k], deltas[k], new_m[k], new_v[k] = g[None], d[None], mn[None], vn[None]
    d, mn, vn = _adamw(pad_taps(conv_w), g_conv, pad_taps(m_conv_w), pad_taps(v_conv_w), "adamw_conv_w")
    grads["conv_w"] = g_conv[None, :CONV_K]
    deltas["conv_w"], new_m["conv_w"], new_v["conv_w"] = d[None, :CONV_K], mn[None, :CONV_K], vn[None, :CONV_K]

    order = ("ln_in_g", "ln_in_b", "w_in", "w_sb_proj", "conv_w", "conv_b", "conv_ln_g", "conv_ln_b",
             "w_cv_proj", "w_out", "ln_post_g", "ln_post_b")
    return (loss, grad_x.reshape(x.shape), *[grads[k] for k in order], *[deltas[k] for k in order],
            *[new_m[k] for k in order], *[new_v[k] for k in order])
```

```python
import functools

import jax
import jax.numpy as jnp
from jax import lax
from jax.experimental import pallas as pl
from jax.experimental.pallas import tpu as pltpu

F32 = jnp.float32
BF16 = jnp.bfloat16

D_MODEL = 1024
N_HEADS = 16
HEAD_DIM = 64
HEAD_GROUP = 4
GROUP_W = HEAD_GROUP * HEAD_DIM
N_GROUPS = N_HEADS // HEAD_GROUP
N_PIECES = 9
IN_WIDTH = N_PIECES * D_MODEL
Q_BLOCK = 256
Q_TILE = 2 * Q_BLOCK
CONV_K = 31
CONV_ROWS = 32
HALO = 32
LN_EPS = 1e-5
ALPHA = 2.0 ** 0.25
Q_SCALE = 0.125
N_CHIPS = 4
N_DEV = 8
SHARD_IN = IN_WIDTH // N_CHIPS
SHARD_SQ = D_MODEL // N_CHIPS
HALF = D_MODEL // 2
SMALL_ROWS = 40
N_CHUNKS = 16

ADAM_LR = 0.001
ADAM_B1 = 0.9
ADAM_B2 = 0.999
ADAM_EPS = 1e-08
ADAM_WD = 0.01
ADAM_STEP = 10

MESH = pl.DeviceIdType.MESH
ANY = pl.BlockSpec(memory_space=pl.ANY)
VMEM_LIMIT = 60 * 1024 * 1024

NT = (((1,), (1,)), ((), ()))
TN = (((0,), (0,)), ((), ()))


def _sigmoid(x):
    return 1.0 / (1.0 + jnp.exp(-x))


def _dot(a, b):
    return jnp.dot(a, b, preferred_element_type=F32)


def _sum_along_keys(x, tri_twice):
    return _dot(jnp.concatenate(_split_bf16(x), axis=1), tri_twice)


def _dot_nt(a, b):
    return lax.dot_general(a, b, NT, preferred_element_type=F32)


def _dot_tn(a, b):
    return lax.dot_general(a, b, TN, preferred_element_type=F32)


def _split_bf16(x):
    hi = x.astype(BF16)
    lo = (x - hi.astype(F32)).astype(BF16)
    return hi, lo


def _ln_stats(x):
    mu = jnp.mean(x, axis=-1, keepdims=True)
    xc = x - mu
    var = jnp.mean(xc * xc, axis=-1, keepdims=True)
    rstd = lax.rsqrt(var + LN_EPS)
    return xc * rstd, rstd


def _ln_bwd(dy, xhat, rstd, g):
    dxh = dy * g
    m1 = jnp.mean(dxh, axis=-1, keepdims=True)
    m2 = jnp.mean(dxh * xhat, axis=-1, keepdims=True)
    return rstd * (dxh - m1 - xhat * m2)


def _params(*sem):
    return pltpu.CompilerParams(dimension_semantics=sem, vmem_limit_bytes=VMEM_LIMIT)


def _ln_in(x, g, b):
    t = x.shape[0]
    tm = min(512, t)

    def body(x_ref, g_ref, b_ref, hf_ref, hb_ref):
        xhat, _ = _ln_stats(x_ref[...])
        y = xhat * g_ref[...] + b_ref[...]
        hf_ref[...] = y
        hb_ref[...] = y.astype(BF16)

    row = pl.BlockSpec((tm, D_MODEL), lambda i: (i, 0))
    vec = pl.BlockSpec((1, D_MODEL), lambda i: (0, 0))
    return pl.pallas_call(
        body, grid=(t // tm,), in_specs=[row, vec, vec], out_specs=[row, row],
        out_shape=[jax.ShapeDtypeStruct((t, D_MODEL), F32), jax.ShapeDtypeStruct((t, D_MODEL), BF16)],
        compiler_params=_params("parallel"), name="ln_in")(x, g, b)


def _in_proj(h, w, col0, ncol, out_dtype, scale_first, name):
    t = h.shape[0]
    tm = min(1024, t)

    def body(h_ref, w_ref, o_ref):
        res = _dot(h_ref[...], w_ref[...])
        if scale_first:
            res = res * jnp.where(pl.program_id(0) == 0, Q_SCALE, 1.0)
        o_ref[...] = res.astype(out_dtype)

    return pl.pallas_call(
        body, grid=(ncol, t // tm),
        in_specs=[pl.BlockSpec((tm, D_MODEL), lambda j, i: (i, 0)),
                  pl.BlockSpec((D_MODEL, D_MODEL), lambda j, i: (0, j + col0))],
        out_specs=pl.BlockSpec((tm, D_MODEL), lambda j, i: (i, j)),
        out_shape=jax.ShapeDtypeStruct((t, ncol * D_MODEL), out_dtype),
        compiler_params=_params("parallel", "parallel"), name=name)(h, w)


def _sb_block(l, diag, tri, carry):
    lnb = jnp.minimum(-l, 0.0) - jnp.log(1.0 + jnp.exp(-jnp.abs(l)))
    if diag is not None:
        lnb = jnp.where(diag, lnb, 0.0)
    suffix = _sum_along_keys(lnb, tri)
    a = jnp.exp(l + suffix + carry)
    if diag is not None:
        a = jnp.where(diag, a, 0.0)
    return a, lnb, suffix[:, 0:1]


def _attn_consts():
    head = lax.broadcasted_iota(jnp.int32, (Q_TILE, GROUP_W), 1) // HEAD_DIM
    row = lax.broadcasted_iota(jnp.int32, (Q_TILE, Q_BLOCK), 0)
    col = lax.broadcasted_iota(jnp.int32, (Q_TILE, Q_BLOCK), 1)
    key = row % Q_BLOCK
    return head, col < row, (key >= col).astype(BF16), (key <= col).astype(BF16)


def _split_heads(t, head):
    zero = jnp.zeros_like(t)
    head = head[:t.shape[0]]
    return [jnp.where(head == h, t, zero) for h in range(HEAD_GROUP)]


def _key_block(ref, kb):
    return ref[pl.ds(pl.multiple_of(kb * Q_BLOCK, Q_BLOCK), Q_BLOCK), :]


def _attn_fwd(qkv, nb, seq, w_in_early, shards):
    nq = seq // Q_TILE

    def body(q_ref, k_ref, v_ref, full_in, s0, s1, s2, s3, s4, o_ref, d0, d1, d2, d3, d4, send_sems, recv_sems, local_sems):
        del full_in
        qi = pl.program_id(2)
        first = (pl.program_id(0) == 0) & (pl.program_id(1) == 0) & (qi == 0)
        last = (pl.program_id(0) == nb - 1) & (pl.program_id(1) == N_GROUPS - 1) & (qi == nq - 1)
        start_gather, finish_gather = _late_gather([s0, s1, s2, s3, s4], [d0, d1, d2, d3, d4],
                                                   send_sems, recv_sems, local_sems)
        pl.when(first)(start_gather)
        head, diag, tri, _ = _attn_consts()
        qh = _split_heads(q_ref[...], head)
        o_ref[...] = jnp.zeros_like(o_ref)
        zero_col = jnp.zeros((Q_BLOCK, 1), F32)

        def block(kb, carries, rows, masked):
            kblk = _key_block(k_ref, kb)
            vstack = jnp.concatenate(_split_heads(_key_block(v_ref, kb), head), axis=0)
            mask = diag[:Q_TILE - rows.start] if masked else None
            weights, out = [], []
            for h in range(HEAD_GROUP):
                a, _, blk_sum = _sb_block(_dot_nt(qh[h][rows], kblk), mask, tri, carries[h][rows])
                weights.append(a.astype(BF16))
                out.append(carries[h][rows] + blk_sum)
            o_ref[rows, :] += _dot(jnp.concatenate(weights, axis=1), vstack)
            return out

        second = block(2 * qi + 1, [jnp.zeros((Q_TILE, 1), F32)] * HEAD_GROUP, slice(Q_BLOCK, Q_TILE), True)
        carries = tuple(jnp.concatenate([zero_col, c], axis=0) for c in second)
        everything = slice(0, Q_TILE)
        carries = tuple(block(2 * qi, carries, everything, True))
        lax.fori_loop(1, 2 * qi + 1, lambda jj, cs: tuple(block(2 * qi - jj, cs, everything, False)), carries)
        pl.when(last)(finish_gather)

    n_shards = len(shards)
    full = [jax.ShapeDtypeStruct((D_MODEL, IN_WIDTH), BF16)] + [jax.ShapeDtypeStruct((D_MODEL, D_MODEL), BF16)] * 3 \
        + [jax.ShapeDtypeStruct((CONV_ROWS, D_MODEL), F32)]
    res = pl.pallas_call(
        body, grid=(nb, N_GROUPS, nq),
        in_specs=[pl.BlockSpec((Q_TILE, GROUP_W), lambda b, g, i: (b * nq + i, g)),
                  pl.BlockSpec((seq, GROUP_W), lambda b, g, i: (b, N_GROUPS + g)),
                  pl.BlockSpec((seq, GROUP_W), lambda b, g, i: (b, 2 * N_GROUPS + g))] + [ANY] * (1 + n_shards),
        out_specs=[pl.BlockSpec((Q_TILE, GROUP_W), lambda b, g, i: (b * nq + i, g))] + [ANY] * n_shards,
        out_shape=[jax.ShapeDtypeStruct((nb * seq, D_MODEL), F32)] + full,
        input_output_aliases={3: 1},
        scratch_shapes=[pltpu.SemaphoreType.DMA((3 * n_shards,)), pltpu.SemaphoreType.DMA((3 * n_shards,)),
                        pltpu.SemaphoreType.DMA((n_shards - 1,))],
        compiler_params=_params("arbitrary", "arbitrary", "arbitrary"), name="attn_fwd")(
            qkv, qkv, qkv, w_in_early, *shards)
    return res[0], res[1:]


def _attn_bwd(qkv, do, nb, seq):
    nq = seq // Q_TILE
    n_kb = seq // Q_BLOCK

    def body(q_ref, k_ref, v_ref, do_ref, dq_ref, dk_ref, dv_ref, g_s, beta_s, dq_acc, dk_acc, dv_acc):
        qi = pl.program_id(2)

        @pl.when(qi == 0)
        def _():
            dk_acc[...] = jnp.zeros_like(dk_acc)
            dv_acc[...] = jnp.zeros_like(dv_acc)

        head, diag, tri_suffix, tri_prefix = _attn_consts()
        qh = _split_heads(q_ref[...], head)
        doh = _split_heads(do_ref[...], head)
        dq_acc[...] = jnp.zeros_like(dq_acc)
        zero_col = jnp.zeros((Q_BLOCK, 1), F32)
        everything = slice(0, Q_TILE)
        second = slice(Q_BLOCK, Q_TILE)

        def block_a(kb, carries, rows, masked):
            ks = pl.multiple_of(kb * Q_BLOCK, Q_BLOCK)
            kblk = _key_block(k_ref, kb)
            vblk = _key_block(v_ref, kb)
            mask = diag[:Q_TILE - rows.start] if masked else None
            weights, out = [], []
            for h in range(HEAD_GROUP):
                l = _dot_nt(qh[h][rows], kblk)
                a, lnb, blk_sum = _sb_block(l, mask, tri_suffix, carries[h][rows])
                beta = jnp.exp(l + lnb)
                if masked:
                    beta = jnp.where(mask, beta, 0.0)
                g_s[h, kb, rows, :] = a * _dot_nt(doh[h][rows], vblk)
                beta_s[h, kb, rows, :] = beta.astype(BF16)
                weights.append(a.astype(BF16))
                out.append(carries[h][rows] + blk_sum)
            dv_acc[pl.ds(ks, Q_BLOCK), :] += _dot_tn(jnp.concatenate(weights, axis=0),
                                                     jnp.concatenate([t[rows] for t in doh], axis=0))
            return out

        part = block_a(2 * qi + 1, [jnp.zeros((Q_TILE, 1), F32)] * HEAD_GROUP, second, True)
        carries = tuple(jnp.concatenate([zero_col, c], axis=0) for c in part)
        carries = tuple(block_a(2 * qi, carries, everything, True))
        lax.fori_loop(1, 2 * qi + 1, lambda jj, cs: tuple(block_a(2 * qi - jj, cs, everything, False)), carries)

        def block_b(kb, prefixes, rows, masked):
            ks = pl.multiple_of(kb * Q_BLOCK, Q_BLOCK)
            kstack = jnp.concatenate(_split_heads(_key_block(k_ref, kb), head), axis=0)
            dls, out = [], []
            for h in range(HEAD_GROUP):
                g = g_s[h, kb, rows, :]
                beta = beta_s[h, kb, rows, :].astype(F32)
                incl = _sum_along_keys(g, tri_prefix)
                before = incl - g + prefixes[h][rows]
                dl = g - beta * (g + before)
                if masked:
                    dl = jnp.where(diag[:Q_TILE - rows.start], dl, 0.0)
                dls.append(dl.astype(BF16))
                out.append(prefixes[h][rows] + incl[:, Q_BLOCK - 1:Q_BLOCK])
            dq_acc[rows, :] += _dot(jnp.concatenate(dls, axis=1), kstack)
            dk_acc[pl.ds(ks, Q_BLOCK), :] += _dot_tn(jnp.concatenate(dls, axis=0),
                                                     jnp.concatenate([t[rows] for t in qh], axis=0))
            return out

        zeros = (jnp.zeros((Q_TILE, 1), F32),) * HEAD_GROUP
        prefixes = lax.fori_loop(0, 2 * qi, lambda kb, ps: tuple(block_b(kb, ps, everything, False)), zeros)
        prefixes = block_b(2 * qi, prefixes, everything, True)
        block_b(2 * qi + 1, prefixes, second, True)
        dq_ref[...] = (dq_acc[...] * Q_SCALE).astype(BF16)

        @pl.when(qi == nq - 1)
        def _():
            dk_ref[...] = dk_acc[...].astype(BF16)
            dv_ref[...] = dv_acc[...].astype(BF16)

    t = nb * seq
    qspec = pl.BlockSpec((Q_TILE, GROUP_W), lambda b, g, i: (b * nq + i, g))
    kvout = pl.BlockSpec((seq, GROUP_W), lambda b, g, i: (b, g))
    return pl.pallas_call(
        body, grid=(nb, N_GROUPS, nq),
        in_specs=[qspec,
                  pl.BlockSpec((seq, GROUP_W), lambda b, g, i: (b, N_GROUPS + g)),
                  pl.BlockSpec((seq, GROUP_W), lambda b, g, i: (b, 2 * N_GROUPS + g)),
                  qspec],
        out_specs=[qspec, kvout, kvout],
        out_shape=[jax.ShapeDtypeStruct((t, D_MODEL), BF16)] * 3,
        scratch_shapes=[pltpu.VMEM((HEAD_GROUP, n_kb, Q_TILE, Q_BLOCK), F32),
                        pltpu.VMEM((HEAD_GROUP, n_kb, Q_TILE, Q_BLOCK), BF16),
                        pltpu.VMEM((Q_TILE, GROUP_W), F32),
                        pltpu.VMEM((seq, GROUP_W), F32), pltpu.VMEM((seq, GROUP_W), F32)],
        compiler_params=_params("parallel", "parallel", "arbitrary"), name="attn_bwd")(qkv, qkv, qkv, do)


def _conv_block_rows(seq):
    return min(256, seq)


SUBLANES = 8
CONV_SUB = 32
CONV_GROUPS = CONV_SUB // SUBLANES


def _fill_shifts(sh_ref, n):
    for r in range(1, SUBLANES):
        sh_ref[r, 0:n, :] = sh_ref[0, pl.ds(r, n), :]


def _shifted(sh_ref, start, cs):
    r = start % SUBLANES
    return sh_ref[r, start - r:start - r + CONV_SUB, cs].reshape(CONV_GROUPS, SUBLANES, -1)


def _tap(w_ref, k, cs):
    row = w_ref[k:k + 1, cs]
    return jnp.broadcast_to(row, (SUBLANES, row.shape[1]))[None]


def _tree_sum(x):
    parts = [x[i] for i in range(x.shape[0])]
    while len(parts) > 1:
        parts = [parts[i] + parts[i + 1] for i in range(0, len(parts), 2)]
    return parts[0]


def _conv_fwd(hrest, conv_w, conv_b, ln_g, ln_b, nb, seq):
    ts = _conv_block_rows(seq)
    nblk = seq // ts
    lanes = 128

    def body(cv_ref, cg_ref, cvh_ref, cgh_ref, z_ref, w_ref, cb_ref, g_ref, b_ref, u1_ref, cvin_ref, ush):
        i = pl.program_id(1)
        halo = cvh_ref[...] * _sigmoid(cgh_ref[...])
        ush[0, 0:HALO, :] = jnp.where(i > 0, halo, 0.0)
        ush[0, HALO:HALO + ts, :] = cv_ref[...] * _sigmoid(cg_ref[...])
        _fill_shifts(ush, HALO + ts - SUBLANES)
        for cc in range(D_MODEL // lanes):
            cs = slice(cc * lanes, (cc + 1) * lanes)
            for r0 in range(0, ts, CONV_SUB):
                acc = jnp.zeros((CONV_GROUPS, SUBLANES, lanes), F32) + _tap(cb_ref, 0, cs)
                for k in range(CONV_K):
                    acc = acc + _tap(w_ref, k, cs) * _shifted(ush, HALO - CONV_K + 1 + k + r0, cs)
                u1_ref[r0:r0 + CONV_SUB, cs] = acc.reshape(CONV_SUB, lanes)
        xhat, _ = _ln_stats(u1_ref[...])
        u2 = xhat * g_ref[...] + b_ref[...]
        z = z_ref[...]
        cvin_ref[...] = (u2 * _sigmoid(u2) * z * _sigmoid(z)).astype(BF16)

    def main(colblk):
        return pl.BlockSpec((ts, D_MODEL), lambda b, i: (b * nblk + i, colblk))

    def halo(colblk):
        return pl.BlockSpec((HALO, D_MODEL),
                            lambda b, i: (jnp.maximum((b * seq + i * ts) // HALO - 1, 0), colblk))

    vec = pl.BlockSpec((1, D_MODEL), lambda b, i: (0, 0))
    t = nb * seq
    return pl.pallas_call(
        body, grid=(nb, nblk),
        in_specs=[main(1), main(2), halo(1), halo(2), main(3),
                  pl.BlockSpec((CONV_ROWS, D_MODEL), lambda b, i: (0, 0)), vec, vec, vec],
        out_specs=[main(0), main(0)],
        out_shape=[jax.ShapeDtypeStruct((t, D_MODEL), F32), jax.ShapeDtypeStruct((t, D_MODEL), BF16)],
        scratch_shapes=[pltpu.VMEM((SUBLANES, HALO + ts, D_MODEL), F32)],
        compiler_params=_params("parallel", "parallel"), name="conv_fwd")(
            hrest, hrest, hrest, hrest, hrest, conv_w, conv_b, ln_g, ln_b)


def _tail(o, hrest, cvin, h0, tgt, w_sb, w_cv, w_out, ln_g, ln_b):
    t = o.shape[0]
    tm = min(256, t)

    def body(o_ref, z_ref, gs_ref, gc_ref, cvin_ref, h0_ref, tgt_ref, wsb_ref, wcv_ref, wout_ref, g_ref, b_ref,
             dr_ref, drb_ref, a_ref, mg_ref, dysb_ref, dycv_ref, do_ref, dz_ref, dgs_ref, dgc_ref, dcvin_ref,
             st_ref):
        @pl.when(pl.program_id(0) == 0)
        def _():
            st_ref[...] = jnp.zeros_like(st_ref)

        o = o_ref[...]
        z = z_ref[...]
        sz = _sigmoid(z)
        a = (o * z * sz).astype(BF16)
        a_ref[...] = a
        y_sb = _dot(a, wsb_ref[...])
        y_cv = _dot(cvin_ref[...], wcv_ref[...])
        s_sb = _sigmoid(gs_ref[...])
        s_cv = _sigmoid(gc_ref[...])
        merged = (s_sb * y_sb + s_cv * y_cv).astype(BF16)
        mg_ref[...] = merged
        r = ALPHA * h0_ref[...] + _dot(merged, wout_ref[...])
        xhat, rstd = _ln_stats(r)
        g = g_ref[...]
        err = xhat * g + b_ref[...] - tgt_ref[...]
        dy = err * (1.0 / D_MODEL)
        st_ref[0:1, :] += jnp.sum(dy * xhat, axis=0, keepdims=True)
        st_ref[1:2, :] += jnp.sum(dy, axis=0, keepdims=True)
        st_ref[2:3, :] += (0.5 / D_MODEL) * jnp.sum(err * err, axis=0, keepdims=True)
        dr = _ln_bwd(dy, xhat, rstd, g)
        dr_ref[...] = dr
        drb = dr.astype(BF16)
        drb_ref[...] = drb
        dm = _dot_nt(drb, wout_ref[...])
        dy_sb = (dm * s_sb).astype(BF16)
        dy_cv = (dm * s_cv).astype(BF16)
        dysb_ref[...] = dy_sb
        dycv_ref[...] = dy_cv
        dgs_ref[...] = (dm * y_sb * s_sb * (1.0 - s_sb)).astype(BF16)
        dgc_ref[...] = (dm * y_cv * s_cv * (1.0 - s_cv)).astype(BF16)
        da = _dot_nt(dy_sb, wsb_ref[...])
        dcvin_ref[...] = _dot_nt(dy_cv, wcv_ref[...])
        do_ref[...] = (da * z * sz).astype(BF16)
        dz_ref[...] = (da * o * sz * (1.0 + z * (1.0 - sz))).astype(BF16)

    def tok(colblk=0):
        return pl.BlockSpec((tm, D_MODEL), lambda i: (i, colblk))

    wspec = pl.BlockSpec((D_MODEL, D_MODEL), lambda i: (0, 0), pipeline_mode=pl.Buffered(1))
    vec = pl.BlockSpec((1, D_MODEL), lambda i: (0, 0))
    bf = jax.ShapeDtypeStruct((t, D_MODEL), BF16)
    f32 = jax.ShapeDtypeStruct((t, D_MODEL), F32)
    return pl.pallas_call(
        body, grid=(t // tm,),
        in_specs=[tok(), tok(0), tok(4), tok(5), tok(), tok(), tok(), wspec, wspec, wspec, vec, vec],
        out_specs=[tok()] * 11 + [pl.BlockSpec((8, D_MODEL), lambda i: (0, 0))],
        out_shape=[f32] + [bf] * 9 + [f32, jax.ShapeDtypeStruct((8, D_MODEL), F32)],
        compiler_params=_params("arbitrary"), name="tail")(
            o, hrest, hrest, hrest, cvin, h0, tgt, w_sb, w_cv, w_out, ln_g, ln_b)


def _conv_branch_bwd(dcvin, u1, hrest, ln_g, ln_b):
    t = u1.shape[0]
    tm = min(512, t)

    def body(dc_ref, u1_ref, z_ref, g_ref, b_ref, du1_ref, dz_ref, st_ref):
        @pl.when(pl.program_id(0) == 0)
        def _():
            st_ref[...] = jnp.zeros_like(st_ref)

        xhat, rstd = _ln_stats(u1_ref[...])
        g = g_ref[...]
        u2 = xhat * g + b_ref[...]
        s2 = _sigmoid(u2)
        z = z_ref[...]
        sz = _sigmoid(z)
        dc = dc_ref[...]
        dz_ref[...] = (dc * (u2 * s2) * sz * (1.0 + z * (1.0 - sz))).astype(BF16)
        du2 = dc * (z * sz) * s2 * (1.0 + u2 * (1.0 - s2))
        st_ref[0:1, :] += jnp.sum(du2 * xhat, axis=0, keepdims=True)
        st_ref[1:2, :] += jnp.sum(du2, axis=0, keepdims=True)
        du1 = _ln_bwd(du2, xhat, rstd, g)
        du1_ref[...] = du1
        st_ref[2:3, :] += jnp.sum(du1, axis=0, keepdims=True)

    tok = pl.BlockSpec((tm, D_MODEL), lambda i: (i, 0))
    vec = pl.BlockSpec((1, D_MODEL), lambda i: (0, 0))
    return pl.pallas_call(
        body, grid=(t // tm,),
        in_specs=[tok, tok, pl.BlockSpec((tm, D_MODEL), lambda i: (i, 3)), vec, vec],
        out_specs=[tok, tok, pl.BlockSpec((8, D_MODEL), lambda i: (0, 0))],
        out_shape=[jax.ShapeDtypeStruct((t, D_MODEL), F32), jax.ShapeDtypeStruct((t, D_MODEL), BF16),
                   jax.ShapeDtypeStruct((8, D_MODEL), F32)],
        compiler_params=_params("arbitrary"), name="conv_branch_bwd")(dcvin, u1, hrest, ln_g, ln_b)


def _conv_bwd(du1, hrest, conv_w, nb, seq):
    ts = _conv_block_rows(seq)
    nblk = seq // ts
    lanes = 128
    last_halo = nb * seq // HALO - 1

    def body(du_ref, duh_ref, cv_ref, cg_ref, cvh_ref, cgh_ref, w_ref, dcv_ref, dcg_ref, dw_ref, ush, dsh, dw_part):
        b = pl.program_id(0)
        i = pl.program_id(1)

        @pl.when((b == 0) & (i == 0))
        def _():
            dw_part[...] = jnp.zeros_like(dw_part)

        cv = cv_ref[...]
        sg = _sigmoid(cg_ref[...])
        halo = cvh_ref[...] * _sigmoid(cgh_ref[...])
        ush[0, 0:HALO, :] = jnp.where(i > 0, halo, 0.0)
        ush[0, HALO:HALO + ts, :] = cv * sg
        dsh[0, 0:ts, :] = du_ref[...]
        dsh[0, ts:ts + HALO, :] = jnp.where(i < nblk - 1, duh_ref[...], 0.0)
        _fill_shifts(ush, HALO + ts - SUBLANES)
        _fill_shifts(dsh, HALO + ts - SUBLANES)
        for cc in range(D_MODEL // lanes):
            cs = slice(cc * lanes, (cc + 1) * lanes)
            for r0 in range(0, ts, CONV_SUB):
                rows = slice(r0, r0 + CONV_SUB)
                dmain = _shifted(dsh, r0, cs)
                acc = jnp.zeros((CONV_GROUPS, SUBLANES, lanes), F32)
                for k in range(CONV_K):
                    acc = acc + _tap(w_ref, k, cs) * _shifted(dsh, CONV_K - 1 - k + r0, cs)
                    dw_part[k, :, cs] += _tree_sum(dmain * _shifted(ush, HALO - CONV_K + 1 + k + r0, cs))
                acc = acc.reshape(CONV_SUB, lanes)
                dcv_ref[rows, cs] = (acc * sg[rows, cs]).astype(BF16)
                dcg_ref[rows, cs] = (acc * cv[rows, cs] * sg[rows, cs] * (1.0 - sg[rows, cs])).astype(BF16)

        @pl.when((b == nb - 1) & (i == nblk - 1))
        def _():
            for k in range(CONV_ROWS):
                dw_ref[k:k + 1, :] = jnp.sum(dw_part[k], axis=0, keepdims=True)

    def main(colblk):
        return pl.BlockSpec((ts, D_MODEL), lambda b, i: (b * nblk + i, colblk))

    def halo_before(colblk):
        return pl.BlockSpec((HALO, D_MODEL),
                            lambda b, i: (jnp.maximum((b * seq + i * ts) // HALO - 1, 0), colblk))

    halo_after = pl.BlockSpec((HALO, D_MODEL),
                              lambda b, i: (jnp.minimum((b * seq + (i + 1) * ts) // HALO, last_halo), 0))
    t = nb * seq
    return pl.pallas_call(
        body, grid=(nb, nblk),
        in_specs=[main(0), halo_after, main(1), main(2), halo_before(1), halo_before(2),
                  pl.BlockSpec((CONV_ROWS, D_MODEL), lambda b, i: (0, 0))],
        out_specs=[main(0), main(0), pl.BlockSpec((CONV_ROWS, D_MODEL), lambda b, i: (0, 0))],
        out_shape=[jax.ShapeDtypeStruct((t, D_MODEL), BF16), jax.ShapeDtypeStruct((t, D_MODEL), BF16),
                   jax.ShapeDtypeStruct((CONV_ROWS, D_MODEL), F32)],
        scratch_shapes=[pltpu.VMEM((SUBLANES, HALO + ts, D_MODEL), F32), pltpu.VMEM((SUBLANES, ts + HALO, D_MODEL), F32),
                        pltpu.VMEM((CONV_ROWS, SUBLANES, D_MODEL), F32)],
        compiler_params=_params("arbitrary", "arbitrary"), name="conv_bwd")(
            du1, du1, hrest, hrest, hrest, hrest, conv_w)


def _weight_grad(x, dys, name):
    n = len(dys)
    t = x.shape[0]
    tm = min(1024, t)

    def body(x_ref, *refs):
        o_ref = refs[n]
        j = pl.program_id(0)

        @pl.when(pl.program_id(1) == 0)
        def _():
            o_ref[...] = jnp.zeros_like(o_ref)

        for p in range(n):
            @pl.when(j == p)
            def _(p=p):
                o_ref[...] += _dot_tn(x_ref[...], refs[p][...])

    def dy_spec(p):
        return pl.BlockSpec((tm, D_MODEL), lambda j, i: (jnp.where(j == p, i, 0), 0))

    return pl.pallas_call(
        body, grid=(n, t // tm),
        in_specs=[pl.BlockSpec((tm, D_MODEL), lambda j, i: (i, 0))] + [dy_spec(p) for p in range(n)],
        out_specs=pl.BlockSpec((D_MODEL, D_MODEL), lambda j, i: (0, j)),
        out_shape=jax.ShapeDtypeStruct((D_MODEL, n * D_MODEL), F32),
        compiler_params=_params("parallel", "arbitrary"), name=name)(x, *dys)


def _in_proj_bwd(dhs, w_in, dr, x, ln_g, pair):
    n = len(dhs)
    t = x.shape[0]
    tm = min(512, t)
    n_i = t // tm

    def body(*refs):
        dh_refs = refs[:n]
        w_ref, dr_ref, x_ref, g_ref = refs[n:n + 4]
        pair_refs = refs[n + 4:n + 8]
        gx_ref, st_ref = refs[n + 8:n + 10]
        got_refs = refs[n + 10:n + 14]
        acc, send_sems, recv_sems = refs[n + 14:]
        p = pl.program_id(0)
        i = pl.program_id(1)
        pieces, wholes = _chip_exchange_copies(pair_refs, got_refs, send_sems, recv_sems)

        @pl.when((p == 0) & (i == 0))
        def _():
            st_ref[...] = jnp.zeros_like(st_ref)
            for cp in pieces:
                cp.start()

        @pl.when((p == n - 1) & (i == n_i - 1))
        def _():
            for cp in wholes:
                cp.wait_recv()
            for cp in wholes:
                cp.wait_send()

        @pl.when(p == 0)
        def _():
            acc[i] = ALPHA * dr_ref[...]

        for q in range(n):
            @pl.when(p == q)
            def _(q=q):
                acc[i] += _dot_nt(dh_refs[q][...], w_ref[...])

        @pl.when(p == n - 1)
        def _():
            xhat, rstd = _ln_stats(x_ref[...])
            dh0 = acc[i]
            st_ref[0:1, :] += jnp.sum(dh0 * xhat, axis=0, keepdims=True)
            st_ref[1:2, :] += jnp.sum(dh0, axis=0, keepdims=True)
            gx_ref[...] = _ln_bwd(dh0, xhat, rstd, g_ref[...])

    def tok(piece):
        return pl.BlockSpec((tm, D_MODEL), lambda p, i: (jnp.where(p == piece, i, 0), 0))

    got = [jax.ShapeDtypeStruct((3, HALF, SHARD_IN), BF16)] + [jax.ShapeDtypeStruct((3, SHARD_SQ, HALF), BF16)] * 3
    res = pl.pallas_call(
        body, grid=(n, n_i),
        in_specs=[tok(q) for q in range(n)] + [pl.BlockSpec((D_MODEL, D_MODEL), lambda p, i: (0, p)), tok(0), tok(n - 1),
                                               pl.BlockSpec((1, D_MODEL), lambda p, i: (0, 0))] + [ANY] * 4,
        out_specs=[tok(n - 1), pl.BlockSpec((8, D_MODEL), lambda p, i: (0, 0))] + [ANY] * 4,
        out_shape=[jax.ShapeDtypeStruct((t, D_MODEL), F32), jax.ShapeDtypeStruct((8, D_MODEL), F32)] + got,
        scratch_shapes=[pltpu.VMEM((n_i, tm, D_MODEL), F32), pltpu.SemaphoreType.DMA((12,)),
                        pltpu.SemaphoreType.DMA((12,))],
        compiler_params=_params("arbitrary", "arbitrary"), name="in_proj_bwd")(*dhs, w_in, dr, x, ln_g, *pair)
    return res[0], res[1], res[2:]


def _forward_backward(x, tgt, nb, seq, shards, vecs):
    w_in_early = _gather_first(shards[0])
    h0, h0b = _ln_in(x, vecs["ln_in_g"], vecs["ln_in_b"])
    qkv = _in_proj(h0b, w_in_early, 0, 3, BF16, True, "in_proj_qkv")
    o, (w_in, w_sb, w_cv, w_out, conv_w) = _attn_fwd(qkv, nb, seq, w_in_early, shards)
    hrest = _in_proj(h0b, w_in, 3, 6, F32, False, "in_proj_rest")
    u1, cvin = _conv_fwd(hrest, conv_w, vecs["conv_b"], vecs["conv_ln_g"], vecs["conv_ln_b"], nb, seq)
    (dr, drb, a, merged, dy_sb, dy_cv, do, dz_sb, dg_sb, dg_cv, dcvin, st_tail) = _tail(
        o, hrest, cvin, h0, tgt, w_sb, w_cv, w_out, vecs["ln_post_g"], vecs["ln_post_b"])
    d_w_sb = _weight_grad(a, [dy_sb], "grad_w_sb")
    d_w_cv = _weight_grad(cvin, [dy_cv], "grad_w_cv")
    d_w_out = _weight_grad(merged, [drb], "grad_w_out")
    du1, dz_cv, st_conv = _conv_branch_bwd(dcvin, u1, hrest, vecs["conv_ln_g"], vecs["conv_ln_b"])
    dc_val, dc_gate, d_conv_w = _conv_bwd(du1, hrest, conv_w, nb, seq)
    dq, dk, dv = _attn_bwd(qkv, do, nb, seq)
    dhs = [dq, dk, dv, dz_sb, dc_val, dc_gate, dz_cv, dg_sb, dg_cv]
    d_w_in = _weight_grad(h0b, dhs, "grad_w_in")
    small = jnp.concatenate([st_conv[2:3], st_conv[0:2], st_tail[0:3], d_conv_w], axis=0)
    return w_in, dhs, dr, (d_w_in, d_w_sb, d_w_cv, d_w_out), small


def _place():
    x, y, c = lax.axis_index("x"), lax.axis_index("y"), lax.axis_index("c")
    chips = [(1 - x, y), (x, 1 - y), (1 - x, 1 - y)]
    return x, y, c, chips


EARLY_COLS = tuple(min(max(3 * D_MODEL - chip * SHARD_IN, 0), SHARD_IN) for chip in range(N_CHIPS))
EARLY = tuple((chip, cols) for chip, cols in enumerate(EARLY_COLS) if cols)
LATE = tuple((chip, cols) for chip, cols in enumerate(EARLY_COLS) if cols < SHARD_IN)
GATHER_PARTS = 4


def _w_in_slot(full, chip):
    return full.at[:, pl.ds(pl.multiple_of(chip * SHARD_IN, 128), SHARD_IN)]


def _gather_first(w_in):
    part_rows = HALF // GATHER_PARTS
    n_sem = 3 * GATHER_PARTS

    def body(src, dst, stage, ici_send, ici_recv, fwd_send, fwd_recv, local_sems):
        x, y, c, chips = _place()
        me = 2 * x + y
        sibling = (x, y, 1 - c)
        owner = [2 * px + py for px, py in chips]

        def part(ref, k, half, cols):
            return ref.at[pl.ds(pl.multiple_of(half * HALF + k * part_rows, 16), part_rows), pl.ds(0, cols)]

        def held(j, k, half, cols):
            return part(_w_in_slot(dst, owner[j]), k, half, cols)

        def sent(j, k, cols):
            px, py = chips[j]
            return _remote(part(src, k, c, cols), part(_w_in_slot(dst, me), k, c, cols),
                           ici_send.at[j * GATHER_PARTS + k], ici_recv.at[j * GATHER_PARTS + k], (px, py, c))

        def landed(j, k, cols):
            px, py = chips[j]
            return _remote(held(j, k, c, cols), held(j, k, c, cols), ici_send.at[j * GATHER_PARTS + k],
                           ici_recv.at[j * GATHER_PARTS + k], (px, py, c))

        def passed(j, k, half, cols):
            return _remote(held(j, k, half, cols), held(j, k, half, cols), fwd_send.at[j * GATHER_PARTS + k],
                           fwd_recv.at[j * GATHER_PARTS + k], sibling)

        load = pltpu.make_async_copy(src, stage, local_sems.at[0])
        local = pltpu.make_async_copy(stage, _w_in_slot(dst, me), local_sems.at[1])
        load.start()

        for chip, cols in EARLY:
            @pl.when(me == chip)
            def _(cols=cols):
                for j in range(3):
                    for k in range(GATHER_PARTS):
                        sent(j, k, cols).start()

        load.wait()
        local.start()

        for j in range(3):
            for chip, cols in EARLY:
                @pl.when(owner[j] == chip)
                def _(j=j, cols=cols):
                    for k in range(GATHER_PARTS):
                        landed(j, k, cols).wait_recv()
                        passed(j, k, c, cols).start()

        for j in range(3):
            for chip, cols in EARLY:
                @pl.when(owner[j] == chip)
                def _(j=j, cols=cols):
                    for k in range(GATHER_PARTS):
                        passed(j, k, 1 - c, cols).wait_recv()
                    for k in range(GATHER_PARTS):
                        passed(j, k, c, cols).wait_send()

        for chip, cols in EARLY:
            @pl.when(me == chip)
            def _(cols=cols):
                for j in range(3):
                    for k in range(GATHER_PARTS):
                        sent(j, k, cols).wait_send()

        local.wait()

    return pl.pallas_call(
        body, in_specs=[ANY], out_specs=ANY, out_shape=jax.ShapeDtypeStruct((D_MODEL, IN_WIDTH), BF16),
        scratch_shapes=[pltpu.VMEM((D_MODEL, SHARD_IN), BF16),
                        pltpu.SemaphoreType.DMA((n_sem,)), pltpu.SemaphoreType.DMA((n_sem,)),
                        pltpu.SemaphoreType.DMA((n_sem,)), pltpu.SemaphoreType.DMA((n_sem,)),
                        pltpu.SemaphoreType.DMA((2,))],
        compiler_params=pltpu.CompilerParams(vmem_limit_bytes=VMEM_LIMIT), name="gather_first")(w_in)


def _late_gather(srcs, dsts, send_sems, recv_sems, local_sems):
    x, y, c, chips = _place()
    me = 2 * x + y
    small = range(1, 5)

    def slot(a, chip):
        if a == 0:
            return _w_in_slot(dsts[0], chip)
        if a == 4:
            return dsts[a].at[:, pl.ds(pl.multiple_of(chip * SHARD_SQ, 128), SHARD_SQ)]
        return dsts[a].at[pl.ds(pl.multiple_of(chip * SHARD_SQ, 16), SHARD_SQ), :]

    def sems(a, j):
        return send_sems.at[3 * a + j], recv_sems.at[3 * a + j]

    def whole(a, j):
        px, py = chips[j]
        return _remote(srcs[a], slot(a, me), *sems(a, j), (px, py, c))

    def arrival(a, j):
        px, py = chips[j]
        return _arrival(slot(a, 2 * px + py), *sems(a, j), (px, py, c))

    def local(a):
        return pltpu.make_async_copy(srcs[a], slot(a, me), local_sems.at[a - 1])

    def late_src(first):
        return srcs[0].at[:, pl.ds(first, SHARD_IN - first)]

    def late_dst(chip, first):
        return _w_in_slot(dsts[0], chip).at[:, pl.ds(first, SHARD_IN - first)]

    def start():
        for a in small:
            local(a).start()
            for j in range(3):
                whole(a, j).start()

        for chip, first in LATE:
            @pl.when(me == chip)
            def _(chip=chip, first=first):
                for j, (px, py) in enumerate(chips):
                    for cp in _chunk_copies(late_src(first), late_dst(chip, first), GATHER_PARTS, *sems(0, j), (px, py, c)):
                        cp.start()

    def finish():
        for j, (px, py) in enumerate(chips):
            for a in small:
                arrival(a, j).wait_recv()

            for chip, first in LATE:
                @pl.when(2 * px + py == chip)
                def _(j=j, px=px, py=py, chip=chip, first=first):
                    _arrival(late_dst(chip, first), *sems(0, j), (px, py, c)).wait_recv()

        for a in small:
            for j in range(3):
                whole(a, j).wait_send()
            local(a).wait()

        for chip, first in LATE:
            @pl.when(me == chip)
            def _(chip=chip, first=first):
                for j, (px, py) in enumerate(chips):
                    _remote(late_src(first), late_dst(chip, first), *sems(0, j), (px, py, c)).wait_send()

    return start, finish


def _remote(src, dst, send_sem, recv_sem, device):
    return pltpu.make_async_remote_copy(src_ref=src, dst_ref=dst, send_sem=send_sem, recv_sem=recv_sem,
                                        device_id=device, device_id_type=MESH)


def _chunk_copies(src, dst, n, send_sem, recv_sem, device):
    rows = src.shape[0] // n
    return [_remote(src.at[pl.ds(k * rows, rows)], dst.at[pl.ds(k * rows, rows)], send_sem, recv_sem, device)
            for k in range(n)]


def _row_chunks(src, dst, n, send_sem, recv_sem, device):
    return _chunk_copies(src, dst, n, send_sem, recv_sem, device), _remote(src, dst, send_sem, recv_sem, device)


def _arrival(dst, send_sem, recv_sem, device):
    return _remote(dst, dst, send_sem, recv_sem, device)


def _pair_exchange(d_in, d_sb, d_cv, d_out):
    def body(i0, i1, i2, i3, r0, r1, r2, r3, send_sems, recv_sems):
        x, y, c, _ = _place()
        srcs = [i0, i1, i2, i3]
        theirs = [r0, r1, r2, r3]
        sibling = (x, y, 1 - c)
        start = pl.multiple_of((1 - c) * HALF, 128)
        views = [srcs[0].at[pl.ds(start, HALF), :]] + [srcs[a].at[:, pl.ds(start, HALF)] for a in range(1, 4)]
        pieces, wholes = [], []
        for a in range(4):
            cps, whole = _row_chunks(views[a], theirs[a], N_CHUNKS, send_sems.at[a], recv_sems.at[a], sibling)
            pieces += cps
            wholes.append(whole)
        for cp in pieces:
            cp.start()
        for cp in wholes:
            cp.wait_recv()
        for cp in wholes:
            cp.wait_send()

    shapes = [jax.ShapeDtypeStruct((HALF, IN_WIDTH), F32)] + [jax.ShapeDtypeStruct((D_MODEL, HALF), F32)] * 3
    return pl.pallas_call(
        body, in_specs=[ANY] * 4, out_specs=[ANY] * 4, out_shape=shapes,
        scratch_shapes=[pltpu.SemaphoreType.DMA((4,)), pltpu.SemaphoreType.DMA((4,))],
        name="pair_exchange")(d_in, d_sb, d_cv, d_out)


def _chip_exchange_copies(srcs, theirs, send_sems, recv_sems):
    _, _, c, chips = _place()

    def shard(a, px, py):
        chip = 2 * px + py
        if a == 0:
            return srcs[a].at[:, pl.ds(pl.multiple_of(chip * SHARD_IN, 128), SHARD_IN)]
        return srcs[a].at[pl.ds(pl.multiple_of(chip * SHARD_SQ, 16), SHARD_SQ), :]

    pieces, wholes = [], []
    for j, (px, py) in enumerate(chips):
        for a in range(4):
            cps, whole = _row_chunks(shard(a, px, py), theirs[a].at[j], 4, send_sems.at[3 * a + j],
                                     recv_sems.at[3 * a + j], (px, py, c))
            pieces += cps
            wholes.append(whole)
    return pieces, wholes


def _pair_share(f_in, f_sb, f_cv, f_out):
    def body(i0, i1, i2, i3, o0, o1, o2, o3, send_sems, recv_sems):
        del i0, i1, i2, i3
        x, y, c, _ = _place()
        dsts = [o0, o1, o2, o3]
        sibling = (x, y, 1 - c)

        def half(a, which):
            start = pl.multiple_of(which * HALF, 128)
            if a == 0:
                return dsts[a].at[pl.ds(start, HALF), :]
            return dsts[a].at[:, pl.ds(start, HALF)]

        pieces, sent, arrived = [], [], []
        for a in range(4):
            cps, whole = _row_chunks(half(a, c), half(a, c), N_CHUNKS, send_sems.at[a], recv_sems.at[a], sibling)
            pieces += cps
            sent.append(whole)
            arrived.append(_arrival(half(a, 1 - c), send_sems.at[a], recv_sems.at[a], sibling))
        for cp in pieces:
            cp.start()
        for cp in arrived:
            cp.wait_recv()
        for cp in sent:
            cp.wait_send()

    return pl.pallas_call(
        body, in_specs=[ANY] * 4, out_specs=[ANY] * 4,
        out_shape=[jax.ShapeDtypeStruct((D_MODEL, SHARD_IN), F32)] + [jax.ShapeDtypeStruct((SHARD_SQ, D_MODEL), F32)] * 3,
        input_output_aliases={0: 0, 1: 1, 2: 2, 3: 3},
        scratch_shapes=[pltpu.SemaphoreType.DMA((4,)), pltpu.SemaphoreType.DMA((4,))],
        name="pair_share")(f_in, f_sb, f_cv, f_out)


def _small_allreduce(small):
    def body(s_ref, o_ref, slots, send_sems, recv_sems):
        x, y, c, _ = _place()
        me = 4 * x + 2 * y + c
        slots[me] = s_ref[...]
        sends = []
        for k in range(1, N_DEV):
            px, py, pc = (x + (k >> 2)) % 2, (y + ((k >> 1) & 1)) % 2, (c + (k & 1)) % 2
            sends.append(pltpu.make_async_remote_copy(
                src_ref=s_ref, dst_ref=slots.at[me], send_sem=send_sems.at[k - 1], recv_sem=recv_sems.at[k - 1],
                device_id=(px, py, pc), device_id_type=MESH))
        for cp in sends:
            cp.start()
        for k in range(1, N_DEV):
            px, py, pc = (x + (k >> 2)) % 2, (y + ((k >> 1) & 1)) % 2, (c + (k & 1)) % 2
            pltpu.make_async_remote_copy(
                src_ref=s_ref, dst_ref=slots.at[4 * px + 2 * py + pc], send_sem=send_sems.at[k - 1],
                recv_sem=recv_sems.at[k - 1], device_id=(px, py, pc), device_id_type=MESH).wait_recv()
        for cp in sends:
            cp.wait_send()
        total = slots[0]
        for d in range(1, N_DEV):
            total = total + slots[d]
        o_ref[...] = total
        o_ref[7:8, :] = jnp.zeros((1, D_MODEL), F32) + jnp.sum(total[7:8, :], axis=1, keepdims=True)

    return pl.pallas_call(
        body, in_specs=[pl.BlockSpec(memory_space=pltpu.VMEM)], out_specs=pl.BlockSpec(memory_space=pltpu.VMEM),
        out_shape=jax.ShapeDtypeStruct((SMALL_ROWS, D_MODEL), F32),
        scratch_shapes=[pltpu.VMEM((N_DEV, SMALL_ROWS, D_MODEL), F32), pltpu.SemaphoreType.DMA((N_DEV - 1,)),
                        pltpu.SemaphoreType.DMA((N_DEV - 1,))],
        name="small_allreduce")(small)


SUM_BLOCKS = 8


def _pair_sum(place, full, theirs, by_rows, name):
    rows, cols = theirs.shape
    tr = rows // SUM_BLOCKS

    def body(p_ref, a_ref, b_ref, o_ref):
        del p_ref
        o_ref[...] = (a_ref[...] + b_ref[...]).astype(BF16)

    mine = (lambda i, p: (p[1] * SUM_BLOCKS + i, 0)) if by_rows else (lambda i, p: (i, p[1]))
    spec = pl.BlockSpec((tr, cols), lambda i, p: (i, 0))
    return pl.pallas_call(
        body, out_shape=jax.ShapeDtypeStruct(theirs.shape, BF16),
        grid_spec=pltpu.PrefetchScalarGridSpec(num_scalar_prefetch=1, grid=(SUM_BLOCKS,),
                                               in_specs=[pl.BlockSpec((tr, cols), mine), spec], out_specs=spec),
        compiler_params=_params("parallel"), name=name)(place, full, theirs)


def _chip_sum(place, pair, got, by_rows, name):
    _, rows, cols = got.shape
    tr = rows // SUM_BLOCKS

    def body(p_ref, a_ref, g_ref, o_ref):
        del p_ref
        o_ref[...] = ((a_ref[...].astype(F32) + g_ref[0].astype(F32)) + g_ref[1].astype(F32)) + g_ref[2].astype(F32)

    if by_rows:
        own = lambda i, p: (i, p[0])
        out = lambda i, p: (p[1] * SUM_BLOCKS + i, 0)
        full = (2 * rows, cols)
    else:
        own = lambda i, p: (p[0] * SUM_BLOCKS + i, 0)
        out = lambda i, p: (i, p[1])
        full = (rows, 2 * cols)
    return pl.pallas_call(
        body, out_shape=jax.ShapeDtypeStruct(full, F32),
        grid_spec=pltpu.PrefetchScalarGridSpec(
            num_scalar_prefetch=1, grid=(SUM_BLOCKS,),
            in_specs=[pl.BlockSpec((tr, cols), own), pl.BlockSpec((3, tr, cols), lambda i, p: (0, i, 0))],
            out_specs=pl.BlockSpec((tr, cols), out)),
        compiler_params=_params("parallel"), name=name)(place, pair, got)


def _adamw(w, g, m, v, name):
    rows, cols = w.shape
    tr = rows // 4 if rows % 32 == 0 else rows
    c1 = 1.0 - ADAM_B1 ** ADAM_STEP
    c2 = 1.0 - ADAM_B2 ** ADAM_STEP

    def body(w_ref, g_ref, m_ref, v_ref, d_ref, mo_ref, vo_ref):
        g = g_ref[...]
        mn = ADAM_B1 * m_ref[...] + (1.0 - ADAM_B1) * g
        vn = ADAM_B2 * v_ref[...] + (1.0 - ADAM_B2) * (g * g)
        mo_ref[...] = mn
        vo_ref[...] = vn
        d_ref[...] = -ADAM_LR * ((mn / c1) / (jnp.sqrt(vn / c2) + ADAM_EPS) + ADAM_WD * w_ref[...])

    spec = pl.BlockSpec((tr, cols), lambda i: (i, 0))
    shape = jax.ShapeDtypeStruct(w.shape, F32)
    return pl.pallas_call(body, grid=(rows // tr,), in_specs=[spec] * 4, out_specs=[spec] * 3,
                          out_shape=[shape] * 3, compiler_params=_params("parallel"), name=name)(w, g, m, v)


GRAD_NAMES = ("in", "sb", "cv", "out")


def _place_scalars():
    return jnp.stack([2 * lax.axis_index("x") + lax.axis_index("y"), lax.axis_index("c")]).astype(jnp.int32)


def _pair_sums(grads):
    place = _place_scalars()
    theirs = _pair_exchange(*grads)
    return [_pair_sum(place, grads[a], theirs[a], a == 0, "pair_sum_" + GRAD_NAMES[a]) for a in range(4)]


def _finish_weight_grads(pair, got):
    place = _place_scalars()
    done = [_chip_sum(place, pair[a], got[a], a == 0, "chip_sum_" + GRAD_NAMES[a]) for a in range(4)]
    return _pair_share(*done)


def kernel(x, ln_in_g, ln_in_b, w_in, w_sb_proj, conv_w, conv_b, conv_ln_g, conv_ln_b, w_cv_proj, w_out, ln_post_g, ln_post_b, loss_target, m_ln_in_g, m_ln_in_b, m_w_in, m_w_sb_proj, m_conv_w, m_conv_b, m_conv_ln_g, m_conv_ln_b, m_w_cv_proj, m_w_out, m_ln_post_g, m_ln_post_b, v_ln_in_g, v_ln_in_b, v_w_in, v_w_sb_proj, v_conv_w, v_conv_b, v_conv_ln_g, v_conv_ln_b, v_w_cv_proj, v_w_out, v_ln_post_g, v_ln_post_b):
    nb, seq, _ = x.shape
    t = nb * seq
    vec_names = ("ln_in_g", "ln_in_b", "conv_b", "conv_ln_g", "conv_ln_b", "ln_post_g", "ln_post_b")
    vec_w = dict(zip(vec_names, (ln_in_g, ln_in_b, conv_b, conv_ln_g, conv_ln_b, ln_post_g, ln_post_b)))
    vec_m = dict(zip(vec_names, (m_ln_in_g, m_ln_in_b, m_conv_b, m_conv_ln_g, m_conv_ln_b, m_ln_post_g, m_ln_post_b)))
    vec_v = dict(zip(vec_names, (v_ln_in_g, v_ln_in_b, v_conv_b, v_conv_ln_g, v_conv_ln_b, v_ln_post_g, v_ln_post_b)))
    vecs = {k: a.reshape(1, D_MODEL) for k, a in vec_w.items()}

    pad_taps = lambda a: jnp.pad(a.reshape(CONV_K, SHARD_SQ), ((0, CONV_ROWS - CONV_K), (0, 0)))
    shards = (w_in[0].astype(BF16), w_sb_proj[0].astype(BF16), w_cv_proj[0].astype(BF16), w_out[0].astype(BF16),
              pad_taps(conv_w))
    x2 = x.reshape(t, D_MODEL)
    full_in, dhs, dr, big, small = _forward_backward(x2, loss_target.reshape(t, D_MODEL), nb, seq, shards, vecs)
    pair = _pair_sums(big)
    grad_x, st_in, got = _in_proj_bwd(dhs, full_in, dr, x2, vecs["ln_in_g"], pair)
    g_in, g_sb, g_cv, g_out = _finish_weight_grads(pair, got)
    small = _small_allreduce(jnp.concatenate([st_in[0:2], small], axis=0))

    chip = 2 * lax.axis_index("x") + lax.axis_index("y")
    g_conv = lax.dynamic_slice(small, (8, chip * SHARD_SQ), (CONV_ROWS, SHARD_SQ))
    loss = small[7, 0]

    grads, deltas, new_m, new_v = {}, {}, {}, {}
    stack = lambda d: jnp.concatenate([d[k].reshape(1, D_MODEL) for k in vec_names] + [jnp.zeros((1, D_MODEL), F32)])
    vd, vm, vv = _adamw(stack(vec_w), small[0:8], stack(vec_m), stack(vec_v), "adamw_vectors")
    for r, k in enumerate(vec_names):
        shape = vec_w[k].shape
        grads[k] = small[r].reshape(shape)
        deltas[k], new_m[k], new_v[k] = vd[r].reshape(shape), vm[r].reshape(shape), vv[r].reshape(shape)

    big_w = {"w_in": (w_in, g_in, m_w_in, v_w_in), "w_sb_proj": (w_sb_proj, g_sb, m_w_sb_proj, v_w_sb_proj),
             "w_cv_proj": (w_cv_proj, g_cv, m_w_cv_proj, v_w_cv_proj), "w_out": (w_out, g_out, m_w_out, v_w_out)}
    for k, (w, g, m, v) in big_w.items():
        d, mn, vn = _adamw(w[0], g, m[0], v[0], "adamw_" + k)
        grads[k], deltas[k], new_m[k], new_v[k] = g[None], d[None], mn[None], vn[None]
    d, mn, vn = _adamw(pad_taps(conv_w), g_conv, pad_taps(m_conv_w), pad_taps(v_conv_w), "adamw_conv_w")
    grads["conv_w"] = g_conv[None, :CONV_K]
    deltas["conv_w"], new_m["conv_w"], new_v["conv_w"] = d[None, :CONV_K], mn[None, :CONV_K], vn[None, :CONV_K]

    order = ("ln_in_g", "ln_in_b", "w_in", "w_sb_proj", "conv_w", "conv_b", "conv_ln_g", "conv_ln_b",
             "w_cv_proj", "w_out", "ln_post_g", "ln_post_b")
    return (loss, grad_x.reshape(x.shape), *[grads[k] for k in order], *[deltas[k] for k in order],
            *[new_m[k] for k in order], *[new_v[k] for k in order])
```

```python
import functools

import jax
import jax.numpy as jnp
from jax import lax
from jax.experimental import pallas as pl
from jax.experimental.pallas import tpu as pltpu

F32 = jnp.float32
BF16 = jnp.bfloat16

D_MODEL = 1024
N_HEADS = 16
HEAD_DIM = 64
HEAD_GROUP = 4
GROUP_W = HEAD_GROUP * HEAD_DIM
N_GROUPS = N_HEADS // HEAD_GROUP
N_PIECES = 9
IN_WIDTH = N_PIECES * D_MODEL
Q_BLOCK = 256
Q_TILE = 2 * Q_BLOCK
CONV_K = 31
CONV_ROWS = 32
HALO = 32
LN_EPS = 1e-5
ALPHA = 2.0 ** 0.25
Q_SCALE = 0.125
N_CHIPS = 4
N_DEV = 8
SHARD_IN = IN_WIDTH // N_CHIPS
SHARD_SQ = D_MODEL // N_CHIPS
HALF = D_MODEL // 2
SMALL_ROWS = 40
N_CHUNKS = 16

ADAM_LR = 0.001
ADAM_B1 = 0.9
ADAM_B2 = 0.999
ADAM_EPS = 1e-08
ADAM_WD = 0.01
ADAM_STEP = 10

MESH = pl.DeviceIdType.MESH
ANY = pl.BlockSpec(memory_space=pl.ANY)
VMEM_LIMIT = 60 * 1024 * 1024

NT = (((1,), (1,)), ((), ()))
TN = (((0,), (0,)), ((), ()))


def _sigmoid(x):
    return 1.0 / (1.0 + jnp.exp(-x))


def _dot(a, b):
    return jnp.dot(a, b, preferred_element_type=F32)


def _sum_along_keys(x, tri_twice):
    return _dot(jnp.concatenate(_split_bf16(x), axis=1), tri_twice)


def _dot_nt(a, b):
    return lax.dot_general(a, b, NT, preferred_element_type=F32)


def _dot_tn(a, b):
    return lax.dot_general(a, b, TN, preferred_element_type=F32)


def _split_bf16(x):
    hi = x.astype(BF16)
    lo = (x - hi.astype(F32)).astype(BF16)
    return hi, lo


def _ln_stats(x):
    mu = jnp.mean(x, axis=-1, keepdims=True)
    xc = x - mu
    var = jnp.mean(xc * xc, axis=-1, keepdims=True)
    rstd = lax.rsqrt(var + LN_EPS)
    return xc * rstd, rstd


def _ln_bwd(dy, xhat, rstd, g):
    dxh = dy * g
    m1 = jnp.mean(dxh, axis=-1, keepdims=True)
    m2 = jnp.mean(dxh * xhat, axis=-1, keepdims=True)
    return rstd * (dxh - m1 - xhat * m2)


def _params(*sem):
    return pltpu.CompilerParams(dimension_semantics=sem, vmem_limit_bytes=VMEM_LIMIT)


def _ln_in(x, g, b):
    t = x.shape[0]
    tm = min(512, t)

    def body(x_ref, g_ref, b_ref, hf_ref, hb_ref):
        xhat, _ = _ln_stats(x_ref[...])
        y = xhat * g_ref[...] + b_ref[...]
        hf_ref[...] = y
        hb_ref[...] = y.astype(BF16)

    row = pl.BlockSpec((tm, D_MODEL), lambda i: (i, 0))
    vec = pl.BlockSpec((1, D_MODEL), lambda i: (0, 0))
    return pl.pallas_call(
        body, grid=(t // tm,), in_specs=[row, vec, vec], out_specs=[row, row],
        out_shape=[jax.ShapeDtypeStruct((t, D_MODEL), F32), jax.ShapeDtypeStruct((t, D_MODEL), BF16)],
        compiler_params=_params("parallel"), name="ln_in")(x, g, b)


def _in_proj(h, w, col0, ncol, out_dtype, scale_first, name):
    t = h.shape[0]
    tm = min(1024, t)

    def body(h_ref, w_ref, o_ref):
        res = _dot(h_ref[...], w_ref[...])
        if scale_first:
            res = res * jnp.where(pl.program_id(0) == 0, Q_SCALE, 1.0)
        o_ref[...] = res.astype(out_dtype)

    return pl.pallas_call(
        body, grid=(ncol, t // tm),
        in_specs=[pl.BlockSpec((tm, D_MODEL), lambda j, i: (i, 0)),
                  pl.BlockSpec((D_MODEL, D_MODEL), lambda j, i: (0, j + col0))],
        out_specs=pl.BlockSpec((tm, D_MODEL), lambda j, i: (i, j)),
        out_shape=jax.ShapeDtypeStruct((t, ncol * D_MODEL), out_dtype),
        compiler_params=_params("parallel", "parallel"), name=name)(h, w)


def _sb_block(l, diag, tri, carry):
    lnb = jnp.minimum(-l, 0.0) - jnp.log(1.0 + jnp.exp(-jnp.abs(l)))
    if diag is not None:
        lnb = jnp.where(diag, lnb, 0.0)
    suffix = _sum_along_keys(lnb, tri)
    a = jnp.exp(l + suffix + carry)
    if diag is not None:
        a = jnp.where(diag, a, 0.0)
    return a, lnb, suffix[:, 0:1]


def _attn_consts():
    head = lax.broadcasted_iota(jnp.int32, (Q_TILE, GROUP_W), 1) // HEAD_DIM
    row = lax.broadcasted_iota(jnp.int32, (Q_TILE, Q_BLOCK), 0)
    col = lax.broadcasted_iota(jnp.int32, (Q_TILE, Q_BLOCK), 1)
    key = row % Q_BLOCK
    return head, col < row, (key >= col).astype(BF16), (key <= col).astype(BF16)


def _split_heads(t, head):
    zero = jnp.zeros_like(t)
    head = head[:t.shape[0]]
    return [jnp.where(head == h, t, zero) for h in range(HEAD_GROUP)]


def _key_block(ref, kb):
    return ref[pl.ds(pl.multiple_of(kb * Q_BLOCK, Q_BLOCK), Q_BLOCK), :]


def _attn_fwd(qkv, nb, seq, w_in_early, shards):
    nq = seq // Q_TILE

    def body(q_ref, k_ref, v_ref, full_in, s0, s1, s2, s3, s4, o_ref, d0, d1, d2, d3, d4, send_sems, recv_sems, local_sems):
        del full_in
        qi = pl.program_id(2)
        first = (pl.program_id(0) == 0) & (pl.program_id(1) == 0) & (qi == 0)
        last = (pl.program_id(0) == nb - 1) & (pl.program_id(1) == N_GROUPS - 1) & (qi == nq - 1)
        start_gather, finish_gather = _late_gather([s0, s1, s2, s3, s4], [d0, d1, d2, d3, d4],
                                                   send_sems, recv_sems, local_sems)
        pl.when(first)(start_gather)
        head, diag, tri, _ = _attn_consts()
        qh = _split_heads(q_ref[...], head)
        o_ref[...] = jnp.zeros_like(o_ref)
        zero_col = jnp.zeros((Q_BLOCK, 1), F32)

        def block(kb, carries, rows, masked):
            kblk = _key_block(k_ref, kb)
            vstack = jnp.concatenate(_split_heads(_key_block(v_ref, kb), head), axis=0)
            mask = diag[:Q_TILE - rows.start] if masked else None
            weights, out = [], []
            for h in range(HEAD_GROUP):
                a, _, blk_sum = _sb_block(_dot_nt(qh[h][rows], kblk), mask, tri, carries[h][rows])
                weights.append(a.astype(BF16))
                out.append(carries[h][rows] + blk_sum)
            o_ref[rows, :] += _dot(jnp.concatenate(weights, axis=1), vstack)
            return out

        second = block(2 * qi + 1, [jnp.zeros((Q_TILE, 1), F32)] * HEAD_GROUP, slice(Q_BLOCK, Q_TILE), True)
        carries = tuple(jnp.concatenate([zero_col, c], axis=0) for c in second)
        everything = slice(0, Q_TILE)
        carries = tuple(block(2 * qi, carries, everything, True))
        lax.fori_loop(1, 2 * qi + 1, lambda jj, cs: tuple(block(2 * qi - jj, cs, everything, False)), carries)
        pl.when(last)(finish_gather)

    n_shards = len(shards)
    full = [jax.ShapeDtypeStruct((D_MODEL, IN_WIDTH), BF16)] + [jax.ShapeDtypeStruct((D_MODEL, D_MODEL), BF16)] * 3 \
        + [jax.ShapeDtypeStruct((CONV_ROWS, D_MODEL), F32)]
    res = pl.pallas_call(
        body, grid=(nb, N_GROUPS, nq),
        in_specs=[pl.BlockSpec((Q_TILE, GROUP_W), lambda b, g, i: (b * nq + i, g)),
                  pl.BlockSpec((seq, GROUP_W), lambda b, g, i: (b, N_GROUPS + g)),
                  pl.BlockSpec((seq, GROUP_W), lambda b, g, i: (b, 2 * N_GROUPS + g))] + [ANY] * (1 + n_shards),
        out_specs=[pl.BlockSpec((Q_TILE, GROUP_W), lambda b, g, i: (b * nq + i, g))] + [ANY] * n_shards,
        out_shape=[jax.ShapeDtypeStruct((nb * seq, D_MODEL), F32)] + full,
        input_output_aliases={3: 1},
        scratch_shapes=[pltpu.SemaphoreType.DMA((3 * n_shards,)), pltpu.SemaphoreType.DMA((3 * n_shards,)),
                        pltpu.SemaphoreType.DMA((n_shards - 1,))],
        compiler_params=_params("arbitrary", "arbitrary", "arbitrary"), name="attn_fwd")(
            qkv, qkv, qkv, w_in_early, *shards)
    return res[0], res[1:]


def _attn_bwd(qkv, do, nb, seq):
    nq = seq // Q_TILE
    n_kb = seq // Q_BLOCK

    def body(q_ref, k_ref, v_ref, do_ref, dq_ref, dk_ref, dv_ref, g_s, beta_s, dq_acc, dk_acc, dv_acc):
        qi = pl.program_id(2)

        @pl.when(qi == 0)
        def _():
            dk_acc[...] = jnp.zeros_like(dk_acc)
            dv_acc[...] = jnp.zeros_like(dv_acc)

        head, diag, tri_suffix, tri_prefix = _attn_consts()
        qh = _split_heads(q_ref[...], head)
        doh = _split_heads(do_ref[...], head)
        dq_acc[...] = jnp.zeros_like(dq_acc)
        zero_col = jnp.zeros((Q_BLOCK, 1), F32)
        everything = slice(0, Q_TILE)
        second = slice(Q_BLOCK, Q_TILE)

        def block_a(kb, carries, rows, masked):
            ks = pl.multiple_of(kb * Q_BLOCK, Q_BLOCK)
            kblk = _key_block(k_ref, kb)
            vblk = _key_block(v_ref, kb)
            mask = diag[:Q_TILE - rows.start] if masked else None
            weights, out = [], []
            for h in range(HEAD_GROUP):
                l = _dot_nt(qh[h][rows], kblk)
                a, lnb, blk_sum = _sb_block(l, mask, tri_suffix, carries[h][rows])
                beta = jnp.exp(l + lnb)
                if masked:
                    beta = jnp.where(mask, beta, 0.0)
                g_s[h, kb, rows, :] = a * _dot_nt(doh[h][rows], vblk)
                beta_s[h, kb, rows, :] = beta.astype(BF16)
                weights.append(a.astype(BF16))
                out.append(carries[h][rows] + blk_sum)
            dv_acc[pl.ds(ks, Q_BLOCK), :] += _dot_tn(jnp.concatenate(weights, axis=0),
                                                     jnp.concatenate([t[rows] for t in doh], axis=0))
            return out

        part = block_a(2 * qi + 1, [jnp.zeros((Q_TILE, 1), F32)] * HEAD_GROUP, second, True)
        carries = tuple(jnp.concatenate([zero_col, c], axis=0) for c in part)
        carries = tuple(block_a(2 * qi, carries, everything, True))
        lax.fori_loop(1, 2 * qi + 1, lambda jj, cs: tuple(block_a(2 * qi - jj, cs, everything, False)), carries)

        def block_b(kb, prefixes, rows, masked):
            ks = pl.multiple_of(kb * Q_BLOCK, Q_BLOCK)
            kstack = jnp.concatenate(_split_heads(_key_block(k_ref, kb), head), axis=0)
            dls, out = [], []
            for h in range(HEAD_GROUP):
                g = g_s[h, kb, rows, :]
                beta = beta_s[h, kb, rows, :].astype(F32)
                incl = _sum_along_keys(g, tri_prefix)
                before = incl - g + prefixes[h][rows]
                dl = g - beta * (g + before)
                if masked:
                    dl = jnp.where(diag[:Q_TILE - rows.start], dl, 0.0)
                dls.append(dl.astype(BF16))
                out.append(prefixes[h][rows] + incl[:, Q_BLOCK - 1:Q_BLOCK])
            dq_acc[rows, :] += _dot(jnp.concatenate(dls, axis=1), kstack)
            dk_acc[pl.ds(ks, Q_BLOCK), :] += _dot_tn(jnp.concatenate(dls, axis=0),
                                                     jnp.concatenate([t[rows] for t in qh], axis=0))
            return out

        zeros = (jnp.zeros((Q_TILE, 1), F32),) * HEAD_GROUP
        prefixes = lax.fori_loop(0, 2 * qi, lambda kb, ps: tuple(block_b(kb, ps, everything, False)), zeros)
        prefixes = block_b(2 * qi, prefixes, everything, True)
        block_b(2 * qi + 1, prefixes, second, True)
        dq_ref[...] = (dq_acc[...] * Q_SCALE).astype(BF16)

        @pl.when(qi == nq - 1)
        def _():
            dk_ref[...] = dk_acc[...].astype(BF16)
            dv_ref[...] = dv_acc[...].astype(BF16)

    t = nb * seq
    qspec = pl.BlockSpec((Q_TILE, GROUP_W), lambda b, g, i: (b * nq + i, g))
    kvout = pl.BlockSpec((seq, GROUP_W), lambda b, g, i: (b, g))
    return pl.pallas_call(
        body, grid=(nb, N_GROUPS, nq),
        in_specs=[qspec,
                  pl.BlockSpec((seq, GROUP_W), lambda b, g, i: (b, N_GROUPS + g)),
                  pl.BlockSpec((seq, GROUP_W), lambda b, g, i: (b, 2 * N_GROUPS + g)),
                  qspec],
        out_specs=[qspec, kvout, kvout],
        out_shape=[jax.ShapeDtypeStruct((t, D_MODEL), BF16)] * 3,
        scratch_shapes=[pltpu.VMEM((HEAD_GROUP, n_kb, Q_TILE, Q_BLOCK), F32),
                        pltpu.VMEM((HEAD_GROUP, n_kb, Q_TILE, Q_BLOCK), BF16),
                        pltpu.VMEM((Q_TILE, GROUP_W), F32),
                        pltpu.VMEM((seq, GROUP_W), F32), pltpu.VMEM((seq, GROUP_W), F32)],
        compiler_params=_params("parallel", "parallel", "arbitrary"), name="attn_bwd")(qkv, qkv, qkv, do)


def _conv_block_rows(seq):
    return min(256, seq)


SUBLANES = 8
CONV_SUB = 32
CONV_GROUPS = CONV_SUB // SUBLANES


def _fill_shifts(sh_ref, n):
    for r in range(1, SUBLANES):
        sh_ref[r, 0:n, :] = sh_ref[0, pl.ds(r, n), :]


def _shifted(sh_ref, start, cs):
    r = start % SUBLANES
    return sh_ref[r, start - r:start - r + CONV_SUB, cs].reshape(CONV_GROUPS, SUBLANES, -1)


def _tap(w_ref, k, cs):
    row = w_ref[k:k + 1, cs]
    return jnp.broadcast_to(row, (SUBLANES, row.shape[1]))[None]


def _tree_sum(x):
    parts = [x[i] for i in range(x.shape[0])]
    while len(parts) > 1:
        parts = [parts[i] + parts[i + 1] for i in range(0, len(parts), 2)]
    return parts[0]


def _conv_fwd(hrest, conv_w, conv_b, ln_g, ln_b, nb, seq):
    ts = _conv_block_rows(seq)
    nblk = seq // ts
    lanes = 128

    def body(cv_ref, cg_ref, cvh_ref, cgh_ref, z_ref, w_ref, cb_ref, g_ref, b_ref, u1_ref, cvin_ref, ush):
        i = pl.program_id(1)
        halo = cvh_ref[...] * _sigmoid(cgh_ref[...])
        ush[0, 0:HALO, :] = jnp.where(i > 0, halo, 0.0)
        ush[0, HALO:HALO + ts, :] = cv_ref[...] * _sigmoid(cg_ref[...])
        _fill_shifts(ush, HALO + ts - SUBLANES)
        for cc in range(D_MODEL // lanes):
            cs = slice(cc * lanes, (cc + 1) * lanes)
            for r0 in range(0, ts, CONV_SUB):
                acc = jnp.zeros((CONV_GROUPS, SUBLANES, lanes), F32) + _tap(cb_ref, 0, cs)
                for k in range(CONV_K):
                    acc = acc + _tap(w_ref, k, cs) * _shifted(ush, HALO - CONV_K + 1 + k + r0, cs)
                u1_ref[r0:r0 + CONV_SUB, cs] = acc.reshape(CONV_SUB, lanes)
        xhat, _ = _ln_stats(u1_ref[...])
        u2 = xhat * g_ref[...] + b_ref[...]
        z = z_ref[...]
        cvin_ref[...] = (u2 * _sigmoid(u2) * z * _sigmoid(z)).astype(BF16)

    def main(colblk):
        return pl.BlockSpec((ts, D_MODEL), lambda b, i: (b * nblk + i, colblk))

    def halo(colblk):
        return pl.BlockSpec((HALO, D_MODEL),
                            lambda b, i: (jnp.maximum((b * seq + i * ts) // HALO - 1, 0), colblk))

    vec = pl.BlockSpec((1, D_MODEL), lambda b, i: (0, 0))
    t = nb * seq
    return pl.pallas_call(
        body, grid=(nb, nblk),
        in_specs=[main(1), main(2), halo(1), halo(2), main(3),
                  pl.BlockSpec((CONV_ROWS, D_MODEL), lambda b, i: (0, 0)), vec, vec, vec],
        out_specs=[main(0), main(0)],
        out_shape=[jax.ShapeDtypeStruct((t, D_MODEL), F32), jax.ShapeDtypeStruct((t, D_MODEL), BF16)],
        scratch_shapes=[pltpu.VMEM((SUBLANES, HALO + ts, D_MODEL), F32)],
        compiler_params=_params("parallel", "parallel"), name="conv_fwd")(
            hrest, hrest, hrest, hrest, hrest, conv_w, conv_b, ln_g, ln_b)


def _tail(o, hrest, cvin, h0, tgt, w_sb, w_cv, w_out, ln_g, ln_b):
    t = o.shape[0]
    tm = min(256, t)

    def body(o_ref, z_ref, gs_ref, gc_ref, cvin_ref, h0_ref, tgt_ref, wsb_ref, wcv_ref, wout_ref, g_ref, b_ref,
             dr_ref, drb_ref, a_ref, mg_ref, dysb_ref, dycv_ref, do_ref, dz_ref, dgs_ref, dgc_ref, dcvin_ref,
             st_ref):
        @pl.when(pl.program_id(0) == 0)
        def _():
            st_ref[...] = jnp.zeros_like(st_ref)

        o = o_ref[...]
        z = z_ref[...]
        sz = _sigmoid(z)
        a = (o * z * sz).astype(BF16)
        a_ref[...] = a
        y_sb = _dot(a, wsb_ref[...])
        y_cv = _dot(cvin_ref[...], wcv_ref[...])
        s_sb = _sigmoid(gs_ref[...])
        s_cv = _sigmoid(gc_ref[...])
        merged = (s_sb * y_sb + s_cv * y_cv).astype(BF16)
        mg_ref[...] = merged
        r = ALPHA * h0_ref[...] + _dot(merged, wout_ref[...])
        xhat, rstd = _ln_stats(r)
        g = g_ref[...]
        err = xhat * g + b_ref[...] - tgt_ref[...]
        dy = err * (1.0 / D_MODEL)
        st_ref[0:1, :] += jnp.sum(dy * xhat, axis=0, keepdims=True)
        st_ref[1:2, :] += jnp.sum(dy, axis=0, keepdims=True)
        st_ref[2:3, :] += (0.5 / D_MODEL) * jnp.sum(err * err, axis=0, keepdims=True)
        dr = _ln_bwd(dy, xhat, rstd, g)
        dr_ref[...] = dr
        drb = dr.astype(BF16)
        drb_ref[...] = drb
        dm = _dot_nt(drb, wout_ref[...])
        dy_sb = (dm * s_sb).astype(BF16)
        dy_cv = (dm * s_cv).astype(BF16)
        dysb_ref[...] = dy_sb
        dycv_ref[...] = dy_cv
        dgs_ref[...] = (dm * y_sb * s_sb * (1.0 - s_sb)).astype(BF16)
        dgc_ref[...] = (dm * y_cv * s_cv * (1.0 - s_cv)).astype(BF16)
        da = _dot_nt(dy_sb, wsb_ref[...])
        dcvin_ref[...] = _dot_nt(dy_cv, wcv_ref[...])
        do_ref[...] = (da * z * sz).astype(BF16)
        dz_ref[...] = (da * o * sz * (1.0 + z * (1.0 - sz))).astype(BF16)

    def tok(colblk=0):
        return pl.BlockSpec((tm, D_MODEL), lambda i: (i, colblk))

    wspec = pl.BlockSpec((D_MODEL, D_MODEL), lambda i: (0, 0), pipeline_mode=pl.Buffered(1))
    vec = pl.BlockSpec((1, D_MODEL), lambda i: (0, 0))
    bf = jax.ShapeDtypeStruct((t, D_MODEL), BF16)
    f32 = jax.ShapeDtypeStruct((t, D_MODEL), F32)
    return pl.pallas_call(
        body, grid=(t // tm,),
        in_specs=[tok(), tok(0), tok(4), tok(5), tok(), tok(), tok(), wspec, wspec, wspec, vec, vec],
        out_specs=[tok()] * 11 + [pl.BlockSpec((8, D_MODEL), lambda i: (0, 0))],
        out_shape=[f32] + [bf] * 9 + [f32, jax.ShapeDtypeStruct((8, D_MODEL), F32)],
        compiler_params=_params("arbitrary"), name="tail")(
            o, hrest, hrest, hrest, cvin, h0, tgt, w_sb, w_cv, w_out, ln_g, ln_b)


def _conv_branch_bwd(dcvin, u1, hrest, ln_g, ln_b):
    t = u1.shape[0]
    tm = min(512, t)

    def body(dc_ref, u1_ref, z_ref, g_ref, b_ref, du1_ref, dz_ref, st_ref):
        @pl.when(pl.program_id(0) == 0)
        def _():
            st_ref[...] = jnp.zeros_like(st_ref)

        xhat, rstd = _ln_stats(u1_ref[...])
        g = g_ref[...]
        u2 = xhat * g + b_ref[...]
        s2 = _sigmoid(u2)
        z = z_ref[...]
        sz = _sigmoid(z)
        dc = dc_ref[...]
        dz_ref[...] = (dc * (u2 * s2) * sz * (1.0 + z * (1.0 - sz))).astype(BF16)
        du2 = dc * (z * sz) * s2 * (1.0 + u2 * (1.0 - s2))
        st_ref[0:1, :] += jnp.sum(du2 * xhat, axis=0, keepdims=True)
        st_ref[1:2, :] += jnp.sum(du2, axis=0, keepdims=True)
        du1 = _ln_bwd(du2, xhat, rstd, g)
        du1_ref[...] = du1
        st_ref[2:3, :] += jnp.sum(du1, axis=0, keepdims=True)

    tok = pl.BlockSpec((tm, D_MODEL), lambda i: (i, 0))
    vec = pl.BlockSpec((1, D_MODEL), lambda i: (0, 0))
    return pl.pallas_call(
        body, grid=(t // tm,),
        in_specs=[tok, tok, pl.BlockSpec((tm, D_MODEL), lambda i: (i, 3)), vec, vec],
        out_specs=[tok, tok, pl.BlockSpec((8, D_MODEL), lambda i: (0, 0))],
        out_shape=[jax.ShapeDtypeStruct((t, D_MODEL), F32), jax.ShapeDtypeStruct((t, D_MODEL), BF16),
                   jax.ShapeDtypeStruct((8, D_MODEL), F32)],
        compiler_params=_params("arbitrary"), name="conv_branch_bwd")(dcvin, u1, hrest, ln_g, ln_b)


def _conv_bwd(du1, hrest, conv_w, nb, seq):
    ts = _conv_block_rows(seq)
    nblk = seq // ts
    lanes = 128
    last_halo = nb * seq // HALO - 1

    def body(du_ref, duh_ref, cv_ref, cg_ref, cvh_ref, cgh_ref, w_ref, dcv_ref, dcg_ref, dw_ref, ush, dsh, dw_part):
        b = pl.program_id(0)
        i = pl.program_id(1)

        @pl.when((b == 0) & (i == 0))
        def _():
            dw_part[...] = jnp.zeros_like(dw_part)

        cv = cv_ref[...]
        sg = _sigmoid(cg_ref[...])
        halo = cvh_ref[...] * _sigmoid(cgh_ref[...])
        ush[0, 0:HALO, :] = jnp.where(i > 0, halo, 0.0)
        ush[0, HALO:HALO + ts, :] = cv * sg
        dsh[0, 0:ts, :] = du_ref[...]
        dsh[0, ts:ts + HALO, :] = jnp.where(i < nblk - 1, duh_ref[...], 0.0)
        _fill_shifts(ush, HALO + ts - SUBLANES)
        _fill_shifts(dsh, HALO + ts - SUBLANES)
        for cc in range(D_MODEL // lanes):
            cs = slice(cc * lanes, (cc + 1) * lanes)
            for r0 in range(0, ts, CONV_SUB):
                rows = slice(r0, r0 + CONV_SUB)
                dmain = _shifted(dsh, r0, cs)
                acc = jnp.zeros((CONV_GROUPS, SUBLANES, lanes), F32)
                for k in range(CONV_K):
                    acc = acc + _tap(w_ref, k, cs) * _shifted(dsh, CONV_K - 1 - k + r0, cs)
                    dw_part[k, :, cs] += _tree_sum(dmain * _shifted(ush, HALO - CONV_K + 1 + k + r0, cs))
                acc = acc.reshape(CONV_SUB, lanes)
                dcv_ref[rows, cs] = (acc * sg[rows, cs]).astype(BF16)
                dcg_ref[rows, cs] = (acc * cv[rows, cs] * sg[rows, cs] * (1.0 - sg[rows, cs])).astype(BF16)

        @pl.when((b == nb - 1) & (i == nblk - 1))
        def _():
            for k in range(CONV_ROWS):
                dw_ref[k:k + 1, :] = jnp.sum(dw_part[k], axis=0, keepdims=True)

    def main(colblk):
        return pl.BlockSpec((ts, D_MODEL), lambda b, i: (b * nblk + i, colblk))

    def halo_before(colblk):
        return pl.BlockSpec((HALO, D_MODEL),
                            lambda b, i: (jnp.maximum((b * seq + i * ts) // HALO - 1, 0), colblk))

    halo_after = pl.BlockSpec((HALO, D_MODEL),
                              lambda b, i: (jnp.minimum((b * seq + (i + 1) * ts) // HALO, last_halo), 0))
    t = nb * seq
    return pl.pallas_call(
        body, grid=(nb, nblk),
        in_specs=[main(0), halo_after, main(1), main(2), halo_before(1), halo_before(2),
                  pl.BlockSpec((CONV_ROWS, D_MODEL), lambda b, i: (0, 0))],
        out_specs=[main(0), main(0), pl.BlockSpec((CONV_ROWS, D_MODEL), lambda b, i: (0, 0))],
        out_shape=[jax.ShapeDtypeStruct((t, D_MODEL), BF16), jax.ShapeDtypeStruct((t, D_MODEL), BF16),
                   jax.ShapeDtypeStruct((CONV_ROWS, D_MODEL), F32)],
        scratch_shapes=[pltpu.VMEM((SUBLANES, HALO + ts, D_MODEL), F32), pltpu.VMEM((SUBLANES, ts + HALO, D_MODEL), F32),
                        pltpu.VMEM((CONV_ROWS, SUBLANES, D_MODEL), F32)],
        compiler_params=_params("arbitrary", "arbitrary"), name="conv_bwd")(
            du1, du1, hrest, hrest, hrest, hrest, conv_w)


def _weight_grad(x, dys, name):
    n = len(dys)
    t = x.shape[0]
    tm = min(1024, t)

    def body(x_ref, *refs):
        o_ref = refs[n]
        j = pl.program_id(0)

        @pl.when(pl.program_id(1) == 0)
        def _():
            o_ref[...] = jnp.zeros_like(o_ref)

        for p in range(n):
            @pl.when(j == p)
            def _(p=p):
                o_ref[...] += _dot_tn(x_ref[...], refs[p][...])

    def dy_spec(p):
        return pl.BlockSpec((tm, D_MODEL), lambda j, i: (jnp.where(j == p, i, 0), 0))

    return pl.pallas_call(
        body, grid=(n, t // tm),
        in_specs=[pl.BlockSpec((tm, D_MODEL), lambda j, i: (i, 0))] + [dy_spec(p) for p in range(n)],
        out_specs=pl.BlockSpec((D_MODEL, D_MODEL), lambda j, i: (0, j)),
        out_shape=jax.ShapeDtypeStruct((D_MODEL, n * D_MODEL), F32),
        compiler_params=_params("parallel", "arbitrary"), name=name)(x, *dys)


def _in_proj_bwd(dhs, w_in, dr, x, ln_g, pair):
    n = len(dhs)
    t = x.shape[0]
    tm = min(512, t)
    n_i = t // tm

    def body(*refs):
        dh_refs = refs[:n]
        w_ref, dr_ref, x_ref, g_ref = refs[n:n + 4]
        pair_refs = refs[n + 4:n + 8]
        gx_ref, st_ref = refs[n + 8:n + 10]
        got_refs = refs[n + 10:n + 14]
        acc, send_sems, recv_sems = refs[n + 14:]
        p = pl.program_id(0)
        i = pl.program_id(1)
        pieces, wholes = _chip_exchange_copies(pair_refs, got_refs, send_sems, recv_sems)

        @pl.when((p == 0) & (i == 0))
        def _():
            st_ref[...] = jnp.zeros_like(st_ref)
            for cp in pieces:
                cp.start()

        @pl.when((p == n - 1) & (i == n_i - 1))
        def _():
            for cp in wholes:
                cp.wait_recv()
            for cp in wholes:
                cp.wait_send()

        @pl.when(p == 0)
        def _():
            acc[i] = ALPHA * dr_ref[...]

        for q in range(n):
            @pl.when(p == q)
            def _(q=q):
                acc[i] += _dot_nt(dh_refs[q][...], w_ref[...])

        @pl.when(p == n - 1)
        def _():
            xhat, rstd = _ln_stats(x_ref[...])
            dh0 = acc[i]
            st_ref[0:1, :] += jnp.sum(dh0 * xhat, axis=0, keepdims=True)
            st_ref[1:2, :] += jnp.sum(dh0, axis=0, keepdims=True)
            gx_ref[...] = _ln_bwd(dh0, xhat, rstd, g_ref[...])

    def tok(piece):
        return pl.BlockSpec((tm, D_MODEL), lambda p, i: (jnp.where(p == piece, i, 0), 0))

    got = [jax.ShapeDtypeStruct((3, HALF, SHARD_IN), BF16)] + [jax.ShapeDtypeStruct((3, SHARD_SQ, HALF), BF16)] * 3
    res = pl.pallas_call(
        body, grid=(n, n_i),
        in_specs=[tok(q) for q in range(n)] + [pl.BlockSpec((D_MODEL, D_MODEL), lambda p, i: (0, p)), tok(0), tok(n - 1),
                                               pl.BlockSpec((1, D_MODEL), lambda p, i: (0, 0))] + [ANY] * 4,
        out_specs=[tok(n - 1), pl.BlockSpec((8, D_MODEL), lambda p, i: (0, 0))] + [ANY] * 4,
        out_shape=[jax.ShapeDtypeStruct((t, D_MODEL), F32), jax.ShapeDtypeStruct((8, D_MODEL), F32)] + got,
        scratch_shapes=[pltpu.VMEM((n_i, tm, D_MODEL), F32), pltpu.SemaphoreType.DMA((12,)),
                        pltpu.SemaphoreType.DMA((12,))],
        compiler_params=_params("arbitrary", "arbitrary"), name="in_proj_bwd")(*dhs, w_in, dr, x, ln_g, *pair)
    return res[0], res[1], res[2:]


def _forward_backward(x, tgt, nb, seq, shards, vecs):
    w_in_early = _gather_first(shards[0])
    h0, h0b = _ln_in(x, vecs["ln_in_g"], vecs["ln_in_b"])
    qkv = _in_proj(h0b, w_in_early, 0, 3, BF16, True, "in_proj_qkv")
    o, (w_in, w_sb, w_cv, w_out, conv_w) = _attn_fwd(qkv, nb, seq, w_in_early, shards)
    hrest = _in_proj(h0b, w_in, 3, 6, F32, False, "in_proj_rest")
    u1, cvin = _conv_fwd(hrest, conv_w, vecs["conv_b"], vecs["conv_ln_g"], vecs["conv_ln_b"], nb, seq)
    (dr, drb, a, merged, dy_sb, dy_cv, do, dz_sb, dg_sb, dg_cv, dcvin, st_tail) = _tail(
        o, hrest, cvin, h0, tgt, w_sb, w_cv, w_out, vecs["ln_post_g"], vecs["ln_post_b"])
    d_w_sb = _weight_grad(a, [dy_sb], "grad_w_sb")
    d_w_cv = _weight_grad(cvin, [dy_cv], "grad_w_cv")
    d_w_out = _weight_grad(merged, [drb], "grad_w_out")
    du1, dz_cv, st_conv = _conv_branch_bwd(dcvin, u1, hrest, vecs["conv_ln_g"], vecs["conv_ln_b"])
    dc_val, dc_gate, d_conv_w = _conv_bwd(du1, hrest, conv_w, nb, seq)
    dq, dk, dv = _attn_bwd(qkv, do, nb, seq)
    dhs = [dq, dk, dv, dz_sb, dc_val, dc_gate, dz_cv, dg_sb, dg_cv]
    d_w_in = _weight_grad(h0b, dhs, "grad_w_in")
    small = jnp.concatenate([st_conv[2:3], st_conv[0:2], st_tail[0:3], d_conv_w], axis=0)
    return w_in, dhs, dr, (d_w_in, d_w_sb, d_w_cv, d_w_out), small


def _place():
    x, y, c = lax.axis_index("x"), lax.axis_index("y"), lax.axis_index("c")
    chips = [(1 - x, y), (x, 1 - y), (1 - x, 1 - y)]
    return x, y, c, chips


EARLY_COLS = tuple(min(max(3 * D_MODEL - chip * SHARD_IN, 0), SHARD_IN) for chip in range(N_CHIPS))
EARLY = tuple((chip, cols) for chip, cols in enumerate(EARLY_COLS) if cols)
LATE = tuple((chip, cols) for chip, cols in enumerate(EARLY_COLS) if cols < SHARD_IN)
GATHER_PARTS = 4


def _w_in_slot(full, chip):
    return full.at[:, pl.ds(pl.multiple_of(chip * SHARD_IN, 128), SHARD_IN)]


def _gather_first(w_in):
    part_rows = HALF // GATHER_PARTS
    n_sem = 3 * GATHER_PARTS

    def body(src, dst, stage, ici_send, ici_recv, fwd_send, fwd_recv, local_sems):
        x, y, c, chips = _place()
        me = 2 * x + y
        sibling = (x, y, 1 - c)
        owner = [2 * px + py for px, py in chips]

        def part(ref, k, half, cols):
            return ref.at[pl.ds(pl.multiple_of(half * HALF + k * part_rows, 16), part_rows), pl.ds(0, cols)]

        def held(j, k, half, cols):
            return part(_w_in_slot(dst, owner[j]), k, half, cols)

        def sent(j, k, cols):
            px, py = chips[j]
            return _remote(part(src, k, c, cols), part(_w_in_slot(dst, me), k, c, cols),
                           ici_send.at[j * GATHER_PARTS + k], ici_recv.at[j * GATHER_PARTS + k], (px, py, c))

        def landed(j, k, cols):
            px, py = chips[j]
            return _remote(held(j, k, c, cols), held(j, k, c, cols), ici_send.at[j * GATHER_PARTS + k],
                           ici_recv.at[j * GATHER_PARTS + k], (px, py, c))

        def passed(j, k, half, cols):
            return _remote(held(j, k, half, cols), held(j, k, half, cols), fwd_send.at[j * GATHER_PARTS + k],
                           fwd_recv.at[j * GATHER_PARTS + k], sibling)

        own = _w_in_slot(dst, me)
        rows = D_MODEL // N_CHUNKS
        chunk = [pl.ds(k * rows, rows) for k in range(N_CHUNKS)]
        loads = [pltpu.make_async_copy(src.at[r], stage.at[r], local_sems.at[k]) for k, r in enumerate(chunk)]
        stores = [pltpu.make_async_copy(stage.at[r], own.at[r], local_sems.at[N_CHUNKS + k]) for k, r in enumerate(chunk)]
        for cp in loads:
            cp.start()

        for chip, cols in EARLY:
            @pl.when(me == chip)
            def _(cols=cols):
                for j in range(3):
                    for k in range(GATHER_PARTS):
                        sent(j, k, cols).start()

        for load, store in zip(loads, stores):
            load.wait()
            store.start()

        for j in range(3):
            for chip, cols in EARLY:
                @pl.when(owner[j] == chip)
                def _(j=j, cols=cols):
                    for k in range(GATHER_PARTS):
                        landed(j, k, cols).wait_recv()
                        passed(j, k, c, cols).start()

        for j in range(3):
            for chip, cols in EARLY:
                @pl.when(owner[j] == chip)
                def _(j=j, cols=cols):
                    for k in range(GATHER_PARTS):
                        passed(j, k, 1 - c, cols).wait_recv()
                    for k in range(GATHER_PARTS):
                        passed(j, k, c, cols).wait_send()

        for chip, cols in EARLY:
            @pl.when(me == chip)
            def _(cols=cols):
                for j in range(3):
                    for k in range(GATHER_PARTS):
                        sent(j, k, cols).wait_send()

        for store in stores:
            store.wait()

    return pl.pallas_call(
        body, in_specs=[ANY], out_specs=ANY, out_shape=jax.ShapeDtypeStruct((D_MODEL, IN_WIDTH), BF16),
        scratch_shapes=[pltpu.VMEM((D_MODEL, SHARD_IN), BF16),
                        pltpu.SemaphoreType.DMA((n_sem,)), pltpu.SemaphoreType.DMA((n_sem,)),
                        pltpu.SemaphoreType.DMA((n_sem,)), pltpu.SemaphoreType.DMA((n_sem,)),
                        pltpu.SemaphoreType.DMA((2 * N_CHUNKS,))],
        compiler_params=pltpu.CompilerParams(vmem_limit_bytes=VMEM_LIMIT), name="gather_first")(w_in)


def _late_gather(srcs, dsts, send_sems, recv_sems, local_sems):
    x, y, c, chips = _place()
    me = 2 * x + y
    small = range(1, 5)

    def slot(a, chip):
        if a == 0:
            return _w_in_slot(dsts[0], chip)
        if a == 4:
            return dsts[a].at[:, pl.ds(pl.multiple_of(chip * SHARD_SQ, 128), SHARD_SQ)]
        return dsts[a].at[pl.ds(pl.multiple_of(chip * SHARD_SQ, 16), SHARD_SQ), :]

    def sems(a, j):
        return send_sems.at[3 * a + j], recv_sems.at[3 * a + j]

    def whole(a, j):
        px, py = chips[j]
        return _remote(srcs[a], slot(a, me), *sems(a, j), (px, py, c))

    def arrival(a, j):
        px, py = chips[j]
        return _arrival(slot(a, 2 * px + py), *sems(a, j), (px, py, c))

    def local(a):
        return pltpu.make_async_copy(srcs[a], slot(a, me), local_sems.at[a - 1])

    def late_src(first):
        return srcs[0].at[:, pl.ds(first, SHARD_IN - first)]

    def late_dst(chip, first):
        return _w_in_slot(dsts[0], chip).at[:, pl.ds(first, SHARD_IN - first)]

    def start():
        for a in small:
            local(a).start()
            for j in range(3):
                whole(a, j).start()

        for chip, first in LATE:
            @pl.when(me == chip)
            def _(chip=chip, first=first):
                for j, (px, py) in enumerate(chips):
                    for cp in _chunk_copies(late_src(first), late_dst(chip, first), GATHER_PARTS, *sems(0, j), (px, py, c)):
                        cp.start()

    def finish():
        for j, (px, py) in enumerate(chips):
            for a in small:
                arrival(a, j).wait_recv()

            for chip, first in LATE:
                @pl.when(2 * px + py == chip)
                def _(j=j, px=px, py=py, chip=chip, first=first):
                    _arrival(late_dst(chip, first), *sems(0, j), (px, py, c)).wait_recv()

        for a in small:
            for j in range(3):
                whole(a, j).wait_send()
            local(a).wait()

        for chip, first in LATE:
            @pl.when(me == chip)
            def _(chip=chip, first=first):
                for j, (px, py) in enumerate(chips):
                    _remote(late_src(first), late_dst(chip, first), *sems(0, j), (px, py, c)).wait_send()

    return start, finish


def _remote(src, dst, send_sem, recv_sem, device):
    return pltpu.make_async_remote_copy(src_ref=src, dst_ref=dst, send_sem=send_sem, recv_sem=recv_sem,
                                        device_id=device, device_id_type=MESH)


def _chunk_copies(src, dst, n, send_sem, recv_sem, device):
    rows = src.shape[0] // n
    return [_remote(src.at[pl.ds(k * rows, rows)], dst.at[pl.ds(k * rows, rows)], send_sem, recv_sem, device)
            for k in range(n)]


def _row_chunks(src, dst, n, send_sem, recv_sem, device):
    return _chunk_copies(src, dst, n, send_sem, recv_sem, device), _remote(src, dst, send_sem, recv_sem, device)


def _arrival(dst, send_sem, recv_sem, device):
    return _remote(dst, dst, send_sem, recv_sem, device)


def _pair_exchange(d_in, d_sb, d_cv, d_out):
    def body(i0, i1, i2, i3, r0, r1, r2, r3, send_sems, recv_sems):
        x, y, c, _ = _place()
        srcs = [i0, i1, i2, i3]
        theirs = [r0, r1, r2, r3]
        sibling = (x, y, 1 - c)
        start = pl.multiple_of((1 - c) * HALF, 128)
        views = [srcs[0].at[pl.ds(start, HALF), :]] + [srcs[a].at[:, pl.ds(start, HALF)] for a in range(1, 4)]
        pieces, wholes = [], []
        for a in range(4):
            cps, whole = _row_chunks(views[a], theirs[a], N_CHUNKS, send_sems.at[a], recv_sems.at[a], sibling)
            pieces += cps
            wholes.append(whole)
        for cp in pieces:
            cp.start()
        for cp in wholes:
            cp.wait_recv()
        for cp in wholes:
            cp.wait_send()

    shapes = [jax.ShapeDtypeStruct((HALF, IN_WIDTH), F32)] + [jax.ShapeDtypeStruct((D_MODEL, HALF), F32)] * 3
    return pl.pallas_call(
        body, in_specs=[ANY] * 4, out_specs=[ANY] * 4, out_shape=shapes,
        scratch_shapes=[pltpu.SemaphoreType.DMA((4,)), pltpu.SemaphoreType.DMA((4,))],
        name="pair_exchange")(d_in, d_sb, d_cv, d_out)


def _chip_exchange_copies(srcs, theirs, send_sems, recv_sems):
    _, _, c, chips = _place()

    def shard(a, px, py):
        chip = 2 * px + py
        if a == 0:
            return srcs[a].at[:, pl.ds(pl.multiple_of(chip * SHARD_IN, 128), SHARD_IN)]
        return srcs[a].at[pl.ds(pl.multiple_of(chip * SHARD_SQ, 16), SHARD_SQ), :]

    pieces, wholes = [], []
    for j, (px, py) in enumerate(chips):
        for a in range(4):
            cps, whole = _row_chunks(shard(a, px, py), theirs[a].at[j], 4, send_sems.at[3 * a + j],
                                     recv_sems.at[3 * a + j], (px, py, c))
            pieces += cps
            wholes.append(whole)
    return pieces, wholes


def _pair_share(f_in, f_sb, f_cv, f_out):
    def body(i0, i1, i2, i3, o0, o1, o2, o3, send_sems, recv_sems):
        del i0, i1, i2, i3
        x, y, c, _ = _place()
        dsts = [o0, o1, o2, o3]
        sibling = (x, y, 1 - c)

        def half(a, which):
            start = pl.multiple_of(which * HALF, 128)
            if a == 0:
                return dsts[a].at[pl.ds(start, HALF), :]
            return dsts[a].at[:, pl.ds(start, HALF)]

        pieces, sent, arrived = [], [], []
        for a in range(4):
            cps, whole = _row_chunks(half(a, c), half(a, c), N_CHUNKS, send_sems.at[a], recv_sems.at[a], sibling)
            pieces += cps
            sent.append(whole)
            arrived.append(_arrival(half(a, 1 - c), send_sems.at[a], recv_sems.at[a], sibling))
        for cp in pieces:
            cp.start()
        for cp in arrived:
            cp.wait_recv()
        for cp in sent:
            cp.wait_send()

    return pl.pallas_call(
        body, in_specs=[ANY] * 4, out_specs=[ANY] * 4,
        out_shape=[jax.ShapeDtypeStruct((D_MODEL, SHARD_IN), F32)] + [jax.ShapeDtypeStruct((SHARD_SQ, D_MODEL), F32)] * 3,
        input_output_aliases={0: 0, 1: 1, 2: 2, 3: 3},
        scratch_shapes=[pltpu.SemaphoreType.DMA((4,)), pltpu.SemaphoreType.DMA((4,))],
        name="pair_share")(f_in, f_sb, f_cv, f_out)


def _small_allreduce(small):
    def body(s_ref, o_ref, slots, send_sems, recv_sems):
        x, y, c, _ = _place()
        me = 4 * x + 2 * y + c
        slots[me] = s_ref[...]
        sends = []
        for k in range(1, N_DEV):
            px, py, pc = (x + (k >> 2)) % 2, (y + ((k >> 1) & 1)) % 2, (c + (k & 1)) % 2
            sends.append(pltpu.make_async_remote_copy(
                src_ref=s_ref, dst_ref=slots.at[me], send_sem=send_sems.at[k - 1], recv_sem=recv_sems.at[k - 1],
                device_id=(px, py, pc), device_id_type=MESH))
        for cp in sends:
            cp.start()
        for k in range(1, N_DEV):
            px, py, pc = (x + (k >> 2)) % 2, (y + ((k >> 1) & 1)) % 2, (c + (k & 1)) % 2
            pltpu.make_async_remote_copy(
                src_ref=s_ref, dst_ref=slots.at[4 * px + 2 * py + pc], send_sem=send_sems.at[k - 1],
                recv_sem=recv_sems.at[k - 1], device_id=(px, py, pc), device_id_type=MESH).wait_recv()
        for cp in sends:
            cp.wait_send()
        total = slots[0]
        for d in range(1, N_DEV):
            total = total + slots[d]
        o_ref[...] = total
        o_ref[7:8, :] = jnp.zeros((1, D_MODEL), F32) + jnp.sum(total[7:8, :], axis=1, keepdims=True)

    return pl.pallas_call(
        body, in_specs=[pl.BlockSpec(memory_space=pltpu.VMEM)], out_specs=pl.BlockSpec(memory_space=pltpu.VMEM),
        out_shape=jax.ShapeDtypeStruct((SMALL_ROWS, D_MODEL), F32),
        scratch_shapes=[pltpu.VMEM((N_DEV, SMALL_ROWS, D_MODEL), F32), pltpu.SemaphoreType.DMA((N_DEV - 1,)),
                        pltpu.SemaphoreType.DMA((N_DEV - 1,))],
        name="small_allreduce")(small)


SUM_BLOCKS = 8


def _pair_sum(place, full, theirs, by_rows, name):
    rows, cols = theirs.shape
    tr = rows // SUM_BLOCKS

    def body(p_ref, a_ref, b_ref, o_ref):
        del p_ref
        o_ref[...] = (a_ref[...] + b_ref[...]).astype(BF16)

    mine = (lambda i, p: (p[1] * SUM_BLOCKS + i, 0)) if by_rows else (lambda i, p: (i, p[1]))
    spec = pl.BlockSpec((tr, cols), lambda i, p: (i, 0))
    return pl.pallas_call(
        body, out_shape=jax.ShapeDtypeStruct(theirs.shape, BF16),
        grid_spec=pltpu.PrefetchScalarGridSpec(num_scalar_prefetch=1, grid=(SUM_BLOCKS,),
                                               in_specs=[pl.BlockSpec((tr, cols), mine), spec], out_specs=spec),
        compiler_params=_params("parallel"), name=name)(place, full, theirs)


def _chip_sum(place, pair, got, by_rows, name):
    _, rows, cols = got.shape
    tr = rows // SUM_BLOCKS

    def body(p_ref, a_ref, g_ref, o_ref):
        del p_ref
        o_ref[...] = ((a_ref[...].astype(F32) + g_ref[0].astype(F32)) + g_ref[1].astype(F32)) + g_ref[2].astype(F32)

    if by_rows:
        own = lambda i, p: (i, p[0])
        out = lambda i, p: (p[1] * SUM_BLOCKS + i, 0)
        full = (2 * rows, cols)
    else:
        own = lambda i, p: (p[0] * SUM_BLOCKS + i, 0)
        out = lambda i, p: (i, p[1])
        full = (rows, 2 * cols)
    return pl.pallas_call(
        body, out_shape=jax.ShapeDtypeStruct(full, F32),
        grid_spec=pltpu.PrefetchScalarGridSpec(
            num_scalar_prefetch=1, grid=(SUM_BLOCKS,),
            in_specs=[pl.BlockSpec((tr, cols), own), pl.BlockSpec((3, tr, cols), lambda i, p: (0, i, 0))],
            out_specs=pl.BlockSpec((tr, cols), out)),
        compiler_params=_params("parallel"), name=name)(place, pair, got)


def _adamw(w, g, m, v, name):
    rows, cols = w.shape
    tr = rows // 4 if rows % 32 == 0 else rows
    c1 = 1.0 - ADAM_B1 ** ADAM_STEP
    c2 = 1.0 - ADAM_B2 ** ADAM_STEP

    def body(w_ref, g_ref, m_ref, v_ref, d_ref, mo_ref, vo_ref):
        g = g_ref[...]
        mn = ADAM_B1 * m_ref[...] + (1.0 - ADAM_B1) * g
        vn = ADAM_B2 * v_ref[...] + (1.0 - ADAM_B2) * (g * g)
        mo_ref[...] = mn
        vo_ref[...] = vn
        d_ref[...] = -ADAM_LR * ((mn / c1) / (jnp.sqrt(vn / c2) + ADAM_EPS) + ADAM_WD * w_ref[...])

    spec = pl.BlockSpec((tr, cols), lambda i: (i, 0))
    shape = jax.ShapeDtypeStruct(w.shape, F32)
    return pl.pallas_call(body, grid=(rows // tr,), in_specs=[spec] * 4, out_specs=[spec] * 3,
                          out_shape=[shape] * 3, compiler_params=_params("parallel"), name=name)(w, g, m, v)


GRAD_NAMES = ("in", "sb", "cv", "out")


def _place_scalars():
    return jnp.stack([2 * lax.axis_index("x") + lax.axis_index("y"), lax.axis_index("c")]).astype(jnp.int32)


def _pair_sums(grads):
    place = _place_scalars()
    theirs = _pair_exchange(*grads)
    return [_pair_sum(place, grads[a], theirs[a], a == 0, "pair_sum_" + GRAD_NAMES[a]) for a in range(4)]


def _finish_weight_grads(pair, got):
    place = _place_scalars()
    done = [_chip_sum(place, pair[a], got[a], a == 0, "chip_sum_" + GRAD_NAMES[a]) for a in range(4)]
    return _pair_share(*done)


def kernel(x, ln_in_g, ln_in_b, w_in, w_sb_proj, conv_w, conv_b, conv_ln_g, conv_ln_b, w_cv_proj, w_out, ln_post_g, ln_post_b, loss_target, m_ln_in_g, m_ln_in_b, m_w_in, m_w_sb_proj, m_conv_w, m_conv_b, m_conv_ln_g, m_conv_ln_b, m_w_cv_proj, m_w_out, m_ln_post_g, m_ln_post_b, v_ln_in_g, v_ln_in_b, v_w_in, v_w_sb_proj, v_conv_w, v_conv_b, v_conv_ln_g, v_conv_ln_b, v_w_cv_proj, v_w_out, v_ln_post_g, v_ln_post_b):
    nb, seq, _ = x.shape
    t = nb * seq
    vec_names = ("ln_in_g", "ln_in_b", "conv_b", "conv_ln_g", "conv_ln_b", "ln_post_g", "ln_post_b")
    vec_w = dict(zip(vec_names, (ln_in_g, ln_in_b, conv_b, conv_ln_g, conv_ln_b, ln_post_g, ln_post_b)))
    vec_m = dict(zip(vec_names, (m_ln_in_g, m_ln_in_b, m_conv_b, m_conv_ln_g, m_conv_ln_b, m_ln_post_g, m_ln_post_b)))
    vec_v = dict(zip(vec_names, (v_ln_in_g, v_ln_in_b, v_conv_b, v_conv_ln_g, v_conv_ln_b, v_ln_post_g, v_ln_post_b)))
    vecs = {k: a.reshape(1, D_MODEL) for k, a in vec_w.items()}

    pad_taps = lambda a: jnp.pad(a.reshape(CONV_K, SHARD_SQ), ((0, CONV_ROWS - CONV_K), (0, 0)))
    shards = (w_in[0].astype(BF16), w_sb_proj[0].astype(BF16), w_cv_proj[0].astype(BF16), w_out[0].astype(BF16),
              pad_taps(conv_w))
    x2 = x.reshape(t, D_MODEL)
    full_in, dhs, dr, big, small = _forward_backward(x2, loss_target.reshape(t, D_MODEL), nb, seq, shards, vecs)
    pair = _pair_sums(big)
    grad_x, st_in, got = _in_proj_bwd(dhs, full_in, dr, x2, vecs["ln_in_g"], pair)
    g_in, g_sb, g_cv, g_out = _finish_weight_grads(pair, got)
    small = _small_allreduce(jnp.concatenate([st_in[0:2], small], axis=0))

    chip = 2 * lax.axis_index("x") + lax.axis_index("y")
    g_conv = lax.dynamic_slice(small, (8, chip * SHARD_SQ), (CONV_ROWS, SHARD_SQ))
    loss = small[7, 0]

    grads, deltas, new_m, new_v = {}, {}, {}, {}
    stack = lambda d: jnp.concatenate([d[k].reshape(1, D_MODEL) for k in vec_names] + [jnp.zeros((1, D_MODEL), F32)])
    vd, vm, vv = _adamw(stack(vec_w), small[0:8], stack(vec_m), stack(vec_v), "adamw_vectors")
    for r, k in enumerate(vec_names):
        shape = vec_w[k].shape
        grads[k] = small[r].reshape(shape)
        deltas[k], new_m[k], new_v[k] = vd[r].reshape(shape), vm[r].reshape(shape), vv[r].reshape(shape)

    big_w = {"w_in": (w_in, g_in, m_w_in, v_w_in), "w_sb_proj": (w_sb_proj, g_sb, m_w_sb_proj, v_w_sb_proj),
             "w_cv_proj": (w_cv_proj, g_cv, m_w_cv_proj, v_w_cv_proj), "w_out": (w_out, g_out, m_w_out, v_w_out)}
    for k, (w, g, m, v) in big_w.items():
        d, mn, vn = _adamw(w[0], g, m[0], v[0], "adamw_" + k)
        grads[k], deltas[k], new_m[k], new_v[k] = g[None], d[None], mn[None], vn[None]
    d, mn, vn = _adamw(pad_taps(conv_w), g_conv, pad_taps(m_conv_w), pad_taps(v_conv_w), "adamw_conv_w")
    grads["conv_w"] = g_conv[None, :CONV_K]
    deltas["conv_w"], new_m["conv_w"], new_v["conv_w"] = d[None, :CONV_K], mn[None, :CONV_K], vn[None, :CONV_K]

    order = ("ln_in_g", "ln_in_b", "w_in", "w_sb_proj", "conv_w", "conv_b", "conv_ln_g", "conv_ln_b",
             "w_cv_proj", "w_out", "ln_post_g", "ln_post_b")
    return (loss, grad_x.reshape(x.shape), *[grads[k] for k in order], *[deltas[k] for k in order],
            *[new_m[k] for k in order], *[new_v[k] for k in order])
```
